```python
import math
import jax
import jax.numpy as jnp
from jax import lax
import numpy as np

D_MODEL = 1024
BATCH = 4
SEQ = 4096
DEPTH = 2
DEC_BATCH = 32
DEC_SEQ = 1
PAST_LEN = 16384
PAGE_SIZE = 128

HEAD_DIM = 64
N_RET_HEADS = 4
N_MLSTM_HEADS = 4
N_GDN_HEADS = 4
N_NSA_HEADS = 4
D_RET = N_RET_HEADS * HEAD_DIM
D_MLSTM = N_MLSTM_HEADS * HEAD_DIM
D_GDN = N_GDN_HEADS * HEAD_DIM
D_NSA = N_NSA_HEADS * HEAD_DIM
D_MIX = D_RET + D_MLSTM + D_GDN + D_NSA
CHUNK = 64
GDN_CONV = 4
NSA_KV_ROWS = 6
NSA_CACHE_ROWS = 4
CMP_STRIDE = 16
CMP_LEN = 2 * CMP_STRIDE
SLC_BLOCK = 64
SLC_TOPK = 16
WINDOW = 512
Q_BLOCK = 128
T5_BUCKETS = 32
T5_MAX_DIST = 1024
ROPE_BASE = 10000.0
D_FF = 2816
FFN_CONV = 3
EPS = 1e-6
IN_WIDTHS = (D_RET, D_RET, D_RET, D_RET,
             D_MLSTM, D_MLSTM, D_MLSTM, N_MLSTM_HEADS, N_MLSTM_HEADS, D_MLSTM,
             3 * D_GDN, N_GDN_HEADS, N_GDN_HEADS, D_GDN,
             D_NSA, NSA_KV_ROWS * HEAD_DIM, 3 * N_NSA_HEADS)
D_IN = sum(IN_WIDTHS)
IN_SPLITS = tuple(int(s) for s in np.cumsum(IN_WIDTHS)[:-1])
STATE_KEYS = ('nsa', 'swa', 'ret', 'ml_C', 'ml_n', 'ml_m', 'gdn', 'gdn_conv', 'ffn_conv')

kernel_name = 'hybrid_retention_mlstm_gdn_nsa_step'


def _rmsnorm(x, g):
    x32 = x.astype(jnp.float32)
    y = x32 * lax.rsqrt(jnp.mean(x32 * x32, axis=-1, keepdims=True) + EPS)
    return (y * g.astype(jnp.float32)).astype(x.dtype)


def _head_rmsnorm(o, g):
    B, L, H, dh = o.shape
    o32 = o.astype(jnp.float32)
    y = o32 * lax.rsqrt(jnp.mean(o32 * o32, axis=-1, keepdims=True) + EPS)
    return y.reshape(B, L, H * dh) * g.astype(jnp.float32)


def _l2norm(x):
    return x * lax.rsqrt(jnp.sum(x * x, axis=-1, keepdims=True) + EPS)


def _causal_dwconv(buf, x, w):
    width = w.shape[0]
    L = x.shape[1]
    xp = jnp.concatenate([buf.astype(x.dtype), x], axis=1)
    out = sum(xp[:, j:j + L] * w[j].astype(x.dtype) for j in range(width))
    return out, xp[:, L:]


def _rope(x, pos):
    half = x.shape[-1] // 2
    inv = ROPE_BASE ** (-jnp.linspace(0.0, 1.0, half, dtype=jnp.float32))
    ang = pos.astype(jnp.float32)[:, None] * inv[None, :]
    cos = jnp.cos(ang)[None, :, None, :]
    sin = jnp.sin(ang)[None, :, None, :]
    x1, x2 = x[..., :half], x[..., half:]
    return jnp.concatenate([x1 * cos - x2 * sin, x1 * sin + x2 * cos], axis=-1)


def _chunk_len(L):
    return CHUNK if L % CHUNK == 0 else L


def _to_chunks(t, C):
    B, L = t.shape[:2]
    return jnp.moveaxis(t.reshape((B, L // C, C) + t.shape[2:]), 1, 0)


def _from_chunks(t):
    n, B, C = t.shape[:3]
    return jnp.moveaxis(t, 0, 1).reshape((B, n * C) + t.shape[3:])


def _retention(q, k, v, S0):
    H, L = q.shape[2], q.shape[1]
    C = _chunk_len(L)
    log_g = jnp.log(1.0 - 2.0 ** (-5.0 - jnp.arange(H, dtype=jnp.float32)))
    idx = jnp.arange(C, dtype=jnp.float32)
    diff = idx[:, None] - idx[None, :]
    causal = diff >= 0
    dmat = jnp.where(causal[None], jnp.exp(log_g[:, None, None] * jnp.where(causal, diff, 0.0)[None]), 0.0)
    xi = jnp.exp(log_g[None, :] * (idx[:, None] + 1.0))
    zeta = jnp.exp(log_g[None, :] * (C - 1.0 - idx[:, None]))
    g_chunk = jnp.exp(log_g * C)

    def step(S, inp):
        qc, kc, vc = inp
        s = jnp.einsum('bihd,bjhd->bhij', qc, kc) * dmat[None]
        o = (jnp.einsum('bhij,bjhe->bihe', s, vc)
             + jnp.einsum('bihd,bhde->bihe', qc * xi[None, :, :, None], S))
        S = S * g_chunk[None, :, None, None] + jnp.einsum('bjhd,bjhe->bhde', kc * zeta[None, :, :, None], vc)
        return S, o

    S, o = lax.scan(step, S0, (_to_chunks(q, C), _to_chunks(k, C), _to_chunks(v, C)))
    return _from_chunks(o), S


def _mlstm(q, k, v, i_pre, f_pre, C0, n0, m0):
    L = q.shape[1]
    Cn = _chunk_len(L)
    log_f = jax.nn.log_sigmoid(f_pre)
    tri = jnp.tril(jnp.ones((Cn, Cn), dtype=bool))[None, :, :, None]

    def step(carry, inp):
        Cm, nv, m = carry
        qc, kc, vc, ic, fc = inp
        b = jnp.cumsum(fc, axis=1)
        dlog = jnp.where(tri, b[:, :, None, :] - b[:, None, :, :] + ic[:, None, :, :], -jnp.inf)
        inter = b + m[:, None, :]
        m_t = jnp.maximum(inter, jnp.max(dlog, axis=2))
        w = jnp.exp(dlog - m_t[:, :, None, :])
        g_in = jnp.exp(inter - m_t)
        qk = jnp.einsum('bthd,bshd->btsh', qc, kc) * w
        num = (jnp.einsum('btsh,bshe->bthe', qk, vc)
               + g_in[..., None] * jnp.einsum('bthd,bhde->bthe', qc, Cm))
        den = jnp.sum(qk, axis=2) + g_in * jnp.einsum('bthd,bhd->bth', qc, nv)
        h = num / jnp.maximum(jnp.abs(den), jnp.exp(-m_t))[..., None]
        m_end = m_t[:, -1]
        w_end = jnp.exp(b[:, -1:, :] - b + ic - m_end[:, None, :])
        g_end = jnp.exp(inter[:, -1] - m_end)
        Cm = g_end[:, :, None, None] * Cm + jnp.einsum('bsh,bshd,bshe->bhde', w_end, kc, vc)
        nv = g_end[:, :, None] * nv + jnp.einsum('bsh,bshd->bhd', w_end, kc)
        return (Cm, nv, m_end), h

    xs = (_to_chunks(q, Cn), _to_chunks(k, Cn), _to_chunks(v, Cn), _to_chunks(i_pre, Cn), _to_chunks(log_f, Cn))
    (Cm, nv, m), h = lax.scan(step, (C0, n0, m0), xs)
    return _from_chunks(h), Cm, nv, m


def _gated_delta(q, k, v, g, beta, S0):
    L = q.shape[1]
    Cn = _chunk_len(L)
    incl = jnp.tril(jnp.ones((Cn, Cn), dtype=bool))
    strict = jnp.tril(jnp.ones((Cn, Cn), dtype=bool), -1)
    eye = jnp.eye(Cn, dtype=jnp.float32)

    def step(S, inp):
        qc, kc, vc, gc, bc = inp
        G = jnp.cumsum(gc, axis=1)
        Gh = jnp.moveaxis(G, 2, 1)
        decay = jnp.exp(jnp.where(incl, Gh[..., :, None] - Gh[..., None, :], -jnp.inf))
        kk = jnp.einsum('bthd,bshd->bhts', kc, kc)
        bh = jnp.moveaxis(bc, 2, 1)
        tmat = jnp.where(strict, bh[..., :, None] * kk * decay, 0.0) + eye
        rhs_v = jnp.moveaxis(vc * bc[..., None], 2, 1)
        rhs_k = jnp.moveaxis(kc * (bc * jnp.exp(G))[..., None], 2, 1)
        u = lax.linalg.triangular_solve(tmat, rhs_v, left_side=True, lower=True, unit_diagonal=True)
        w = lax.linalg.triangular_solve(tmat, rhs_k, left_side=True, lower=True, unit_diagonal=True)
        delta = u - jnp.einsum('bhtd,bhde->bhte', w, S)
        qk = jnp.einsum('bthd,bshd->bhts', qc, kc) * decay
        o = (jnp.einsum('bhts,bhse->bthe', qk, delta)
             + jnp.einsum('bthd,bhde->bthe', qc * jnp.exp(G)[..., None], S))
        g_end = Gh[..., -1]
        w_end = jnp.exp(g_end[..., None] - Gh)
        S = jnp.exp(g_end)[..., None, None] * S + jnp.einsum('bhs,bshd,bhse->bhde', w_end, kc, delta)
        return S, o

    xs = (_to_chunks(q, Cn), _to_chunks(k, Cn), _to_chunks(v, Cn), _to_chunks(g, Cn), _to_chunks(beta, Cn))
    S, o = lax.scan(step, S0, xs)
    return _from_chunks(o), S


def _t5_bias(rel_bias, dist):
    n = jnp.maximum(dist, 0)
    exact = T5_BUCKETS // 2
    large = exact + (jnp.log(jnp.maximum(n, 1).astype(jnp.float32) / exact)
                     / math.log(T5_MAX_DIST / exact) * (T5_BUCKETS - exact)).astype(jnp.int32)
    bucket = jnp.where(n < exact, n, jnp.minimum(large, T5_BUCKETS - 1))
    return rel_bias[bucket].astype(jnp.float32)


def _masked_softmax(s, valid):
    s = jnp.where(valid, s, -jnp.inf)
    m = jnp.max(s, axis=-1, keepdims=True)
    m = jnp.where(jnp.isfinite(m), m, 0.0)
    p = jnp.exp(s - m)
    den = jnp.sum(p, axis=-1, keepdims=True)
    return p / jnp.where(den > 0, den, 1.0)


def _nsa(q, gates, kv_new, kv_past, swa_buf, pos0, cmp_pe, cmp_w, rel_bias):
    B, L, H, dh = q.shape
    scale = dh ** -0.5
    full = jnp.concatenate([kv_past.astype(kv_new.dtype), kv_new[:, :, :NSA_CACHE_ROWS]], axis=1)
    T = full.shape[1]
    n16 = T // CMP_STRIDE
    rows = full[:, :n16 * CMP_STRIDE, :2].reshape(B, n16, CMP_STRIDE, 2, dh)
    part_a = jnp.einsum('bnjcd,cjde->bnce', rows, cmp_w[:, :CMP_STRIDE])
    part_b = jnp.einsum('bnjcd,cjde->bnce', rows, cmp_w[:, CMP_STRIDE:])
    pe_term = jnp.einsum('jcd,cjde->ce', cmp_pe, cmp_w)
    kv_cmp = part_a[:, :-1] + part_b[:, 1:] + pe_term
    n_cmp = kv_cmp.shape[1]
    cmp_start = jnp.arange(n_cmp) * CMP_STRIDE
    cmp_end = cmp_start + (CMP_LEN - 1)
    n_slc = -(-T // SLC_BLOCK)
    slc = jnp.pad(full[:, :, 2:4], ((0, 0), (0, n_slc * SLC_BLOCK - T), (0, 0), (0, 0)))
    slc = slc.reshape(B, n_slc, SLC_BLOCK, 2, dh)
    slc_start = jnp.arange(n_slc) * SLC_BLOCK
    overlap = ((cmp_start[:, None] < slc_start[None, :] + SLC_BLOCK)
               & (cmp_start[:, None] + CMP_LEN > slc_start[None, :])).astype(jnp.float32)
    topk = min(SLC_TOPK, n_slc)
    swa = jnp.concatenate([swa_buf.astype(kv_new.dtype), kv_new[:, :, 4:6]], axis=1)
    qb_len = Q_BLOCK if L % Q_BLOCK == 0 else L
    nb = L // qb_len
    band = WINDOW + qb_len
    jidx = jnp.arange(n_slc)

    def block(inp):
        bi, qb, gb = inp
        start = bi * qb_len
        qpos = pos0 + start + jnp.arange(qb_len)
        s_c = (jnp.einsum('bqhd,bnd->bhqn', qb, kv_cmp[:, :, 0]).astype(jnp.float32) * scale
               + jnp.moveaxis(_t5_bias(rel_bias, qpos[:, None] - cmp_end[None, :]), -1, 0)[None])
        p_c = _masked_softmax(s_c, (cmp_end[None, :] <= qpos[:, None])[None, None])
        o_c = jnp.einsum('bhqn,bnd->bqhd', p_c, kv_cmp[:, :, 1])
        imp = jnp.einsum('bhqn,nj->bqj', p_c, overlap)
        cur = qpos // SLC_BLOCK
        forced = (jidx[None] == 0) | (jidx[None] == cur[:, None]) | (jidx[None] == cur[:, None] - 1)
        score = jnp.where(forced[None], jnp.inf, imp)
        score = jnp.where((jidx[None] > cur[:, None])[None], -jnp.inf, score)
        top_s, top_i = lax.top_k(score, topk)
        sel = jax.vmap(lambda r, i: r[i])(slc, top_i)
        kpos = top_i[..., None] * SLC_BLOCK + jnp.arange(SLC_BLOCK)
        ok_s = (top_s > -jnp.inf)[..., None] & (kpos <= qpos[None, :, None, None])
        s_s = (jnp.einsum('bqhd,bqkjd->bhqkj', qb, sel[..., 0, :]).astype(jnp.float32) * scale
               + jnp.moveaxis(_t5_bias(rel_bias, qpos[None, :, None, None] - kpos), -1, 1))
        p_s = _masked_softmax(s_s.reshape(B, H, qb_len, topk * SLC_BLOCK),
                              ok_s.reshape(B, 1, qb_len, topk * SLC_BLOCK))
        o_s = jnp.einsum('bhqk,bqkd->bqhd', p_s, sel[..., 1, :].reshape(B, qb_len, topk * SLC_BLOCK, dh))
        kw = lax.dynamic_slice_in_dim(swa, start, band, axis=1)
        kwpos = pos0 - WINDOW + start + jnp.arange(band)
        dist = qpos[:, None] - kwpos[None, :]
        ok_w = (dist >= 0) & (dist < WINDOW) & (kwpos[None, :] >= 0)
        s_w = (jnp.einsum('bqhd,bkd->bhqk', qb, kw[:, :, 0]).astype(jnp.float32) * scale
               + jnp.moveaxis(_t5_bias(rel_bias, dist), -1, 0)[None])
        p_w = _masked_softmax(s_w, ok_w[None, None])
        o_w = jnp.einsum('bhqk,bkd->bqhd', p_w, kw[:, :, 1])
        gb = gb.astype(jnp.float32)
        return gb[..., 0:1] * o_c + gb[..., 1:2] * o_s + gb[..., 2:3] * o_w

    o = lax.map(block, (jnp.arange(nb), _to_chunks(q, qb_len), _to_chunks(gates, qb_len)))
    return _from_chunks(o), swa[:, L:]


def _layer(x, pos0, l, st, prm):
    B, L, _ = x.shape
    dt = x.dtype
    f32 = jnp.float32
    pos = pos0 + jnp.arange(L)
    scale = HEAD_DIM ** -0.5
    h = _rmsnorm(x, prm['norm_pre_mix'][l])
    z = h @ prm['w_in'][l]
    (rq, rk, rv, rg, mq, mk, mv, mi, mf, mo, gqkv, ga, gbeta, gg, nq, nkv, ngate) = jnp.split(z, IN_SPLITS, axis=-1)
    heads = lambda t: t.reshape(B, L, -1, HEAD_DIM).astype(f32)
    o_ret, ret_S = _retention(_rope(heads(rq), pos), _rope(heads(rk), pos) * scale, heads(rv), st['ret'].astype(f32))
    y_ret = _head_rmsnorm(o_ret, prm['ret_norm'][l]) * jax.nn.silu(rg.astype(f32))
    o_ml, ml_C, ml_n, ml_m = _mlstm(heads(mq), heads(mk) * scale, heads(mv),
                                    (mi + prm['mlstm_b_i'][l]).astype(f32), (mf + prm['mlstm_b_f'][l]).astype(f32),
                                    st['ml_C'].astype(f32), st['ml_n'].astype(f32), st['ml_m'].astype(f32))
    y_ml = _head_rmsnorm(jax.nn.sigmoid(heads(mo)) * o_ml, prm['mlstm_norm'][l])
    gconv, gdn_conv = _causal_dwconv(st['gdn_conv'], gqkv, prm['gdn_conv_w'][l])
    gq, gk, gv = jnp.split(jax.nn.silu(gconv), 3, axis=-1)
    decay = -jnp.exp(prm['gdn_A_log'][l].astype(f32)) * jax.nn.softplus(ga.astype(f32) + prm['gdn_dt_bias'][l].astype(f32))
    o_g, gdn_S = _gated_delta(_l2norm(heads(gq)) * scale, _l2norm(heads(gk)), heads(gv), decay,
                              jax.nn.sigmoid(gbeta.astype(f32)), st['gdn'].astype(f32))
    y_gdn = _head_rmsnorm(o_g, prm['gdn_norm'][l]) * jax.nn.silu(gg.astype(f32))
    nkv = nkv.reshape(B, L, NSA_KV_ROWS, HEAD_DIM)
    o_n, swa_new = _nsa(nq.reshape(B, L, N_NSA_HEADS, HEAD_DIM),
                        jax.nn.sigmoid(ngate.reshape(B, L, N_NSA_HEADS, 3).astype(f32)),
                        nkv, st['nsa_past'], st['swa'], pos0,
                        prm['nsa_cmp_pe'][l], prm['nsa_cmp_w'][l], prm['rel_bias'])
    y_nsa = o_n.reshape(B, L, D_NSA)
    y = jnp.concatenate([y_ret, y_ml, y_gdn, y_nsa], axis=-1).astype(dt)
    x = x + _rmsnorm(y @ prm['w_out'][l], prm['norm_post_mix'][l])
    h2 = _rmsnorm(x, prm['norm_pre_ffn'][l])
    gate, ffn_conv = _causal_dwconv(st['ffn_conv'], h2 @ prm['w_ffn_gate'][l], prm['ffn_conv_w'][l])
    f = (jax.nn.gelu(gate, approximate=True) * (h2 @ prm['w_ffn_up'][l])) @ prm['w_ffn_down'][l]
    x = x + _rmsnorm(f, prm['norm_post_ffn'][l])
    new = {'nsa': nkv[:, :, :NSA_CACHE_ROWS], 'swa': swa_new.astype(st['swa'].dtype),
           'ret': ret_S.astype(st['ret'].dtype), 'ml_C': ml_C.astype(st['ml_C'].dtype),
           'ml_n': ml_n.astype(st['ml_n'].dtype), 'ml_m': ml_m.astype(st['ml_m'].dtype),
           'gdn': gdn_S.astype(st['gdn'].dtype), 'gdn_conv': gdn_conv, 'ffn_conv': ffn_conv}
    return x, new


def _trunk(x, pos0, states, prm):
    new_states = []
    for l in range(DEPTH):
        x, new = _layer(x, pos0, l, states[l], prm)
        new_states.append(new)
    return x, {k: jnp.stack([s[k] for s in new_states]) for k in STATE_KEYS}


def _zero_states(B, dt):
    return {'nsa_past': jnp.zeros((B, 0, NSA_CACHE_ROWS, HEAD_DIM), dt),
            'swa': jnp.zeros((B, WINDOW, 2, HEAD_DIM), dt),
            'ret': jnp.zeros((B, N_RET_HEADS, HEAD_DIM, HEAD_DIM), dt),
            'ml_C': jnp.zeros((B, N_MLSTM_HEADS, HEAD_DIM, HEAD_DIM), dt),
            'ml_n': jnp.zeros((B, N_MLSTM_HEADS, HEAD_DIM), dt),
            'ml_m': jnp.zeros((B, N_MLSTM_HEADS), dt),
            'gdn': jnp.zeros((B, N_GDN_HEADS, HEAD_DIM, HEAD_DIM), dt),
            'gdn_conv': jnp.zeros((B, GDN_CONV - 1, 3 * D_GDN), dt),
            'ffn_conv': jnp.zeros((B, FFN_CONV - 1, D_FF), dt)}


def setup_inputs(seed: int = 0) -> dict:
    key = jax.random.key(seed)
    ks = iter(jax.random.split(key, 48))
    nrm = lambda shape, s=1.0: s * jax.random.normal(next(ks), shape, jnp.float32)
    gain = lambda shape: 1.0 + nrm(shape, 0.05)
    n_pages = PAST_LEN // PAGE_SIZE
    used = DEC_BATCH * n_pages
    n_pool = used + max(1, used // 4)
    page_table = jax.random.permutation(next(ks), n_pool)[:used].reshape(DEC_BATCH, n_pages).astype(jnp.int32)
    return {
        'x_prompt': nrm((BATCH, SEQ, D_MODEL)),
        'x_sample': nrm((DEC_BATCH, DEC_SEQ, D_MODEL)),
        'cache_nsa_kv': nrm((DEPTH, n_pool, PAGE_SIZE, NSA_CACHE_ROWS, HEAD_DIM)),
        'cache_swa_kv': nrm((DEPTH, DEC_BATCH, WINDOW, 2, HEAD_DIM)),
        'state_ret': nrm((DEPTH, DEC_BATCH, N_RET_HEADS, HEAD_DIM, HEAD_DIM), 0.1),
        'state_mlstm_C': nrm((DEPTH, DEC_BATCH, N_MLSTM_HEADS, HEAD_DIM, HEAD_DIM), 0.1),
        'state_mlstm_n': nrm((DEPTH, DEC_BATCH, N_MLSTM_HEADS, HEAD_DIM), 0.5),
        'state_mlstm_m': nrm((DEPTH, DEC_BATCH, N_MLSTM_HEADS)),
        'state_gdn': nrm((DEPTH, DEC_BATCH, N_GDN_HEADS, HEAD_DIM, HEAD_DIM), 0.1),
        'state_gdn_conv': nrm((DEPTH, DEC_BATCH, GDN_CONV - 1, 3 * D_GDN)),
        'state_ffn_conv': nrm((DEPTH, DEC_BATCH, FFN_CONV - 1, D_FF)),
        'page_table': page_table,
        'rel_bias': nrm((T5_BUCKETS, N_NSA_HEADS), 0.5),
        'norm_pre_mix': gain((DEPTH, D_MODEL)),
        'w_in': nrm((DEPTH, D_MODEL, D_IN), D_MODEL ** -0.5),
        'mlstm_b_i': nrm((DEPTH, N_MLSTM_HEADS), 0.1),
        'mlstm_b_f': jnp.linspace(3.0, 6.0, N_MLSTM_HEADS, dtype=jnp.float32)[None] + nrm((DEPTH, N_MLSTM_HEADS), 0.1),
        'gdn_conv_w': nrm((DEPTH, GDN_CONV, 3 * D_GDN), 0.5),
        'gdn_A_log': jnp.log(jax.random.uniform(next(ks), (DEPTH, N_GDN_HEADS), jnp.float32, 1.0, 16.0)),
        'gdn_dt_bias': jnp.log(jnp.expm1(jax.random.uniform(next(ks), (DEPTH, N_GDN_HEADS), jnp.float32, 0.001, 0.1))),
        'nsa_cmp_pe': nrm((DEPTH, CMP_LEN, 2, HEAD_DIM), 0.02),
        'nsa_cmp_w': nrm((DEPTH, 2, CMP_LEN, HEAD_DIM, HEAD_DIM), (CMP_LEN * HEAD_DIM) ** -0.5),
        'ret_norm': gain((DEPTH, D_RET)),
        'mlstm_norm': gain((DEPTH, D_MLSTM)),
        'gdn_norm': gain((DEPTH, D_GDN)),
        'w_out': nrm((DEPTH, D_MIX, D_MODEL), D_MIX ** -0.5),
        'norm_post_mix': gain((DEPTH, D_MODEL)),
        'norm_pre_ffn': gain((DEPTH, D_MODEL)),
        'w_ffn_gate': nrm((DEPTH, D_MODEL, D_FF), D_MODEL ** -0.5),
        'w_ffn_up': nrm((DEPTH, D_MODEL, D_FF), D_MODEL ** -0.5),
        'ffn_conv_w': nrm((DEPTH, FFN_CONV, D_FF), 0.5),
        'w_ffn_down': nrm((DEPTH, D_FF, D_MODEL), D_FF ** -0.5),
        'norm_post_ffn': gain((DEPTH, D_MODEL)),
    }


def reference(x_prompt, x_sample, cache_nsa_kv, cache_swa_kv, state_ret, state_mlstm_C, state_mlstm_n,
              state_mlstm_m, state_gdn, state_gdn_conv, state_ffn_conv, page_table, rel_bias, norm_pre_mix,
              w_in, mlstm_b_i, mlstm_b_f, gdn_conv_w, gdn_A_log, gdn_dt_bias, nsa_cmp_pe, nsa_cmp_w,
              ret_norm, mlstm_norm, gdn_norm, w_out, norm_post_mix, norm_pre_ffn, w_ffn_gate, w_ffn_up,
              ffn_conv_w, w_ffn_down, norm_post_ffn):
    prm = {'rel_bias': rel_bias, 'norm_pre_mix': norm_pre_mix, 'w_in': w_in, 'mlstm_b_i': mlstm_b_i,
           'mlstm_b_f': mlstm_b_f, 'gdn_conv_w': gdn_conv_w, 'gdn_A_log': gdn_A_log, 'gdn_dt_bias': gdn_dt_bias,
           'nsa_cmp_pe': nsa_cmp_pe, 'nsa_cmp_w': nsa_cmp_w, 'ret_norm': ret_norm, 'mlstm_norm': mlstm_norm,
           'gdn_norm': gdn_norm, 'w_out': w_out, 'norm_post_mix': norm_post_mix, 'norm_pre_ffn': norm_pre_ffn,
           'w_ffn_gate': w_ffn_gate, 'w_ffn_up': w_ffn_up, 'ffn_conv_w': ffn_conv_w, 'w_ffn_down': w_ffn_down,
           'norm_post_ffn': norm_post_ffn}
    p_init = [_zero_states(x_prompt.shape[0], x_prompt.dtype) for _ in range(DEPTH)]
    y_prompt, p_st = _trunk(x_prompt, 0, p_init, prm)
    dec_b = x_sample.shape[0]
    past_len = page_table.shape[1] * PAGE_SIZE
    s_init = [{'nsa_past': cache_nsa_kv[l][page_table].reshape(dec_b, past_len, NSA_CACHE_ROWS, HEAD_DIM),
               'swa': cache_swa_kv[l], 'ret': state_ret[l], 'ml_C': state_mlstm_C[l], 'ml_n': state_mlstm_n[l],
               'ml_m': state_mlstm_m[l], 'gdn': state_gdn[l], 'gdn_conv': state_gdn_conv[l],
               'ffn_conv': state_ffn_conv[l]} for l in range(DEPTH)]
    y_sample, s_st = _trunk(x_sample, past_len, s_init, prm)
    return (y_prompt, y_sample,
            p_st['nsa'], p_st['swa'], p_st['ret'], p_st['ml_C'], p_st['ml_n'], p_st['ml_m'],
            p_st['gdn'], p_st['gdn_conv'], p_st['ffn_conv'],
            s_st['nsa'], s_st['swa'], s_st['ret'], s_st['ml_C'], s_st['ml_n'], s_st['ml_m'],
            s_st['gdn'], s_st['gdn_conv'], s_st['ffn_conv'])
```

```python
import functools
import math

import jax
import jax.numpy as jnp
import numpy as np
from jax import lax
from jax.experimental import pallas as pl
from jax.experimental.pallas import tpu as pltpu

F32 = jnp.float32
BF16 = jnp.bfloat16
HI = lax.Precision.HIGHEST

D_MODEL = 1024
HEAD_DIM = 64
N_HEADS = 4
D_HEADS = N_HEADS * HEAD_DIM
D_FF = 2816
GDN_CONV = 4
FFN_CONV = 3
PAGE_SIZE = 128
CMP_STRIDE = 16
CMP_LEN = 32
SLC_BLOCK = 64
SLC_TOPK = 16
WINDOW = 512
Q_BLOCK = 128
T5_BUCKETS = 32
T5_MAX_DIST = 1024
ROPE_BASE = 10000.0
EPS = 1e-6
SCALE = HEAD_DIM ** -0.5
IN_WIDTHS = (256, 256, 256, 256, 256, 256, 256, 4, 4, 256, 768, 4, 4, 256, 256, 384, 12)
LANES = 128
SUBLANES = 8
VMEM_LIMIT = 56 * 1024 * 1024
NEG_INF = float("-inf")

SM_MI, SM_MF, SM_GA, SM_GB, SM_NG = 0, 4, 8, 12, 16


def _cparams(*sem):
    return pltpu.CompilerParams(dimension_semantics=sem, vmem_limit_bytes=VMEM_LIMIT)


def _rms(x, g):
    return x * lax.rsqrt(jnp.mean(x * x, axis=-1, keepdims=True) + EPS) * g


def _dot(a, b, **kw):
    return jnp.dot(a, b, preferred_element_type=F32, **kw)


def _dot_nt(a, b, **kw):
    return lax.dot_general(a, b, (((1,), (1,)), ((), ())), preferred_element_type=F32, **kw)


def _dot_tn(a, b, **kw):
    return lax.dot_general(a, b, (((0,), (0,)), ((), ())), preferred_element_type=F32, **kw)


def _bf(x):
    return x.astype(BF16)


def _sigmoid(x):
    return 1.0 / (1.0 + jnp.exp(-x))


def _silu(x):
    return x * _sigmoid(x)


def _softplus(x):
    return jnp.maximum(x, 0.0) + jnp.log(1.0 + jnp.exp(-jnp.abs(x)))


def _head_rms(o):
    return o * lax.rsqrt(jnp.mean(o * o, axis=-1, keepdims=True) + EPS)


PROJ_WIDTHS = (1024, 1024, 1024, 256, 384, 128)


def _prep_w_in(w):
    parts, off = [], 0
    for wd in IN_WIDTHS:
        parts.append(w[:, off:off + wd])
        off += wd
    (rq, rk, rv, rg, mq, mk, mv, mi, mf, mo, gqkv, ga, gbeta, gg, nq, nkv, ngate) = parts
    small = jnp.concatenate([mi, mf, ga, gbeta, ngate], axis=1)
    small = jnp.pad(small, ((0, 0), (0, LANES - small.shape[1])))
    cat = jnp.concatenate([rq, rk, rv, rg, mq, mk, mv, mo, gqkv, gg, nq, nkv, small], axis=1)
    return cat.astype(BF16)


def _in_proj_body(x_ref, g_ref, w_ref, *out_refs):
    h = _bf(_rms(x_ref[...], g_ref[...]))
    off = 0
    for ref in out_refs:
        n = ref.shape[-1]
        ref[...] = _dot(h, w_ref[:, off:off + n])
        off += n


def _in_proj(x2d, g, w_cat, tm):
    m = x2d.shape[0]
    ntot = sum(PROJ_WIDTHS)
    return pl.pallas_call(
        _in_proj_body,
        grid=(m // tm,),
        in_specs=[pl.BlockSpec((tm, D_MODEL), lambda i: (i, 0)),
                  pl.BlockSpec((1, D_MODEL), lambda i: (0, 0)),
                  pl.BlockSpec((D_MODEL, ntot), lambda i: (0, 0))],
        out_specs=[pl.BlockSpec((tm, n), lambda i: (i, 0)) for n in PROJ_WIDTHS],
        out_shape=[jax.ShapeDtypeStruct((m, n), F32) for n in PROJ_WIDTHS],
        compiler_params=_cparams("parallel"),
        name="in_proj",
    )(x2d, g.reshape(1, D_MODEL), w_cat)


def _out_proj_body(x_ref, y0, y1, y2, y3, w_ref, gpost_ref, gpre_ref, x1_ref, h2_ref):
    acc = None
    for k, y in enumerate((y0, y1, y2, y3)):
        t = _dot(_bf(y[...]), w_ref[k * D_HEADS:(k + 1) * D_HEADS, :])
        acc = t if acc is None else acc + t
    x1 = x_ref[...] + _rms(acc, gpost_ref[...])
    x1_ref[...] = x1
    h2_ref[...] = _bf(_rms(x1, gpre_ref[...]))


def _out_proj(x2d, ys, w_out_bf, gpost, gpre, tm):
    m = x2d.shape[0]
    row = lambda n: pl.BlockSpec((tm, n), lambda i: (i, 0))
    vec = pl.BlockSpec((1, D_MODEL), lambda i: (0, 0))
    return pl.pallas_call(
        _out_proj_body,
        grid=(m // tm,),
        in_specs=[row(D_MODEL)] + [row(D_HEADS)] * 4
        + [pl.BlockSpec((D_MODEL, D_MODEL), lambda i: (0, 0)), vec, vec],
        out_specs=[row(D_MODEL), row(D_MODEL)],
        out_shape=[jax.ShapeDtypeStruct((m, D_MODEL), F32), jax.ShapeDtypeStruct((m, D_MODEL), BF16)],
        compiler_params=_cparams("parallel"),
        name="out_proj",
    )(x2d, *ys, w_out_bf, gpost.reshape(1, -1), gpre.reshape(1, -1))


FFN_TN = 256


def _gelu_tanh(x):
    return 0.5 * x * (1.0 + jnp.tanh(0.7978845608028654 * (x + 0.044715 * x * x * x)))


def _ffn_seq_body(x1_ref, h2_ref, wg_ref, wu_ref, wd_ref, cw_ref, gpost_ref, x2_ref, tail_ref, carry_ref, *, tm):
    @pl.when(pl.program_id(1) == 0)
    def _():
        carry_ref[...] = jnp.zeros_like(carry_ref)

    h2 = h2_ref[0]
    rows = lax.broadcasted_iota(jnp.int32, (tm, FFN_TN), 0)
    acc = jnp.zeros((tm, D_MODEL), F32)
    for n0 in range(0, D_FF, FFN_TN):
        g = _dot(h2, wg_ref[:, n0:n0 + FFN_TN])
        prev = carry_ref[:, n0:n0 + FFN_TN]
        g1 = jnp.where(rows == 0, prev[7:8, :], pltpu.roll(g, 1, 0))
        g2 = pltpu.roll(g, 2, 0)
        g2 = jnp.where(rows == 0, prev[6:7, :], jnp.where(rows == 1, prev[7:8, :], g2))
        cw = cw_ref[:, n0:n0 + FFN_TN]
        gate = cw[0:1, :] * g2 + cw[1:2, :] * g1 + cw[2:3, :] * g
        u = _dot(h2, wu_ref[:, n0:n0 + FFN_TN])
        a = _gelu_tanh(gate) * u
        acc = acc + _dot(_bf(a), wd_ref[n0:n0 + FFN_TN, :])
        carry_ref[:, n0:n0 + FFN_TN] = g[tm - SUBLANES:tm, :]
    x2_ref[0] = x1_ref[0] + _rms(acc, gpost_ref[...])
    tail_ref[0] = carry_ref[...]


def _ffn_seq(x1, h2, wg, wu, wd, cw, gpost, tm):
    b, l, _ = x1.shape
    const = lambda shape: pl.BlockSpec(shape, lambda i, j: (0,) * len(shape))
    return pl.pallas_call(
        functools.partial(_ffn_seq_body, tm=tm),
        grid=(b, l // tm),
        in_specs=[pl.BlockSpec((1, tm, D_MODEL), lambda i, j: (i, j, 0)),
                  pl.BlockSpec((1, tm, D_MODEL), lambda i, j: (i, j, 0)),
                  const((D_MODEL, D_FF)), const((D_MODEL, D_FF)), const((D_FF, D_MODEL)),
                  const((FFN_CONV, D_FF)), const((1, D_MODEL))],
        out_specs=[pl.BlockSpec((1, tm, D_MODEL), lambda i, j: (i, j, 0)),
                   pl.BlockSpec((1, SUBLANES, D_FF), lambda i, j: (i, 0, 0))],
        out_shape=[jax.ShapeDtypeStruct((b, l, D_MODEL), F32),
                   jax.ShapeDtypeStruct((b, SUBLANES, D_FF), F32)],
        scratch_shapes=[pltpu.VMEM((SUBLANES, D_FF), F32)],
        compiler_params=_cparams("parallel", "arbitrary"),
        name="ffn_seq",
    )(x1, h2, wg, wu, wd, cw, gpost.reshape(1, -1))


def _ffn_step_body(x1_ref, h2_ref, b0_ref, b1_ref, wg_ref, wu_ref, wd_ref, cw_ref, gpost_ref, x2_ref, g_ref):
    h2 = h2_ref[...]
    acc = jnp.zeros(x1_ref.shape, F32)
    for n0 in range(0, D_FF, FFN_TN):
        sl = slice(n0, n0 + FFN_TN)
        g = _dot(h2, wg_ref[:, sl])
        cw = cw_ref[:, sl]
        gate = cw[0:1, :] * b0_ref[:, sl] + cw[1:2, :] * b1_ref[:, sl] + cw[2:3, :] * g
        u = _dot(h2, wu_ref[:, sl])
        acc = acc + _dot(_bf(_gelu_tanh(gate) * u), wd_ref[sl, :])
        g_ref[:, sl] = g
    x2_ref[...] = x1_ref[...] + _rms(acc, gpost_ref[...])


def _ffn_step(x1, h2, b0, b1, wg, wu, wd, cw, gpost):
    m = x1.shape[0]
    return pl.pallas_call(
        _ffn_step_body,
        out_shape=[jax.ShapeDtypeStruct((m, D_MODEL), F32), jax.ShapeDtypeStruct((m, D_FF), F32)],
        compiler_params=pltpu.CompilerParams(vmem_limit_bytes=VMEM_LIMIT),
        name="ffn_step",
    )(x1, h2, b0, b1, wg, wu, wd, cw, gpost.reshape(1, -1))


def _rope_tables(pos):
    half = HEAD_DIM // 2
    inv = ROPE_BASE ** (-jnp.linspace(0.0, 1.0, half, dtype=F32))
    ang = pos.astype(F32)[:, None] * inv[None, :]
    return jnp.tile(jnp.cos(ang), (1, LANES // half)), jnp.tile(jnp.sin(ang), (1, LANES // half))


def _rope128(x, cos, sin):
    lane = lax.broadcasted_iota(jnp.int32, x.shape, 1)
    first = (lane % HEAD_DIM) < (HEAD_DIM // 2)
    other = jnp.where(first, -pltpu.roll(x, LANES - HEAD_DIM // 2, 1), pltpu.roll(x, HEAD_DIM // 2, 1))
    return x * cos + other * sin


def _rope256(x, cos, sin):
    return jnp.concatenate([_rope128(x[:, :LANES], cos, sin), _rope128(x[:, LANES:], cos, sin)], axis=1)


def _ret_log_decay(h):
    return math.log(1.0 - 2.0 ** (-5.0 - h))


def _head(x, h):
    return x[:, h * HEAD_DIM:(h + 1) * HEAD_DIM]


def _ret_body(z_ref, cos_ref, sin_ref, nw_ref, y_ref, s_ref, *, chunk):
    @pl.when(pl.program_id(1) == 0)
    def _():
        s_ref[...] = jnp.zeros_like(s_ref)

    z = z_ref[0]
    cos, sin = cos_ref[...], sin_ref[...]
    q = _rope256(z[:, 0:256], cos, sin)
    k = _rope256(z[:, 256:512], cos, sin) * SCALE
    v = z[:, 512:768]
    t = lax.broadcasted_iota(jnp.int32, (chunk, chunk), 0)
    s = lax.broadcasted_iota(jnp.int32, (chunk, chunk), 1)
    causal = t >= s
    diff = jnp.where(causal, t - s, 0).astype(F32)
    tcol = lax.broadcasted_iota(jnp.int32, (chunk, 1), 0).astype(F32)
    outs = []
    for h in range(N_HEADS):
        lg = _ret_log_decay(h)
        dmat = jnp.where(causal, jnp.exp(lg * diff), 0.0)
        xi = jnp.exp(lg * (tcol + 1.0))
        zeta = jnp.exp(lg * (chunk - 1.0 - tcol))
        qh, kh, vh = _head(q, h), _head(k, h), _bf(_head(v, h))
        st = s_ref[0, h]
        sc = _dot_nt(_bf(qh), _bf(kh)) * dmat
        o = _dot(_bf(sc), vh) + _dot(_bf(qh * xi), _bf(st))
        s_ref[0, h] = st * math.exp(lg * chunk) + _dot_tn(_bf(kh * zeta), vh)
        outs.append(_head_rms(o))
    y_ref[0] = jnp.concatenate(outs, axis=1) * nw_ref[...] * _silu(z[:, 768:1024])


def _retention_seq(z_ret, cos, sin, nw, chunk):
    b, l, _ = z_ret.shape
    return pl.pallas_call(
        functools.partial(_ret_body, chunk=chunk),
        grid=(b, l // chunk),
        in_specs=[pl.BlockSpec((1, chunk, 1024), lambda i, j: (i, j, 0)),
                  pl.BlockSpec((chunk, LANES), lambda i, j: (j, 0)),
                  pl.BlockSpec((chunk, LANES), lambda i, j: (j, 0)),
                  pl.BlockSpec((1, D_HEADS), lambda i, j: (0, 0))],
        out_specs=[pl.BlockSpec((1, chunk, D_HEADS), lambda i, j: (i, j, 0)),
                   pl.BlockSpec((1, N_HEADS, HEAD_DIM, HEAD_DIM), lambda i, j: (i, 0, 0, 0))],
        out_shape=[jax.ShapeDtypeStruct((b, l, D_HEADS), F32),
                   jax.ShapeDtypeStruct((b, N_HEADS, HEAD_DIM, HEAD_DIM), F32)],
        compiler_params=_cparams("parallel", "arbitrary"),
        name="retention_seq",
    )(z_ret, cos, sin, nw.reshape(1, -1))


def _tri_incl(n):
    t = lax.broadcasted_iota(jnp.int32, (n, n), 0)
    s = lax.broadcasted_iota(jnp.int32, (n, n), 1)
    return t >= s


def _ml_body(z_ref, sm_ref, bias_ref, nw_ref, y_ref, c_ref, n_ref, m_ref, *, chunk):
    @pl.when(pl.program_id(1) == 0)
    def _():
        c_ref[...] = jnp.zeros_like(c_ref)
        n_ref[...] = jnp.zeros_like(n_ref)
        m_ref[...] = jnp.zeros_like(m_ref)

    z = z_ref[0]
    q, k, v, og = z[:, 0:256], z[:, 256:512] * SCALE, z[:, 512:768], z[:, 768:1024]
    pre = sm_ref[0] + bias_ref[...]
    logf = jnp.minimum(pre, 0.0) - jnp.log(1.0 + jnp.exp(-jnp.abs(pre)))
    incl = _tri_incl(chunk)
    bcum = _dot(incl.astype(F32), logf, precision=HI)
    pre_t, bcum_t = pre.T, bcum.T
    m_all = m_ref[0]
    outs, m_new = [], []
    for h in range(N_HEADS):
        i_col = pre[:, SM_MI + h:SM_MI + h + 1]
        b_col = bcum[:, SM_MF + h:SM_MF + h + 1]
        i_row = pre_t[SM_MI + h:SM_MI + h + 1, :]
        b_row = bcum_t[SM_MF + h:SM_MF + h + 1, :]
        m_prev = m_all[:, h:h + 1]
        inter = b_col + m_prev
        dlog = jnp.where(incl, b_col - b_row + i_row, NEG_INF)
        m_t = jnp.maximum(inter, jnp.max(dlog, axis=1, keepdims=True))
        w = jnp.exp(dlog - m_t)
        g_in = jnp.exp(inter - m_t)
        qh, kh, vh = _head(q, h), _head(k, h), _bf(_head(v, h))
        cm, nv = c_ref[0, h], n_ref[0, h:h + 1, :]
        qk = _dot_nt(_bf(qh), _bf(kh)) * w
        num = _dot(_bf(qk), vh) + g_in * _dot(_bf(qh), _bf(cm))
        den = jnp.sum(qk, axis=1, keepdims=True) + g_in * jnp.sum(qh * nv, axis=1, keepdims=True)
        hh = num / jnp.maximum(jnp.abs(den), jnp.exp(-m_t))
        m_end = m_t[chunk - 1:chunk, :]
        w_end = jnp.exp(b_col[chunk - 1:chunk, :] - b_col + i_col - m_end)
        g_end = jnp.exp(inter[chunk - 1:chunk, :] - m_end)
        c_ref[0, h] = g_end * cm + _dot_tn(_bf(kh * w_end), vh)
        n_ref[0, h:h + 1, :] = g_end * nv + jnp.sum(w_end * kh, axis=0, keepdims=True)
        m_new.append(m_end)
        outs.append(_head_rms(_sigmoid(_head(og, h)) * hh))
    lane = lax.broadcasted_iota(jnp.int32, (1, LANES), 1)
    m_out = jnp.zeros((1, LANES), F32)
    for h in range(N_HEADS):
        m_out = jnp.where(lane == h, m_new[h], m_out)
    m_ref[0] = m_out
    y_ref[0] = jnp.concatenate(outs, axis=1) * nw_ref[...]


def _mlstm_seq(z_ml, small, bias_vec, nw, chunk):
    b, l, _ = z_ml.shape
    return pl.pallas_call(
        functools.partial(_ml_body, chunk=chunk),
        grid=(b, l // chunk),
        in_specs=[pl.BlockSpec((1, chunk, 1024), lambda i, j: (i, j, 0)),
                  pl.BlockSpec((1, chunk, LANES), lambda i, j: (i, j, 0)),
                  pl.BlockSpec((1, LANES), lambda i, j: (0, 0)),
                  pl.BlockSpec((1, D_HEADS), lambda i, j: (0, 0))],
        out_specs=[pl.BlockSpec((1, chunk, D_HEADS), lambda i, j: (i, j, 0)),
                   pl.BlockSpec((1, N_HEADS, HEAD_DIM, HEAD_DIM), lambda i, j: (i, 0, 0, 0)),
                   pl.BlockSpec((1, N_HEADS, HEAD_DIM), lambda i, j: (i, 0, 0)),
                   pl.BlockSpec((1, 1, LANES), lambda i, j: (i, 0, 0))],
        out_shape=[jax.ShapeDtypeStruct((b, l, D_HEADS), F32),
                   jax.ShapeDtypeStruct((b, N_HEADS, HEAD_DIM, HEAD_DIM), F32),
                   jax.ShapeDtypeStruct((b, N_HEADS, HEAD_DIM), F32),
                   jax.ShapeDtypeStruct((b, 1, LANES), F32)],
        compiler_params=_cparams("parallel", "arbitrary"),
        name="mlstm_seq",
    )(z_ml, small, bias_vec, nw.reshape(1, -1))


def _unit_lower_inverse(a, n):
    t = lax.broadcasted_iota(jnp.int32, (n, n), 0)
    s = lax.broadcasted_iota(jnp.int32, (n, n), 1)
    eye = (t == s).astype(F32)
    p = eye - a
    pw = a
    k = 2
    while k < n:
        pw = _dot(pw, pw, precision=HI)
        p = p + _dot(p, pw, precision=HI)
        k *= 2
    return p


def _l2norm(x):
    return x * lax.rsqrt(jnp.sum(x * x, axis=-1, keepdims=True) + EPS)


def _gdn_body(z_ref, sm_ref, cw_ref, dtb_ref, alog_ref, nw_ref, y_ref, s_ref, buf_ref, *, chunk):
    @pl.when(pl.program_id(1) == 0)
    def _():
        s_ref[...] = jnp.zeros_like(s_ref)
        buf_ref[0:SUBLANES, :] = jnp.zeros((SUBLANES, 3 * D_HEADS), F32)

    z = z_ref[0]
    buf_ref[SUBLANES:SUBLANES + chunk, :] = z[:, 0:768]
    conv = None
    for j in range(GDN_CONV):
        term = buf_ref[pl.ds(SUBLANES - (GDN_CONV - 1) + j, chunk), :] * cw_ref[j:j + 1, :]
        conv = term if conv is None else conv + term
    buf_ref[0:SUBLANES, :] = buf_ref[chunk:chunk + SUBLANES, :]
    act = _silu(conv)
    q, k, v = act[:, 0:256], act[:, 256:512], act[:, 512:768]

    sm = sm_ref[0]
    g_all = -jnp.exp(alog_ref[...]) * _softplus(sm + dtb_ref[...])
    beta_all = _sigmoid(sm)
    incl = _tri_incl(chunk)
    strict = lax.broadcasted_iota(jnp.int32, (chunk, chunk), 0) > lax.broadcasted_iota(jnp.int32, (chunk, chunk), 1)
    gcum = _dot(incl.astype(F32), g_all, precision=HI)
    gcum_t = gcum.T
    outs = []
    for h in range(N_HEADS):
        g_col = gcum[:, SM_GA + h:SM_GA + h + 1]
        g_row = gcum_t[SM_GA + h:SM_GA + h + 1, :]
        beta = beta_all[:, SM_GB + h:SM_GB + h + 1]
        decay = jnp.where(incl, jnp.exp(jnp.where(incl, g_col - g_row, 0.0)), 0.0)
        qh = _l2norm(_head(q, h)) * SCALE
        kh = _l2norm(_head(k, h))
        vh = _head(v, h)
        kb = _bf(kh)
        a = jnp.where(strict, beta * _dot_nt(kb, kb) * decay, 0.0)
        tinv = _unit_lower_inverse(a, chunk)
        eg = jnp.exp(g_col)
        rhs = jnp.concatenate([vh * beta, kh * (beta * eg)], axis=1)
        uw = _dot(tinv, rhs, precision=HI)
        st = s_ref[0, h]
        stb = _bf(st)
        delta = uw[:, 0:HEAD_DIM] - _dot(_bf(uw[:, HEAD_DIM:]), stb)
        qk = _dot_nt(_bf(qh), kb) * decay
        o = _dot(_bf(qk), _bf(delta)) + _dot(_bf(qh * eg), stb)
        g_end = g_col[chunk - 1:chunk, :]
        w_end = jnp.exp(g_end - g_col)
        s_ref[0, h] = jnp.exp(g_end) * st + _dot_tn(_bf(kh * w_end), _bf(delta))
        outs.append(_head_rms(o))
    y_ref[0] = jnp.concatenate(outs, axis=1) * nw_ref[...] * _silu(z[:, 768:1024])


def _gdn_seq(z_gdn, small, conv_w, dtb_vec, alog_vec, nw, chunk):
    b, l, _ = z_gdn.shape
    return pl.pallas_call(
        functools.partial(_gdn_body, chunk=chunk),
        grid=(b, l // chunk),
        in_specs=[pl.BlockSpec((1, chunk, 1024), lambda i, j: (i, j, 0)),
                  pl.BlockSpec((1, chunk, LANES), lambda i, j: (i, j, 0)),
                  pl.BlockSpec((GDN_CONV, 3 * D_HEADS), lambda i, j: (0, 0)),
                  pl.BlockSpec((1, LANES), lambda i, j: (0, 0)),
                  pl.BlockSpec((1, LANES), lambda i, j: (0, 0)),
                  pl.BlockSpec((1, D_HEADS), lambda i, j: (0, 0))],
        out_specs=[pl.BlockSpec((1, chunk, D_HEADS), lambda i, j: (i, j, 0)),
                   pl.BlockSpec((1, N_HEADS, HEAD_DIM, HEAD_DIM), lambda i, j: (i, 0, 0, 0))],
        out_shape=[jax.ShapeDtypeStruct((b, l, D_HEADS), F32),
                   jax.ShapeDtypeStruct((b, N_HEADS, HEAD_DIM, HEAD_DIM), F32)],
        scratch_shapes=[pltpu.VMEM((SUBLANES + chunk, 3 * D_HEADS), F32)],
        compiler_params=_cparams("parallel", "arbitrary"),
        name="gdn_seq",
    )(z_gdn, small, conv_w, dtb_vec, alog_vec, nw.reshape(1, -1))


def _row_to_col(row):
    n = row.shape[1]
    eye = lax.broadcasted_iota(jnp.int32, (n, n), 0) == lax.broadcasted_iota(jnp.int32, (n, n), 1)
    return jnp.sum(jnp.where(eye, jnp.broadcast_to(row, (n, n)), 0.0), axis=1, keepdims=True)


def _vec_mat(col, mat):
    return jnp.sum(col * mat, axis=0, keepdims=True)


def _step_body(zr_ref, zm_ref, zg_ref, sm_ref, cos_ref, sin_ref, mlb_ref, dtb_ref, alog_ref, cw_ref,
               nwr_ref, nwm_ref, nwg_ref, sr_ref, mc_ref, mn_ref, mm_ref, gs_ref, gc_ref,
               yr_ref, ym_ref, yg_ref, sr_o, mc_o, mn_o, mm_o, gs_o, gc_o):
    sm = sm_ref[0]
    lane = lax.broadcasted_iota(jnp.int32, (1, LANES), 1)
    zr = zr_ref[0]
    cos, sin = cos_ref[...], sin_ref[...]
    q = _rope256(zr[:, 0:256], cos, sin)
    k = _rope256(zr[:, 256:512], cos, sin) * SCALE
    v = zr[:, 512:768]
    outs = []
    for h in range(N_HEADS):
        gam = math.exp(_ret_log_decay(h))
        qh, kh, vh = _head(q, h), _head(k, h), _head(v, h)
        st = sr_ref[0, h]
        o = jnp.sum(qh * kh, axis=1, keepdims=True) * vh + _vec_mat(_row_to_col(qh) * gam, st)
        sr_o[0, h] = st * gam + _row_to_col(kh) * vh
        outs.append(_head_rms(o))
    yr_ref[0] = jnp.concatenate(outs, axis=1) * nwr_ref[...] * _silu(zr[:, 768:1024])
    zm = zm_ref[0]
    q, k, v, og = zm[:, 0:256], zm[:, 256:512] * SCALE, zm[:, 512:768], zm[:, 768:1024]
    pre = sm + mlb_ref[...]
    logf = jnp.minimum(pre, 0.0) - jnp.log(1.0 + jnp.exp(-jnp.abs(pre)))
    m_all = mm_ref[0]
    m_out = jnp.zeros((1, LANES), F32)
    outs = []
    for h in range(N_HEADS):
        i_g = pre[:, SM_MI + h:SM_MI + h + 1]
        b_g = logf[:, SM_MF + h:SM_MF + h + 1]
        inter = b_g + m_all[:, h:h + 1]
        m_t = jnp.maximum(inter, i_g)
        w = jnp.exp(i_g - m_t)
        g_in = jnp.exp(inter - m_t)
        qh, kh, vh = _head(q, h), _head(k, h), _head(v, h)
        cm, nv = mc_ref[0, h], mn_ref[0, h:h + 1, :]
        qk = jnp.sum(qh * kh, axis=1, keepdims=True) * w
        num = qk * vh + g_in * _vec_mat(_row_to_col(qh), cm)
        den = qk + g_in * jnp.sum(qh * nv, axis=1, keepdims=True)
        hh = num / jnp.maximum(jnp.abs(den), jnp.exp(-m_t))
        mc_o[0, h] = g_in * cm + _row_to_col(kh * w) * vh
        mn_o[0, h:h + 1, :] = g_in * nv + w * kh
        m_out = jnp.where(lane == h, m_t, m_out)
        outs.append(_head_rms(_sigmoid(_head(og, h)) * hh))
    mm_o[0] = m_out
    ym_ref[0] = jnp.concatenate(outs, axis=1) * nwm_ref[...]
    zg = zg_ref[0]
    x = zg[:, 0:768]
    buf = gc_ref[0]
    conv = x * cw_ref[GDN_CONV - 1:GDN_CONV, :]
    for j in range(GDN_CONV - 1):
        conv = conv + buf[j:j + 1, :] * cw_ref[j:j + 1, :]
    gc_o[0, 0:GDN_CONV - 2, :] = buf[1:GDN_CONV - 1, :]
    gc_o[0, GDN_CONV - 2:GDN_CONV - 1, :] = x
    act = _silu(conv)
    q, k, v = act[:, 0:256], act[:, 256:512], act[:, 512:768]
    g_all = -jnp.exp(alog_ref[...]) * _softplus(sm + dtb_ref[...])
    beta_all = _sigmoid(sm)
    outs = []
    for h in range(N_HEADS):
        eg = jnp.exp(g_all[:, SM_GA + h:SM_GA + h + 1])
        beta = beta_all[:, SM_GB + h:SM_GB + h + 1]
        qh = _l2norm(_head(q, h)) * SCALE
        kh = _l2norm(_head(k, h))
        vh = _head(v, h)
        st = gs_ref[0, h]
        delta = vh * beta - _vec_mat(_row_to_col(kh * (beta * eg)), st)
        o = jnp.sum(qh * kh, axis=1, keepdims=True) * delta + _vec_mat(_row_to_col(qh * eg), st)
        gs_o[0, h] = eg * st + _row_to_col(kh) * delta
        outs.append(_head_rms(o))
    yg_ref[0] = jnp.concatenate(outs, axis=1) * nwg_ref[...] * _silu(zg[:, 768:1024])


def _mixers_step(z_ret, z_ml, z_gdn, small, cos, sin, mlb, dtb, alog, conv_w, nwr, nwm, nwg,
                 s_ret, ml_c, ml_n, ml_m, gdn_s, gdn_conv):
    b = z_ret.shape[0]
    row = lambda n: pl.BlockSpec((1, 1, n), lambda i: (i, 0, 0))
    vec = lambda n: pl.BlockSpec((1, n), lambda i: (0, 0))
    mat = pl.BlockSpec((1, N_HEADS, HEAD_DIM, HEAD_DIM), lambda i: (i, 0, 0, 0))
    nsp = pl.BlockSpec((1, N_HEADS, HEAD_DIM), lambda i: (i, 0, 0))
    csp = pl.BlockSpec((1, GDN_CONV - 1, 3 * D_HEADS), lambda i: (i, 0, 0))
    mshape = jax.ShapeDtypeStruct((b, N_HEADS, HEAD_DIM, HEAD_DIM), F32)
    yshape = jax.ShapeDtypeStruct((b, 1, D_HEADS), F32)
    return pl.pallas_call(
        _step_body,
        grid=(b,),
        in_specs=[row(1024), row(1024), row(1024), row(LANES), vec(LANES), vec(LANES), vec(LANES), vec(LANES), vec(LANES),
                  pl.BlockSpec((GDN_CONV, 3 * D_HEADS), lambda i: (0, 0)), vec(D_HEADS), vec(D_HEADS), vec(D_HEADS),
                  mat, mat, nsp, row(LANES), mat, csp],
        out_specs=[row(D_HEADS), row(D_HEADS), row(D_HEADS), mat, mat, nsp, row(LANES), mat, csp],
        out_shape=[yshape, yshape, yshape, mshape, mshape,
                   jax.ShapeDtypeStruct((b, N_HEADS, HEAD_DIM), F32), jax.ShapeDtypeStruct((b, 1, LANES), F32),
                   mshape, jax.ShapeDtypeStruct((b, GDN_CONV - 1, 3 * D_HEADS), F32)],
        compiler_params=_cparams("parallel"),
        name="mixers_step",
    )(z_ret, z_ml, z_gdn, small, cos, sin, mlb, dtb, alog, conv_w,
      nwr.reshape(1, -1), nwm.reshape(1, -1), nwg.reshape(1, -1), s_ret, ml_c, ml_n, ml_m, gdn_s, gdn_conv)


def _t5_bucket_np(n):
    exact = T5_BUCKETS // 2
    n = np.maximum(np.asarray(n, np.int64), 0)
    x = np.maximum(n, 1).astype(np.float32) / np.float32(exact)
    large = exact + (np.log(x) / np.float32(math.log(T5_MAX_DIST / exact)) * np.float32(T5_BUCKETS - exact)).astype(np.int32)
    return np.where(n < exact, n, np.minimum(large, T5_BUCKETS - 1)).astype(np.int32)


def _bucket_thresholds():
    b = _t5_bucket_np(np.arange(4 * T5_MAX_DIST))
    return [int(np.argmax(b >= k)) for k in range(T5_BUCKETS // 2 + 1, T5_BUCKETS)]


def _bias_table_body(rbt_ref, oh_ref, o_ref):
    o_ref[...] = _dot(rbt_ref[...], oh_ref[...].astype(F32), precision=HI)


def _bias_tables(rel_bias, dists):
    n = len(dists)
    npad = -(-n // 512) * 512
    onehot = np.zeros((T5_BUCKETS, npad), np.float32)
    onehot[_t5_bucket_np(dists), np.arange(n)] = 1.0
    rbt = jnp.pad(rel_bias.T, ((0, SUBLANES - N_HEADS), (0, 0)))
    tn = 512
    out = pl.pallas_call(
        _bias_table_body,
        grid=(npad // tn,),
        in_specs=[pl.BlockSpec((SUBLANES, T5_BUCKETS), lambda i: (0, 0)),
                  pl.BlockSpec((T5_BUCKETS, tn), lambda i: (0, i))],
        out_specs=pl.BlockSpec((SUBLANES, tn), lambda i: (0, i)),
        out_shape=jax.ShapeDtypeStruct((SUBLANES, npad), F32),
        name="bias_tables",
    )(rbt, jnp.asarray(onehot, BF16))
    return out[:, :n]


def _prep_cmp_w(cmp_w, cmp_pe):
    eye2 = jnp.eye(2, dtype=F32)

    def half(w):
        full = jnp.einsum('cjde,cf->jcdfe', w, eye2)
        return full.reshape(CMP_STRIDE * 2 * HEAD_DIM, 2 * HEAD_DIM)

    wab = jnp.concatenate([half(cmp_w[:, :CMP_STRIDE]), half(cmp_w[:, CMP_STRIDE:])], axis=1)
    pe = jnp.concatenate([cmp_pe[:CMP_STRIDE].reshape(1, -1), cmp_pe[CMP_STRIDE:].reshape(1, -1)], axis=1)
    return wab, jnp.pad(pe, ((0, SUBLANES - 1), (0, 0)))


def _pe_term(pe_ref, wab_ref):
    kin = CMP_STRIDE * 2 * HEAD_DIM
    t = (_dot(pe_ref[:, 0:kin], wab_ref[:, 0:LANES].astype(F32), precision=HI)
         + _dot(pe_ref[:, kin:2 * kin], wab_ref[:, LANES:2 * LANES].astype(F32), precision=HI))
    return t[0:1, :]


def _combine_cmp(parts, pe_term):
    rows = parts.shape[0]
    nxt = pltpu.roll(parts[:, LANES:2 * LANES], rows - 1, 0)
    r = lax.broadcasted_iota(jnp.int32, (rows, LANES), 0)
    return jnp.where(r < rows - 1, parts[:, 0:LANES] + nxt + pe_term, 0.0)


def _compress_seq_body(r_ref, wab_ref, pe_ref, o_ref):
    parts = _dot(r_ref[0], wab_ref[...], precision=HI)
    o_ref[0] = _combine_cmp(parts, _pe_term(pe_ref, wab_ref))


def _compress_seq(rows, wab, pe):
    b, n16, kin = rows.shape
    return pl.pallas_call(
        _compress_seq_body,
        grid=(b,),
        in_specs=[pl.BlockSpec((1, n16, kin), lambda i: (i, 0, 0)),
                  pl.BlockSpec(wab.shape, lambda i: (0, 0)),
                  pl.BlockSpec(pe.shape, lambda i: (0, 0))],
        out_specs=pl.BlockSpec((1, n16, LANES), lambda i: (i, 0, 0)),
        out_shape=jax.ShapeDtypeStruct((b, n16, LANES), F32),
        compiler_params=_cparams("parallel"),
        name="nsa_compress_seq",
    )(rows, wab, pe)


def _bias_from_dist(dist, rb_ref, h):
    n = jnp.maximum(dist, 0)
    exact = T5_BUCKETS // 2
    out = jnp.full(dist.shape, rb_ref[exact, h], F32)
    for k, thr in enumerate(_bucket_thresholds()):
        out = jnp.where(n >= thr, rb_ref[exact + 1 + k, h], out)
    for j in range(exact):
        out = jnp.where(n == j, rb_ref[j, h], out)
    return out


def _softmax_rows(s, valid):
    s = jnp.where(valid, s, NEG_INF)
    m = jnp.max(s, axis=1, keepdims=True)
    m = jnp.where(m == NEG_INF, 0.0, m)
    p = jnp.exp(s - m)
    den = jnp.sum(p, axis=1, keepdims=True)
    return p / jnp.where(den > 0, den, 1.0)


def _top_blocks(score, k):
    lane = lax.broadcasted_iota(jnp.int32, score.shape, 1)
    width = score.shape[1]
    sel = jnp.zeros(score.shape, jnp.bool_)
    for _ in range(k):
        m = jnp.max(score, axis=1, keepdims=True)
        idx = jnp.min(jnp.where(score == m, lane, width), axis=1, keepdims=True)
        pick = (lane == idx) & (m > NEG_INF)
        sel = sel | pick
        score = jnp.where(pick, NEG_INF, score)
    return sel


def _tile_mask(ok):
    return jnp.concatenate([jnp.where(ok, 1.0, 0.0)] * N_HEADS, axis=0) > 0.5


def _stack_heads(x):
    return jnp.concatenate([_head(x, h) for h in range(N_HEADS)], axis=0)


def _toeplitz(tab_ref, h, r0, nrow, rows):
    strip = tab_ref[h, pl.ds(r0, nrow), :]
    flat = jnp.concatenate([strip[r:r + 1, :] for r in range(nrow)], axis=1)
    rolled = pltpu.roll(jnp.broadcast_to(flat, (rows, nrow * LANES)), 0, 1, stride=1, stride_axis=0)
    return rolled[:, LANES:]


SLC_KC = 512


def _nsa_seq_body(rb_ref, q_ref, sm_ref, nkv_ref, kvc_ref, tab_ref, ovl_ref, exp_ref, y_ref, msk_ref, *, seq, p0):
    qb = Q_BLOCK
    n16 = seq // CMP_STRIDE
    bi = pl.program_id(1)
    q0 = bi * qb
    qs = _stack_heads(q_ref[0]) * SCALE
    qsb = _bf(qs)
    qpos = q0 + lax.broadcasted_iota(jnp.int32, (qb, 1), 0)

    kvc = kvc_ref[0]
    s_c = _dot_nt(qs, kvc[:, 0:HEAD_DIM], precision=HI)
    cmp_end = lax.broadcasted_iota(jnp.int32, (qb, n16), 1) * CMP_STRIDE + (CMP_LEN - 1)
    dist_c = qpos - cmp_end
    bias_c = jnp.concatenate([_bias_from_dist(dist_c, rb_ref, h) for h in range(N_HEADS)], axis=0)
    valid_c = _tile_mask(dist_c >= 0)
    p_c = _softmax_rows(s_c + bias_c, valid_c)
    o_c = _dot(_bf(p_c), _bf(kvc[:, HEAD_DIM:2 * HEAD_DIM]))
    p_sum = p_c[0:qb] + p_c[qb:2 * qb] + p_c[2 * qb:3 * qb] + p_c[3 * qb:4 * qb]
    imp = _dot(p_sum, ovl_ref[...], precision=HI)

    lane = lax.broadcasted_iota(jnp.int32, (qb, LANES), 1)
    cur = qpos // SLC_BLOCK
    forced = (lane == 0) | (lane == cur) | (lane == cur - 1)
    score = jnp.where(forced, jnp.inf, imp)
    score = jnp.where(lane > cur, NEG_INF, score)
    selm = _bf(_top_blocks(score, SLC_TOPK).astype(F32))
    for c in range(seq // SLC_KC):
        msk_ref[c] = _dot(selm, exp_ref[:, c * SLC_KC:(c + 1) * SLC_KC])

    nrow_s = SLC_KC // LANES + 1
    icol = lax.broadcasted_iota(jnp.int32, (qb, SLC_KC), 0)
    jrow = lax.broadcasted_iota(jnp.int32, (qb, SLC_KC), 1)

    def chunk_step(c, carry):
        m, l, acc = carry
        k0 = pl.multiple_of(c * SLC_KC, SLC_KC)
        ks = _bf(nkv_ref[0, pl.ds(k0, SLC_KC), 2 * HEAD_DIM:3 * HEAD_DIM])
        vs = _bf(nkv_ref[0, pl.ds(k0, SLC_KC), 3 * HEAD_DIM:4 * HEAD_DIM])
        base = q0 - k0
        r0 = (p0 - base) // LANES - 1
        bias = jnp.concatenate([_toeplitz(tab_ref, h, r0, nrow_s, qb) for h in range(N_HEADS)], axis=0)
        ok = (msk_ref[c] > 0.5) & (base + icol - jrow >= 0)
        ok = _tile_mask(ok)
        s = jnp.where(ok, _dot_nt(qsb, ks) + bias, NEG_INF)
        m_new = jnp.maximum(m, jnp.max(s, axis=1, keepdims=True))
        m_safe = jnp.where(m_new == NEG_INF, 0.0, m_new)
        alpha = jnp.exp(m - m_safe)
        p = jnp.exp(s - m_safe)
        l = alpha * l + jnp.sum(p, axis=1, keepdims=True)
        acc = alpha * acc + _dot(_bf(p), vs)
        return m_new, l, acc

    init = (jnp.full((N_HEADS * qb, 1), NEG_INF, F32), jnp.zeros((N_HEADS * qb, 1), F32),
            jnp.zeros((N_HEADS * qb, HEAD_DIM), F32))
    _, l_s, acc_s = lax.fori_loop(0, q0 // SLC_KC + 1, chunk_step, init)
    o_s = acc_s / jnp.where(l_s > 0, l_s, 1.0)

    band = WINDOW + qb
    sw = pl.multiple_of(jnp.maximum(q0 - WINDOW, 0), qb)
    kw = _bf(nkv_ref[0, pl.ds(sw, band), 4 * HEAD_DIM:5 * HEAD_DIM])
    vw = _bf(nkv_ref[0, pl.ds(sw, band), 5 * HEAD_DIM:6 * HEAD_DIM])
    base_w = q0 - sw
    r0w = (p0 - base_w) // LANES - 1
    nrow_w = band // LANES + 1
    bias_w = jnp.concatenate([_toeplitz(tab_ref, h, r0w, nrow_w, qb) for h in range(N_HEADS)], axis=0)
    dist_w = (base_w + lax.broadcasted_iota(jnp.int32, (qb, band), 0)
              - lax.broadcasted_iota(jnp.int32, (qb, band), 1))
    ok_w = _tile_mask((dist_w >= 0) & (dist_w < WINDOW))
    p_w = _softmax_rows(_dot_nt(qsb, kw) + bias_w, ok_w)
    o_w = _dot(_bf(p_w), vw)

    gates = _sigmoid(sm_ref[0])
    outs = []
    for h in range(N_HEADS):
        rs = slice(h * qb, (h + 1) * qb)
        g = lambda t: gates[:, SM_NG + 3 * h + t:SM_NG + 3 * h + t + 1]
        outs.append(g(0) * o_c[rs] + g(1) * o_s[rs] + g(2) * o_w[rs])
    y_ref[0] = jnp.concatenate(outs, axis=1)


def _nsa_tab_seq_dists(seq):
    p0 = seq + Q_BLOCK
    length = p0 + WINDOW + Q_BLOCK + LANES
    return p0, np.maximum(p0 - np.arange(length), 0)


def _nsa_seq(rel_bias, nq, small, nkv, kvc, tab, seq_p0):
    b, l, _ = nq.shape
    n16 = l // CMP_STRIDE
    n_slc = l // SLC_BLOCK
    cmp_start = np.arange(n16)[:, None] * CMP_STRIDE
    slc_start = np.arange(LANES)[None, :] * SLC_BLOCK
    ovl = ((cmp_start < slc_start + SLC_BLOCK) & (cmp_start + CMP_LEN > slc_start)
           & (np.arange(n16)[:, None] < n16 - 1) & (np.arange(LANES)[None, :] < n_slc)).astype(np.float32)
    expand = (np.arange(LANES)[:, None] == (np.arange(l)[None, :] // SLC_BLOCK)).astype(np.float32)
    nrows = tab.shape[1]
    return pl.pallas_call(
        functools.partial(_nsa_seq_body, seq=l, p0=seq_p0),
        grid=(b, l // Q_BLOCK),
        in_specs=[pl.BlockSpec(memory_space=pltpu.SMEM),
                  pl.BlockSpec((1, Q_BLOCK, D_HEADS), lambda i, j: (i, j, 0)),
                  pl.BlockSpec((1, Q_BLOCK, LANES), lambda i, j: (i, j, 0)),
                  pl.BlockSpec((1, l, 6 * HEAD_DIM), lambda i, j: (i, 0, 0)),
                  pl.BlockSpec((1, n16, LANES), lambda i, j: (i, 0, 0)),
                  pl.BlockSpec((SUBLANES, nrows, LANES), lambda i, j: (0, 0, 0)),
                  pl.BlockSpec((n16, LANES), lambda i, j: (0, 0)),
                  pl.BlockSpec((LANES, l), lambda i, j: (0, 0))],
        out_specs=pl.BlockSpec((1, Q_BLOCK, D_HEADS), lambda i, j: (i, j, 0)),
        out_shape=jax.ShapeDtypeStruct((b, l, D_HEADS), F32),
        scratch_shapes=[pltpu.VMEM((l // SLC_KC, Q_BLOCK, SLC_KC), F32)],
        compiler_params=_cparams("parallel", "arbitrary"),
        name="nsa_seq",
    )(rel_bias, nq, small, nkv, kvc, tab, jnp.asarray(ovl), jnp.asarray(expand, BF16))


PAGES_PER_STEP = 16


def _pad_rows(x, rows=SUBLANES):
    return jnp.concatenate([x, jnp.zeros((rows - x.shape[0], x.shape[1]), x.dtype)], axis=0)


def _nsa_select_body(pt_ref, *refs, past, n_steps):
    del pt_ref
    pages = refs[:PAGES_PER_STEP]
    wab_ref, pe_ref, q_ref, tabc_ref, ovl_ref, oc_ref, top_ref, parts_ref, rows_ref = refs[PAGES_PER_STEP:]
    step = pl.program_id(1)
    blocks_per_page = PAGE_SIZE // CMP_STRIDE
    rows_per_step = PAGES_PER_STEP * blocks_per_page
    for p, pg in enumerate(pages):
        for c in range(2):
            rows_ref[p * PAGE_SIZE:(p + 1) * PAGE_SIZE, c * HEAD_DIM:(c + 1) * HEAD_DIM] = pg[c].T
    acc = jnp.zeros((rows_per_step, 2 * LANES), F32)
    for j in range(CMP_STRIDE):
        xj = rows_ref[pl.ds(j, rows_per_step, stride=CMP_STRIDE), :]
        acc = acc + _dot(_bf(xj), wab_ref[j * LANES:(j + 1) * LANES, :])
    parts_ref[pl.ds(pl.multiple_of(step * rows_per_step, rows_per_step), rows_per_step), :] = acc

    @pl.when(step == n_steps - 1)
    def _():
        n16 = past // CMP_STRIDE
        n_cmp = n16 - 1
        cur = past // SLC_BLOCK
        pe_term = _pe_term(pe_ref, wab_ref)
        kvc = _combine_cmp(parts_ref[...], pe_term)
        qs = _pad_rows(_stack_heads(q_ref[0]) * SCALE)
        s_c = _dot_nt(qs, kvc[:, 0:HEAD_DIM], precision=HI) + tabc_ref[...]
        lane = lax.broadcasted_iota(jnp.int32, (SUBLANES, n16), 1)
        p_c = _softmax_rows(s_c, lane < n_cmp)
        o_c = _dot(_bf(p_c), _bf(kvc[:, HEAD_DIM:2 * HEAD_DIM]))
        oc_ref[0] = jnp.concatenate([o_c[h:h + 1, :] for h in range(N_HEADS)], axis=1)
        p_sum = p_c[0:1] + p_c[1:2] + p_c[2:3] + p_c[3:4]
        imp = _dot(_pad_rows(p_sum), ovl_ref[...], precision=HI)[0:1, :]
        width = imp.shape[1]
        lane = lax.broadcasted_iota(jnp.int32, (1, width), 1)
        forced = (lane == 0) | (lane == cur) | (lane == cur - 1)
        score = jnp.where(forced, jnp.inf, imp)
        score = jnp.where(lane > cur, NEG_INF, score)
        out_lane = lax.broadcasted_iota(jnp.int32, (1, LANES), 1)
        top = jnp.zeros((1, LANES), jnp.int32)
        for r in range(SLC_TOPK):
            m = jnp.max(score, axis=1, keepdims=True)
            idx = jnp.min(jnp.where(score == m, lane, width), axis=1, keepdims=True)
            top = jnp.where(out_lane == r, idx, top)
            score = jnp.where(lane == idx, NEG_INF, score)
        top_ref[0] = top


def _nsa_select(layer, page_table, cache_t, wab_bf, pe, nq, tabc, past):
    b, n_pages = page_table.shape
    n_steps = n_pages // PAGES_PER_STEP
    n16 = past // CMP_STRIDE
    n_slc = past // SLC_BLOCK + 1
    width = -(-n_slc // LANES) * LANES
    cmp_start = np.arange(n16)[:, None] * CMP_STRIDE
    slc_start = np.arange(width)[None, :] * SLC_BLOCK
    ovl = ((cmp_start < slc_start + SLC_BLOCK) & (cmp_start + CMP_LEN > slc_start)
           & (np.arange(n16)[:, None] < n16 - 1) & (np.arange(width)[None, :] < n_slc)).astype(np.float32)

    n_pool = cache_t.shape[1]

    def page_spec(p):
        def imap(i, s, pt):
            pg = pt[jnp.minimum(i, b - 1), jnp.minimum(s, n_steps - 1) * PAGES_PER_STEP + p]
            return (layer, jnp.clip(pg, 0, n_pool - 1), 0, 0, 0)
        return pl.BlockSpec((None, None, 2, HEAD_DIM, PAGE_SIZE), imap)

    const = lambda shape: pl.BlockSpec(shape, lambda i, s, pt: (0,) * len(shape))
    grid_spec = pltpu.PrefetchScalarGridSpec(
        num_scalar_prefetch=1,
        grid=(b, n_steps),
        in_specs=[page_spec(p) for p in range(PAGES_PER_STEP)]
        + [const(wab_bf.shape), const(pe.shape), pl.BlockSpec((1, 1, D_HEADS), lambda i, s, pt: (i, 0, 0)),
           const(tabc.shape), const(ovl.shape)],
        out_specs=[pl.BlockSpec((1, 1, D_HEADS), lambda i, s, pt: (i, 0, 0)),
                   pl.BlockSpec((1, 1, LANES), lambda i, s, pt: (i, 0, 0))],
        scratch_shapes=[pltpu.VMEM((n16, 2 * LANES), F32),
                        pltpu.VMEM((PAGES_PER_STEP * PAGE_SIZE, 2 * HEAD_DIM), F32)],
    )
    return pl.pallas_call(
        functools.partial(_nsa_select_body, past=past, n_steps=n_steps),
        grid_spec=grid_spec,
        out_shape=[jax.ShapeDtypeStruct((b, 1, D_HEADS), F32), jax.ShapeDtypeStruct((b, 1, LANES), jnp.int32)],
        compiler_params=_cparams("parallel", "arbitrary"),
        name="nsa_select",
    )(page_table, *([cache_t] * PAGES_PER_STEP), wab_bf, pe, nq, tabc, jnp.asarray(ovl))


def _nsa_attend_body(pt_ref, top_ref, *refs, past):
    del pt_ref
    pages = refs[:SLC_TOPK]
    (q_ref, sm_ref, new_ref, oc_ref, swa_ref, tabs_ref, tabw_ref, y_ref, swa_out) = refs[SLC_TOPK:]
    b = pl.program_id(0)
    new_blk = past // SLC_BLOCK
    blocks_per_page = PAGE_SIZE // SLC_BLOCK
    qs = _bf(_pad_rows(_stack_heads(q_ref[0]) * SCALE))
    new = new_ref[0]
    lane_t = lax.broadcasted_iota(jnp.int32, (HEAD_DIM, PAGE_SIZE), 1)
    lane = lax.broadcasted_iota(jnp.int32, (SUBLANES, PAGE_SIZE), 1)
    new_kt = jnp.where(lane_t == 0, _row_to_col(new[:, 2 * HEAD_DIM:3 * HEAD_DIM]), 0.0)
    new_vt = jnp.where(lane_t == 0, _row_to_col(new[:, 3 * HEAD_DIM:4 * HEAD_DIM]), 0.0)
    scores, oks, vts = [], [], []
    for s in range(SLC_TOPK):
        ti = jnp.clip(top_ref[b, s], 0, new_blk)
        is_new = ti == new_blk
        kt = jnp.where(is_new, new_kt, pages[s][0])
        vts.append(jnp.where(is_new, new_vt, pages[s][1]))
        scores.append(_dot(qs, _bf(kt)) + tabs_ref[ti])
        kpos = ti * SLC_BLOCK + lane % SLC_BLOCK
        ok = (lane // SLC_BLOCK == ti % blocks_per_page) & (kpos <= past)
        oks.append(jnp.where(ok, 1.0, 0.0))
    p_s = _softmax_rows(jnp.concatenate(scores, axis=1), jnp.concatenate(oks, axis=1) > 0.5)
    o_s = None
    for s in range(SLC_TOPK):
        t = _dot_nt(_bf(p_s[:, s * PAGE_SIZE:(s + 1) * PAGE_SIZE]), _bf(vts[s]))
        o_s = t if o_s is None else o_s + t

    wlane = lax.broadcasted_iota(jnp.int32, (HEAD_DIM, WINDOW), 1)
    win = []
    for c in range(2):
        col = _row_to_col(new[:, (4 + c) * HEAD_DIM:(5 + c) * HEAD_DIM])
        win.append(jnp.where(wlane == WINDOW - 1, col, pltpu.roll(swa_ref[c], WINDOW - 1, 1)))
        swa_out[c] = win[c]
    s_w = _dot(qs, _bf(win[0])) + tabw_ref[...]
    p_w = _softmax_rows(s_w, jnp.full(s_w.shape, True))
    o_w = _dot_nt(_bf(p_w), _bf(win[1]))

    gates = _sigmoid(sm_ref[0])
    o_c = oc_ref[0]
    outs = []
    for h in range(N_HEADS):
        g = lambda t: gates[:, SM_NG + 3 * h + t:SM_NG + 3 * h + t + 1]
        outs.append(g(0) * _head(o_c, h) + g(1) * o_s[h:h + 1, :] + g(2) * o_w[h:h + 1, :])
    y_ref[0] = jnp.concatenate(outs, axis=1)


def _nsa_attend(layer, page_table, top, cache_t, nq, small, nkv, o_c, swa_t, tabs, tabw, past):
    b = nq.shape[0]
    last_page = past // PAGE_SIZE - 1
    blocks_per_page = PAGE_SIZE // SLC_BLOCK

    n_pool = cache_t.shape[1]

    def page_spec(s):
        def imap(i, pt, tp):
            ii = jnp.minimum(i, b - 1)
            pg = jnp.clip(tp[ii, s] // blocks_per_page, 0, last_page)
            return (layer, jnp.clip(pt[ii, pg], 0, n_pool - 1), 1, 0, 0)
        return pl.BlockSpec((None, None, 2, HEAD_DIM, PAGE_SIZE), imap)

    row = lambda n: pl.BlockSpec((1, 1, n), lambda i, pt, tp: (i, 0, 0))
    const = lambda shape: pl.BlockSpec(shape, lambda i, pt, tp: (0,) * len(shape))
    grid_spec = pltpu.PrefetchScalarGridSpec(
        num_scalar_prefetch=2,
        grid=(b,),
        in_specs=[page_spec(s) for s in range(SLC_TOPK)]
        + [row(D_HEADS), row(LANES), row(6 * HEAD_DIM), row(D_HEADS),
           pl.BlockSpec((None, None, 2, HEAD_DIM, WINDOW), lambda i, pt, tp: (layer, i, 0, 0, 0)),
           const(tabs.shape), const(tabw.shape)],
        out_specs=[row(D_HEADS), pl.BlockSpec((None, 2, HEAD_DIM, WINDOW), lambda i, pt, tp: (i, 0, 0, 0))],
    )
    return pl.pallas_call(
        functools.partial(_nsa_attend_body, past=past),
        grid_spec=grid_spec,
        out_shape=[jax.ShapeDtypeStruct((b, 1, D_HEADS), F32),
                   jax.ShapeDtypeStruct((b, 2, HEAD_DIM, WINDOW), F32)],
        compiler_params=_cparams("parallel"),
        name="nsa_attend",
    )(page_table, top, *([cache_t] * SLC_TOPK), nq, small, nkv, o_c, swa_t, tabs, tabw)


SEQ_CHUNK = 64
ROW_TILE = 256


def _lane_vec(pairs):
    v = jnp.zeros((1, LANES), F32)
    for off, vals in pairs:
        v = v.at[0, off:off + vals.shape[0]].set(vals.astype(F32))
    return v


def kernel(x_prompt, x_sample, cache_nsa_kv, cache_swa_kv, state_ret, state_mlstm_C, state_mlstm_n, state_mlstm_m, state_gdn, state_gdn_conv, state_ffn_conv, page_table, rel_bias, norm_pre_mix, w_in, mlstm_b_i, mlstm_b_f, gdn_conv_w, gdn_A_log, gdn_dt_bias, nsa_cmp_pe, nsa_cmp_w, ret_norm, mlstm_norm, gdn_norm, w_out, norm_post_mix, norm_pre_ffn, w_ffn_gate, w_ffn_up, ffn_conv_w, w_ffn_down, norm_post_ffn):
    depth = w_in.shape[0]
    bp, seq, _ = x_prompt.shape
    bs, dec_seq, _ = x_sample.shape
    assert dec_seq == 1 and seq % SLC_KC == 0 and seq >= WINDOW + Q_BLOCK
    n_pool = cache_nsa_kv.shape[1]
    past = page_table.shape[1] * PAGE_SIZE
    assert past >= WINDOW and past % SLC_BLOCK == 0 and page_table.shape[1] % PAGES_PER_STEP == 0

    p0, d_seq = _nsa_tab_seq_dists(seq)
    n16 = past // CMP_STRIDE
    n_slc = past // SLC_BLOCK + 1
    d_cmp = np.maximum(past - (np.arange(n16) * CMP_STRIDE + CMP_LEN - 1), 0)
    d_slc = np.maximum(past - np.arange(n_slc * SLC_BLOCK), 0)
    d_win = WINDOW - 1 - np.arange(WINDOW)
    tab = _bias_tables(rel_bias, np.concatenate([d_seq, d_cmp, d_slc, d_win]))
    o1, o2, o3 = len(d_seq), len(d_seq) + len(d_cmp), len(d_seq) + len(d_cmp) + len(d_slc)
    tab_seq = tab[:, :o1].reshape(SUBLANES, o1 // LANES, LANES)
    tabc = tab[:, o1:o2]
    tabs = jnp.transpose(tab[:, o2:o3].reshape(SUBLANES, n_slc, SLC_BLOCK), (1, 0, 2))
    tabs = jnp.concatenate([tabs, tabs], axis=2)
    tabw = tab[:, o3:]

    cache_t = jnp.transpose(cache_nsa_kv, (0, 1, 3, 4, 2))
    swa_t = jnp.transpose(cache_swa_kv, (0, 1, 3, 4, 2))
    cos_p, sin_p = _rope_tables(jnp.arange(seq))
    cos_s, sin_s = _rope_tables(jnp.arange(past, past + 1))

    xp = x_prompt
    xs = x_sample.reshape(bs, D_MODEL)
    p_states, s_states = [], []
    for l in range(depth):
        w_cat = _prep_w_in(w_in[l])
        w_out_bf = _bf(w_out[l])
        wg, wu, wd = _bf(w_ffn_gate[l]), _bf(w_ffn_up[l]), _bf(w_ffn_down[l])
        wab, pe = _prep_cmp_w(nsa_cmp_w[l], nsa_cmp_pe[l])
        ml_bias = _lane_vec([(SM_MI, mlstm_b_i[l]), (SM_MF, mlstm_b_f[l])])
        dtb = _lane_vec([(SM_GA, gdn_dt_bias[l])])
        alog = _lane_vec([(SM_GA, gdn_A_log[l])])

        z = _in_proj(xp.reshape(bp * seq, D_MODEL), norm_pre_mix[l], w_cat, ROW_TILE)
        z_ret, z_ml, z_gdn, z_nq, z_nkv, z_sm = [t.reshape(bp, seq, -1) for t in z]
        y_ret, st_ret = _retention_seq(z_ret, cos_p, sin_p, ret_norm[l], SEQ_CHUNK)
        y_ml, st_c, st_n, st_m = _mlstm_seq(z_ml, z_sm, ml_bias, mlstm_norm[l], SEQ_CHUNK)
        y_gdn, st_g = _gdn_seq(z_gdn, z_sm, gdn_conv_w[l], dtb, alog, gdn_norm[l], SEQ_CHUNK)
        rows = z_nkv[..., 0:2 * HEAD_DIM].reshape(bp, seq // CMP_STRIDE, CMP_STRIDE * 2 * HEAD_DIM)
        kvc = _compress_seq(rows, wab, pe)
        y_nsa = _nsa_seq(rel_bias, z_nq, z_sm, z_nkv, kvc, tab_seq, p0)
        ys = [t.reshape(bp * seq, D_HEADS) for t in (y_ret, y_ml, y_gdn, y_nsa)]
        x1, h2 = _out_proj(xp.reshape(bp * seq, D_MODEL), ys, w_out_bf, norm_post_mix[l], norm_pre_ffn[l], ROW_TILE)
        xp, tail = _ffn_seq(x1.reshape(bp, seq, D_MODEL), h2.reshape(bp, seq, D_MODEL), wg, wu, wd,
                            ffn_conv_w[l], norm_post_ffn[l], ROW_TILE)
        p_states.append((
            z_nkv[..., 0:4 * HEAD_DIM].reshape(bp, seq, 4, HEAD_DIM),
            z_nkv[:, seq - WINDOW:, 4 * HEAD_DIM:].reshape(bp, WINDOW, 2, HEAD_DIM),
            st_ret, st_c, st_n, st_m[:, 0, :N_HEADS], st_g,
            z_gdn[:, seq - (GDN_CONV - 1):, 0:3 * D_HEADS],
            tail[:, SUBLANES - (FFN_CONV - 1):, :]))

        z = _in_proj(xs, norm_pre_mix[l], w_cat, bs)
        z_ret, z_ml, z_gdn, z_nq, z_nkv, z_sm = [t.reshape(bs, 1, -1) for t in z]
        m_pad = jnp.pad(state_mlstm_m[l], ((0, 0), (0, LANES - N_HEADS))).reshape(bs, 1, LANES)
        (y_ret, y_ml, y_gdn, st_ret, st_c, st_n, st_m, st_g, st_gc) = _mixers_step(
            z_ret, z_ml, z_gdn, z_sm, cos_s, sin_s, ml_bias, dtb, alog, gdn_conv_w[l],
            ret_norm[l], mlstm_norm[l], gdn_norm[l],
            state_ret[l], state_mlstm_C[l], state_mlstm_n[l], m_pad, state_gdn[l], state_gdn_conv[l])
        o_c, top = _nsa_select(l, page_table, cache_t, _bf(wab), pe, z_nq, tabc, past)
        y_nsa, swa_new = _nsa_attend(l, page_table, top[:, 0, :SLC_TOPK], cache_t, z_nq, z_sm, z_nkv, o_c,
                                     swa_t, tabs, tabw, past)
        ys = [t.reshape(bs, D_HEADS) for t in (y_ret, y_ml, y_gdn, y_nsa)]
        x1, h2 = _out_proj(xs, ys, w_out_bf, norm_post_mix[l], norm_pre_ffn[l], bs)
        xs, g_new = _ffn_step(x1, h2, state_ffn_conv[l][:, 0], state_ffn_conv[l][:, 1], wg, wu, wd,
                              ffn_conv_w[l], norm_post_ffn[l])
        s_states.append((
            z_nkv[..., 0:4 * HEAD_DIM].reshape(bs, 1, 4, HEAD_DIM),
            jnp.transpose(swa_new, (0, 3, 1, 2)),
            st_ret, st_c, st_n, st_m[:, 0, :N_HEADS], st_g, st_gc,
            jnp.concatenate([state_ffn_conv[l][:, 1:], g_new[:, None, :]], axis=1)))

    stack = lambda states: tuple(jnp.stack([s[k] for s in states]) for k in range(len(states[0])))
    return (xp, xs.reshape(bs, 1, D_MODEL)) + stack(p_states) + stack(s_states)
```

```python
import functools
import math

import jax
import jax.numpy as jnp
import numpy as np
from jax import lax
from jax.experimental import pallas as pl
from jax.experimental.pallas import tpu as pltpu

F32 = jnp.float32
BF16 = jnp.bfloat16
HI = lax.Precision.HIGHEST

D_MODEL = 1024
HEAD_DIM = 64
N_HEADS = 4
D_HEADS = N_HEADS * HEAD_DIM
D_FF = 2816
GDN_CONV = 4
FFN_CONV = 3
PAGE_SIZE = 128
CMP_STRIDE = 16
CMP_LEN = 32
SLC_BLOCK = 64
SLC_TOPK = 16
WINDOW = 512
Q_BLOCK = 128
T5_BUCKETS = 32
T5_MAX_DIST = 1024
ROPE_BASE = 10000.0
EPS = 1e-6
SCALE = HEAD_DIM ** -0.5
IN_WIDTHS = (256, 256, 256, 256, 256, 256, 256, 4, 4, 256, 768, 4, 4, 256, 256, 384, 12)
LANES = 128
SUBLANES = 8
VMEM_LIMIT = 56 * 1024 * 1024
NEG_INF = float("-inf")

SM_MI, SM_MF, SM_GA, SM_GB, SM_NG = 0, 4, 8, 12, 16


def _cparams(*sem):
    return pltpu.CompilerParams(dimension_semantics=sem, vmem_limit_bytes=VMEM_LIMIT)


def _rms(x, g):
    return x * lax.rsqrt(jnp.mean(x * x, axis=-1, keepdims=True) + EPS) * g


def _dot(a, b, **kw):
    return jnp.dot(a, b, preferred_element_type=F32, **kw)


def _dot_nt(a, b, **kw):
    return lax.dot_general(a, b, (((1,), (1,)), ((), ())), preferred_element_type=F32, **kw)


def _dot_tn(a, b, **kw):
    return lax.dot_general(a, b, (((0,), (0,)), ((), ())), preferred_element_type=F32, **kw)


def _bf(x):
    return x.astype(BF16)


def _sigmoid(x):
    return 1.0 / (1.0 + jnp.exp(-x))


def _silu(x):
    return x * _sigmoid(x)


def _softplus(x):
    return jnp.maximum(x, 0.0) + jnp.log(1.0 + jnp.exp(-jnp.abs(x)))


def _head_rms(o):
    return o * lax.rsqrt(jnp.mean(o * o, axis=-1, keepdims=True) + EPS)


PROJ_WIDTHS = (1024, 1024, 1024, 256, 384, 128)


def _prep_w_in(w):
    parts, off = [], 0
    for wd in IN_WIDTHS:
        parts.append(w[:, off:off + wd])
        off += wd
    (rq, rk, rv, rg, mq, mk, mv, mi, mf, mo, gqkv, ga, gbeta, gg, nq, nkv, ngate) = parts
    small = jnp.concatenate([mi, mf, ga, gbeta, ngate], axis=1)
    small = jnp.pad(small, ((0, 0), (0, LANES - small.shape[1])))
    cat = jnp.concatenate([rq, rk, rv, rg, mq, mk, mv, mo, gqkv, gg, nq, nkv, small], axis=1)
    return cat.astype(BF16)


def _in_proj_body(x_ref, g_ref, w_ref, *out_refs):
    h = _bf(_rms(x_ref[...], g_ref[...]))
    off = 0
    for ref in out_refs:
        n = ref.shape[-1]
        ref[...] = _dot(h, w_ref[:, off:off + n])
        off += n


def _in_proj(x2d, g, w_cat, tm):
    m = x2d.shape[0]
    ntot = sum(PROJ_WIDTHS)
    return pl.pallas_call(
        _in_proj_body,
        grid=(m // tm,),
        in_specs=[pl.BlockSpec((tm, D_MODEL), lambda i: (i, 0)),
                  pl.BlockSpec((1, D_MODEL), lambda i: (0, 0)),
                  pl.BlockSpec((D_MODEL, ntot), lambda i: (0, 0))],
        out_specs=[pl.BlockSpec((tm, n), lambda i: (i, 0)) for n in PROJ_WIDTHS],
        out_shape=[jax.ShapeDtypeStruct((m, n), F32) for n in PROJ_WIDTHS],
        compiler_params=_cparams("parallel"),
        name="in_proj",
    )(x2d, g.reshape(1, D_MODEL), w_cat)


def _out_proj_body(x_ref, y0, y1, y2, y3, w_ref, gpost_ref, gpre_ref, x1_ref, h2_ref):
    acc = None
    for k, y in enumerate((y0, y1, y2, y3)):
        t = _dot(_bf(y[...]), w_ref[k * D_HEADS:(k + 1) * D_HEADS, :])
        acc = t if acc is None else acc + t
    x1 = x_ref[...] + _rms(acc, gpost_ref[...])
    x1_ref[...] = x1
    h2_ref[...] = _bf(_rms(x1, gpre_ref[...]))


def _out_proj(x2d, ys, w_out_bf, gpost, gpre, tm):
    m = x2d.shape[0]
    row = lambda n: pl.BlockSpec((tm, n), lambda i: (i, 0))
    vec = pl.BlockSpec((1, D_MODEL), lambda i: (0, 0))
    return pl.pallas_call(
        _out_proj_body,
        grid=(m // tm,),
        in_specs=[row(D_MODEL)] + [row(D_HEADS)] * 4
        + [pl.BlockSpec((D_MODEL, D_MODEL), lambda i: (0, 0)), vec, vec],
        out_specs=[row(D_MODEL), row(D_MODEL)],
        out_shape=[jax.ShapeDtypeStruct((m, D_MODEL), F32), jax.ShapeDtypeStruct((m, D_MODEL), BF16)],
        compiler_params=_cparams("parallel"),
        name="out_proj",
    )(x2d, *ys, w_out_bf, gpost.reshape(1, -1), gpre.reshape(1, -1))


FFN_TN = 256


def _gelu_tanh(x):
    return 0.5 * x * (1.0 + jnp.tanh(0.7978845608028654 * (x + 0.044715 * x * x * x)))


def _ffn_seq_body(x1_ref, h2_ref, wg_ref, wu_ref, wd_ref, cw_ref, gpost_ref, x2_ref, tail_ref, carry_ref, *, tm):
    @pl.when(pl.program_id(1) == 0)
    def _():
        carry_ref[...] = jnp.zeros_like(carry_ref)

    h2 = h2_ref[0]
    rows = lax.broadcasted_iota(jnp.int32, (tm, FFN_TN), 0)
    acc = jnp.zeros((tm, D_MODEL), F32)
    for n0 in range(0, D_FF, FFN_TN):
        g = _dot(h2, wg_ref[:, n0:n0 + FFN_TN])
        prev = carry_ref[:, n0:n0 + FFN_TN]
        g1 = jnp.where(rows == 0, prev[7:8, :], pltpu.roll(g, 1, 0))
        g2 = pltpu.roll(g, 2, 0)
        g2 = jnp.where(rows == 0, prev[6:7, :], jnp.where(rows == 1, prev[7:8, :], g2))
        cw = cw_ref[:, n0:n0 + FFN_TN]
        gate = cw[0:1, :] * g2 + cw[1:2, :] * g1 + cw[2:3, :] * g
        u = _dot(h2, wu_ref[:, n0:n0 + FFN_TN])
        a = _gelu_tanh(gate) * u
        acc = acc + _dot(_bf(a), wd_ref[n0:n0 + FFN_TN, :])
        carry_ref[:, n0:n0 + FFN_TN] = g[tm - SUBLANES:tm, :]
    x2_ref[0] = x1_ref[0] + _rms(acc, gpost_ref[...])
    tail_ref[0] = carry_ref[...]


def _ffn_seq(x1, h2, wg, wu, wd, cw, gpost, tm):
    b, l, _ = x1.shape
    const = lambda shape: pl.BlockSpec(shape, lambda i, j: (0,) * len(shape))
    return pl.pallas_call(
        functools.partial(_ffn_seq_body, tm=tm),
        grid=(b, l // tm),
        in_specs=[pl.BlockSpec((1, tm, D_MODEL), lambda i, j: (i, j, 0)),
                  pl.BlockSpec((1, tm, D_MODEL), lambda i, j: (i, j, 0)),
                  const((D_MODEL, D_FF)), const((D_MODEL, D_FF)), const((D_FF, D_MODEL)),
                  const((FFN_CONV, D_FF)), const((1, D_MODEL))],
        out_specs=[pl.BlockSpec((1, tm, D_MODEL), lambda i, j: (i, j, 0)),
                   pl.BlockSpec((1, SUBLANES, D_FF), lambda i, j: (i, 0, 0))],
        out_shape=[jax.ShapeDtypeStruct((b, l, D_MODEL), F32),
                   jax.ShapeDtypeStruct((b, SUBLANES, D_FF), F32)],
        scratch_shapes=[pltpu.VMEM((SUBLANES, D_FF), F32)],
        compiler_params=_cparams("parallel", "arbitrary"),
        name="ffn_seq",
    )(x1, h2, wg, wu, wd, cw, gpost.reshape(1, -1))


def _ffn_step_body(x1_ref, h2_ref, b0_ref, b1_ref, wg_ref, wu_ref, wd_ref, cw_ref, gpost_ref, x2_ref, g_ref):
    h2 = h2_ref[...]
    acc = jnp.zeros(x1_ref.shape, F32)
    for n0 in range(0, D_FF, FFN_TN):
        sl = slice(n0, n0 + FFN_TN)
        g = _dot(h2, wg_ref[:, sl])
        cw = cw_ref[:, sl]
        gate = cw[0:1, :] * b0_ref[:, sl] + cw[1:2, :] * b1_ref[:, sl] + cw[2:3, :] * g
        u = _dot(h2, wu_ref[:, sl])
        acc = acc + _dot(_bf(_gelu_tanh(gate) * u), wd_ref[sl, :])
        g_ref[:, sl] = g
    x2_ref[...] = x1_ref[...] + _rms(acc, gpost_ref[...])


def _ffn_step(x1, h2, b0, b1, wg, wu, wd, cw, gpost):
    m = x1.shape[0]
    return pl.pallas_call(
        _ffn_step_body,
        out_shape=[jax.ShapeDtypeStruct((m, D_MODEL), F32), jax.ShapeDtypeStruct((m, D_FF), F32)],
        compiler_params=pltpu.CompilerParams(vmem_limit_bytes=VMEM_LIMIT),
        name="ffn_step",
    )(x1, h2, b0, b1, wg, wu, wd, cw, gpost.reshape(1, -1))


def _rope_tables(pos):
    half = HEAD_DIM // 2
    inv = ROPE_BASE ** (-jnp.linspace(0.0, 1.0, half, dtype=F32))
    ang = pos.astype(F32)[:, None] * inv[None, :]
    return jnp.tile(jnp.cos(ang), (1, LANES // half)), jnp.tile(jnp.sin(ang), (1, LANES // half))


def _rope128(x, cos, sin):
    lane = lax.broadcasted_iota(jnp.int32, x.shape, 1)
    first = (lane % HEAD_DIM) < (HEAD_DIM // 2)
    other = jnp.where(first, -pltpu.roll(x, LANES - HEAD_DIM // 2, 1), pltpu.roll(x, HEAD_DIM // 2, 1))
    return x * cos + other * sin


def _rope256(x, cos, sin):
    return jnp.concatenate([_rope128(x[:, :LANES], cos, sin), _rope128(x[:, LANES:], cos, sin)], axis=1)


def _ret_log_decay(h):
    return math.log(1.0 - 2.0 ** (-5.0 - h))


def _head(x, h):
    return x[:, h * HEAD_DIM:(h + 1) * HEAD_DIM]


def _ret_body(z_ref, cos_ref, sin_ref, nw_ref, y_ref, s_ref, *, chunk):
    @pl.when(pl.program_id(1) == 0)
    def _():
        s_ref[...] = jnp.zeros_like(s_ref)

    nb = z_ref.shape[0]
    cos, sin = cos_ref[...], sin_ref[...]
    t = lax.broadcasted_iota(jnp.int32, (chunk, chunk), 0)
    s = lax.broadcasted_iota(jnp.int32, (chunk, chunk), 1)
    causal = t >= s
    diff = jnp.where(causal, t - s, 0).astype(F32)
    tcol = lax.broadcasted_iota(jnp.int32, (chunk, 1), 0).astype(F32)
    lgs = [_ret_log_decay(h) for h in range(N_HEADS)]
    dmat = [jnp.where(causal, jnp.exp(lg * diff), 0.0) for lg in lgs]
    xi = [jnp.exp(lg * (tcol + 1.0)) for lg in lgs]
    zeta = [jnp.exp(lg * (chunk - 1.0 - tcol)) for lg in lgs]
    probs = [(bb, h) for bb in range(nb) for h in range(N_HEADS)]
    qkv = []
    for bb in range(nb):
        q = _rope256(z_ref[bb, :, 0:256], cos, sin)
        k = _rope256(z_ref[bb, :, 256:512], cos, sin) * SCALE
        qkv.append((q, k, z_ref[bb, :, 512:768]))
    qh = [_head(qkv[bb][0], h) for bb, h in probs]
    kh = [_head(qkv[bb][1], h) for bb, h in probs]
    vh = [_bf(_head(qkv[bb][2], h)) for bb, h in probs]
    st = [s_ref[bb, h] for bb, h in probs]
    sc = [_dot_nt(_bf(q), _bf(k)) for q, k in zip(qh, kh)]
    cross = [_dot(_bf(q * xi[h]), _bf(s0)) for q, s0, (_, h) in zip(qh, st, probs)]
    upd = [_dot_tn(_bf(k * zeta[h]), v) for k, v, (_, h) in zip(kh, vh, probs)]
    o = [_dot(_bf(x * dmat[h]), v) + c for x, v, c, (_, h) in zip(sc, vh, cross, probs)]
    for i, (bb, h) in enumerate(probs):
        s_ref[bb, h] = st[i] * math.exp(lgs[h] * chunk) + upd[i]
    for bb in range(nb):
        outs = [_head_rms(o[bb * N_HEADS + h]) for h in range(N_HEADS)]
        y_ref[bb] = jnp.concatenate(outs, axis=1) * nw_ref[...] * _silu(z_ref[bb, :, 768:1024])


def _retention_seq(z_ret, cos, sin, nw, chunk, nb):
    b, l, _ = z_ret.shape
    return pl.pallas_call(
        functools.partial(_ret_body, chunk=chunk),
        grid=(b // nb, l // chunk),
        in_specs=[pl.BlockSpec((nb, chunk, 1024), lambda i, j: (i, j, 0)),
                  pl.BlockSpec((chunk, LANES), lambda i, j: (j, 0)),
                  pl.BlockSpec((chunk, LANES), lambda i, j: (j, 0)),
                  pl.BlockSpec((1, D_HEADS), lambda i, j: (0, 0))],
        out_specs=[pl.BlockSpec((nb, chunk, D_HEADS), lambda i, j: (i, j, 0)),
                   pl.BlockSpec((nb, N_HEADS, HEAD_DIM, HEAD_DIM), lambda i, j: (i, 0, 0, 0))],
        out_shape=[jax.ShapeDtypeStruct((b, l, D_HEADS), F32),
                   jax.ShapeDtypeStruct((b, N_HEADS, HEAD_DIM, HEAD_DIM), F32)],
        compiler_params=_cparams("parallel", "arbitrary"),
        name="retention_seq",
    )(z_ret, cos, sin, nw.reshape(1, -1))


def _lanes_to_rows(x, n=2 * SUBLANES):
    sel = (lax.broadcasted_iota(jnp.int32, (n, x.shape[1]), 0)
           == lax.broadcasted_iota(jnp.int32, (n, x.shape[1]), 1)).astype(F32)
    return _dot_nt(sel, x, precision=HI)


def _tri_incl(n):
    t = lax.broadcasted_iota(jnp.int32, (n, n), 0)
    s = lax.broadcasted_iota(jnp.int32, (n, n), 1)
    return t >= s


def _ml_body(z_ref, sm_ref, bias_ref, nw_ref, y_ref, c_ref, n_ref, m_ref, *, chunk):
    @pl.when(pl.program_id(1) == 0)
    def _():
        c_ref[...] = jnp.zeros_like(c_ref)
        n_ref[...] = jnp.zeros_like(n_ref)
        m_ref[...] = jnp.zeros_like(m_ref)

    nb = z_ref.shape[0]
    incl = _tri_incl(chunk)
    tri = incl.astype(F32)
    probs = [(bb, h) for bb in range(nb) for h in range(N_HEADS)]
    gate = []
    for bb in range(nb):
        pre = sm_ref[bb] + bias_ref[...]
        logf = jnp.minimum(pre, 0.0) - jnp.log(1.0 + jnp.exp(-jnp.abs(pre)))
        bcum = _dot(tri, logf, precision=HI)
        gate.append((pre, bcum, _lanes_to_rows(pre), _lanes_to_rows(bcum), m_ref[bb]))
    qh = [_head(z_ref[bb, :, 0:256], h) for bb, h in probs]
    kh = [_head(z_ref[bb, :, 256:512], h) * SCALE for bb, h in probs]
    vh = [_bf(_head(z_ref[bb, :, 512:768], h)) for bb, h in probs]
    cm = [c_ref[bb, h] for bb, h in probs]
    nv = [n_ref[bb, h:h + 1, :] for bb, h in probs]
    qk_raw = [_dot_nt(_bf(q), _bf(k)) for q, k in zip(qh, kh)]
    q_c = [_dot(_bf(q), _bf(c)) for q, c in zip(qh, cm)]
    i_col, b_col, inter, dlog = [], [], [], []
    for bb, h in probs:
        pre, bcum, pre_t, bcum_t, m_all = gate[bb]
        i_col.append(pre[:, SM_MI + h:SM_MI + h + 1])
        b_col.append(bcum[:, SM_MF + h:SM_MF + h + 1])
        inter.append(b_col[-1] + m_all[:, h:h + 1])
        dlog.append(jnp.where(incl, b_col[-1] - bcum_t[SM_MF + h:SM_MF + h + 1, :]
                              + pre_t[SM_MI + h:SM_MI + h + 1, :], NEG_INF))
    dmax = [jnp.max(x, axis=1, keepdims=True) for x in dlog]
    q_n = [jnp.sum(q * n, axis=1, keepdims=True) for q, n in zip(qh, nv)]
    stab = []
    for i in range(len(probs)):
        m_t = jnp.maximum(inter[i], dmax[i])
        m_end = m_t[chunk - 1:chunk, :]
        stab.append(dict(m_t=m_t, w=jnp.exp(dlog[i] - m_t), g_in=jnp.exp(inter[i] - m_t), m_end=m_end,
                         w_end=jnp.exp(b_col[i][chunk - 1:chunk, :] - b_col[i] + i_col[i] - m_end),
                         g_end=jnp.exp(inter[i][chunk - 1:chunk, :] - m_end)))
    qk = [x * d["w"] for x, d in zip(qk_raw, stab)]
    num = [_dot(_bf(x), v) + d["g_in"] * c for x, v, d, c in zip(qk, vh, stab, q_c)]
    upd = [_dot_tn(_bf(k * d["w_end"]), v) for k, d, v in zip(kh, stab, vh)]
    qk_sum = [jnp.sum(x, axis=1, keepdims=True) for x in qk]
    k_sum = [jnp.sum(d["w_end"] * k, axis=0, keepdims=True) for d, k in zip(stab, kh)]
    hh = []
    for i, (bb, h) in enumerate(probs):
        d = stab[i]
        den = qk_sum[i] + d["g_in"] * q_n[i]
        hh.append(num[i] / jnp.maximum(jnp.abs(den), jnp.exp(-d["m_t"])))
        c_ref[bb, h] = d["g_end"] * cm[i] + upd[i]
        n_ref[bb, h:h + 1, :] = d["g_end"] * nv[i] + k_sum[i]
    lane = lax.broadcasted_iota(jnp.int32, (1, LANES), 1)
    for bb in range(nb):
        m_out = jnp.zeros((1, LANES), F32)
        outs = []
        for h in range(N_HEADS):
            i = bb * N_HEADS + h
            m_out = jnp.where(lane == h, stab[i]["m_end"], m_out)
            outs.append(_head_rms(_sigmoid(_head(z_ref[bb, :, 768:1024], h)) * hh[i]))
        m_ref[bb] = m_out
        y_ref[bb] = jnp.concatenate(outs, axis=1) * nw_ref[...]


def _mlstm_seq(z_ml, small, bias_vec, nw, chunk, nb):
    b, l, _ = z_ml.shape
    return pl.pallas_call(
        functools.partial(_ml_body, chunk=chunk),
        grid=(b // nb, l // chunk),
        in_specs=[pl.BlockSpec((nb, chunk, 1024), lambda i, j: (i, j, 0)),
                  pl.BlockSpec((nb, chunk, LANES), lambda i, j: (i, j, 0)),
                  pl.BlockSpec((1, LANES), lambda i, j: (0, 0)),
                  pl.BlockSpec((1, D_HEADS), lambda i, j: (0, 0))],
        out_specs=[pl.BlockSpec((nb, chunk, D_HEADS), lambda i, j: (i, j, 0)),
                   pl.BlockSpec((nb, N_HEADS, HEAD_DIM, HEAD_DIM), lambda i, j: (i, 0, 0, 0)),
                   pl.BlockSpec((nb, N_HEADS, HEAD_DIM), lambda i, j: (i, 0, 0)),
                   pl.BlockSpec((nb, 1, LANES), lambda i, j: (i, 0, 0))],
        out_shape=[jax.ShapeDtypeStruct((b, l, D_HEADS), F32),
                   jax.ShapeDtypeStruct((b, N_HEADS, HEAD_DIM, HEAD_DIM), F32),
                   jax.ShapeDtypeStruct((b, N_HEADS, HEAD_DIM), F32),
                   jax.ShapeDtypeStruct((b, 1, LANES), F32)],
        compiler_params=_cparams("parallel", "arbitrary"),
        name="mlstm_seq",
    )(z_ml, small, bias_vec, nw.reshape(1, -1))


def _unit_lower_inverse(a, n):
    t = lax.broadcasted_iota(jnp.int32, (n, n), 0)
    s = lax.broadcasted_iota(jnp.int32, (n, n), 1)
    eye = (t == s).astype(F32)
    p = eye - a
    pw = a
    k = 2
    while k < n:
        pw = _dot(pw, pw, precision=HI)
        p = p + _dot(p, pw, precision=HI)
        k *= 2
    return p


def _l2norm(x):
    return x * lax.rsqrt(jnp.sum(x * x, axis=-1, keepdims=True) + EPS)


def _gdn_body(z_ref, sm_ref, cw_ref, dtb_ref, alog_ref, nw_ref, y_ref, s_ref, buf_ref, *, chunk):
    @pl.when(pl.program_id(1) == 0)
    def _():
        s_ref[...] = jnp.zeros_like(s_ref)
        buf_ref[:, 0:SUBLANES, :] = jnp.zeros((buf_ref.shape[0], SUBLANES, 3 * D_HEADS), F32)

    nb = z_ref.shape[0]
    incl = _tri_incl(chunk)
    strict = lax.broadcasted_iota(jnp.int32, (chunk, chunk), 0) > lax.broadcasted_iota(jnp.int32, (chunk, chunk), 1)
    eye = (lax.broadcasted_iota(jnp.int32, (chunk, chunk), 0)
           == lax.broadcasted_iota(jnp.int32, (chunk, chunk), 1)).astype(F32)
    tri = incl.astype(F32)
    gates, acts = [], []
    for bb in range(nb):
        buf_ref[bb, SUBLANES:SUBLANES + chunk, :] = z_ref[bb, :, 0:768]
        conv = None
        for j in range(GDN_CONV):
            term = buf_ref[bb, pl.ds(SUBLANES - (GDN_CONV - 1) + j, chunk), :] * cw_ref[j:j + 1, :]
            conv = term if conv is None else conv + term
        buf_ref[bb, 0:SUBLANES, :] = buf_ref[bb, chunk:chunk + SUBLANES, :]
        acts.append(_silu(conv))
        sm = sm_ref[bb]
        g_all = -jnp.exp(alog_ref[...]) * _softplus(sm + dtb_ref[...])
        gcum = _dot(tri, g_all, precision=HI)
        gates.append((gcum, _lanes_to_rows(gcum), _sigmoid(sm)))

    probs = [(bb, h) for bb in range(nb) for h in range(N_HEADS)]
    pre = []
    for bb, h in probs:
        gcum, gcum_t, beta_all = gates[bb]
        g_col = gcum[:, SM_GA + h:SM_GA + h + 1]
        g_row = gcum_t[SM_GA + h:SM_GA + h + 1, :]
        beta = beta_all[:, SM_GB + h:SM_GB + h + 1]
        decay = jnp.where(incl, jnp.exp(jnp.where(incl, g_col - g_row, 0.0)), 0.0)
        act = acts[bb]
        qh = _l2norm(_head(act[:, 0:256], h)) * SCALE
        kh = _l2norm(_head(act[:, 256:512], h))
        vh = _head(act[:, 512:768], h)
        eg = jnp.exp(g_col)
        pre.append(dict(g_col=g_col, beta=beta, decay=decay, qh=qh, kh=kh, kb=_bf(kh), eg=eg,
                        rhs=jnp.concatenate([vh * beta, kh * (beta * eg)], axis=1)))
    a_l = [jnp.where(strict, d["beta"] * _dot_nt(d["kb"], d["kb"]) * d["decay"], 0.0) for d in pre]
    p_l = [eye - a for a in a_l]
    pw_l = a_l
    lvl = 2
    while lvl < chunk:
        pw_l = [_dot(x, x, precision=HI) for x in pw_l]
        p_l = [p + _dot(p, x, precision=HI) for p, x in zip(p_l, pw_l)]
        lvl *= 2
    uw_l = [_dot(p, d["rhs"], precision=HI) for p, d in zip(p_l, pre)]
    st_l = [s_ref[bb, h] for bb, h in probs]
    stb_l = [_bf(st) for st in st_l]
    delta_l = [uw[:, 0:HEAD_DIM] - _dot(_bf(uw[:, HEAD_DIM:]), stb) for uw, stb in zip(uw_l, stb_l)]
    qk_l = [_dot_nt(_bf(d["qh"]), d["kb"]) * d["decay"] for d in pre]
    o_l = [_dot(_bf(qk), _bf(delta)) + _dot(_bf(d["qh"] * d["eg"]), stb)
           for qk, delta, d, stb in zip(qk_l, delta_l, pre, stb_l)]
    for (bb, h), d, st, delta in zip(probs, pre, st_l, delta_l):
        g_end = d["g_col"][chunk - 1:chunk, :]
        w_end = jnp.exp(g_end - d["g_col"])
        s_ref[bb, h] = jnp.exp(g_end) * st + _dot_tn(_bf(d["kh"] * w_end), _bf(delta))
    for bb in range(nb):
        outs = [_head_rms(o_l[bb * N_HEADS + h]) for h in range(N_HEADS)]
        y_ref[bb] = jnp.concatenate(outs, axis=1) * nw_ref[...] * _silu(z_ref[bb, :, 768:1024])


def _gdn_seq(z_gdn, small, conv_w, dtb_vec, alog_vec, nw, chunk, nb):
    b, l, _ = z_gdn.shape
    return pl.pallas_call(
        functools.partial(_gdn_body, chunk=chunk),
        grid=(b // nb, l // chunk),
        in_specs=[pl.BlockSpec((nb, chunk, 1024), lambda i, j: (i, j, 0)),
                  pl.BlockSpec((nb, chunk, LANES), lambda i, j: (i, j, 0)),
                  pl.BlockSpec((GDN_CONV, 3 * D_HEADS), lambda i, j: (0, 0)),
                  pl.BlockSpec((1, LANES), lambda i, j: (0, 0)),
                  pl.BlockSpec((1, LANES), lambda i, j: (0, 0)),
                  pl.BlockSpec((1, D_HEADS), lambda i, j: (0, 0))],
        out_specs=[pl.BlockSpec((nb, chunk, D_HEADS), lambda i, j: (i, j, 0)),
                   pl.BlockSpec((nb, N_HEADS, HEAD_DIM, HEAD_DIM), lambda i, j: (i, 0, 0, 0))],
        out_shape=[jax.ShapeDtypeStruct((b, l, D_HEADS), F32),
                   jax.ShapeDtypeStruct((b, N_HEADS, HEAD_DIM, HEAD_DIM), F32)],
        scratch_shapes=[pltpu.VMEM((nb, SUBLANES + chunk, 3 * D_HEADS), F32)],
        compiler_params=_cparams("parallel", "arbitrary"),
        name="gdn_seq",
    )(z_gdn, small, conv_w, dtb_vec, alog_vec, nw.reshape(1, -1))


def _row_to_col(row):
    n = row.shape[1]
    eye = lax.broadcasted_iota(jnp.int32, (n, n), 0) == lax.broadcasted_iota(jnp.int32, (n, n), 1)
    return jnp.sum(jnp.where(eye, jnp.broadcast_to(row, (n, n)), 0.0), axis=1, keepdims=True)


def _vec_mat(col, mat):
    return jnp.sum(col * mat, axis=0, keepdims=True)


def _step_body(zr_ref, zm_ref, zg_ref, sm_ref, cos_ref, sin_ref, mlb_ref, dtb_ref, alog_ref, cw_ref,
               nwr_ref, nwm_ref, nwg_ref, sr_ref, mc_ref, mn_ref, mm_ref, gs_ref, gc_ref,
               yr_ref, ym_ref, yg_ref, sr_o, mc_o, mn_o, mm_o, gs_o, gc_o):
    sm = sm_ref[0]
    lane = lax.broadcasted_iota(jnp.int32, (1, LANES), 1)
    zr = zr_ref[0]
    cos, sin = cos_ref[...], sin_ref[...]
    q = _rope256(zr[:, 0:256], cos, sin)
    k = _rope256(zr[:, 256:512], cos, sin) * SCALE
    v = zr[:, 512:768]
    outs = []
    for h in range(N_HEADS):
        gam = math.exp(_ret_log_decay(h))
        qh, kh, vh = _head(q, h), _head(k, h), _head(v, h)
        st = sr_ref[0, h]
        o = jnp.sum(qh * kh, axis=1, keepdims=True) * vh + _vec_mat(_row_to_col(qh) * gam, st)
        sr_o[0, h] = st * gam + _row_to_col(kh) * vh
        outs.append(_head_rms(o))
    yr_ref[0] = jnp.concatenate(outs, axis=1) * nwr_ref[...] * _silu(zr[:, 768:1024])
    zm = zm_ref[0]
    q, k, v, og = zm[:, 0:256], zm[:, 256:512] * SCALE, zm[:, 512:768], zm[:, 768:1024]
    pre = sm + mlb_ref[...]
    logf = jnp.minimum(pre, 0.0) - jnp.log(1.0 + jnp.exp(-jnp.abs(pre)))
    m_all = mm_ref[0]
    m_out = jnp.zeros((1, LANES), F32)
    outs = []
    for h in range(N_HEADS):
        i_g = pre[:, SM_MI + h:SM_MI + h + 1]
        b_g = logf[:, SM_MF + h:SM_MF + h + 1]
        inter = b_g + m_all[:, h:h + 1]
        m_t = jnp.maximum(inter, i_g)
        w = jnp.exp(i_g - m_t)
        g_in = jnp.exp(inter - m_t)
        qh, kh, vh = _head(q, h), _head(k, h), _head(v, h)
        cm, nv = mc_ref[0, h], mn_ref[0, h:h + 1, :]
        qk = jnp.sum(qh * kh, axis=1, keepdims=True) * w
        num = qk * vh + g_in * _vec_mat(_row_to_col(qh), cm)
        den = qk + g_in * jnp.sum(qh * nv, axis=1, keepdims=True)
        hh = num / jnp.maximum(jnp.abs(den), jnp.exp(-m_t))
        mc_o[0, h] = g_in * cm + _row_to_col(kh * w) * vh
        mn_o[0, h:h + 1, :] = g_in * nv + w * kh
        m_out = jnp.where(lane == h, m_t, m_out)
        outs.append(_head_rms(_sigmoid(_head(og, h)) * hh))
    mm_o[0] = m_out
    ym_ref[0] = jnp.concatenate(outs, axis=1) * nwm_ref[...]
    zg = zg_ref[0]
    x = zg[:, 0:768]
    buf = gc_ref[0]
    conv = x * cw_ref[GDN_CONV - 1:GDN_CONV, :]
    for j in range(GDN_CONV - 1):
        conv = conv + buf[j:j + 1, :] * cw_ref[j:j + 1, :]
    gc_o[0, 0:GDN_CONV - 2, :] = buf[1:GDN_CONV - 1, :]
    gc_o[0, GDN_CONV - 2:GDN_CONV - 1, :] = x
    act = _silu(conv)
    q, k, v = act[:, 0:256], act[:, 256:512], act[:, 512:768]
    g_all = -jnp.exp(alog_ref[...]) * _softplus(sm + dtb_ref[...])
    beta_all = _sigmoid(sm)
    outs = []
    for h in range(N_HEADS):
        eg = jnp.exp(g_all[:, SM_GA + h:SM_GA + h + 1])
        beta = beta_all[:, SM_GB + h:SM_GB + h + 1]
        qh = _l2norm(_head(q, h)) * SCALE
        kh = _l2norm(_head(k, h))
        vh = _head(v, h)
        st = gs_ref[0, h]
        delta = vh * beta - _vec_mat(_row_to_col(kh * (beta * eg)), st)
        o = jnp.sum(qh * kh, axis=1, keepdims=True) * delta + _vec_mat(_row_to_col(qh * eg), st)
        gs_o[0, h] = eg * st + _row_to_col(kh) * delta
        outs.append(_head_rms(o))
    yg_ref[0] = jnp.concatenate(outs, axis=1) * nwg_ref[...] * _silu(zg[:, 768:1024])


def _mixers_step(z_ret, z_ml, z_gdn, small, cos, sin, mlb, dtb, alog, conv_w, nwr, nwm, nwg,
                 s_ret, ml_c, ml_n, ml_m, gdn_s, gdn_conv):
    b = z_ret.shape[0]
    row = lambda n: pl.BlockSpec((1, 1, n), lambda i: (i, 0, 0))
    vec = lambda n: pl.BlockSpec((1, n), lambda i: (0, 0))
    mat = pl.BlockSpec((1, N_HEADS, HEAD_DIM, HEAD_DIM), lambda i: (i, 0, 0, 0))
    nsp = pl.BlockSpec((1, N_HEADS, HEAD_DIM), lambda i: (i, 0, 0))
    csp = pl.BlockSpec((1, GDN_CONV - 1, 3 * D_HEADS), lambda i: (i, 0, 0))
    mshape = jax.ShapeDtypeStruct((b, N_HEADS, HEAD_DIM, HEAD_DIM), F32)
    yshape = jax.ShapeDtypeStruct((b, 1, D_HEADS), F32)
    return pl.pallas_call(
        _step_body,
        grid=(b,),
        in_specs=[row(1024), row(1024), row(1024), row(LANES), vec(LANES), vec(LANES), vec(LANES), vec(LANES), vec(LANES),
                  pl.BlockSpec((GDN_CONV, 3 * D_HEADS), lambda i: (0, 0)), vec(D_HEADS), vec(D_HEADS), vec(D_HEADS),
                  mat, mat, nsp, row(LANES), mat, csp],
        out_specs=[row(D_HEADS), row(D_HEADS), row(D_HEADS), mat, mat, nsp, row(LANES), mat, csp],
        out_shape=[yshape, yshape, yshape, mshape, mshape,
                   jax.ShapeDtypeStruct((b, N_HEADS, HEAD_DIM), F32), jax.ShapeDtypeStruct((b, 1, LANES), F32),
                   mshape, jax.ShapeDtypeStruct((b, GDN_CONV - 1, 3 * D_HEADS), F32)],
        compiler_params=_cparams("parallel"),
        name="mixers_step",
    )(z_ret, z_ml, z_gdn, small, cos, sin, mlb, dtb, alog, conv_w,
      nwr.reshape(1, -1), nwm.reshape(1, -1), nwg.reshape(1, -1), s_ret, ml_c, ml_n, ml_m, gdn_s, gdn_conv)


def _t5_bucket_np(n):
    exact = T5_BUCKETS // 2
    n = np.maximum(np.asarray(n, np.int64), 0)
    x = np.maximum(n, 1).astype(np.float32) / np.float32(exact)
    large = exact + (np.log(x) / np.float32(math.log(T5_MAX_DIST / exact)) * np.float32(T5_BUCKETS - exact)).astype(np.int32)
    return np.where(n < exact, n, np.minimum(large, T5_BUCKETS - 1)).astype(np.int32)


def _bucket_thresholds():
    b = _t5_bucket_np(np.arange(4 * T5_MAX_DIST))
    return [int(np.argmax(b >= k)) for k in range(T5_BUCKETS // 2 + 1, T5_BUCKETS)]


def _bias_table_body(rbt_ref, oh_ref, o_ref):
    o_ref[...] = _dot(rbt_ref[...], oh_ref[...].astype(F32), precision=HI)


def _bias_tables(rel_bias, dists):
    n = len(dists)
    npad = -(-n // 512) * 512
    onehot = np.zeros((T5_BUCKETS, npad), np.float32)
    onehot[_t5_bucket_np(dists), np.arange(n)] = 1.0
    rbt = jnp.pad(rel_bias.T, ((0, SUBLANES - N_HEADS), (0, 0)))
    tn = 512
    out = pl.pallas_call(
        _bias_table_body,
        grid=(npad // tn,),
        in_specs=[pl.BlockSpec((SUBLANES, T5_BUCKETS), lambda i: (0, 0)),
                  pl.BlockSpec((T5_BUCKETS, tn), lambda i: (0, i))],
        out_specs=pl.BlockSpec((SUBLANES, tn), lambda i: (0, i)),
        out_shape=jax.ShapeDtypeStruct((SUBLANES, npad), F32),
        name="bias_tables",
    )(rbt, jnp.asarray(onehot, BF16))
    return out[:, :n]


def _prep_cmp_w(cmp_w, cmp_pe):
    eye2 = jnp.eye(2, dtype=F32)

    def half(w):
        full = jnp.einsum('cjde,cf->jcdfe', w, eye2)
        return full.reshape(CMP_STRIDE * 2 * HEAD_DIM, 2 * HEAD_DIM)

    wab = jnp.concatenate([half(cmp_w[:, :CMP_STRIDE]), half(cmp_w[:, CMP_STRIDE:])], axis=1)
    pe = jnp.concatenate([cmp_pe[:CMP_STRIDE].reshape(1, -1), cmp_pe[CMP_STRIDE:].reshape(1, -1)], axis=1)
    return wab, jnp.pad(pe, ((0, SUBLANES - 1), (0, 0)))


def _pe_term(pe_ref, wab_ref):
    kin = CMP_STRIDE * 2 * HEAD_DIM
    t = (_dot(pe_ref[:, 0:kin], wab_ref[:, 0:LANES].astype(F32), precision=HI)
         + _dot(pe_ref[:, kin:2 * kin], wab_ref[:, LANES:2 * LANES].astype(F32), precision=HI))
    return t[0:1, :]


def _combine_cmp(parts, pe_term):
    rows = parts.shape[0]
    nxt = pltpu.roll(parts[:, LANES:2 * LANES], rows - 1, 0)
    r = lax.broadcasted_iota(jnp.int32, (rows, LANES), 0)
    return jnp.where(r < rows - 1, parts[:, 0:LANES] + nxt + pe_term, 0.0)


def _compress_seq_body(r_ref, wab_ref, pe_ref, o_ref):
    parts = _dot(r_ref[0], wab_ref[...], precision=HI)
    o_ref[0] = _combine_cmp(parts, _pe_term(pe_ref, wab_ref))


def _compress_seq(rows, wab, pe):
    b, n16, kin = rows.shape
    return pl.pallas_call(
        _compress_seq_body,
        grid=(b,),
        in_specs=[pl.BlockSpec((1, n16, kin), lambda i: (i, 0, 0)),
                  pl.BlockSpec(wab.shape, lambda i: (0, 0)),
                  pl.BlockSpec(pe.shape, lambda i: (0, 0))],
        out_specs=pl.BlockSpec((1, n16, LANES), lambda i: (i, 0, 0)),
        out_shape=jax.ShapeDtypeStruct((b, n16, LANES), F32),
        compiler_params=_cparams("parallel"),
        name="nsa_compress_seq",
    )(rows, wab, pe)


def _bias_from_dist(dist, rb_ref, h):
    n = jnp.maximum(dist, 0)
    exact = T5_BUCKETS // 2
    out = jnp.full(dist.shape, rb_ref[exact, h], F32)
    for k, thr in enumerate(_bucket_thresholds()):
        out = jnp.where(n >= thr, rb_ref[exact + 1 + k, h], out)
    for j in range(exact):
        out = jnp.where(n == j, rb_ref[j, h], out)
    return out


def _softmax_rows(s, valid):
    s = jnp.where(valid, s, NEG_INF)
    m = jnp.max(s, axis=1, keepdims=True)
    m = jnp.where(m == NEG_INF, 0.0, m)
    p = jnp.exp(s - m)
    den = jnp.sum(p, axis=1, keepdims=True)
    return p / jnp.where(den > 0, den, 1.0)


def _top_blocks(score, k):
    lane = lax.broadcasted_iota(jnp.int32, score.shape, 1)
    width = score.shape[1]
    sel = jnp.zeros(score.shape, jnp.bool_)
    for _ in range(k):
        m = jnp.max(score, axis=1, keepdims=True)
        idx = jnp.min(jnp.where(score == m, lane, width), axis=1, keepdims=True)
        pick = (lane == idx) & (m > NEG_INF)
        sel = sel | pick
        score = jnp.where(pick, NEG_INF, score)
    return sel


def _tile_mask(ok):
    return jnp.concatenate([jnp.where(ok, 1.0, 0.0)] * N_HEADS, axis=0) > 0.5


def _stack_heads(x):
    return jnp.concatenate([_head(x, h) for h in range(N_HEADS)], axis=0)


def _toeplitz(tab_ref, h, r0, nrow, rows):
    strip = tab_ref[h, pl.ds(r0, nrow), :]
    flat = jnp.concatenate([strip[r:r + 1, :] for r in range(nrow)], axis=1)
    rolled = pltpu.roll(jnp.broadcast_to(flat, (rows, nrow * LANES)), 0, 1, stride=1, stride_axis=0)
    return rolled[:, LANES:]


SLC_KC = 512


def _nsa_seq_body(rb_ref, q_ref, sm_ref, nkv_ref, kvc_ref, tab_ref, ovl_ref, exp_ref, y_ref, msk_ref, *, seq, p0):
    qb = Q_BLOCK
    n16 = seq // CMP_STRIDE
    bi = pl.program_id(1)
    q0 = bi * qb
    qs = _stack_heads(q_ref[0]) * SCALE
    qsb = _bf(qs)
    qpos = q0 + lax.broadcasted_iota(jnp.int32, (qb, 1), 0)

    kvc = kvc_ref[0]
    s_c = _dot_nt(qs, kvc[:, 0:HEAD_DIM], precision=HI)
    cmp_end = lax.broadcasted_iota(jnp.int32, (qb, n16), 1) * CMP_STRIDE + (CMP_LEN - 1)
    dist_c = qpos - cmp_end
    bias_c = jnp.concatenate([_bias_from_dist(dist_c, rb_ref, h) for h in range(N_HEADS)], axis=0)
    valid_c = _tile_mask(dist_c >= 0)
    p_c = _softmax_rows(s_c + bias_c, valid_c)
    o_c = _dot(_bf(p_c), _bf(kvc[:, HEAD_DIM:2 * HEAD_DIM]))
    p_sum = p_c[0:qb] + p_c[qb:2 * qb] + p_c[2 * qb:3 * qb] + p_c[3 * qb:4 * qb]
    imp = _dot(p_sum, ovl_ref[...], precision=HI)

    lane = lax.broadcasted_iota(jnp.int32, (qb, LANES), 1)
    cur = qpos // SLC_BLOCK
    forced = (lane == 0) | (lane == cur) | (lane == cur - 1)
    score = jnp.where(forced, jnp.inf, imp)
    score = jnp.where(lane > cur, NEG_INF, score)
    selm = _bf(_top_blocks(score, SLC_TOPK).astype(F32))
    for c in range(seq // SLC_KC):
        msk_ref[c] = _dot(selm, exp_ref[:, c * SLC_KC:(c + 1) * SLC_KC])

    nrow_s = SLC_KC // LANES + 1
    icol = lax.broadcasted_iota(jnp.int32, (qb, SLC_KC), 0)
    jrow = lax.broadcasted_iota(jnp.int32, (qb, SLC_KC), 1)

    def chunk_step(c, carry):
        m, l, acc = carry
        k0 = pl.multiple_of(c * SLC_KC, SLC_KC)
        ks = _bf(nkv_ref[0, pl.ds(k0, SLC_KC), 2 * HEAD_DIM:3 * HEAD_DIM])
        vs = _bf(nkv_ref[0, pl.ds(k0, SLC_KC), 3 * HEAD_DIM:4 * HEAD_DIM])
        base = q0 - k0
        r0 = (p0 - base) // LANES - 1
        bias = jnp.concatenate([_toeplitz(tab_ref, h, r0, nrow_s, qb) for h in range(N_HEADS)], axis=0)
        ok = (msk_ref[c] > 0.5) & (base + icol - jrow >= 0)
        ok = _tile_mask(ok)
        s = jnp.where(ok, _dot_nt(qsb, ks) + bias, NEG_INF)
        m_new = jnp.maximum(m, jnp.max(s, axis=1, keepdims=True))
        m_safe = jnp.where(m_new == NEG_INF, 0.0, m_new)
        alpha = jnp.exp(m - m_safe)
        p = jnp.exp(s - m_safe)
        l = alpha * l + jnp.sum(p, axis=1, keepdims=True)
        acc = alpha * acc + _dot(_bf(p), vs)
        return m_new, l, acc

    init = (jnp.full((N_HEADS * qb, 1), NEG_INF, F32), jnp.zeros((N_HEADS * qb, 1), F32),
            jnp.zeros((N_HEADS * qb, HEAD_DIM), F32))
    _, l_s, acc_s = lax.fori_loop(0, q0 // SLC_KC + 1, chunk_step, init)
    o_s = acc_s / jnp.where(l_s > 0, l_s, 1.0)

    band = WINDOW + qb
    sw = pl.multiple_of(jnp.maximum(q0 - WINDOW, 0), qb)
    kw = _bf(nkv_ref[0, pl.ds(sw, band), 4 * HEAD_DIM:5 * HEAD_DIM])
    vw = _bf(nkv_ref[0, pl.ds(sw, band), 5 * HEAD_DIM:6 * HEAD_DIM])
    base_w = q0 - sw
    r0w = (p0 - base_w) // LANES - 1
    nrow_w = band // LANES + 1
    bias_w = jnp.concatenate([_toeplitz(tab_ref, h, r0w, nrow_w, qb) for h in range(N_HEADS)], axis=0)
    dist_w = (base_w + lax.broadcasted_iota(jnp.int32, (qb, band), 0)
              - lax.broadcasted_iota(jnp.int32, (qb, band), 1))
    ok_w = _tile_mask((dist_w >= 0) & (dist_w < WINDOW))
    p_w = _softmax_rows(_dot_nt(qsb, kw) + bias_w, ok_w)
    o_w = _dot(_bf(p_w), vw)

    gates = _sigmoid(sm_ref[0])
    outs = []
    for h in range(N_HEADS):
        rs = slice(h * qb, (h + 1) * qb)
        g = lambda t: gates[:, SM_NG + 3 * h + t:SM_NG + 3 * h + t + 1]
        outs.append(g(0) * o_c[rs] + g(1) * o_s[rs] + g(2) * o_w[rs])
    y_ref[0] = jnp.concatenate(outs, axis=1)


def _nsa_tab_seq_dists(seq):
    p0 = seq + Q_BLOCK
    length = p0 + WINDOW + Q_BLOCK + LANES
    return p0, np.maximum(p0 - np.arange(length), 0)


def _nsa_seq(rel_bias, nq, small, nkv, kvc, tab, seq_p0):
    b, l, _ = nq.shape
    n16 = l // CMP_STRIDE
    n_slc = l // SLC_BLOCK
    cmp_start = np.arange(n16)[:, None] * CMP_STRIDE
    slc_start = np.arange(LANES)[None, :] * SLC_BLOCK
    ovl = ((cmp_start < slc_start + SLC_BLOCK) & (cmp_start + CMP_LEN > slc_start)
           & (np.arange(n16)[:, None] < n16 - 1) & (np.arange(LANES)[None, :] < n_slc)).astype(np.float32)
    expand = (np.arange(LANES)[:, None] == (np.arange(l)[None, :] // SLC_BLOCK)).astype(np.float32)
    nrows = tab.shape[1]
    return pl.pallas_call(
        functools.partial(_nsa_seq_body, seq=l, p0=seq_p0),
        grid=(b, l // Q_BLOCK),
        in_specs=[pl.BlockSpec(memory_space=pltpu.SMEM),
                  pl.BlockSpec((1, Q_BLOCK, D_HEADS), lambda i, j: (i, j, 0)),
                  pl.BlockSpec((1, Q_BLOCK, LANES), lambda i, j: (i, j, 0)),
                  pl.BlockSpec((1, l, 6 * HEAD_DIM), lambda i, j: (i, 0, 0)),
                  pl.BlockSpec((1, n16, LANES), lambda i, j: (i, 0, 0)),
                  pl.BlockSpec((SUBLANES, nrows, LANES), lambda i, j: (0, 0, 0)),
                  pl.BlockSpec((n16, LANES), lambda i, j: (0, 0)),
                  pl.BlockSpec((LANES, l), lambda i, j: (0, 0))],
        out_specs=pl.BlockSpec((1, Q_BLOCK, D_HEADS), lambda i, j: (i, j, 0)),
        out_shape=jax.ShapeDtypeStruct((b, l, D_HEADS), F32),
        scratch_shapes=[pltpu.VMEM((l // SLC_KC, Q_BLOCK, SLC_KC), F32)],
        compiler_params=_cparams("parallel", "arbitrary"),
        name="nsa_seq",
    )(rel_bias, nq, small, nkv, kvc, tab, jnp.asarray(ovl), jnp.asarray(expand, BF16))


PAGES_PER_STEP = 16


def _pad_rows(x, rows=SUBLANES):
    return jnp.concatenate([x, jnp.zeros((rows - x.shape[0], x.shape[1]), x.dtype)], axis=0)


def _nsa_select_body(pt_ref, *refs, past, n_steps):
    del pt_ref
    pages = refs[:PAGES_PER_STEP]
    wab_ref, pe_ref, q_ref, tabc_ref, ovl_ref, oc_ref, top_ref, parts_ref, rows_ref = refs[PAGES_PER_STEP:]
    step = pl.program_id(1)
    blocks_per_page = PAGE_SIZE // CMP_STRIDE
    rows_per_step = PAGES_PER_STEP * blocks_per_page
    for p, pg in enumerate(pages):
        for c in range(2):
            rows_ref[p * PAGE_SIZE:(p + 1) * PAGE_SIZE, c * HEAD_DIM:(c + 1) * HEAD_DIM] = pg[c].T
    acc = jnp.zeros((rows_per_step, 2 * LANES), F32)
    for j in range(CMP_STRIDE):
        xj = rows_ref[pl.ds(j, rows_per_step, stride=CMP_STRIDE), :]
        acc = acc + _dot(_bf(xj), wab_ref[j * LANES:(j + 1) * LANES, :])
    parts_ref[pl.ds(pl.multiple_of(step * rows_per_step, rows_per_step), rows_per_step), :] = acc

    @pl.when(step == n_steps - 1)
    def _():
        n16 = past // CMP_STRIDE
        n_cmp = n16 - 1
        cur = past // SLC_BLOCK
        pe_term = _pe_term(pe_ref, wab_ref)
        kvc = _combine_cmp(parts_ref[...], pe_term)
        qs = _pad_rows(_stack_heads(q_ref[0]) * SCALE)
        s_c = _dot_nt(qs, kvc[:, 0:HEAD_DIM], precision=HI) + tabc_ref[...]
        lane = lax.broadcasted_iota(jnp.int32, (SUBLANES, n16), 1)
        p_c = _softmax_rows(s_c, lane < n_cmp)
        o_c = _dot(_bf(p_c), _bf(kvc[:, HEAD_DIM:2 * HEAD_DIM]))
        oc_ref[0] = jnp.concatenate([o_c[h:h + 1, :] for h in range(N_HEADS)], axis=1)
        p_sum = p_c[0:1] + p_c[1:2] + p_c[2:3] + p_c[3:4]
        imp = _dot(_pad_rows(p_sum), ovl_ref[...], precision=HI)[0:1, :]
        width = imp.shape[1]
        lane = lax.broadcasted_iota(jnp.int32, (1, width), 1)
        forced = (lane == 0) | (lane == cur) | (lane == cur - 1)
        score = jnp.where(forced, jnp.inf, imp)
        score = jnp.where(lane > cur, NEG_INF, score)
        out_lane = lax.broadcasted_iota(jnp.int32, (1, LANES), 1)
        top = jnp.zeros((1, LANES), jnp.int32)
        for r in range(SLC_TOPK):
            m = jnp.max(score, axis=1, keepdims=True)
            idx = jnp.min(jnp.where(score == m, lane, width), axis=1, keepdims=True)
            top = jnp.where(out_lane == r, idx, top)
            score = jnp.where(lane == idx, NEG_INF, score)
        top_ref[0] = top


def _nsa_select(layer, page_table, cache_t, wab_bf, pe, nq, tabc, past):
    b, n_pages = page_table.shape
    n_steps = n_pages // PAGES_PER_STEP
    n16 = past // CMP_STRIDE
    n_slc = past // SLC_BLOCK + 1
    width = -(-n_slc // LANES) * LANES
    cmp_start = np.arange(n16)[:, None] * CMP_STRIDE
    slc_start = np.arange(width)[None, :] * SLC_BLOCK
    ovl = ((cmp_start < slc_start + SLC_BLOCK) & (cmp_start + CMP_LEN > slc_start)
           & (np.arange(n16)[:, None] < n16 - 1) & (np.arange(width)[None, :] < n_slc)).astype(np.float32)

    n_pool = cache_t.shape[1]

    def page_spec(p):
        def imap(i, s, pt):
            pg = pt[jnp.minimum(i, b - 1), jnp.minimum(s, n_steps - 1) * PAGES_PER_STEP + p]
            return (layer, jnp.clip(pg, 0, n_pool - 1), 0, 0, 0)
        return pl.BlockSpec((None, None, 2, HEAD_DIM, PAGE_SIZE), imap)

    const = lambda shape: pl.BlockSpec(shape, lambda i, s, pt: (0,) * len(shape))
    grid_spec = pltpu.PrefetchScalarGridSpec(
        num_scalar_prefetch=1,
        grid=(b, n_steps),
        in_specs=[page_spec(p) for p in range(PAGES_PER_STEP)]
        + [const(wab_bf.shape), const(pe.shape), pl.BlockSpec((1, 1, D_HEADS), lambda i, s, pt: (i, 0, 0)),
           const(tabc.shape), const(ovl.shape)],
        out_specs=[pl.BlockSpec((1, 1, D_HEADS), lambda i, s, pt: (i, 0, 0)),
                   pl.BlockSpec((1, 1, LANES), lambda i, s, pt: (i, 0, 0))],
        scratch_shapes=[pltpu.VMEM((n16, 2 * LANES), F32),
                        pltpu.VMEM((PAGES_PER_STEP * PAGE_SIZE, 2 * HEAD_DIM), F32)],
    )
    return pl.pallas_call(
        functools.partial(_nsa_select_body, past=past, n_steps=n_steps),
        grid_spec=grid_spec,
        out_shape=[jax.ShapeDtypeStruct((b, 1, D_HEADS), F32), jax.ShapeDtypeStruct((b, 1, LANES), jnp.int32)],
        compiler_params=_cparams("parallel", "arbitrary"),
        name="nsa_select",
    )(page_table, *([cache_t] * PAGES_PER_STEP), wab_bf, pe, nq, tabc, jnp.asarray(ovl))


def _nsa_attend_body(pt_ref, top_ref, *refs, past):
    del pt_ref
    pages = refs[:SLC_TOPK]
    (q_ref, sm_ref, new_ref, oc_ref, swa_ref, tabs_ref, tabw_ref, y_ref, swa_out) = refs[SLC_TOPK:]
    b = pl.program_id(0)
    new_blk = past // SLC_BLOCK
    blocks_per_page = PAGE_SIZE // SLC_BLOCK
    qs = _bf(_pad_rows(_stack_heads(q_ref[0]) * SCALE))
    new = new_ref[0]
    lane_t = lax.broadcasted_iota(jnp.int32, (HEAD_DIM, PAGE_SIZE), 1)
    lane = lax.broadcasted_iota(jnp.int32, (SUBLANES, PAGE_SIZE), 1)
    new_kt = jnp.where(lane_t == 0, _row_to_col(new[:, 2 * HEAD_DIM:3 * HEAD_DIM]), 0.0)
    new_vt = jnp.where(lane_t == 0, _row_to_col(new[:, 3 * HEAD_DIM:4 * HEAD_DIM]), 0.0)
    scores, oks, vts = [], [], []
    for s in range(SLC_TOPK):
        ti = jnp.clip(top_ref[b, s], 0, new_blk)
        is_new = ti == new_blk
        kt = jnp.where(is_new, new_kt, pages[s][0])
        vts.append(jnp.where(is_new, new_vt, pages[s][1]))
        scores.append(_dot(qs, _bf(kt)) + tabs_ref[ti])
        kpos = ti * SLC_BLOCK + lane % SLC_BLOCK
        ok = (lane // SLC_BLOCK == ti % blocks_per_page) & (kpos <= past)
        oks.append(jnp.where(ok, 1.0, 0.0))
    p_s = _softmax_rows(jnp.concatenate(scores, axis=1), jnp.concatenate(oks, axis=1) > 0.5)
    o_s = None
    for s in range(SLC_TOPK):
        t = _dot_nt(_bf(p_s[:, s * PAGE_SIZE:(s + 1) * PAGE_SIZE]), _bf(vts[s]))
        o_s = t if o_s is None else o_s + t

    wlane = lax.broadcasted_iota(jnp.int32, (HEAD_DIM, WINDOW), 1)
    win = []
    for c in range(2):
        col = _row_to_col(new[:, (4 + c) * HEAD_DIM:(5 + c) * HEAD_DIM])
        win.append(jnp.where(wlane == WINDOW - 1, col, pltpu.roll(swa_ref[c], WINDOW - 1, 1)))
        swa_out[c] = win[c]
    s_w = _dot(qs, _bf(win[0])) + tabw_ref[...]
    p_w = _softmax_rows(s_w, jnp.full(s_w.shape, True))
    o_w = _dot_nt(_bf(p_w), _bf(win[1]))

    gates = _sigmoid(sm_ref[0])
    o_c = oc_ref[0]
    outs = []
    for h in range(N_HEADS):
        g = lambda t: gates[:, SM_NG + 3 * h + t:SM_NG + 3 * h + t + 1]
        outs.append(g(0) * _head(o_c, h) + g(1) * o_s[h:h + 1, :] + g(2) * o_w[h:h + 1, :])
    y_ref[0] = jnp.concatenate(outs, axis=1)


def _nsa_attend(layer, page_table, top, cache_t, nq, small, nkv, o_c, swa_t, tabs, tabw, past):
    b = nq.shape[0]
    last_page = past // PAGE_SIZE - 1
    blocks_per_page = PAGE_SIZE // SLC_BLOCK

    n_pool = cache_t.shape[1]

    def page_spec(s):
        def imap(i, pt, tp):
            ii = jnp.minimum(i, b - 1)
            pg = jnp.clip(tp[ii, s] // blocks_per_page, 0, last_page)
            return (layer, jnp.clip(pt[ii, pg], 0, n_pool - 1), 1, 0, 0)
        return pl.BlockSpec((None, None, 2, HEAD_DIM, PAGE_SIZE), imap)

    row = lambda n: pl.BlockSpec((1, 1, n), lambda i, pt, tp: (i, 0, 0))
    const = lambda shape: pl.BlockSpec(shape, lambda i, pt, tp: (0,) * len(shape))
    grid_spec = pltpu.PrefetchScalarGridSpec(
        num_scalar_prefetch=2,
        grid=(b,),
        in_specs=[page_spec(s) for s in range(SLC_TOPK)]
        + [row(D_HEADS), row(LANES), row(6 * HEAD_DIM), row(D_HEADS),
           pl.BlockSpec((None, None, 2, HEAD_DIM, WINDOW), lambda i, pt, tp: (layer, i, 0, 0, 0)),
           const(tabs.shape), const(tabw.shape)],
        out_specs=[row(D_HEADS), pl.BlockSpec((None, 2, HEAD_DIM, WINDOW), lambda i, pt, tp: (i, 0, 0, 0))],
    )
    return pl.pallas_call(
        functools.partial(_nsa_attend_body, past=past),
        grid_spec=grid_spec,
        out_shape=[jax.ShapeDtypeStruct((b, 1, D_HEADS), F32),
                   jax.ShapeDtypeStruct((b, 2, HEAD_DIM, WINDOW), F32)],
        compiler_params=_cparams("parallel"),
        name="nsa_attend",
    )(page_table, top, *([cache_t] * SLC_TOPK), nq, small, nkv, o_c, swa_t, tabs, tabw)


SEQ_CHUNK = 64
ROW_TILE = 256


def _lane_vec(pairs):
    v = jnp.zeros((1, LANES), F32)
    for off, vals in pairs:
        v = v.at[0, off:off + vals.shape[0]].set(vals.astype(F32))
    return v


def kernel(x_prompt, x_sample, cache_nsa_kv, cache_swa_kv, state_ret, state_mlstm_C, state_mlstm_n, state_mlstm_m, state_gdn, state_gdn_conv, state_ffn_conv, page_table, rel_bias, norm_pre_mix, w_in, mlstm_b_i, mlstm_b_f, gdn_conv_w, gdn_A_log, gdn_dt_bias, nsa_cmp_pe, nsa_cmp_w, ret_norm, mlstm_norm, gdn_norm, w_out, norm_post_mix, norm_pre_ffn, w_ffn_gate, w_ffn_up, ffn_conv_w, w_ffn_down, norm_post_ffn):
    depth = w_in.shape[0]
    bp, seq, _ = x_prompt.shape
    bs, dec_seq, _ = x_sample.shape
    assert dec_seq == 1 and seq % SLC_KC == 0 and seq >= WINDOW + Q_BLOCK
    n_pool = cache_nsa_kv.shape[1]
    past = page_table.shape[1] * PAGE_SIZE
    assert past >= WINDOW and past % SLC_BLOCK == 0 and page_table.shape[1] % PAGES_PER_STEP == 0

    p0, d_seq = _nsa_tab_seq_dists(seq)
    n16 = past // CMP_STRIDE
    n_slc = past // SLC_BLOCK + 1
    d_cmp = np.maximum(past - (np.arange(n16) * CMP_STRIDE + CMP_LEN - 1), 0)
    d_slc = np.maximum(past - np.arange(n_slc * SLC_BLOCK), 0)
    d_win = WINDOW - 1 - np.arange(WINDOW)
    tab = _bias_tables(rel_bias, np.concatenate([d_seq, d_cmp, d_slc, d_win]))
    o1, o2, o3 = len(d_seq), len(d_seq) + len(d_cmp), len(d_seq) + len(d_cmp) + len(d_slc)
    tab_seq = tab[:, :o1].reshape(SUBLANES, o1 // LANES, LANES)
    tabc = tab[:, o1:o2]
    tabs = jnp.transpose(tab[:, o2:o3].reshape(SUBLANES, n_slc, SLC_BLOCK), (1, 0, 2))
    tabs = jnp.concatenate([tabs, tabs], axis=2)
    tabw = tab[:, o3:]

    cache_t = jnp.transpose(cache_nsa_kv, (0, 1, 3, 4, 2))
    swa_t = jnp.transpose(cache_swa_kv, (0, 1, 3, 4, 2))
    cos_p, sin_p = _rope_tables(jnp.arange(seq))
    cos_s, sin_s = _rope_tables(jnp.arange(past, past + 1))

    xp = x_prompt
    xs = x_sample.reshape(bs, D_MODEL)
    p_states, s_states = [], []
    for l in range(depth):
        w_cat = _prep_w_in(w_in[l])
        w_out_bf = _bf(w_out[l])
        wg, wu, wd = _bf(w_ffn_gate[l]), _bf(w_ffn_up[l]), _bf(w_ffn_down[l])
        wab, pe = _prep_cmp_w(nsa_cmp_w[l], nsa_cmp_pe[l])
        ml_bias = _lane_vec([(SM_MI, mlstm_b_i[l]), (SM_MF, mlstm_b_f[l])])
        dtb = _lane_vec([(SM_GA, gdn_dt_bias[l])])
        alog = _lane_vec([(SM_GA, gdn_A_log[l])])

        z = _in_proj(xp.reshape(bp * seq, D_MODEL), norm_pre_mix[l], w_cat, ROW_TILE)
        z_ret, z_ml, z_gdn, z_nq, z_nkv, z_sm = [t.reshape(bp, seq, -1) for t in z]
        y_ret, st_ret = _retention_seq(z_ret, cos_p, sin_p, ret_norm[l], SEQ_CHUNK, bp)
        y_ml, st_c, st_n, st_m = _mlstm_seq(z_ml, z_sm, ml_bias, mlstm_norm[l], SEQ_CHUNK, bp)
        y_gdn, st_g = _gdn_seq(z_gdn, z_sm, gdn_conv_w[l], dtb, alog, gdn_norm[l], SEQ_CHUNK, bp)
        rows = z_nkv[..., 0:2 * HEAD_DIM].reshape(bp, seq // CMP_STRIDE, CMP_STRIDE * 2 * HEAD_DIM)
        kvc = _compress_seq(rows, wab, pe)
        y_nsa = _nsa_seq(rel_bias, z_nq, z_sm, z_nkv, kvc, tab_seq, p0)
        ys = [t.reshape(bp * seq, D_HEADS) for t in (y_ret, y_ml, y_gdn, y_nsa)]
        x1, h2 = _out_proj(xp.reshape(bp * seq, D_MODEL), ys, w_out_bf, norm_post_mix[l], norm_pre_ffn[l], ROW_TILE)
        xp, tail = _ffn_seq(x1.reshape(bp, seq, D_MODEL), h2.reshape(bp, seq, D_MODEL), wg, wu, wd,
                            ffn_conv_w[l], norm_post_ffn[l], ROW_TILE)
        p_states.append((
            z_nkv[..., 0:4 * HEAD_DIM].reshape(bp, seq, 4, HEAD_DIM),
            z_nkv[:, seq - WINDOW:, 4 * HEAD_DIM:].reshape(bp, WINDOW, 2, HEAD_DIM),
            st_ret, st_c, st_n, st_m[:, 0, :N_HEADS], st_g,
            z_gdn[:, seq - (GDN_CONV - 1):, 0:3 * D_HEADS],
            tail[:, SUBLANES - (FFN_CONV - 1):, :]))

        z = _in_proj(xs, norm_pre_mix[l], w_cat, bs)
        z_ret, z_ml, z_gdn, z_nq, z_nkv, z_sm = [t.reshape(bs, 1, -1) for t in z]
        m_pad = jnp.pad(state_mlstm_m[l], ((0, 0), (0, LANES - N_HEADS))).reshape(bs, 1, LANES)
        (y_ret, y_ml, y_gdn, st_ret, st_c, st_n, st_m, st_g, st_gc) = _mixers_step(
            z_ret, z_ml, z_gdn, z_sm, cos_s, sin_s, ml_bias, dtb, alog, gdn_conv_w[l],
            ret_norm[l], mlstm_norm[l], gdn_norm[l],
            state_ret[l], state_mlstm_C[l], state_mlstm_n[l], m_pad, state_gdn[l], state_gdn_conv[l])
        o_c, top = _nsa_select(l, page_table, cache_t, _bf(wab), pe, z_nq, tabc, past)
        y_nsa, swa_new = _nsa_attend(l, page_table, top[:, 0, :SLC_TOPK], cache_t, z_nq, z_sm, z_nkv, o_c,
                                     swa_t, tabs, tabw, past)
        ys = [t.reshape(bs, D_HEADS) for t in (y_ret, y_ml, y_gdn, y_nsa)]
        x1, h2 = _out_proj(xs, ys, w_out_bf, norm_post_mix[l], norm_pre_ffn[l], bs)
        xs, g_new = _ffn_step(x1, h2, state_ffn_conv[l][:, 0], state_ffn_conv[l][:, 1], wg, wu, wd,
                              ffn_conv_w[l], norm_post_ffn[l])
        s_states.append((
            z_nkv[..., 0:4 * HEAD_DIM].reshape(bs, 1, 4, HEAD_DIM),
            jnp.transpose(swa_new, (0, 3, 1, 2)),
            st_ret, st_c, st_n, st_m[:, 0, :N_HEADS], st_g, st_gc,
            jnp.concatenate([state_ffn_conv[l][:, 1:], g_new[:, None, :]], axis=1)))

    stack = lambda states: tuple(jnp.stack([s[k] for s in states]) for k in range(len(states[0])))
    return (xp, xs.reshape(bs, 1, D_MODEL)) + stack(p_states) + stack(s_states)
```

```python
import functools
import math

import jax
import jax.numpy as jnp
import numpy as np
from jax import lax
from jax.experimental import pallas as pl
from jax.experimental.pallas import tpu as pltpu

F32 = jnp.float32
BF16 = jnp.bfloat16
HI = lax.Precision.HIGHEST

D_MODEL = 1024
HEAD_DIM = 64
N_HEADS = 4
D_HEADS = N_HEADS * HEAD_DIM
D_FF = 2816
GDN_CONV = 4
FFN_CONV = 3
PAGE_SIZE = 128
CMP_STRIDE = 16
CMP_LEN = 32
SLC_BLOCK = 64
SLC_TOPK = 16
WINDOW = 512
Q_BLOCK = 128
T5_BUCKETS = 32
T5_MAX_DIST = 1024
ROPE_BASE = 10000.0
EPS = 1e-6
SCALE = HEAD_DIM ** -0.5
IN_WIDTHS = (256, 256, 256, 256, 256, 256, 256, 4, 4, 256, 768, 4, 4, 256, 256, 384, 12)
LANES = 128
SUBLANES = 8
VMEM_LIMIT = 56 * 1024 * 1024
NEG_INF = float("-inf")

SM_MI, SM_MF, SM_GA, SM_GB, SM_NG = 0, 4, 8, 12, 16


def _cparams(*sem):
    return pltpu.CompilerParams(dimension_semantics=sem, vmem_limit_bytes=VMEM_LIMIT)


def _rms(x, g):
    return x * lax.rsqrt(jnp.mean(x * x, axis=-1, keepdims=True) + EPS) * g


def _dot(a, b, **kw):
    return jnp.dot(a, b, preferred_element_type=F32, **kw)


def _dot_nt(a, b, **kw):
    return lax.dot_general(a, b, (((1,), (1,)), ((), ())), preferred_element_type=F32, **kw)


def _dot_tn(a, b, **kw):
    return lax.dot_general(a, b, (((0,), (0,)), ((), ())), preferred_element_type=F32, **kw)


def _bf(x):
    return x.astype(BF16)


def _sigmoid(x):
    return 1.0 / (1.0 + jnp.exp(-x))


def _silu(x):
    return x * _sigmoid(x)


def _softplus(x):
    return jnp.maximum(x, 0.0) + jnp.log(1.0 + jnp.exp(-jnp.abs(x)))


def _head_rms(o):
    return o * lax.rsqrt(jnp.mean(o * o, axis=-1, keepdims=True) + EPS)


PROJ_WIDTHS = (1024, 1024, 1024, 256, 384, 128)


def _prep_w_in(w):
    parts, off = [], 0
    for wd in IN_WIDTHS:
        parts.append(w[:, off:off + wd])
        off += wd
    (rq, rk, rv, rg, mq, mk, mv, mi, mf, mo, gqkv, ga, gbeta, gg, nq, nkv, ngate) = parts
    small = jnp.concatenate([mi, mf, ga, gbeta, ngate], axis=1)
    small = jnp.pad(small, ((0, 0), (0, LANES - small.shape[1])))
    cat = jnp.concatenate([rq, rk, rv, rg, mq, mk, mv, mo, gqkv, gg, nq, nkv, small], axis=1)
    return cat.astype(BF16)


def _in_proj_body(x_ref, g_ref, w_ref, *out_refs):
    h = _bf(_rms(x_ref[...], g_ref[...]))
    off = 0
    for ref in out_refs:
        n = ref.shape[-1]
        ref[...] = _dot(h, w_ref[:, off:off + n])
        off += n


def _in_proj(x2d, g, w_cat, tm):
    m = x2d.shape[0]
    ntot = sum(PROJ_WIDTHS)
    return pl.pallas_call(
        _in_proj_body,
        grid=(m // tm,),
        in_specs=[pl.BlockSpec((tm, D_MODEL), lambda i: (i, 0)),
                  pl.BlockSpec((1, D_MODEL), lambda i: (0, 0)),
                  pl.BlockSpec((D_MODEL, ntot), lambda i: (0, 0))],
        out_specs=[pl.BlockSpec((tm, n), lambda i: (i, 0)) for n in PROJ_WIDTHS],
        out_shape=[jax.ShapeDtypeStruct((m, n), F32) for n in PROJ_WIDTHS],
        compiler_params=_cparams("parallel"),
        name="in_proj",
    )(x2d, g.reshape(1, D_MODEL), w_cat)


def _out_proj_body(x_ref, y0, y1, y2, y3, w_ref, gpost_ref, gpre_ref, x1_ref, h2_ref):
    acc = None
    for k, y in enumerate((y0, y1, y2, y3)):
        t = _dot(_bf(y[...]), w_ref[k * D_HEADS:(k + 1) * D_HEADS, :])
        acc = t if acc is None else acc + t
    x1 = x_ref[...] + _rms(acc, gpost_ref[...])
    x1_ref[...] = x1
    h2_ref[...] = _bf(_rms(x1, gpre_ref[...]))


def _out_proj(x2d, ys, w_out_bf, gpost, gpre, tm):
    m = x2d.shape[0]
    row = lambda n: pl.BlockSpec((tm, n), lambda i: (i, 0))
    vec = pl.BlockSpec((1, D_MODEL), lambda i: (0, 0))
    return pl.pallas_call(
        _out_proj_body,
        grid=(m // tm,),
        in_specs=[row(D_MODEL)] + [row(D_HEADS)] * 4
        + [pl.BlockSpec((D_MODEL, D_MODEL), lambda i: (0, 0)), vec, vec],
        out_specs=[row(D_MODEL), row(D_MODEL)],
        out_shape=[jax.ShapeDtypeStruct((m, D_MODEL), F32), jax.ShapeDtypeStruct((m, D_MODEL), BF16)],
        compiler_params=_cparams("parallel"),
        name="out_proj",
    )(x2d, *ys, w_out_bf, gpost.reshape(1, -1), gpre.reshape(1, -1))


FFN_TN = 256


def _gelu_tanh(x):
    return 0.5 * x * (1.0 + jnp.tanh(0.7978845608028654 * (x + 0.044715 * x * x * x)))


def _ffn_seq_body(x1_ref, h2_ref, wg_ref, wu_ref, wd_ref, cw_ref, gpost_ref, x2_ref, tail_ref, carry_ref, *, tm):
    @pl.when(pl.program_id(1) == 0)
    def _():
        carry_ref[...] = jnp.zeros_like(carry_ref)

    h2 = h2_ref[0]
    rows = lax.broadcasted_iota(jnp.int32, (tm, FFN_TN), 0)
    acc = jnp.zeros((tm, D_MODEL), F32)
    for n0 in range(0, D_FF, FFN_TN):
        g = _dot(h2, wg_ref[:, n0:n0 + FFN_TN])
        prev = carry_ref[:, n0:n0 + FFN_TN]
        g1 = jnp.where(rows == 0, prev[7:8, :], pltpu.roll(g, 1, 0))
        g2 = pltpu.roll(g, 2, 0)
        g2 = jnp.where(rows == 0, prev[6:7, :], jnp.where(rows == 1, prev[7:8, :], g2))
        cw = cw_ref[:, n0:n0 + FFN_TN]
        gate = cw[0:1, :] * g2 + cw[1:2, :] * g1 + cw[2:3, :] * g
        u = _dot(h2, wu_ref[:, n0:n0 + FFN_TN])
        a = _gelu_tanh(gate) * u
        acc = acc + _dot(_bf(a), wd_ref[n0:n0 + FFN_TN, :])
        carry_ref[:, n0:n0 + FFN_TN] = g[tm - SUBLANES:tm, :]
    x2_ref[0] = x1_ref[0] + _rms(acc, gpost_ref[...])
    tail_ref[0] = carry_ref[...]


def _ffn_seq(x1, h2, wg, wu, wd, cw, gpost, tm):
    b, l, _ = x1.shape
    const = lambda shape: pl.BlockSpec(shape, lambda i, j: (0,) * len(shape))
    return pl.pallas_call(
        functools.partial(_ffn_seq_body, tm=tm),
        grid=(b, l // tm),
        in_specs=[pl.BlockSpec((1, tm, D_MODEL), lambda i, j: (i, j, 0)),
                  pl.BlockSpec((1, tm, D_MODEL), lambda i, j: (i, j, 0)),
                  const((D_MODEL, D_FF)), const((D_MODEL, D_FF)), const((D_FF, D_MODEL)),
                  const((FFN_CONV, D_FF)), const((1, D_MODEL))],
        out_specs=[pl.BlockSpec((1, tm, D_MODEL), lambda i, j: (i, j, 0)),
                   pl.BlockSpec((1, SUBLANES, D_FF), lambda i, j: (i, 0, 0))],
        out_shape=[jax.ShapeDtypeStruct((b, l, D_MODEL), F32),
                   jax.ShapeDtypeStruct((b, SUBLANES, D_FF), F32)],
        scratch_shapes=[pltpu.VMEM((SUBLANES, D_FF), F32)],
        compiler_params=_cparams("parallel", "arbitrary"),
        name="ffn_seq",
    )(x1, h2, wg, wu, wd, cw, gpost.reshape(1, -1))


def _ffn_step_body(x1_ref, h2_ref, b0_ref, b1_ref, wg_ref, wu_ref, wd_ref, cw_ref, gpost_ref, x2_ref, g_ref):
    h2 = h2_ref[...]
    acc = jnp.zeros(x1_ref.shape, F32)
    for n0 in range(0, D_FF, FFN_TN):
        sl = slice(n0, n0 + FFN_TN)
        g = _dot(h2, wg_ref[:, sl])
        cw = cw_ref[:, sl]
        gate = cw[0:1, :] * b0_ref[:, sl] + cw[1:2, :] * b1_ref[:, sl] + cw[2:3, :] * g
        u = _dot(h2, wu_ref[:, sl])
        acc = acc + _dot(_bf(_gelu_tanh(gate) * u), wd_ref[sl, :])
        g_ref[:, sl] = g
    x2_ref[...] = x1_ref[...] + _rms(acc, gpost_ref[...])


def _ffn_step(x1, h2, b0, b1, wg, wu, wd, cw, gpost):
    m = x1.shape[0]
    return pl.pallas_call(
        _ffn_step_body,
        out_shape=[jax.ShapeDtypeStruct((m, D_MODEL), F32), jax.ShapeDtypeStruct((m, D_FF), F32)],
        compiler_params=pltpu.CompilerParams(vmem_limit_bytes=VMEM_LIMIT),
        name="ffn_step",
    )(x1, h2, b0, b1, wg, wu, wd, cw, gpost.reshape(1, -1))


def _rope_tables(pos):
    half = HEAD_DIM // 2
    inv = ROPE_BASE ** (-jnp.linspace(0.0, 1.0, half, dtype=F32))
    ang = pos.astype(F32)[:, None] * inv[None, :]
    return jnp.tile(jnp.cos(ang), (1, LANES // half)), jnp.tile(jnp.sin(ang), (1, LANES // half))


def _rope128(x, cos, sin):
    lane = lax.broadcasted_iota(jnp.int32, x.shape, 1)
    first = (lane % HEAD_DIM) < (HEAD_DIM // 2)
    other = jnp.where(first, -pltpu.roll(x, LANES - HEAD_DIM // 2, 1), pltpu.roll(x, HEAD_DIM // 2, 1))
    return x * cos + other * sin


def _rope256(x, cos, sin):
    return jnp.concatenate([_rope128(x[:, :LANES], cos, sin), _rope128(x[:, LANES:], cos, sin)], axis=1)


def _ret_log_decay(h):
    return math.log(1.0 - 2.0 ** (-5.0 - h))


def _head(x, h):
    return x[:, h * HEAD_DIM:(h + 1) * HEAD_DIM]


def _ret_body(z_ref, cos_ref, sin_ref, nw_ref, y_ref, s_ref, *, chunk):
    @pl.when(pl.program_id(1) == 0)
    def _():
        s_ref[...] = jnp.zeros_like(s_ref)

    nb = z_ref.shape[0]
    cos, sin = cos_ref[...], sin_ref[...]
    t = lax.broadcasted_iota(jnp.int32, (chunk, chunk), 0)
    s = lax.broadcasted_iota(jnp.int32, (chunk, chunk), 1)
    causal = t >= s
    diff = jnp.where(causal, t - s, 0).astype(F32)
    tcol = lax.broadcasted_iota(jnp.int32, (chunk, 1), 0).astype(F32)
    lgs = [_ret_log_decay(h) for h in range(N_HEADS)]
    dmat = [jnp.where(causal, jnp.exp(lg * diff), 0.0) for lg in lgs]
    xi = [jnp.exp(lg * (tcol + 1.0)) for lg in lgs]
    zeta = [jnp.exp(lg * (chunk - 1.0 - tcol)) for lg in lgs]
    probs = [(bb, h) for bb in range(nb) for h in range(N_HEADS)]
    qkv = []
    for bb in range(nb):
        q = _rope256(z_ref[bb, :, 0:256], cos, sin)
        k = _rope256(z_ref[bb, :, 256:512], cos, sin) * SCALE
        qkv.append((q, k, z_ref[bb, :, 512:768]))
    qh = [_head(qkv[bb][0], h) for bb, h in probs]
    kh = [_head(qkv[bb][1], h) for bb, h in probs]
    vh = [_bf(_head(qkv[bb][2], h)) for bb, h in probs]
    st = [s_ref[bb, h] for bb, h in probs]
    sc = [_dot_nt(_bf(q), _bf(k)) for q, k in zip(qh, kh)]
    cross = [_dot(_bf(q * xi[h]), _bf(s0)) for q, s0, (_, h) in zip(qh, st, probs)]
    upd = [_dot_tn(_bf(k * zeta[h]), v) for k, v, (_, h) in zip(kh, vh, probs)]
    o = [_dot(_bf(x * dmat[h]), v) + c for x, v, c, (_, h) in zip(sc, vh, cross, probs)]
    for i, (bb, h) in enumerate(probs):
        s_ref[bb, h] = st[i] * math.exp(lgs[h] * chunk) + upd[i]
    for bb in range(nb):
        outs = [_head_rms(o[bb * N_HEADS + h]) for h in range(N_HEADS)]
        y_ref[bb] = jnp.concatenate(outs, axis=1) * nw_ref[...] * _silu(z_ref[bb, :, 768:1024])


def _retention_seq(z_ret, cos, sin, nw, chunk, nb):
    b, l, _ = z_ret.shape
    return pl.pallas_call(
        functools.partial(_ret_body, chunk=chunk),
        grid=(b // nb, l // chunk),
        in_specs=[pl.BlockSpec((nb, chunk, 1024), lambda i, j: (i, j, 0)),
                  pl.BlockSpec((chunk, LANES), lambda i, j: (j, 0)),
                  pl.BlockSpec((chunk, LANES), lambda i, j: (j, 0)),
                  pl.BlockSpec((1, D_HEADS), lambda i, j: (0, 0))],
        out_specs=[pl.BlockSpec((nb, chunk, D_HEADS), lambda i, j: (i, j, 0)),
                   pl.BlockSpec((nb, N_HEADS, HEAD_DIM, HEAD_DIM), lambda i, j: (i, 0, 0, 0))],
        out_shape=[jax.ShapeDtypeStruct((b, l, D_HEADS), F32),
                   jax.ShapeDtypeStruct((b, N_HEADS, HEAD_DIM, HEAD_DIM), F32)],
        compiler_params=_cparams("parallel", "arbitrary"),
        name="retention_seq",
    )(z_ret, cos, sin, nw.reshape(1, -1))


def _lanes_to_rows(x, n=2 * SUBLANES):
    sel = (lax.broadcasted_iota(jnp.int32, (n, x.shape[1]), 0)
           == lax.broadcasted_iota(jnp.int32, (n, x.shape[1]), 1)).astype(F32)
    return _dot_nt(sel, x, precision=HI)


def _tri_incl(n):
    t = lax.broadcasted_iota(jnp.int32, (n, n), 0)
    s = lax.broadcasted_iota(jnp.int32, (n, n), 1)
    return t >= s


def _ml_body(z_ref, sm_ref, bias_ref, nw_ref, y_ref, c_ref, n_ref, m_ref, *, chunk):
    @pl.when(pl.program_id(1) == 0)
    def _():
        c_ref[...] = jnp.zeros_like(c_ref)
        n_ref[...] = jnp.zeros_like(n_ref)
        m_ref[...] = jnp.zeros_like(m_ref)

    nb = z_ref.shape[0]
    incl = _tri_incl(chunk)
    tri = incl.astype(F32)
    probs = [(bb, h) for bb in range(nb) for h in range(N_HEADS)]
    gate = []
    for bb in range(nb):
        pre = sm_ref[bb] + bias_ref[...]
        logf = jnp.minimum(pre, 0.0) - jnp.log(1.0 + jnp.exp(-jnp.abs(pre)))
        bcum = _dot(tri, logf, precision=HI)
        gate.append((pre, bcum, _lanes_to_rows(pre), _lanes_to_rows(bcum), m_ref[bb]))
    qh = [_head(z_ref[bb, :, 0:256], h) for bb, h in probs]
    kh = [_head(z_ref[bb, :, 256:512], h) * SCALE for bb, h in probs]
    vh = [_bf(_head(z_ref[bb, :, 512:768], h)) for bb, h in probs]
    cm = [c_ref[bb, h] for bb, h in probs]
    nv = [n_ref[bb, h:h + 1, :] for bb, h in probs]
    qk_raw = [_dot_nt(_bf(q), _bf(k)) for q, k in zip(qh, kh)]
    q_c = [_dot(_bf(q), _bf(c)) for q, c in zip(qh, cm)]
    i_col, b_col, inter, dlog = [], [], [], []
    for bb, h in probs:
        pre, bcum, pre_t, bcum_t, m_all = gate[bb]
        i_col.append(pre[:, SM_MI + h:SM_MI + h + 1])
        b_col.append(bcum[:, SM_MF + h:SM_MF + h + 1])
        inter.append(b_col[-1] + m_all[:, h:h + 1])
        dlog.append(jnp.where(incl, b_col[-1] - bcum_t[SM_MF + h:SM_MF + h + 1, :]
                              + pre_t[SM_MI + h:SM_MI + h + 1, :], NEG_INF))
    dmax = [jnp.max(x, axis=1, keepdims=True) for x in dlog]
    q_n = [jnp.sum(q * n, axis=1, keepdims=True) for q, n in zip(qh, nv)]
    stab = []
    for i in range(len(probs)):
        m_t = jnp.maximum(inter[i], dmax[i])
        m_end = m_t[chunk - 1:chunk, :]
        stab.append(dict(m_t=m_t, w=jnp.exp(dlog[i] - m_t), g_in=jnp.exp(inter[i] - m_t), m_end=m_end,
                         w_end=jnp.exp(b_col[i][chunk - 1:chunk, :] - b_col[i] + i_col[i] - m_end),
                         g_end=jnp.exp(inter[i][chunk - 1:chunk, :] - m_end)))
    qk = [x * d["w"] for x, d in zip(qk_raw, stab)]
    num = [_dot(_bf(x), v) + d["g_in"] * c for x, v, d, c in zip(qk, vh, stab, q_c)]
    upd = [_dot_tn(_bf(k * d["w_end"]), v) for k, d, v in zip(kh, stab, vh)]
    qk_sum = [jnp.sum(x, axis=1, keepdims=True) for x in qk]
    k_sum = [jnp.sum(d["w_end"] * k, axis=0, keepdims=True) for d, k in zip(stab, kh)]
    hh = []
    for i, (bb, h) in enumerate(probs):
        d = stab[i]
        den = qk_sum[i] + d["g_in"] * q_n[i]
        hh.append(num[i] / jnp.maximum(jnp.abs(den), jnp.exp(-d["m_t"])))
        c_ref[bb, h] = d["g_end"] * cm[i] + upd[i]
        n_ref[bb, h:h + 1, :] = d["g_end"] * nv[i] + k_sum[i]
    lane = lax.broadcasted_iota(jnp.int32, (1, LANES), 1)
    for bb in range(nb):
        m_out = jnp.zeros((1, LANES), F32)
        outs = []
        for h in range(N_HEADS):
            i = bb * N_HEADS + h
            m_out = jnp.where(lane == h, stab[i]["m_end"], m_out)
            outs.append(_head_rms(_sigmoid(_head(z_ref[bb, :, 768:1024], h)) * hh[i]))
        m_ref[bb] = m_out
        y_ref[bb] = jnp.concatenate(outs, axis=1) * nw_ref[...]


def _mlstm_seq(z_ml, small, bias_vec, nw, chunk, nb):
    b, l, _ = z_ml.shape
    return pl.pallas_call(
        functools.partial(_ml_body, chunk=chunk),
        grid=(b // nb, l // chunk),
        in_specs=[pl.BlockSpec((nb, chunk, 1024), lambda i, j: (i, j, 0)),
                  pl.BlockSpec((nb, chunk, LANES), lambda i, j: (i, j, 0)),
                  pl.BlockSpec((1, LANES), lambda i, j: (0, 0)),
                  pl.BlockSpec((1, D_HEADS), lambda i, j: (0, 0))],
        out_specs=[pl.BlockSpec((nb, chunk, D_HEADS), lambda i, j: (i, j, 0)),
                   pl.BlockSpec((nb, N_HEADS, HEAD_DIM, HEAD_DIM), lambda i, j: (i, 0, 0, 0)),
                   pl.BlockSpec((nb, N_HEADS, HEAD_DIM), lambda i, j: (i, 0, 0)),
                   pl.BlockSpec((nb, 1, LANES), lambda i, j: (i, 0, 0))],
        out_shape=[jax.ShapeDtypeStruct((b, l, D_HEADS), F32),
                   jax.ShapeDtypeStruct((b, N_HEADS, HEAD_DIM, HEAD_DIM), F32),
                   jax.ShapeDtypeStruct((b, N_HEADS, HEAD_DIM), F32),
                   jax.ShapeDtypeStruct((b, 1, LANES), F32)],
        compiler_params=_cparams("parallel", "arbitrary"),
        name="mlstm_seq",
    )(z_ml, small, bias_vec, nw.reshape(1, -1))


def _unit_lower_inverse(a, n):
    t = lax.broadcasted_iota(jnp.int32, (n, n), 0)
    s = lax.broadcasted_iota(jnp.int32, (n, n), 1)
    eye = (t == s).astype(F32)
    p = eye - a
    pw = a
    k = 2
    while k < n:
        pw = _dot(pw, pw, precision=HI)
        p = p + _dot(p, pw, precision=HI)
        k *= 2
    return p


def _l2norm(x):
    return x * lax.rsqrt(jnp.sum(x * x, axis=-1, keepdims=True) + EPS)


def _gdn_body(z_ref, sm_ref, cw_ref, dtb_ref, alog_ref, nw_ref, y_ref, s_ref, buf_ref, *, chunk):
    @pl.when(pl.program_id(1) == 0)
    def _():
        s_ref[...] = jnp.zeros_like(s_ref)
        buf_ref[:, 0:SUBLANES, :] = jnp.zeros((buf_ref.shape[0], SUBLANES, 3 * D_HEADS), F32)

    nb = z_ref.shape[0]
    incl = _tri_incl(chunk)
    strict = lax.broadcasted_iota(jnp.int32, (chunk, chunk), 0) > lax.broadcasted_iota(jnp.int32, (chunk, chunk), 1)
    eye = (lax.broadcasted_iota(jnp.int32, (chunk, chunk), 0)
           == lax.broadcasted_iota(jnp.int32, (chunk, chunk), 1)).astype(F32)
    tri = incl.astype(F32)
    gates, acts = [], []
    for bb in range(nb):
        buf_ref[bb, SUBLANES:SUBLANES + chunk, :] = z_ref[bb, :, 0:768]
        conv = None
        for j in range(GDN_CONV):
            term = buf_ref[bb, pl.ds(SUBLANES - (GDN_CONV - 1) + j, chunk), :] * cw_ref[j:j + 1, :]
            conv = term if conv is None else conv + term
        buf_ref[bb, 0:SUBLANES, :] = buf_ref[bb, chunk:chunk + SUBLANES, :]
        acts.append(_silu(conv))
        sm = sm_ref[bb]
        g_all = -jnp.exp(alog_ref[...]) * _softplus(sm + dtb_ref[...])
        gcum = _dot(tri, g_all, precision=HI)
        gates.append((gcum, _lanes_to_rows(gcum), _sigmoid(sm)))

    probs = [(bb, h) for bb in range(nb) for h in range(N_HEADS)]
    pre = []
    for bb, h in probs:
        gcum, gcum_t, beta_all = gates[bb]
        g_col = gcum[:, SM_GA + h:SM_GA + h + 1]
        g_row = gcum_t[SM_GA + h:SM_GA + h + 1, :]
        beta = beta_all[:, SM_GB + h:SM_GB + h + 1]
        decay = jnp.where(incl, jnp.exp(jnp.where(incl, g_col - g_row, 0.0)), 0.0)
        act = acts[bb]
        qh = _l2norm(_head(act[:, 0:256], h)) * SCALE
        kh = _l2norm(_head(act[:, 256:512], h))
        vh = _head(act[:, 512:768], h)
        eg = jnp.exp(g_col)
        pre.append(dict(g_col=g_col, beta=beta, decay=decay, qh=qh, kh=kh, kb=_bf(kh), eg=eg,
                        rhs=jnp.concatenate([vh * beta, kh * (beta * eg)], axis=1)))
    a_l = [jnp.where(strict, d["beta"] * _dot_nt(d["kb"], d["kb"]) * d["decay"], 0.0) for d in pre]
    p_l = [eye - a for a in a_l]
    pw_l = a_l
    lvl = 2
    while lvl < chunk:
        pw_l = [_dot(x, x, precision=HI) for x in pw_l]
        p_l = [p + _dot(p, x, precision=HI) for p, x in zip(p_l, pw_l)]
        lvl *= 2
    uw_l = [_dot(p, d["rhs"], precision=HI) for p, d in zip(p_l, pre)]
    st_l = [s_ref[bb, h] for bb, h in probs]
    stb_l = [_bf(st) for st in st_l]
    delta_l = [uw[:, 0:HEAD_DIM] - _dot(_bf(uw[:, HEAD_DIM:]), stb) for uw, stb in zip(uw_l, stb_l)]
    qk_l = [_dot_nt(_bf(d["qh"]), d["kb"]) * d["decay"] for d in pre]
    o_l = [_dot(_bf(qk), _bf(delta)) + _dot(_bf(d["qh"] * d["eg"]), stb)
           for qk, delta, d, stb in zip(qk_l, delta_l, pre, stb_l)]
    for (bb, h), d, st, delta in zip(probs, pre, st_l, delta_l):
        g_end = d["g_col"][chunk - 1:chunk, :]
        w_end = jnp.exp(g_end - d["g_col"])
        s_ref[bb, h] = jnp.exp(g_end) * st + _dot_tn(_bf(d["kh"] * w_end), _bf(delta))
    for bb in range(nb):
        outs = [_head_rms(o_l[bb * N_HEADS + h]) for h in range(N_HEADS)]
        y_ref[bb] = jnp.concatenate(outs, axis=1) * nw_ref[...] * _silu(z_ref[bb, :, 768:1024])


def _gdn_seq(z_gdn, small, conv_w, dtb_vec, alog_vec, nw, chunk, nb):
    b, l, _ = z_gdn.shape
    return pl.pallas_call(
        functools.partial(_gdn_body, chunk=chunk),
        grid=(b // nb, l // chunk),
        in_specs=[pl.BlockSpec((nb, chunk, 1024), lambda i, j: (i, j, 0)),
                  pl.BlockSpec((nb, chunk, LANES), lambda i, j: (i, j, 0)),
                  pl.BlockSpec((GDN_CONV, 3 * D_HEADS), lambda i, j: (0, 0)),
                  pl.BlockSpec((1, LANES), lambda i, j: (0, 0)),
                  pl.BlockSpec((1, LANES), lambda i, j: (0, 0)),
                  pl.BlockSpec((1, D_HEADS), lambda i, j: (0, 0))],
        out_specs=[pl.BlockSpec((nb, chunk, D_HEADS), lambda i, j: (i, j, 0)),
                   pl.BlockSpec((nb, N_HEADS, HEAD_DIM, HEAD_DIM), lambda i, j: (i, 0, 0, 0))],
        out_shape=[jax.ShapeDtypeStruct((b, l, D_HEADS), F32),
                   jax.ShapeDtypeStruct((b, N_HEADS, HEAD_DIM, HEAD_DIM), F32)],
        scratch_shapes=[pltpu.VMEM((nb, SUBLANES + chunk, 3 * D_HEADS), F32)],
        compiler_params=_cparams("parallel", "arbitrary"),
        name="gdn_seq",
    )(z_gdn, small, conv_w, dtb_vec, alog_vec, nw.reshape(1, -1))


def _row_to_col(row):
    n = row.shape[1]
    eye = lax.broadcasted_iota(jnp.int32, (n, n), 0) == lax.broadcasted_iota(jnp.int32, (n, n), 1)
    return jnp.sum(jnp.where(eye, jnp.broadcast_to(row, (n, n)), 0.0), axis=1, keepdims=True)


def _vec_mat(col, mat):
    return jnp.sum(col * mat, axis=0, keepdims=True)


def _step_body(zr_ref, zm_ref, zg_ref, sm_ref, cos_ref, sin_ref, mlb_ref, dtb_ref, alog_ref, cw_ref,
               nwr_ref, nwm_ref, nwg_ref, sr_ref, mc_ref, mn_ref, mm_ref, gs_ref, gc_ref,
               yr_ref, ym_ref, yg_ref, sr_o, mc_o, mn_o, mm_o, gs_o, gc_o):
    sm = sm_ref[0]
    lane = lax.broadcasted_iota(jnp.int32, (1, LANES), 1)
    zr = zr_ref[0]
    cos, sin = cos_ref[...], sin_ref[...]
    q = _rope256(zr[:, 0:256], cos, sin)
    k = _rope256(zr[:, 256:512], cos, sin) * SCALE
    v = zr[:, 512:768]
    outs = []
    for h in range(N_HEADS):
        gam = math.exp(_ret_log_decay(h))
        qh, kh, vh = _head(q, h), _head(k, h), _head(v, h)
        st = sr_ref[0, h]
        o = jnp.sum(qh * kh, axis=1, keepdims=True) * vh + _vec_mat(_row_to_col(qh) * gam, st)
        sr_o[0, h] = st * gam + _row_to_col(kh) * vh
        outs.append(_head_rms(o))
    yr_ref[0] = jnp.concatenate(outs, axis=1) * nwr_ref[...] * _silu(zr[:, 768:1024])
    zm = zm_ref[0]
    q, k, v, og = zm[:, 0:256], zm[:, 256:512] * SCALE, zm[:, 512:768], zm[:, 768:1024]
    pre = sm + mlb_ref[...]
    logf = jnp.minimum(pre, 0.0) - jnp.log(1.0 + jnp.exp(-jnp.abs(pre)))
    m_all = mm_ref[0]
    m_out = jnp.zeros((1, LANES), F32)
    outs = []
    for h in range(N_HEADS):
        i_g = pre[:, SM_MI + h:SM_MI + h + 1]
        b_g = logf[:, SM_MF + h:SM_MF + h + 1]
        inter = b_g + m_all[:, h:h + 1]
        m_t = jnp.maximum(inter, i_g)
        w = jnp.exp(i_g - m_t)
        g_in = jnp.exp(inter - m_t)
        qh, kh, vh = _head(q, h), _head(k, h), _head(v, h)
        cm, nv = mc_ref[0, h], mn_ref[0, h:h + 1, :]
        qk = jnp.sum(qh * kh, axis=1, keepdims=True) * w
        num = qk * vh + g_in * _vec_mat(_row_to_col(qh), cm)
        den = qk + g_in * jnp.sum(qh * nv, axis=1, keepdims=True)
        hh = num / jnp.maximum(jnp.abs(den), jnp.exp(-m_t))
        mc_o[0, h] = g_in * cm + _row_to_col(kh * w) * vh
        mn_o[0, h:h + 1, :] = g_in * nv + w * kh
        m_out = jnp.where(lane == h, m_t, m_out)
        outs.append(_head_rms(_sigmoid(_head(og, h)) * hh))
    mm_o[0] = m_out
    ym_ref[0] = jnp.concatenate(outs, axis=1) * nwm_ref[...]
    zg = zg_ref[0]
    x = zg[:, 0:768]
    buf = gc_ref[0]
    conv = x * cw_ref[GDN_CONV - 1:GDN_CONV, :]
    for j in range(GDN_CONV - 1):
        conv = conv + buf[j:j + 1, :] * cw_ref[j:j + 1, :]
    gc_o[0, 0:GDN_CONV - 2, :] = buf[1:GDN_CONV - 1, :]
    gc_o[0, GDN_CONV - 2:GDN_CONV - 1, :] = x
    act = _silu(conv)
    q, k, v = act[:, 0:256], act[:, 256:512], act[:, 512:768]
    g_all = -jnp.exp(alog_ref[...]) * _softplus(sm + dtb_ref[...])
    beta_all = _sigmoid(sm)
    outs = []
    for h in range(N_HEADS):
        eg = jnp.exp(g_all[:, SM_GA + h:SM_GA + h + 1])
        beta = beta_all[:, SM_GB + h:SM_GB + h + 1]
        qh = _l2norm(_head(q, h)) * SCALE
        kh = _l2norm(_head(k, h))
        vh = _head(v, h)
        st = gs_ref[0, h]
        delta = vh * beta - _vec_mat(_row_to_col(kh * (beta * eg)), st)
        o = jnp.sum(qh * kh, axis=1, keepdims=True) * delta + _vec_mat(_row_to_col(qh * eg), st)
        gs_o[0, h] = eg * st + _row_to_col(kh) * delta
        outs.append(_head_rms(o))
    yg_ref[0] = jnp.concatenate(outs, axis=1) * nwg_ref[...] * _silu(zg[:, 768:1024])


def _mixers_step(z_ret, z_ml, z_gdn, small, cos, sin, mlb, dtb, alog, conv_w, nwr, nwm, nwg,
                 s_ret, ml_c, ml_n, ml_m, gdn_s, gdn_conv):
    b = z_ret.shape[0]
    row = lambda n: pl.BlockSpec((1, 1, n), lambda i: (i, 0, 0))
    vec = lambda n: pl.BlockSpec((1, n), lambda i: (0, 0))
    mat = pl.BlockSpec((1, N_HEADS, HEAD_DIM, HEAD_DIM), lambda i: (i, 0, 0, 0))
    nsp = pl.BlockSpec((1, N_HEADS, HEAD_DIM), lambda i: (i, 0, 0))
    csp = pl.BlockSpec((1, GDN_CONV - 1, 3 * D_HEADS), lambda i: (i, 0, 0))
    mshape = jax.ShapeDtypeStruct((b, N_HEADS, HEAD_DIM, HEAD_DIM), F32)
    yshape = jax.ShapeDtypeStruct((b, 1, D_HEADS), F32)
    return pl.pallas_call(
        _step_body,
        grid=(b,),
        in_specs=[row(1024), row(1024), row(1024), row(LANES), vec(LANES), vec(LANES), vec(LANES), vec(LANES), vec(LANES),
                  pl.BlockSpec((GDN_CONV, 3 * D_HEADS), lambda i: (0, 0)), vec(D_HEADS), vec(D_HEADS), vec(D_HEADS),
                  mat, mat, nsp, row(LANES), mat, csp],
        out_specs=[row(D_HEADS), row(D_HEADS), row(D_HEADS), mat, mat, nsp, row(LANES), mat, csp],
        out_shape=[yshape, yshape, yshape, mshape, mshape,
                   jax.ShapeDtypeStruct((b, N_HEADS, HEAD_DIM), F32), jax.ShapeDtypeStruct((b, 1, LANES), F32),
                   mshape, jax.ShapeDtypeStruct((b, GDN_CONV - 1, 3 * D_HEADS), F32)],
        compiler_params=_cparams("parallel"),
        name="mixers_step",
    )(z_ret, z_ml, z_gdn, small, cos, sin, mlb, dtb, alog, conv_w,
      nwr.reshape(1, -1), nwm.reshape(1, -1), nwg.reshape(1, -1), s_ret, ml_c, ml_n, ml_m, gdn_s, gdn_conv)


def _t5_bucket_np(n):
    exact = T5_BUCKETS // 2
    n = np.maximum(np.asarray(n, np.int64), 0)
    x = np.maximum(n, 1).astype(np.float32) / np.float32(exact)
    large = exact + (np.log(x) / np.float32(math.log(T5_MAX_DIST / exact)) * np.float32(T5_BUCKETS - exact)).astype(np.int32)
    return np.where(n < exact, n, np.minimum(large, T5_BUCKETS - 1)).astype(np.int32)


def _bucket_thresholds():
    b = _t5_bucket_np(np.arange(4 * T5_MAX_DIST))
    return [int(np.argmax(b >= k)) for k in range(T5_BUCKETS // 2 + 1, T5_BUCKETS)]


def _bias_table_body(rbt_ref, oh_ref, o_ref):
    o_ref[...] = _dot(rbt_ref[...], oh_ref[...].astype(F32), precision=HI)


def _bias_tables(rel_bias, dists):
    n = len(dists)
    npad = -(-n // 512) * 512
    onehot = np.zeros((T5_BUCKETS, npad), np.float32)
    onehot[_t5_bucket_np(dists), np.arange(n)] = 1.0
    rbt = jnp.pad(rel_bias.T, ((0, SUBLANES - N_HEADS), (0, 0)))
    tn = 512
    out = pl.pallas_call(
        _bias_table_body,
        grid=(npad // tn,),
        in_specs=[pl.BlockSpec((SUBLANES, T5_BUCKETS), lambda i: (0, 0)),
                  pl.BlockSpec((T5_BUCKETS, tn), lambda i: (0, i))],
        out_specs=pl.BlockSpec((SUBLANES, tn), lambda i: (0, i)),
        out_shape=jax.ShapeDtypeStruct((SUBLANES, npad), F32),
        name="bias_tables",
    )(rbt, jnp.asarray(onehot, BF16))
    return out[:, :n]


def _prep_cmp_w(cmp_w, cmp_pe):
    eye2 = jnp.eye(2, dtype=F32)

    def half(w):
        full = jnp.einsum('cjde,cf->jcdfe', w, eye2)
        return full.reshape(CMP_STRIDE * 2 * HEAD_DIM, 2 * HEAD_DIM)

    wab = jnp.concatenate([half(cmp_w[:, :CMP_STRIDE]), half(cmp_w[:, CMP_STRIDE:])], axis=1)
    pe = jnp.concatenate([cmp_pe[:CMP_STRIDE].reshape(1, -1), cmp_pe[CMP_STRIDE:].reshape(1, -1)], axis=1)
    return wab, jnp.pad(pe, ((0, SUBLANES - 1), (0, 0)))


def _pe_term(pe_ref, wab_ref):
    kin = CMP_STRIDE * 2 * HEAD_DIM
    t = (_dot(pe_ref[:, 0:kin], wab_ref[:, 0:LANES].astype(F32), precision=HI)
         + _dot(pe_ref[:, kin:2 * kin], wab_ref[:, LANES:2 * LANES].astype(F32), precision=HI))
    return t[0:1, :]


def _combine_cmp(parts, pe_term):
    rows = parts.shape[0]
    nxt = pltpu.roll(parts[:, LANES:2 * LANES], rows - 1, 0)
    r = lax.broadcasted_iota(jnp.int32, (rows, LANES), 0)
    return jnp.where(r < rows - 1, parts[:, 0:LANES] + nxt + pe_term, 0.0)


def _compress_seq_body(r_ref, wab_ref, pe_ref, o_ref):
    parts = _dot(r_ref[0], wab_ref[...], precision=HI)
    o_ref[0] = _combine_cmp(parts, _pe_term(pe_ref, wab_ref))


def _compress_seq(rows, wab, pe):
    b, n16, kin = rows.shape
    return pl.pallas_call(
        _compress_seq_body,
        grid=(b,),
        in_specs=[pl.BlockSpec((1, n16, kin), lambda i: (i, 0, 0)),
                  pl.BlockSpec(wab.shape, lambda i: (0, 0)),
                  pl.BlockSpec(pe.shape, lambda i: (0, 0))],
        out_specs=pl.BlockSpec((1, n16, LANES), lambda i: (i, 0, 0)),
        out_shape=jax.ShapeDtypeStruct((b, n16, LANES), F32),
        compiler_params=_cparams("parallel"),
        name="nsa_compress_seq",
    )(rows, wab, pe)


def _bias_from_dist(dist, rb_ref, h):
    n = jnp.maximum(dist, 0)
    exact = T5_BUCKETS // 2
    out = jnp.full(dist.shape, rb_ref[exact, h], F32)
    for k, thr in enumerate(_bucket_thresholds()):
        out = jnp.where(n >= thr, rb_ref[exact + 1 + k, h], out)
    for j in range(exact):
        out = jnp.where(n == j, rb_ref[j, h], out)
    return out


def _softmax_rows(s, valid):
    s = jnp.where(valid, s, NEG_INF)
    m = jnp.max(s, axis=1, keepdims=True)
    m = jnp.where(m == NEG_INF, 0.0, m)
    p = jnp.exp(s - m)
    den = jnp.sum(p, axis=1, keepdims=True)
    return p / jnp.where(den > 0, den, 1.0)


def _top_blocks(score, k):
    lane = lax.broadcasted_iota(jnp.int32, score.shape, 1)
    width = score.shape[1]
    sel = jnp.zeros(score.shape, jnp.bool_)
    for _ in range(k):
        m = jnp.max(score, axis=1, keepdims=True)
        idx = jnp.min(jnp.where(score == m, lane, width), axis=1, keepdims=True)
        pick = (lane == idx) & (m > NEG_INF)
        sel = sel | pick
        score = jnp.where(pick, NEG_INF, score)
    return sel


def _tile_mask(ok):
    return jnp.concatenate([jnp.where(ok, 1.0, 0.0)] * N_HEADS, axis=0) > 0.5


def _stack_heads(x):
    return jnp.concatenate([_head(x, h) for h in range(N_HEADS)], axis=0)


def _toeplitz(tab_ref, h, r0, nrow, rows):
    strip = tab_ref[h, pl.ds(r0, nrow), :]
    flat = jnp.concatenate([strip[r:r + 1, :] for r in range(nrow)], axis=1)
    rolled = pltpu.roll(jnp.broadcast_to(flat, (rows, nrow * LANES)), 0, 1, stride=1, stride_axis=0)
    return rolled[:, LANES:]


SLC_KC = 512


MASK_BIG = 1e30


def _top_blocks_t(score, k):
    row = lax.broadcasted_iota(jnp.int32, score.shape, 0)
    n = score.shape[0]
    sel = jnp.zeros(score.shape, jnp.bool_)
    for _ in range(k):
        m = jnp.max(score, axis=0, keepdims=True)
        idx = jnp.min(jnp.where(score == m, row, n), axis=0, keepdims=True)
        pick = (row == idx) & (m > NEG_INF)
        sel = sel | pick
        score = jnp.where(pick, NEG_INF, score)
    return sel


def _bucket_index(dist):
    n = jnp.maximum(dist, 0)
    exact = T5_BUCKETS // 2
    big = jnp.full(dist.shape, exact, jnp.int32)
    for thr in _bucket_thresholds():
        big = big + (n >= thr).astype(jnp.int32)
    return jnp.where(n < exact, n, big)


def _gather_bias(rbt_ref, h, bucket):
    rows, width = bucket.shape
    piece = min(width, LANES)
    tab = jnp.broadcast_to(rbt_ref[h:h + 1, 0:piece], (rows, piece))
    parts = [jnp.take_along_axis(tab, bucket[:, s:s + piece], axis=1) for s in range(0, width, piece)]
    return parts[0] if len(parts) == 1 else jnp.concatenate(parts, axis=1)


def _tile_rows(x, n=N_HEADS):
    return jnp.concatenate([x] * n, axis=0)


def _nsa_seq_body(rb_ref, q_ref, sm_ref, nkv_ref, kvc_ref, tab_ref, rbt_ref, ovlt_ref, ehat_ref, y_ref, *, seq, p0):
    qb = Q_BLOCK
    rows = N_HEADS * qb
    n16 = seq // CMP_STRIDE
    nblk = seq // SLC_BLOCK
    bi = pl.program_id(1)
    q0 = bi * qb
    qs = _stack_heads(q_ref[0]) * SCALE
    qsb = _bf(qs)
    qpos = q0 + lax.broadcasted_iota(jnp.int32, (qb, 1), 0)
    head_of_row = lax.broadcasted_iota(jnp.int32, (rows, 1), 0) // qb

    kvc = kvc_ref[0]
    s_c = _dot_nt(qs, kvc[:, 0:HEAD_DIM], precision=HI)
    band = WINDOW + qb
    sw = pl.multiple_of(jnp.maximum(q0 - WINDOW, 0), qb)
    kw = _bf(nkv_ref[0, pl.ds(sw, band), 4 * HEAD_DIM:5 * HEAD_DIM])
    vw = _bf(nkv_ref[0, pl.ds(sw, band), 5 * HEAD_DIM:6 * HEAD_DIM])
    s_w = _dot_nt(qsb, kw)

    cmp_end = lax.broadcasted_iota(jnp.int32, (qb, n16), 1) * CMP_STRIDE + (CMP_LEN - 1)
    dist_c = qpos - cmp_end
    bucket_c = _bucket_index(dist_c)
    bias_c = jnp.concatenate([_gather_bias(rbt_ref, h, bucket_c) for h in range(N_HEADS)], axis=0)
    p_c = _softmax_rows(s_c + bias_c, _tile_mask(dist_c >= 0))
    o_c = _dot(_bf(p_c), _bf(kvc[:, HEAD_DIM:2 * HEAD_DIM]))
    p_sum = p_c[0:qb] + p_c[qb:2 * qb] + p_c[2 * qb:3 * qb] + p_c[3 * qb:4 * qb]
    imp_t = _dot_nt(ovlt_ref[...], p_sum, precision=HI)

    base_w = q0 - sw
    r0w = (p0 - base_w) // LANES - 1
    nrow_w = band // LANES + 1
    bias_w = jnp.concatenate([_toeplitz(tab_ref, h, r0w, nrow_w, qb) for h in range(N_HEADS)], axis=0)
    dist_w = (base_w + lax.broadcasted_iota(jnp.int32, (qb, band), 0)
              - lax.broadcasted_iota(jnp.int32, (qb, band), 1))
    p_w = _softmax_rows(s_w + bias_w, _tile_mask((dist_w >= 0) & (dist_w < WINDOW)))
    o_w = _dot(_bf(p_w), vw)

    qlane = q0 + lax.broadcasted_iota(jnp.int32, (1, qb), 1)
    cur = qlane // SLC_BLOCK
    rowb = lax.broadcasted_iota(jnp.int32, (nblk, qb), 0)
    forced = (rowb == 0) | (rowb == cur) | (rowb == cur - 1)
    score = jnp.where(forced, jnp.inf, imp_t)
    score = jnp.where(rowb > cur, NEG_INF, score)
    sel = jnp.where(_top_blocks_t(score, SLC_TOPK), 1.0, 0.0).T
    negm = _bf(_tile_rows((sel - 1.0) * MASK_BIG))

    thr_far = _bucket_thresholds()[-1] + SLC_KC - 1
    n_chunks = q0 // SLC_KC + 1
    n_far = jnp.maximum(q0 - thr_far + SLC_KC, 0) // SLC_KC
    far_bias = jnp.zeros((rows, 1), F32)
    for h in range(N_HEADS):
        far_bias = jnp.where(head_of_row == h, rb_ref[T5_BUCKETS - 1, h], far_bias)
    nrow_s = SLC_KC // LANES + 1
    causal_ij = (lax.broadcasted_iota(jnp.int32, (qb, SLC_KC), 0)
                 - lax.broadcasted_iota(jnp.int32, (qb, SLC_KC), 1))

    def scores(c):
        k0 = pl.multiple_of(c * SLC_KC, SLC_KC)
        ks = _bf(nkv_ref[0, pl.ds(k0, SLC_KC), 2 * HEAD_DIM:3 * HEAD_DIM])
        vs = _bf(nkv_ref[0, pl.ds(k0, SLC_KC), 3 * HEAD_DIM:4 * HEAD_DIM])
        return q0 - k0, _dot_nt(qsb, ks) + _dot_nt(negm, ehat_ref[c]), vs

    def online(carry, s, vs):
        m, l, acc = carry
        m_new = jnp.maximum(m, jnp.max(s, axis=1, keepdims=True))
        alpha = jnp.exp(m - m_new)
        p = jnp.exp(s - m_new)
        return m_new, alpha * l + jnp.sum(p, axis=1, keepdims=True), alpha * acc + _dot(_bf(p), vs)

    def far_step(c, carry):
        _, s, vs = scores(c)
        return online(carry, s + far_bias, vs)

    def near_step(c, carry):
        base, s, vs = scores(c)
        r0 = (p0 - base) // LANES - 1
        bias = jnp.concatenate([_toeplitz(tab_ref, h, r0, nrow_s, qb) for h in range(N_HEADS)], axis=0)
        keep = _tile_rows(jnp.where(base + causal_ij >= 0, 0.0, -MASK_BIG))
        return online(carry, s + bias + keep, vs)

    init = (jnp.full((rows, 1), -MASK_BIG / 10, F32), jnp.zeros((rows, 1), F32), jnp.zeros((rows, HEAD_DIM), F32))
    carry = lax.fori_loop(0, n_far, far_step, init)
    _, l_s, acc_s = lax.fori_loop(n_far, n_chunks, near_step, carry)
    o_s = acc_s / jnp.where(l_s > 0, l_s, 1.0)

    gates = _sigmoid(sm_ref[0])
    outs = []
    for h in range(N_HEADS):
        rs = slice(h * qb, (h + 1) * qb)
        g = lambda t: gates[:, SM_NG + 3 * h + t:SM_NG + 3 * h + t + 1]
        outs.append(g(0) * o_c[rs] + g(1) * o_s[rs] + g(2) * o_w[rs])
    y_ref[0] = jnp.concatenate(outs, axis=1)


def _nsa_seq(rel_bias, nq, small, nkv, kvc, tab, seq_p0):
    b, l, _ = nq.shape
    n16 = l // CMP_STRIDE
    nblk = l // SLC_BLOCK
    cmp_start = np.arange(n16)[None, :] * CMP_STRIDE
    slc_start = np.arange(nblk)[:, None] * SLC_BLOCK
    ovlt = ((cmp_start < slc_start + SLC_BLOCK) & (cmp_start + CMP_LEN > slc_start)
            & (np.arange(n16)[None, :] < n16 - 1)).astype(np.float32)
    ehat = (np.arange(l)[:, None] // SLC_BLOCK == np.arange(nblk)[None, :]).astype(np.float32)
    ehat = ehat.reshape(l // SLC_KC, SLC_KC, nblk)
    rbt = jnp.pad(rel_bias.T, ((0, SUBLANES - N_HEADS), (0, LANES - T5_BUCKETS)))
    nrows = tab.shape[1]
    const = lambda shape: pl.BlockSpec(shape, lambda i, j: (0,) * len(shape))
    return pl.pallas_call(
        functools.partial(_nsa_seq_body, seq=l, p0=seq_p0),
        grid=(b, l // Q_BLOCK),
        in_specs=[pl.BlockSpec(memory_space=pltpu.SMEM),
                  pl.BlockSpec((1, Q_BLOCK, D_HEADS), lambda i, j: (i, j, 0)),
                  pl.BlockSpec((1, Q_BLOCK, LANES), lambda i, j: (i, j, 0)),
                  pl.BlockSpec((1, l, 6 * HEAD_DIM), lambda i, j: (i, 0, 0)),
                  pl.BlockSpec((1, n16, LANES), lambda i, j: (i, 0, 0)),
                  const((SUBLANES, nrows, LANES)), const((SUBLANES, LANES)), const((nblk, n16)),
                  const((l // SLC_KC, SLC_KC, nblk))],
        out_specs=pl.BlockSpec((1, Q_BLOCK, D_HEADS), lambda i, j: (i, j, 0)),
        out_shape=jax.ShapeDtypeStruct((b, l, D_HEADS), F32),
        compiler_params=_cparams("parallel", "arbitrary"),
        name="nsa_seq",
    )(rel_bias, nq, small, nkv, kvc, tab, rbt, jnp.asarray(ovlt), jnp.asarray(ehat, BF16))


def _nsa_tab_seq_dists(seq):
    p0 = seq + Q_BLOCK
    length = p0 + WINDOW + Q_BLOCK + LANES
    return p0, np.maximum(p0 - np.arange(length), 0)


PAGES_PER_STEP = 16


def _pad_rows(x, rows=SUBLANES):
    return jnp.concatenate([x, jnp.zeros((rows - x.shape[0], x.shape[1]), x.dtype)], axis=0)


def _nsa_select_body(pt_ref, *refs, past, n_steps):
    del pt_ref
    pages = refs[:PAGES_PER_STEP]
    wab_ref, pe_ref, q_ref, tabc_ref, ovl_ref, oc_ref, top_ref, parts_ref, rows_ref = refs[PAGES_PER_STEP:]
    step = pl.program_id(1)
    blocks_per_page = PAGE_SIZE // CMP_STRIDE
    rows_per_step = PAGES_PER_STEP * blocks_per_page
    for p, pg in enumerate(pages):
        for c in range(2):
            rows_ref[p * PAGE_SIZE:(p + 1) * PAGE_SIZE, c * HEAD_DIM:(c + 1) * HEAD_DIM] = pg[c].T
    acc = jnp.zeros((rows_per_step, 2 * LANES), F32)
    for j in range(CMP_STRIDE):
        xj = rows_ref[pl.ds(j, rows_per_step, stride=CMP_STRIDE), :]
        acc = acc + _dot(_bf(xj), wab_ref[j * LANES:(j + 1) * LANES, :])
    parts_ref[pl.ds(pl.multiple_of(step * rows_per_step, rows_per_step), rows_per_step), :] = acc

    @pl.when(step == n_steps - 1)
    def _():
        n16 = past // CMP_STRIDE
        n_cmp = n16 - 1
        cur = past // SLC_BLOCK
        pe_term = _pe_term(pe_ref, wab_ref)
        kvc = _combine_cmp(parts_ref[...], pe_term)
        qs = _pad_rows(_stack_heads(q_ref[0]) * SCALE)
        s_c = _dot_nt(qs, kvc[:, 0:HEAD_DIM], precision=HI) + tabc_ref[...]
        lane = lax.broadcasted_iota(jnp.int32, (SUBLANES, n16), 1)
        p_c = _softmax_rows(s_c, lane < n_cmp)
        o_c = _dot(_bf(p_c), _bf(kvc[:, HEAD_DIM:2 * HEAD_DIM]))
        oc_ref[0] = jnp.concatenate([o_c[h:h + 1, :] for h in range(N_HEADS)], axis=1)
        p_sum = p_c[0:1] + p_c[1:2] + p_c[2:3] + p_c[3:4]
        imp = _dot(_pad_rows(p_sum), ovl_ref[...], precision=HI)[0:1, :]
        width = imp.shape[1]
        lane = lax.broadcasted_iota(jnp.int32, (1, width), 1)
        forced = (lane == 0) | (lane == cur) | (lane == cur - 1)
        score = jnp.where(forced, jnp.inf, imp)
        score = jnp.where(lane > cur, NEG_INF, score)
        out_lane = lax.broadcasted_iota(jnp.int32, (1, LANES), 1)
        top = jnp.zeros((1, LANES), jnp.int32)
        for r in range(SLC_TOPK):
            m = jnp.max(score, axis=1, keepdims=True)
            idx = jnp.min(jnp.where(score == m, lane, width), axis=1, keepdims=True)
            top = jnp.where(out_lane == r, idx, top)
            score = jnp.where(lane == idx, NEG_INF, score)
        top_ref[0] = top


def _nsa_select(layer, page_table, cache_t, wab_bf, pe, nq, tabc, past):
    b, n_pages = page_table.shape
    n_steps = n_pages // PAGES_PER_STEP
    n16 = past // CMP_STRIDE
    n_slc = past // SLC_BLOCK + 1
    width = -(-n_slc // LANES) * LANES
    cmp_start = np.arange(n16)[:, None] * CMP_STRIDE
    slc_start = np.arange(width)[None, :] * SLC_BLOCK
    ovl = ((cmp_start < slc_start + SLC_BLOCK) & (cmp_start + CMP_LEN > slc_start)
           & (np.arange(n16)[:, None] < n16 - 1) & (np.arange(width)[None, :] < n_slc)).astype(np.float32)

    n_pool = cache_t.shape[1]

    def page_spec(p):
        def imap(i, s, pt):
            pg = pt[jnp.minimum(i, b - 1), jnp.minimum(s, n_steps - 1) * PAGES_PER_STEP + p]
            return (layer, jnp.clip(pg, 0, n_pool - 1), 0, 0, 0)
        return pl.BlockSpec((None, None, 2, HEAD_DIM, PAGE_SIZE), imap)

    const = lambda shape: pl.BlockSpec(shape, lambda i, s, pt: (0,) * len(shape))
    grid_spec = pltpu.PrefetchScalarGridSpec(
        num_scalar_prefetch=1,
        grid=(b, n_steps),
        in_specs=[page_spec(p) for p in range(PAGES_PER_STEP)]
        + [const(wab_bf.shape), const(pe.shape), pl.BlockSpec((1, 1, D_HEADS), lambda i, s, pt: (i, 0, 0)),
           const(tabc.shape), const(ovl.shape)],
        out_specs=[pl.BlockSpec((1, 1, D_HEADS), lambda i, s, pt: (i, 0, 0)),
                   pl.BlockSpec((1, 1, LANES), lambda i, s, pt: (i, 0, 0))],
        scratch_shapes=[pltpu.VMEM((n16, 2 * LANES), F32),
                        pltpu.VMEM((PAGES_PER_STEP * PAGE_SIZE, 2 * HEAD_DIM), F32)],
    )
    return pl.pallas_call(
        functools.partial(_nsa_select_body, past=past, n_steps=n_steps),
        grid_spec=grid_spec,
        out_shape=[jax.ShapeDtypeStruct((b, 1, D_HEADS), F32), jax.ShapeDtypeStruct((b, 1, LANES), jnp.int32)],
        compiler_params=_cparams("parallel", "arbitrary"),
        name="nsa_select",
    )(page_table, *([cache_t] * PAGES_PER_STEP), wab_bf, pe, nq, tabc, jnp.asarray(ovl))


def _nsa_attend_body(pt_ref, top_ref, *refs, past):
    del pt_ref
    pages = refs[:SLC_TOPK]
    (q_ref, sm_ref, new_ref, oc_ref, swa_ref, tabs_ref, tabw_ref, y_ref, swa_out) = refs[SLC_TOPK:]
    b = pl.program_id(0)
    new_blk = past // SLC_BLOCK
    blocks_per_page = PAGE_SIZE // SLC_BLOCK
    qs = _bf(_pad_rows(_stack_heads(q_ref[0]) * SCALE))
    new = new_ref[0]
    lane_t = lax.broadcasted_iota(jnp.int32, (HEAD_DIM, PAGE_SIZE), 1)
    lane = lax.broadcasted_iota(jnp.int32, (SUBLANES, PAGE_SIZE), 1)
    new_kt = jnp.where(lane_t == 0, _row_to_col(new[:, 2 * HEAD_DIM:3 * HEAD_DIM]), 0.0)
    new_vt = jnp.where(lane_t == 0, _row_to_col(new[:, 3 * HEAD_DIM:4 * HEAD_DIM]), 0.0)
    scores, oks, vts = [], [], []
    for s in range(SLC_TOPK):
        ti = jnp.clip(top_ref[b, s], 0, new_blk)
        is_new = ti == new_blk
        kt = jnp.where(is_new, new_kt, pages[s][0])
        vts.append(jnp.where(is_new, new_vt, pages[s][1]))
        scores.append(_dot(qs, _bf(kt)) + tabs_ref[ti])
        kpos = ti * SLC_BLOCK + lane % SLC_BLOCK
        ok = (lane // SLC_BLOCK == ti % blocks_per_page) & (kpos <= past)
        oks.append(jnp.where(ok, 1.0, 0.0))
    p_s = _softmax_rows(jnp.concatenate(scores, axis=1), jnp.concatenate(oks, axis=1) > 0.5)
    o_s = None
    for s in range(SLC_TOPK):
        t = _dot_nt(_bf(p_s[:, s * PAGE_SIZE:(s + 1) * PAGE_SIZE]), _bf(vts[s]))
        o_s = t if o_s is None else o_s + t

    wlane = lax.broadcasted_iota(jnp.int32, (HEAD_DIM, WINDOW), 1)
    win = []
    for c in range(2):
        col = _row_to_col(new[:, (4 + c) * HEAD_DIM:(5 + c) * HEAD_DIM])
        win.append(jnp.where(wlane == WINDOW - 1, col, pltpu.roll(swa_ref[c], WINDOW - 1, 1)))
        swa_out[c] = win[c]
    s_w = _dot(qs, _bf(win[0])) + tabw_ref[...]
    p_w = _softmax_rows(s_w, jnp.full(s_w.shape, True))
    o_w = _dot_nt(_bf(p_w), _bf(win[1]))

    gates = _sigmoid(sm_ref[0])
    o_c = oc_ref[0]
    outs = []
    for h in range(N_HEADS):
        g = lambda t: gates[:, SM_NG + 3 * h + t:SM_NG + 3 * h + t + 1]
        outs.append(g(0) * _head(o_c, h) + g(1) * o_s[h:h + 1, :] + g(2) * o_w[h:h + 1, :])
    y_ref[0] = jnp.concatenate(outs, axis=1)


def _nsa_attend(layer, page_table, top, cache_t, nq, small, nkv, o_c, swa_t, tabs, tabw, past):
    b = nq.shape[0]
    last_page = past // PAGE_SIZE - 1
    blocks_per_page = PAGE_SIZE // SLC_BLOCK

    n_pool = cache_t.shape[1]

    def page_spec(s):
        def imap(i, pt, tp):
            ii = jnp.minimum(i, b - 1)
            pg = jnp.clip(tp[ii, s] // blocks_per_page, 0, last_page)
            return (layer, jnp.clip(pt[ii, pg], 0, n_pool - 1), 1, 0, 0)
        return pl.BlockSpec((None, None, 2, HEAD_DIM, PAGE_SIZE), imap)

    row = lambda n: pl.BlockSpec((1, 1, n), lambda i, pt, tp: (i, 0, 0))
    const = lambda shape: pl.BlockSpec(shape, lambda i, pt, tp: (0,) * len(shape))
    grid_spec = pltpu.PrefetchScalarGridSpec(
        num_scalar_prefetch=2,
        grid=(b,),
        in_specs=[page_spec(s) for s in range(SLC_TOPK)]
        + [row(D_HEADS), row(LANES), row(6 * HEAD_DIM), row(D_HEADS),
           pl.BlockSpec((None, None, 2, HEAD_DIM, WINDOW), lambda i, pt, tp: (layer, i, 0, 0, 0)),
           const(tabs.shape), const(tabw.shape)],
        out_specs=[row(D_HEADS), pl.BlockSpec((None, 2, HEAD_DIM, WINDOW), lambda i, pt, tp: (i, 0, 0, 0))],
    )
    return pl.pallas_call(
        functools.partial(_nsa_attend_body, past=past),
        grid_spec=grid_spec,
        out_shape=[jax.ShapeDtypeStruct((b, 1, D_HEADS), F32),
                   jax.ShapeDtypeStruct((b, 2, HEAD_DIM, WINDOW), F32)],
        compiler_params=_cparams("parallel"),
        name="nsa_attend",
    )(page_table, top, *([cache_t] * SLC_TOPK), nq, small, nkv, o_c, swa_t, tabs, tabw)


SEQ_CHUNK = 64
ROW_TILE = 256


def _lane_vec(pairs):
    v = jnp.zeros((1, LANES), F32)
    for off, vals in pairs:
        v = v.at[0, off:off + vals.shape[0]].set(vals.astype(F32))
    return v


def kernel(x_prompt, x_sample, cache_nsa_kv, cache_swa_kv, state_ret, state_mlstm_C, state_mlstm_n, state_mlstm_m, state_gdn, state_gdn_conv, state_ffn_conv, page_table, rel_bias, norm_pre_mix, w_in, mlstm_b_i, mlstm_b_f, gdn_conv_w, gdn_A_log, gdn_dt_bias, nsa_cmp_pe, nsa_cmp_w, ret_norm, mlstm_norm, gdn_norm, w_out, norm_post_mix, norm_pre_ffn, w_ffn_gate, w_ffn_up, ffn_conv_w, w_ffn_down, norm_post_ffn):
    depth = w_in.shape[0]
    bp, seq, _ = x_prompt.shape
    bs, dec_seq, _ = x_sample.shape
    assert dec_seq == 1 and seq % SLC_KC == 0 and seq >= WINDOW + Q_BLOCK
    n_pool = cache_nsa_kv.shape[1]
    past = page_table.shape[1] * PAGE_SIZE
    assert past >= WINDOW and past % SLC_BLOCK == 0 and page_table.shape[1] % PAGES_PER_STEP == 0

    p0, d_seq = _nsa_tab_seq_dists(seq)
    n16 = past // CMP_STRIDE
    n_slc = past // SLC_BLOCK + 1
    d_cmp = np.maximum(past - (np.arange(n16) * CMP_STRIDE + CMP_LEN - 1), 0)
    d_slc = np.maximum(past - np.arange(n_slc * SLC_BLOCK), 0)
    d_win = WINDOW - 1 - np.arange(WINDOW)
    tab = _bias_tables(rel_bias, np.concatenate([d_seq, d_cmp, d_slc, d_win]))
    o1, o2, o3 = len(d_seq), len(d_seq) + len(d_cmp), len(d_seq) + len(d_cmp) + len(d_slc)
    tab_seq = tab[:, :o1].reshape(SUBLANES, o1 // LANES, LANES)
    tabc = tab[:, o1:o2]
    tabs = jnp.transpose(tab[:, o2:o3].reshape(SUBLANES, n_slc, SLC_BLOCK), (1, 0, 2))
    tabs = jnp.concatenate([tabs, tabs], axis=2)
    tabw = tab[:, o3:]

    cache_t = jnp.transpose(cache_nsa_kv, (0, 1, 3, 4, 2))
    swa_t = jnp.transpose(cache_swa_kv, (0, 1, 3, 4, 2))
    cos_p, sin_p = _rope_tables(jnp.arange(seq))
    cos_s, sin_s = _rope_tables(jnp.arange(past, past + 1))

    xp = x_prompt
    xs = x_sample.reshape(bs, D_MODEL)
    p_states, s_states = [], []
    for l in range(depth):
        w_cat = _prep_w_in(w_in[l])
        w_out_bf = _bf(w_out[l])
        wg, wu, wd = _bf(w_ffn_gate[l]), _bf(w_ffn_up[l]), _bf(w_ffn_down[l])
        wab, pe = _prep_cmp_w(nsa_cmp_w[l], nsa_cmp_pe[l])
        ml_bias = _lane_vec([(SM_MI, mlstm_b_i[l]), (SM_MF, mlstm_b_f[l])])
        dtb = _lane_vec([(SM_GA, gdn_dt_bias[l])])
        alog = _lane_vec([(SM_GA, gdn_A_log[l])])

        z = _in_proj(xp.reshape(bp * seq, D_MODEL), norm_pre_mix[l], w_cat, ROW_TILE)
        z_ret, z_ml, z_gdn, z_nq, z_nkv, z_sm = [t.reshape(bp, seq, -1) for t in z]
        y_ret, st_ret = _retention_seq(z_ret, cos_p, sin_p, ret_norm[l], SEQ_CHUNK, bp)
        y_ml, st_c, st_n, st_m = _mlstm_seq(z_ml, z_sm, ml_bias, mlstm_norm[l], SEQ_CHUNK, bp)
        y_gdn, st_g = _gdn_seq(z_gdn, z_sm, gdn_conv_w[l], dtb, alog, gdn_norm[l], SEQ_CHUNK, bp)
        rows = z_nkv[..., 0:2 * HEAD_DIM].reshape(bp, seq // CMP_STRIDE, CMP_STRIDE * 2 * HEAD_DIM)
        kvc = _compress_seq(rows, wab, pe)
        y_nsa = _nsa_seq(rel_bias, z_nq, z_sm, z_nkv, kvc, tab_seq, p0)
        ys = [t.reshape(bp * seq, D_HEADS) for t in (y_ret, y_ml, y_gdn, y_nsa)]
        x1, h2 = _out_proj(xp.reshape(bp * seq, D_MODEL), ys, w_out_bf, norm_post_mix[l], norm_pre_ffn[l], ROW_TILE)
        xp, tail = _ffn_seq(x1.reshape(bp, seq, D_MODEL), h2.reshape(bp, seq, D_MODEL), wg, wu, wd,
                            ffn_conv_w[l], norm_post_ffn[l], ROW_TILE)
        p_states.append((
            z_nkv[..., 0:4 * HEAD_DIM].reshape(bp, seq, 4, HEAD_DIM),
            z_nkv[:, seq - WINDOW:, 4 * HEAD_DIM:].reshape(bp, WINDOW, 2, HEAD_DIM),
            st_ret, st_c, st_n, st_m[:, 0, :N_HEADS], st_g,
            z_gdn[:, seq - (GDN_CONV - 1):, 0:3 * D_HEADS],
            tail[:, SUBLANES - (FFN_CONV - 1):, :]))

        z = _in_proj(xs, norm_pre_mix[l], w_cat, bs)
        z_ret, z_ml, z_gdn, z_nq, z_nkv, z_sm = [t.reshape(bs, 1, -1) for t in z]
        m_pad = jnp.pad(state_mlstm_m[l], ((0, 0), (0, LANES - N_HEADS))).reshape(bs, 1, LANES)
        (y_ret, y_ml, y_gdn, st_ret, st_c, st_n, st_m, st_g, st_gc) = _mixers_step(
            z_ret, z_ml, z_gdn, z_sm, cos_s, sin_s, ml_bias, dtb, alog, gdn_conv_w[l],
            ret_norm[l], mlstm_norm[l], gdn_norm[l],
            state_ret[l], state_mlstm_C[l], state_mlstm_n[l], m_pad, state_gdn[l], state_gdn_conv[l])
        o_c, top = _nsa_select(l, page_table, cache_t, _bf(wab), pe, z_nq, tabc, past)
        y_nsa, swa_new = _nsa_attend(l, page_table, top[:, 0, :SLC_TOPK], cache_t, z_nq, z_sm, z_nkv, o_c,
                                     swa_t, tabs, tabw, past)
        ys = [t.reshape(bs, D_HEADS) for t in (y_ret, y_ml, y_gdn, y_nsa)]
        x1, h2 = _out_proj(xs, ys, w_out_bf, norm_post_mix[l], norm_pre_ffn[l], bs)
        xs, g_new = _ffn_step(x1, h2, state_ffn_conv[l][:, 0], state_ffn_conv[l][:, 1], wg, wu, wd,
                              ffn_conv_w[l], norm_post_ffn[l])
        s_states.append((
            z_nkv[..., 0:4 * HEAD_DIM].reshape(bs, 1, 4, HEAD_DIM),
            jnp.transpose(swa_new, (0, 3, 1, 2)),
            st_ret, st_c, st_n, st_m[:, 0, :N_HEADS], st_g, st_gc,
            jnp.concatenate([state_ffn_conv[l][:, 1:], g_new[:, None, :]], axis=1)))

    stack = lambda states: tuple(jnp.stack([s[k] for s in states]) for k in range(len(states[0])))
    return (xp, xs.reshape(bs, 1, D_MODEL)) + stack(p_states) + stack(s_states)
```

```python
import functools
import math

import jax
import jax.numpy as jnp
import numpy as np
from jax import lax
from jax.experimental import pallas as pl
from jax.experimental.pallas import tpu as pltpu

F32 = jnp.float32
BF16 = jnp.bfloat16
HI = lax.Precision.HIGHEST

D_MODEL = 1024
HEAD_DIM = 64
N_HEADS = 4
D_HEADS = N_HEADS * HEAD_DIM
D_FF = 2816
GDN_CONV = 4
FFN_CONV = 3
PAGE_SIZE = 128
CMP_STRIDE = 16
CMP_LEN = 32
SLC_BLOCK = 64
SLC_TOPK = 16
WINDOW = 512
Q_BLOCK = 128
T5_BUCKETS = 32
T5_MAX_DIST = 1024
ROPE_BASE = 10000.0
EPS = 1e-6
SCALE = HEAD_DIM ** -0.5
IN_WIDTHS = (256, 256, 256, 256, 256, 256, 256, 4, 4, 256, 768, 4, 4, 256, 256, 384, 12)
LANES = 128
SUBLANES = 8
VMEM_LIMIT = 56 * 1024 * 1024
NEG_INF = float("-inf")

SM_MI, SM_MF, SM_GA, SM_GB, SM_NG = 0, 4, 8, 12, 16


def _cparams(*sem):
    return pltpu.CompilerParams(dimension_semantics=sem, vmem_limit_bytes=VMEM_LIMIT)


def _rms(x, g):
    return x * lax.rsqrt(jnp.mean(x * x, axis=-1, keepdims=True) + EPS) * g


def _dot(a, b, **kw):
    return jnp.dot(a, b, preferred_element_type=F32, **kw)


def _dot_nt(a, b, **kw):
    return lax.dot_general(a, b, (((1,), (1,)), ((), ())), preferred_element_type=F32, **kw)


def _dot_tn(a, b, **kw):
    return lax.dot_general(a, b, (((0,), (0,)), ((), ())), preferred_element_type=F32, **kw)


def _bf(x):
    return x.astype(BF16)


def _sigmoid(x):
    return 1.0 / (1.0 + jnp.exp(-x))


def _silu(x):
    return x * _sigmoid(x)


def _softplus(x):
    return jnp.maximum(x, 0.0) + jnp.log(1.0 + jnp.exp(-jnp.abs(x)))


def _head_rms(o):
    return o * lax.rsqrt(jnp.mean(o * o, axis=-1, keepdims=True) + EPS)


PROJ_WIDTHS = (1024, 1024, 1024, 256, 384, 128)


def _prep_w_in(w):
    parts, off = [], 0
    for wd in IN_WIDTHS:
        parts.append(w[:, off:off + wd])
        off += wd
    (rq, rk, rv, rg, mq, mk, mv, mi, mf, mo, gqkv, ga, gbeta, gg, nq, nkv, ngate) = parts
    small = jnp.concatenate([mi, mf, ga, gbeta, ngate], axis=1)
    small = jnp.pad(small, ((0, 0), (0, LANES - small.shape[1])))
    cat = jnp.concatenate([rq, rk, rv, rg, mq, mk, mv, mo, gqkv, gg, nq, nkv, small], axis=1)
    return cat.astype(BF16)


NKV_OFF = sum(PROJ_WIDTHS[:4])


def _in_proj_body(x_ref, g_ref, w_ref, *refs, with_kv_t):
    h = _bf(_rms(x_ref[...], g_ref[...]))
    out_refs = refs[1:] if with_kv_t else refs
    off = 0
    for ref in out_refs[:len(PROJ_WIDTHS)]:
        n = ref.shape[-1]
        ref[...] = _dot(h, w_ref[:, off:off + n])
        off += n
    if with_kv_t:
        out_refs[-1][0] = _dot_nt(refs[0][...], h)


def _in_proj(x2d, g, w_cat, tm, seq=None):
    m = x2d.shape[0]
    ntot = sum(PROJ_WIDTHS)
    with_kv_t = seq is not None
    const = lambda shape: pl.BlockSpec(shape, lambda i: (0,) * len(shape), pipeline_mode=pl.Buffered(1))
    in_specs = [pl.BlockSpec((tm, D_MODEL), lambda i: (i, 0)), const((1, D_MODEL)), const((D_MODEL, ntot))]
    out_specs = [pl.BlockSpec((tm, n), lambda i: (i, 0)) for n in PROJ_WIDTHS]
    out_shape = [jax.ShapeDtypeStruct((m, n), F32) for n in PROJ_WIDTHS]
    args = [x2d, g.reshape(1, D_MODEL), w_cat]
    if with_kv_t:
        per_seq = seq // tm
        nkv = PROJ_WIDTHS[4]
        in_specs.append(const((nkv, D_MODEL)))
        args.append(w_cat[:, NKV_OFF:NKV_OFF + nkv].T)
        out_specs.append(pl.BlockSpec((1, nkv, tm), lambda i: (i // per_seq, 0, i % per_seq)))
        out_shape.append(jax.ShapeDtypeStruct((m // seq, nkv, seq), F32))
    return pl.pallas_call(
        functools.partial(_in_proj_body, with_kv_t=with_kv_t),
        grid=(m // tm,),
        in_specs=in_specs,
        out_specs=out_specs,
        out_shape=out_shape,
        compiler_params=_cparams("parallel"),
        name="in_proj",
    )(*args)


def _out_proj_body(x_ref, y0, y1, y2, y3, w_ref, gpost_ref, gpre_ref, x1_ref, h2_ref):
    acc = None
    for k, y in enumerate((y0, y1, y2, y3)):
        t = _dot(_bf(y[...]), w_ref[k * D_HEADS:(k + 1) * D_HEADS, :])
        acc = t if acc is None else acc + t
    x1 = x_ref[...] + _rms(acc, gpost_ref[...])
    x1_ref[...] = x1
    h2_ref[...] = _bf(_rms(x1, gpre_ref[...]))


def _out_proj(x2d, ys, w_out_bf, gpost, gpre, tm):
    m = x2d.shape[0]
    row = lambda n: pl.BlockSpec((tm, n), lambda i: (i, 0))
    vec = pl.BlockSpec((1, D_MODEL), lambda i: (0, 0))
    return pl.pallas_call(
        _out_proj_body,
        grid=(m // tm,),
        in_specs=[row(D_MODEL)] + [row(D_HEADS)] * 4
        + [pl.BlockSpec((D_MODEL, D_MODEL), lambda i: (0, 0)), vec, vec],
        out_specs=[row(D_MODEL), row(D_MODEL)],
        out_shape=[jax.ShapeDtypeStruct((m, D_MODEL), F32), jax.ShapeDtypeStruct((m, D_MODEL), BF16)],
        compiler_params=_cparams("parallel"),
        name="out_proj",
    )(x2d, *ys, w_out_bf, gpost.reshape(1, -1), gpre.reshape(1, -1))


FFN_TN = 256


def _gelu_tanh(x):
    return 0.5 * x * (1.0 + jnp.tanh(0.7978845608028654 * (x + 0.044715 * x * x * x)))


def _ffn_seq_body(x1_ref, h2_ref, wg_ref, wu_ref, wd_ref, cw_ref, gpost_ref, x2_ref, tail_ref, carry_ref, *, tm):
    @pl.when(pl.program_id(1) == 0)
    def _():
        carry_ref[...] = jnp.zeros_like(carry_ref)

    h2 = h2_ref[0]
    rows = lax.broadcasted_iota(jnp.int32, (tm, FFN_TN), 0)
    acc = jnp.zeros((tm, D_MODEL), F32)
    for n0 in range(0, D_FF, FFN_TN):
        g = _dot(h2, wg_ref[:, n0:n0 + FFN_TN])
        prev = carry_ref[:, n0:n0 + FFN_TN]
        g1 = jnp.where(rows == 0, prev[7:8, :], pltpu.roll(g, 1, 0))
        g2 = pltpu.roll(g, 2, 0)
        g2 = jnp.where(rows == 0, prev[6:7, :], jnp.where(rows == 1, prev[7:8, :], g2))
        cw = cw_ref[:, n0:n0 + FFN_TN]
        gate = cw[0:1, :] * g2 + cw[1:2, :] * g1 + cw[2:3, :] * g
        u = _dot(h2, wu_ref[:, n0:n0 + FFN_TN])
        a = _gelu_tanh(gate) * u
        acc = acc + _dot(_bf(a), wd_ref[n0:n0 + FFN_TN, :])
        carry_ref[:, n0:n0 + FFN_TN] = g[tm - SUBLANES:tm, :]
    x2_ref[0] = x1_ref[0] + _rms(acc, gpost_ref[...])
    tail_ref[0] = carry_ref[...]


def _ffn_seq(x1, h2, wg, wu, wd, cw, gpost, tm):
    b, l, _ = x1.shape
    const = lambda shape: pl.BlockSpec(shape, lambda i, j: (0,) * len(shape), pipeline_mode=pl.Buffered(1))
    return pl.pallas_call(
        functools.partial(_ffn_seq_body, tm=tm),
        grid=(b, l // tm),
        in_specs=[pl.BlockSpec((1, tm, D_MODEL), lambda i, j: (i, j, 0)),
                  pl.BlockSpec((1, tm, D_MODEL), lambda i, j: (i, j, 0)),
                  const((D_MODEL, D_FF)), const((D_MODEL, D_FF)), const((D_FF, D_MODEL)),
                  const((FFN_CONV, D_FF)), const((1, D_MODEL))],
        out_specs=[pl.BlockSpec((1, tm, D_MODEL), lambda i, j: (i, j, 0)),
                   pl.BlockSpec((1, SUBLANES, D_FF), lambda i, j: (i, 0, 0))],
        out_shape=[jax.ShapeDtypeStruct((b, l, D_MODEL), F32),
                   jax.ShapeDtypeStruct((b, SUBLANES, D_FF), F32)],
        scratch_shapes=[pltpu.VMEM((SUBLANES, D_FF), F32)],
        compiler_params=_cparams("parallel", "arbitrary"),
        name="ffn_seq",
    )(x1, h2, wg, wu, wd, cw, gpost.reshape(1, -1))


def _ffn_step_body(x1_ref, h2_ref, b0_ref, b1_ref, wg_ref, wu_ref, wd_ref, cw_ref, gpost_ref, x2_ref, g_ref):
    h2 = h2_ref[...]
    acc = jnp.zeros(x1_ref.shape, F32)
    for n0 in range(0, D_FF, FFN_TN):
        sl = slice(n0, n0 + FFN_TN)
        g = _dot(h2, wg_ref[:, sl])
        cw = cw_ref[:, sl]
        gate = cw[0:1, :] * b0_ref[:, sl] + cw[1:2, :] * b1_ref[:, sl] + cw[2:3, :] * g
        u = _dot(h2, wu_ref[:, sl])
        acc = acc + _dot(_bf(_gelu_tanh(gate) * u), wd_ref[sl, :])
        g_ref[:, sl] = g
    x2_ref[...] = x1_ref[...] + _rms(acc, gpost_ref[...])


def _ffn_step(x1, h2, b0, b1, wg, wu, wd, cw, gpost):
    m = x1.shape[0]
    return pl.pallas_call(
        _ffn_step_body,
        out_shape=[jax.ShapeDtypeStruct((m, D_MODEL), F32), jax.ShapeDtypeStruct((m, D_FF), F32)],
        compiler_params=pltpu.CompilerParams(vmem_limit_bytes=VMEM_LIMIT),
        name="ffn_step",
    )(x1, h2, b0, b1, wg, wu, wd, cw, gpost.reshape(1, -1))


def _rope_tables(pos):
    half = HEAD_DIM // 2
    inv = ROPE_BASE ** (-jnp.linspace(0.0, 1.0, half, dtype=F32))
    ang = pos.astype(F32)[:, None] * inv[None, :]
    return jnp.tile(jnp.cos(ang), (1, LANES // half)), jnp.tile(jnp.sin(ang), (1, LANES // half))


def _rope128(x, cos, sin):
    lane = lax.broadcasted_iota(jnp.int32, x.shape, 1)
    first = (lane % HEAD_DIM) < (HEAD_DIM // 2)
    other = jnp.where(first, -pltpu.roll(x, LANES - HEAD_DIM // 2, 1), pltpu.roll(x, HEAD_DIM // 2, 1))
    return x * cos + other * sin


def _rope256(x, cos, sin):
    return jnp.concatenate([_rope128(x[:, :LANES], cos, sin), _rope128(x[:, LANES:], cos, sin)], axis=1)


def _ret_log_decay(h):
    return math.log(1.0 - 2.0 ** (-5.0 - h))


def _head(x, h):
    return x[:, h * HEAD_DIM:(h + 1) * HEAD_DIM]


def _ret_body(z_ref, cos_ref, sin_ref, nw_ref, y_ref, s_ref, *, chunk):
    @pl.when(pl.program_id(1) == 0)
    def _():
        s_ref[...] = jnp.zeros_like(s_ref)

    nb = z_ref.shape[0]
    cos, sin = cos_ref[...], sin_ref[...]
    t = lax.broadcasted_iota(jnp.int32, (chunk, chunk), 0)
    s = lax.broadcasted_iota(jnp.int32, (chunk, chunk), 1)
    causal = t >= s
    diff = jnp.where(causal, t - s, 0).astype(F32)
    tcol = lax.broadcasted_iota(jnp.int32, (chunk, 1), 0).astype(F32)
    lgs = [_ret_log_decay(h) for h in range(N_HEADS)]
    dmat = [jnp.where(causal, jnp.exp(lg * diff), 0.0) for lg in lgs]
    xi = [jnp.exp(lg * (tcol + 1.0)) for lg in lgs]
    zeta = [jnp.exp(lg * (chunk - 1.0 - tcol)) for lg in lgs]
    probs = [(bb, h) for bb in range(nb) for h in range(N_HEADS)]
    qkv = []
    for bb in range(nb):
        q = _rope256(z_ref[bb, :, 0:256], cos, sin)
        k = _rope256(z_ref[bb, :, 256:512], cos, sin) * SCALE
        qkv.append((q, k, z_ref[bb, :, 512:768]))
    qh = [_head(qkv[bb][0], h) for bb, h in probs]
    kh = [_head(qkv[bb][1], h) for bb, h in probs]
    vh = [_bf(_head(qkv[bb][2], h)) for bb, h in probs]
    st = [s_ref[bb, h] for bb, h in probs]
    sc = [_dot_nt(_bf(q), _bf(k)) for q, k in zip(qh, kh)]
    cross = [_dot(_bf(q * xi[h]), _bf(s0)) for q, s0, (_, h) in zip(qh, st, probs)]
    upd = [_dot_tn(_bf(k * zeta[h]), v) for k, v, (_, h) in zip(kh, vh, probs)]
    o = [_dot(_bf(x * dmat[h]), v) + c for x, v, c, (_, h) in zip(sc, vh, cross, probs)]
    for i, (bb, h) in enumerate(probs):
        s_ref[bb, h] = st[i] * math.exp(lgs[h] * chunk) + upd[i]
    for bb in range(nb):
        outs = [_head_rms(o[bb * N_HEADS + h]) for h in range(N_HEADS)]
        y_ref[bb] = jnp.concatenate(outs, axis=1) * nw_ref[...] * _silu(z_ref[bb, :, 768:1024])


def _retention_seq(z_ret, cos, sin, nw, chunk, nb):
    b, l, _ = z_ret.shape
    return pl.pallas_call(
        functools.partial(_ret_body, chunk=chunk),
        grid=(b // nb, l // chunk),
        in_specs=[pl.BlockSpec((nb, chunk, 1024), lambda i, j: (i, j, 0)),
                  pl.BlockSpec((chunk, LANES), lambda i, j: (j, 0)),
                  pl.BlockSpec((chunk, LANES), lambda i, j: (j, 0)),
                  pl.BlockSpec((1, D_HEADS), lambda i, j: (0, 0))],
        out_specs=[pl.BlockSpec((nb, chunk, D_HEADS), lambda i, j: (i, j, 0)),
                   pl.BlockSpec((nb, N_HEADS, HEAD_DIM, HEAD_DIM), lambda i, j: (i, 0, 0, 0))],
        out_shape=[jax.ShapeDtypeStruct((b, l, D_HEADS), F32),
                   jax.ShapeDtypeStruct((b, N_HEADS, HEAD_DIM, HEAD_DIM), F32)],
        compiler_params=_cparams("parallel", "arbitrary"),
        name="retention_seq",
    )(z_ret, cos, sin, nw.reshape(1, -1))


def _lanes_to_rows(x, n=2 * SUBLANES):
    sel = (lax.broadcasted_iota(jnp.int32, (n, x.shape[1]), 0)
           == lax.broadcasted_iota(jnp.int32, (n, x.shape[1]), 1)).astype(F32)
    return _dot_nt(sel, x, precision=HI)


def _tri_incl(n):
    t = lax.broadcasted_iota(jnp.int32, (n, n), 0)
    s = lax.broadcasted_iota(jnp.int32, (n, n), 1)
    return t >= s


def _ml_body(z_ref, sm_ref, bias_ref, nw_ref, y_ref, c_ref, n_ref, m_ref, *, chunk):
    @pl.when(pl.program_id(1) == 0)
    def _():
        c_ref[...] = jnp.zeros_like(c_ref)
        n_ref[...] = jnp.zeros_like(n_ref)
        m_ref[...] = jnp.zeros_like(m_ref)

    nb = z_ref.shape[0]
    incl = _tri_incl(chunk)
    tri = incl.astype(F32)
    probs = [(bb, h) for bb in range(nb) for h in range(N_HEADS)]
    gate = []
    for bb in range(nb):
        pre = sm_ref[bb] + bias_ref[...]
        logf = jnp.minimum(pre, 0.0) - jnp.log(1.0 + jnp.exp(-jnp.abs(pre)))
        bcum = _dot(tri, logf, precision=HI)
        gate.append((pre, bcum, _lanes_to_rows(pre), _lanes_to_rows(bcum), m_ref[bb]))
    qh = [_head(z_ref[bb, :, 0:256], h) for bb, h in probs]
    kh = [_head(z_ref[bb, :, 256:512], h) * SCALE for bb, h in probs]
    vh = [_bf(_head(z_ref[bb, :, 512:768], h)) for bb, h in probs]
    cm = [c_ref[bb, h] for bb, h in probs]
    nv = [n_ref[bb, h:h + 1, :] for bb, h in probs]
    qk_raw = [_dot_nt(_bf(q), _bf(k)) for q, k in zip(qh, kh)]
    q_c = [_dot(_bf(q), _bf(c)) for q, c in zip(qh, cm)]
    i_col, b_col, inter, dlog = [], [], [], []
    for bb, h in probs:
        pre, bcum, pre_t, bcum_t, m_all = gate[bb]
        i_col.append(pre[:, SM_MI + h:SM_MI + h + 1])
        b_col.append(bcum[:, SM_MF + h:SM_MF + h + 1])
        inter.append(b_col[-1] + m_all[:, h:h + 1])
        dlog.append(jnp.where(incl, b_col[-1] - bcum_t[SM_MF + h:SM_MF + h + 1, :]
                              + pre_t[SM_MI + h:SM_MI + h + 1, :], NEG_INF))
    dmax = [jnp.max(x, axis=1, keepdims=True) for x in dlog]
    q_n = [jnp.sum(q * n, axis=1, keepdims=True) for q, n in zip(qh, nv)]
    stab = []
    for i in range(len(probs)):
        m_t = jnp.maximum(inter[i], dmax[i])
        m_end = m_t[chunk - 1:chunk, :]
        stab.append(dict(m_t=m_t, w=jnp.exp(dlog[i] - m_t), g_in=jnp.exp(inter[i] - m_t), m_end=m_end,
                         w_end=jnp.exp(b_col[i][chunk - 1:chunk, :] - b_col[i] + i_col[i] - m_end),
                         g_end=jnp.exp(inter[i][chunk - 1:chunk, :] - m_end)))
    qk = [x * d["w"] for x, d in zip(qk_raw, stab)]
    num = [_dot(_bf(x), v) + d["g_in"] * c for x, v, d, c in zip(qk, vh, stab, q_c)]
    upd = [_dot_tn(_bf(k * d["w_end"]), v) for k, d, v in zip(kh, stab, vh)]
    qk_sum = [jnp.sum(x, axis=1, keepdims=True) for x in qk]
    k_sum = [jnp.sum(d["w_end"] * k, axis=0, keepdims=True) for d, k in zip(stab, kh)]
    hh = []
    for i, (bb, h) in enumerate(probs):
        d = stab[i]
        den = qk_sum[i] + d["g_in"] * q_n[i]
        hh.append(num[i] / jnp.maximum(jnp.abs(den), jnp.exp(-d["m_t"])))
        c_ref[bb, h] = d["g_end"] * cm[i] + upd[i]
        n_ref[bb, h:h + 1, :] = d["g_end"] * nv[i] + k_sum[i]
    lane = lax.broadcasted_iota(jnp.int32, (1, LANES), 1)
    for bb in range(nb):
        m_out = jnp.zeros((1, LANES), F32)
        outs = []
        for h in range(N_HEADS):
            i = bb * N_HEADS + h
            m_out = jnp.where(lane == h, stab[i]["m_end"], m_out)
            outs.append(_head_rms(_sigmoid(_head(z_ref[bb, :, 768:1024], h)) * hh[i]))
        m_ref[bb] = m_out
        y_ref[bb] = jnp.concatenate(outs, axis=1) * nw_ref[...]


def _mlstm_seq(z_ml, small, bias_vec, nw, chunk, nb):
    b, l, _ = z_ml.shape
    return pl.pallas_call(
        functools.partial(_ml_body, chunk=chunk),
        grid=(b // nb, l // chunk),
        in_specs=[pl.BlockSpec((nb, chunk, 1024), lambda i, j: (i, j, 0)),
                  pl.BlockSpec((nb, chunk, LANES), lambda i, j: (i, j, 0)),
                  pl.BlockSpec((1, LANES), lambda i, j: (0, 0)),
                  pl.BlockSpec((1, D_HEADS), lambda i, j: (0, 0))],
        out_specs=[pl.BlockSpec((nb, chunk, D_HEADS), lambda i, j: (i, j, 0)),
                   pl.BlockSpec((nb, N_HEADS, HEAD_DIM, HEAD_DIM), lambda i, j: (i, 0, 0, 0)),
                   pl.BlockSpec((nb, N_HEADS, HEAD_DIM), lambda i, j: (i, 0, 0)),
                   pl.BlockSpec((nb, 1, LANES), lambda i, j: (i, 0, 0))],
        out_shape=[jax.ShapeDtypeStruct((b, l, D_HEADS), F32),
                   jax.ShapeDtypeStruct((b, N_HEADS, HEAD_DIM, HEAD_DIM), F32),
                   jax.ShapeDtypeStruct((b, N_HEADS, HEAD_DIM), F32),
                   jax.ShapeDtypeStruct((b, 1, LANES), F32)],
        compiler_params=_cparams("parallel", "arbitrary"),
        name="mlstm_seq",
    )(z_ml, small, bias_vec, nw.reshape(1, -1))


def _split_bf16(x):
    hi = _bf(x)
    return hi, _bf(x - hi.astype(F32))


def _dot3(a, b):
    (ah, al), (bh, bl) = a, b
    return _dot(ah, bh) + (_dot(ah, bl) + _dot(al, bh))


def _l2norm(x):
    return x * lax.rsqrt(jnp.sum(x * x, axis=-1, keepdims=True) + EPS)


def _gdn_body(z_ref, sm_ref, cw_ref, dtb_ref, alog_ref, nw_ref, y_ref, s_ref, buf_ref, *, chunk):
    @pl.when(pl.program_id(1) == 0)
    def _():
        s_ref[...] = jnp.zeros_like(s_ref)
        buf_ref[:, 0:SUBLANES, :] = jnp.zeros((buf_ref.shape[0], SUBLANES, 3 * D_HEADS), F32)

    nb = z_ref.shape[0]
    incl = _tri_incl(chunk)
    strict = lax.broadcasted_iota(jnp.int32, (chunk, chunk), 0) > lax.broadcasted_iota(jnp.int32, (chunk, chunk), 1)
    eye = (lax.broadcasted_iota(jnp.int32, (chunk, chunk), 0)
           == lax.broadcasted_iota(jnp.int32, (chunk, chunk), 1)).astype(F32)
    tri = incl.astype(F32)
    gates, acts = [], []
    for bb in range(nb):
        buf_ref[bb, SUBLANES:SUBLANES + chunk, :] = z_ref[bb, :, 0:768]
        conv = None
        for j in range(GDN_CONV):
            term = buf_ref[bb, pl.ds(SUBLANES - (GDN_CONV - 1) + j, chunk), :] * cw_ref[j:j + 1, :]
            conv = term if conv is None else conv + term
        buf_ref[bb, 0:SUBLANES, :] = buf_ref[bb, chunk:chunk + SUBLANES, :]
        acts.append(_silu(conv))
        sm = sm_ref[bb]
        g_all = -jnp.exp(alog_ref[...]) * _softplus(sm + dtb_ref[...])
        gcum = _dot(tri, g_all, precision=HI)
        gates.append((gcum, _lanes_to_rows(gcum), _sigmoid(sm)))

    probs = [(bb, h) for bb in range(nb) for h in range(N_HEADS)]
    pre = []
    for bb, h in probs:
        gcum, gcum_t, beta_all = gates[bb]
        g_col = gcum[:, SM_GA + h:SM_GA + h + 1]
        g_row = gcum_t[SM_GA + h:SM_GA + h + 1, :]
        beta = beta_all[:, SM_GB + h:SM_GB + h + 1]
        decay = jnp.where(incl, jnp.exp(jnp.where(incl, g_col - g_row, 0.0)), 0.0)
        act = acts[bb]
        qh = _l2norm(_head(act[:, 0:256], h)) * SCALE
        kh = _l2norm(_head(act[:, 256:512], h))
        vh = _head(act[:, 512:768], h)
        eg = jnp.exp(g_col)
        pre.append(dict(g_col=g_col, beta=beta, decay=decay, qh=qh, kh=kh, kb=_bf(kh), eg=eg,
                        rhs=jnp.concatenate([vh * beta, kh * (beta * eg)], axis=1)))
    a_l = [jnp.where(strict, d["beta"] * _dot_nt(d["kb"], d["kb"]) * d["decay"], 0.0) for d in pre]
    p_l = [eye - a for a in a_l]
    pw_s = [_split_bf16(a) for a in a_l]
    lvl = 2
    while lvl < chunk:
        pw_s = [_split_bf16(_dot3(x, x)) for x in pw_s]
        p_l = [p + _dot3(_split_bf16(p), x) for p, x in zip(p_l, pw_s)]
        lvl *= 2
    uw_l = [_dot3(_split_bf16(p), _split_bf16(d["rhs"])) for p, d in zip(p_l, pre)]
    st_l = [s_ref[bb, h] for bb, h in probs]
    stb_l = [_bf(st) for st in st_l]
    delta_l = [uw[:, 0:HEAD_DIM] - _dot(_bf(uw[:, HEAD_DIM:]), stb) for uw, stb in zip(uw_l, stb_l)]
    qk_l = [_dot_nt(_bf(d["qh"]), d["kb"]) * d["decay"] for d in pre]
    o_l = [_dot(_bf(qk), _bf(delta)) + _dot(_bf(d["qh"] * d["eg"]), stb)
           for qk, delta, d, stb in zip(qk_l, delta_l, pre, stb_l)]
    for (bb, h), d, st, delta in zip(probs, pre, st_l, delta_l):
        g_end = d["g_col"][chunk - 1:chunk, :]
        w_end = jnp.exp(g_end - d["g_col"])
        s_ref[bb, h] = jnp.exp(g_end) * st + _dot_tn(_bf(d["kh"] * w_end), _bf(delta))
    for bb in range(nb):
        outs = [_head_rms(o_l[bb * N_HEADS + h]) for h in range(N_HEADS)]
        y_ref[bb] = jnp.concatenate(outs, axis=1) * nw_ref[...] * _silu(z_ref[bb, :, 768:1024])


def _gdn_seq(z_gdn, small, conv_w, dtb_vec, alog_vec, nw, chunk, nb):
    b, l, _ = z_gdn.shape
    return pl.pallas_call(
        functools.partial(_gdn_body, chunk=chunk),
        grid=(b // nb, l // chunk),
        in_specs=[pl.BlockSpec((nb, chunk, 1024), lambda i, j: (i, j, 0)),
                  pl.BlockSpec((nb, chunk, LANES), lambda i, j: (i, j, 0)),
                  pl.BlockSpec((GDN_CONV, 3 * D_HEADS), lambda i, j: (0, 0)),
                  pl.BlockSpec((1, LANES), lambda i, j: (0, 0)),
                  pl.BlockSpec((1, LANES), lambda i, j: (0, 0)),
                  pl.BlockSpec((1, D_HEADS), lambda i, j: (0, 0))],
        out_specs=[pl.BlockSpec((nb, chunk, D_HEADS), lambda i, j: (i, j, 0)),
                   pl.BlockSpec((nb, N_HEADS, HEAD_DIM, HEAD_DIM), lambda i, j: (i, 0, 0, 0))],
        out_shape=[jax.ShapeDtypeStruct((b, l, D_HEADS), F32),
                   jax.ShapeDtypeStruct((b, N_HEADS, HEAD_DIM, HEAD_DIM), F32)],
        scratch_shapes=[pltpu.VMEM((nb, SUBLANES + chunk, 3 * D_HEADS), F32)],
        compiler_params=_cparams("parallel", "arbitrary"),
        name="gdn_seq",
    )(z_gdn, small, conv_w, dtb_vec, alog_vec, nw.reshape(1, -1))


def _row_to_col(row):
    n = row.shape[1]
    eye = lax.broadcasted_iota(jnp.int32, (n, n), 0) == lax.broadcasted_iota(jnp.int32, (n, n), 1)
    return jnp.sum(jnp.where(eye, jnp.broadcast_to(row, (n, n)), 0.0), axis=1, keepdims=True)


def _vec_mat(col, mat):
    return jnp.sum(col * mat, axis=0, keepdims=True)


def _step_body(zr_ref, zm_ref, zg_ref, sm_ref, cos_ref, sin_ref, mlb_ref, dtb_ref, alog_ref, cw_ref,
               nwr_ref, nwm_ref, nwg_ref, sr_ref, mc_ref, mn_ref, mm_ref, gs_ref, gc_ref,
               yr_ref, ym_ref, yg_ref, sr_o, mc_o, mn_o, mm_o, gs_o, gc_o):
    sm = sm_ref[0]
    lane = lax.broadcasted_iota(jnp.int32, (1, LANES), 1)
    zr = zr_ref[0]
    cos, sin = cos_ref[...], sin_ref[...]
    q = _rope256(zr[:, 0:256], cos, sin)
    k = _rope256(zr[:, 256:512], cos, sin) * SCALE
    v = zr[:, 512:768]
    outs = []
    for h in range(N_HEADS):
        gam = math.exp(_ret_log_decay(h))
        qh, kh, vh = _head(q, h), _head(k, h), _head(v, h)
        st = sr_ref[0, h]
        o = jnp.sum(qh * kh, axis=1, keepdims=True) * vh + _vec_mat(_row_to_col(qh) * gam, st)
        sr_o[0, h] = st * gam + _row_to_col(kh) * vh
        outs.append(_head_rms(o))
    yr_ref[0] = jnp.concatenate(outs, axis=1) * nwr_ref[...] * _silu(zr[:, 768:1024])
    zm = zm_ref[0]
    q, k, v, og = zm[:, 0:256], zm[:, 256:512] * SCALE, zm[:, 512:768], zm[:, 768:1024]
    pre = sm + mlb_ref[...]
    logf = jnp.minimum(pre, 0.0) - jnp.log(1.0 + jnp.exp(-jnp.abs(pre)))
    m_all = mm_ref[0]
    m_out = jnp.zeros((1, LANES), F32)
    outs = []
    for h in range(N_HEADS):
        i_g = pre[:, SM_MI + h:SM_MI + h + 1]
        b_g = logf[:, SM_MF + h:SM_MF + h + 1]
        inter = b_g + m_all[:, h:h + 1]
        m_t = jnp.maximum(inter, i_g)
        w = jnp.exp(i_g - m_t)
        g_in = jnp.exp(inter - m_t)
        qh, kh, vh = _head(q, h), _head(k, h), _head(v, h)
        cm, nv = mc_ref[0, h], mn_ref[0, h:h + 1, :]
        qk = jnp.sum(qh * kh, axis=1, keepdims=True) * w
        num = qk * vh + g_in * _vec_mat(_row_to_col(qh), cm)
        den = qk + g_in * jnp.sum(qh * nv, axis=1, keepdims=True)
        hh = num / jnp.maximum(jnp.abs(den), jnp.exp(-m_t))
        mc_o[0, h] = g_in * cm + _row_to_col(kh * w) * vh
        mn_o[0, h:h + 1, :] = g_in * nv + w * kh
        m_out = jnp.where(lane == h, m_t, m_out)
        outs.append(_head_rms(_sigmoid(_head(og, h)) * hh))
    mm_o[0] = m_out
    ym_ref[0] = jnp.concatenate(outs, axis=1) * nwm_ref[...]
    zg = zg_ref[0]
    x = zg[:, 0:768]
    buf = gc_ref[0]
    conv = x * cw_ref[GDN_CONV - 1:GDN_CONV, :]
    for j in range(GDN_CONV - 1):
        conv = conv + buf[j:j + 1, :] * cw_ref[j:j + 1, :]
    gc_o[0, 0:GDN_CONV - 2, :] = buf[1:GDN_CONV - 1, :]
    gc_o[0, GDN_CONV - 2:GDN_CONV - 1, :] = x
    act = _silu(conv)
    q, k, v = act[:, 0:256], act[:, 256:512], act[:, 512:768]
    g_all = -jnp.exp(alog_ref[...]) * _softplus(sm + dtb_ref[...])
    beta_all = _sigmoid(sm)
    outs = []
    for h in range(N_HEADS):
        eg = jnp.exp(g_all[:, SM_GA + h:SM_GA + h + 1])
        beta = beta_all[:, SM_GB + h:SM_GB + h + 1]
        qh = _l2norm(_head(q, h)) * SCALE
        kh = _l2norm(_head(k, h))
        vh = _head(v, h)
        st = gs_ref[0, h]
        delta = vh * beta - _vec_mat(_row_to_col(kh * (beta * eg)), st)
        o = jnp.sum(qh * kh, axis=1, keepdims=True) * delta + _vec_mat(_row_to_col(qh * eg), st)
        gs_o[0, h] = eg * st + _row_to_col(kh) * delta
        outs.append(_head_rms(o))
    yg_ref[0] = jnp.concatenate(outs, axis=1) * nwg_ref[...] * _silu(zg[:, 768:1024])


def _mixers_step(z_ret, z_ml, z_gdn, small, cos, sin, mlb, dtb, alog, conv_w, nwr, nwm, nwg,
                 s_ret, ml_c, ml_n, ml_m, gdn_s, gdn_conv):
    b = z_ret.shape[0]
    row = lambda n: pl.BlockSpec((1, 1, n), lambda i: (i, 0, 0))
    vec = lambda n: pl.BlockSpec((1, n), lambda i: (0, 0))
    mat = pl.BlockSpec((1, N_HEADS, HEAD_DIM, HEAD_DIM), lambda i: (i, 0, 0, 0))
    nsp = pl.BlockSpec((1, N_HEADS, HEAD_DIM), lambda i: (i, 0, 0))
    csp = pl.BlockSpec((1, GDN_CONV - 1, 3 * D_HEADS), lambda i: (i, 0, 0))
    mshape = jax.ShapeDtypeStruct((b, N_HEADS, HEAD_DIM, HEAD_DIM), F32)
    yshape = jax.ShapeDtypeStruct((b, 1, D_HEADS), F32)
    return pl.pallas_call(
        _step_body,
        grid=(b,),
        in_specs=[row(1024), row(1024), row(1024), row(LANES), vec(LANES), vec(LANES), vec(LANES), vec(LANES), vec(LANES),
                  pl.BlockSpec((GDN_CONV, 3 * D_HEADS), lambda i: (0, 0)), vec(D_HEADS), vec(D_HEADS), vec(D_HEADS),
                  mat, mat, nsp, row(LANES), mat, csp],
        out_specs=[row(D_HEADS), row(D_HEADS), row(D_HEADS), mat, mat, nsp, row(LANES), mat, csp],
        out_shape=[yshape, yshape, yshape, mshape, mshape,
                   jax.ShapeDtypeStruct((b, N_HEADS, HEAD_DIM), F32), jax.ShapeDtypeStruct((b, 1, LANES), F32),
                   mshape, jax.ShapeDtypeStruct((b, GDN_CONV - 1, 3 * D_HEADS), F32)],
        compiler_params=_cparams("parallel"),
        name="mixers_step",
    )(z_ret, z_ml, z_gdn, small, cos, sin, mlb, dtb, alog, conv_w,
      nwr.reshape(1, -1), nwm.reshape(1, -1), nwg.reshape(1, -1), s_ret, ml_c, ml_n, ml_m, gdn_s, gdn_conv)


def _t5_bucket_np(n):
    exact = T5_BUCKETS // 2
    n = np.maximum(np.asarray(n, np.int64), 0)
    x = np.maximum(n, 1).astype(np.float32) / np.float32(exact)
    large = exact + (np.log(x) / np.float32(math.log(T5_MAX_DIST / exact)) * np.float32(T5_BUCKETS - exact)).astype(np.int32)
    return np.where(n < exact, n, np.minimum(large, T5_BUCKETS - 1)).astype(np.int32)


def _bucket_thresholds():
    b = _t5_bucket_np(np.arange(4 * T5_MAX_DIST))
    return [int(np.argmax(b >= k)) for k in range(T5_BUCKETS // 2 + 1, T5_BUCKETS)]


def _bias_table_body(rbt_ref, oh_ref, o_ref):
    o_ref[...] = _dot(rbt_ref[...], oh_ref[...].astype(F32), precision=HI)


def _bias_tables(rel_bias, dists):
    n = len(dists)
    npad = -(-n // 512) * 512
    onehot = np.zeros((T5_BUCKETS, npad), np.float32)
    onehot[_t5_bucket_np(dists), np.arange(n)] = 1.0
    rbt = jnp.pad(rel_bias.T, ((0, SUBLANES - N_HEADS), (0, 0)))
    tn = 512
    out = pl.pallas_call(
        _bias_table_body,
        grid=(npad // tn,),
        in_specs=[pl.BlockSpec((SUBLANES, T5_BUCKETS), lambda i: (0, 0)),
                  pl.BlockSpec((T5_BUCKETS, tn), lambda i: (0, i))],
        out_specs=pl.BlockSpec((SUBLANES, tn), lambda i: (0, i)),
        out_shape=jax.ShapeDtypeStruct((SUBLANES, npad), F32),
        name="bias_tables",
    )(rbt, jnp.asarray(onehot, BF16))
    return out[:, :n]


def _prep_cmp_w(cmp_w, cmp_pe):
    eye2 = jnp.eye(2, dtype=F32)

    def half(w):
        full = jnp.einsum('cjde,cf->jcdfe', w, eye2)
        return full.reshape(CMP_STRIDE * 2 * HEAD_DIM, 2 * HEAD_DIM)

    wab = jnp.concatenate([half(cmp_w[:, :CMP_STRIDE]), half(cmp_w[:, CMP_STRIDE:])], axis=1)
    pe = jnp.concatenate([cmp_pe[:CMP_STRIDE].reshape(1, -1), cmp_pe[CMP_STRIDE:].reshape(1, -1)], axis=1)
    return wab, jnp.pad(pe, ((0, SUBLANES - 1), (0, 0)))


def _pe_term(pe_ref, wab_ref):
    kin = CMP_STRIDE * 2 * HEAD_DIM
    t = (_dot(pe_ref[:, 0:kin], wab_ref[:, 0:LANES].astype(F32), precision=HI)
         + _dot(pe_ref[:, kin:2 * kin], wab_ref[:, LANES:2 * LANES].astype(F32), precision=HI))
    return t[0:1, :]


def _combine_cmp(parts, pe_term):
    rows = parts.shape[0]
    nxt = pltpu.roll(parts[:, LANES:2 * LANES], rows - 1, 0)
    r = lax.broadcasted_iota(jnp.int32, (rows, LANES), 0)
    return jnp.where(r < rows - 1, parts[:, 0:LANES] + nxt + pe_term, 0.0)


def _compress_seq_body(r_ref, wab_ref, pe_ref, o_ref):
    parts = _dot(_bf(r_ref[0]), _bf(wab_ref[...]))
    o_ref[0] = _combine_cmp(parts, _pe_term(pe_ref, wab_ref))


def _compress_seq(rows, wab, pe):
    b, n16, kin = rows.shape
    return pl.pallas_call(
        _compress_seq_body,
        grid=(b,),
        in_specs=[pl.BlockSpec((1, n16, kin), lambda i: (i, 0, 0)),
                  pl.BlockSpec(wab.shape, lambda i: (0, 0)),
                  pl.BlockSpec(pe.shape, lambda i: (0, 0))],
        out_specs=pl.BlockSpec((1, n16, LANES), lambda i: (i, 0, 0)),
        out_shape=jax.ShapeDtypeStruct((b, n16, LANES), F32),
        compiler_params=_cparams("parallel"),
        name="nsa_compress_seq",
    )(rows, wab, pe)


def _bias_from_dist(dist, rb_ref, h):
    n = jnp.maximum(dist, 0)
    exact = T5_BUCKETS // 2
    out = jnp.full(dist.shape, rb_ref[exact, h], F32)
    for k, thr in enumerate(_bucket_thresholds()):
        out = jnp.where(n >= thr, rb_ref[exact + 1 + k, h], out)
    for j in range(exact):
        out = jnp.where(n == j, rb_ref[j, h], out)
    return out


def _softmax_rows(s, valid):
    s = jnp.where(valid, s, NEG_INF)
    m = jnp.max(s, axis=1, keepdims=True)
    m = jnp.where(m == NEG_INF, 0.0, m)
    p = jnp.exp(s - m)
    den = jnp.sum(p, axis=1, keepdims=True)
    return p / jnp.where(den > 0, den, 1.0)


def _top_blocks(score, k):
    lane = lax.broadcasted_iota(jnp.int32, score.shape, 1)
    width = score.shape[1]
    sel = jnp.zeros(score.shape, jnp.bool_)
    for _ in range(k):
        m = jnp.max(score, axis=1, keepdims=True)
        idx = jnp.min(jnp.where(score == m, lane, width), axis=1, keepdims=True)
        pick = (lane == idx) & (m > NEG_INF)
        sel = sel | pick
        score = jnp.where(pick, NEG_INF, score)
    return sel


def _tile_mask(ok):
    return jnp.concatenate([jnp.where(ok, 1.0, 0.0)] * N_HEADS, axis=0) > 0.5


def _stack_heads(x):
    return jnp.concatenate([_head(x, h) for h in range(N_HEADS)], axis=0)


def _toeplitz(tab_ref, h, r0, nrow, rows):
    strip = tab_ref[h, pl.ds(r0, nrow), :]
    flat = jnp.concatenate([strip[r:r + 1, :] for r in range(nrow)], axis=1)
    rolled = pltpu.roll(jnp.broadcast_to(flat, (rows, nrow * LANES)), 0, 1, stride=1, stride_axis=0)
    return rolled[:, LANES:]


SLC_KC = 512


MASK_BIG = 1e30


def _top_blocks_t(score, k):
    row = lax.broadcasted_iota(jnp.int32, score.shape, 0)
    n = score.shape[0]
    sel = jnp.zeros(score.shape, jnp.bool_)
    for _ in range(k):
        m = jnp.max(score, axis=0, keepdims=True)
        idx = jnp.min(jnp.where(score == m, row, n), axis=0, keepdims=True)
        pick = (row == idx) & (m > NEG_INF)
        sel = sel | pick
        score = jnp.where(pick, NEG_INF, score)
    return sel


def _bucket_index(dist):
    n = jnp.maximum(dist, 0)
    exact = T5_BUCKETS // 2
    big = jnp.full(dist.shape, exact, jnp.int32)
    for thr in _bucket_thresholds():
        big = big + (n >= thr).astype(jnp.int32)
    return jnp.where(n < exact, n, big)


def _gather_bias(rbt_ref, h, bucket):
    rows, width = bucket.shape
    piece = min(width, LANES)
    tab = jnp.broadcast_to(rbt_ref[h:h + 1, 0:piece], (rows, piece))
    parts = [jnp.take_along_axis(tab, bucket[:, s:s + piece], axis=1) for s in range(0, width, piece)]
    return parts[0] if len(parts) == 1 else jnp.concatenate(parts, axis=1)


def _tile_rows(x, n=N_HEADS):
    return jnp.concatenate([x] * n, axis=0)


def _nsa_seq_body(rb_ref, q_ref, sm_ref, nkv_ref, kvc_ref, tab_ref, rbt_ref, ovlt_ref, ehat_ref, y_ref, *, seq, p0):
    qb = Q_BLOCK
    rows = N_HEADS * qb
    n16 = seq // CMP_STRIDE
    nblk = seq // SLC_BLOCK
    bi = pl.program_id(1)
    q0 = bi * qb
    qs = _stack_heads(q_ref[0]) * SCALE
    qsb = _bf(qs)
    qpos = q0 + lax.broadcasted_iota(jnp.int32, (qb, 1), 0)
    head_of_row = lax.broadcasted_iota(jnp.int32, (rows, 1), 0) // qb

    kvc = kvc_ref[0]
    s_c = _dot_nt(qsb, _bf(kvc[:, 0:HEAD_DIM]))
    band = WINDOW + qb
    sw = pl.multiple_of(jnp.maximum(q0 - WINDOW, 0), qb)
    kw = _bf(nkv_ref[0, pl.ds(sw, band), 4 * HEAD_DIM:5 * HEAD_DIM])
    vw = _bf(nkv_ref[0, pl.ds(sw, band), 5 * HEAD_DIM:6 * HEAD_DIM])
    s_w = _dot_nt(qsb, kw)

    cmp_end = lax.broadcasted_iota(jnp.int32, (qb, n16), 1) * CMP_STRIDE + (CMP_LEN - 1)
    dist_c = qpos - cmp_end
    bucket_c = _bucket_index(dist_c)
    bias_c = jnp.concatenate([_gather_bias(rbt_ref, h, bucket_c) for h in range(N_HEADS)], axis=0)
    p_c = _softmax_rows(s_c + bias_c, _tile_mask(dist_c >= 0))
    p_cb = _bf(p_c)
    o_c = _dot(p_cb, _bf(kvc[:, HEAD_DIM:2 * HEAD_DIM]))
    p_r = p_cb.astype(F32)
    p_sum = p_r[0:qb] + p_r[qb:2 * qb] + p_r[2 * qb:3 * qb] + p_r[3 * qb:4 * qb]
    imp_t = _dot_nt(ovlt_ref[...], p_sum, precision=HI)

    base_w = q0 - sw
    r0w = (p0 - base_w) // LANES - 1
    nrow_w = band // LANES + 1
    bias_w = jnp.concatenate([_toeplitz(tab_ref, h, r0w, nrow_w, qb) for h in range(N_HEADS)], axis=0)
    dist_w = (base_w + lax.broadcasted_iota(jnp.int32, (qb, band), 0)
              - lax.broadcasted_iota(jnp.int32, (qb, band), 1))
    p_w = _softmax_rows(s_w + bias_w, _tile_mask((dist_w >= 0) & (dist_w < WINDOW)))
    o_w = _dot(_bf(p_w), vw)

    qlane = q0 + lax.broadcasted_iota(jnp.int32, (1, qb), 1)
    cur = qlane // SLC_BLOCK
    rowb = lax.broadcasted_iota(jnp.int32, (nblk, qb), 0)
    forced = (rowb == 0) | (rowb == cur) | (rowb == cur - 1)
    score = jnp.where(forced, jnp.inf, imp_t)
    score = jnp.where(rowb > cur, NEG_INF, score)
    sel = jnp.where(_top_blocks_t(score, SLC_TOPK), 1.0, 0.0).T
    negm = _bf(_tile_rows((sel - 1.0) * MASK_BIG))

    thr_far = _bucket_thresholds()[-1] + SLC_KC - 1
    n_chunks = q0 // SLC_KC + 1
    n_far = jnp.maximum(q0 - thr_far + SLC_KC, 0) // SLC_KC
    far_bias = jnp.zeros((rows, 1), F32)
    for h in range(N_HEADS):
        far_bias = jnp.where(head_of_row == h, rb_ref[T5_BUCKETS - 1, h], far_bias)
    nrow_s = SLC_KC // LANES + 1
    causal_ij = (lax.broadcasted_iota(jnp.int32, (qb, SLC_KC), 0)
                 - lax.broadcasted_iota(jnp.int32, (qb, SLC_KC), 1))

    def scores(c):
        k0 = pl.multiple_of(c * SLC_KC, SLC_KC)
        ks = _bf(nkv_ref[0, pl.ds(k0, SLC_KC), 2 * HEAD_DIM:3 * HEAD_DIM])
        vs = _bf(nkv_ref[0, pl.ds(k0, SLC_KC), 3 * HEAD_DIM:4 * HEAD_DIM])
        return q0 - k0, _dot_nt(qsb, ks) + _dot_nt(negm, ehat_ref[c]), vs

    def online(carry, s, vs):
        m, l, acc = carry
        m_new = jnp.maximum(m, jnp.max(s, axis=1, keepdims=True))
        alpha = jnp.exp(m - m_new)
        p = jnp.exp(s - m_new)
        return m_new, alpha * l + jnp.sum(p, axis=1, keepdims=True), alpha * acc + _dot(_bf(p), vs)

    def far_step(c, carry):
        _, s, vs = scores(c)
        return online(carry, s + far_bias, vs)

    def near_step(c, carry):
        base, s, vs = scores(c)
        r0 = (p0 - base) // LANES - 1
        bias = jnp.concatenate([_toeplitz(tab_ref, h, r0, nrow_s, qb) for h in range(N_HEADS)], axis=0)
        keep = _tile_rows(jnp.where(base + causal_ij >= 0, 0.0, -MASK_BIG))
        return online(carry, s + bias + keep, vs)

    init = (jnp.full((rows, 1), -MASK_BIG / 10, F32), jnp.zeros((rows, 1), F32), jnp.zeros((rows, HEAD_DIM), F32))
    carry = lax.fori_loop(0, n_far, far_step, init)
    _, l_s, acc_s = lax.fori_loop(n_far, n_chunks, near_step, carry)
    o_s = acc_s / jnp.where(l_s > 0, l_s, 1.0)

    gates = _sigmoid(sm_ref[0])
    outs = []
    for h in range(N_HEADS):
        rs = slice(h * qb, (h + 1) * qb)
        g = lambda t: gates[:, SM_NG + 3 * h + t:SM_NG + 3 * h + t + 1]
        outs.append(g(0) * o_c[rs] + g(1) * o_s[rs] + g(2) * o_w[rs])
    y_ref[0] = jnp.concatenate(outs, axis=1)


def _nsa_seq(rel_bias, nq, small, nkv, kvc, tab, seq_p0):
    b, l, _ = nq.shape
    n16 = l // CMP_STRIDE
    nblk = l // SLC_BLOCK
    cmp_start = np.arange(n16)[None, :] * CMP_STRIDE
    slc_start = np.arange(nblk)[:, None] * SLC_BLOCK
    ovlt = ((cmp_start < slc_start + SLC_BLOCK) & (cmp_start + CMP_LEN > slc_start)
            & (np.arange(n16)[None, :] < n16 - 1)).astype(np.float32)
    ehat = (np.arange(l)[:, None] // SLC_BLOCK == np.arange(nblk)[None, :]).astype(np.float32)
    ehat = ehat.reshape(l // SLC_KC, SLC_KC, nblk)
    rbt = jnp.pad(rel_bias.T, ((0, SUBLANES - N_HEADS), (0, LANES - T5_BUCKETS)))
    nrows = tab.shape[1]
    const = lambda shape: pl.BlockSpec(shape, lambda i, j: (0,) * len(shape))
    return pl.pallas_call(
        functools.partial(_nsa_seq_body, seq=l, p0=seq_p0),
        grid=(b, l // Q_BLOCK),
        in_specs=[pl.BlockSpec(memory_space=pltpu.SMEM),
                  pl.BlockSpec((1, Q_BLOCK, D_HEADS), lambda i, j: (i, j, 0)),
                  pl.BlockSpec((1, Q_BLOCK, LANES), lambda i, j: (i, j, 0)),
                  pl.BlockSpec((1, l, 6 * HEAD_DIM), lambda i, j: (i, 0, 0)),
                  pl.BlockSpec((1, n16, LANES), lambda i, j: (i, 0, 0)),
                  const((SUBLANES, nrows, LANES)), const((SUBLANES, LANES)), const((nblk, n16)),
                  const((l // SLC_KC, SLC_KC, nblk))],
        out_specs=pl.BlockSpec((1, Q_BLOCK, D_HEADS), lambda i, j: (i, j, 0)),
        out_shape=jax.ShapeDtypeStruct((b, l, D_HEADS), F32),
        compiler_params=_cparams("parallel", "arbitrary"),
        name="nsa_seq",
    )(rel_bias, nq, small, nkv, kvc, tab, rbt, jnp.asarray(ovlt), jnp.asarray(ehat, BF16))


def _nsa_tab_seq_dists(seq):
    p0 = seq + Q_BLOCK
    length = p0 + WINDOW + Q_BLOCK + LANES
    return p0, np.maximum(p0 - np.arange(length), 0)


PAGE_GROUP = 16
PAGES_PER_STEP = 32


def _pad_rows(x, rows=SUBLANES):
    return jnp.concatenate([x, jnp.zeros((rows - x.shape[0], x.shape[1]), x.dtype)], axis=0)


def _nsa_select_body(pt_ref, *refs, past, n_steps):
    del pt_ref
    pages = refs[:PAGES_PER_STEP]
    wab_ref, pe_ref, q_ref, tabc_ref, ovlt_ref, oc_ref, top_ref, parts_ref, rows_ref = refs[PAGES_PER_STEP:]
    step = pl.program_id(1)
    blocks_per_page = PAGE_SIZE // CMP_STRIDE
    group_rows = PAGE_GROUP * blocks_per_page
    n_groups = PAGES_PER_STEP // PAGE_GROUP
    for p, pg in enumerate(pages):
        g, pp = divmod(p, PAGE_GROUP)
        for c in range(2):
            rows_ref[g, pp * PAGE_SIZE:(pp + 1) * PAGE_SIZE, c * HEAD_DIM:(c + 1) * HEAD_DIM] = pg[c].T
    for g in range(n_groups):
        acc = jnp.zeros((group_rows, 2 * LANES), F32)
        for j in range(CMP_STRIDE):
            xj = rows_ref[g, pl.ds(j, group_rows, stride=CMP_STRIDE), :]
            acc = acc + _dot(_bf(xj), wab_ref[j * LANES:(j + 1) * LANES, :])
        row0 = pl.multiple_of((step * n_groups + g) * group_rows, group_rows)
        parts_ref[pl.ds(row0, group_rows), :] = acc

    @pl.when(step == n_steps - 1)
    def _():
        n16 = past // CMP_STRIDE
        n_cmp = n16 - 1
        cur = past // SLC_BLOCK
        pe_term = _pe_term(pe_ref, wab_ref)
        kvc = _combine_cmp(parts_ref[...], pe_term)
        qs = _pad_rows(_stack_heads(q_ref[0]) * SCALE)
        s_c = _dot_nt(_bf(qs), _bf(kvc[:, 0:HEAD_DIM])) + tabc_ref[...]
        lane = lax.broadcasted_iota(jnp.int32, (SUBLANES, n16), 1)
        p_c = _softmax_rows(s_c, lane < n_cmp)
        p_cb = _bf(p_c)
        o_c = _dot(p_cb, _bf(kvc[:, HEAD_DIM:2 * HEAD_DIM]))
        oc_ref[0] = jnp.concatenate([o_c[h:h + 1, :] for h in range(N_HEADS)], axis=1)
        imp_h = _dot_nt(ovlt_ref[...], p_cb)
        imp = imp_h[:, 0:1] + imp_h[:, 1:2] + imp_h[:, 2:3] + imp_h[:, 3:4]
        nrow = imp.shape[0]
        rowb = lax.broadcasted_iota(jnp.int32, (nrow, 1), 0)
        forced = (rowb == 0) | (rowb == cur) | (rowb == cur - 1)
        score = jnp.where(forced, jnp.inf, imp)
        score = jnp.where(rowb > cur, NEG_INF, score)
        out_lane = lax.broadcasted_iota(jnp.int32, (1, LANES), 1)
        top = jnp.zeros((1, LANES), jnp.int32)
        for r in range(SLC_TOPK):
            m = jnp.max(score, axis=0, keepdims=True)
            idx = jnp.min(jnp.where(score == m, rowb, nrow), axis=0, keepdims=True)
            top = jnp.where(out_lane == r, idx, top)
            score = jnp.where(rowb == idx, NEG_INF, score)
        top_ref[0] = top


def _nsa_select(layer, page_table, cache_t, wab_bf, pe, nq, tabc, past):
    b, n_pages = page_table.shape
    n_steps = n_pages // PAGES_PER_STEP
    n16 = past // CMP_STRIDE
    n_slc = past // SLC_BLOCK + 1
    nrow = -(-n_slc // SUBLANES) * SUBLANES
    cmp_start = np.arange(n16)[None, :] * CMP_STRIDE
    slc_start = np.arange(nrow)[:, None] * SLC_BLOCK
    ovlt = ((cmp_start < slc_start + SLC_BLOCK) & (cmp_start + CMP_LEN > slc_start)
            & (np.arange(n16)[None, :] < n16 - 1) & (np.arange(nrow)[:, None] < n_slc)).astype(np.float32)

    n_pool = cache_t.shape[1]

    def page_spec(p):
        def imap(i, s, pt):
            pg = pt[jnp.minimum(i, b - 1), jnp.minimum(s, n_steps - 1) * PAGES_PER_STEP + p]
            return (layer, jnp.clip(pg, 0, n_pool - 1), 0, 0, 0)
        return pl.BlockSpec((None, None, 2, HEAD_DIM, PAGE_SIZE), imap)

    const = lambda shape: pl.BlockSpec(shape, lambda i, s, pt: (0,) * len(shape))
    grid_spec = pltpu.PrefetchScalarGridSpec(
        num_scalar_prefetch=1,
        grid=(b, n_steps),
        in_specs=[page_spec(p) for p in range(PAGES_PER_STEP)]
        + [const(wab_bf.shape), const(pe.shape), pl.BlockSpec((1, 1, D_HEADS), lambda i, s, pt: (i, 0, 0)),
           const(tabc.shape), const(ovlt.shape)],
        out_specs=[pl.BlockSpec((1, 1, D_HEADS), lambda i, s, pt: (i, 0, 0)),
                   pl.BlockSpec((1, 1, LANES), lambda i, s, pt: (i, 0, 0))],
        scratch_shapes=[pltpu.VMEM((n16, 2 * LANES), F32),
                        pltpu.VMEM((PAGES_PER_STEP // PAGE_GROUP, PAGE_GROUP * PAGE_SIZE, 2 * HEAD_DIM), F32)],
    )
    return pl.pallas_call(
        functools.partial(_nsa_select_body, past=past, n_steps=n_steps),
        grid_spec=grid_spec,
        out_shape=[jax.ShapeDtypeStruct((b, 1, D_HEADS), F32), jax.ShapeDtypeStruct((b, 1, LANES), jnp.int32)],
        compiler_params=_cparams("parallel", "arbitrary"),
        name="nsa_select",
    )(page_table, *([cache_t] * PAGES_PER_STEP), wab_bf, pe, nq, tabc, jnp.asarray(ovlt, BF16))


def _nsa_attend_body(pt_ref, top_ref, *refs, past):
    del pt_ref
    pages = refs[:SLC_TOPK]
    (q_ref, sm_ref, new_ref, oc_ref, swa_ref, tabs_ref, tabw_ref, y_ref, swa_out) = refs[SLC_TOPK:]
    b = pl.program_id(0)
    new_blk = past // SLC_BLOCK
    blocks_per_page = PAGE_SIZE // SLC_BLOCK
    qs = _bf(_pad_rows(_stack_heads(q_ref[0]) * SCALE))
    new = new_ref[0]
    lane_t = lax.broadcasted_iota(jnp.int32, (HEAD_DIM, PAGE_SIZE), 1)
    lane = lax.broadcasted_iota(jnp.int32, (SUBLANES, PAGE_SIZE), 1)
    new_kt = jnp.where(lane_t == 0, _row_to_col(new[:, 2 * HEAD_DIM:3 * HEAD_DIM]), 0.0)
    new_vt = jnp.where(lane_t == 0, _row_to_col(new[:, 3 * HEAD_DIM:4 * HEAD_DIM]), 0.0)
    scores, oks, vts = [], [], []
    for s in range(SLC_TOPK):
        ti = jnp.clip(top_ref[b, s], 0, new_blk)
        is_new = ti == new_blk
        kt = jnp.where(is_new, new_kt, pages[s][0])
        vts.append(jnp.where(is_new, new_vt, pages[s][1]))
        scores.append(_dot(qs, _bf(kt)) + tabs_ref[ti])
        kpos = ti * SLC_BLOCK + lane % SLC_BLOCK
        ok = (lane // SLC_BLOCK == ti % blocks_per_page) & (kpos <= past)
        oks.append(jnp.where(ok, 1.0, 0.0))
    p_s = _softmax_rows(jnp.concatenate(scores, axis=1), jnp.concatenate(oks, axis=1) > 0.5)
    o_s = None
    for s in range(SLC_TOPK):
        t = _dot_nt(_bf(p_s[:, s * PAGE_SIZE:(s + 1) * PAGE_SIZE]), _bf(vts[s]))
        o_s = t if o_s is None else o_s + t

    wlane = lax.broadcasted_iota(jnp.int32, (HEAD_DIM, WINDOW), 1)
    win = []
    for c in range(2):
        col = _row_to_col(new[:, (4 + c) * HEAD_DIM:(5 + c) * HEAD_DIM])
        win.append(jnp.where(wlane == WINDOW - 1, col, pltpu.roll(swa_ref[c], WINDOW - 1, 1)))
        swa_out[c] = win[c]
    s_w = _dot(qs, _bf(win[0])) + tabw_ref[...]
    p_w = _softmax_rows(s_w, jnp.full(s_w.shape, True))
    o_w = _dot_nt(_bf(p_w), _bf(win[1]))

    gates = _sigmoid(sm_ref[0])
    o_c = oc_ref[0]
    outs = []
    for h in range(N_HEADS):
        g = lambda t: gates[:, SM_NG + 3 * h + t:SM_NG + 3 * h + t + 1]
        outs.append(g(0) * _head(o_c, h) + g(1) * o_s[h:h + 1, :] + g(2) * o_w[h:h + 1, :])
    y_ref[0] = jnp.concatenate(outs, axis=1)


def _nsa_attend(layer, page_table, top, cache_t, nq, small, nkv, o_c, swa_t, tabs, tabw, past):
    b = nq.shape[0]
    last_page = past // PAGE_SIZE - 1
    blocks_per_page = PAGE_SIZE // SLC_BLOCK

    n_pool = cache_t.shape[1]

    def page_spec(s):
        def imap(i, pt, tp):
            ii = jnp.minimum(i, b - 1)
            pg = jnp.clip(tp[ii, s] // blocks_per_page, 0, last_page)
            return (layer, jnp.clip(pt[ii, pg], 0, n_pool - 1), 1, 0, 0)
        return pl.BlockSpec((None, None, 2, HEAD_DIM, PAGE_SIZE), imap)

    row = lambda n: pl.BlockSpec((1, 1, n), lambda i, pt, tp: (i, 0, 0))
    const = lambda shape: pl.BlockSpec(shape, lambda i, pt, tp: (0,) * len(shape))
    grid_spec = pltpu.PrefetchScalarGridSpec(
        num_scalar_prefetch=2,
        grid=(b,),
        in_specs=[page_spec(s) for s in range(SLC_TOPK)]
        + [row(D_HEADS), row(LANES), row(6 * HEAD_DIM), row(D_HEADS),
           pl.BlockSpec((None, None, 2, HEAD_DIM, WINDOW), lambda i, pt, tp: (layer, i, 0, 0, 0)),
           const(tabs.shape), const(tabw.shape)],
        out_specs=[row(D_HEADS), pl.BlockSpec((None, 2, HEAD_DIM, WINDOW), lambda i, pt, tp: (i, 0, 0, 0))],
    )
    return pl.pallas_call(
        functools.partial(_nsa_attend_body, past=past),
        grid_spec=grid_spec,
        out_shape=[jax.ShapeDtypeStruct((b, 1, D_HEADS), F32),
                   jax.ShapeDtypeStruct((b, 2, HEAD_DIM, WINDOW), F32)],
        compiler_params=_cparams("parallel"),
        name="nsa_attend",
    )(page_table, top, *([cache_t] * SLC_TOPK), nq, small, nkv, o_c, swa_t, tabs, tabw)


SEQ_CHUNK = 64
ROW_TILE = 512
FFN_TILE = 512


def _lane_vec(pairs):
    v = jnp.zeros((1, LANES), F32)
    for off, vals in pairs:
        v = v.at[0, off:off + vals.shape[0]].set(vals.astype(F32))
    return v


def kernel(x_prompt, x_sample, cache_nsa_kv, cache_swa_kv, state_ret, state_mlstm_C, state_mlstm_n, state_mlstm_m, state_gdn, state_gdn_conv, state_ffn_conv, page_table, rel_bias, norm_pre_mix, w_in, mlstm_b_i, mlstm_b_f, gdn_conv_w, gdn_A_log, gdn_dt_bias, nsa_cmp_pe, nsa_cmp_w, ret_norm, mlstm_norm, gdn_norm, w_out, norm_post_mix, norm_pre_ffn, w_ffn_gate, w_ffn_up, ffn_conv_w, w_ffn_down, norm_post_ffn):
    depth = w_in.shape[0]
    bp, seq, _ = x_prompt.shape
    bs, dec_seq, _ = x_sample.shape
    assert dec_seq == 1 and seq % SLC_KC == 0 and seq >= WINDOW + Q_BLOCK
    n_pool = cache_nsa_kv.shape[1]
    past = page_table.shape[1] * PAGE_SIZE
    assert past >= WINDOW and past % SLC_BLOCK == 0 and page_table.shape[1] % PAGES_PER_STEP == 0

    p0, d_seq = _nsa_tab_seq_dists(seq)
    n16 = past // CMP_STRIDE
    n_slc = past // SLC_BLOCK + 1
    d_cmp = np.maximum(past - (np.arange(n16) * CMP_STRIDE + CMP_LEN - 1), 0)
    d_slc = np.maximum(past - np.arange(n_slc * SLC_BLOCK), 0)
    d_win = WINDOW - 1 - np.arange(WINDOW)
    tab = _bias_tables(rel_bias, np.concatenate([d_seq, d_cmp, d_slc, d_win]))
    o1, o2, o3 = len(d_seq), len(d_seq) + len(d_cmp), len(d_seq) + len(d_cmp) + len(d_slc)
    tab_seq = tab[:, :o1].reshape(SUBLANES, o1 // LANES, LANES)
    tabc = tab[:, o1:o2]
    tabs = jnp.transpose(tab[:, o2:o3].reshape(SUBLANES, n_slc, SLC_BLOCK), (1, 0, 2))
    tabs = jnp.concatenate([tabs, tabs], axis=2)
    tabw = tab[:, o3:]

    cache_t = jnp.transpose(cache_nsa_kv, (0, 1, 3, 4, 2))
    swa_t = jnp.transpose(cache_swa_kv, (0, 1, 3, 4, 2))
    cos_p, sin_p = _rope_tables(jnp.arange(seq))
    cos_s, sin_s = _rope_tables(jnp.arange(past, past + 1))

    xp = x_prompt
    xs = x_sample.reshape(bs, D_MODEL)
    p_states, s_states = [], []
    for l in range(depth):
        w_cat = _prep_w_in(w_in[l])
        w_out_bf = _bf(w_out[l])
        wg, wu, wd = _bf(w_ffn_gate[l]), _bf(w_ffn_up[l]), _bf(w_ffn_down[l])
        wab, pe = _prep_cmp_w(nsa_cmp_w[l], nsa_cmp_pe[l])
        ml_bias = _lane_vec([(SM_MI, mlstm_b_i[l]), (SM_MF, mlstm_b_f[l])])
        dtb = _lane_vec([(SM_GA, gdn_dt_bias[l])])
        alog = _lane_vec([(SM_GA, gdn_A_log[l])])

        z = _in_proj(xp.reshape(bp * seq, D_MODEL), norm_pre_mix[l], w_cat, ROW_TILE, seq=seq)
        z_ret, z_ml, z_gdn, z_nq, z_nkv, z_sm = [t.reshape(bp, seq, -1) for t in z[:-1]]
        nkv_t = z[-1].reshape(bp, 6, HEAD_DIM, seq)
        y_ret, st_ret = _retention_seq(z_ret, cos_p, sin_p, ret_norm[l], SEQ_CHUNK, bp)
        y_ml, st_c, st_n, st_m = _mlstm_seq(z_ml, z_sm, ml_bias, mlstm_norm[l], SEQ_CHUNK, bp)
        y_gdn, st_g = _gdn_seq(z_gdn, z_sm, gdn_conv_w[l], dtb, alog, gdn_norm[l], SEQ_CHUNK, bp)
        rows = z_nkv[..., 0:2 * HEAD_DIM].reshape(bp, seq // CMP_STRIDE, CMP_STRIDE * 2 * HEAD_DIM)
        kvc = _compress_seq(rows, wab, pe)
        y_nsa = _nsa_seq(rel_bias, z_nq, z_sm, z_nkv, kvc, tab_seq, p0)
        ys = [t.reshape(bp * seq, D_HEADS) for t in (y_ret, y_ml, y_gdn, y_nsa)]
        x1, h2 = _out_proj(xp.reshape(bp * seq, D_MODEL), ys, w_out_bf, norm_post_mix[l], norm_pre_ffn[l], ROW_TILE)
        ffn_tile = min(seq, FFN_TILE)
        xp, tail = _ffn_seq(x1.reshape(bp, seq, D_MODEL), h2.reshape(bp, seq, D_MODEL), wg, wu, wd,
                            ffn_conv_w[l], norm_post_ffn[l], ffn_tile)
        p_states.append((
            jnp.transpose(nkv_t[:, 0:4], (0, 3, 1, 2)),
            jnp.transpose(nkv_t[:, 4:6, :, seq - WINDOW:], (0, 3, 1, 2)),
            st_ret, st_c, st_n, st_m[:, 0, :N_HEADS], st_g,
            z_gdn[:, seq - (GDN_CONV - 1):, 0:3 * D_HEADS],
            tail[:, SUBLANES - (FFN_CONV - 1):, :]))

        z = _in_proj(xs, norm_pre_mix[l], w_cat, bs)
        z_ret, z_ml, z_gdn, z_nq, z_nkv, z_sm = [t.reshape(bs, 1, -1) for t in z]
        m_pad = jnp.pad(state_mlstm_m[l], ((0, 0), (0, LANES - N_HEADS))).reshape(bs, 1, LANES)
        (y_ret, y_ml, y_gdn, st_ret, st_c, st_n, st_m, st_g, st_gc) = _mixers_step(
            z_ret, z_ml, z_gdn, z_sm, cos_s, sin_s, ml_bias, dtb, alog, gdn_conv_w[l],
            ret_norm[l], mlstm_norm[l], gdn_norm[l],
            state_ret[l], state_mlstm_C[l], state_mlstm_n[l], m_pad, state_gdn[l], state_gdn_conv[l])
        o_c, top = _nsa_select(l, page_table, cache_t, _bf(wab), pe, z_nq, tabc, past)
        y_nsa, swa_new = _nsa_attend(l, page_table, top[:, 0, :SLC_TOPK], cache_t, z_nq, z_sm, z_nkv, o_c,
                                     swa_t, tabs, tabw, past)
        ys = [t.reshape(bs, D_HEADS) for t in (y_ret, y_ml, y_gdn, y_nsa)]
        x1, h2 = _out_proj(xs, ys, w_out_bf, norm_post_mix[l], norm_pre_ffn[l], bs)
        xs, g_new = _ffn_step(x1, h2, state_ffn_conv[l][:, 0], state_ffn_conv[l][:, 1], wg, wu, wd,
                              ffn_conv_w[l], norm_post_ffn[l])
        s_states.append((
            z_nkv[..., 0:4 * HEAD_DIM].reshape(bs, 1, 4, HEAD_DIM),
            jnp.transpose(swa_new, (0, 3, 1, 2)),
            st_ret, st_c, st_n, st_m[:, 0, :N_HEADS], st_g, st_gc,
            jnp.concatenate([state_ffn_conv[l][:, 1:], g_new[:, None, :]], axis=1)))

    stack = lambda states: tuple(jnp.stack([s[k] for s in states]) for k in range(len(states[0])))
    return (xp, xs.reshape(bs, 1, D_MODEL)) + stack(p_states) + stack(s_states)
```

```python
import functools
import math

import jax
import jax.numpy as jnp
import numpy as np
from jax import lax
from jax.experimental import pallas as pl
from jax.experimental.pallas import tpu as pltpu

F32 = jnp.float32
BF16 = jnp.bfloat16
HI = lax.Precision.HIGHEST

D_MODEL = 1024
HEAD_DIM = 64
N_HEADS = 4
D_HEADS = N_HEADS * HEAD_DIM
D_FF = 2816
GDN_CONV = 4
FFN_CONV = 3
PAGE_SIZE = 128
CMP_STRIDE = 16
CMP_LEN = 32
SLC_BLOCK = 64
SLC_TOPK = 16
WINDOW = 512
Q_BLOCK = 128
T5_BUCKETS = 32
T5_MAX_DIST = 1024
ROPE_BASE = 10000.0
EPS = 1e-6
SCALE = HEAD_DIM ** -0.5
IN_WIDTHS = (256, 256, 256, 256, 256, 256, 256, 4, 4, 256, 768, 4, 4, 256, 256, 384, 12)
LANES = 128
SUBLANES = 8
VMEM_LIMIT = 56 * 1024 * 1024
NEG_INF = float("-inf")

SM_MI, SM_MF, SM_GA, SM_GB, SM_NG = 0, 4, 8, 12, 16


def _cparams(*sem):
    return pltpu.CompilerParams(dimension_semantics=sem, vmem_limit_bytes=VMEM_LIMIT)


def _rms(x, g):
    return x * lax.rsqrt(jnp.mean(x * x, axis=-1, keepdims=True) + EPS) * g


def _dot(a, b, **kw):
    return jnp.dot(a, b, preferred_element_type=F32, **kw)


def _dot_nt(a, b, **kw):
    return lax.dot_general(a, b, (((1,), (1,)), ((), ())), preferred_element_type=F32, **kw)


def _dot_tn(a, b, **kw):
    return lax.dot_general(a, b, (((0,), (0,)), ((), ())), preferred_element_type=F32, **kw)


def _bf(x):
    return x.astype(BF16)


def _sigmoid(x):
    return 1.0 / (1.0 + jnp.exp(-x))


def _silu(x):
    return x * _sigmoid(x)


def _softplus(x):
    return jnp.maximum(x, 0.0) + jnp.log(1.0 + jnp.exp(-jnp.abs(x)))


def _head_rms(o):
    return o * lax.rsqrt(jnp.mean(o * o, axis=-1, keepdims=True) + EPS)


PROJ_WIDTHS = (1024, 1024, 1024, 256, 384, 128)


def _prep_w_in(w):
    wt = w.T
    parts, off = [], 0
    for wd in IN_WIDTHS:
        parts.append(wt[off:off + wd])
        off += wd
    (rq, rk, rv, rg, mq, mk, mv, mi, mf, mo, gqkv, ga, gbeta, gg, nq, nkv, ngate) = parts
    small = jnp.concatenate([mi, mf, ga, gbeta, ngate], axis=0)
    small = jnp.pad(small, ((0, LANES - small.shape[0]), (0, 0)))
    cat = jnp.concatenate([rq, rk, rv, rg, mq, mk, mv, mo, gqkv, gg, nq, nkv, small], axis=0)
    return cat.astype(BF16)


NKV_OFF = sum(PROJ_WIDTHS[:4])


def _in_proj_body(x_ref, g_ref, wt_ref, *out_refs, with_kv_t):
    h = _bf(_rms(x_ref[...], g_ref[...]))
    off = 0
    for ref in out_refs[:len(PROJ_WIDTHS)]:
        n = ref.shape[-1]
        ref[...] = _dot_nt(h, wt_ref[off:off + n, :])
        off += n
    if with_kv_t:
        out_refs[-1][0] = _dot_nt(wt_ref[NKV_OFF:NKV_OFF + PROJ_WIDTHS[4], :], h)


def _in_proj(x2d, g, w_cat_t, tm, seq=None):
    m = x2d.shape[0]
    ntot = sum(PROJ_WIDTHS)
    with_kv_t = seq is not None
    const = lambda shape: pl.BlockSpec(shape, lambda i: (0,) * len(shape), pipeline_mode=pl.Buffered(1))
    out_specs = [pl.BlockSpec((tm, n), lambda i: (i, 0)) for n in PROJ_WIDTHS]
    out_shape = [jax.ShapeDtypeStruct((m, n), F32) for n in PROJ_WIDTHS]
    if with_kv_t:
        per_seq = seq // tm
        nkv = PROJ_WIDTHS[4]
        out_specs.append(pl.BlockSpec((1, nkv, tm), lambda i: (i // per_seq, 0, i % per_seq)))
        out_shape.append(jax.ShapeDtypeStruct((m // seq, nkv, seq), F32))
    return pl.pallas_call(
        functools.partial(_in_proj_body, with_kv_t=with_kv_t),
        grid=(m // tm,),
        in_specs=[pl.BlockSpec((tm, D_MODEL), lambda i: (i, 0)), const((1, D_MODEL)), const((ntot, D_MODEL))],
        out_specs=out_specs,
        out_shape=out_shape,
        compiler_params=_cparams("parallel"),
        name="in_proj",
    )(x2d, g.reshape(1, D_MODEL), w_cat_t)


def _out_proj_body(x_ref, y0, y1, y2, y3, w_ref, gpost_ref, gpre_ref, x1_ref, h2_ref):
    acc = None
    for k, y in enumerate((y0, y1, y2, y3)):
        t = _dot(_bf(y[...]), w_ref[k * D_HEADS:(k + 1) * D_HEADS, :])
        acc = t if acc is None else acc + t
    x1 = x_ref[...] + _rms(acc, gpost_ref[...])
    x1_ref[...] = x1
    h2_ref[...] = _bf(_rms(x1, gpre_ref[...]))


def _out_proj(x2d, ys, w_out_bf, gpost, gpre, tm):
    m = x2d.shape[0]
    row = lambda n: pl.BlockSpec((tm, n), lambda i: (i, 0))
    vec = pl.BlockSpec((1, D_MODEL), lambda i: (0, 0))
    return pl.pallas_call(
        _out_proj_body,
        grid=(m // tm,),
        in_specs=[row(D_MODEL)] + [row(D_HEADS)] * 4
        + [pl.BlockSpec((D_MODEL, D_MODEL), lambda i: (0, 0)), vec, vec],
        out_specs=[row(D_MODEL), row(D_MODEL)],
        out_shape=[jax.ShapeDtypeStruct((m, D_MODEL), F32), jax.ShapeDtypeStruct((m, D_MODEL), BF16)],
        compiler_params=_cparams("parallel"),
        name="out_proj",
    )(x2d, *ys, w_out_bf, gpost.reshape(1, -1), gpre.reshape(1, -1))


FFN_TN = 256


def _gelu_tanh(x):
    c = 0.7978845608028654
    hx = 0.5 * x
    return hx + hx * jnp.tanh(x * (c + (0.044715 * c) * (x * x)))


def _ffn_seq_body(x1_ref, h2_ref, wg_ref, wu_ref, wd_ref, cw_ref, gpost_ref, x2_ref, tail_ref, carry_ref, act_ref,
                  *, tm):
    @pl.when(pl.program_id(1) == 0)
    def _():
        carry_ref[...] = jnp.zeros_like(carry_ref)

    h2 = h2_ref[0]
    row8 = lax.broadcasted_iota(jnp.int32, (SUBLANES, FFN_TN), 0)
    starts = list(range(0, D_FF, FFN_TN))
    up = lambda n0: (_dot(h2, wg_ref[:, n0:n0 + FFN_TN]), _dot(h2, wu_ref[:, n0:n0 + FFN_TN]))
    nxt = up(starts[0])
    for i, n0 in enumerate(starts):
        g, u = nxt
        if i + 1 < len(starts):
            nxt = up(starts[i + 1])
        prev = carry_ref[:, n0:n0 + FFN_TN]
        g1, g2 = pltpu.roll(g, 1, 0), pltpu.roll(g, 2, 0)
        top1 = jnp.where(row8 == 0, prev[7:8, :], g1[0:SUBLANES])
        top2 = jnp.where(row8 == 0, prev[6:7, :], jnp.where(row8 == 1, prev[7:8, :], g2[0:SUBLANES]))
        g1 = jnp.concatenate([top1, g1[SUBLANES:]], axis=0)
        g2 = jnp.concatenate([top2, g2[SUBLANES:]], axis=0)
        cw = cw_ref[:, n0:n0 + FFN_TN]
        gate = cw[0:1, :] * g2 + cw[1:2, :] * g1 + cw[2:3, :] * g
        a = _gelu_tanh(gate) * u
        act_ref[:, n0:n0 + FFN_TN] = _bf(a)
        carry_ref[:, n0:n0 + FFN_TN] = g[tm - SUBLANES:tm, :]
    acc = _dot(act_ref[...], wd_ref[...])
    x2_ref[0] = x1_ref[0] + _rms(acc, gpost_ref[...])
    tail_ref[0] = carry_ref[...]


def _ffn_seq(x1, h2, wg, wu, wd, cw, gpost, tm):
    b, l, _ = x1.shape
    const = lambda shape: pl.BlockSpec(shape, lambda i, j: (0,) * len(shape), pipeline_mode=pl.Buffered(1))
    return pl.pallas_call(
        functools.partial(_ffn_seq_body, tm=tm),
        grid=(b, l // tm),
        in_specs=[pl.BlockSpec((1, tm, D_MODEL), lambda i, j: (i, j, 0)),
                  pl.BlockSpec((1, tm, D_MODEL), lambda i, j: (i, j, 0)),
                  const((D_MODEL, D_FF)), const((D_MODEL, D_FF)), const((D_FF, D_MODEL)),
                  const((FFN_CONV, D_FF)), const((1, D_MODEL))],
        out_specs=[pl.BlockSpec((1, tm, D_MODEL), lambda i, j: (i, j, 0)),
                   pl.BlockSpec((1, SUBLANES, D_FF), lambda i, j: (i, 0, 0))],
        out_shape=[jax.ShapeDtypeStruct((b, l, D_MODEL), F32),
                   jax.ShapeDtypeStruct((b, SUBLANES, D_FF), F32)],
        scratch_shapes=[pltpu.VMEM((SUBLANES, D_FF), F32), pltpu.VMEM((tm, D_FF), BF16)],
        compiler_params=_cparams("parallel", "arbitrary"),
        name="ffn_seq",
    )(x1, h2, wg, wu, wd, cw, gpost.reshape(1, -1))


def _ffn_step_body(x1_ref, h2_ref, b0_ref, b1_ref, wg_ref, wu_ref, wd_ref, cw_ref, gpost_ref, x2_ref, g_ref):
    h2 = h2_ref[...]
    acc = jnp.zeros(x1_ref.shape, F32)
    for n0 in range(0, D_FF, FFN_TN):
        sl = slice(n0, n0 + FFN_TN)
        g = _dot(h2, wg_ref[:, sl])
        cw = cw_ref[:, sl]
        gate = cw[0:1, :] * b0_ref[:, sl] + cw[1:2, :] * b1_ref[:, sl] + cw[2:3, :] * g
        u = _dot(h2, wu_ref[:, sl])
        acc = acc + _dot(_bf(_gelu_tanh(gate) * u), wd_ref[sl, :])
        g_ref[:, sl] = g
    x2_ref[...] = x1_ref[...] + _rms(acc, gpost_ref[...])


def _ffn_step(x1, h2, b0, b1, wg, wu, wd, cw, gpost):
    m = x1.shape[0]
    return pl.pallas_call(
        _ffn_step_body,
        out_shape=[jax.ShapeDtypeStruct((m, D_MODEL), F32), jax.ShapeDtypeStruct((m, D_FF), F32)],
        compiler_params=pltpu.CompilerParams(vmem_limit_bytes=VMEM_LIMIT),
        name="ffn_step",
    )(x1, h2, b0, b1, wg, wu, wd, cw, gpost.reshape(1, -1))


def _rope_tables(pos):
    half = HEAD_DIM // 2
    inv = ROPE_BASE ** (-jnp.linspace(0.0, 1.0, half, dtype=F32))
    ang = pos.astype(F32)[:, None] * inv[None, :]
    return jnp.tile(jnp.cos(ang), (1, LANES // half)), jnp.tile(jnp.sin(ang), (1, LANES // half))


def _rope128(x, cos, sin):
    lane = lax.broadcasted_iota(jnp.int32, x.shape, 1)
    first = (lane % HEAD_DIM) < (HEAD_DIM // 2)
    other = jnp.where(first, -pltpu.roll(x, LANES - HEAD_DIM // 2, 1), pltpu.roll(x, HEAD_DIM // 2, 1))
    return x * cos + other * sin


def _rope256(x, cos, sin):
    return jnp.concatenate([_rope128(x[:, :LANES], cos, sin), _rope128(x[:, LANES:], cos, sin)], axis=1)


def _ret_log_decay(h):
    return math.log(1.0 - 2.0 ** (-5.0 - h))


def _head(x, h):
    return x[:, h * HEAD_DIM:(h + 1) * HEAD_DIM]


def _ret_body(z_ref, cos_ref, sin_ref, nw_ref, y_ref, s_ref, *, chunk):
    @pl.when(pl.program_id(1) == 0)
    def _():
        s_ref[...] = jnp.zeros_like(s_ref)

    nb = z_ref.shape[0]
    cos, sin = cos_ref[...], sin_ref[...]
    t = lax.broadcasted_iota(jnp.int32, (chunk, chunk), 0)
    s = lax.broadcasted_iota(jnp.int32, (chunk, chunk), 1)
    causal = t >= s
    diff = jnp.where(causal, t - s, 0).astype(F32)
    tcol = lax.broadcasted_iota(jnp.int32, (chunk, 1), 0).astype(F32)
    lgs = [_ret_log_decay(h) for h in range(N_HEADS)]
    dmat = [jnp.where(causal, jnp.exp(lg * diff), 0.0) for lg in lgs]
    xi = [jnp.exp(lg * (tcol + 1.0)) for lg in lgs]
    zeta = [jnp.exp(lg * (chunk - 1.0 - tcol)) for lg in lgs]
    probs = [(bb, h) for bb in range(nb) for h in range(N_HEADS)]
    qkv = []
    for bb in range(nb):
        q = _rope256(z_ref[bb, :, 0:256], cos, sin)
        k = _rope256(z_ref[bb, :, 256:512], cos, sin) * SCALE
        qkv.append((q, k, z_ref[bb, :, 512:768]))
    qh = [_head(qkv[bb][0], h) for bb, h in probs]
    kh = [_head(qkv[bb][1], h) for bb, h in probs]
    vh = [_bf(_head(qkv[bb][2], h)) for bb, h in probs]
    st = [s_ref[bb, h] for bb, h in probs]
    sc = [_dot_nt(_bf(q), _bf(k)) for q, k in zip(qh, kh)]
    cross = [_dot(_bf(q * xi[h]), _bf(s0)) for q, s0, (_, h) in zip(qh, st, probs)]
    upd = [_dot_tn(_bf(k * zeta[h]), v) for k, v, (_, h) in zip(kh, vh, probs)]
    o = [_dot(_bf(x * dmat[h]), v) + c for x, v, c, (_, h) in zip(sc, vh, cross, probs)]
    for i, (bb, h) in enumerate(probs):
        s_ref[bb, h] = st[i] * math.exp(lgs[h] * chunk) + upd[i]
    for bb in range(nb):
        outs = [_head_rms(o[bb * N_HEADS + h]) for h in range(N_HEADS)]
        y_ref[bb] = jnp.concatenate(outs, axis=1) * nw_ref[...] * _silu(z_ref[bb, :, 768:1024])


def _retention_seq(z_ret, cos, sin, nw, chunk, nb):
    b, l, _ = z_ret.shape
    return pl.pallas_call(
        functools.partial(_ret_body, chunk=chunk),
        grid=(b // nb, l // chunk),
        in_specs=[pl.BlockSpec((nb, chunk, 1024), lambda i, j: (i, j, 0)),
                  pl.BlockSpec((chunk, LANES), lambda i, j: (j, 0)),
                  pl.BlockSpec((chunk, LANES), lambda i, j: (j, 0)),
                  pl.BlockSpec((1, D_HEADS), lambda i, j: (0, 0))],
        out_specs=[pl.BlockSpec((nb, chunk, D_HEADS), lambda i, j: (i, j, 0)),
                   pl.BlockSpec((nb, N_HEADS, HEAD_DIM, HEAD_DIM), lambda i, j: (i, 0, 0, 0))],
        out_shape=[jax.ShapeDtypeStruct((b, l, D_HEADS), F32),
                   jax.ShapeDtypeStruct((b, N_HEADS, HEAD_DIM, HEAD_DIM), F32)],
        compiler_params=_cparams("parallel", "arbitrary"),
        name="retention_seq",
    )(z_ret, cos, sin, nw.reshape(1, -1))


def _lanes_to_rows(x, n=2 * SUBLANES):
    sel = (lax.broadcasted_iota(jnp.int32, (n, x.shape[1]), 0)
           == lax.broadcasted_iota(jnp.int32, (n, x.shape[1]), 1)).astype(F32)
    return _dot_nt(sel, x, precision=HI)


def _tri_incl(n):
    t = lax.broadcasted_iota(jnp.int32, (n, n), 0)
    s = lax.broadcasted_iota(jnp.int32, (n, n), 1)
    return t >= s


def _ml_body(z_ref, sm_ref, bias_ref, nw_ref, y_ref, c_ref, n_ref, m_ref, *, chunk):
    @pl.when(pl.program_id(1) == 0)
    def _():
        c_ref[...] = jnp.zeros_like(c_ref)
        n_ref[...] = jnp.zeros_like(n_ref)
        m_ref[...] = jnp.zeros_like(m_ref)

    nb = z_ref.shape[0]
    incl = _tri_incl(chunk)
    tri = incl.astype(F32)
    probs = [(bb, h) for bb in range(nb) for h in range(N_HEADS)]
    gate = []
    for bb in range(nb):
        pre = sm_ref[bb] + bias_ref[...]
        logf = jnp.minimum(pre, 0.0) - jnp.log(1.0 + jnp.exp(-jnp.abs(pre)))
        bcum = _dot(tri, logf, precision=HI)
        gate.append((pre, bcum, _lanes_to_rows(pre), _lanes_to_rows(bcum), m_ref[bb]))
    qh = [_head(z_ref[bb, :, 0:256], h) for bb, h in probs]
    kh = [_head(z_ref[bb, :, 256:512], h) * SCALE for bb, h in probs]
    vh = [_bf(_head(z_ref[bb, :, 512:768], h)) for bb, h in probs]
    cm = [c_ref[bb, h] for bb, h in probs]
    nv = [n_ref[bb, h:h + 1, :] for bb, h in probs]
    qk_raw = [_dot_nt(_bf(q), _bf(k)) for q, k in zip(qh, kh)]
    q_c = [_dot(_bf(q), _bf(c)) for q, c in zip(qh, cm)]
    i_col, b_col, inter, dlog = [], [], [], []
    for bb, h in probs:
        pre, bcum, pre_t, bcum_t, m_all = gate[bb]
        i_col.append(pre[:, SM_MI + h:SM_MI + h + 1])
        b_col.append(bcum[:, SM_MF + h:SM_MF + h + 1])
        inter.append(b_col[-1] + m_all[:, h:h + 1])
        dlog.append(jnp.where(incl, b_col[-1] - bcum_t[SM_MF + h:SM_MF + h + 1, :]
                              + pre_t[SM_MI + h:SM_MI + h + 1, :], NEG_INF))
    dmax = [jnp.max(x, axis=1, keepdims=True) for x in dlog]
    q_n = [jnp.sum(q * n, axis=1, keepdims=True) for q, n in zip(qh, nv)]
    stab = []
    for i in range(len(probs)):
        m_t = jnp.maximum(inter[i], dmax[i])
        m_end = m_t[chunk - 1:chunk, :]
        stab.append(dict(m_t=m_t, w=jnp.exp(dlog[i] - m_t), g_in=jnp.exp(inter[i] - m_t), m_end=m_end,
                         w_end=jnp.exp(b_col[i][chunk - 1:chunk, :] - b_col[i] + i_col[i] - m_end),
                         g_end=jnp.exp(inter[i][chunk - 1:chunk, :] - m_end)))
    qk = [x * d["w"] for x, d in zip(qk_raw, stab)]
    num = [_dot(_bf(x), v) + d["g_in"] * c for x, v, d, c in zip(qk, vh, stab, q_c)]
    upd = [_dot_tn(_bf(k * d["w_end"]), v) for k, d, v in zip(kh, stab, vh)]
    qk_sum = [jnp.sum(x, axis=1, keepdims=True) for x in qk]
    k_sum = [jnp.sum(d["w_end"] * k, axis=0, keepdims=True) for d, k in zip(stab, kh)]
    hh = []
    for i, (bb, h) in enumerate(probs):
        d = stab[i]
        den = qk_sum[i] + d["g_in"] * q_n[i]
        hh.append(num[i] / jnp.maximum(jnp.abs(den), jnp.exp(-d["m_t"])))
        c_ref[bb, h] = d["g_end"] * cm[i] + upd[i]
        n_ref[bb, h:h + 1, :] = d["g_end"] * nv[i] + k_sum[i]
    lane = lax.broadcasted_iota(jnp.int32, (1, LANES), 1)
    for bb in range(nb):
        m_out = jnp.zeros((1, LANES), F32)
        outs = []
        for h in range(N_HEADS):
            i = bb * N_HEADS + h
            m_out = jnp.where(lane == h, stab[i]["m_end"], m_out)
            outs.append(_head_rms(_sigmoid(_head(z_ref[bb, :, 768:1024], h)) * hh[i]))
        m_ref[bb] = m_out
        y_ref[bb] = jnp.concatenate(outs, axis=1) * nw_ref[...]


def _mlstm_seq(z_ml, small, bias_vec, nw, chunk, nb):
    b, l, _ = z_ml.shape
    return pl.pallas_call(
        functools.partial(_ml_body, chunk=chunk),
        grid=(b // nb, l // chunk),
        in_specs=[pl.BlockSpec((nb, chunk, 1024), lambda i, j: (i, j, 0)),
                  pl.BlockSpec((nb, chunk, LANES), lambda i, j: (i, j, 0)),
                  pl.BlockSpec((1, LANES), lambda i, j: (0, 0)),
                  pl.BlockSpec((1, D_HEADS), lambda i, j: (0, 0))],
        out_specs=[pl.BlockSpec((nb, chunk, D_HEADS), lambda i, j: (i, j, 0)),
                   pl.BlockSpec((nb, N_HEADS, HEAD_DIM, HEAD_DIM), lambda i, j: (i, 0, 0, 0)),
                   pl.BlockSpec((nb, N_HEADS, HEAD_DIM), lambda i, j: (i, 0, 0)),
                   pl.BlockSpec((nb, 1, LANES), lambda i, j: (i, 0, 0))],
        out_shape=[jax.ShapeDtypeStruct((b, l, D_HEADS), F32),
                   jax.ShapeDtypeStruct((b, N_HEADS, HEAD_DIM, HEAD_DIM), F32),
                   jax.ShapeDtypeStruct((b, N_HEADS, HEAD_DIM), F32),
                   jax.ShapeDtypeStruct((b, 1, LANES), F32)],
        compiler_params=_cparams("parallel", "arbitrary"),
        name="mlstm_seq",
    )(z_ml, small, bias_vec, nw.reshape(1, -1))


def _split_bf16(x):
    hi = _bf(x)
    return hi, _bf(x - hi.astype(F32))


def _dot3(a, b):
    (ah, al), (bh, bl) = a, b
    return _dot(ah, bh) + (_dot(ah, bl) + _dot(al, bh))


def _l2norm(x):
    return x * lax.rsqrt(jnp.sum(x * x, axis=-1, keepdims=True) + EPS)


def _gdn_body(z_ref, sm_ref, cw_ref, dtb_ref, alog_ref, nw_ref, y_ref, s_ref, buf_ref, *, chunk):
    @pl.when(pl.program_id(1) == 0)
    def _():
        s_ref[...] = jnp.zeros_like(s_ref)
        buf_ref[:, 0:SUBLANES, :] = jnp.zeros((buf_ref.shape[0], SUBLANES, 3 * D_HEADS), F32)

    nb = z_ref.shape[0]
    incl = _tri_incl(chunk)
    strict = lax.broadcasted_iota(jnp.int32, (chunk, chunk), 0) > lax.broadcasted_iota(jnp.int32, (chunk, chunk), 1)
    eye = (lax.broadcasted_iota(jnp.int32, (chunk, chunk), 0)
           == lax.broadcasted_iota(jnp.int32, (chunk, chunk), 1)).astype(F32)
    tri = incl.astype(F32)
    gates, acts = [], []
    for bb in range(nb):
        buf_ref[bb, SUBLANES:SUBLANES + chunk, :] = z_ref[bb, :, 0:768]
        conv = None
        for j in range(GDN_CONV):
            term = buf_ref[bb, pl.ds(SUBLANES - (GDN_CONV - 1) + j, chunk), :] * cw_ref[j:j + 1, :]
            conv = term if conv is None else conv + term
        buf_ref[bb, 0:SUBLANES, :] = buf_ref[bb, chunk:chunk + SUBLANES, :]
        acts.append(_silu(conv))
        sm = sm_ref[bb]
        g_all = -jnp.exp(alog_ref[...]) * _softplus(sm + dtb_ref[...])
        gcum = _dot(tri, g_all, precision=HI)
        gates.append((gcum, _lanes_to_rows(gcum), _sigmoid(sm)))

    probs = [(bb, h) for bb in range(nb) for h in range(N_HEADS)]
    pre = []
    for bb, h in probs:
        gcum, gcum_t, beta_all = gates[bb]
        g_col = gcum[:, SM_GA + h:SM_GA + h + 1]
        g_row = gcum_t[SM_GA + h:SM_GA + h + 1, :]
        beta = beta_all[:, SM_GB + h:SM_GB + h + 1]
        decay = jnp.where(incl, jnp.exp(jnp.where(incl, g_col - g_row, 0.0)), 0.0)
        act = acts[bb]
        qh = _l2norm(_head(act[:, 0:256], h)) * SCALE
        kh = _l2norm(_head(act[:, 256:512], h))
        vh = _head(act[:, 512:768], h)
        eg = jnp.exp(g_col)
        pre.append(dict(g_col=g_col, beta=beta, decay=decay, qh=qh, kh=kh, kb=_bf(kh), eg=eg,
                        rhs=jnp.concatenate([vh * beta, kh * (beta * eg)], axis=1)))
    a_l = [jnp.where(strict, d["beta"] * _dot_nt(d["kb"], d["kb"]) * d["decay"], 0.0) for d in pre]
    p_l = [eye - a for a in a_l]
    pw_s = [_split_bf16(a) for a in a_l]
    lvl = 2
    while lvl < chunk:
        pw_s = [_split_bf16(_dot3(x, x)) for x in pw_s]
        p_l = [p + _dot3(_split_bf16(p), x) for p, x in zip(p_l, pw_s)]
        lvl *= 2
    uw_l = [_dot3(_split_bf16(p), _split_bf16(d["rhs"])) for p, d in zip(p_l, pre)]
    st_l = [s_ref[bb, h] for bb, h in probs]
    stb_l = [_bf(st) for st in st_l]
    delta_l = [uw[:, 0:HEAD_DIM] - _dot(_bf(uw[:, HEAD_DIM:]), stb) for uw, stb in zip(uw_l, stb_l)]
    qk_l = [_dot_nt(_bf(d["qh"]), d["kb"]) * d["decay"] for d in pre]
    o_l = [_dot(_bf(qk), _bf(delta)) + _dot(_bf(d["qh"] * d["eg"]), stb)
           for qk, delta, d, stb in zip(qk_l, delta_l, pre, stb_l)]
    for (bb, h), d, st, delta in zip(probs, pre, st_l, delta_l):
        g_end = d["g_col"][chunk - 1:chunk, :]
        w_end = jnp.exp(g_end - d["g_col"])
        s_ref[bb, h] = jnp.exp(g_end) * st + _dot_tn(_bf(d["kh"] * w_end), _bf(delta))
    for bb in range(nb):
        outs = [_head_rms(o_l[bb * N_HEADS + h]) for h in range(N_HEADS)]
        y_ref[bb] = jnp.concatenate(outs, axis=1) * nw_ref[...] * _silu(z_ref[bb, :, 768:1024])


def _gdn_seq(z_gdn, small, conv_w, dtb_vec, alog_vec, nw, chunk, nb):
    b, l, _ = z_gdn.shape
    return pl.pallas_call(
        functools.partial(_gdn_body, chunk=chunk),
        grid=(b // nb, l // chunk),
        in_specs=[pl.BlockSpec((nb, chunk, 1024), lambda i, j: (i, j, 0)),
                  pl.BlockSpec((nb, chunk, LANES), lambda i, j: (i, j, 0)),
                  pl.BlockSpec((GDN_CONV, 3 * D_HEADS), lambda i, j: (0, 0)),
                  pl.BlockSpec((1, LANES), lambda i, j: (0, 0)),
                  pl.BlockSpec((1, LANES), lambda i, j: (0, 0)),
                  pl.BlockSpec((1, D_HEADS), lambda i, j: (0, 0))],
        out_specs=[pl.BlockSpec((nb, chunk, D_HEADS), lambda i, j: (i, j, 0)),
                   pl.BlockSpec((nb, N_HEADS, HEAD_DIM, HEAD_DIM), lambda i, j: (i, 0, 0, 0))],
        out_shape=[jax.ShapeDtypeStruct((b, l, D_HEADS), F32),
                   jax.ShapeDtypeStruct((b, N_HEADS, HEAD_DIM, HEAD_DIM), F32)],
        scratch_shapes=[pltpu.VMEM((nb, SUBLANES + chunk, 3 * D_HEADS), F32)],
        compiler_params=_cparams("parallel", "arbitrary"),
        name="gdn_seq",
    )(z_gdn, small, conv_w, dtb_vec, alog_vec, nw.reshape(1, -1))


def _row_to_col(row):
    n = row.shape[1]
    eye = lax.broadcasted_iota(jnp.int32, (n, n), 0) == lax.broadcasted_iota(jnp.int32, (n, n), 1)
    return jnp.sum(jnp.where(eye, jnp.broadcast_to(row, (n, n)), 0.0), axis=1, keepdims=True)


def _vec_mat(col, mat):
    return jnp.sum(col * mat, axis=0, keepdims=True)


def _step_body(zr_ref, zm_ref, zg_ref, sm_ref, cos_ref, sin_ref, mlb_ref, dtb_ref, alog_ref, cw_ref,
               nwr_ref, nwm_ref, nwg_ref, sr_ref, mc_ref, mn_ref, mm_ref, gs_ref, gc_ref,
               yr_ref, ym_ref, yg_ref, sr_o, mc_o, mn_o, mm_o, gs_o, gc_o):
    sm = sm_ref[0]
    lane = lax.broadcasted_iota(jnp.int32, (1, LANES), 1)
    zr = zr_ref[0]
    cos, sin = cos_ref[...], sin_ref[...]
    q = _rope256(zr[:, 0:256], cos, sin)
    k = _rope256(zr[:, 256:512], cos, sin) * SCALE
    v = zr[:, 512:768]
    outs = []
    for h in range(N_HEADS):
        gam = math.exp(_ret_log_decay(h))
        qh, kh, vh = _head(q, h), _head(k, h), _head(v, h)
        st = sr_ref[0, h]
        o = jnp.sum(qh * kh, axis=1, keepdims=True) * vh + _vec_mat(_row_to_col(qh) * gam, st)
        sr_o[0, h] = st * gam + _row_to_col(kh) * vh
        outs.append(_head_rms(o))
    yr_ref[0] = jnp.concatenate(outs, axis=1) * nwr_ref[...] * _silu(zr[:, 768:1024])
    zm = zm_ref[0]
    q, k, v, og = zm[:, 0:256], zm[:, 256:512] * SCALE, zm[:, 512:768], zm[:, 768:1024]
    pre = sm + mlb_ref[...]
    logf = jnp.minimum(pre, 0.0) - jnp.log(1.0 + jnp.exp(-jnp.abs(pre)))
    m_all = mm_ref[0]
    m_out = jnp.zeros((1, LANES), F32)
    outs = []
    for h in range(N_HEADS):
        i_g = pre[:, SM_MI + h:SM_MI + h + 1]
        b_g = logf[:, SM_MF + h:SM_MF + h + 1]
        inter = b_g + m_all[:, h:h + 1]
        m_t = jnp.maximum(inter, i_g)
        w = jnp.exp(i_g - m_t)
        g_in = jnp.exp(inter - m_t)
        qh, kh, vh = _head(q, h), _head(k, h), _head(v, h)
        cm, nv = mc_ref[0, h], mn_ref[0, h:h + 1, :]
        qk = jnp.sum(qh * kh, axis=1, keepdims=True) * w
        num = qk * vh + g_in * _vec_mat(_row_to_col(qh), cm)
        den = qk + g_in * jnp.sum(qh * nv, axis=1, keepdims=True)
        hh = num / jnp.maximum(jnp.abs(den), jnp.exp(-m_t))
        mc_o[0, h] = g_in * cm + _row_to_col(kh * w) * vh
        mn_o[0, h:h + 1, :] = g_in * nv + w * kh
        m_out = jnp.where(lane == h, m_t, m_out)
        outs.append(_head_rms(_sigmoid(_head(og, h)) * hh))
    mm_o[0] = m_out
    ym_ref[0] = jnp.concatenate(outs, axis=1) * nwm_ref[...]
    zg = zg_ref[0]
    x = zg[:, 0:768]
    buf = gc_ref[0]
    conv = x * cw_ref[GDN_CONV - 1:GDN_CONV, :]
    for j in range(GDN_CONV - 1):
        conv = conv + buf[j:j + 1, :] * cw_ref[j:j + 1, :]
    gc_o[0, 0:GDN_CONV - 2, :] = buf[1:GDN_CONV - 1, :]
    gc_o[0, GDN_CONV - 2:GDN_CONV - 1, :] = x
    act = _silu(conv)
    q, k, v = act[:, 0:256], act[:, 256:512], act[:, 512:768]
    g_all = -jnp.exp(alog_ref[...]) * _softplus(sm + dtb_ref[...])
    beta_all = _sigmoid(sm)
    outs = []
    for h in range(N_HEADS):
        eg = jnp.exp(g_all[:, SM_GA + h:SM_GA + h + 1])
        beta = beta_all[:, SM_GB + h:SM_GB + h + 1]
        qh = _l2norm(_head(q, h)) * SCALE
        kh = _l2norm(_head(k, h))
        vh = _head(v, h)
        st = gs_ref[0, h]
        delta = vh * beta - _vec_mat(_row_to_col(kh * (beta * eg)), st)
        o = jnp.sum(qh * kh, axis=1, keepdims=True) * delta + _vec_mat(_row_to_col(qh * eg), st)
        gs_o[0, h] = eg * st + _row_to_col(kh) * delta
        outs.append(_head_rms(o))
    yg_ref[0] = jnp.concatenate(outs, axis=1) * nwg_ref[...] * _silu(zg[:, 768:1024])


def _mixers_step(z_ret, z_ml, z_gdn, small, cos, sin, mlb, dtb, alog, conv_w, nwr, nwm, nwg,
                 s_ret, ml_c, ml_n, ml_m, gdn_s, gdn_conv):
    b = z_ret.shape[0]
    row = lambda n: pl.BlockSpec((1, 1, n), lambda i: (i, 0, 0))
    vec = lambda n: pl.BlockSpec((1, n), lambda i: (0, 0))
    mat = pl.BlockSpec((1, N_HEADS, HEAD_DIM, HEAD_DIM), lambda i: (i, 0, 0, 0))
    nsp = pl.BlockSpec((1, N_HEADS, HEAD_DIM), lambda i: (i, 0, 0))
    csp = pl.BlockSpec((1, GDN_CONV - 1, 3 * D_HEADS), lambda i: (i, 0, 0))
    mshape = jax.ShapeDtypeStruct((b, N_HEADS, HEAD_DIM, HEAD_DIM), F32)
    yshape = jax.ShapeDtypeStruct((b, 1, D_HEADS), F32)
    return pl.pallas_call(
        _step_body,
        grid=(b,),
        in_specs=[row(1024), row(1024), row(1024), row(LANES), vec(LANES), vec(LANES), vec(LANES), vec(LANES), vec(LANES),
                  pl.BlockSpec((GDN_CONV, 3 * D_HEADS), lambda i: (0, 0)), vec(D_HEADS), vec(D_HEADS), vec(D_HEADS),
                  mat, mat, nsp, row(LANES), mat, csp],
        out_specs=[row(D_HEADS), row(D_HEADS), row(D_HEADS), mat, mat, nsp, row(LANES), mat, csp],
        out_shape=[yshape, yshape, yshape, mshape, mshape,
                   jax.ShapeDtypeStruct((b, N_HEADS, HEAD_DIM), F32), jax.ShapeDtypeStruct((b, 1, LANES), F32),
                   mshape, jax.ShapeDtypeStruct((b, GDN_CONV - 1, 3 * D_HEADS), F32)],
        compiler_params=_cparams("parallel"),
        name="mixers_step",
    )(z_ret, z_ml, z_gdn, small, cos, sin, mlb, dtb, alog, conv_w,
      nwr.reshape(1, -1), nwm.reshape(1, -1), nwg.reshape(1, -1), s_ret, ml_c, ml_n, ml_m, gdn_s, gdn_conv)


def _t5_bucket_np(n):
    exact = T5_BUCKETS // 2
    n = np.maximum(np.asarray(n, np.int64), 0)
    x = np.maximum(n, 1).astype(np.float32) / np.float32(exact)
    large = exact + (np.log(x) / np.float32(math.log(T5_MAX_DIST / exact)) * np.float32(T5_BUCKETS - exact)).astype(np.int32)
    return np.where(n < exact, n, np.minimum(large, T5_BUCKETS - 1)).astype(np.int32)


def _bucket_thresholds():
    b = _t5_bucket_np(np.arange(4 * T5_MAX_DIST))
    return [int(np.argmax(b >= k)) for k in range(T5_BUCKETS // 2 + 1, T5_BUCKETS)]


def _bias_table_body(rbt_ref, oh_ref, o_ref):
    o_ref[...] = _dot(rbt_ref[...], oh_ref[...].astype(F32), precision=HI)


def _bias_tables(rel_bias, dists):
    n = len(dists)
    npad = -(-n // 512) * 512
    onehot = np.zeros((T5_BUCKETS, npad), np.float32)
    onehot[_t5_bucket_np(dists), np.arange(n)] = 1.0
    rbt = jnp.pad(rel_bias.T, ((0, SUBLANES - N_HEADS), (0, 0)))
    tn = 512
    out = pl.pallas_call(
        _bias_table_body,
        grid=(npad // tn,),
        in_specs=[pl.BlockSpec((SUBLANES, T5_BUCKETS), lambda i: (0, 0)),
                  pl.BlockSpec((T5_BUCKETS, tn), lambda i: (0, i))],
        out_specs=pl.BlockSpec((SUBLANES, tn), lambda i: (0, i)),
        out_shape=jax.ShapeDtypeStruct((SUBLANES, npad), F32),
        name="bias_tables",
    )(rbt, jnp.asarray(onehot, BF16))
    return out[:, :n]


def _prep_cmp_w(cmp_w, cmp_pe):
    eye2 = jnp.eye(2, dtype=F32)

    def half(w):
        full = jnp.einsum('cjde,cf->jcdfe', w, eye2)
        return full.reshape(CMP_STRIDE * 2 * HEAD_DIM, 2 * HEAD_DIM)

    wab = jnp.concatenate([half(cmp_w[:, :CMP_STRIDE]), half(cmp_w[:, CMP_STRIDE:])], axis=1)
    pe = jnp.concatenate([cmp_pe[:CMP_STRIDE].reshape(1, -1), cmp_pe[CMP_STRIDE:].reshape(1, -1)], axis=1)
    return wab, jnp.pad(pe, ((0, SUBLANES - 1), (0, 0)))


def _pe_term(pe_ref, wab_ref):
    kin = CMP_STRIDE * 2 * HEAD_DIM
    t = (_dot(pe_ref[:, 0:kin], wab_ref[:, 0:LANES].astype(F32), precision=HI)
         + _dot(pe_ref[:, kin:2 * kin], wab_ref[:, LANES:2 * LANES].astype(F32), precision=HI))
    return t[0:1, :]


def _combine_cmp(parts, pe_term):
    rows = parts.shape[0]
    nxt = pltpu.roll(parts[:, LANES:2 * LANES], rows - 1, 0)
    r = lax.broadcasted_iota(jnp.int32, (rows, LANES), 0)
    return jnp.where(r < rows - 1, parts[:, 0:LANES] + nxt + pe_term, 0.0)


def _compress_seq_body(r_ref, wab_ref, pe_ref, o_ref):
    parts = _dot(_bf(r_ref[0]), _bf(wab_ref[...]))
    o_ref[0] = _combine_cmp(parts, _pe_term(pe_ref, wab_ref))


def _compress_seq(rows, wab, pe):
    b, n16, kin = rows.shape
    return pl.pallas_call(
        _compress_seq_body,
        grid=(b,),
        in_specs=[pl.BlockSpec((1, n16, kin), lambda i: (i, 0, 0)),
                  pl.BlockSpec(wab.shape, lambda i: (0, 0)),
                  pl.BlockSpec(pe.shape, lambda i: (0, 0))],
        out_specs=pl.BlockSpec((1, n16, LANES), lambda i: (i, 0, 0)),
        out_shape=jax.ShapeDtypeStruct((b, n16, LANES), F32),
        compiler_params=_cparams("parallel"),
        name="nsa_compress_seq",
    )(rows, wab, pe)


def _bias_from_dist(dist, rb_ref, h):
    n = jnp.maximum(dist, 0)
    exact = T5_BUCKETS // 2
    out = jnp.full(dist.shape, rb_ref[exact, h], F32)
    for k, thr in enumerate(_bucket_thresholds()):
        out = jnp.where(n >= thr, rb_ref[exact + 1 + k, h], out)
    for j in range(exact):
        out = jnp.where(n == j, rb_ref[j, h], out)
    return out


def _softmax_rows(s, valid):
    s = jnp.where(valid, s, NEG_INF)
    m = jnp.max(s, axis=1, keepdims=True)
    m = jnp.where(m == NEG_INF, 0.0, m)
    p = jnp.exp(s - m)
    den = jnp.sum(p, axis=1, keepdims=True)
    return p / jnp.where(den > 0, den, 1.0)


def _top_blocks(score, k):
    lane = lax.broadcasted_iota(jnp.int32, score.shape, 1)
    width = score.shape[1]
    sel = jnp.zeros(score.shape, jnp.bool_)
    for _ in range(k):
        m = jnp.max(score, axis=1, keepdims=True)
        idx = jnp.min(jnp.where(score == m, lane, width), axis=1, keepdims=True)
        pick = (lane == idx) & (m > NEG_INF)
        sel = sel | pick
        score = jnp.where(pick, NEG_INF, score)
    return sel


def _tile_mask(ok):
    return jnp.concatenate([jnp.where(ok, 1.0, 0.0)] * N_HEADS, axis=0) > 0.5


def _stack_heads(x):
    return jnp.concatenate([_head(x, h) for h in range(N_HEADS)], axis=0)


def _toeplitz(tab_ref, h, r0, nrow, rows):
    strip = tab_ref[h, pl.ds(r0, nrow), :]
    flat = jnp.concatenate([strip[r:r + 1, :] for r in range(nrow)], axis=1)
    rolled = pltpu.roll(jnp.broadcast_to(flat, (rows, nrow * LANES)), 0, 1, stride=1, stride_axis=0)
    return rolled[:, LANES:]


SLC_KC = 512


MASK_BIG = 1e30


def _top_blocks_t(score, k):
    row = lax.broadcasted_iota(jnp.int32, score.shape, 0)
    n = score.shape[0]
    sel = jnp.zeros(score.shape, jnp.bool_)
    for _ in range(k):
        m = jnp.max(score, axis=0, keepdims=True)
        idx = jnp.min(jnp.where(score == m, row, n), axis=0, keepdims=True)
        pick = (row == idx) & (m > NEG_INF)
        sel = sel | pick
        score = jnp.where(pick, NEG_INF, score)
    return sel


def _bucket_index(dist):
    n = jnp.maximum(dist, 0)
    exact = T5_BUCKETS // 2
    big = jnp.full(dist.shape, exact, jnp.int32)
    for thr in _bucket_thresholds():
        big = big + (n >= thr).astype(jnp.int32)
    return jnp.where(n < exact, n, big)


def _gather_bias(rbt_ref, h, bucket):
    rows, width = bucket.shape
    piece = min(width, LANES)
    tab = jnp.broadcast_to(rbt_ref[h:h + 1, 0:piece], (rows, piece))
    parts = [jnp.take_along_axis(tab, bucket[:, s:s + piece], axis=1) for s in range(0, width, piece)]
    return parts[0] if len(parts) == 1 else jnp.concatenate(parts, axis=1)


def _tile_rows(x, n=N_HEADS):
    return jnp.concatenate([x] * n, axis=0)


def _near_chunks():
    return (_bucket_thresholds()[-1] + SLC_KC - 2) // SLC_KC + 1


def _bias_tile(tab_ref, base, ncol, p0, lo, hi):
    r0 = (p0 - base) // LANES - 1
    bias = jnp.concatenate([_toeplitz(tab_ref, h, r0, ncol // LANES + 1, Q_BLOCK) for h in range(N_HEADS)], axis=0)
    dist = (base + lax.broadcasted_iota(jnp.int32, (Q_BLOCK, ncol), 0)
            - lax.broadcasted_iota(jnp.int32, (Q_BLOCK, ncol), 1))
    return bias + _tile_rows(jnp.where((dist >= lo) & (dist < hi), 0.0, -MASK_BIG))


def _nsa_seq_body(rb_ref, q_ref, sm_ref, kx_ref, vs_ref, kvw_ref, kvc_ref, tab_ref, rbt_ref, ovlt_ref, y_ref,
                  stile_ref, wtile_ref, *, seq, p0):
    qb = Q_BLOCK
    rows = N_HEADS * qb
    n16 = seq // CMP_STRIDE
    nblk = seq // SLC_BLOCK
    bi = pl.program_id(1)
    q0 = bi * qb

    @pl.when((pl.program_id(0) == 0) & (bi == 0))
    def _():
        per_chunk = SLC_KC // qb

        def slc_tile(t, carry):
            base = (t // _near_chunks()) * qb + (t % _near_chunks()) * SLC_KC
            stile_ref[t] = _bias_tile(tab_ref, base, SLC_KC, p0, 0, seq + SLC_KC)
            return carry

        def win_tile(t, carry):
            wtile_ref[t] = _bias_tile(tab_ref, t * qb, WINDOW + qb, p0, 0, WINDOW)
            return carry

        lax.fori_loop(0, per_chunk * _near_chunks(), slc_tile, 0)
        lax.fori_loop(0, WINDOW // qb + 1, win_tile, 0)
    qs = _stack_heads(q_ref[0]) * SCALE
    qsb = _bf(qs)
    qpos = q0 + lax.broadcasted_iota(jnp.int32, (qb, 1), 0)
    head_of_row = lax.broadcasted_iota(jnp.int32, (rows, 1), 0) // qb

    kvc = kvc_ref[0]
    s_c = _dot_nt(qsb, _bf(kvc[:, 0:HEAD_DIM]))
    band = WINDOW + qb
    sw = pl.multiple_of(jnp.maximum(q0 - WINDOW, 0), qb)
    kw = kvw_ref[0, pl.ds(sw, band), 0:HEAD_DIM]
    vw = kvw_ref[0, pl.ds(sw, band), HEAD_DIM:2 * HEAD_DIM]
    s_w = _dot_nt(qsb, kw)

    cmp_end = lax.broadcasted_iota(jnp.int32, (qb, n16), 1) * CMP_STRIDE + (CMP_LEN - 1)
    dist_c = qpos - cmp_end
    bucket_c = _bucket_index(dist_c)
    bias_c = jnp.concatenate([_gather_bias(rbt_ref, h, bucket_c) for h in range(N_HEADS)], axis=0)
    p_c = _softmax_rows(s_c + bias_c, _tile_mask(dist_c >= 0))
    p_cb = _bf(p_c)
    o_c = _dot(p_cb, _bf(kvc[:, HEAD_DIM:2 * HEAD_DIM]))
    p_r = p_cb.astype(F32)
    p_sum = p_r[0:qb] + p_r[qb:2 * qb] + p_r[2 * qb:3 * qb] + p_r[3 * qb:4 * qb]
    imp_t = _dot_nt(ovlt_ref[...], p_sum, precision=HI)

    base_w = q0 - sw
    sb_w = s_w + wtile_ref[base_w // qb]
    e_w = jnp.exp(sb_w - jnp.max(sb_w, axis=1, keepdims=True))
    o_w = _dot(_bf(e_w / jnp.sum(e_w, axis=1, keepdims=True)), vw)

    qlane = q0 + lax.broadcasted_iota(jnp.int32, (1, qb), 1)
    cur = qlane // SLC_BLOCK
    rowb = lax.broadcasted_iota(jnp.int32, (nblk, qb), 0)
    forced = (rowb == 0) | (rowb == cur) | (rowb == cur - 1)
    score = jnp.where(forced, jnp.inf, imp_t)
    score = jnp.where(rowb > cur, NEG_INF, score)
    sel = jnp.where(_top_blocks_t(score, SLC_TOPK), 1.0, 0.0).T
    negm = _bf(_tile_rows((sel - 1.0) * MASK_BIG))
    q_ext = jnp.concatenate([qsb, negm], axis=1)

    thr_far = _bucket_thresholds()[-1] + SLC_KC - 1
    n_chunks = q0 // SLC_KC + 1
    n_far = jnp.maximum(q0 - thr_far + SLC_KC, 0) // SLC_KC
    far_bias = jnp.zeros((rows, 1), F32)
    for h in range(N_HEADS):
        far_bias = jnp.where(head_of_row == h, rb_ref[T5_BUCKETS - 1, h], far_bias)
    n_near = _near_chunks()

    def scores(c):
        k0 = pl.multiple_of(c * SLC_KC, SLC_KC)
        kx = kx_ref[0, pl.ds(k0, SLC_KC), :]
        return q0 - k0, _dot_nt(q_ext, kx), vs_ref[0, pl.ds(k0, SLC_KC), :]

    def online(carry, s, vs):
        m, l, acc = carry
        m_new = jnp.maximum(m, jnp.max(s, axis=1, keepdims=True))
        alpha = jnp.exp(m - m_new)
        p = jnp.exp(s - m_new)
        return m_new, alpha * l + jnp.sum(p, axis=1, keepdims=True), alpha * acc + _dot(_bf(p), vs)

    def far_step(c, carry):
        _, s, vs = scores(c)
        return online(carry, s + far_bias, vs)

    def near_step(c, carry):
        base, s, vs = scores(c)
        tile = ((q0 % SLC_KC) // qb) * n_near + base // SLC_KC
        return online(carry, s + stile_ref[tile], vs)

    init = (jnp.full((rows, 1), -MASK_BIG / 10, F32), jnp.zeros((rows, 1), F32), jnp.zeros((rows, HEAD_DIM), F32))
    carry = lax.fori_loop(0, n_far, far_step, init)
    _, l_s, acc_s = lax.fori_loop(n_far, n_chunks, near_step, carry)
    o_s = acc_s / jnp.where(l_s > 0, l_s, 1.0)

    gates = _sigmoid(sm_ref[0])
    outs = []
    for h in range(N_HEADS):
        rs = slice(h * qb, (h + 1) * qb)
        g = lambda t: gates[:, SM_NG + 3 * h + t:SM_NG + 3 * h + t + 1]
        outs.append(g(0) * o_c[rs] + g(1) * o_s[rs] + g(2) * o_w[rs])
    y_ref[0] = jnp.concatenate(outs, axis=1)


def _nsa_seq(rel_bias, nq, small, nkv, kvc, tab, seq_p0):
    b, l, _ = nq.shape
    n16 = l // CMP_STRIDE
    nblk = l // SLC_BLOCK
    cmp_start = np.arange(n16)[None, :] * CMP_STRIDE
    slc_start = np.arange(nblk)[:, None] * SLC_BLOCK
    ovlt = ((cmp_start < slc_start + SLC_BLOCK) & (cmp_start + CMP_LEN > slc_start)
            & (np.arange(n16)[None, :] < n16 - 1)).astype(np.float32)
    block_id = (np.arange(l)[:, None] // SLC_BLOCK == np.arange(nblk)[None, :]).astype(np.float32)
    kx = _bf(jnp.concatenate([nkv[..., 2 * HEAD_DIM:3 * HEAD_DIM],
                              jnp.broadcast_to(jnp.asarray(block_id), (b, l, nblk))], axis=-1))
    vs = _bf(nkv[..., 3 * HEAD_DIM:4 * HEAD_DIM])
    kvw = _bf(nkv[..., 4 * HEAD_DIM:6 * HEAD_DIM])
    rbt = jnp.pad(rel_bias.T, ((0, SUBLANES - N_HEADS), (0, LANES - T5_BUCKETS)))
    nrows = tab.shape[1]
    const = lambda shape: pl.BlockSpec(shape, lambda i, j: (0,) * len(shape))
    seq_spec = lambda n: pl.BlockSpec((1, l, n), lambda i, j: (i, 0, 0))
    return pl.pallas_call(
        functools.partial(_nsa_seq_body, seq=l, p0=seq_p0),
        grid=(b, l // Q_BLOCK),
        in_specs=[pl.BlockSpec(memory_space=pltpu.SMEM),
                  pl.BlockSpec((1, Q_BLOCK, D_HEADS), lambda i, j: (i, j, 0)),
                  pl.BlockSpec((1, Q_BLOCK, LANES), lambda i, j: (i, j, 0)),
                  seq_spec(HEAD_DIM + nblk), seq_spec(HEAD_DIM), seq_spec(2 * HEAD_DIM),
                  pl.BlockSpec((1, n16, LANES), lambda i, j: (i, 0, 0)),
                  const((SUBLANES, nrows, LANES)), const((SUBLANES, LANES)), const((nblk, n16))],
        out_specs=pl.BlockSpec((1, Q_BLOCK, D_HEADS), lambda i, j: (i, j, 0)),
        out_shape=jax.ShapeDtypeStruct((b, l, D_HEADS), F32),
        scratch_shapes=[pltpu.VMEM(((SLC_KC // Q_BLOCK) * _near_chunks(), N_HEADS * Q_BLOCK, SLC_KC), F32),
                        pltpu.VMEM((WINDOW // Q_BLOCK + 1, N_HEADS * Q_BLOCK, WINDOW + Q_BLOCK), F32)],
        compiler_params=_cparams("arbitrary", "arbitrary"),
        name="nsa_seq",
    )(rel_bias, nq, small, kx, vs, kvw, kvc, tab, rbt, jnp.asarray(ovlt))


def _nsa_tab_seq_dists(seq):
    max_base = max(seq, SLC_KC + (_near_chunks() - 1) * SLC_KC)
    p0 = max_base + Q_BLOCK
    length = p0 + WINDOW + Q_BLOCK + LANES
    return p0, np.maximum(p0 - np.arange(length), 0)


PAGE_GROUP = 16
PAGES_PER_STEP = 32


def _pad_rows(x, rows=SUBLANES):
    return jnp.concatenate([x, jnp.zeros((rows - x.shape[0], x.shape[1]), x.dtype)], axis=0)


def _nsa_select_body(pt_ref, *refs, past, n_steps):
    del pt_ref
    pages = refs[:PAGES_PER_STEP]
    wab_ref, pe_ref, q_ref, tabc_ref, ovlt_ref, oc_ref, top_ref, parts_ref, rows_ref = refs[PAGES_PER_STEP:]
    step = pl.program_id(1)
    blocks_per_page = PAGE_SIZE // CMP_STRIDE
    group_rows = PAGE_GROUP * blocks_per_page
    n_groups = PAGES_PER_STEP // PAGE_GROUP
    for p, pg in enumerate(pages):
        g, pp = divmod(p, PAGE_GROUP)
        for c in range(2):
            rows_ref[g, pp * PAGE_SIZE:(pp + 1) * PAGE_SIZE, c * HEAD_DIM:(c + 1) * HEAD_DIM] = pg[c].T
    for g in range(n_groups):
        acc = jnp.zeros((group_rows, 2 * LANES), F32)
        for j in range(CMP_STRIDE):
            xj = rows_ref[g, pl.ds(j, group_rows, stride=CMP_STRIDE), :]
            acc = acc + _dot(_bf(xj), wab_ref[j * LANES:(j + 1) * LANES, :])
        row0 = pl.multiple_of((step * n_groups + g) * group_rows, group_rows)
        parts_ref[pl.ds(row0, group_rows), :] = acc

    @pl.when(step == n_steps - 1)
    def _():
        n16 = past // CMP_STRIDE
        n_cmp = n16 - 1
        cur = past // SLC_BLOCK
        pe_term = _pe_term(pe_ref, wab_ref)
        kvc = _combine_cmp(parts_ref[...], pe_term)
        qs = _pad_rows(_stack_heads(q_ref[0]) * SCALE)
        s_c = _dot_nt(_bf(qs), _bf(kvc[:, 0:HEAD_DIM])) + tabc_ref[...]
        lane = lax.broadcasted_iota(jnp.int32, (SUBLANES, n16), 1)
        p_c = _softmax_rows(s_c, lane < n_cmp)
        p_cb = _bf(p_c)
        o_c = _dot(p_cb, _bf(kvc[:, HEAD_DIM:2 * HEAD_DIM]))
        oc_ref[0] = jnp.concatenate([o_c[h:h + 1, :] for h in range(N_HEADS)], axis=1)
        imp_h = _dot_nt(ovlt_ref[...], p_cb)
        imp = imp_h[:, 0:1] + imp_h[:, 1:2] + imp_h[:, 2:3] + imp_h[:, 3:4]
        nrow = imp.shape[0]
        rowb = lax.broadcasted_iota(jnp.int32, (nrow, 1), 0)
        forced = (rowb == 0) | (rowb == cur) | (rowb == cur - 1)
        score = jnp.where(forced, jnp.inf, imp)
        score = jnp.where(rowb > cur, NEG_INF, score)
        out_lane = lax.broadcasted_iota(jnp.int32, (1, LANES), 1)
        top = jnp.zeros((1, LANES), jnp.int32)
        for r in range(SLC_TOPK):
            m = jnp.max(score, axis=0, keepdims=True)
            idx = jnp.min(jnp.where(score == m, rowb, nrow), axis=0, keepdims=True)
            top = jnp.where(out_lane == r, idx, top)
            score = jnp.where(rowb == idx, NEG_INF, score)
        top_ref[0] = top


def _nsa_select(layer, page_table, cache_t, wab_bf, pe, nq, tabc, past):
    b, n_pages = page_table.shape
    n_steps = n_pages // PAGES_PER_STEP
    n16 = past // CMP_STRIDE
    n_slc = past // SLC_BLOCK + 1
    nrow = -(-n_slc // SUBLANES) * SUBLANES
    cmp_start = np.arange(n16)[None, :] * CMP_STRIDE
    slc_start = np.arange(nrow)[:, None] * SLC_BLOCK
    ovlt = ((cmp_start < slc_start + SLC_BLOCK) & (cmp_start + CMP_LEN > slc_start)
            & (np.arange(n16)[None, :] < n16 - 1) & (np.arange(nrow)[:, None] < n_slc)).astype(np.float32)

    n_pool = cache_t.shape[1]

    def page_spec(p):
        def imap(i, s, pt):
            pg = pt[jnp.minimum(i, b - 1), jnp.minimum(s, n_steps - 1) * PAGES_PER_STEP + p]
            return (layer, jnp.clip(pg, 0, n_pool - 1), 0, 0, 0)
        return pl.BlockSpec((None, None, 2, HEAD_DIM, PAGE_SIZE), imap)

    const = lambda shape: pl.BlockSpec(shape, lambda i, s, pt: (0,) * len(shape))
    grid_spec = pltpu.PrefetchScalarGridSpec(
        num_scalar_prefetch=1,
        grid=(b, n_steps),
        in_specs=[page_spec(p) for p in range(PAGES_PER_STEP)]
        + [const(wab_bf.shape), const(pe.shape), pl.BlockSpec((1, 1, D_HEADS), lambda i, s, pt: (i, 0, 0)),
           const(tabc.shape), const(ovlt.shape)],
        out_specs=[pl.BlockSpec((1, 1, D_HEADS), lambda i, s, pt: (i, 0, 0)),
                   pl.BlockSpec((1, 1, LANES), lambda i, s, pt: (i, 0, 0))],
        scratch_shapes=[pltpu.VMEM((n16, 2 * LANES), F32),
                        pltpu.VMEM((PAGES_PER_STEP // PAGE_GROUP, PAGE_GROUP * PAGE_SIZE, 2 * HEAD_DIM), F32)],
    )
    return pl.pallas_call(
        functools.partial(_nsa_select_body, past=past, n_steps=n_steps),
        grid_spec=grid_spec,
        out_shape=[jax.ShapeDtypeStruct((b, 1, D_HEADS), F32), jax.ShapeDtypeStruct((b, 1, LANES), jnp.int32)],
        compiler_params=_cparams("parallel", "arbitrary"),
        name="nsa_select",
    )(page_table, *([cache_t] * PAGES_PER_STEP), wab_bf, pe, nq, tabc, jnp.asarray(ovlt, BF16))


def _nsa_attend_body(pt_ref, top_ref, *refs, past):
    del pt_ref
    pages = refs[:SLC_TOPK]
    (q_ref, sm_ref, new_ref, oc_ref, swa_ref, tabs_ref, tabw_ref, y_ref, swa_out) = refs[SLC_TOPK:]
    b = pl.program_id(0)
    new_blk = past // SLC_BLOCK
    blocks_per_page = PAGE_SIZE // SLC_BLOCK
    qs = _bf(_pad_rows(_stack_heads(q_ref[0]) * SCALE))
    new = new_ref[0]
    lane_t = lax.broadcasted_iota(jnp.int32, (HEAD_DIM, PAGE_SIZE), 1)
    lane = lax.broadcasted_iota(jnp.int32, (SUBLANES, PAGE_SIZE), 1)
    new_kt = jnp.where(lane_t == 0, _row_to_col(new[:, 2 * HEAD_DIM:3 * HEAD_DIM]), 0.0)
    new_vt = jnp.where(lane_t == 0, _row_to_col(new[:, 3 * HEAD_DIM:4 * HEAD_DIM]), 0.0)
    scores, oks, vts = [], [], []
    for s in range(SLC_TOPK):
        ti = jnp.clip(top_ref[b, s], 0, new_blk)
        is_new = ti == new_blk
        kt = jnp.where(is_new, new_kt, pages[s][0])
        vts.append(jnp.where(is_new, new_vt, pages[s][1]))
        scores.append(_dot(qs, _bf(kt)) + tabs_ref[ti])
        kpos = ti * SLC_BLOCK + lane % SLC_BLOCK
        ok = (lane // SLC_BLOCK == ti % blocks_per_page) & (kpos <= past)
        oks.append(jnp.where(ok, 1.0, 0.0))
    p_s = _softmax_rows(jnp.concatenate(scores, axis=1), jnp.concatenate(oks, axis=1) > 0.5)
    o_s = None
    for s in range(SLC_TOPK):
        t = _dot_nt(_bf(p_s[:, s * PAGE_SIZE:(s + 1) * PAGE_SIZE]), _bf(vts[s]))
        o_s = t if o_s is None else o_s + t

    wlane = lax.broadcasted_iota(jnp.int32, (HEAD_DIM, WINDOW), 1)
    win = []
    for c in range(2):
        col = _row_to_col(new[:, (4 + c) * HEAD_DIM:(5 + c) * HEAD_DIM])
        win.append(jnp.where(wlane == WINDOW - 1, col, pltpu.roll(swa_ref[c], WINDOW - 1, 1)))
        swa_out[c] = win[c]
    s_w = _dot(qs, _bf(win[0])) + tabw_ref[...]
    p_w = _softmax_rows(s_w, jnp.full(s_w.shape, True))
    o_w = _dot_nt(_bf(p_w), _bf(win[1]))

    gates = _sigmoid(sm_ref[0])
    o_c = oc_ref[0]
    outs = []
    for h in range(N_HEADS):
        g = lambda t: gates[:, SM_NG + 3 * h + t:SM_NG + 3 * h + t + 1]
        outs.append(g(0) * _head(o_c, h) + g(1) * o_s[h:h + 1, :] + g(2) * o_w[h:h + 1, :])
    y_ref[0] = jnp.concatenate(outs, axis=1)


def _nsa_attend(layer, page_table, top, cache_t, nq, small, nkv, o_c, swa_t, tabs, tabw, past):
    b = nq.shape[0]
    last_page = past // PAGE_SIZE - 1
    blocks_per_page = PAGE_SIZE // SLC_BLOCK

    n_pool = cache_t.shape[1]

    def page_spec(s):
        def imap(i, pt, tp):
            ii = jnp.minimum(i, b - 1)
            pg = jnp.clip(tp[ii, s] // blocks_per_page, 0, last_page)
            return (layer, jnp.clip(pt[ii, pg], 0, n_pool - 1), 1, 0, 0)
        return pl.BlockSpec((None, None, 2, HEAD_DIM, PAGE_SIZE), imap)

    row = lambda n: pl.BlockSpec((1, 1, n), lambda i, pt, tp: (i, 0, 0))
    const = lambda shape: pl.BlockSpec(shape, lambda i, pt, tp: (0,) * len(shape))
    grid_spec = pltpu.PrefetchScalarGridSpec(
        num_scalar_prefetch=2,
        grid=(b,),
        in_specs=[page_spec(s) for s in range(SLC_TOPK)]
        + [row(D_HEADS), row(LANES), row(6 * HEAD_DIM), row(D_HEADS),
           pl.BlockSpec((None, None, 2, HEAD_DIM, WINDOW), lambda i, pt, tp: (layer, i, 0, 0, 0)),
           const(tabs.shape), const(tabw.shape)],
        out_specs=[row(D_HEADS), pl.BlockSpec((None, 2, HEAD_DIM, WINDOW), lambda i, pt, tp: (i, 0, 0, 0))],
    )
    return pl.pallas_call(
        functools.partial(_nsa_attend_body, past=past),
        grid_spec=grid_spec,
        out_shape=[jax.ShapeDtypeStruct((b, 1, D_HEADS), F32),
                   jax.ShapeDtypeStruct((b, 2, HEAD_DIM, WINDOW), F32)],
        compiler_params=_cparams("parallel"),
        name="nsa_attend",
    )(page_table, top, *([cache_t] * SLC_TOPK), nq, small, nkv, o_c, swa_t, tabs, tabw)


SEQ_CHUNK = 64
ROW_TILE = 512
FFN_TILE = 512


def _lane_vec(pairs):
    v = jnp.zeros((1, LANES), F32)
    for off, vals in pairs:
        v = v.at[0, off:off + vals.shape[0]].set(vals.astype(F32))
    return v


def kernel(x_prompt, x_sample, cache_nsa_kv, cache_swa_kv, state_ret, state_mlstm_C, state_mlstm_n, state_mlstm_m, state_gdn, state_gdn_conv, state_ffn_conv, page_table, rel_bias, norm_pre_mix, w_in, mlstm_b_i, mlstm_b_f, gdn_conv_w, gdn_A_log, gdn_dt_bias, nsa_cmp_pe, nsa_cmp_w, ret_norm, mlstm_norm, gdn_norm, w_out, norm_post_mix, norm_pre_ffn, w_ffn_gate, w_ffn_up, ffn_conv_w, w_ffn_down, norm_post_ffn):
    depth = w_in.shape[0]
    bp, seq, _ = x_prompt.shape
    bs, dec_seq, _ = x_sample.shape
    assert dec_seq == 1 and seq % SLC_KC == 0 and seq >= WINDOW + Q_BLOCK
    n_pool = cache_nsa_kv.shape[1]
    past = page_table.shape[1] * PAGE_SIZE
    assert past >= WINDOW and past % SLC_BLOCK == 0 and page_table.shape[1] % PAGES_PER_STEP == 0

    p0, d_seq = _nsa_tab_seq_dists(seq)
    n16 = past // CMP_STRIDE
    n_slc = past // SLC_BLOCK + 1
    d_cmp = np.maximum(past - (np.arange(n16) * CMP_STRIDE + CMP_LEN - 1), 0)
    d_slc = np.maximum(past - np.arange(n_slc * SLC_BLOCK), 0)
    d_win = WINDOW - 1 - np.arange(WINDOW)
    tab = _bias_tables(rel_bias, np.concatenate([d_seq, d_cmp, d_slc, d_win]))
    o1, o2, o3 = len(d_seq), len(d_seq) + len(d_cmp), len(d_seq) + len(d_cmp) + len(d_slc)
    tab_seq = tab[:, :o1].reshape(SUBLANES, o1 // LANES, LANES)
    tabc = tab[:, o1:o2]
    tabs = jnp.transpose(tab[:, o2:o3].reshape(SUBLANES, n_slc, SLC_BLOCK), (1, 0, 2))
    tabs = jnp.concatenate([tabs, tabs], axis=2)
    tabw = tab[:, o3:]

    cache_t = jnp.transpose(cache_nsa_kv, (0, 1, 3, 4, 2))
    swa_t = jnp.transpose(cache_swa_kv, (0, 1, 3, 4, 2))
    cos_p, sin_p = _rope_tables(jnp.arange(seq))
    cos_s, sin_s = _rope_tables(jnp.arange(past, past + 1))

    xp = x_prompt
    xs = x_sample.reshape(bs, D_MODEL)
    p_states, s_states = [], []
    for l in range(depth):
        w_cat = _prep_w_in(w_in[l])
        w_out_bf = _bf(w_out[l])
        wg, wu, wd = _bf(w_ffn_gate[l]), _bf(w_ffn_up[l]), _bf(w_ffn_down[l])
        wab, pe = _prep_cmp_w(nsa_cmp_w[l], nsa_cmp_pe[l])
        ml_bias = _lane_vec([(SM_MI, mlstm_b_i[l]), (SM_MF, mlstm_b_f[l])])
        dtb = _lane_vec([(SM_GA, gdn_dt_bias[l])])
        alog = _lane_vec([(SM_GA, gdn_A_log[l])])

        z = _in_proj(xp.reshape(bp * seq, D_MODEL), norm_pre_mix[l], w_cat, ROW_TILE, seq=seq)
        z_ret, z_ml, z_gdn, z_nq, z_nkv, z_sm = [t.reshape(bp, seq, -1) for t in z[:-1]]
        nkv_t = z[-1].reshape(bp, 6, HEAD_DIM, seq)
        y_ret, st_ret = _retention_seq(z_ret, cos_p, sin_p, ret_norm[l], SEQ_CHUNK, bp)
        y_ml, st_c, st_n, st_m = _mlstm_seq(z_ml, z_sm, ml_bias, mlstm_norm[l], SEQ_CHUNK, bp)
        y_gdn, st_g = _gdn_seq(z_gdn, z_sm, gdn_conv_w[l], dtb, alog, gdn_norm[l], SEQ_CHUNK, bp)
        rows = z_nkv[..., 0:2 * HEAD_DIM].reshape(bp, seq // CMP_STRIDE, CMP_STRIDE * 2 * HEAD_DIM)
        kvc = _compress_seq(rows, wab, pe)
        y_nsa = _nsa_seq(rel_bias, z_nq, z_sm, z_nkv, kvc, tab_seq, p0)
        ys = [t.reshape(bp * seq, D_HEADS) for t in (y_ret, y_ml, y_gdn, y_nsa)]
        x1, h2 = _out_proj(xp.reshape(bp * seq, D_MODEL), ys, w_out_bf, norm_post_mix[l], norm_pre_ffn[l], ROW_TILE)
        ffn_tile = min(seq, FFN_TILE)
        xp, tail = _ffn_seq(x1.reshape(bp, seq, D_MODEL), h2.reshape(bp, seq, D_MODEL), wg, wu, wd,
                            ffn_conv_w[l], norm_post_ffn[l], ffn_tile)
        p_states.append((
            jnp.transpose(nkv_t[:, 0:4], (0, 3, 1, 2)),
            jnp.transpose(nkv_t[:, 4:6, :, seq - WINDOW:], (0, 3, 1, 2)),
            st_ret, st_c, st_n, st_m[:, 0, :N_HEADS], st_g,
            z_gdn[:, seq - (GDN_CONV - 1):, 0:3 * D_HEADS],
            tail[:, SUBLANES - (FFN_CONV - 1):, :]))

        z = _in_proj(xs, norm_pre_mix[l], w_cat, bs)
        z_ret, z_ml, z_gdn, z_nq, z_nkv, z_sm = [t.reshape(bs, 1, -1) for t in z]
        m_pad = jnp.pad(state_mlstm_m[l], ((0, 0), (0, LANES - N_HEADS))).reshape(bs, 1, LANES)
        (y_ret, y_ml, y_gdn, st_ret, st_c, st_n, st_m, st_g, st_gc) = _mixers_step(
            z_ret, z_ml, z_gdn, z_sm, cos_s, sin_s, ml_bias, dtb, alog, gdn_conv_w[l],
            ret_norm[l], mlstm_norm[l], gdn_norm[l],
            state_ret[l], state_mlstm_C[l], state_mlstm_n[l], m_pad, state_gdn[l], state_gdn_conv[l])
        o_c, top = _nsa_select(l, page_table, cache_t, _bf(wab), pe, z_nq, tabc, past)
        y_nsa, swa_new = _nsa_attend(l, page_table, top[:, 0, :SLC_TOPK], cache_t, z_nq, z_sm, z_nkv, o_c,
                                     swa_t, tabs, tabw, past)
        ys = [t.reshape(bs, D_HEADS) for t in (y_ret, y_ml, y_gdn, y_nsa)]
        x1, h2 = _out_proj(xs, ys, w_out_bf, norm_post_mix[l], norm_pre_ffn[l], bs)
        xs, g_new = _ffn_step(x1, h2, state_ffn_conv[l][:, 0], state_ffn_conv[l][:, 1], wg, wu, wd,
                              ffn_conv_w[l], norm_post_ffn[l])
        s_states.append((
            z_nkv[..., 0:4 * HEAD_DIM].reshape(bs, 1, 4, HEAD_DIM),
            jnp.transpose(swa_new, (0, 3, 1, 2)),
            st_ret, st_c, st_n, st_m[:, 0, :N_HEADS], st_g, st_gc,
            jnp.concatenate([state_ffn_conv[l][:, 1:], g_new[:, None, :]], axis=1)))

    stack = lambda states: tuple(jnp.stack([s[k] for s in states]) for k in range(len(states[0])))
    return (xp, xs.reshape(bs, 1, D_MODEL)) + stack(p_states) + stack(s_states)
```

```python
import functools
import math

import jax
import jax.numpy as jnp
import numpy as np
from jax import lax
from jax.experimental import pallas as pl
from jax.experimental.pallas import tpu as pltpu

F32 = jnp.float32
BF16 = jnp.bfloat16
HI = lax.Precision.HIGHEST

D_MODEL = 1024
HEAD_DIM = 64
N_HEADS = 4
D_HEADS = N_HEADS * HEAD_DIM
D_FF = 2816
GDN_CONV = 4
FFN_CONV = 3
PAGE_SIZE = 128
CMP_STRIDE = 16
CMP_LEN = 32
SLC_BLOCK = 64
SLC_TOPK = 16
WINDOW = 512
Q_BLOCK = 128
T5_BUCKETS = 32
T5_MAX_DIST = 1024
ROPE_BASE = 10000.0
EPS = 1e-6
SCALE = HEAD_DIM ** -0.5
IN_WIDTHS = (256, 256, 256, 256, 256, 256, 256, 4, 4, 256, 768, 4, 4, 256, 256, 384, 12)
LANES = 128
SUBLANES = 8
VMEM_LIMIT = 56 * 1024 * 1024
NEG_INF = float("-inf")

SM_MI, SM_MF, SM_GA, SM_GB, SM_NG = 0, 4, 8, 12, 16


def _cparams(*sem):
    return pltpu.CompilerParams(dimension_semantics=sem, vmem_limit_bytes=VMEM_LIMIT)


def _rms(x, g):
    return x * lax.rsqrt(jnp.mean(x * x, axis=-1, keepdims=True) + EPS) * g


def _dot(a, b, **kw):
    return jnp.dot(a, b, preferred_element_type=F32, **kw)


def _dot_nt(a, b, **kw):
    return lax.dot_general(a, b, (((1,), (1,)), ((), ())), preferred_element_type=F32, **kw)


def _dot_tn(a, b, **kw):
    return lax.dot_general(a, b, (((0,), (0,)), ((), ())), preferred_element_type=F32, **kw)


def _bf(x):
    return x.astype(BF16)


def _sigmoid(x):
    return 1.0 / (1.0 + jnp.exp(-x))


def _silu(x):
    return x * _sigmoid(x)


def _softplus(x):
    return jnp.maximum(x, 0.0) + jnp.log(1.0 + jnp.exp(-jnp.abs(x)))


def _head_rms(o):
    return o * lax.rsqrt(jnp.mean(o * o, axis=-1, keepdims=True) + EPS)


PROJ_WIDTHS = (1024, 1024, 1024, 256, 384, 128)


def _prep_w_in(w):
    wt = w.T
    parts, off = [], 0
    for wd in IN_WIDTHS:
        parts.append(wt[off:off + wd])
        off += wd
    (rq, rk, rv, rg, mq, mk, mv, mi, mf, mo, gqkv, ga, gbeta, gg, nq, nkv, ngate) = parts
    small = jnp.concatenate([mi, mf, ga, gbeta, ngate], axis=0)
    small = jnp.pad(small, ((0, LANES - small.shape[0]), (0, 0)))
    cat = jnp.concatenate([rq, rk, rv, rg, mq, mk, mv, mo, gqkv, gg, nq, nkv, small], axis=0)
    return cat.astype(BF16)


NKV_OFF = sum(PROJ_WIDTHS[:4])


def _in_proj_body(x_ref, g_ref, wt_ref, *out_refs, with_kv_t):
    h = _bf(_rms(x_ref[...], g_ref[...]))
    off = 0
    for ref in out_refs[:len(PROJ_WIDTHS)]:
        n = ref.shape[-1]
        ref[...] = _dot_nt(h, wt_ref[off:off + n, :])
        off += n
    if with_kv_t:
        out_refs[-1][0] = _dot_nt(wt_ref[NKV_OFF:NKV_OFF + PROJ_WIDTHS[4], :], h)


def _in_proj(x2d, g, w_cat_t, tm, seq=None):
    m = x2d.shape[0]
    ntot = sum(PROJ_WIDTHS)
    with_kv_t = seq is not None
    const = lambda shape: pl.BlockSpec(shape, lambda i: (0,) * len(shape), pipeline_mode=pl.Buffered(1))
    out_specs = [pl.BlockSpec((tm, n), lambda i: (i, 0)) for n in PROJ_WIDTHS]
    out_shape = [jax.ShapeDtypeStruct((m, n), F32) for n in PROJ_WIDTHS]
    if with_kv_t:
        per_seq = seq // tm
        nkv = PROJ_WIDTHS[4]
        out_specs.append(pl.BlockSpec((1, nkv, tm), lambda i: (i // per_seq, 0, i % per_seq)))
        out_shape.append(jax.ShapeDtypeStruct((m // seq, nkv, seq), F32))
    return pl.pallas_call(
        functools.partial(_in_proj_body, with_kv_t=with_kv_t),
        grid=(m // tm,),
        in_specs=[pl.BlockSpec((tm, D_MODEL), lambda i: (i, 0)), const((1, D_MODEL)), const((ntot, D_MODEL))],
        out_specs=out_specs,
        out_shape=out_shape,
        compiler_params=_cparams("parallel"),
        name="in_proj",
    )(x2d, g.reshape(1, D_MODEL), w_cat_t)


def _out_proj_body(x_ref, y0, y1, y2, y3, w_ref, gpost_ref, gpre_ref, x1_ref, h2_ref):
    acc = None
    for k, y in enumerate((y0, y1, y2, y3)):
        t = _dot(_bf(y[...]), w_ref[k * D_HEADS:(k + 1) * D_HEADS, :])
        acc = t if acc is None else acc + t
    x1 = x_ref[...] + _rms(acc, gpost_ref[...])
    x1_ref[...] = x1
    h2_ref[...] = _bf(_rms(x1, gpre_ref[...]))


def _out_proj(x2d, ys, w_out_bf, gpost, gpre, tm):
    m = x2d.shape[0]
    row = lambda n: pl.BlockSpec((tm, n), lambda i: (i, 0))
    vec = pl.BlockSpec((1, D_MODEL), lambda i: (0, 0))
    return pl.pallas_call(
        _out_proj_body,
        grid=(m // tm,),
        in_specs=[row(D_MODEL)] + [row(D_HEADS)] * 4
        + [pl.BlockSpec((D_MODEL, D_MODEL), lambda i: (0, 0)), vec, vec],
        out_specs=[row(D_MODEL), row(D_MODEL)],
        out_shape=[jax.ShapeDtypeStruct((m, D_MODEL), F32), jax.ShapeDtypeStruct((m, D_MODEL), BF16)],
        compiler_params=_cparams("parallel"),
        name="out_proj",
    )(x2d, *ys, w_out_bf, gpost.reshape(1, -1), gpre.reshape(1, -1))


FFN_TN = 256


def _gelu_tanh(x):
    c = 0.7978845608028654
    hx = 0.5 * x
    return hx + hx * jnp.tanh(x * (c + (0.044715 * c) * (x * x)))


def _ffn_seq_body(x1_ref, h2_ref, wg_ref, wu_ref, wd_ref, cw_ref, gpost_ref, x2_ref, tail_ref, carry_ref, act_ref,
                  *, tm):
    @pl.when(pl.program_id(1) == 0)
    def _():
        carry_ref[...] = jnp.zeros_like(carry_ref)

    h2 = h2_ref[0]
    row8 = lax.broadcasted_iota(jnp.int32, (SUBLANES, FFN_TN), 0)
    starts = list(range(0, D_FF, FFN_TN))
    up = lambda n0: (_dot(h2, wg_ref[:, n0:n0 + FFN_TN]), _dot(h2, wu_ref[:, n0:n0 + FFN_TN]))
    nxt = up(starts[0])
    for i, n0 in enumerate(starts):
        g, u = nxt
        if i + 1 < len(starts):
            nxt = up(starts[i + 1])
        prev = carry_ref[:, n0:n0 + FFN_TN]
        g1, g2 = pltpu.roll(g, 1, 0), pltpu.roll(g, 2, 0)
        top1 = jnp.where(row8 == 0, prev[7:8, :], g1[0:SUBLANES])
        top2 = jnp.where(row8 == 0, prev[6:7, :], jnp.where(row8 == 1, prev[7:8, :], g2[0:SUBLANES]))
        g1 = jnp.concatenate([top1, g1[SUBLANES:]], axis=0)
        g2 = jnp.concatenate([top2, g2[SUBLANES:]], axis=0)
        cw = cw_ref[:, n0:n0 + FFN_TN]
        gate = cw[0:1, :] * g2 + cw[1:2, :] * g1 + cw[2:3, :] * g
        a = _gelu_tanh(gate) * u
        act_ref[:, n0:n0 + FFN_TN] = _bf(a)
        carry_ref[:, n0:n0 + FFN_TN] = g[tm - SUBLANES:tm, :]
    acc = _dot(act_ref[...], wd_ref[...])
    x2_ref[0] = x1_ref[0] + _rms(acc, gpost_ref[...])
    tail_ref[0] = carry_ref[...]


def _ffn_seq(x1, h2, wg, wu, wd, cw, gpost, tm):
    b, l, _ = x1.shape
    const = lambda shape: pl.BlockSpec(shape, lambda i, j: (0,) * len(shape), pipeline_mode=pl.Buffered(1))
    return pl.pallas_call(
        functools.partial(_ffn_seq_body, tm=tm),
        grid=(b, l // tm),
        in_specs=[pl.BlockSpec((1, tm, D_MODEL), lambda i, j: (i, j, 0)),
                  pl.BlockSpec((1, tm, D_MODEL), lambda i, j: (i, j, 0)),
                  const((D_MODEL, D_FF)), const((D_MODEL, D_FF)), const((D_FF, D_MODEL)),
                  const((FFN_CONV, D_FF)), const((1, D_MODEL))],
        out_specs=[pl.BlockSpec((1, tm, D_MODEL), lambda i, j: (i, j, 0)),
                   pl.BlockSpec((1, SUBLANES, D_FF), lambda i, j: (i, 0, 0))],
        out_shape=[jax.ShapeDtypeStruct((b, l, D_MODEL), F32),
                   jax.ShapeDtypeStruct((b, SUBLANES, D_FF), F32)],
        scratch_shapes=[pltpu.VMEM((SUBLANES, D_FF), F32), pltpu.VMEM((tm, D_FF), BF16)],
        compiler_params=_cparams("parallel", "arbitrary"),
        name="ffn_seq",
    )(x1, h2, wg, wu, wd, cw, gpost.reshape(1, -1))


def _ffn_step_body(x1_ref, h2_ref, b0_ref, b1_ref, wg_ref, wu_ref, wd_ref, cw_ref, gpost_ref, x2_ref, g_ref):
    h2 = h2_ref[...]
    acc = jnp.zeros(x1_ref.shape, F32)
    for n0 in range(0, D_FF, FFN_TN):
        sl = slice(n0, n0 + FFN_TN)
        g = _dot(h2, wg_ref[:, sl])
        cw = cw_ref[:, sl]
        gate = cw[0:1, :] * b0_ref[:, sl] + cw[1:2, :] * b1_ref[:, sl] + cw[2:3, :] * g
        u = _dot(h2, wu_ref[:, sl])
        acc = acc + _dot(_bf(_gelu_tanh(gate) * u), wd_ref[sl, :])
        g_ref[:, sl] = g
    x2_ref[...] = x1_ref[...] + _rms(acc, gpost_ref[...])


def _ffn_step(x1, h2, b0, b1, wg, wu, wd, cw, gpost):
    m = x1.shape[0]
    return pl.pallas_call(
        _ffn_step_body,
        out_shape=[jax.ShapeDtypeStruct((m, D_MODEL), F32), jax.ShapeDtypeStruct((m, D_FF), F32)],
        compiler_params=pltpu.CompilerParams(vmem_limit_bytes=VMEM_LIMIT),
        name="ffn_step",
    )(x1, h2, b0, b1, wg, wu, wd, cw, gpost.reshape(1, -1))


def _rope_tables(pos):
    half = HEAD_DIM // 2
    inv = ROPE_BASE ** (-jnp.linspace(0.0, 1.0, half, dtype=F32))
    ang = pos.astype(F32)[:, None] * inv[None, :]
    return jnp.tile(jnp.cos(ang), (1, LANES // half)), jnp.tile(jnp.sin(ang), (1, LANES // half))


def _rope128(x, cos, sin):
    lane = lax.broadcasted_iota(jnp.int32, x.shape, 1)
    first = (lane % HEAD_DIM) < (HEAD_DIM // 2)
    other = jnp.where(first, -pltpu.roll(x, LANES - HEAD_DIM // 2, 1), pltpu.roll(x, HEAD_DIM // 2, 1))
    return x * cos + other * sin


def _rope256(x, cos, sin):
    return jnp.concatenate([_rope128(x[:, :LANES], cos, sin), _rope128(x[:, LANES:], cos, sin)], axis=1)


def _ret_log_decay(h):
    return math.log(1.0 - 2.0 ** (-5.0 - h))


def _head(x, h):
    return x[:, h * HEAD_DIM:(h + 1) * HEAD_DIM]


def _ret_body(z_ref, cos_ref, sin_ref, nw_ref, y_ref, s_ref, *, chunk):
    @pl.when(pl.program_id(1) == 0)
    def _():
        s_ref[...] = jnp.zeros_like(s_ref)

    nb = z_ref.shape[0]
    cos, sin = cos_ref[...], sin_ref[...]
    t = lax.broadcasted_iota(jnp.int32, (chunk, chunk), 0)
    s = lax.broadcasted_iota(jnp.int32, (chunk, chunk), 1)
    causal = t >= s
    diff = jnp.where(causal, t - s, 0).astype(F32)
    tcol = lax.broadcasted_iota(jnp.int32, (chunk, 1), 0).astype(F32)
    lgs = [_ret_log_decay(h) for h in range(N_HEADS)]
    dmat = [jnp.where(causal, jnp.exp(lg * diff), 0.0) for lg in lgs]
    xi = [jnp.exp(lg * (tcol + 1.0)) for lg in lgs]
    zeta = [jnp.exp(lg * (chunk - 1.0 - tcol)) for lg in lgs]
    probs = [(bb, h) for bb in range(nb) for h in range(N_HEADS)]
    qkv = []
    for bb in range(nb):
        q = _rope256(z_ref[bb, :, 0:256], cos, sin)
        k = _rope256(z_ref[bb, :, 256:512], cos, sin) * SCALE
        qkv.append((q, k, z_ref[bb, :, 512:768]))
    qh = [_head(qkv[bb][0], h) for bb, h in probs]
    kh = [_head(qkv[bb][1], h) for bb, h in probs]
    vh = [_bf(_head(qkv[bb][2], h)) for bb, h in probs]
    st = [s_ref[bb, h] for bb, h in probs]
    sc = [_dot_nt(_bf(q), _bf(k)) for q, k in zip(qh, kh)]
    cross = [_dot(_bf(q * xi[h]), _bf(s0)) for q, s0, (_, h) in zip(qh, st, probs)]
    upd = [_dot_tn(_bf(k * zeta[h]), v) for k, v, (_, h) in zip(kh, vh, probs)]
    o = [_dot(_bf(x * dmat[h]), v) + c for x, v, c, (_, h) in zip(sc, vh, cross, probs)]
    for i, (bb, h) in enumerate(probs):
        s_ref[bb, h] = st[i] * math.exp(lgs[h] * chunk) + upd[i]
    for bb in range(nb):
        outs = [_head_rms(o[bb * N_HEADS + h]) for h in range(N_HEADS)]
        y_ref[bb] = jnp.concatenate(outs, axis=1) * nw_ref[...] * _silu(z_ref[bb, :, 768:1024])


def _retention_seq(z_ret, cos, sin, nw, chunk, nb):
    b, l, _ = z_ret.shape
    return pl.pallas_call(
        functools.partial(_ret_body, chunk=chunk),
        grid=(b // nb, l // chunk),
        in_specs=[pl.BlockSpec((nb, chunk, 1024), lambda i, j: (i, j, 0)),
                  pl.BlockSpec((chunk, LANES), lambda i, j: (j, 0)),
                  pl.BlockSpec((chunk, LANES), lambda i, j: (j, 0)),
                  pl.BlockSpec((1, D_HEADS), lambda i, j: (0, 0))],
        out_specs=[pl.BlockSpec((nb, chunk, D_HEADS), lambda i, j: (i, j, 0)),
                   pl.BlockSpec((nb, N_HEADS, HEAD_DIM, HEAD_DIM), lambda i, j: (i, 0, 0, 0))],
        out_shape=[jax.ShapeDtypeStruct((b, l, D_HEADS), F32),
                   jax.ShapeDtypeStruct((b, N_HEADS, HEAD_DIM, HEAD_DIM), F32)],
        compiler_params=_cparams("parallel", "arbitrary"),
        name="retention_seq",
    )(z_ret, cos, sin, nw.reshape(1, -1))


def _lanes_to_rows(x, n=2 * SUBLANES):
    sel = (lax.broadcasted_iota(jnp.int32, (n, x.shape[1]), 0)
           == lax.broadcasted_iota(jnp.int32, (n, x.shape[1]), 1)).astype(F32)
    return _dot_nt(sel, x, precision=HI)


def _tri_incl(n):
    t = lax.broadcasted_iota(jnp.int32, (n, n), 0)
    s = lax.broadcasted_iota(jnp.int32, (n, n), 1)
    return t >= s


def _ml_body(z_ref, sm_ref, bias_ref, nw_ref, y_ref, c_ref, n_ref, m_ref, *, chunk):
    @pl.when(pl.program_id(1) == 0)
    def _():
        c_ref[...] = jnp.zeros_like(c_ref)
        n_ref[...] = jnp.zeros_like(n_ref)
        m_ref[...] = jnp.zeros_like(m_ref)

    nb = z_ref.shape[0]
    incl = _tri_incl(chunk)
    tri = incl.astype(F32)
    probs = [(bb, h) for bb in range(nb) for h in range(N_HEADS)]
    gate = []
    for bb in range(nb):
        pre = sm_ref[bb] + bias_ref[...]
        logf = jnp.minimum(pre, 0.0) - jnp.log(1.0 + jnp.exp(-jnp.abs(pre)))
        bcum = _dot(tri, logf, precision=HI)
        gate.append((pre, bcum, _lanes_to_rows(pre), _lanes_to_rows(bcum), m_ref[bb]))
    qh = [_head(z_ref[bb, :, 0:256], h) for bb, h in probs]
    kh = [_head(z_ref[bb, :, 256:512], h) * SCALE for bb, h in probs]
    vh = [_bf(_head(z_ref[bb, :, 512:768], h)) for bb, h in probs]
    cm = [c_ref[bb, h] for bb, h in probs]
    nv = [n_ref[bb, h:h + 1, :] for bb, h in probs]
    qk_raw = [_dot_nt(_bf(q), _bf(k)) for q, k in zip(qh, kh)]
    q_c = [_dot(_bf(q), _bf(c)) for q, c in zip(qh, cm)]
    i_col, b_col, inter, dlog = [], [], [], []
    for bb, h in probs:
        pre, bcum, pre_t, bcum_t, m_all = gate[bb]
        i_col.append(pre[:, SM_MI + h:SM_MI + h + 1])
        b_col.append(bcum[:, SM_MF + h:SM_MF + h + 1])
        inter.append(b_col[-1] + m_all[:, h:h + 1])
        dlog.append(jnp.where(incl, b_col[-1] - bcum_t[SM_MF + h:SM_MF + h + 1, :]
                              + pre_t[SM_MI + h:SM_MI + h + 1, :], NEG_INF))
    dmax = [jnp.max(x, axis=1, keepdims=True) for x in dlog]
    q_n = [jnp.sum(q * n, axis=1, keepdims=True) for q, n in zip(qh, nv)]
    stab = []
    for i in range(len(probs)):
        m_t = jnp.maximum(inter[i], dmax[i])
        m_end = m_t[chunk - 1:chunk, :]
        stab.append(dict(m_t=m_t, w=jnp.exp(dlog[i] - m_t), g_in=jnp.exp(inter[i] - m_t), m_end=m_end,
                         w_end=jnp.exp(b_col[i][chunk - 1:chunk, :] - b_col[i] + i_col[i] - m_end),
                         g_end=jnp.exp(inter[i][chunk - 1:chunk, :] - m_end)))
    qk = [x * d["w"] for x, d in zip(qk_raw, stab)]
    num = [_dot(_bf(x), v) + d["g_in"] * c for x, v, d, c in zip(qk, vh, stab, q_c)]
    upd = [_dot_tn(_bf(k * d["w_end"]), v) for k, d, v in zip(kh, stab, vh)]
    qk_sum = [jnp.sum(x, axis=1, keepdims=True) for x in qk]
    k_sum = [jnp.sum(d["w_end"] * k, axis=0, keepdims=True) for d, k in zip(stab, kh)]
    hh = []
    for i, (bb, h) in enumerate(probs):
        d = stab[i]
        den = qk_sum[i] + d["g_in"] * q_n[i]
        hh.append(num[i] / jnp.maximum(jnp.abs(den), jnp.exp(-d["m_t"])))
        c_ref[bb, h] = d["g_end"] * cm[i] + upd[i]
        n_ref[bb, h:h + 1, :] = d["g_end"] * nv[i] + k_sum[i]
    lane = lax.broadcasted_iota(jnp.int32, (1, LANES), 1)
    for bb in range(nb):
        m_out = jnp.zeros((1, LANES), F32)
        outs = []
        for h in range(N_HEADS):
            i = bb * N_HEADS + h
            m_out = jnp.where(lane == h, stab[i]["m_end"], m_out)
            outs.append(_head_rms(_sigmoid(_head(z_ref[bb, :, 768:1024], h)) * hh[i]))
        m_ref[bb] = m_out
        y_ref[bb] = jnp.concatenate(outs, axis=1) * nw_ref[...]


def _mlstm_seq(z_ml, small, bias_vec, nw, chunk, nb):
    b, l, _ = z_ml.shape
    return pl.pallas_call(
        functools.partial(_ml_body, chunk=chunk),
        grid=(b // nb, l // chunk),
        in_specs=[pl.BlockSpec((nb, chunk, 1024), lambda i, j: (i, j, 0)),
                  pl.BlockSpec((nb, chunk, LANES), lambda i, j: (i, j, 0)),
                  pl.BlockSpec((1, LANES), lambda i, j: (0, 0)),
                  pl.BlockSpec((1, D_HEADS), lambda i, j: (0, 0))],
        out_specs=[pl.BlockSpec((nb, chunk, D_HEADS), lambda i, j: (i, j, 0)),
                   pl.BlockSpec((nb, N_HEADS, HEAD_DIM, HEAD_DIM), lambda i, j: (i, 0, 0, 0)),
                   pl.BlockSpec((nb, N_HEADS, HEAD_DIM), lambda i, j: (i, 0, 0)),
                   pl.BlockSpec((nb, 1, LANES), lambda i, j: (i, 0, 0))],
        out_shape=[jax.ShapeDtypeStruct((b, l, D_HEADS), F32),
                   jax.ShapeDtypeStruct((b, N_HEADS, HEAD_DIM, HEAD_DIM), F32),
                   jax.ShapeDtypeStruct((b, N_HEADS, HEAD_DIM), F32),
                   jax.ShapeDtypeStruct((b, 1, LANES), F32)],
        compiler_params=_cparams("parallel", "arbitrary"),
        name="mlstm_seq",
    )(z_ml, small, bias_vec, nw.reshape(1, -1))


def _split_bf16(x):
    hi = _bf(x)
    return hi, _bf(x - hi.astype(F32))


def _dot3(a, b):
    (ah, al), (bh, bl) = a, b
    return _dot(ah, bh) + (_dot(ah, bl) + _dot(al, bh))


def _l2norm(x):
    return x * lax.rsqrt(jnp.sum(x * x, axis=-1, keepdims=True) + EPS)


def _gdn_body(z_ref, sm_ref, cw_ref, dtb_ref, alog_ref, nw_ref, y_ref, s_ref, buf_ref, *, chunk):
    @pl.when(pl.program_id(1) == 0)
    def _():
        s_ref[...] = jnp.zeros_like(s_ref)
        buf_ref[:, 0:SUBLANES, :] = jnp.zeros((buf_ref.shape[0], SUBLANES, 3 * D_HEADS), F32)

    nb = z_ref.shape[0]
    incl = _tri_incl(chunk)
    strict = lax.broadcasted_iota(jnp.int32, (chunk, chunk), 0) > lax.broadcasted_iota(jnp.int32, (chunk, chunk), 1)
    eye = (lax.broadcasted_iota(jnp.int32, (chunk, chunk), 0)
           == lax.broadcasted_iota(jnp.int32, (chunk, chunk), 1)).astype(F32)
    tri = incl.astype(F32)
    gates, acts = [], []
    for bb in range(nb):
        buf_ref[bb, SUBLANES:SUBLANES + chunk, :] = z_ref[bb, :, 0:768]
        conv = None
        for j in range(GDN_CONV):
            term = buf_ref[bb, pl.ds(SUBLANES - (GDN_CONV - 1) + j, chunk), :] * cw_ref[j:j + 1, :]
            conv = term if conv is None else conv + term
        buf_ref[bb, 0:SUBLANES, :] = buf_ref[bb, chunk:chunk + SUBLANES, :]
        acts.append(_silu(conv))
        sm = sm_ref[bb]
        g_all = -jnp.exp(alog_ref[...]) * _softplus(sm + dtb_ref[...])
        gcum = _dot(tri, g_all, precision=HI)
        gates.append((gcum, _lanes_to_rows(gcum), _sigmoid(sm)))

    probs = [(bb, h) for bb in range(nb) for h in range(N_HEADS)]
    pre = []
    for bb, h in probs:
        gcum, gcum_t, beta_all = gates[bb]
        g_col = gcum[:, SM_GA + h:SM_GA + h + 1]
        g_row = gcum_t[SM_GA + h:SM_GA + h + 1, :]
        beta = beta_all[:, SM_GB + h:SM_GB + h + 1]
        decay = jnp.where(incl, jnp.exp(jnp.where(incl, g_col - g_row, 0.0)), 0.0)
        act = acts[bb]
        qh = _l2norm(_head(act[:, 0:256], h)) * SCALE
        kh = _l2norm(_head(act[:, 256:512], h))
        vh = _head(act[:, 512:768], h)
        eg = jnp.exp(g_col)
        pre.append(dict(g_col=g_col, beta=beta, decay=decay, qh=qh, kh=kh, kb=_bf(kh), eg=eg,
                        rhs=jnp.concatenate([vh * beta, kh * (beta * eg)], axis=1)))
    a_l = [jnp.where(strict, d["beta"] * _dot_nt(d["kb"], d["kb"]) * d["decay"], 0.0) for d in pre]
    p_l = [eye - a for a in a_l]
    pw_s = [_split_bf16(a) for a in a_l]
    lvl = 2
    while lvl < chunk:
        pw_s = [_split_bf16(_dot3(x, x)) for x in pw_s]
        p_l = [p + _dot3(_split_bf16(p), x) for p, x in zip(p_l, pw_s)]
        lvl *= 2
    uw_l = [_dot3(_split_bf16(p), _split_bf16(d["rhs"])) for p, d in zip(p_l, pre)]
    st_l = [s_ref[bb, h] for bb, h in probs]
    stb_l = [_bf(st) for st in st_l]
    delta_l = [uw[:, 0:HEAD_DIM] - _dot(_bf(uw[:, HEAD_DIM:]), stb) for uw, stb in zip(uw_l, stb_l)]
    qk_l = [_dot_nt(_bf(d["qh"]), d["kb"]) * d["decay"] for d in pre]
    o_l = [_dot(_bf(qk), _bf(delta)) + _dot(_bf(d["qh"] * d["eg"]), stb)
           for qk, delta, d, stb in zip(qk_l, delta_l, pre, stb_l)]
    for (bb, h), d, st, delta in zip(probs, pre, st_l, delta_l):
        g_end = d["g_col"][chunk - 1:chunk, :]
        w_end = jnp.exp(g_end - d["g_col"])
        s_ref[bb, h] = jnp.exp(g_end) * st + _dot_tn(_bf(d["kh"] * w_end), _bf(delta))
    for bb in range(nb):
        outs = [_head_rms(o_l[bb * N_HEADS + h]) for h in range(N_HEADS)]
        y_ref[bb] = jnp.concatenate(outs, axis=1) * nw_ref[...] * _silu(z_ref[bb, :, 768:1024])


def _gdn_seq(z_gdn, small, conv_w, dtb_vec, alog_vec, nw, chunk, nb):
    b, l, _ = z_gdn.shape
    return pl.pallas_call(
        functools.partial(_gdn_body, chunk=chunk),
        grid=(b // nb, l // chunk),
        in_specs=[pl.BlockSpec((nb, chunk, 1024), lambda i, j: (i, j, 0)),
                  pl.BlockSpec((nb, chunk, LANES), lambda i, j: (i, j, 0)),
                  pl.BlockSpec((GDN_CONV, 3 * D_HEADS), lambda i, j: (0, 0)),
                  pl.BlockSpec((1, LANES), lambda i, j: (0, 0)),
                  pl.BlockSpec((1, LANES), lambda i, j: (0, 0)),
                  pl.BlockSpec((1, D_HEADS), lambda i, j: (0, 0))],
        out_specs=[pl.BlockSpec((nb, chunk, D_HEADS), lambda i, j: (i, j, 0)),
                   pl.BlockSpec((nb, N_HEADS, HEAD_DIM, HEAD_DIM), lambda i, j: (i, 0, 0, 0))],
        out_shape=[jax.ShapeDtypeStruct((b, l, D_HEADS), F32),
                   jax.ShapeDtypeStruct((b, N_HEADS, HEAD_DIM, HEAD_DIM), F32)],
        scratch_shapes=[pltpu.VMEM((nb, SUBLANES + chunk, 3 * D_HEADS), F32)],
        compiler_params=_cparams("parallel", "arbitrary"),
        name="gdn_seq",
    )(z_gdn, small, conv_w, dtb_vec, alog_vec, nw.reshape(1, -1))


def _row_to_col(row):
    n = row.shape[1]
    eye = lax.broadcasted_iota(jnp.int32, (n, n), 0) == lax.broadcasted_iota(jnp.int32, (n, n), 1)
    return jnp.sum(jnp.where(eye, jnp.broadcast_to(row, (n, n)), 0.0), axis=1, keepdims=True)


def _vec_mat(col, mat):
    return jnp.sum(col * mat, axis=0, keepdims=True)


def _step_body(zr_ref, zm_ref, zg_ref, sm_ref, cos_ref, sin_ref, mlb_ref, dtb_ref, alog_ref, cw_ref,
               nwr_ref, nwm_ref, nwg_ref, sr_ref, mc_ref, mn_ref, mm_ref, gs_ref, gc_ref,
               yr_ref, ym_ref, yg_ref, sr_o, mc_o, mn_o, mm_o, gs_o, gc_o):
    sm = sm_ref[0]
    lane = lax.broadcasted_iota(jnp.int32, (1, LANES), 1)
    zr = zr_ref[0]
    cos, sin = cos_ref[...], sin_ref[...]
    q = _rope256(zr[:, 0:256], cos, sin)
    k = _rope256(zr[:, 256:512], cos, sin) * SCALE
    v = zr[:, 512:768]
    outs = []
    for h in range(N_HEADS):
        gam = math.exp(_ret_log_decay(h))
        qh, kh, vh = _head(q, h), _head(k, h), _head(v, h)
        st = sr_ref[0, h]
        o = jnp.sum(qh * kh, axis=1, keepdims=True) * vh + _vec_mat(_row_to_col(qh) * gam, st)
        sr_o[0, h] = st * gam + _row_to_col(kh) * vh
        outs.append(_head_rms(o))
    yr_ref[0] = jnp.concatenate(outs, axis=1) * nwr_ref[...] * _silu(zr[:, 768:1024])
    zm = zm_ref[0]
    q, k, v, og = zm[:, 0:256], zm[:, 256:512] * SCALE, zm[:, 512:768], zm[:, 768:1024]
    pre = sm + mlb_ref[...]
    logf = jnp.minimum(pre, 0.0) - jnp.log(1.0 + jnp.exp(-jnp.abs(pre)))
    m_all = mm_ref[0]
    m_out = jnp.zeros((1, LANES), F32)
    outs = []
    for h in range(N_HEADS):
        i_g = pre[:, SM_MI + h:SM_MI + h + 1]
        b_g = logf[:, SM_MF + h:SM_MF + h + 1]
        inter = b_g + m_all[:, h:h + 1]
        m_t = jnp.maximum(inter, i_g)
        w = jnp.exp(i_g - m_t)
        g_in = jnp.exp(inter - m_t)
        qh, kh, vh = _head(q, h), _head(k, h), _head(v, h)
        cm, nv = mc_ref[0, h], mn_ref[0, h:h + 1, :]
        qk = jnp.sum(qh * kh, axis=1, keepdims=True) * w
        num = qk * vh + g_in * _vec_mat(_row_to_col(qh), cm)
        den = qk + g_in * jnp.sum(qh * nv, axis=1, keepdims=True)
        hh = num / jnp.maximum(jnp.abs(den), jnp.exp(-m_t))
        mc_o[0, h] = g_in * cm + _row_to_col(kh * w) * vh
        mn_o[0, h:h + 1, :] = g_in * nv + w * kh
        m_out = jnp.where(lane == h, m_t, m_out)
        outs.append(_head_rms(_sigmoid(_head(og, h)) * hh))
    mm_o[0] = m_out
    ym_ref[0] = jnp.concatenate(outs, axis=1) * nwm_ref[...]
    zg = zg_ref[0]
    x = zg[:, 0:768]
    buf = gc_ref[0]
    conv = x * cw_ref[GDN_CONV - 1:GDN_CONV, :]
    for j in range(GDN_CONV - 1):
        conv = conv + buf[j:j + 1, :] * cw_ref[j:j + 1, :]
    gc_o[0, 0:GDN_CONV - 2, :] = buf[1:GDN_CONV - 1, :]
    gc_o[0, GDN_CONV - 2:GDN_CONV - 1, :] = x
    act = _silu(conv)
    q, k, v = act[:, 0:256], act[:, 256:512], act[:, 512:768]
    g_all = -jnp.exp(alog_ref[...]) * _softplus(sm + dtb_ref[...])
    beta_all = _sigmoid(sm)
    outs = []
    for h in range(N_HEADS):
        eg = jnp.exp(g_all[:, SM_GA + h:SM_GA + h + 1])
        beta = beta_all[:, SM_GB + h:SM_GB + h + 1]
        qh = _l2norm(_head(q, h)) * SCALE
        kh = _l2norm(_head(k, h))
        vh = _head(v, h)
        st = gs_ref[0, h]
        delta = vh * beta - _vec_mat(_row_to_col(kh * (beta * eg)), st)
        o = jnp.sum(qh * kh, axis=1, keepdims=True) * delta + _vec_mat(_row_to_col(qh * eg), st)
        gs_o[0, h] = eg * st + _row_to_col(kh) * delta
        outs.append(_head_rms(o))
    yg_ref[0] = jnp.concatenate(outs, axis=1) * nwg_ref[...] * _silu(zg[:, 768:1024])


def _mixers_step(z_ret, z_ml, z_gdn, small, cos, sin, mlb, dtb, alog, conv_w, nwr, nwm, nwg,
                 s_ret, ml_c, ml_n, ml_m, gdn_s, gdn_conv):
    b = z_ret.shape[0]
    row = lambda n: pl.BlockSpec((1, 1, n), lambda i: (i, 0, 0))
    vec = lambda n: pl.BlockSpec((1, n), lambda i: (0, 0))
    mat = pl.BlockSpec((1, N_HEADS, HEAD_DIM, HEAD_DIM), lambda i: (i, 0, 0, 0))
    nsp = pl.BlockSpec((1, N_HEADS, HEAD_DIM), lambda i: (i, 0, 0))
    csp = pl.BlockSpec((1, GDN_CONV - 1, 3 * D_HEADS), lambda i: (i, 0, 0))
    mshape = jax.ShapeDtypeStruct((b, N_HEADS, HEAD_DIM, HEAD_DIM), F32)
    yshape = jax.ShapeDtypeStruct((b, 1, D_HEADS), F32)
    return pl.pallas_call(
        _step_body,
        grid=(b,),
        in_specs=[row(1024), row(1024), row(1024), row(LANES), vec(LANES), vec(LANES), vec(LANES), vec(LANES), vec(LANES),
                  pl.BlockSpec((GDN_CONV, 3 * D_HEADS), lambda i: (0, 0)), vec(D_HEADS), vec(D_HEADS), vec(D_HEADS),
                  mat, mat, nsp, row(LANES), mat, csp],
        out_specs=[row(D_HEADS), row(D_HEADS), row(D_HEADS), mat, mat, nsp, row(LANES), mat, csp],
        out_shape=[yshape, yshape, yshape, mshape, mshape,
                   jax.ShapeDtypeStruct((b, N_HEADS, HEAD_DIM), F32), jax.ShapeDtypeStruct((b, 1, LANES), F32),
                   mshape, jax.ShapeDtypeStruct((b, GDN_CONV - 1, 3 * D_HEADS), F32)],
        compiler_params=_cparams("parallel"),
        name="mixers_step",
    )(z_ret, z_ml, z_gdn, small, cos, sin, mlb, dtb, alog, conv_w,
      nwr.reshape(1, -1), nwm.reshape(1, -1), nwg.reshape(1, -1), s_ret, ml_c, ml_n, ml_m, gdn_s, gdn_conv)


def _t5_bucket_np(n):
    exact = T5_BUCKETS // 2
    n = np.maximum(np.asarray(n, np.int64), 0)
    x = np.maximum(n, 1).astype(np.float32) / np.float32(exact)
    large = exact + (np.log(x) / np.float32(math.log(T5_MAX_DIST / exact)) * np.float32(T5_BUCKETS - exact)).astype(np.int32)
    return np.where(n < exact, n, np.minimum(large, T5_BUCKETS - 1)).astype(np.int32)


def _bucket_thresholds():
    b = _t5_bucket_np(np.arange(4 * T5_MAX_DIST))
    return [int(np.argmax(b >= k)) for k in range(T5_BUCKETS // 2 + 1, T5_BUCKETS)]


def _bias_table_body(rbt_ref, oh_ref, o_ref):
    o_ref[...] = _dot(rbt_ref[...], oh_ref[...].astype(F32), precision=HI)


def _bias_tables(rel_bias, dists):
    n = len(dists)
    npad = -(-n // 512) * 512
    onehot = np.zeros((T5_BUCKETS, npad), np.float32)
    onehot[_t5_bucket_np(dists), np.arange(n)] = 1.0
    rbt = jnp.pad(rel_bias.T, ((0, SUBLANES - N_HEADS), (0, 0)))
    tn = 512
    out = pl.pallas_call(
        _bias_table_body,
        grid=(npad // tn,),
        in_specs=[pl.BlockSpec((SUBLANES, T5_BUCKETS), lambda i: (0, 0)),
                  pl.BlockSpec((T5_BUCKETS, tn), lambda i: (0, i))],
        out_specs=pl.BlockSpec((SUBLANES, tn), lambda i: (0, i)),
        out_shape=jax.ShapeDtypeStruct((SUBLANES, npad), F32),
        name="bias_tables",
    )(rbt, jnp.asarray(onehot, BF16))
    return out[:, :n]


def _prep_cmp_w(cmp_w, cmp_pe):
    eye2 = jnp.eye(2, dtype=F32)

    def half(w):
        full = jnp.einsum('cjde,cf->jcdfe', w, eye2)
        return full.reshape(CMP_STRIDE * 2 * HEAD_DIM, 2 * HEAD_DIM)

    wab = jnp.concatenate([half(cmp_w[:, :CMP_STRIDE]), half(cmp_w[:, CMP_STRIDE:])], axis=1)
    pe = jnp.concatenate([cmp_pe[:CMP_STRIDE].reshape(1, -1), cmp_pe[CMP_STRIDE:].reshape(1, -1)], axis=1)
    return wab, jnp.pad(pe, ((0, SUBLANES - 1), (0, 0)))


def _pe_term(pe_ref, wab_ref):
    kin = CMP_STRIDE * 2 * HEAD_DIM
    t = (_dot(pe_ref[:, 0:kin], wab_ref[:, 0:LANES].astype(F32), precision=HI)
         + _dot(pe_ref[:, kin:2 * kin], wab_ref[:, LANES:2 * LANES].astype(F32), precision=HI))
    return t[0:1, :]


def _combine_cmp(parts, pe_term):
    rows = parts.shape[0]
    nxt = pltpu.roll(parts[:, LANES:2 * LANES], rows - 1, 0)
    r = lax.broadcasted_iota(jnp.int32, (rows, LANES), 0)
    return jnp.where(r < rows - 1, parts[:, 0:LANES] + nxt + pe_term, 0.0)


def _compress_seq_body(r_ref, wab_ref, pe_ref, o_ref):
    parts = _dot(_bf(r_ref[0]), _bf(wab_ref[...]))
    o_ref[0] = _combine_cmp(parts, _pe_term(pe_ref, wab_ref))


def _compress_seq(rows, wab, pe):
    b, n16, kin = rows.shape
    return pl.pallas_call(
        _compress_seq_body,
        grid=(b,),
        in_specs=[pl.BlockSpec((1, n16, kin), lambda i: (i, 0, 0)),
                  pl.BlockSpec(wab.shape, lambda i: (0, 0)),
                  pl.BlockSpec(pe.shape, lambda i: (0, 0))],
        out_specs=pl.BlockSpec((1, n16, LANES), lambda i: (i, 0, 0)),
        out_shape=jax.ShapeDtypeStruct((b, n16, LANES), F32),
        compiler_params=_cparams("parallel"),
        name="nsa_compress_seq",
    )(rows, wab, pe)


def _bias_from_dist(dist, rb_ref, h):
    n = jnp.maximum(dist, 0)
    exact = T5_BUCKETS // 2
    out = jnp.full(dist.shape, rb_ref[exact, h], F32)
    for k, thr in enumerate(_bucket_thresholds()):
        out = jnp.where(n >= thr, rb_ref[exact + 1 + k, h], out)
    for j in range(exact):
        out = jnp.where(n == j, rb_ref[j, h], out)
    return out


def _softmax_rows(s, valid):
    s = jnp.where(valid, s, NEG_INF)
    m = jnp.max(s, axis=1, keepdims=True)
    m = jnp.where(m == NEG_INF, 0.0, m)
    p = jnp.exp(s - m)
    den = jnp.sum(p, axis=1, keepdims=True)
    return p / jnp.where(den > 0, den, 1.0)


def _top_blocks(score, k):
    lane = lax.broadcasted_iota(jnp.int32, score.shape, 1)
    width = score.shape[1]
    sel = jnp.zeros(score.shape, jnp.bool_)
    for _ in range(k):
        m = jnp.max(score, axis=1, keepdims=True)
        idx = jnp.min(jnp.where(score == m, lane, width), axis=1, keepdims=True)
        pick = (lane == idx) & (m > NEG_INF)
        sel = sel | pick
        score = jnp.where(pick, NEG_INF, score)
    return sel


def _tile_mask(ok):
    return jnp.concatenate([jnp.where(ok, 1.0, 0.0)] * N_HEADS, axis=0) > 0.5


def _stack_heads(x):
    return jnp.concatenate([_head(x, h) for h in range(N_HEADS)], axis=0)


def _toeplitz(tab_ref, h, r0, nrow, rows):
    strip = tab_ref[h, pl.ds(r0, nrow), :]
    flat = jnp.concatenate([strip[r:r + 1, :] for r in range(nrow)], axis=1)
    rolled = pltpu.roll(jnp.broadcast_to(flat, (rows, nrow * LANES)), 0, 1, stride=1, stride_axis=0)
    return rolled[:, LANES:]


SLC_KC = 512


MASK_BIG = 1e30


def _top_blocks_t(score, k):
    row = lax.broadcasted_iota(jnp.int32, score.shape, 0)
    n = score.shape[0]
    sel = jnp.zeros(score.shape, jnp.bool_)
    for _ in range(k):
        m = jnp.max(score, axis=0, keepdims=True)
        idx = jnp.min(jnp.where(score == m, row, n), axis=0, keepdims=True)
        pick = (row == idx) & (m > NEG_INF)
        sel = sel | pick
        score = jnp.where(pick, NEG_INF, score)
    return sel


def _bucket_index(dist):
    n = jnp.maximum(dist, 0)
    exact = T5_BUCKETS // 2
    big = jnp.full(dist.shape, exact, jnp.int32)
    for thr in _bucket_thresholds():
        big = big + (n >= thr).astype(jnp.int32)
    return jnp.where(n < exact, n, big)


def _gather_bias(rbt_ref, h, bucket):
    rows, width = bucket.shape
    piece = min(width, LANES)
    tab = jnp.broadcast_to(rbt_ref[h:h + 1, 0:piece], (rows, piece))
    parts = [jnp.take_along_axis(tab, bucket[:, s:s + piece], axis=1) for s in range(0, width, piece)]
    return parts[0] if len(parts) == 1 else jnp.concatenate(parts, axis=1)


def _tile_rows(x, n=N_HEADS):
    return jnp.concatenate([x] * n, axis=0)


def _near_chunks():
    return (_bucket_thresholds()[-1] + SLC_KC - 2) // SLC_KC + 1


def _bias_tile(tab_ref, base, ncol, p0, lo, hi):
    r0 = (p0 - base) // LANES - 1
    bias = jnp.concatenate([_toeplitz(tab_ref, h, r0, ncol // LANES + 1, Q_BLOCK) for h in range(N_HEADS)], axis=0)
    dist = (base + lax.broadcasted_iota(jnp.int32, (Q_BLOCK, ncol), 0)
            - lax.broadcasted_iota(jnp.int32, (Q_BLOCK, ncol), 1))
    return bias + _tile_rows(jnp.where((dist >= lo) & (dist < hi), 0.0, -MASK_BIG))


def _nsa_seq_body(rb_ref, q_ref, sm_ref, kx_ref, vs_ref, kvw_ref, kvc_ref, tab_ref, rbt_ref, ovlt_ref, y_ref,
                  stile_ref, wtile_ref, *, seq, p0):
    qb = Q_BLOCK
    rows = N_HEADS * qb
    n16 = seq // CMP_STRIDE
    nblk = seq // SLC_BLOCK
    bi = pl.program_id(1)
    q0 = bi * qb

    @pl.when((pl.program_id(0) == 0) & (bi == 0))
    def _():
        per_chunk = SLC_KC // qb

        def slc_tile(t, carry):
            base = (t // _near_chunks()) * qb + (t % _near_chunks()) * SLC_KC
            stile_ref[t] = _bias_tile(tab_ref, base, SLC_KC, p0, 0, seq + SLC_KC)
            return carry

        def win_tile(t, carry):
            wtile_ref[t] = _bias_tile(tab_ref, t * qb, WINDOW + qb, p0, 0, WINDOW)
            return carry

        lax.fori_loop(0, per_chunk * _near_chunks(), slc_tile, 0)
        lax.fori_loop(0, WINDOW // qb + 1, win_tile, 0)
    qs = _stack_heads(q_ref[0]) * SCALE
    qsb = _bf(qs)
    qpos = q0 + lax.broadcasted_iota(jnp.int32, (qb, 1), 0)
    head_of_row = lax.broadcasted_iota(jnp.int32, (rows, 1), 0) // qb

    kvc = kvc_ref[0]
    s_c = _dot_nt(qsb, _bf(kvc[:, 0:HEAD_DIM]))
    band = WINDOW + qb
    sw = pl.multiple_of(jnp.maximum(q0 - WINDOW, 0), qb)
    kw = kvw_ref[0, pl.ds(sw, band), 0:HEAD_DIM]
    vw = kvw_ref[0, pl.ds(sw, band), HEAD_DIM:2 * HEAD_DIM]
    s_w = _dot_nt(qsb, kw)

    cmp_end = lax.broadcasted_iota(jnp.int32, (qb, n16), 1) * CMP_STRIDE + (CMP_LEN - 1)
    dist_c = qpos - cmp_end
    bucket_c = _bucket_index(dist_c)
    bias_c = jnp.concatenate([_gather_bias(rbt_ref, h, bucket_c) for h in range(N_HEADS)], axis=0)
    p_c = _softmax_rows(s_c + bias_c, _tile_mask(dist_c >= 0))
    p_cb = _bf(p_c)
    o_c = _dot(p_cb, _bf(kvc[:, HEAD_DIM:2 * HEAD_DIM]))
    p_r = p_cb.astype(F32)
    p_sum = p_r[0:qb] + p_r[qb:2 * qb] + p_r[2 * qb:3 * qb] + p_r[3 * qb:4 * qb]
    imp_t = _dot_nt(ovlt_ref[...], p_sum, precision=HI)

    base_w = q0 - sw
    sb_w = s_w + wtile_ref[base_w // qb]
    e_w = jnp.exp(sb_w - jnp.max(sb_w, axis=1, keepdims=True))
    o_w = _dot(_bf(e_w / jnp.sum(e_w, axis=1, keepdims=True)), vw)

    qlane = q0 + lax.broadcasted_iota(jnp.int32, (1, qb), 1)
    cur = qlane // SLC_BLOCK
    rowb = lax.broadcasted_iota(jnp.int32, (nblk, qb), 0)
    forced = (rowb == 0) | (rowb == cur) | (rowb == cur - 1)
    score = jnp.where(forced, jnp.inf, imp_t)
    score = jnp.where(rowb > cur, NEG_INF, score)
    sel = jnp.where(_top_blocks_t(score, SLC_TOPK), 1.0, 0.0).T
    negm = _bf(_tile_rows((sel - 1.0) * MASK_BIG))
    q_ext = jnp.concatenate([qsb, negm], axis=1)

    thr_far = _bucket_thresholds()[-1] + SLC_KC - 1
    n_chunks = q0 // SLC_KC + 1
    n_far = jnp.maximum(q0 - thr_far + SLC_KC, 0) // SLC_KC
    far_bias = jnp.zeros((rows, 1), F32)
    for h in range(N_HEADS):
        far_bias = jnp.where(head_of_row == h, rb_ref[T5_BUCKETS - 1, h], far_bias)
    n_near = _near_chunks()

    def scores(c):
        k0 = pl.multiple_of(c * SLC_KC, SLC_KC)
        kx = kx_ref[0, pl.ds(k0, SLC_KC), :]
        return q0 - k0, _dot_nt(q_ext, kx), vs_ref[0, pl.ds(k0, SLC_KC), :]

    def online(carry, s, vs):
        m, l, acc = carry
        m_new = jnp.maximum(m, jnp.max(s, axis=1, keepdims=True))
        alpha = jnp.exp(m - m_new)
        p = jnp.exp(s - m_new)
        return m_new, alpha * l + jnp.sum(p, axis=1, keepdims=True), alpha * acc + _dot(_bf(p), vs)

    def far_step(c, carry):
        _, s, vs = scores(c)
        return online(carry, s + far_bias, vs)

    def near_step(c, carry):
        base, s, vs = scores(c)
        tile = ((q0 % SLC_KC) // qb) * n_near + base // SLC_KC
        return online(carry, s + stile_ref[tile], vs)

    init = (jnp.full((rows, 1), -MASK_BIG / 10, F32), jnp.zeros((rows, 1), F32), jnp.zeros((rows, HEAD_DIM), F32))
    carry = lax.fori_loop(0, n_far, far_step, init)
    _, l_s, acc_s = lax.fori_loop(n_far, n_chunks, near_step, carry)
    o_s = acc_s / jnp.where(l_s > 0, l_s, 1.0)

    gates = _sigmoid(sm_ref[0])
    outs = []
    for h in range(N_HEADS):
        rs = slice(h * qb, (h + 1) * qb)
        g = lambda t: gates[:, SM_NG + 3 * h + t:SM_NG + 3 * h + t + 1]
        outs.append(g(0) * o_c[rs] + g(1) * o_s[rs] + g(2) * o_w[rs])
    y_ref[0] = jnp.concatenate(outs, axis=1)


def _nsa_seq(rel_bias, nq, small, nkv, kvc, tab, seq_p0):
    b, l, _ = nq.shape
    n16 = l // CMP_STRIDE
    nblk = l // SLC_BLOCK
    cmp_start = np.arange(n16)[None, :] * CMP_STRIDE
    slc_start = np.arange(nblk)[:, None] * SLC_BLOCK
    ovlt = ((cmp_start < slc_start + SLC_BLOCK) & (cmp_start + CMP_LEN > slc_start)
            & (np.arange(n16)[None, :] < n16 - 1)).astype(np.float32)
    block_id = (np.arange(l)[:, None] // SLC_BLOCK == np.arange(nblk)[None, :]).astype(np.float32)
    kx = _bf(jnp.concatenate([nkv[..., 2 * HEAD_DIM:3 * HEAD_DIM],
                              jnp.broadcast_to(jnp.asarray(block_id), (b, l, nblk))], axis=-1))
    vs = _bf(nkv[..., 3 * HEAD_DIM:4 * HEAD_DIM])
    kvw = _bf(nkv[..., 4 * HEAD_DIM:6 * HEAD_DIM])
    rbt = jnp.pad(rel_bias.T, ((0, SUBLANES - N_HEADS), (0, LANES - T5_BUCKETS)))
    nrows = tab.shape[1]
    const = lambda shape: pl.BlockSpec(shape, lambda i, j: (0,) * len(shape))
    seq_spec = lambda n: pl.BlockSpec((1, l, n), lambda i, j: (i, 0, 0))
    return pl.pallas_call(
        functools.partial(_nsa_seq_body, seq=l, p0=seq_p0),
        grid=(b, l // Q_BLOCK),
        in_specs=[pl.BlockSpec(memory_space=pltpu.SMEM),
                  pl.BlockSpec((1, Q_BLOCK, D_HEADS), lambda i, j: (i, j, 0)),
                  pl.BlockSpec((1, Q_BLOCK, LANES), lambda i, j: (i, j, 0)),
                  seq_spec(HEAD_DIM + nblk), seq_spec(HEAD_DIM), seq_spec(2 * HEAD_DIM),
                  pl.BlockSpec((1, n16, LANES), lambda i, j: (i, 0, 0)),
                  const((SUBLANES, nrows, LANES)), const((SUBLANES, LANES)), const((nblk, n16))],
        out_specs=pl.BlockSpec((1, Q_BLOCK, D_HEADS), lambda i, j: (i, j, 0)),
        out_shape=jax.ShapeDtypeStruct((b, l, D_HEADS), F32),
        scratch_shapes=[pltpu.VMEM(((SLC_KC // Q_BLOCK) * _near_chunks(), N_HEADS * Q_BLOCK, SLC_KC), F32),
                        pltpu.VMEM((WINDOW // Q_BLOCK + 1, N_HEADS * Q_BLOCK, WINDOW + Q_BLOCK), F32)],
        compiler_params=_cparams("arbitrary", "arbitrary"),
        name="nsa_seq",
    )(rel_bias, nq, small, kx, vs, kvw, kvc, tab, rbt, jnp.asarray(ovlt))


def _nsa_tab_seq_dists(seq):
    max_base = max(seq, SLC_KC + (_near_chunks() - 1) * SLC_KC)
    p0 = max_base + Q_BLOCK
    length = p0 + WINDOW + Q_BLOCK + LANES
    return p0, np.maximum(p0 - np.arange(length), 0)


PAGE_GROUP = 16
PAGES_PER_STEP = 32


def _pad_rows(x, rows=SUBLANES):
    return jnp.concatenate([x, jnp.zeros((rows - x.shape[0], x.shape[1]), x.dtype)], axis=0)


def _nsa_select_body(pt_ref, *refs, past, n_steps):
    del pt_ref
    pages = refs[:PAGES_PER_STEP]
    wab_ref, pe_ref, q_ref, tabc_ref, ovlt_ref, oc_ref, top_ref, parts_ref, rows_ref = refs[PAGES_PER_STEP:]
    step = pl.program_id(1)
    blocks_per_page = PAGE_SIZE // CMP_STRIDE
    group_rows = PAGE_GROUP * blocks_per_page
    n_groups = PAGES_PER_STEP // PAGE_GROUP
    for p, pg in enumerate(pages):
        g, pp = divmod(p, PAGE_GROUP)
        rows_ref[g, pp * PAGE_SIZE:(pp + 1) * PAGE_SIZE, :] = pg[...].reshape(2 * HEAD_DIM, PAGE_SIZE).T
    for g in range(n_groups):
        acc = jnp.zeros((group_rows, 2 * LANES), F32)
        for j in range(CMP_STRIDE):
            xj = rows_ref[g, pl.ds(j, group_rows, stride=CMP_STRIDE), :]
            acc = acc + _dot(_bf(xj), wab_ref[j * LANES:(j + 1) * LANES, :])
        row0 = pl.multiple_of((step * n_groups + g) * group_rows, group_rows)
        parts_ref[pl.ds(row0, group_rows), :] = acc

    @pl.when(step == n_steps - 1)
    def _():
        n16 = past // CMP_STRIDE
        n_cmp = n16 - 1
        cur = past // SLC_BLOCK
        pe_term = _pe_term(pe_ref, wab_ref)
        kvc = _combine_cmp(parts_ref[...], pe_term)
        qs = _pad_rows(_stack_heads(q_ref[0]) * SCALE)
        s_c = _dot_nt(_bf(qs), _bf(kvc[:, 0:HEAD_DIM])) + tabc_ref[...]
        lane = lax.broadcasted_iota(jnp.int32, (SUBLANES, n16), 1)
        p_c = _softmax_rows(s_c, lane < n_cmp)
        p_cb = _bf(p_c)
        o_c = _dot(p_cb, _bf(kvc[:, HEAD_DIM:2 * HEAD_DIM]))
        oc_ref[0] = jnp.concatenate([o_c[h:h + 1, :] for h in range(N_HEADS)], axis=1)
        imp_h = _dot_nt(ovlt_ref[...], p_cb)
        imp = imp_h[:, 0:1] + imp_h[:, 1:2] + imp_h[:, 2:3] + imp_h[:, 3:4]
        nrow = imp.shape[0]
        rowb = lax.broadcasted_iota(jnp.int32, (nrow, 1), 0)
        forced = (rowb == 0) | (rowb == cur) | (rowb == cur - 1)
        score = jnp.where(forced, jnp.inf, imp)
        score = jnp.where(rowb > cur, NEG_INF, score)
        out_lane = lax.broadcasted_iota(jnp.int32, (1, LANES), 1)
        top = jnp.zeros((1, LANES), jnp.int32)
        for r in range(SLC_TOPK):
            m = jnp.max(score, axis=0, keepdims=True)
            idx = jnp.min(jnp.where(score == m, rowb, nrow), axis=0, keepdims=True)
            top = jnp.where(out_lane == r, idx, top)
            score = jnp.where(rowb == idx, NEG_INF, score)
        top_ref[0] = top


def _nsa_select(layer, page_table, cache_t, wab_bf, pe, nq, tabc, past):
    b, n_pages = page_table.shape
    n_steps = n_pages // PAGES_PER_STEP
    n16 = past // CMP_STRIDE
    n_slc = past // SLC_BLOCK + 1
    nrow = -(-n_slc // SUBLANES) * SUBLANES
    cmp_start = np.arange(n16)[None, :] * CMP_STRIDE
    slc_start = np.arange(nrow)[:, None] * SLC_BLOCK
    ovlt = ((cmp_start < slc_start + SLC_BLOCK) & (cmp_start + CMP_LEN > slc_start)
            & (np.arange(n16)[None, :] < n16 - 1) & (np.arange(nrow)[:, None] < n_slc)).astype(np.float32)

    n_pool = cache_t.shape[1]

    def page_spec(p):
        def imap(i, s, pt):
            pg = pt[jnp.minimum(i, b - 1), jnp.minimum(s, n_steps - 1) * PAGES_PER_STEP + p]
            return (layer, jnp.clip(pg, 0, n_pool - 1), 0, 0, 0)
        return pl.BlockSpec((None, None, 2, HEAD_DIM, PAGE_SIZE), imap)

    const = lambda shape: pl.BlockSpec(shape, lambda i, s, pt: (0,) * len(shape))
    grid_spec = pltpu.PrefetchScalarGridSpec(
        num_scalar_prefetch=1,
        grid=(b, n_steps),
        in_specs=[page_spec(p) for p in range(PAGES_PER_STEP)]
        + [const(wab_bf.shape), const(pe.shape), pl.BlockSpec((1, 1, D_HEADS), lambda i, s, pt: (i, 0, 0)),
           const(tabc.shape), const(ovlt.shape)],
        out_specs=[pl.BlockSpec((1, 1, D_HEADS), lambda i, s, pt: (i, 0, 0)),
                   pl.BlockSpec((1, 1, LANES), lambda i, s, pt: (i, 0, 0))],
        scratch_shapes=[pltpu.VMEM((n16, 2 * LANES), F32),
                        pltpu.VMEM((PAGES_PER_STEP // PAGE_GROUP, PAGE_GROUP * PAGE_SIZE, 2 * HEAD_DIM), F32)],
    )
    return pl.pallas_call(
        functools.partial(_nsa_select_body, past=past, n_steps=n_steps),
        grid_spec=grid_spec,
        out_shape=[jax.ShapeDtypeStruct((b, 1, D_HEADS), F32), jax.ShapeDtypeStruct((b, 1, LANES), jnp.int32)],
        compiler_params=_cparams("parallel", "arbitrary"),
        name="nsa_select",
    )(page_table, *([cache_t] * PAGES_PER_STEP), wab_bf, pe, nq, tabc, jnp.asarray(ovlt, BF16))


def _nsa_attend_body(pt_ref, top_ref, *refs, past):
    del pt_ref
    pages = refs[:SLC_TOPK]
    (q_ref, sm_ref, new_ref, oc_ref, swa_ref, tabs_ref, tabw_ref, y_ref, swa_out) = refs[SLC_TOPK:]
    b = pl.program_id(0)
    new_blk = past // SLC_BLOCK
    blocks_per_page = PAGE_SIZE // SLC_BLOCK
    qs = _bf(_pad_rows(_stack_heads(q_ref[0]) * SCALE))
    new = new_ref[0]
    lane_t = lax.broadcasted_iota(jnp.int32, (HEAD_DIM, PAGE_SIZE), 1)
    lane = lax.broadcasted_iota(jnp.int32, (SUBLANES, PAGE_SIZE), 1)
    new_kt = jnp.where(lane_t == 0, _row_to_col(new[:, 2 * HEAD_DIM:3 * HEAD_DIM]), 0.0)
    new_vt = jnp.where(lane_t == 0, _row_to_col(new[:, 3 * HEAD_DIM:4 * HEAD_DIM]), 0.0)
    scores, oks, vts = [], [], []
    for s in range(SLC_TOPK):
        ti = jnp.clip(top_ref[b, s], 0, new_blk)
        is_new = ti == new_blk
        kt = jnp.where(is_new, new_kt, pages[s][0])
        vts.append(jnp.where(is_new, new_vt, pages[s][1]))
        scores.append(_dot(qs, _bf(kt)) + tabs_ref[ti])
        kpos = ti * SLC_BLOCK + lane % SLC_BLOCK
        ok = (lane // SLC_BLOCK == ti % blocks_per_page) & (kpos <= past)
        oks.append(jnp.where(ok, 1.0, 0.0))
    p_s = _softmax_rows(jnp.concatenate(scores, axis=1), jnp.concatenate(oks, axis=1) > 0.5)
    o_s = None
    for s in range(SLC_TOPK):
        t = _dot_nt(_bf(p_s[:, s * PAGE_SIZE:(s + 1) * PAGE_SIZE]), _bf(vts[s]))
        o_s = t if o_s is None else o_s + t

    wlane = lax.broadcasted_iota(jnp.int32, (HEAD_DIM, WINDOW), 1)
    win = []
    for c in range(2):
        col = _row_to_col(new[:, (4 + c) * HEAD_DIM:(5 + c) * HEAD_DIM])
        win.append(jnp.where(wlane == WINDOW - 1, col, pltpu.roll(swa_ref[c], WINDOW - 1, 1)))
        swa_out[c] = win[c]
    s_w = _dot(qs, _bf(win[0])) + tabw_ref[...]
    p_w = _softmax_rows(s_w, jnp.full(s_w.shape, True))
    o_w = _dot_nt(_bf(p_w), _bf(win[1]))

    gates = _sigmoid(sm_ref[0])
    o_c = oc_ref[0]
    outs = []
    for h in range(N_HEADS):
        g = lambda t: gates[:, SM_NG + 3 * h + t:SM_NG + 3 * h + t + 1]
        outs.append(g(0) * _head(o_c, h) + g(1) * o_s[h:h + 1, :] + g(2) * o_w[h:h + 1, :])
    y_ref[0] = jnp.concatenate(outs, axis=1)


def _nsa_attend(layer, page_table, top, cache_t, nq, small, nkv, o_c, swa_t, tabs, tabw, past):
    b = nq.shape[0]
    last_page = past // PAGE_SIZE - 1
    blocks_per_page = PAGE_SIZE // SLC_BLOCK

    n_pool = cache_t.shape[1]

    def page_spec(s):
        def imap(i, pt, tp):
            ii = jnp.minimum(i, b - 1)
            pg = jnp.clip(tp[ii, s] // blocks_per_page, 0, last_page)
            return (layer, jnp.clip(pt[ii, pg], 0, n_pool - 1), 1, 0, 0)
        return pl.BlockSpec((None, None, 2, HEAD_DIM, PAGE_SIZE), imap)

    row = lambda n: pl.BlockSpec((1, 1, n), lambda i, pt, tp: (i, 0, 0))
    const = lambda shape: pl.BlockSpec(shape, lambda i, pt, tp: (0,) * len(shape))
    grid_spec = pltpu.PrefetchScalarGridSpec(
        num_scalar_prefetch=2,
        grid=(b,),
        in_specs=[page_spec(s) for s in range(SLC_TOPK)]
        + [row(D_HEADS), row(LANES), row(6 * HEAD_DIM), row(D_HEADS),
           pl.BlockSpec((None, None, 2, HEAD_DIM, WINDOW), lambda i, pt, tp: (layer, i, 0, 0, 0)),
           const(tabs.shape), const(tabw.shape)],
        out_specs=[row(D_HEADS), pl.BlockSpec((None, 2, HEAD_DIM, WINDOW), lambda i, pt, tp: (i, 0, 0, 0))],
    )
    return pl.pallas_call(
        functools.partial(_nsa_attend_body, past=past),
        grid_spec=grid_spec,
        out_shape=[jax.ShapeDtypeStruct((b, 1, D_HEADS), F32),
                   jax.ShapeDtypeStruct((b, 2, HEAD_DIM, WINDOW), F32)],
        compiler_params=_cparams("parallel"),
        name="nsa_attend",
    )(page_table, top, *([cache_t] * SLC_TOPK), nq, small, nkv, o_c, swa_t, tabs, tabw)


SEQ_CHUNK = 64
SCAN_CHUNK = 256
ROW_TILE = 512
FFN_TILE = 512


def _lane_vec(pairs):
    v = jnp.zeros((1, LANES), F32)
    for off, vals in pairs:
        v = v.at[0, off:off + vals.shape[0]].set(vals.astype(F32))
    return v


def kernel(x_prompt, x_sample, cache_nsa_kv, cache_swa_kv, state_ret, state_mlstm_C, state_mlstm_n, state_mlstm_m, state_gdn, state_gdn_conv, state_ffn_conv, page_table, rel_bias, norm_pre_mix, w_in, mlstm_b_i, mlstm_b_f, gdn_conv_w, gdn_A_log, gdn_dt_bias, nsa_cmp_pe, nsa_cmp_w, ret_norm, mlstm_norm, gdn_norm, w_out, norm_post_mix, norm_pre_ffn, w_ffn_gate, w_ffn_up, ffn_conv_w, w_ffn_down, norm_post_ffn):
    depth = w_in.shape[0]
    bp, seq, _ = x_prompt.shape
    bs, dec_seq, _ = x_sample.shape
    assert dec_seq == 1 and seq % SLC_KC == 0 and seq >= WINDOW + Q_BLOCK
    n_pool = cache_nsa_kv.shape[1]
    past = page_table.shape[1] * PAGE_SIZE
    assert past >= WINDOW and past % SLC_BLOCK == 0 and page_table.shape[1] % PAGES_PER_STEP == 0

    p0, d_seq = _nsa_tab_seq_dists(seq)
    n16 = past // CMP_STRIDE
    n_slc = past // SLC_BLOCK + 1
    d_cmp = np.maximum(past - (np.arange(n16) * CMP_STRIDE + CMP_LEN - 1), 0)
    d_slc = np.maximum(past - np.arange(n_slc * SLC_BLOCK), 0)
    d_win = WINDOW - 1 - np.arange(WINDOW)
    tab = _bias_tables(rel_bias, np.concatenate([d_seq, d_cmp, d_slc, d_win]))
    o1, o2, o3 = len(d_seq), len(d_seq) + len(d_cmp), len(d_seq) + len(d_cmp) + len(d_slc)
    tab_seq = tab[:, :o1].reshape(SUBLANES, o1 // LANES, LANES)
    tabc = tab[:, o1:o2]
    tabs = jnp.transpose(tab[:, o2:o3].reshape(SUBLANES, n_slc, SLC_BLOCK), (1, 0, 2))
    tabs = jnp.concatenate([tabs, tabs], axis=2)
    tabw = tab[:, o3:]

    cache_t = jnp.transpose(cache_nsa_kv, (0, 1, 3, 4, 2))
    swa_t = jnp.transpose(cache_swa_kv, (0, 1, 3, 4, 2))
    cos_p, sin_p = _rope_tables(jnp.arange(seq))
    cos_s, sin_s = _rope_tables(jnp.arange(past, past + 1))

    xp = x_prompt
    xs = x_sample.reshape(bs, D_MODEL)
    p_states, s_states = [], []
    for l in range(depth):
        w_cat = _prep_w_in(w_in[l])
        w_out_bf = _bf(w_out[l])
        wg, wu, wd = _bf(w_ffn_gate[l]), _bf(w_ffn_up[l]), _bf(w_ffn_down[l])
        wab, pe = _prep_cmp_w(nsa_cmp_w[l], nsa_cmp_pe[l])
        ml_bias = _lane_vec([(SM_MI, mlstm_b_i[l]), (SM_MF, mlstm_b_f[l])])
        dtb = _lane_vec([(SM_GA, gdn_dt_bias[l])])
        alog = _lane_vec([(SM_GA, gdn_A_log[l])])

        z = _in_proj(xp.reshape(bp * seq, D_MODEL), norm_pre_mix[l], w_cat, ROW_TILE, seq=seq)
        z_ret, z_ml, z_gdn, z_nq, z_nkv, z_sm = [t.reshape(bp, seq, -1) for t in z[:-1]]
        nkv_t = z[-1].reshape(bp, 6, HEAD_DIM, seq)
        y_ret, st_ret = _retention_seq(z_ret, cos_p, sin_p, ret_norm[l], SCAN_CHUNK, bp)
        y_ml, st_c, st_n, st_m = _mlstm_seq(z_ml, z_sm, ml_bias, mlstm_norm[l], SCAN_CHUNK, bp)
        y_gdn, st_g = _gdn_seq(z_gdn, z_sm, gdn_conv_w[l], dtb, alog, gdn_norm[l], SEQ_CHUNK, bp)
        rows = z_nkv[..., 0:2 * HEAD_DIM].reshape(bp, seq // CMP_STRIDE, CMP_STRIDE * 2 * HEAD_DIM)
        kvc = _compress_seq(rows, wab, pe)
        y_nsa = _nsa_seq(rel_bias, z_nq, z_sm, z_nkv, kvc, tab_seq, p0)
        ys = [t.reshape(bp * seq, D_HEADS) for t in (y_ret, y_ml, y_gdn, y_nsa)]
        x1, h2 = _out_proj(xp.reshape(bp * seq, D_MODEL), ys, w_out_bf, norm_post_mix[l], norm_pre_ffn[l], ROW_TILE)
        ffn_tile = min(seq, FFN_TILE)
        xp, tail = _ffn_seq(x1.reshape(bp, seq, D_MODEL), h2.reshape(bp, seq, D_MODEL), wg, wu, wd,
                            ffn_conv_w[l], norm_post_ffn[l], ffn_tile)
        p_states.append((
            jnp.transpose(nkv_t[:, 0:4], (0, 3, 1, 2)),
            jnp.transpose(nkv_t[:, 4:6, :, seq - WINDOW:], (0, 3, 1, 2)),
            st_ret, st_c, st_n, st_m[:, 0, :N_HEADS], st_g,
            z_gdn[:, seq - (GDN_CONV - 1):, 0:3 * D_HEADS],
            tail[:, SUBLANES - (FFN_CONV - 1):, :]))

        z = _in_proj(xs, norm_pre_mix[l], w_cat, bs)
        z_ret, z_ml, z_gdn, z_nq, z_nkv, z_sm = [t.reshape(bs, 1, -1) for t in z]
        m_pad = jnp.pad(state_mlstm_m[l], ((0, 0), (0, LANES - N_HEADS))).reshape(bs, 1, LANES)
        (y_ret, y_ml, y_gdn, st_ret, st_c, st_n, st_m, st_g, st_gc) = _mixers_step(
            z_ret, z_ml, z_gdn, z_sm, cos_s, sin_s, ml_bias, dtb, alog, gdn_conv_w[l],
            ret_norm[l], mlstm_norm[l], gdn_norm[l],
            state_ret[l], state_mlstm_C[l], state_mlstm_n[l], m_pad, state_gdn[l], state_gdn_conv[l])
        o_c, top = _nsa_select(l, page_table, cache_t, _bf(wab), pe, z_nq, tabc, past)
        y_nsa, swa_new = _nsa_attend(l, page_table, top[:, 0, :SLC_TOPK], cache_t, z_nq, z_sm, z_nkv, o_c,
                                     swa_t, tabs, tabw, past)
        ys = [t.reshape(bs, D_HEADS) for t in (y_ret, y_ml, y_gdn, y_nsa)]
        x1, h2 = _out_proj(xs, ys, w_out_bf, norm_post_mix[l], norm_pre_ffn[l], bs)
        xs, g_new = _ffn_step(x1, h2, state_ffn_conv[l][:, 0], state_ffn_conv[l][:, 1], wg, wu, wd,
                              ffn_conv_w[l], norm_post_ffn[l])
        s_states.append((
            z_nkv[..., 0:4 * HEAD_DIM].reshape(bs, 1, 4, HEAD_DIM),
            jnp.transpose(swa_new, (0, 3, 1, 2)),
            st_ret, st_c, st_n, st_m[:, 0, :N_HEADS], st_g, st_gc,
            jnp.concatenate([state_ffn_conv[l][:, 1:], g_new[:, None, :]], axis=1)))

    stack = lambda states: tuple(jnp.stack([s[k] for s in states]) for k in range(len(states[0])))
    return (xp, xs.reshape(bs, 1, D_MODEL)) + stack(p_states) + stack(s_states)
```

```python
import functools
import math

import jax
import jax.numpy as jnp
import numpy as np
from jax import lax
from jax.experimental import pallas as pl
from jax.experimental.pallas import tpu as pltpu

F32 = jnp.float32
BF16 = jnp.bfloat16
HI = lax.Precision.HIGHEST

D_MODEL = 1024
HEAD_DIM = 64
N_HEADS = 4
D_HEADS = N_HEADS * HEAD_DIM
D_FF = 2816
GDN_CONV = 4
FFN_CONV = 3
PAGE_SIZE = 128
CMP_STRIDE = 16
CMP_LEN = 32
SLC_BLOCK = 64
SLC_TOPK = 16
WINDOW = 512
Q_BLOCK = 128
T5_BUCKETS = 32
T5_MAX_DIST = 1024
ROPE_BASE = 10000.0
EPS = 1e-6
SCALE = HEAD_DIM ** -0.5
IN_WIDTHS = (256, 256, 256, 256, 256, 256, 256, 4, 4, 256, 768, 4, 4, 256, 256, 384, 12)
LANES = 128
SUBLANES = 8
VMEM_LIMIT = 56 * 1024 * 1024
NEG_INF = float("-inf")

SM_MI, SM_MF, SM_GA, SM_GB, SM_NG = 0, 4, 8, 12, 16


def _cparams(*sem):
    return pltpu.CompilerParams(dimension_semantics=sem, vmem_limit_bytes=VMEM_LIMIT)


def _rms(x, g):
    return x * lax.rsqrt(jnp.mean(x * x, axis=-1, keepdims=True) + EPS) * g


def _dot(a, b, **kw):
    return jnp.dot(a, b, preferred_element_type=F32, **kw)


def _dot_nt(a, b, **kw):
    return lax.dot_general(a, b, (((1,), (1,)), ((), ())), preferred_element_type=F32, **kw)


def _dot_tn(a, b, **kw):
    return lax.dot_general(a, b, (((0,), (0,)), ((), ())), preferred_element_type=F32, **kw)


def _bf(x):
    return x.astype(BF16)


def _sigmoid(x):
    return 1.0 / (1.0 + jnp.exp(-x))


def _silu(x):
    return x * _sigmoid(x)


def _softplus(x):
    return jnp.maximum(x, 0.0) + jnp.log(1.0 + jnp.exp(-jnp.abs(x)))


def _head_rms(o):
    return o * lax.rsqrt(jnp.mean(o * o, axis=-1, keepdims=True) + EPS)


PROJ_WIDTHS = (1024, 1024, 1024, 256, 384, 128)


def _prep_w_in(w):
    wt = w.T
    parts, off = [], 0
    for wd in IN_WIDTHS:
        parts.append(wt[off:off + wd])
        off += wd
    (rq, rk, rv, rg, mq, mk, mv, mi, mf, mo, gqkv, ga, gbeta, gg, nq, nkv, ngate) = parts
    small = jnp.concatenate([mi, mf, ga, gbeta, ngate], axis=0)
    small = jnp.pad(small, ((0, LANES - small.shape[0]), (0, 0)))
    cat = jnp.concatenate([rq, rk, rv, rg, mq, mk, mv, mo, gqkv, gg, nq, nkv, small], axis=0)
    return cat.astype(BF16)


NKV_OFF = sum(PROJ_WIDTHS[:4])


def _in_proj_body(x_ref, g_ref, wt_ref, *out_refs, with_kv_t):
    h = _bf(_rms(x_ref[...], g_ref[...]))
    off = 0
    for ref in out_refs[:len(PROJ_WIDTHS)]:
        n = ref.shape[-1]
        ref[...] = _dot_nt(h, wt_ref[off:off + n, :])
        off += n
    if with_kv_t:
        out_refs[-1][0] = _dot_nt(wt_ref[NKV_OFF:NKV_OFF + PROJ_WIDTHS[4], :], h)


def _in_proj(x2d, g, w_cat_t, tm, seq=None):
    m = x2d.shape[0]
    ntot = sum(PROJ_WIDTHS)
    with_kv_t = seq is not None
    const = lambda shape: pl.BlockSpec(shape, lambda i: (0,) * len(shape), pipeline_mode=pl.Buffered(1))
    out_specs = [pl.BlockSpec((tm, n), lambda i: (i, 0)) for n in PROJ_WIDTHS]
    out_shape = [jax.ShapeDtypeStruct((m, n), F32) for n in PROJ_WIDTHS]
    if with_kv_t:
        per_seq = seq // tm
        nkv = PROJ_WIDTHS[4]
        out_specs.append(pl.BlockSpec((1, nkv, tm), lambda i: (i // per_seq, 0, i % per_seq)))
        out_shape.append(jax.ShapeDtypeStruct((m // seq, nkv, seq), F32))
    return pl.pallas_call(
        functools.partial(_in_proj_body, with_kv_t=with_kv_t),
        grid=(m // tm,),
        in_specs=[pl.BlockSpec((tm, D_MODEL), lambda i: (i, 0)), const((1, D_MODEL)), const((ntot, D_MODEL))],
        out_specs=out_specs,
        out_shape=out_shape,
        compiler_params=_cparams("parallel"),
        name="in_proj",
    )(x2d, g.reshape(1, D_MODEL), w_cat_t)


def _out_proj_body(x_ref, y0, y1, y2, y3, w_ref, gpost_ref, gpre_ref, x1_ref, h2_ref):
    acc = None
    for k, y in enumerate((y0, y1, y2, y3)):
        t = _dot(_bf(y[...]), w_ref[k * D_HEADS:(k + 1) * D_HEADS, :])
        acc = t if acc is None else acc + t
    x1 = x_ref[...] + _rms(acc, gpost_ref[...])
    x1_ref[...] = x1
    h2_ref[...] = _bf(_rms(x1, gpre_ref[...]))


def _out_proj(x2d, ys, w_out_bf, gpost, gpre, tm):
    m = x2d.shape[0]
    row = lambda n: pl.BlockSpec((tm, n), lambda i: (i, 0))
    vec = pl.BlockSpec((1, D_MODEL), lambda i: (0, 0))
    return pl.pallas_call(
        _out_proj_body,
        grid=(m // tm,),
        in_specs=[row(D_MODEL)] + [row(D_HEADS)] * 4
        + [pl.BlockSpec((D_MODEL, D_MODEL), lambda i: (0, 0)), vec, vec],
        out_specs=[row(D_MODEL), row(D_MODEL)],
        out_shape=[jax.ShapeDtypeStruct((m, D_MODEL), F32), jax.ShapeDtypeStruct((m, D_MODEL), BF16)],
        compiler_params=_cparams("parallel"),
        name="out_proj",
    )(x2d, *ys, w_out_bf, gpost.reshape(1, -1), gpre.reshape(1, -1))


FFN_TN = 256


def _gelu_tanh(x):
    c = 0.7978845608028654
    hx = 0.5 * x
    return hx + hx * jnp.tanh(x * (c + (0.044715 * c) * (x * x)))


def _ffn_seq_body(x1_ref, h2_ref, wg_ref, wu_ref, wd_ref, cw_ref, gpost_ref, x2_ref, tail_ref, carry_ref, act_ref,
                  *, tm):
    @pl.when(pl.program_id(1) == 0)
    def _():
        carry_ref[...] = jnp.zeros_like(carry_ref)

    h2 = h2_ref[0]
    row8 = lax.broadcasted_iota(jnp.int32, (SUBLANES, FFN_TN), 0)
    starts = list(range(0, D_FF, FFN_TN))
    up = lambda n0: (_dot(h2, wg_ref[:, n0:n0 + FFN_TN]), _dot(h2, wu_ref[:, n0:n0 + FFN_TN]))
    nxt = up(starts[0])
    for i, n0 in enumerate(starts):
        g, u = nxt
        if i + 1 < len(starts):
            nxt = up(starts[i + 1])
        prev = carry_ref[:, n0:n0 + FFN_TN]
        g1, g2 = pltpu.roll(g, 1, 0), pltpu.roll(g, 2, 0)
        top1 = jnp.where(row8 == 0, prev[7:8, :], g1[0:SUBLANES])
        top2 = jnp.where(row8 == 0, prev[6:7, :], jnp.where(row8 == 1, prev[7:8, :], g2[0:SUBLANES]))
        g1 = jnp.concatenate([top1, g1[SUBLANES:]], axis=0)
        g2 = jnp.concatenate([top2, g2[SUBLANES:]], axis=0)
        cw = cw_ref[:, n0:n0 + FFN_TN]
        gate = cw[0:1, :] * g2 + cw[1:2, :] * g1 + cw[2:3, :] * g
        a = _gelu_tanh(gate) * u
        act_ref[:, n0:n0 + FFN_TN] = _bf(a)
        carry_ref[:, n0:n0 + FFN_TN] = g[tm - SUBLANES:tm, :]
    acc = _dot(act_ref[...], wd_ref[...])
    x2_ref[0] = x1_ref[0] + _rms(acc, gpost_ref[...])
    tail_ref[0] = carry_ref[...]


def _ffn_seq(x1, h2, wg, wu, wd, cw, gpost, tm):
    b, l, _ = x1.shape
    const = lambda shape: pl.BlockSpec(shape, lambda i, j: (0,) * len(shape), pipeline_mode=pl.Buffered(1))
    return pl.pallas_call(
        functools.partial(_ffn_seq_body, tm=tm),
        grid=(b, l // tm),
        in_specs=[pl.BlockSpec((1, tm, D_MODEL), lambda i, j: (i, j, 0)),
                  pl.BlockSpec((1, tm, D_MODEL), lambda i, j: (i, j, 0)),
                  const((D_MODEL, D_FF)), const((D_MODEL, D_FF)), const((D_FF, D_MODEL)),
                  const((FFN_CONV, D_FF)), const((1, D_MODEL))],
        out_specs=[pl.BlockSpec((1, tm, D_MODEL), lambda i, j: (i, j, 0)),
                   pl.BlockSpec((1, SUBLANES, D_FF), lambda i, j: (i, 0, 0))],
        out_shape=[jax.ShapeDtypeStruct((b, l, D_MODEL), F32),
                   jax.ShapeDtypeStruct((b, SUBLANES, D_FF), F32)],
        scratch_shapes=[pltpu.VMEM((SUBLANES, D_FF), F32), pltpu.VMEM((tm, D_FF), BF16)],
        compiler_params=_cparams("parallel", "arbitrary"),
        name="ffn_seq",
    )(x1, h2, wg, wu, wd, cw, gpost.reshape(1, -1))


def _ffn_step_body(x1_ref, h2_ref, b0_ref, b1_ref, wg_ref, wu_ref, wd_ref, cw_ref, gpost_ref, x2_ref, g_ref):
    h2 = h2_ref[...]
    acc = jnp.zeros(x1_ref.shape, F32)
    for n0 in range(0, D_FF, FFN_TN):
        sl = slice(n0, n0 + FFN_TN)
        g = _dot(h2, wg_ref[:, sl])
        cw = cw_ref[:, sl]
        gate = cw[0:1, :] * b0_ref[:, sl] + cw[1:2, :] * b1_ref[:, sl] + cw[2:3, :] * g
        u = _dot(h2, wu_ref[:, sl])
        acc = acc + _dot(_bf(_gelu_tanh(gate) * u), wd_ref[sl, :])
        g_ref[:, sl] = g
    x2_ref[...] = x1_ref[...] + _rms(acc, gpost_ref[...])


def _ffn_step(x1, h2, b0, b1, wg, wu, wd, cw, gpost):
    m = x1.shape[0]
    return pl.pallas_call(
        _ffn_step_body,
        out_shape=[jax.ShapeDtypeStruct((m, D_MODEL), F32), jax.ShapeDtypeStruct((m, D_FF), F32)],
        compiler_params=pltpu.CompilerParams(vmem_limit_bytes=VMEM_LIMIT),
        name="ffn_step",
    )(x1, h2, b0, b1, wg, wu, wd, cw, gpost.reshape(1, -1))


def _rope_tables(pos):
    half = HEAD_DIM // 2
    inv = ROPE_BASE ** (-jnp.linspace(0.0, 1.0, half, dtype=F32))
    ang = pos.astype(F32)[:, None] * inv[None, :]
    return jnp.tile(jnp.cos(ang), (1, LANES // half)), jnp.tile(jnp.sin(ang), (1, LANES // half))


def _rope128(x, cos, sin):
    lane = lax.broadcasted_iota(jnp.int32, x.shape, 1)
    first = (lane % HEAD_DIM) < (HEAD_DIM // 2)
    other = jnp.where(first, -pltpu.roll(x, LANES - HEAD_DIM // 2, 1), pltpu.roll(x, HEAD_DIM // 2, 1))
    return x * cos + other * sin


def _rope256(x, cos, sin):
    return jnp.concatenate([_rope128(x[:, :LANES], cos, sin), _rope128(x[:, LANES:], cos, sin)], axis=1)


def _ret_log_decay(h):
    return math.log(1.0 - 2.0 ** (-5.0 - h))


def _head(x, h):
    return x[:, h * HEAD_DIM:(h + 1) * HEAD_DIM]


def _ret_body(z_ref, cos_ref, sin_ref, nw_ref, y_ref, s_ref, *, chunk):
    @pl.when(pl.program_id(1) == 0)
    def _():
        s_ref[...] = jnp.zeros_like(s_ref)

    nb = z_ref.shape[0]
    cos, sin = cos_ref[...], sin_ref[...]
    t = lax.broadcasted_iota(jnp.int32, (chunk, chunk), 0)
    s = lax.broadcasted_iota(jnp.int32, (chunk, chunk), 1)
    causal = t >= s
    diff = jnp.where(causal, t - s, 0).astype(F32)
    tcol = lax.broadcasted_iota(jnp.int32, (chunk, 1), 0).astype(F32)
    lgs = [_ret_log_decay(h) for h in range(N_HEADS)]
    dmat = [jnp.where(causal, jnp.exp(lg * diff), 0.0) for lg in lgs]
    xi = [jnp.exp(lg * (tcol + 1.0)) for lg in lgs]
    zeta = [jnp.exp(lg * (chunk - 1.0 - tcol)) for lg in lgs]
    probs = [(bb, h) for bb in range(nb) for h in range(N_HEADS)]
    qkv = []
    for bb in range(nb):
        q = _rope256(z_ref[bb, :, 0:256], cos, sin)
        k = _rope256(z_ref[bb, :, 256:512], cos, sin) * SCALE
        qkv.append((q, k, z_ref[bb, :, 512:768]))
    qh = [_head(qkv[bb][0], h) for bb, h in probs]
    kh = [_head(qkv[bb][1], h) for bb, h in probs]
    vh = [_bf(_head(qkv[bb][2], h)) for bb, h in probs]
    st = [s_ref[bb, h] for bb, h in probs]
    sc = [_dot_nt(_bf(q), _bf(k)) for q, k in zip(qh, kh)]
    cross = [_dot(_bf(q * xi[h]), _bf(s0)) for q, s0, (_, h) in zip(qh, st, probs)]
    upd = [_dot_tn(_bf(k * zeta[h]), v) for k, v, (_, h) in zip(kh, vh, probs)]
    o = [_dot(_bf(x * dmat[h]), v) + c for x, v, c, (_, h) in zip(sc, vh, cross, probs)]
    for i, (bb, h) in enumerate(probs):
        s_ref[bb, h] = st[i] * math.exp(lgs[h] * chunk) + upd[i]
    for bb in range(nb):
        outs = [_head_rms(o[bb * N_HEADS + h]) for h in range(N_HEADS)]
        y_ref[bb] = jnp.concatenate(outs, axis=1) * nw_ref[...] * _silu(z_ref[bb, :, 768:1024])


def _retention_seq(z_ret, cos, sin, nw, chunk, nb):
    b, l, _ = z_ret.shape
    return pl.pallas_call(
        functools.partial(_ret_body, chunk=chunk),
        grid=(b // nb, l // chunk),
        in_specs=[pl.BlockSpec((nb, chunk, 1024), lambda i, j: (i, j, 0)),
                  pl.BlockSpec((chunk, LANES), lambda i, j: (j, 0)),
                  pl.BlockSpec((chunk, LANES), lambda i, j: (j, 0)),
                  pl.BlockSpec((1, D_HEADS), lambda i, j: (0, 0))],
        out_specs=[pl.BlockSpec((nb, chunk, D_HEADS), lambda i, j: (i, j, 0)),
                   pl.BlockSpec((nb, N_HEADS, HEAD_DIM, HEAD_DIM), lambda i, j: (i, 0, 0, 0))],
        out_shape=[jax.ShapeDtypeStruct((b, l, D_HEADS), F32),
                   jax.ShapeDtypeStruct((b, N_HEADS, HEAD_DIM, HEAD_DIM), F32)],
        compiler_params=_cparams("parallel", "arbitrary"),
        name="retention_seq",
    )(z_ret, cos, sin, nw.reshape(1, -1))


def _lanes_to_rows(x, n=2 * SUBLANES):
    sel = (lax.broadcasted_iota(jnp.int32, (n, x.shape[1]), 0)
           == lax.broadcasted_iota(jnp.int32, (n, x.shape[1]), 1)).astype(F32)
    return _dot_nt(sel, x, precision=HI)


def _tri_incl(n):
    t = lax.broadcasted_iota(jnp.int32, (n, n), 0)
    s = lax.broadcasted_iota(jnp.int32, (n, n), 1)
    return t >= s


def _ml_body(z_ref, sm_ref, bias_ref, nw_ref, y_ref, c_ref, n_ref, m_ref, *, chunk):
    @pl.when(pl.program_id(1) == 0)
    def _():
        c_ref[...] = jnp.zeros_like(c_ref)
        n_ref[...] = jnp.zeros_like(n_ref)
        m_ref[...] = jnp.zeros_like(m_ref)

    nb = z_ref.shape[0]
    incl = _tri_incl(chunk)
    tri = incl.astype(F32)
    probs = [(bb, h) for bb in range(nb) for h in range(N_HEADS)]
    gate = []
    for bb in range(nb):
        pre = sm_ref[bb] + bias_ref[...]
        logf = jnp.minimum(pre, 0.0) - jnp.log(1.0 + jnp.exp(-jnp.abs(pre)))
        bcum = _dot(tri, logf, precision=HI)
        gate.append((pre, bcum, _lanes_to_rows(pre), _lanes_to_rows(bcum), m_ref[bb]))
    qh = [_head(z_ref[bb, :, 0:256], h) for bb, h in probs]
    kh = [_head(z_ref[bb, :, 256:512], h) * SCALE for bb, h in probs]
    vh = [_bf(_head(z_ref[bb, :, 512:768], h)) for bb, h in probs]
    cm = [c_ref[bb, h] for bb, h in probs]
    nv = [n_ref[bb, h:h + 1, :] for bb, h in probs]
    qk_raw = [_dot_nt(_bf(q), _bf(k)) for q, k in zip(qh, kh)]
    q_c = [_dot(_bf(q), _bf(c)) for q, c in zip(qh, cm)]
    i_col, b_col, inter, dlog = [], [], [], []
    for bb, h in probs:
        pre, bcum, pre_t, bcum_t, m_all = gate[bb]
        i_col.append(pre[:, SM_MI + h:SM_MI + h + 1])
        b_col.append(bcum[:, SM_MF + h:SM_MF + h + 1])
        inter.append(b_col[-1] + m_all[:, h:h + 1])
        dlog.append(jnp.where(incl, b_col[-1] - bcum_t[SM_MF + h:SM_MF + h + 1, :]
                              + pre_t[SM_MI + h:SM_MI + h + 1, :], NEG_INF))
    dmax = [jnp.max(x, axis=1, keepdims=True) for x in dlog]
    q_n = [jnp.sum(q * n, axis=1, keepdims=True) for q, n in zip(qh, nv)]
    stab = []
    for i in range(len(probs)):
        m_t = jnp.maximum(inter[i], dmax[i])
        m_end = m_t[chunk - 1:chunk, :]
        stab.append(dict(m_t=m_t, w=jnp.exp(dlog[i] - m_t), g_in=jnp.exp(inter[i] - m_t), m_end=m_end,
                         w_end=jnp.exp(b_col[i][chunk - 1:chunk, :] - b_col[i] + i_col[i] - m_end),
                         g_end=jnp.exp(inter[i][chunk - 1:chunk, :] - m_end)))
    qk = [x * d["w"] for x, d in zip(qk_raw, stab)]
    num = [_dot(_bf(x), v) + d["g_in"] * c for x, v, d, c in zip(qk, vh, stab, q_c)]
    upd = [_dot_tn(_bf(k * d["w_end"]), v) for k, d, v in zip(kh, stab, vh)]
    qk_sum = [jnp.sum(x, axis=1, keepdims=True) for x in qk]
    k_sum = [jnp.sum(d["w_end"] * k, axis=0, keepdims=True) for d, k in zip(stab, kh)]
    hh = []
    for i, (bb, h) in enumerate(probs):
        d = stab[i]
        den = qk_sum[i] + d["g_in"] * q_n[i]
        hh.append(num[i] / jnp.maximum(jnp.abs(den), jnp.exp(-d["m_t"])))
        c_ref[bb, h] = d["g_end"] * cm[i] + upd[i]
        n_ref[bb, h:h + 1, :] = d["g_end"] * nv[i] + k_sum[i]
    lane = lax.broadcasted_iota(jnp.int32, (1, LANES), 1)
    for bb in range(nb):
        m_out = jnp.zeros((1, LANES), F32)
        outs = []
        for h in range(N_HEADS):
            i = bb * N_HEADS + h
            m_out = jnp.where(lane == h, stab[i]["m_end"], m_out)
            outs.append(_head_rms(_sigmoid(_head(z_ref[bb, :, 768:1024], h)) * hh[i]))
        m_ref[bb] = m_out
        y_ref[bb] = jnp.concatenate(outs, axis=1) * nw_ref[...]


def _mlstm_seq(z_ml, small, bias_vec, nw, chunk, nb):
    b, l, _ = z_ml.shape
    return pl.pallas_call(
        functools.partial(_ml_body, chunk=chunk),
        grid=(b // nb, l // chunk),
        in_specs=[pl.BlockSpec((nb, chunk, 1024), lambda i, j: (i, j, 0)),
                  pl.BlockSpec((nb, chunk, LANES), lambda i, j: (i, j, 0)),
                  pl.BlockSpec((1, LANES), lambda i, j: (0, 0)),
                  pl.BlockSpec((1, D_HEADS), lambda i, j: (0, 0))],
        out_specs=[pl.BlockSpec((nb, chunk, D_HEADS), lambda i, j: (i, j, 0)),
                   pl.BlockSpec((nb, N_HEADS, HEAD_DIM, HEAD_DIM), lambda i, j: (i, 0, 0, 0)),
                   pl.BlockSpec((nb, N_HEADS, HEAD_DIM), lambda i, j: (i, 0, 0)),
                   pl.BlockSpec((nb, 1, LANES), lambda i, j: (i, 0, 0))],
        out_shape=[jax.ShapeDtypeStruct((b, l, D_HEADS), F32),
                   jax.ShapeDtypeStruct((b, N_HEADS, HEAD_DIM, HEAD_DIM), F32),
                   jax.ShapeDtypeStruct((b, N_HEADS, HEAD_DIM), F32),
                   jax.ShapeDtypeStruct((b, 1, LANES), F32)],
        compiler_params=_cparams("parallel", "arbitrary"),
        name="mlstm_seq",
    )(z_ml, small, bias_vec, nw.reshape(1, -1))


def _split_bf16(x):
    hi = _bf(x)
    return hi, _bf(x - hi.astype(F32))


def _dot3(a, b):
    (ah, al), (bh, bl) = a, b
    return _dot(ah, bh) + (_dot(ah, bl) + _dot(al, bh))


def _l2norm(x):
    return x * lax.rsqrt(jnp.sum(x * x, axis=-1, keepdims=True) + EPS)


def _gdn_body(z_ref, sm_ref, cw_ref, dtb_ref, alog_ref, nw_ref, y_ref, s_ref, buf_ref, *, chunk):
    @pl.when(pl.program_id(1) == 0)
    def _():
        s_ref[...] = jnp.zeros_like(s_ref)
        buf_ref[:, 0:SUBLANES, :] = jnp.zeros((buf_ref.shape[0], SUBLANES, 3 * D_HEADS), F32)

    nb = z_ref.shape[0]
    rows = z_ref.shape[1]
    nc = rows // chunk
    incl = _tri_incl(chunk)
    strict = lax.broadcasted_iota(jnp.int32, (chunk, chunk), 0) > lax.broadcasted_iota(jnp.int32, (chunk, chunk), 1)
    eye = (lax.broadcasted_iota(jnp.int32, (chunk, chunk), 0)
           == lax.broadcasted_iota(jnp.int32, (chunk, chunk), 1)).astype(F32)
    tri = incl.astype(F32)
    gates, acts = {}, []
    for bb in range(nb):
        buf_ref[bb, SUBLANES:SUBLANES + rows, :] = z_ref[bb, :, 0:768]
        conv = None
        for j in range(GDN_CONV):
            term = buf_ref[bb, pl.ds(SUBLANES - (GDN_CONV - 1) + j, rows), :] * cw_ref[j:j + 1, :]
            conv = term if conv is None else conv + term
        buf_ref[bb, 0:SUBLANES, :] = buf_ref[bb, rows:rows + SUBLANES, :]
        acts.append(_silu(conv))
        sm = sm_ref[bb]
        g_all = -jnp.exp(alog_ref[...]) * _softplus(sm + dtb_ref[...])
        beta_all = _sigmoid(sm)
        for cc in range(nc):
            rs = slice(cc * chunk, (cc + 1) * chunk)
            gcum = _dot(tri, g_all[rs], precision=HI)
            gates[bb, cc] = (gcum, _lanes_to_rows(gcum), beta_all[rs])

    probs = [(bb, cc, h) for cc in range(nc) for bb in range(nb) for h in range(N_HEADS)]
    pre = []
    for bb, cc, h in probs:
        rs = slice(cc * chunk, (cc + 1) * chunk)
        gcum, gcum_t, beta_all = gates[bb, cc]
        g_col = gcum[:, SM_GA + h:SM_GA + h + 1]
        g_row = gcum_t[SM_GA + h:SM_GA + h + 1, :]
        beta = beta_all[:, SM_GB + h:SM_GB + h + 1]
        decay = jnp.where(incl, jnp.exp(jnp.where(incl, g_col - g_row, 0.0)), 0.0)
        act = acts[bb]
        qh = _l2norm(_head(act[rs, 0:256], h)) * SCALE
        kh = _l2norm(_head(act[rs, 256:512], h))
        vh = _head(act[rs, 512:768], h)
        eg = jnp.exp(g_col)
        pre.append(dict(g_col=g_col, beta=beta, decay=decay, qh=qh, kh=kh, kb=_bf(kh), eg=eg,
                        rhs=jnp.concatenate([vh * beta, kh * (beta * eg)], axis=1)))
    a_l = [jnp.where(strict, d["beta"] * _dot_nt(d["kb"], d["kb"]) * d["decay"], 0.0) for d in pre]
    p_l = [eye - a for a in a_l]
    pw_s = [_split_bf16(a) for a in a_l]
    lvl = 2
    while lvl < chunk:
        pw_s = [_split_bf16(_dot3(x, x)) for x in pw_s]
        p_l = [p + _dot3(_split_bf16(p), x) for p, x in zip(p_l, pw_s)]
        lvl *= 2
    uw_l = [_dot3(_split_bf16(p), _split_bf16(d["rhs"])) for p, d in zip(p_l, pre)]
    qk_l = [_dot_nt(_bf(d["qh"]), d["kb"]) * d["decay"] for d in pre]
    per_chunk = nb * N_HEADS
    st_l = [s_ref[bb, h] for bb, cc, h in probs[:per_chunk]]
    o_all = []
    for cc in range(nc):
        sl = slice(cc * per_chunk, (cc + 1) * per_chunk)
        stb_l = [_bf(st) for st in st_l]
        delta_l = [uw[:, 0:HEAD_DIM] - _dot(_bf(uw[:, HEAD_DIM:]), stb) for uw, stb in zip(uw_l[sl], stb_l)]
        o_all += [_dot(_bf(qk), _bf(delta)) + _dot(_bf(d["qh"] * d["eg"]), stb)
                  for qk, delta, d, stb in zip(qk_l[sl], delta_l, pre[sl], stb_l)]
        new_st = []
        for d, st, delta in zip(pre[sl], st_l, delta_l):
            g_end = d["g_col"][chunk - 1:chunk, :]
            w_end = jnp.exp(g_end - d["g_col"])
            new_st.append(jnp.exp(g_end) * st + _dot_tn(_bf(d["kh"] * w_end), _bf(delta)))
        st_l = new_st
    for (bb, cc, h), st in zip(probs[:per_chunk], st_l):
        s_ref[bb, h] = st
    for bb in range(nb):
        for cc in range(nc):
            outs = [_head_rms(o_all[cc * per_chunk + bb * N_HEADS + h]) for h in range(N_HEADS)]
            rs = slice(cc * chunk, (cc + 1) * chunk)
            y_ref[bb, rs, :] = jnp.concatenate(outs, axis=1) * nw_ref[...] * _silu(z_ref[bb, rs, 768:1024])


def _gdn_seq(z_gdn, small, conv_w, dtb_vec, alog_vec, nw, chunk, nb, nc):
    b, l, _ = z_gdn.shape
    rows = nc * chunk
    return pl.pallas_call(
        functools.partial(_gdn_body, chunk=chunk),
        grid=(b // nb, l // rows),
        in_specs=[pl.BlockSpec((nb, rows, 1024), lambda i, j: (i, j, 0)),
                  pl.BlockSpec((nb, rows, LANES), lambda i, j: (i, j, 0)),
                  pl.BlockSpec((GDN_CONV, 3 * D_HEADS), lambda i, j: (0, 0)),
                  pl.BlockSpec((1, LANES), lambda i, j: (0, 0)),
                  pl.BlockSpec((1, LANES), lambda i, j: (0, 0)),
                  pl.BlockSpec((1, D_HEADS), lambda i, j: (0, 0))],
        out_specs=[pl.BlockSpec((nb, rows, D_HEADS), lambda i, j: (i, j, 0)),
                   pl.BlockSpec((nb, N_HEADS, HEAD_DIM, HEAD_DIM), lambda i, j: (i, 0, 0, 0))],
        out_shape=[jax.ShapeDtypeStruct((b, l, D_HEADS), F32),
                   jax.ShapeDtypeStruct((b, N_HEADS, HEAD_DIM, HEAD_DIM), F32)],
        scratch_shapes=[pltpu.VMEM((nb, SUBLANES + rows, 3 * D_HEADS), F32)],
        compiler_params=_cparams("parallel", "arbitrary"),
        name="gdn_seq",
    )(z_gdn, small, conv_w, dtb_vec, alog_vec, nw.reshape(1, -1))


STEP_SEQS = 1


def _row_to_col(row):
    n = row.shape[1]
    eye = lax.broadcasted_iota(jnp.int32, (n, n), 0) == lax.broadcasted_iota(jnp.int32, (n, n), 1)
    return jnp.sum(jnp.where(eye, jnp.broadcast_to(row, (n, n)), 0.0), axis=1, keepdims=True)


def _vec_mat(col, mat):
    return jnp.sum(col * mat, axis=0, keepdims=True)


def _step_body(*refs):
    writes = []
    for bb in range(refs[0].shape[0]):
        _step_one(bb, writes, *refs)
    for ref, idx, val in writes:
        ref[idx] = val


def _step_one(bb, writes, zr_ref, zm_ref, zg_ref, sm_ref, cos_ref, sin_ref, mlb_ref, dtb_ref, alog_ref, cw_ref,
              nwr_ref, nwm_ref, nwg_ref, sr_ref, mc_ref, mn_ref, mm_ref, gs_ref, gc_ref,
              yr_ref, ym_ref, yg_ref, sr_o, mc_o, mn_o, mm_o, gs_o, gc_o):
    put = lambda ref, idx, val: writes.append((ref, idx, val))
    sm = sm_ref[bb]
    lane = lax.broadcasted_iota(jnp.int32, (1, LANES), 1)
    zr = zr_ref[bb]
    cos, sin = cos_ref[...], sin_ref[...]
    q = _rope256(zr[:, 0:256], cos, sin)
    k = _rope256(zr[:, 256:512], cos, sin) * SCALE
    v = zr[:, 512:768]
    outs = []
    for h in range(N_HEADS):
        gam = math.exp(_ret_log_decay(h))
        qh, kh, vh = _head(q, h), _head(k, h), _head(v, h)
        st = sr_ref[bb, h]
        o = jnp.sum(qh * kh, axis=1, keepdims=True) * vh + _vec_mat(_row_to_col(qh) * gam, st)
        put(sr_o, (bb, h), st * gam + _row_to_col(kh) * vh)
        outs.append(_head_rms(o))
    put(yr_ref, bb, jnp.concatenate(outs, axis=1) * nwr_ref[...] * _silu(zr[:, 768:1024]))
    zm = zm_ref[bb]
    q, k, v, og = zm[:, 0:256], zm[:, 256:512] * SCALE, zm[:, 512:768], zm[:, 768:1024]
    pre = sm + mlb_ref[...]
    logf = jnp.minimum(pre, 0.0) - jnp.log(1.0 + jnp.exp(-jnp.abs(pre)))
    m_all = mm_ref[bb]
    m_out = jnp.zeros((1, LANES), F32)
    outs = []
    for h in range(N_HEADS):
        i_g = pre[:, SM_MI + h:SM_MI + h + 1]
        b_g = logf[:, SM_MF + h:SM_MF + h + 1]
        inter = b_g + m_all[:, h:h + 1]
        m_t = jnp.maximum(inter, i_g)
        w = jnp.exp(i_g - m_t)
        g_in = jnp.exp(inter - m_t)
        qh, kh, vh = _head(q, h), _head(k, h), _head(v, h)
        cm, nv = mc_ref[bb, h], mn_ref[bb, h:h + 1, :]
        qk = jnp.sum(qh * kh, axis=1, keepdims=True) * w
        num = qk * vh + g_in * _vec_mat(_row_to_col(qh), cm)
        den = qk + g_in * jnp.sum(qh * nv, axis=1, keepdims=True)
        hh = num / jnp.maximum(jnp.abs(den), jnp.exp(-m_t))
        put(mc_o, (bb, h), g_in * cm + _row_to_col(kh * w) * vh)
        put(mn_o, (bb, slice(h, h + 1), slice(None)), g_in * nv + w * kh)
        m_out = jnp.where(lane == h, m_t, m_out)
        outs.append(_head_rms(_sigmoid(_head(og, h)) * hh))
    put(mm_o, bb, m_out)
    put(ym_ref, bb, jnp.concatenate(outs, axis=1) * nwm_ref[...])
    zg = zg_ref[bb]
    x = zg[:, 0:768]
    buf = gc_ref[bb]
    conv = x * cw_ref[GDN_CONV - 1:GDN_CONV, :]
    for j in range(GDN_CONV - 1):
        conv = conv + buf[j:j + 1, :] * cw_ref[j:j + 1, :]
    put(gc_o, (bb, slice(0, GDN_CONV - 2), slice(None)), buf[1:GDN_CONV - 1, :])
    put(gc_o, (bb, slice(GDN_CONV - 2, GDN_CONV - 1), slice(None)), x)
    act = _silu(conv)
    q, k, v = act[:, 0:256], act[:, 256:512], act[:, 512:768]
    g_all = -jnp.exp(alog_ref[...]) * _softplus(sm + dtb_ref[...])
    beta_all = _sigmoid(sm)
    outs = []
    for h in range(N_HEADS):
        eg = jnp.exp(g_all[:, SM_GA + h:SM_GA + h + 1])
        beta = beta_all[:, SM_GB + h:SM_GB + h + 1]
        qh = _l2norm(_head(q, h)) * SCALE
        kh = _l2norm(_head(k, h))
        vh = _head(v, h)
        st = gs_ref[bb, h]
        delta = vh * beta - _vec_mat(_row_to_col(kh * (beta * eg)), st)
        o = jnp.sum(qh * kh, axis=1, keepdims=True) * delta + _vec_mat(_row_to_col(qh * eg), st)
        put(gs_o, (bb, h), eg * st + _row_to_col(kh) * delta)
        outs.append(_head_rms(o))
    put(yg_ref, bb, jnp.concatenate(outs, axis=1) * nwg_ref[...] * _silu(zg[:, 768:1024]))


def _mixers_step(z_ret, z_ml, z_gdn, small, cos, sin, mlb, dtb, alog, conv_w, nwr, nwm, nwg,
                 s_ret, ml_c, ml_n, ml_m, gdn_s, gdn_conv):
    b = z_ret.shape[0]
    nb = STEP_SEQS if b % STEP_SEQS == 0 else 1
    row = lambda n: pl.BlockSpec((nb, 1, n), lambda i: (i, 0, 0))
    vec = lambda n: pl.BlockSpec((1, n), lambda i: (0, 0))
    mat = pl.BlockSpec((nb, N_HEADS, HEAD_DIM, HEAD_DIM), lambda i: (i, 0, 0, 0))
    nsp = pl.BlockSpec((nb, N_HEADS, HEAD_DIM), lambda i: (i, 0, 0))
    csp = pl.BlockSpec((nb, GDN_CONV - 1, 3 * D_HEADS), lambda i: (i, 0, 0))
    mshape = jax.ShapeDtypeStruct((b, N_HEADS, HEAD_DIM, HEAD_DIM), F32)
    yshape = jax.ShapeDtypeStruct((b, 1, D_HEADS), F32)
    return pl.pallas_call(
        _step_body,
        grid=(b // nb,),
        in_specs=[row(1024), row(1024), row(1024), row(LANES), vec(LANES), vec(LANES), vec(LANES), vec(LANES), vec(LANES),
                  pl.BlockSpec((GDN_CONV, 3 * D_HEADS), lambda i: (0, 0)), vec(D_HEADS), vec(D_HEADS), vec(D_HEADS),
                  mat, mat, nsp, row(LANES), mat, csp],
        out_specs=[row(D_HEADS), row(D_HEADS), row(D_HEADS), mat, mat, nsp, row(LANES), mat, csp],
        out_shape=[yshape, yshape, yshape, mshape, mshape,
                   jax.ShapeDtypeStruct((b, N_HEADS, HEAD_DIM), F32), jax.ShapeDtypeStruct((b, 1, LANES), F32),
                   mshape, jax.ShapeDtypeStruct((b, GDN_CONV - 1, 3 * D_HEADS), F32)],
        compiler_params=_cparams("parallel"),
        name="mixers_step",
    )(z_ret, z_ml, z_gdn, small, cos, sin, mlb, dtb, alog, conv_w,
      nwr.reshape(1, -1), nwm.reshape(1, -1), nwg.reshape(1, -1), s_ret, ml_c, ml_n, ml_m, gdn_s, gdn_conv)


def _t5_bucket_np(n):
    exact = T5_BUCKETS // 2
    n = np.maximum(np.asarray(n, np.int64), 0)
    x = np.maximum(n, 1).astype(np.float32) / np.float32(exact)
    large = exact + (np.log(x) / np.float32(math.log(T5_MAX_DIST / exact)) * np.float32(T5_BUCKETS - exact)).astype(np.int32)
    return np.where(n < exact, n, np.minimum(large, T5_BUCKETS - 1)).astype(np.int32)


def _bucket_thresholds():
    b = _t5_bucket_np(np.arange(4 * T5_MAX_DIST))
    return [int(np.argmax(b >= k)) for k in range(T5_BUCKETS // 2 + 1, T5_BUCKETS)]


def _bias_table_body(rbt_ref, oh_ref, o_ref):
    o_ref[...] = _dot(rbt_ref[...], oh_ref[...].astype(F32), precision=HI)


def _bias_tables(rel_bias, dists):
    n = len(dists)
    tn = 4096
    npad = -(-n // tn) * tn
    onehot = np.zeros((T5_BUCKETS, npad), np.float32)
    onehot[_t5_bucket_np(dists), np.arange(n)] = 1.0
    rbt = jnp.pad(rel_bias.T, ((0, SUBLANES - N_HEADS), (0, 0)))
    out = pl.pallas_call(
        _bias_table_body,
        grid=(npad // tn,),
        in_specs=[pl.BlockSpec((SUBLANES, T5_BUCKETS), lambda i: (0, 0)),
                  pl.BlockSpec((T5_BUCKETS, tn), lambda i: (0, i))],
        out_specs=pl.BlockSpec((SUBLANES, tn), lambda i: (0, i)),
        out_shape=jax.ShapeDtypeStruct((SUBLANES, npad), F32),
        name="bias_tables",
    )(rbt, jnp.asarray(onehot, BF16))
    return out[:, :n]


def _prep_cmp_w(cmp_w, cmp_pe):
    eye2 = jnp.eye(2, dtype=F32)

    def half(w):
        full = jnp.einsum('cjde,cf->jcdfe', w, eye2)
        return full.reshape(CMP_STRIDE * 2 * HEAD_DIM, 2 * HEAD_DIM)

    wab = jnp.concatenate([half(cmp_w[:, :CMP_STRIDE]), half(cmp_w[:, CMP_STRIDE:])], axis=1)
    pe = jnp.concatenate([cmp_pe[:CMP_STRIDE].reshape(1, -1), cmp_pe[CMP_STRIDE:].reshape(1, -1)], axis=1)
    return wab, jnp.pad(pe, ((0, SUBLANES - 1), (0, 0)))


def _pe_term(pe_ref, wab_ref):
    kin = CMP_STRIDE * 2 * HEAD_DIM
    t = (_dot(pe_ref[:, 0:kin], wab_ref[:, 0:LANES].astype(F32), precision=HI)
         + _dot(pe_ref[:, kin:2 * kin], wab_ref[:, LANES:2 * LANES].astype(F32), precision=HI))
    return t[0:1, :]


def _combine_cmp(parts, pe_term):
    rows = parts.shape[0]
    nxt = pltpu.roll(parts[:, LANES:2 * LANES], rows - 1, 0)
    r = lax.broadcasted_iota(jnp.int32, (rows, LANES), 0)
    return jnp.where(r < rows - 1, parts[:, 0:LANES] + nxt + pe_term, 0.0)


def _compress_seq_body(r_ref, wab_ref, pe_ref, o_ref):
    parts = _dot(_bf(r_ref[0]), _bf(wab_ref[...]))
    o_ref[0] = _combine_cmp(parts, _pe_term(pe_ref, wab_ref))


def _compress_seq(rows, wab, pe):
    b, n16, kin = rows.shape
    return pl.pallas_call(
        _compress_seq_body,
        grid=(b,),
        in_specs=[pl.BlockSpec((1, n16, kin), lambda i: (i, 0, 0)),
                  pl.BlockSpec(wab.shape, lambda i: (0, 0)),
                  pl.BlockSpec(pe.shape, lambda i: (0, 0))],
        out_specs=pl.BlockSpec((1, n16, LANES), lambda i: (i, 0, 0)),
        out_shape=jax.ShapeDtypeStruct((b, n16, LANES), F32),
        compiler_params=_cparams("parallel"),
        name="nsa_compress_seq",
    )(rows, wab, pe)


def _bias_from_dist(dist, rb_ref, h):
    n = jnp.maximum(dist, 0)
    exact = T5_BUCKETS // 2
    out = jnp.full(dist.shape, rb_ref[exact, h], F32)
    for k, thr in enumerate(_bucket_thresholds()):
        out = jnp.where(n >= thr, rb_ref[exact + 1 + k, h], out)
    for j in range(exact):
        out = jnp.where(n == j, rb_ref[j, h], out)
    return out


def _softmax_rows(s, valid):
    s = jnp.where(valid, s, NEG_INF)
    m = jnp.max(s, axis=1, keepdims=True)
    m = jnp.where(m == NEG_INF, 0.0, m)
    p = jnp.exp(s - m)
    den = jnp.sum(p, axis=1, keepdims=True)
    return p / jnp.where(den > 0, den, 1.0)


def _top_blocks(score, k):
    lane = lax.broadcasted_iota(jnp.int32, score.shape, 1)
    width = score.shape[1]
    sel = jnp.zeros(score.shape, jnp.bool_)
    for _ in range(k):
        m = jnp.max(score, axis=1, keepdims=True)
        idx = jnp.min(jnp.where(score == m, lane, width), axis=1, keepdims=True)
        pick = (lane == idx) & (m > NEG_INF)
        sel = sel | pick
        score = jnp.where(pick, NEG_INF, score)
    return sel


def _tile_mask(ok):
    return jnp.concatenate([jnp.where(ok, 1.0, 0.0)] * N_HEADS, axis=0) > 0.5


def _stack_heads(x):
    return jnp.concatenate([_head(x, h) for h in range(N_HEADS)], axis=0)


def _toeplitz(tab_ref, h, r0, rows, ncol):
    nrow = (rows + ncol) // LANES
    strip = tab_ref[h, pl.ds(r0, nrow), :]
    flat = jnp.concatenate([strip[r:r + 1, :] for r in range(nrow)], axis=1)
    rolled = pltpu.roll(jnp.broadcast_to(flat, (rows, nrow * LANES)), 0, 1, stride=1, stride_axis=0)
    return rolled[:, rows:]


SLC_KC = 512


MASK_BIG = 1e30


def _top_blocks_t(score, k):
    row = lax.broadcasted_iota(jnp.int32, score.shape, 0)
    n = score.shape[0]
    sel = jnp.zeros(score.shape, jnp.bool_)
    for _ in range(k):
        m = jnp.max(score, axis=0, keepdims=True)
        idx = jnp.min(jnp.where(score == m, row, n), axis=0, keepdims=True)
        pick = (row == idx) & (m > NEG_INF)
        sel = sel | pick
        score = jnp.where(pick, NEG_INF, score)
    return sel


def _bucket_index(dist):
    n = jnp.maximum(dist, 0)
    exact = T5_BUCKETS // 2
    big = jnp.full(dist.shape, exact, jnp.int32)
    for thr in _bucket_thresholds():
        big = big + (n >= thr).astype(jnp.int32)
    return jnp.where(n < exact, n, big)


def _gather_bias(rbt_ref, h, bucket):
    rows, width = bucket.shape
    piece = min(width, LANES)
    tab = jnp.broadcast_to(rbt_ref[h:h + 1, 0:piece], (rows, piece))
    parts = [jnp.take_along_axis(tab, bucket[:, s:s + piece], axis=1) for s in range(0, width, piece)]
    return parts[0] if len(parts) == 1 else jnp.concatenate(parts, axis=1)


def _tile_rows(x, n=N_HEADS):
    return jnp.concatenate([x] * n, axis=0)


def _near_chunks():
    return (_bucket_thresholds()[-1] + SLC_KC - 2) // SLC_KC + 1


def _bias_tile(tab_ref, base, ncol, p0, lo, hi):
    r0 = (p0 - base - Q_BLOCK) // LANES
    bias = jnp.concatenate([_toeplitz(tab_ref, h, r0, Q_BLOCK, ncol) for h in range(N_HEADS)], axis=0)
    dist = (base + lax.broadcasted_iota(jnp.int32, (Q_BLOCK, ncol), 0)
            - lax.broadcasted_iota(jnp.int32, (Q_BLOCK, ncol), 1))
    return bias + _tile_rows(jnp.where((dist >= lo) & (dist < hi), 0.0, -MASK_BIG))


def _nsa_seq_body(rb_ref, q_ref, sm_ref, kx_ref, vs_ref, kvw_ref, kvc_ref, tab_ref, rbt_ref, ovlt_ref, y_ref,
                  stile_ref, wtile_ref, *, seq, p0):
    qb = Q_BLOCK
    rows = N_HEADS * qb
    n16 = seq // CMP_STRIDE
    nblk = seq // SLC_BLOCK
    bi = pl.program_id(1)
    q0 = bi * qb

    @pl.when((pl.program_id(0) == 0) & (bi == 0))
    def _():
        per_chunk = SLC_KC // qb

        def slc_tile(t, carry):
            base = (t // _near_chunks()) * qb + (t % _near_chunks()) * SLC_KC
            stile_ref[t] = _bias_tile(tab_ref, base, SLC_KC, p0, 0, seq + SLC_KC)
            return carry

        def win_tile(t, carry):
            wtile_ref[t] = _bias_tile(tab_ref, t * qb, WINDOW + qb, p0, 0, WINDOW)
            return carry

        lax.fori_loop(0, per_chunk * _near_chunks(), slc_tile, 0)
        lax.fori_loop(0, WINDOW // qb + 1, win_tile, 0)
    qs = _stack_heads(q_ref[0]) * SCALE
    qsb = _bf(qs)
    qpos = q0 + lax.broadcasted_iota(jnp.int32, (qb, 1), 0)

    kvc = kvc_ref[0]
    s_c = _dot_nt(qsb, _bf(kvc[:, 0:HEAD_DIM]))
    band = WINDOW + qb
    sw = pl.multiple_of(jnp.maximum(q0 - WINDOW, 0), qb)
    kw = kvw_ref[0, pl.ds(sw, band), 0:HEAD_DIM]
    vw = kvw_ref[0, pl.ds(sw, band), HEAD_DIM:2 * HEAD_DIM]
    s_w = _dot_nt(qsb, kw)

    cmp_end = lax.broadcasted_iota(jnp.int32, (qb, n16), 1) * CMP_STRIDE + (CMP_LEN - 1)
    dist_c = qpos - cmp_end
    bucket_c = _bucket_index(dist_c)
    bias_c = jnp.concatenate([_gather_bias(rbt_ref, h, bucket_c) for h in range(N_HEADS)], axis=0)
    p_c = _softmax_rows(s_c + bias_c, _tile_mask(dist_c >= 0))
    p_cb = _bf(p_c)
    o_c = _dot(p_cb, _bf(kvc[:, HEAD_DIM:2 * HEAD_DIM]))
    p_r = p_cb.astype(F32)
    p_sum = p_r[0:qb] + p_r[qb:2 * qb] + p_r[2 * qb:3 * qb] + p_r[3 * qb:4 * qb]
    imp_t = _dot_nt(ovlt_ref[...], p_sum, precision=HI)

    base_w = q0 - sw
    sb_w = s_w + wtile_ref[base_w // qb]
    e_w = jnp.exp(sb_w - jnp.max(sb_w, axis=1, keepdims=True))
    o_w = _dot(_bf(e_w / jnp.sum(e_w, axis=1, keepdims=True)), vw)

    qlane = q0 + lax.broadcasted_iota(jnp.int32, (1, qb), 1)
    cur = qlane // SLC_BLOCK
    rowb = lax.broadcasted_iota(jnp.int32, (nblk, qb), 0)
    forced = (rowb == 0) | (rowb == cur) | (rowb == cur - 1)
    score = jnp.where(forced, jnp.inf, imp_t)
    score = jnp.where(rowb > cur, NEG_INF, score)
    sel = jnp.where(_top_blocks_t(score, SLC_TOPK), 1.0, 0.0).T
    negm = _bf(_tile_rows((sel - 1.0) * MASK_BIG))
    q_ext = jnp.concatenate([qsb, negm], axis=1)

    thr_far = _bucket_thresholds()[-1] + SLC_KC - 1
    n_chunks = q0 // SLC_KC + 1
    n_far = jnp.maximum(q0 - thr_far + SLC_KC, 0) // SLC_KC
    n_near = _near_chunks()

    far_bias = jnp.concatenate([jnp.full((qb, 1), rb_ref[T5_BUCKETS - 1, h], F32) for h in range(N_HEADS)], axis=0)

    def scores(c):
        k0 = pl.multiple_of(c * SLC_KC, SLC_KC)
        kx = kx_ref[0, pl.ds(k0, SLC_KC), :]
        return q0 - k0, _dot_nt(q_ext, kx), vs_ref[0, pl.ds(k0, SLC_KC), :]

    def online(carry, s, vs):
        m, l, acc = carry
        m_new = jnp.maximum(m, jnp.max(s, axis=1, keepdims=True))
        alpha = jnp.exp(m - m_new)
        p = jnp.exp(s - m_new)
        return m_new, alpha * l + jnp.sum(p, axis=1, keepdims=True), alpha * acc + _dot(_bf(p), vs)

    def far_step(c, carry):
        _, s, vs = scores(c)
        return online(carry, s + far_bias, vs)

    def near_step(c, carry):
        base, s, vs = scores(c)
        tile = ((q0 % SLC_KC) // qb) * n_near + base // SLC_KC
        return online(carry, s + stile_ref[tile], vs)

    init = (jnp.full((rows, 1), -MASK_BIG / 10, F32), jnp.zeros((rows, 1), F32), jnp.zeros((rows, HEAD_DIM), F32))
    carry = lax.fori_loop(0, n_far, far_step, init)
    _, l_s, acc_s = lax.fori_loop(n_far, n_chunks, near_step, carry)
    o_s = acc_s / jnp.where(l_s > 0, l_s, 1.0)

    gates = _sigmoid(sm_ref[0])
    outs = []
    for h in range(N_HEADS):
        rs = slice(h * qb, (h + 1) * qb)
        g = lambda t: gates[:, SM_NG + 3 * h + t:SM_NG + 3 * h + t + 1]
        outs.append(g(0) * o_c[rs] + g(1) * o_s[rs] + g(2) * o_w[rs])
    y_ref[0] = jnp.concatenate(outs, axis=1)


def _nsa_seq(rel_bias, nq, small, nkv, kvc, tab, seq_p0):
    b, l, _ = nq.shape
    n16 = l // CMP_STRIDE
    nblk = l // SLC_BLOCK
    cmp_start = np.arange(n16)[None, :] * CMP_STRIDE
    slc_start = np.arange(nblk)[:, None] * SLC_BLOCK
    ovlt = ((cmp_start < slc_start + SLC_BLOCK) & (cmp_start + CMP_LEN > slc_start)
            & (np.arange(n16)[None, :] < n16 - 1)).astype(np.float32)
    block_id = (np.arange(l)[:, None] // SLC_BLOCK == np.arange(nblk)[None, :]).astype(np.float32)
    kx = _bf(jnp.concatenate([nkv[..., 2 * HEAD_DIM:3 * HEAD_DIM],
                              jnp.broadcast_to(jnp.asarray(block_id), (b, l, nblk))], axis=-1))
    vs = _bf(nkv[..., 3 * HEAD_DIM:4 * HEAD_DIM])
    kvw = _bf(nkv[..., 4 * HEAD_DIM:6 * HEAD_DIM])
    rbt = jnp.pad(rel_bias.T, ((0, SUBLANES - N_HEADS), (0, LANES - T5_BUCKETS)))
    nrows = tab.shape[1]
    const = lambda shape: pl.BlockSpec(shape, lambda i, j: (0,) * len(shape))
    seq_spec = lambda n: pl.BlockSpec((1, l, n), lambda i, j: (i, 0, 0))
    return pl.pallas_call(
        functools.partial(_nsa_seq_body, seq=l, p0=seq_p0),
        grid=(b, l // Q_BLOCK),
        in_specs=[pl.BlockSpec(memory_space=pltpu.SMEM),
                  pl.BlockSpec((1, Q_BLOCK, D_HEADS), lambda i, j: (i, j, 0)),
                  pl.BlockSpec((1, Q_BLOCK, LANES), lambda i, j: (i, j, 0)),
                  seq_spec(HEAD_DIM + nblk), seq_spec(HEAD_DIM), seq_spec(2 * HEAD_DIM),
                  pl.BlockSpec((1, n16, LANES), lambda i, j: (i, 0, 0)),
                  const((SUBLANES, nrows, LANES)), const((SUBLANES, LANES)), const((nblk, n16))],
        out_specs=pl.BlockSpec((1, Q_BLOCK, D_HEADS), lambda i, j: (i, j, 0)),
        out_shape=jax.ShapeDtypeStruct((b, l, D_HEADS), F32),
        scratch_shapes=[pltpu.VMEM(((SLC_KC // Q_BLOCK) * _near_chunks(), N_HEADS * Q_BLOCK, SLC_KC), F32),
                        pltpu.VMEM((WINDOW // Q_BLOCK + 1, N_HEADS * Q_BLOCK, WINDOW + Q_BLOCK), F32)],
        compiler_params=_cparams("arbitrary", "arbitrary"),
        name="nsa_seq",
    )(rel_bias, nq, small, kx, vs, kvw, kvc, tab, rbt, jnp.asarray(ovlt))


def _nsa_tab_seq_dists(seq):
    max_base = max(seq, SLC_KC + (_near_chunks() - 1) * SLC_KC)
    p0 = max_base + Q_BLOCK
    length = p0 + WINDOW + Q_BLOCK + LANES
    return p0, np.maximum(p0 - np.arange(length), 0)


PAGE_GROUP = 16
PAGES_PER_STEP = 64


def _pad_rows(x, rows=SUBLANES):
    return jnp.concatenate([x, jnp.zeros((rows - x.shape[0], x.shape[1]), x.dtype)], axis=0)


def _nsa_select_body(pt_ref, *refs, past, n_steps):
    del pt_ref
    pages = refs[:PAGES_PER_STEP]
    wab_ref, pe_ref, q_ref, tabc_ref, ovlt_ref, oc_ref, top_ref, parts_ref, rows_ref = refs[PAGES_PER_STEP:]
    step = pl.program_id(1)
    blocks_per_page = PAGE_SIZE // CMP_STRIDE
    group_rows = PAGE_GROUP * blocks_per_page
    n_groups = PAGES_PER_STEP // PAGE_GROUP
    for p, pg in enumerate(pages):
        g, pp = divmod(p, PAGE_GROUP)
        rows_ref[g, pp * PAGE_SIZE:(pp + 1) * PAGE_SIZE, :] = pg[...].reshape(2 * HEAD_DIM, PAGE_SIZE).T
    for g in range(n_groups):
        acc = jnp.zeros((group_rows, 2 * LANES), F32)
        for j in range(CMP_STRIDE):
            xj = rows_ref[g, pl.ds(j, group_rows, stride=CMP_STRIDE), :]
            acc = acc + _dot(_bf(xj), wab_ref[j * LANES:(j + 1) * LANES, :])
        row0 = pl.multiple_of((step * n_groups + g) * group_rows, group_rows)
        parts_ref[pl.ds(row0, group_rows), :] = acc

    @pl.when(step == n_steps - 1)
    def _():
        n16 = past // CMP_STRIDE
        n_cmp = n16 - 1
        cur = past // SLC_BLOCK
        pe_term = _pe_term(pe_ref, wab_ref)
        kvc = _combine_cmp(parts_ref[...], pe_term)
        qs = _pad_rows(_stack_heads(q_ref[0]) * SCALE)
        s_c = _dot_nt(_bf(qs), _bf(kvc[:, 0:HEAD_DIM])) + tabc_ref[...]
        lane = lax.broadcasted_iota(jnp.int32, (SUBLANES, n16), 1)
        p_c = _softmax_rows(s_c, lane < n_cmp)
        p_cb = _bf(p_c)
        o_c = _dot(p_cb, _bf(kvc[:, HEAD_DIM:2 * HEAD_DIM]))
        oc_ref[0] = jnp.concatenate([o_c[h:h + 1, :] for h in range(N_HEADS)], axis=1)
        imp_h = _dot_nt(ovlt_ref[...], p_cb)
        imp = imp_h[:, 0:1] + imp_h[:, 1:2] + imp_h[:, 2:3] + imp_h[:, 3:4]
        nrow = imp.shape[0]
        rowb = lax.broadcasted_iota(jnp.int32, (nrow, 1), 0)
        forced = (rowb == 0) | (rowb == cur) | (rowb == cur - 1)
        score = jnp.where(forced, jnp.inf, imp)
        score = jnp.where(rowb > cur, NEG_INF, score)
        out_lane = lax.broadcasted_iota(jnp.int32, (1, LANES), 1)
        top = jnp.zeros((1, LANES), jnp.int32)
        for r in range(SLC_TOPK):
            m = jnp.max(score, axis=0, keepdims=True)
            idx = jnp.min(jnp.where(score == m, rowb, nrow), axis=0, keepdims=True)
            top = jnp.where(out_lane == r, idx, top)
            score = jnp.where(rowb == idx, NEG_INF, score)
        top_ref[0] = top


def _nsa_select(layer, page_table, cache_t, wab_bf, pe, nq, tabc, past):
    b, n_pages = page_table.shape
    n_steps = n_pages // PAGES_PER_STEP
    n16 = past // CMP_STRIDE
    n_slc = past // SLC_BLOCK + 1
    nrow = -(-n_slc // SUBLANES) * SUBLANES
    cmp_start = np.arange(n16)[None, :] * CMP_STRIDE
    slc_start = np.arange(nrow)[:, None] * SLC_BLOCK
    ovlt = ((cmp_start < slc_start + SLC_BLOCK) & (cmp_start + CMP_LEN > slc_start)
            & (np.arange(n16)[None, :] < n16 - 1) & (np.arange(nrow)[:, None] < n_slc)).astype(np.float32)

    n_pool = cache_t.shape[1]

    def page_spec(p):
        def imap(i, s, pt):
            pg = pt[jnp.minimum(i, b - 1), jnp.minimum(s, n_steps - 1) * PAGES_PER_STEP + p]
            return (layer, jnp.minimum(pg, n_pool - 1), 0, 0, 0)
        return pl.BlockSpec((None, None, 2, HEAD_DIM, PAGE_SIZE), imap)

    const = lambda shape: pl.BlockSpec(shape, lambda i, s, pt: (0,) * len(shape))
    grid_spec = pltpu.PrefetchScalarGridSpec(
        num_scalar_prefetch=1,
        grid=(b, n_steps),
        in_specs=[page_spec(p) for p in range(PAGES_PER_STEP)]
        + [const(wab_bf.shape), const(pe.shape), pl.BlockSpec((1, 1, D_HEADS), lambda i, s, pt: (i, 0, 0)),
           const(tabc.shape), const(ovlt.shape)],
        out_specs=[pl.BlockSpec((1, 1, D_HEADS), lambda i, s, pt: (i, 0, 0)),
                   pl.BlockSpec((1, 1, LANES), lambda i, s, pt: (i, 0, 0))],
        scratch_shapes=[pltpu.VMEM((n16, 2 * LANES), F32),
                        pltpu.VMEM((PAGES_PER_STEP // PAGE_GROUP, PAGE_GROUP * PAGE_SIZE, 2 * HEAD_DIM), F32)],
    )
    return pl.pallas_call(
        functools.partial(_nsa_select_body, past=past, n_steps=n_steps),
        grid_spec=grid_spec,
        out_shape=[jax.ShapeDtypeStruct((b, 1, D_HEADS), F32), jax.ShapeDtypeStruct((b, 1, LANES), jnp.int32)],
        compiler_params=_cparams("parallel", "arbitrary"),
        name="nsa_select",
    )(page_table, *([cache_t] * PAGES_PER_STEP), wab_bf, pe, nq, tabc, jnp.asarray(ovlt, BF16))


def _nsa_attend_body(pt_ref, top_ref, *refs, past):
    del pt_ref
    pages = refs[:SLC_TOPK]
    (q_ref, sm_ref, new_ref, oc_ref, swa_ref, tabs_ref, tabw_ref, y_ref, swa_out) = refs[SLC_TOPK:]
    b = pl.program_id(0)
    new_blk = past // SLC_BLOCK
    blocks_per_page = PAGE_SIZE // SLC_BLOCK
    qs = _bf(_pad_rows(_stack_heads(q_ref[0]) * SCALE))
    new = new_ref[0]
    lane_t = lax.broadcasted_iota(jnp.int32, (HEAD_DIM, PAGE_SIZE), 1)
    lane = lax.broadcasted_iota(jnp.int32, (SUBLANES, PAGE_SIZE), 1)
    new_kt = jnp.where(lane_t == 0, _row_to_col(new[:, 2 * HEAD_DIM:3 * HEAD_DIM]), 0.0)
    new_vt = jnp.where(lane_t == 0, _row_to_col(new[:, 3 * HEAD_DIM:4 * HEAD_DIM]), 0.0)
    scores, oks, vts = [], [], []
    for s in range(SLC_TOPK):
        ti = jnp.clip(top_ref[b, s], 0, new_blk)
        is_new = ti == new_blk
        kt = jnp.where(is_new, new_kt, pages[s][0])
        vts.append(jnp.where(is_new, new_vt, pages[s][1]))
        scores.append(_dot(qs, _bf(kt)) + tabs_ref[ti])
        kpos = ti * SLC_BLOCK + lane % SLC_BLOCK
        ok = (lane // SLC_BLOCK == ti % blocks_per_page) & (kpos <= past)
        oks.append(jnp.where(ok, 1.0, 0.0))
    p_s = _softmax_rows(jnp.concatenate(scores, axis=1), jnp.concatenate(oks, axis=1) > 0.5)
    o_s = None
    for s in range(SLC_TOPK):
        t = _dot_nt(_bf(p_s[:, s * PAGE_SIZE:(s + 1) * PAGE_SIZE]), _bf(vts[s]))
        o_s = t if o_s is None else o_s + t

    wlane = lax.broadcasted_iota(jnp.int32, (HEAD_DIM, WINDOW), 1)
    win = []
    for c in range(2):
        col = _row_to_col(new[:, (4 + c) * HEAD_DIM:(5 + c) * HEAD_DIM])
        win.append(jnp.where(wlane == WINDOW - 1, col, pltpu.roll(swa_ref[c], WINDOW - 1, 1)))
        swa_out[c] = win[c]
    s_w = _dot(qs, _bf(win[0])) + tabw_ref[...]
    p_w = _softmax_rows(s_w, jnp.full(s_w.shape, True))
    o_w = _dot_nt(_bf(p_w), _bf(win[1]))

    gates = _sigmoid(sm_ref[0])
    o_c = oc_ref[0]
    outs = []
    for h in range(N_HEADS):
        g = lambda t: gates[:, SM_NG + 3 * h + t:SM_NG + 3 * h + t + 1]
        outs.append(g(0) * _head(o_c, h) + g(1) * o_s[h:h + 1, :] + g(2) * o_w[h:h + 1, :])
    y_ref[0] = jnp.concatenate(outs, axis=1)


def _nsa_attend(layer, page_table, top, cache_t, nq, small, nkv, o_c, swa_t, tabs, tabw, past):
    b = nq.shape[0]
    last_page = past // PAGE_SIZE - 1
    blocks_per_page = PAGE_SIZE // SLC_BLOCK

    n_pool = cache_t.shape[1]

    def page_spec(s):
        def imap(i, pt, tp):
            ii = jnp.minimum(i, b - 1)
            pg = jnp.clip(tp[ii, s] // blocks_per_page, 0, last_page)
            return (layer, jnp.clip(pt[ii, pg], 0, n_pool - 1), 1, 0, 0)
        return pl.BlockSpec((None, None, 2, HEAD_DIM, PAGE_SIZE), imap)

    row = lambda n: pl.BlockSpec((1, 1, n), lambda i, pt, tp: (i, 0, 0))
    const = lambda shape: pl.BlockSpec(shape, lambda i, pt, tp: (0,) * len(shape))
    grid_spec = pltpu.PrefetchScalarGridSpec(
        num_scalar_prefetch=2,
        grid=(b,),
        in_specs=[page_spec(s) for s in range(SLC_TOPK)]
        + [row(D_HEADS), row(LANES), row(6 * HEAD_DIM), row(D_HEADS),
           pl.BlockSpec((None, None, 2, HEAD_DIM, WINDOW), lambda i, pt, tp: (layer, i, 0, 0, 0)),
           const(tabs.shape), const(tabw.shape)],
        out_specs=[row(D_HEADS), pl.BlockSpec((None, 2, HEAD_DIM, WINDOW), lambda i, pt, tp: (i, 0, 0, 0))],
    )
    return pl.pallas_call(
        functools.partial(_nsa_attend_body, past=past),
        grid_spec=grid_spec,
        out_shape=[jax.ShapeDtypeStruct((b, 1, D_HEADS), F32),
                   jax.ShapeDtypeStruct((b, 2, HEAD_DIM, WINDOW), F32)],
        compiler_params=_cparams("parallel"),
        name="nsa_attend",
    )(page_table, top, *([cache_t] * SLC_TOPK), nq, small, nkv, o_c, swa_t, tabs, tabw)


SEQ_CHUNK = 64
GDN_CHUNKS = 1
SCAN_CHUNK = 256
ROW_TILE = 512
FFN_TILE = 512


def _lane_vec(pairs):
    v = jnp.zeros((1, LANES), F32)
    for off, vals in pairs:
        v = v.at[0, off:off + vals.shape[0]].set(vals.astype(F32))
    return v


def kernel(x_prompt, x_sample, cache_nsa_kv, cache_swa_kv, state_ret, state_mlstm_C, state_mlstm_n, state_mlstm_m, state_gdn, state_gdn_conv, state_ffn_conv, page_table, rel_bias, norm_pre_mix, w_in, mlstm_b_i, mlstm_b_f, gdn_conv_w, gdn_A_log, gdn_dt_bias, nsa_cmp_pe, nsa_cmp_w, ret_norm, mlstm_norm, gdn_norm, w_out, norm_post_mix, norm_pre_ffn, w_ffn_gate, w_ffn_up, ffn_conv_w, w_ffn_down, norm_post_ffn):
    depth = w_in.shape[0]
    bp, seq, _ = x_prompt.shape
    bs, dec_seq, _ = x_sample.shape
    assert dec_seq == 1 and seq % SLC_KC == 0 and seq >= WINDOW + Q_BLOCK
    n_pool = cache_nsa_kv.shape[1]
    past = page_table.shape[1] * PAGE_SIZE
    assert past >= WINDOW and past % SLC_BLOCK == 0 and page_table.shape[1] % PAGES_PER_STEP == 0

    p0, d_seq = _nsa_tab_seq_dists(seq)
    n16 = past // CMP_STRIDE
    n_slc = past // SLC_BLOCK + 1
    d_cmp = np.maximum(past - (np.arange(n16) * CMP_STRIDE + CMP_LEN - 1), 0)
    d_slc = np.maximum(past - np.arange(n_slc * SLC_BLOCK), 0)
    d_win = WINDOW - 1 - np.arange(WINDOW)
    tab = _bias_tables(rel_bias, np.concatenate([d_seq, d_cmp, d_slc, d_win]))
    o1, o2, o3 = len(d_seq), len(d_seq) + len(d_cmp), len(d_seq) + len(d_cmp) + len(d_slc)
    tab_seq = tab[:, :o1].reshape(SUBLANES, o1 // LANES, LANES)
    tabc = tab[:, o1:o2]
    tabs = jnp.transpose(tab[:, o2:o3].reshape(SUBLANES, n_slc, SLC_BLOCK), (1, 0, 2))
    tabs = jnp.concatenate([tabs, tabs], axis=2)
    tabw = tab[:, o3:]

    cache_t = jnp.transpose(cache_nsa_kv, (0, 1, 3, 4, 2))
    swa_t = jnp.transpose(cache_swa_kv, (0, 1, 3, 4, 2))
    cos_p, sin_p = _rope_tables(jnp.arange(seq))
    cos_s, sin_s = _rope_tables(jnp.arange(past, past + 1))

    xp = x_prompt
    xs = x_sample.reshape(bs, D_MODEL)
    p_states, s_states = [], []
    for l in range(depth):
        w_cat = _prep_w_in(w_in[l])
        w_out_bf = _bf(w_out[l])
        wg, wu, wd = _bf(w_ffn_gate[l]), _bf(w_ffn_up[l]), _bf(w_ffn_down[l])
        wab, pe = _prep_cmp_w(nsa_cmp_w[l], nsa_cmp_pe[l])
        ml_bias = _lane_vec([(SM_MI, mlstm_b_i[l]), (SM_MF, mlstm_b_f[l])])
        dtb = _lane_vec([(SM_GA, gdn_dt_bias[l])])
        alog = _lane_vec([(SM_GA, gdn_A_log[l])])

        z = _in_proj(xp.reshape(bp * seq, D_MODEL), norm_pre_mix[l], w_cat, ROW_TILE, seq=seq)
        z_ret, z_ml, z_gdn, z_nq, z_nkv, z_sm = [t.reshape(bp, seq, -1) for t in z[:-1]]
        nkv_t = z[-1].reshape(bp, 6, HEAD_DIM, seq)
        y_ret, st_ret = _retention_seq(z_ret, cos_p, sin_p, ret_norm[l], SCAN_CHUNK, bp)
        y_ml, st_c, st_n, st_m = _mlstm_seq(z_ml, z_sm, ml_bias, mlstm_norm[l], SCAN_CHUNK, bp)
        y_gdn, st_g = _gdn_seq(z_gdn, z_sm, gdn_conv_w[l], dtb, alog, gdn_norm[l], SEQ_CHUNK, bp, GDN_CHUNKS)
        rows = z_nkv[..., 0:2 * HEAD_DIM].reshape(bp, seq // CMP_STRIDE, CMP_STRIDE * 2 * HEAD_DIM)
        kvc = _compress_seq(rows, wab, pe)
        y_nsa = _nsa_seq(rel_bias, z_nq, z_sm, z_nkv, kvc, tab_seq, p0)
        ys = [t.reshape(bp * seq, D_HEADS) for t in (y_ret, y_ml, y_gdn, y_nsa)]
        x1, h2 = _out_proj(xp.reshape(bp * seq, D_MODEL), ys, w_out_bf, norm_post_mix[l], norm_pre_ffn[l], ROW_TILE)
        ffn_tile = min(seq, FFN_TILE)
        xp, tail = _ffn_seq(x1.reshape(bp, seq, D_MODEL), h2.reshape(bp, seq, D_MODEL), wg, wu, wd,
                            ffn_conv_w[l], norm_post_ffn[l], ffn_tile)
        p_states.append((
            jnp.transpose(nkv_t[:, 0:4], (0, 3, 1, 2)),
            jnp.transpose(nkv_t[:, 4:6, :, seq - WINDOW:], (0, 3, 1, 2)),
            st_ret, st_c, st_n, st_m[:, 0, :N_HEADS], st_g,
            z_gdn[:, seq - (GDN_CONV - 1):, 0:3 * D_HEADS],
            tail[:, SUBLANES - (FFN_CONV - 1):, :]))

        z = _in_proj(xs, norm_pre_mix[l], w_cat, bs)
        z_ret, z_ml, z_gdn, z_nq, z_nkv, z_sm = [t.reshape(bs, 1, -1) for t in z]
        m_pad = jnp.pad(state_mlstm_m[l], ((0, 0), (0, LANES - N_HEADS))).reshape(bs, 1, LANES)
        (y_ret, y_ml, y_gdn, st_ret, st_c, st_n, st_m, st_g, st_gc) = _mixers_step(
            z_ret, z_ml, z_gdn, z_sm, cos_s, sin_s, ml_bias, dtb, alog, gdn_conv_w[l],
            ret_norm[l], mlstm_norm[l], gdn_norm[l],
            state_ret[l], state_mlstm_C[l], state_mlstm_n[l], m_pad, state_gdn[l], state_gdn_conv[l])
        o_c, top = _nsa_select(l, page_table, cache_t, _bf(wab), pe, z_nq, tabc, past)
        y_nsa, swa_new = _nsa_attend(l, page_table, top[:, 0, :SLC_TOPK], cache_t, z_nq, z_sm, z_nkv, o_c,
                                     swa_t, tabs, tabw, past)
        ys = [t.reshape(bs, D_HEADS) for t in (y_ret, y_ml, y_gdn, y_nsa)]
        x1, h2 = _out_proj(xs, ys, w_out_bf, norm_post_mix[l], norm_pre_ffn[l], bs)
        xs, g_new = _ffn_step(x1, h2, state_ffn_conv[l][:, 0], state_ffn_conv[l][:, 1], wg, wu, wd,
                              ffn_conv_w[l], norm_post_ffn[l])
        s_states.append((
            z_nkv[..., 0:4 * HEAD_DIM].reshape(bs, 1, 4, HEAD_DIM),
            jnp.transpose(swa_new, (0, 3, 1, 2)),
            st_ret, st_c, st_n, st_m[:, 0, :N_HEADS], st_g, st_gc,
            jnp.concatenate([state_ffn_conv[l][:, 1:], g_new[:, None, :]], axis=1)))

    stack = lambda states: tuple(jnp.stack([s[k] for s in states]) for k in range(len(states[0])))
    return (xp, xs.reshape(bs, 1, D_MODEL)) + stack(p_states) + stack(s_states)
```

```python
import functools
import math

import jax
import jax.numpy as jnp
import numpy as np
from jax import lax
from jax.experimental import pallas as pl
from jax.experimental.pallas import tpu as pltpu

F32 = jnp.float32
BF16 = jnp.bfloat16
HI = lax.Precision.HIGHEST

D_MODEL = 1024
HEAD_DIM = 64
N_HEADS = 4
D_HEADS = N_HEADS * HEAD_DIM
D_FF = 2816
GDN_CONV = 4
FFN_CONV = 3
PAGE_SIZE = 128
CMP_STRIDE = 16
CMP_LEN = 32
SLC_BLOCK = 64
SLC_TOPK = 16
WINDOW = 512
Q_BLOCK = 128
T5_BUCKETS = 32
T5_MAX_DIST = 1024
ROPE_BASE = 10000.0
EPS = 1e-6
SCALE = HEAD_DIM ** -0.5
IN_WIDTHS = (256, 256, 256, 256, 256, 256, 256, 4, 4, 256, 768, 4, 4, 256, 256, 384, 12)
LANES = 128
SUBLANES = 8
VMEM_LIMIT = 56 * 1024 * 1024
NEG_INF = float("-inf")

SM_MI, SM_MF, SM_GA, SM_GB, SM_NG = 0, 4, 8, 12, 16


def _cparams(*sem):
    return pltpu.CompilerParams(dimension_semantics=sem, vmem_limit_bytes=VMEM_LIMIT)


def _rms(x, g):
    return x * lax.rsqrt(jnp.mean(x * x, axis=-1, keepdims=True) + EPS) * g


def _dot(a, b, **kw):
    return jnp.dot(a, b, preferred_element_type=F32, **kw)


def _dot_nt(a, b, **kw):
    return lax.dot_general(a, b, (((1,), (1,)), ((), ())), preferred_element_type=F32, **kw)


def _dot_tn(a, b, **kw):
    return lax.dot_general(a, b, (((0,), (0,)), ((), ())), preferred_element_type=F32, **kw)


def _bf(x):
    return x.astype(BF16)


def _sigmoid(x):
    return 1.0 / (1.0 + jnp.exp(-x))


def _silu(x):
    return x * _sigmoid(x)


def _softplus(x):
    return jnp.maximum(x, 0.0) + jnp.log(1.0 + jnp.exp(-jnp.abs(x)))


def _head_rms(o):
    return o * lax.rsqrt(jnp.mean(o * o, axis=-1, keepdims=True) + EPS)


PROJ_WIDTHS = (1024, 1024, 1024, 256, 384, 128)


def _prep_w_in(w):
    wt = w.T
    parts, off = [], 0
    for wd in IN_WIDTHS:
        parts.append(wt[off:off + wd])
        off += wd
    (rq, rk, rv, rg, mq, mk, mv, mi, mf, mo, gqkv, ga, gbeta, gg, nq, nkv, ngate) = parts
    small = jnp.concatenate([mi, mf, ga, gbeta, ngate], axis=0)
    small = jnp.pad(small, ((0, LANES - small.shape[0]), (0, 0)))
    cat = jnp.concatenate([rq, rk, rv, rg, mq, mk, mv, mo, gqkv, gg, nq, nkv, small], axis=0)
    return cat.astype(BF16)


NKV_OFF = sum(PROJ_WIDTHS[:4])


def _in_proj_body(x_ref, g_ref, wt_ref, *out_refs, with_kv_t):
    h = _bf(_rms(x_ref[...], g_ref[...]))
    off = 0
    for ref in out_refs[:len(PROJ_WIDTHS)]:
        n = ref.shape[-1]
        ref[...] = _dot_nt(h, wt_ref[off:off + n, :])
        off += n
    if with_kv_t:
        out_refs[-1][0] = _dot_nt(wt_ref[NKV_OFF:NKV_OFF + PROJ_WIDTHS[4], :], h)


def _in_proj(x2d, g, w_cat_t, tm, seq=None):
    m = x2d.shape[0]
    ntot = sum(PROJ_WIDTHS)
    with_kv_t = seq is not None
    const = lambda shape: pl.BlockSpec(shape, lambda i: (0,) * len(shape), pipeline_mode=pl.Buffered(1))
    out_specs = [pl.BlockSpec((tm, n), lambda i: (i, 0)) for n in PROJ_WIDTHS]
    out_shape = [jax.ShapeDtypeStruct((m, n), F32) for n in PROJ_WIDTHS]
    if with_kv_t:
        per_seq = seq // tm
        nkv = PROJ_WIDTHS[4]
        out_specs.append(pl.BlockSpec((1, nkv, tm), lambda i: (i // per_seq, 0, i % per_seq)))
        out_shape.append(jax.ShapeDtypeStruct((m // seq, nkv, seq), F32))
    return pl.pallas_call(
        functools.partial(_in_proj_body, with_kv_t=with_kv_t),
        grid=(m // tm,),
        in_specs=[pl.BlockSpec((tm, D_MODEL), lambda i: (i, 0)), const((1, D_MODEL)), const((ntot, D_MODEL))],
        out_specs=out_specs,
        out_shape=out_shape,
        compiler_params=_cparams("parallel"),
        name="in_proj",
    )(x2d, g.reshape(1, D_MODEL), w_cat_t)


def _out_proj_body(x_ref, y0, y1, y2, y3, w_ref, gpost_ref, gpre_ref, x1_ref, h2_ref):
    acc = None
    for k, y in enumerate((y0, y1, y2, y3)):
        t = _dot(_bf(y[...]), w_ref[k * D_HEADS:(k + 1) * D_HEADS, :])
        acc = t if acc is None else acc + t
    x1 = x_ref[...] + _rms(acc, gpost_ref[...])
    x1_ref[...] = x1
    h2_ref[...] = _bf(_rms(x1, gpre_ref[...]))


def _out_proj(x2d, ys, w_out_bf, gpost, gpre, tm):
    m = x2d.shape[0]
    row = lambda n: pl.BlockSpec((tm, n), lambda i: (i, 0))
    vec = pl.BlockSpec((1, D_MODEL), lambda i: (0, 0))
    return pl.pallas_call(
        _out_proj_body,
        grid=(m // tm,),
        in_specs=[row(D_MODEL)] + [row(D_HEADS)] * 4
        + [pl.BlockSpec((D_MODEL, D_MODEL), lambda i: (0, 0)), vec, vec],
        out_specs=[row(D_MODEL), row(D_MODEL)],
        out_shape=[jax.ShapeDtypeStruct((m, D_MODEL), F32), jax.ShapeDtypeStruct((m, D_MODEL), BF16)],
        compiler_params=_cparams("parallel"),
        name="out_proj",
    )(x2d, *ys, w_out_bf, gpost.reshape(1, -1), gpre.reshape(1, -1))


FFN_TN = 256


def _gelu_tanh(x):
    c = 0.7978845608028654
    hx = 0.5 * x
    return hx + hx * jnp.tanh(x * (c + (0.044715 * c) * (x * x)))


def _ffn_seq_body(x_ref, y0, y1, y2, y3, wo_ref, gmix_ref, gpre_ref, wg_ref, wu_ref, wd_ref, cw_ref, gpost_ref,
                  x2_ref, tail_ref, carry_ref, act_ref, *, tm):
    @pl.when(pl.program_id(1) == 0)
    def _():
        carry_ref[...] = jnp.zeros_like(carry_ref)

    mix = None
    for k, y in enumerate((y0, y1, y2, y3)):
        t = _dot(_bf(y[0]), wo_ref[k * D_HEADS:(k + 1) * D_HEADS, :])
        mix = t if mix is None else mix + t
    x1 = x_ref[0] + _rms(mix, gmix_ref[...])
    h2 = _bf(_rms(x1, gpre_ref[...]))
    row8 = lax.broadcasted_iota(jnp.int32, (SUBLANES, FFN_TN), 0)
    starts = list(range(0, D_FF, FFN_TN))
    up = lambda n0: (_dot(h2, wg_ref[:, n0:n0 + FFN_TN]), _dot(h2, wu_ref[:, n0:n0 + FFN_TN]))
    nxt = up(starts[0])
    for i, n0 in enumerate(starts):
        g, u = nxt
        if i + 1 < len(starts):
            nxt = up(starts[i + 1])
        prev = carry_ref[:, n0:n0 + FFN_TN]
        g1, g2 = pltpu.roll(g, 1, 0), pltpu.roll(g, 2, 0)
        top1 = jnp.where(row8 == 0, prev[7:8, :], g1[0:SUBLANES])
        top2 = jnp.where(row8 == 0, prev[6:7, :], jnp.where(row8 == 1, prev[7:8, :], g2[0:SUBLANES]))
        g1 = jnp.concatenate([top1, g1[SUBLANES:]], axis=0)
        g2 = jnp.concatenate([top2, g2[SUBLANES:]], axis=0)
        cw = cw_ref[:, n0:n0 + FFN_TN]
        gate = cw[0:1, :] * g2 + cw[1:2, :] * g1 + cw[2:3, :] * g
        a = _gelu_tanh(gate) * u
        act_ref[:, n0:n0 + FFN_TN] = _bf(a)
        carry_ref[:, n0:n0 + FFN_TN] = g[tm - SUBLANES:tm, :]
    acc = _dot(act_ref[...], wd_ref[...])
    x2_ref[0] = x1 + _rms(acc, gpost_ref[...])
    tail_ref[0] = carry_ref[...]


def _ffn_seq(x, ys, w_out_bf, gmix, gpre, wg, wu, wd, cw, gpost, tm):
    b, l, _ = x.shape
    const = lambda shape: pl.BlockSpec(shape, lambda i, j: (0,) * len(shape), pipeline_mode=pl.Buffered(1))
    row = lambda n: pl.BlockSpec((1, tm, n), lambda i, j: (i, j, 0))
    return pl.pallas_call(
        functools.partial(_ffn_seq_body, tm=tm),
        grid=(b, l // tm),
        in_specs=[row(D_MODEL)] + [row(D_HEADS)] * 4
        + [const((D_MODEL, D_MODEL)), const((1, D_MODEL)), const((1, D_MODEL)),
           const((D_MODEL, D_FF)), const((D_MODEL, D_FF)), const((D_FF, D_MODEL)),
           const((FFN_CONV, D_FF)), const((1, D_MODEL))],
        out_specs=[pl.BlockSpec((1, tm, D_MODEL), lambda i, j: (i, j, 0)),
                   pl.BlockSpec((1, SUBLANES, D_FF), lambda i, j: (i, 0, 0))],
        out_shape=[jax.ShapeDtypeStruct((b, l, D_MODEL), F32),
                   jax.ShapeDtypeStruct((b, SUBLANES, D_FF), F32)],
        scratch_shapes=[pltpu.VMEM((SUBLANES, D_FF), F32), pltpu.VMEM((tm, D_FF), BF16)],
        compiler_params=_cparams("parallel", "arbitrary"),
        name="ffn_seq",
    )(x, *ys, w_out_bf, gmix.reshape(1, -1), gpre.reshape(1, -1), wg, wu, wd, cw, gpost.reshape(1, -1))


def _ffn_step_body(x1_ref, h2_ref, b0_ref, b1_ref, wg_ref, wu_ref, wd_ref, cw_ref, gpost_ref, x2_ref, g_ref):
    h2 = h2_ref[...]
    acc = jnp.zeros(x1_ref.shape, F32)
    for n0 in range(0, D_FF, FFN_TN):
        sl = slice(n0, n0 + FFN_TN)
        g = _dot(h2, wg_ref[:, sl])
        cw = cw_ref[:, sl]
        gate = cw[0:1, :] * b0_ref[:, sl] + cw[1:2, :] * b1_ref[:, sl] + cw[2:3, :] * g
        u = _dot(h2, wu_ref[:, sl])
        acc = acc + _dot(_bf(_gelu_tanh(gate) * u), wd_ref[sl, :])
        g_ref[:, sl] = g
    x2_ref[...] = x1_ref[...] + _rms(acc, gpost_ref[...])


def _ffn_step(x1, h2, b0, b1, wg, wu, wd, cw, gpost):
    m = x1.shape[0]
    return pl.pallas_call(
        _ffn_step_body,
        out_shape=[jax.ShapeDtypeStruct((m, D_MODEL), F32), jax.ShapeDtypeStruct((m, D_FF), F32)],
        compiler_params=pltpu.CompilerParams(vmem_limit_bytes=VMEM_LIMIT),
        name="ffn_step",
    )(x1, h2, b0, b1, wg, wu, wd, cw, gpost.reshape(1, -1))


def _rope_tables(pos):
    half = HEAD_DIM // 2
    inv = ROPE_BASE ** (-jnp.linspace(0.0, 1.0, half, dtype=F32))
    ang = pos.astype(F32)[:, None] * inv[None, :]
    return jnp.tile(jnp.cos(ang), (1, LANES // half)), jnp.tile(jnp.sin(ang), (1, LANES // half))


def _rope128(x, cos, sin):
    lane = lax.broadcasted_iota(jnp.int32, x.shape, 1)
    first = (lane % HEAD_DIM) < (HEAD_DIM // 2)
    other = jnp.where(first, -pltpu.roll(x, LANES - HEAD_DIM // 2, 1), pltpu.roll(x, HEAD_DIM // 2, 1))
    return x * cos + other * sin


def _rope256(x, cos, sin):
    return jnp.concatenate([_rope128(x[:, :LANES], cos, sin), _rope128(x[:, LANES:], cos, sin)], axis=1)


def _ret_log_decay(h):
    return math.log(1.0 - 2.0 ** (-5.0 - h))


def _head(x, h):
    return x[:, h * HEAD_DIM:(h + 1) * HEAD_DIM]


def _ret_body(z_ref, cos_ref, sin_ref, nw_ref, y_ref, s_ref, *, chunk):
    @pl.when(pl.program_id(1) == 0)
    def _():
        s_ref[...] = jnp.zeros_like(s_ref)

    nb = z_ref.shape[0]
    cos, sin = cos_ref[...], sin_ref[...]
    t = lax.broadcasted_iota(jnp.int32, (chunk, chunk), 0)
    s = lax.broadcasted_iota(jnp.int32, (chunk, chunk), 1)
    causal = t >= s
    diff = jnp.where(causal, t - s, 0).astype(F32)
    tcol = lax.broadcasted_iota(jnp.int32, (chunk, 1), 0).astype(F32)
    lgs = [_ret_log_decay(h) for h in range(N_HEADS)]
    dmat = [jnp.where(causal, jnp.exp(lg * diff), 0.0) for lg in lgs]
    xi = [jnp.exp(lg * (tcol + 1.0)) for lg in lgs]
    zeta = [jnp.exp(lg * (chunk - 1.0 - tcol)) for lg in lgs]
    probs = [(bb, h) for bb in range(nb) for h in range(N_HEADS)]
    qkv = []
    for bb in range(nb):
        q = _rope256(z_ref[bb, :, 0:256], cos, sin)
        k = _rope256(z_ref[bb, :, 256:512], cos, sin) * SCALE
        qkv.append((q, k, z_ref[bb, :, 512:768]))
    qh = [_head(qkv[bb][0], h) for bb, h in probs]
    kh = [_head(qkv[bb][1], h) for bb, h in probs]
    vh = [_bf(_head(qkv[bb][2], h)) for bb, h in probs]
    st = [s_ref[bb, h] for bb, h in probs]
    sc = [_dot_nt(_bf(q), _bf(k)) for q, k in zip(qh, kh)]
    cross = [_dot(_bf(q * xi[h]), _bf(s0)) for q, s0, (_, h) in zip(qh, st, probs)]
    upd = [_dot_tn(_bf(k * zeta[h]), v) for k, v, (_, h) in zip(kh, vh, probs)]
    o = [_dot(_bf(x * dmat[h]), v) + c for x, v, c, (_, h) in zip(sc, vh, cross, probs)]
    for i, (bb, h) in enumerate(probs):
        s_ref[bb, h] = st[i] * math.exp(lgs[h] * chunk) + upd[i]
    for bb in range(nb):
        outs = [_head_rms(o[bb * N_HEADS + h]) for h in range(N_HEADS)]
        y_ref[bb] = jnp.concatenate(outs, axis=1) * nw_ref[...] * _silu(z_ref[bb, :, 768:1024])


def _retention_seq(z_ret, cos, sin, nw, chunk, nb):
    b, l, _ = z_ret.shape
    return pl.pallas_call(
        functools.partial(_ret_body, chunk=chunk),
        grid=(b // nb, l // chunk),
        in_specs=[pl.BlockSpec((nb, chunk, 1024), lambda i, j: (i, j, 0)),
                  pl.BlockSpec((chunk, LANES), lambda i, j: (j, 0)),
                  pl.BlockSpec((chunk, LANES), lambda i, j: (j, 0)),
                  pl.BlockSpec((1, D_HEADS), lambda i, j: (0, 0))],
        out_specs=[pl.BlockSpec((nb, chunk, D_HEADS), lambda i, j: (i, j, 0)),
                   pl.BlockSpec((nb, N_HEADS, HEAD_DIM, HEAD_DIM), lambda i, j: (i, 0, 0, 0))],
        out_shape=[jax.ShapeDtypeStruct((b, l, D_HEADS), F32),
                   jax.ShapeDtypeStruct((b, N_HEADS, HEAD_DIM, HEAD_DIM), F32)],
        compiler_params=_cparams("parallel", "arbitrary"),
        name="retention_seq",
    )(z_ret, cos, sin, nw.reshape(1, -1))


def _lanes_to_rows(x, n=2 * SUBLANES):
    sel = (lax.broadcasted_iota(jnp.int32, (n, x.shape[1]), 0)
           == lax.broadcasted_iota(jnp.int32, (n, x.shape[1]), 1)).astype(F32)
    return _dot_nt(sel, x, precision=HI)


def _tri_incl(n):
    t = lax.broadcasted_iota(jnp.int32, (n, n), 0)
    s = lax.broadcasted_iota(jnp.int32, (n, n), 1)
    return t >= s


def _ml_body(z_ref, sm_ref, bias_ref, nw_ref, y_ref, c_ref, n_ref, m_ref, *, chunk):
    @pl.when(pl.program_id(1) == 0)
    def _():
        c_ref[...] = jnp.zeros_like(c_ref)
        n_ref[...] = jnp.zeros_like(n_ref)
        m_ref[...] = jnp.zeros_like(m_ref)

    nb = z_ref.shape[0]
    incl = _tri_incl(chunk)
    tri = incl.astype(F32)
    probs = [(bb, h) for bb in range(nb) for h in range(N_HEADS)]
    gate = []
    for bb in range(nb):
        pre = sm_ref[bb] + bias_ref[...]
        logf = jnp.minimum(pre, 0.0) - jnp.log(1.0 + jnp.exp(-jnp.abs(pre)))
        bcum = _dot(tri, logf, precision=HI)
        gate.append((pre, bcum, _lanes_to_rows(pre), _lanes_to_rows(bcum), m_ref[bb]))
    qh = [_head(z_ref[bb, :, 0:256], h) for bb, h in probs]
    kh = [_head(z_ref[bb, :, 256:512], h) * SCALE for bb, h in probs]
    vh = [_bf(_head(z_ref[bb, :, 512:768], h)) for bb, h in probs]
    cm = [c_ref[bb, h] for bb, h in probs]
    nv = [n_ref[bb, h:h + 1, :] for bb, h in probs]
    qk_raw = [_dot_nt(_bf(q), _bf(k)) for q, k in zip(qh, kh)]
    q_c = [_dot(_bf(q), _bf(c)) for q, c in zip(qh, cm)]
    i_col, b_col, inter, dlog = [], [], [], []
    for bb, h in probs:
        pre, bcum, pre_t, bcum_t, m_all = gate[bb]
        i_col.append(pre[:, SM_MI + h:SM_MI + h + 1])
        b_col.append(bcum[:, SM_MF + h:SM_MF + h + 1])
        inter.append(b_col[-1] + m_all[:, h:h + 1])
        dlog.append(jnp.where(incl, b_col[-1] - bcum_t[SM_MF + h:SM_MF + h + 1, :]
                              + pre_t[SM_MI + h:SM_MI + h + 1, :], NEG_INF))
    dmax = [jnp.max(x, axis=1, keepdims=True) for x in dlog]
    q_n = [jnp.sum(q * n, axis=1, keepdims=True) for q, n in zip(qh, nv)]
    stab = []
    for i in range(len(probs)):
        m_t = jnp.maximum(inter[i], dmax[i])
        m_end = m_t[chunk - 1:chunk, :]
        stab.append(dict(m_t=m_t, w=jnp.exp(dlog[i] - m_t), g_in=jnp.exp(inter[i] - m_t), m_end=m_end,
                         w_end=jnp.exp(b_col[i][chunk - 1:chunk, :] - b_col[i] + i_col[i] - m_end),
                         g_end=jnp.exp(inter[i][chunk - 1:chunk, :] - m_end)))
    qk = [x * d["w"] for x, d in zip(qk_raw, stab)]
    num = [_dot(_bf(x), v) + d["g_in"] * c for x, v, d, c in zip(qk, vh, stab, q_c)]
    upd = [_dot_tn(_bf(k * d["w_end"]), v) for k, d, v in zip(kh, stab, vh)]
    qk_sum = [jnp.sum(x, axis=1, keepdims=True) for x in qk]
    k_sum = [jnp.sum(d["w_end"] * k, axis=0, keepdims=True) for d, k in zip(stab, kh)]
    hh = []
    for i, (bb, h) in enumerate(probs):
        d = stab[i]
        den = qk_sum[i] + d["g_in"] * q_n[i]
        hh.append(num[i] / jnp.maximum(jnp.abs(den), jnp.exp(-d["m_t"])))
        c_ref[bb, h] = d["g_end"] * cm[i] + upd[i]
        n_ref[bb, h:h + 1, :] = d["g_end"] * nv[i] + k_sum[i]
    lane = lax.broadcasted_iota(jnp.int32, (1, LANES), 1)
    for bb in range(nb):
        m_out = jnp.zeros((1, LANES), F32)
        outs = []
        for h in range(N_HEADS):
            i = bb * N_HEADS + h
            m_out = jnp.where(lane == h, stab[i]["m_end"], m_out)
            outs.append(_head_rms(_sigmoid(_head(z_ref[bb, :, 768:1024], h)) * hh[i]))
        m_ref[bb] = m_out
        y_ref[bb] = jnp.concatenate(outs, axis=1) * nw_ref[...]


def _mlstm_seq(z_ml, small, bias_vec, nw, chunk, nb):
    b, l, _ = z_ml.shape
    return pl.pallas_call(
        functools.partial(_ml_body, chunk=chunk),
        grid=(b // nb, l // chunk),
        in_specs=[pl.BlockSpec((nb, chunk, 1024), lambda i, j: (i, j, 0)),
                  pl.BlockSpec((nb, chunk, LANES), lambda i, j: (i, j, 0)),
                  pl.BlockSpec((1, LANES), lambda i, j: (0, 0)),
                  pl.BlockSpec((1, D_HEADS), lambda i, j: (0, 0))],
        out_specs=[pl.BlockSpec((nb, chunk, D_HEADS), lambda i, j: (i, j, 0)),
                   pl.BlockSpec((nb, N_HEADS, HEAD_DIM, HEAD_DIM), lambda i, j: (i, 0, 0, 0)),
                   pl.BlockSpec((nb, N_HEADS, HEAD_DIM), lambda i, j: (i, 0, 0)),
                   pl.BlockSpec((nb, 1, LANES), lambda i, j: (i, 0, 0))],
        out_shape=[jax.ShapeDtypeStruct((b, l, D_HEADS), F32),
                   jax.ShapeDtypeStruct((b, N_HEADS, HEAD_DIM, HEAD_DIM), F32),
                   jax.ShapeDtypeStruct((b, N_HEADS, HEAD_DIM), F32),
                   jax.ShapeDtypeStruct((b, 1, LANES), F32)],
        compiler_params=_cparams("parallel", "arbitrary"),
        name="mlstm_seq",
    )(z_ml, small, bias_vec, nw.reshape(1, -1))


def _split_bf16(x):
    hi = _bf(x)
    return hi, _bf(x - hi.astype(F32))


def _dot3(a, b):
    (ah, al), (bh, bl) = a, b
    return _dot(ah, bh) + (_dot(ah, bl) + _dot(al, bh))


def _l2norm(x):
    return x * lax.rsqrt(jnp.sum(x * x, axis=-1, keepdims=True) + EPS)


def _gdn_body(z_ref, sm_ref, cw_ref, dtb_ref, alog_ref, nw_ref, y_ref, s_ref, buf_ref, *, chunk):
    @pl.when(pl.program_id(1) == 0)
    def _():
        s_ref[...] = jnp.zeros_like(s_ref)
        buf_ref[:, 0:SUBLANES, :] = jnp.zeros((buf_ref.shape[0], SUBLANES, 3 * D_HEADS), F32)

    nb = z_ref.shape[0]
    rows = z_ref.shape[1]
    nc = rows // chunk
    incl = _tri_incl(chunk)
    strict = lax.broadcasted_iota(jnp.int32, (chunk, chunk), 0) > lax.broadcasted_iota(jnp.int32, (chunk, chunk), 1)
    eye = (lax.broadcasted_iota(jnp.int32, (chunk, chunk), 0)
           == lax.broadcasted_iota(jnp.int32, (chunk, chunk), 1)).astype(F32)
    tri = incl.astype(F32)
    gates, acts = {}, []
    for bb in range(nb):
        buf_ref[bb, SUBLANES:SUBLANES + rows, :] = z_ref[bb, :, 0:768]
        conv = None
        for j in range(GDN_CONV):
            term = buf_ref[bb, pl.ds(SUBLANES - (GDN_CONV - 1) + j, rows), :] * cw_ref[j:j + 1, :]
            conv = term if conv is None else conv + term
        buf_ref[bb, 0:SUBLANES, :] = buf_ref[bb, rows:rows + SUBLANES, :]
        acts.append(_silu(conv))
        sm = sm_ref[bb]
        g_all = -jnp.exp(alog_ref[...]) * _softplus(sm + dtb_ref[...])
        beta_all = _sigmoid(sm)
        for cc in range(nc):
            rs = slice(cc * chunk, (cc + 1) * chunk)
            gcum = _dot(tri, g_all[rs], precision=HI)
            gates[bb, cc] = (gcum, _lanes_to_rows(gcum), beta_all[rs])

    probs = [(bb, cc, h) for cc in range(nc) for bb in range(nb) for h in range(N_HEADS)]
    pre = []
    for bb, cc, h in probs:
        rs = slice(cc * chunk, (cc + 1) * chunk)
        gcum, gcum_t, beta_all = gates[bb, cc]
        g_col = gcum[:, SM_GA + h:SM_GA + h + 1]
        g_row = gcum_t[SM_GA + h:SM_GA + h + 1, :]
        beta = beta_all[:, SM_GB + h:SM_GB + h + 1]
        decay = jnp.where(incl, jnp.exp(jnp.where(incl, g_col - g_row, 0.0)), 0.0)
        act = acts[bb]
        qh = _l2norm(_head(act[rs, 0:256], h)) * SCALE
        kh = _l2norm(_head(act[rs, 256:512], h))
        vh = _head(act[rs, 512:768], h)
        eg = jnp.exp(g_col)
        pre.append(dict(g_col=g_col, beta=beta, decay=decay, qh=qh, kh=kh, kb=_bf(kh), eg=eg,
                        rhs=jnp.concatenate([vh * beta, kh * (beta * eg)], axis=1)))
    a_l = [jnp.where(strict, d["beta"] * _dot_nt(d["kb"], d["kb"]) * d["decay"], 0.0) for d in pre]
    p_l = [eye - a for a in a_l]
    pw_s = [_split_bf16(a) for a in a_l]
    lvl = 2
    while lvl < chunk:
        pw_s = [_split_bf16(_dot3(x, x)) for x in pw_s]
        p_l = [p + _dot3(_split_bf16(p), x) for p, x in zip(p_l, pw_s)]
        lvl *= 2
    uw_l = [_dot3(_split_bf16(p), _split_bf16(d["rhs"])) for p, d in zip(p_l, pre)]
    qk_l = [_dot_nt(_bf(d["qh"]), d["kb"]) * d["decay"] for d in pre]
    per_chunk = nb * N_HEADS
    st_l = [s_ref[bb, h] for bb, cc, h in probs[:per_chunk]]
    o_all = []
    for cc in range(nc):
        sl = slice(cc * per_chunk, (cc + 1) * per_chunk)
        stb_l = [_bf(st) for st in st_l]
        delta_l = [uw[:, 0:HEAD_DIM] - _dot(_bf(uw[:, HEAD_DIM:]), stb) for uw, stb in zip(uw_l[sl], stb_l)]
        o_all += [_dot(_bf(qk), _bf(delta)) + _dot(_bf(d["qh"] * d["eg"]), stb)
                  for qk, delta, d, stb in zip(qk_l[sl], delta_l, pre[sl], stb_l)]
        new_st = []
        for d, st, delta in zip(pre[sl], st_l, delta_l):
            g_end = d["g_col"][chunk - 1:chunk, :]
            w_end = jnp.exp(g_end - d["g_col"])
            new_st.append(jnp.exp(g_end) * st + _dot_tn(_bf(d["kh"] * w_end), _bf(delta)))
        st_l = new_st
    for (bb, cc, h), st in zip(probs[:per_chunk], st_l):
        s_ref[bb, h] = st
    for bb in range(nb):
        for cc in range(nc):
            outs = [_head_rms(o_all[cc * per_chunk + bb * N_HEADS + h]) for h in range(N_HEADS)]
            rs = slice(cc * chunk, (cc + 1) * chunk)
            y_ref[bb, rs, :] = jnp.concatenate(outs, axis=1) * nw_ref[...] * _silu(z_ref[bb, rs, 768:1024])


def _gdn_seq(z_gdn, small, conv_w, dtb_vec, alog_vec, nw, chunk, nb, nc):
    b, l, _ = z_gdn.shape
    rows = nc * chunk
    return pl.pallas_call(
        functools.partial(_gdn_body, chunk=chunk),
        grid=(b // nb, l // rows),
        in_specs=[pl.BlockSpec((nb, rows, 1024), lambda i, j: (i, j, 0)),
                  pl.BlockSpec((nb, rows, LANES), lambda i, j: (i, j, 0)),
                  pl.BlockSpec((GDN_CONV, 3 * D_HEADS), lambda i, j: (0, 0)),
                  pl.BlockSpec((1, LANES), lambda i, j: (0, 0)),
                  pl.BlockSpec((1, LANES), lambda i, j: (0, 0)),
                  pl.BlockSpec((1, D_HEADS), lambda i, j: (0, 0))],
        out_specs=[pl.BlockSpec((nb, rows, D_HEADS), lambda i, j: (i, j, 0)),
                   pl.BlockSpec((nb, N_HEADS, HEAD_DIM, HEAD_DIM), lambda i, j: (i, 0, 0, 0))],
        out_shape=[jax.ShapeDtypeStruct((b, l, D_HEADS), F32),
                   jax.ShapeDtypeStruct((b, N_HEADS, HEAD_DIM, HEAD_DIM), F32)],
        scratch_shapes=[pltpu.VMEM((nb, SUBLANES + rows, 3 * D_HEADS), F32)],
        compiler_params=_cparams("parallel", "arbitrary"),
        name="gdn_seq",
    )(z_gdn, small, conv_w, dtb_vec, alog_vec, nw.reshape(1, -1))


STEP_SEQS = 4


def _row_to_col(row):
    n = row.shape[1]
    eye = lax.broadcasted_iota(jnp.int32, (n, n), 0) == lax.broadcasted_iota(jnp.int32, (n, n), 1)
    return jnp.sum(jnp.where(eye, jnp.broadcast_to(row, (n, n)), 0.0), axis=1, keepdims=True)


def _vec_mat(col, mat):
    return jnp.sum(col * mat, axis=0, keepdims=True)


def _step_body(zr_ref, zm_ref, zg_ref, sm_ref, cos_ref, sin_ref, mlb_ref, dtb_ref, alog_ref, cw_ref,
               nwr_ref, nwm_ref, nwg_ref, sr_ref, mc_ref, mn_ref, mm_ref, gs_ref, gc_ref,
               yr_ref, ym_ref, yg_ref, sr_o, mc_o, mn_o, mm_o, gs_o, gc_o):
    nb = zr_ref.shape[0]
    d = HEAD_DIM
    eye = lax.broadcasted_iota(jnp.int32, (d, d), 0) == lax.broadcasted_iota(jnp.int32, (d, d), 1)
    lane = lax.broadcasted_iota(jnp.int32, (1, LANES), 1)
    cos, sin = cos_ref[...], sin_ref[...]
    rowsum = lambda x: jnp.sum(x, axis=1, keepdims=True)

    def vec_mat(row, mat):
        return _dot(_pad_rows(row), mat, precision=HI)[0:1, :]

    def outer(a_row, b_row):
        diag = jnp.where(eye, jnp.broadcast_to(a_row, (d, d)), 0.0)
        return _dot(diag, jnp.broadcast_to(b_row, (d, d)), precision=HI)

    ret, ml, gdn, seqs = [], [], [], []
    for bb in range(nb):
        sm, zr, zm, zg = sm_ref[bb], zr_ref[bb], zm_ref[bb], zg_ref[bb]
        rq = _rope256(zr[:, 0:256], cos, sin)
        rk = _rope256(zr[:, 256:512], cos, sin) * SCALE
        pre = sm + mlb_ref[...]
        logf = jnp.minimum(pre, 0.0) - jnp.log(1.0 + jnp.exp(-jnp.abs(pre)))
        m_all = mm_ref[bb]
        x = zg[:, 0:768]
        buf = gc_ref[bb]
        conv = x * cw_ref[GDN_CONV - 1:GDN_CONV, :]
        for j in range(GDN_CONV - 1):
            conv = conv + buf[j:j + 1, :] * cw_ref[j:j + 1, :]
        act = _silu(conv)
        g_all = -jnp.exp(alog_ref[...]) * _softplus(sm + dtb_ref[...])
        beta_all = _sigmoid(sm)
        seqs.append(dict(zr=zr, zm=zm, zg=zg, x=x, buf=buf))
        for h in range(N_HEADS):
            ret.append(dict(q=_head(rq, h), k=_head(rk, h), v=_head(zr[:, 512:768], h), st=sr_ref[bb, h],
                            gam=math.exp(_ret_log_decay(h))))
            i_g = pre[:, SM_MI + h:SM_MI + h + 1]
            inter = logf[:, SM_MF + h:SM_MF + h + 1] + m_all[:, h:h + 1]
            m_t = jnp.maximum(inter, i_g)
            ml.append(dict(q=_head(zm[:, 0:256], h), k=_head(zm[:, 256:512], h) * SCALE, v=_head(zm[:, 512:768], h),
                           cm=mc_ref[bb, h], nv=mn_ref[bb, h:h + 1, :], m_t=m_t, w=jnp.exp(i_g - m_t),
                           g_in=jnp.exp(inter - m_t)))
            gdn.append(dict(q=_head(act[:, 0:256], h), k=_head(act[:, 256:512], h), v=_head(act[:, 512:768], h),
                            st=gs_ref[bb, h], eg=jnp.exp(g_all[:, SM_GA + h:SM_GA + h + 1]),
                            beta=beta_all[:, SM_GB + h:SM_GB + h + 1]))
    for p in gdn:
        p["q"], p["k"] = _l2norm(p["q"]) * SCALE, _l2norm(p["k"])
    for p in ret + ml + gdn:
        p["qk"] = rowsum(p["q"] * p["k"])
    for p in ml:
        p["qn"] = rowsum(p["q"] * p["nv"])
    for p in ret:
        p["qs"] = vec_mat(p["q"] * p["gam"], p["st"])
    for p in ml:
        p["qc"] = vec_mat(p["q"], p["cm"])
    for p in gdn:
        p["ks"] = vec_mat(p["k"] * (p["beta"] * p["eg"]), p["st"])
        p["qs"] = vec_mat(p["q"] * p["eg"], p["st"])
    for p in ret:
        p["o"] = p["qk"] * p["v"] + p["qs"]
    for p in ml:
        qk = p["qk"] * p["w"]
        den = qk + p["g_in"] * p["qn"]
        p["o"] = (qk * p["v"] + p["g_in"] * p["qc"]) / jnp.maximum(jnp.abs(den), jnp.exp(-p["m_t"]))
    for p in gdn:
        p["delta"] = p["v"] * p["beta"] - p["ks"]
        p["o"] = p["qk"] * p["delta"] + p["qs"]
    for p in ret:
        p["new"] = p["st"] * p["gam"] + outer(p["k"], p["v"])
    for p in ml:
        p["new"] = p["g_in"] * p["cm"] + outer(p["k"] * p["w"], p["v"])
    for p in gdn:
        p["new"] = p["eg"] * p["st"] + outer(p["k"], p["delta"])
    for bb in range(nb):
        s = seqs[bb]
        ps = slice(bb * N_HEADS, (bb + 1) * N_HEADS)
        m_out = jnp.zeros((1, LANES), F32)
        for h, (pr, pm, pg) in enumerate(zip(ret[ps], ml[ps], gdn[ps])):
            sr_o[bb, h] = pr["new"]
            mc_o[bb, h] = pm["new"]
            mn_o[bb, h:h + 1, :] = pm["g_in"] * pm["nv"] + pm["w"] * pm["k"]
            gs_o[bb, h] = pg["new"]
            m_out = jnp.where(lane == h, pm["m_t"], m_out)
        mm_o[bb] = m_out
        gc_o[bb, 0:GDN_CONV - 2, :] = s["buf"][1:GDN_CONV - 1, :]
        gc_o[bb, GDN_CONV - 2:GDN_CONV - 1, :] = s["x"]
        yr_ref[bb] = (jnp.concatenate([_head_rms(p["o"]) for p in ret[ps]], axis=1) * nwr_ref[...]
                      * _silu(s["zr"][:, 768:1024]))
        og = s["zm"][:, 768:1024]
        ym_ref[bb] = jnp.concatenate([_head_rms(_sigmoid(_head(og, h)) * p["o"])
                                      for h, p in enumerate(ml[ps])], axis=1) * nwm_ref[...]
        yg_ref[bb] = (jnp.concatenate([_head_rms(p["o"]) for p in gdn[ps]], axis=1) * nwg_ref[...]
                      * _silu(s["zg"][:, 768:1024]))


def _mixers_step(z_ret, z_ml, z_gdn, small, cos, sin, mlb, dtb, alog, conv_w, nwr, nwm, nwg,
                 s_ret, ml_c, ml_n, ml_m, gdn_s, gdn_conv):
    b = z_ret.shape[0]
    nb = STEP_SEQS if b % STEP_SEQS == 0 else 1
    row = lambda n: pl.BlockSpec((nb, 1, n), lambda i: (i, 0, 0))
    vec = lambda n: pl.BlockSpec((1, n), lambda i: (0, 0))
    mat = pl.BlockSpec((nb, N_HEADS, HEAD_DIM, HEAD_DIM), lambda i: (i, 0, 0, 0))
    nsp = pl.BlockSpec((nb, N_HEADS, HEAD_DIM), lambda i: (i, 0, 0))
    csp = pl.BlockSpec((nb, GDN_CONV - 1, 3 * D_HEADS), lambda i: (i, 0, 0))
    mshape = jax.ShapeDtypeStruct((b, N_HEADS, HEAD_DIM, HEAD_DIM), F32)
    yshape = jax.ShapeDtypeStruct((b, 1, D_HEADS), F32)
    return pl.pallas_call(
        _step_body,
        grid=(b // nb,),
        in_specs=[row(1024), row(1024), row(1024), row(LANES), vec(LANES), vec(LANES), vec(LANES), vec(LANES), vec(LANES),
                  pl.BlockSpec((GDN_CONV, 3 * D_HEADS), lambda i: (0, 0)), vec(D_HEADS), vec(D_HEADS), vec(D_HEADS),
                  mat, mat, nsp, row(LANES), mat, csp],
        out_specs=[row(D_HEADS), row(D_HEADS), row(D_HEADS), mat, mat, nsp, row(LANES), mat, csp],
        out_shape=[yshape, yshape, yshape, mshape, mshape,
                   jax.ShapeDtypeStruct((b, N_HEADS, HEAD_DIM), F32), jax.ShapeDtypeStruct((b, 1, LANES), F32),
                   mshape, jax.ShapeDtypeStruct((b, GDN_CONV - 1, 3 * D_HEADS), F32)],
        compiler_params=_cparams("parallel"),
        name="mixers_step",
    )(z_ret, z_ml, z_gdn, small, cos, sin, mlb, dtb, alog, conv_w,
      nwr.reshape(1, -1), nwm.reshape(1, -1), nwg.reshape(1, -1), s_ret, ml_c, ml_n, ml_m, gdn_s, gdn_conv)


def _t5_bucket_np(n):
    exact = T5_BUCKETS // 2
    n = np.maximum(np.asarray(n, np.int64), 0)
    x = np.maximum(n, 1).astype(np.float32) / np.float32(exact)
    large = exact + (np.log(x) / np.float32(math.log(T5_MAX_DIST / exact)) * np.float32(T5_BUCKETS - exact)).astype(np.int32)
    return np.where(n < exact, n, np.minimum(large, T5_BUCKETS - 1)).astype(np.int32)


def _bucket_thresholds():
    b = _t5_bucket_np(np.arange(4 * T5_MAX_DIST))
    return [int(np.argmax(b >= k)) for k in range(T5_BUCKETS // 2 + 1, T5_BUCKETS)]


def _bias_table_body(rbt_ref, oh_ref, o_ref):
    o_ref[...] = _dot(rbt_ref[...], oh_ref[...].astype(F32), precision=HI)


def _bias_tables(rel_bias, dists):
    n = len(dists)
    tn = 4096
    npad = -(-n // tn) * tn
    onehot = np.zeros((T5_BUCKETS, npad), np.float32)
    onehot[_t5_bucket_np(dists), np.arange(n)] = 1.0
    rbt = jnp.pad(rel_bias.T, ((0, SUBLANES - N_HEADS), (0, 0)))
    out = pl.pallas_call(
        _bias_table_body,
        grid=(npad // tn,),
        in_specs=[pl.BlockSpec((SUBLANES, T5_BUCKETS), lambda i: (0, 0)),
                  pl.BlockSpec((T5_BUCKETS, tn), lambda i: (0, i))],
        out_specs=pl.BlockSpec((SUBLANES, tn), lambda i: (0, i)),
        out_shape=jax.ShapeDtypeStruct((SUBLANES, npad), F32),
        name="bias_tables",
    )(rbt, jnp.asarray(onehot, BF16))
    return out[:, :n]


def _prep_cmp_w(cmp_w, cmp_pe):
    eye2 = jnp.eye(2, dtype=F32)

    def half(w):
        full = jnp.einsum('cjde,cf->jcdfe', w, eye2)
        return full.reshape(CMP_STRIDE * 2 * HEAD_DIM, 2 * HEAD_DIM)

    wab = jnp.concatenate([half(cmp_w[:, :CMP_STRIDE]), half(cmp_w[:, CMP_STRIDE:])], axis=1)
    pe = jnp.concatenate([cmp_pe[:CMP_STRIDE].reshape(1, -1), cmp_pe[CMP_STRIDE:].reshape(1, -1)], axis=1)
    return wab, jnp.pad(pe, ((0, SUBLANES - 1), (0, 0)))


def _pe_term(pe_ref, wab_ref):
    kin = CMP_STRIDE * 2 * HEAD_DIM
    t = (_dot(pe_ref[:, 0:kin], wab_ref[:, 0:LANES].astype(F32), precision=HI)
         + _dot(pe_ref[:, kin:2 * kin], wab_ref[:, LANES:2 * LANES].astype(F32), precision=HI))
    return t[0:1, :]


def _combine_cmp(parts, pe_term):
    rows = parts.shape[0]
    nxt = pltpu.roll(parts[:, LANES:2 * LANES], rows - 1, 0)
    r = lax.broadcasted_iota(jnp.int32, (rows, LANES), 0)
    return jnp.where(r < rows - 1, parts[:, 0:LANES] + nxt + pe_term, 0.0)


def _compress_seq_body(r_ref, wab_ref, pe_ref, o_ref):
    parts = _dot(_bf(r_ref[0]), _bf(wab_ref[...]))
    o_ref[0] = _combine_cmp(parts, _pe_term(pe_ref, wab_ref))


def _compress_seq(rows, wab, pe):
    b, n16, kin = rows.shape
    return pl.pallas_call(
        _compress_seq_body,
        grid=(b,),
        in_specs=[pl.BlockSpec((1, n16, kin), lambda i: (i, 0, 0)),
                  pl.BlockSpec(wab.shape, lambda i: (0, 0)),
                  pl.BlockSpec(pe.shape, lambda i: (0, 0))],
        out_specs=pl.BlockSpec((1, n16, LANES), lambda i: (i, 0, 0)),
        out_shape=jax.ShapeDtypeStruct((b, n16, LANES), F32),
        compiler_params=_cparams("parallel"),
        name="nsa_compress_seq",
    )(rows, wab, pe)


def _bias_from_dist(dist, rb_ref, h):
    n = jnp.maximum(dist, 0)
    exact = T5_BUCKETS // 2
    out = jnp.full(dist.shape, rb_ref[exact, h], F32)
    for k, thr in enumerate(_bucket_thresholds()):
        out = jnp.where(n >= thr, rb_ref[exact + 1 + k, h], out)
    for j in range(exact):
        out = jnp.where(n == j, rb_ref[j, h], out)
    return out


def _softmax_rows(s, valid):
    s = jnp.where(valid, s, NEG_INF)
    m = jnp.max(s, axis=1, keepdims=True)
    m = jnp.where(m == NEG_INF, 0.0, m)
    p = jnp.exp(s - m)
    den = jnp.sum(p, axis=1, keepdims=True)
    return p / jnp.where(den > 0, den, 1.0)


def _top_blocks(score, k):
    lane = lax.broadcasted_iota(jnp.int32, score.shape, 1)
    width = score.shape[1]
    sel = jnp.zeros(score.shape, jnp.bool_)
    for _ in range(k):
        m = jnp.max(score, axis=1, keepdims=True)
        idx = jnp.min(jnp.where(score == m, lane, width), axis=1, keepdims=True)
        pick = (lane == idx) & (m > NEG_INF)
        sel = sel | pick
        score = jnp.where(pick, NEG_INF, score)
    return sel


def _tile_mask(ok):
    return jnp.concatenate([jnp.where(ok, 1.0, 0.0)] * N_HEADS, axis=0) > 0.5


def _stack_heads(x):
    return jnp.concatenate([_head(x, h) for h in range(N_HEADS)], axis=0)


def _toeplitz(tab_ref, h, r0, rows, ncol):
    nrow = (rows + ncol) // LANES
    strip = tab_ref[h, pl.ds(r0, nrow), :]
    flat = jnp.concatenate([strip[r:r + 1, :] for r in range(nrow)], axis=1)
    rolled = pltpu.roll(jnp.broadcast_to(flat, (rows, nrow * LANES)), 0, 1, stride=1, stride_axis=0)
    return rolled[:, rows:]


SLC_KC = 512


MASK_BIG = 1e30


def _top_blocks_t(score, k):
    row = lax.broadcasted_iota(jnp.int32, score.shape, 0)
    n = score.shape[0]
    sel = jnp.zeros(score.shape, jnp.bool_)
    for _ in range(k):
        m = jnp.max(score, axis=0, keepdims=True)
        idx = jnp.min(jnp.where(score == m, row, n), axis=0, keepdims=True)
        pick = (row == idx) & (m > NEG_INF)
        sel = sel | pick
        score = jnp.where(pick, NEG_INF, score)
    return sel


def _bucket_index(dist):
    n = jnp.maximum(dist, 0)
    exact = T5_BUCKETS // 2
    big = jnp.full(dist.shape, exact, jnp.int32)
    for thr in _bucket_thresholds():
        big = big + (n >= thr).astype(jnp.int32)
    return jnp.where(n < exact, n, big)


def _gather_bias(rbt_ref, h, bucket):
    rows, width = bucket.shape
    piece = min(width, LANES)
    tab = jnp.broadcast_to(rbt_ref[h:h + 1, 0:piece], (rows, piece))
    parts = [jnp.take_along_axis(tab, bucket[:, s:s + piece], axis=1) for s in range(0, width, piece)]
    return parts[0] if len(parts) == 1 else jnp.concatenate(parts, axis=1)


def _tile_rows(x, n=N_HEADS):
    return jnp.concatenate([x] * n, axis=0)


def _near_chunks():
    return (_bucket_thresholds()[-1] + SLC_KC - 2) // SLC_KC + 1


def _bias_tile(tab_ref, base, ncol, p0, lo, hi):
    r0 = (p0 - base - Q_BLOCK) // LANES
    bias = jnp.concatenate([_toeplitz(tab_ref, h, r0, Q_BLOCK, ncol) for h in range(N_HEADS)], axis=0)
    dist = (base + lax.broadcasted_iota(jnp.int32, (Q_BLOCK, ncol), 0)
            - lax.broadcasted_iota(jnp.int32, (Q_BLOCK, ncol), 1))
    return bias + _tile_rows(jnp.where((dist >= lo) & (dist < hi), 0.0, -MASK_BIG))


def _nsa_seq_body(rb_ref, q_ref, sm_ref, kx_ref, vs_ref, kvw_ref, kvc_ref, tab_ref, rbt_ref, ovlt_ref, y_ref,
                  stile_ref, wtile_ref, *, seq, p0):
    qb = Q_BLOCK
    rows = N_HEADS * qb
    n16 = seq // CMP_STRIDE
    nblk = seq // SLC_BLOCK
    bi = pl.program_id(1)
    q0 = bi * qb

    @pl.when((pl.program_id(0) == 0) & (bi == 0))
    def _():
        per_chunk = SLC_KC // qb

        def slc_tile(t, carry):
            base = (t // _near_chunks()) * qb + (t % _near_chunks()) * SLC_KC
            stile_ref[t] = _bias_tile(tab_ref, base, SLC_KC, p0, 0, seq + SLC_KC)
            return carry

        def win_tile(t, carry):
            wtile_ref[t] = _bias_tile(tab_ref, t * qb, WINDOW + qb, p0, 0, WINDOW)
            return carry

        lax.fori_loop(0, per_chunk * _near_chunks(), slc_tile, 0)
        lax.fori_loop(0, WINDOW // qb + 1, win_tile, 0)
    qs = _stack_heads(q_ref[0]) * SCALE
    qsb = _bf(qs)
    qpos = q0 + lax.broadcasted_iota(jnp.int32, (qb, 1), 0)

    kvc = kvc_ref[0]
    s_c = _dot_nt(qsb, _bf(kvc[:, 0:HEAD_DIM]))
    band = WINDOW + qb
    sw = pl.multiple_of(jnp.maximum(q0 - WINDOW, 0), qb)
    kw = kvw_ref[0, pl.ds(sw, band), 0:HEAD_DIM]
    vw = kvw_ref[0, pl.ds(sw, band), HEAD_DIM:2 * HEAD_DIM]
    s_w = _dot_nt(qsb, kw)

    cmp_end = lax.broadcasted_iota(jnp.int32, (qb, n16), 1) * CMP_STRIDE + (CMP_LEN - 1)
    dist_c = qpos - cmp_end
    bucket_c = _bucket_index(dist_c)
    bias_c = jnp.concatenate([_gather_bias(rbt_ref, h, bucket_c) for h in range(N_HEADS)], axis=0)
    p_c = _softmax_rows(s_c + bias_c, _tile_mask(dist_c >= 0))
    p_cb = _bf(p_c)
    o_c = _dot(p_cb, _bf(kvc[:, HEAD_DIM:2 * HEAD_DIM]))
    p_r = p_cb.astype(F32)
    p_sum = p_r[0:qb] + p_r[qb:2 * qb] + p_r[2 * qb:3 * qb] + p_r[3 * qb:4 * qb]
    imp_t = _dot_nt(ovlt_ref[...], p_sum, precision=HI)

    base_w = q0 - sw
    sb_w = s_w + wtile_ref[base_w // qb]
    e_w = jnp.exp(sb_w - jnp.max(sb_w, axis=1, keepdims=True))
    o_w = _dot(_bf(e_w / jnp.sum(e_w, axis=1, keepdims=True)), vw)

    qlane = q0 + lax.broadcasted_iota(jnp.int32, (1, qb), 1)
    cur = qlane // SLC_BLOCK
    rowb = lax.broadcasted_iota(jnp.int32, (nblk, qb), 0)
    forced = (rowb == 0) | (rowb == cur) | (rowb == cur - 1)
    score = jnp.where(forced, jnp.inf, imp_t)
    score = jnp.where(rowb > cur, NEG_INF, score)
    sel = jnp.where(_top_blocks_t(score, SLC_TOPK), 1.0, 0.0).T
    negm = _bf(_tile_rows((sel - 1.0) * MASK_BIG))
    q_ext = jnp.concatenate([qsb, negm], axis=1)

    thr_far = _bucket_thresholds()[-1] + SLC_KC - 1
    n_chunks = q0 // SLC_KC + 1
    n_far = jnp.maximum(q0 - thr_far + SLC_KC, 0) // SLC_KC
    n_near = _near_chunks()

    far_bias = jnp.concatenate([jnp.full((qb, 1), rb_ref[T5_BUCKETS - 1, h], F32) for h in range(N_HEADS)], axis=0)

    def scores(c):
        k0 = pl.multiple_of(c * SLC_KC, SLC_KC)
        kx = kx_ref[0, pl.ds(k0, SLC_KC), :]
        return q0 - k0, _dot_nt(q_ext, kx), vs_ref[0, pl.ds(k0, SLC_KC), :]

    def online(carry, s, vs):
        m, l, acc = carry
        m_new = jnp.maximum(m, jnp.max(s, axis=1, keepdims=True))
        alpha = jnp.exp(m - m_new)
        p = jnp.exp(s - m_new)
        return m_new, alpha * l + jnp.sum(p, axis=1, keepdims=True), alpha * acc + _dot(_bf(p), vs)

    def far_step(c, carry):
        _, s, vs = scores(c)
        return online(carry, s + far_bias, vs)

    def near_step(c, carry):
        base, s, vs = scores(c)
        tile = ((q0 % SLC_KC) // qb) * n_near + base // SLC_KC
        return online(carry, s + stile_ref[tile], vs)

    init = (jnp.full((rows, 1), -MASK_BIG / 10, F32), jnp.zeros((rows, 1), F32), jnp.zeros((rows, HEAD_DIM), F32))
    carry = lax.fori_loop(0, n_far, far_step, init)
    _, l_s, acc_s = lax.fori_loop(n_far, n_chunks, near_step, carry)
    o_s = acc_s / jnp.where(l_s > 0, l_s, 1.0)

    gates = _sigmoid(sm_ref[0])
    outs = []
    for h in range(N_HEADS):
        rs = slice(h * qb, (h + 1) * qb)
        g = lambda t: gates[:, SM_NG + 3 * h + t:SM_NG + 3 * h + t + 1]
        outs.append(g(0) * o_c[rs] + g(1) * o_s[rs] + g(2) * o_w[rs])
    y_ref[0] = jnp.concatenate(outs, axis=1)


def _nsa_seq(rel_bias, nq, small, nkv, kvc, tab, seq_p0):
    b, l, _ = nq.shape
    n16 = l // CMP_STRIDE
    nblk = l // SLC_BLOCK
    cmp_start = np.arange(n16)[None, :] * CMP_STRIDE
    slc_start = np.arange(nblk)[:, None] * SLC_BLOCK
    ovlt = ((cmp_start < slc_start + SLC_BLOCK) & (cmp_start + CMP_LEN > slc_start)
            & (np.arange(n16)[None, :] < n16 - 1)).astype(np.float32)
    block_id = (np.arange(l)[:, None] // SLC_BLOCK == np.arange(nblk)[None, :]).astype(np.float32)
    kx = _bf(jnp.concatenate([nkv[..., 2 * HEAD_DIM:3 * HEAD_DIM],
                              jnp.broadcast_to(jnp.asarray(block_id), (b, l, nblk))], axis=-1))
    vs = _bf(nkv[..., 3 * HEAD_DIM:4 * HEAD_DIM])
    kvw = _bf(nkv[..., 4 * HEAD_DIM:6 * HEAD_DIM])
    rbt = jnp.pad(rel_bias.T, ((0, SUBLANES - N_HEADS), (0, LANES - T5_BUCKETS)))
    nrows = tab.shape[1]
    const = lambda shape: pl.BlockSpec(shape, lambda i, j: (0,) * len(shape))
    seq_spec = lambda n: pl.BlockSpec((1, l, n), lambda i, j: (i, 0, 0))
    return pl.pallas_call(
        functools.partial(_nsa_seq_body, seq=l, p0=seq_p0),
        grid=(b, l // Q_BLOCK),
        in_specs=[pl.BlockSpec(memory_space=pltpu.SMEM),
                  pl.BlockSpec((1, Q_BLOCK, D_HEADS), lambda i, j: (i, j, 0)),
                  pl.BlockSpec((1, Q_BLOCK, LANES), lambda i, j: (i, j, 0)),
                  seq_spec(HEAD_DIM + nblk), seq_spec(HEAD_DIM), seq_spec(2 * HEAD_DIM),
                  pl.BlockSpec((1, n16, LANES), lambda i, j: (i, 0, 0)),
                  const((SUBLANES, nrows, LANES)), const((SUBLANES, LANES)), const((nblk, n16))],
        out_specs=pl.BlockSpec((1, Q_BLOCK, D_HEADS), lambda i, j: (i, j, 0)),
        out_shape=jax.ShapeDtypeStruct((b, l, D_HEADS), F32),
        scratch_shapes=[pltpu.VMEM(((SLC_KC // Q_BLOCK) * _near_chunks(), N_HEADS * Q_BLOCK, SLC_KC), F32),
                        pltpu.VMEM((WINDOW // Q_BLOCK + 1, N_HEADS * Q_BLOCK, WINDOW + Q_BLOCK), F32)],
        compiler_params=_cparams("arbitrary", "arbitrary"),
        name="nsa_seq",
    )(rel_bias, nq, small, kx, vs, kvw, kvc, tab, rbt, jnp.asarray(ovlt))


def _nsa_tab_seq_dists(seq):
    max_base = max(seq, SLC_KC + (_near_chunks() - 1) * SLC_KC)
    p0 = max_base + Q_BLOCK
    length = p0 + WINDOW + Q_BLOCK + LANES
    return p0, np.maximum(p0 - np.arange(length), 0)


PAGE_GROUP = 16
PAGES_PER_STEP = 64


def _pad_rows(x, rows=SUBLANES):
    return jnp.concatenate([x, jnp.zeros((rows - x.shape[0], x.shape[1]), x.dtype)], axis=0)


def _nsa_select_body(pt_ref, *refs, past, n_steps):
    del pt_ref
    pages = refs[:PAGES_PER_STEP]
    wab_ref, pe_ref, q_ref, tabc_ref, ovlt_ref, oc_ref, top_ref, parts_ref, rows_ref = refs[PAGES_PER_STEP:]
    step = pl.program_id(1)
    blocks_per_page = PAGE_SIZE // CMP_STRIDE
    group_rows = PAGE_GROUP * blocks_per_page
    n_groups = PAGES_PER_STEP // PAGE_GROUP
    for p, pg in enumerate(pages):
        g, pp = divmod(p, PAGE_GROUP)
        rows_ref[g, pp * PAGE_SIZE:(pp + 1) * PAGE_SIZE, :] = pg[...].reshape(2 * HEAD_DIM, PAGE_SIZE).T
    for g in range(n_groups):
        acc = jnp.zeros((group_rows, 2 * LANES), F32)
        for j in range(CMP_STRIDE):
            xj = rows_ref[g, pl.ds(j, group_rows, stride=CMP_STRIDE), :]
            acc = acc + _dot(_bf(xj), wab_ref[j * LANES:(j + 1) * LANES, :])
        row0 = pl.multiple_of((step * n_groups + g) * group_rows, group_rows)
        parts_ref[pl.ds(row0, group_rows), :] = acc

    @pl.when(step == n_steps - 1)
    def _():
        n16 = past // CMP_STRIDE
        n_cmp = n16 - 1
        cur = past // SLC_BLOCK
        pe_term = _pe_term(pe_ref, wab_ref)
        kvc = _combine_cmp(parts_ref[...], pe_term)
        qs = _pad_rows(_stack_heads(q_ref[0]) * SCALE)
        s_c = _dot_nt(_bf(qs), _bf(kvc[:, 0:HEAD_DIM])) + tabc_ref[...]
        lane = lax.broadcasted_iota(jnp.int32, (SUBLANES, n16), 1)
        p_c = _softmax_rows(s_c, lane < n_cmp)
        p_cb = _bf(p_c)
        o_c = _dot(p_cb, _bf(kvc[:, HEAD_DIM:2 * HEAD_DIM]))
        oc_ref[0] = jnp.concatenate([o_c[h:h + 1, :] for h in range(N_HEADS)], axis=1)
        imp_h = _dot_nt(ovlt_ref[...], p_cb)
        imp = imp_h[:, 0:1] + imp_h[:, 1:2] + imp_h[:, 2:3] + imp_h[:, 3:4]
        nrow = imp.shape[0]
        rowb = lax.broadcasted_iota(jnp.int32, (nrow, 1), 0)
        forced = (rowb == 0) | (rowb == cur) | (rowb == cur - 1)
        score = jnp.where(forced, jnp.inf, imp)
        score = jnp.where(rowb > cur, NEG_INF, score)
        out_lane = lax.broadcasted_iota(jnp.int32, (1, LANES), 1)
        top = jnp.zeros((1, LANES), jnp.int32)
        for r in range(SLC_TOPK):
            m = jnp.max(score, axis=0, keepdims=True)
            idx = jnp.min(jnp.where(score == m, rowb, nrow), axis=0, keepdims=True)
            top = jnp.where(out_lane == r, idx, top)
            score = jnp.where(rowb == idx, NEG_INF, score)
        top_ref[0] = top


def _nsa_select(layer, page_table, cache_t, wab_bf, pe, nq, tabc, past):
    b, n_pages = page_table.shape
    n_steps = n_pages // PAGES_PER_STEP
    n16 = past // CMP_STRIDE
    n_slc = past // SLC_BLOCK + 1
    nrow = -(-n_slc // SUBLANES) * SUBLANES
    cmp_start = np.arange(n16)[None, :] * CMP_STRIDE
    slc_start = np.arange(nrow)[:, None] * SLC_BLOCK
    ovlt = ((cmp_start < slc_start + SLC_BLOCK) & (cmp_start + CMP_LEN > slc_start)
            & (np.arange(n16)[None, :] < n16 - 1) & (np.arange(nrow)[:, None] < n_slc)).astype(np.float32)

    n_pool = cache_t.shape[1]

    def page_spec(p):
        def imap(i, s, pt):
            pg = pt[jnp.minimum(i, b - 1), jnp.minimum(s, n_steps - 1) * PAGES_PER_STEP + p]
            return (layer, jnp.minimum(pg, n_pool - 1), 0, 0, 0)
        return pl.BlockSpec((None, None, 2, HEAD_DIM, PAGE_SIZE), imap)

    const = lambda shape: pl.BlockSpec(shape, lambda i, s, pt: (0,) * len(shape))
    grid_spec = pltpu.PrefetchScalarGridSpec(
        num_scalar_prefetch=1,
        grid=(b, n_steps),
        in_specs=[page_spec(p) for p in range(PAGES_PER_STEP)]
        + [const(wab_bf.shape), const(pe.shape), pl.BlockSpec((1, 1, D_HEADS), lambda i, s, pt: (i, 0, 0)),
           const(tabc.shape), const(ovlt.shape)],
        out_specs=[pl.BlockSpec((1, 1, D_HEADS), lambda i, s, pt: (i, 0, 0)),
                   pl.BlockSpec((1, 1, LANES), lambda i, s, pt: (i, 0, 0))],
        scratch_shapes=[pltpu.VMEM((n16, 2 * LANES), F32),
                        pltpu.VMEM((PAGES_PER_STEP // PAGE_GROUP, PAGE_GROUP * PAGE_SIZE, 2 * HEAD_DIM), F32)],
    )
    return pl.pallas_call(
        functools.partial(_nsa_select_body, past=past, n_steps=n_steps),
        grid_spec=grid_spec,
        out_shape=[jax.ShapeDtypeStruct((b, 1, D_HEADS), F32), jax.ShapeDtypeStruct((b, 1, LANES), jnp.int32)],
        compiler_params=_cparams("parallel", "arbitrary"),
        name="nsa_select",
    )(page_table, *([cache_t] * PAGES_PER_STEP), wab_bf, pe, nq, tabc, jnp.asarray(ovlt, BF16))


def _nsa_attend_body(pt_ref, top_ref, *refs, past):
    del pt_ref
    pages = refs[:SLC_TOPK]
    (q_ref, sm_ref, new_ref, oc_ref, swa_ref, tabs_ref, tabw_ref, y_ref, swa_out) = refs[SLC_TOPK:]
    b = pl.program_id(0)
    new_blk = past // SLC_BLOCK
    blocks_per_page = PAGE_SIZE // SLC_BLOCK
    qs = _bf(_pad_rows(_stack_heads(q_ref[0]) * SCALE))
    new = new_ref[0]
    lane_t = lax.broadcasted_iota(jnp.int32, (HEAD_DIM, PAGE_SIZE), 1)
    lane = lax.broadcasted_iota(jnp.int32, (SUBLANES, PAGE_SIZE), 1)
    new_kt = jnp.where(lane_t == 0, _row_to_col(new[:, 2 * HEAD_DIM:3 * HEAD_DIM]), 0.0)
    new_vt = jnp.where(lane_t == 0, _row_to_col(new[:, 3 * HEAD_DIM:4 * HEAD_DIM]), 0.0)
    scores, oks, vts = [], [], []
    for s in range(SLC_TOPK):
        ti = jnp.clip(top_ref[b, s], 0, new_blk)
        is_new = ti == new_blk
        kt = jnp.where(is_new, new_kt, pages[s][0])
        vts.append(jnp.where(is_new, new_vt, pages[s][1]))
        scores.append(_dot(qs, _bf(kt)) + tabs_ref[ti])
        kpos = ti * SLC_BLOCK + lane % SLC_BLOCK
        ok = (lane // SLC_BLOCK == ti % blocks_per_page) & (kpos <= past)
        oks.append(jnp.where(ok, 1.0, 0.0))
    p_s = _softmax_rows(jnp.concatenate(scores, axis=1), jnp.concatenate(oks, axis=1) > 0.5)
    o_s = None
    for s in range(SLC_TOPK):
        t = _dot_nt(_bf(p_s[:, s * PAGE_SIZE:(s + 1) * PAGE_SIZE]), _bf(vts[s]))
        o_s = t if o_s is None else o_s + t

    wlane = lax.broadcasted_iota(jnp.int32, (HEAD_DIM, WINDOW), 1)
    win = []
    for c in range(2):
        col = _row_to_col(new[:, (4 + c) * HEAD_DIM:(5 + c) * HEAD_DIM])
        win.append(jnp.where(wlane == WINDOW - 1, col, pltpu.roll(swa_ref[c], WINDOW - 1, 1)))
        swa_out[c] = win[c]
    s_w = _dot(qs, _bf(win[0])) + tabw_ref[...]
    p_w = _softmax_rows(s_w, jnp.full(s_w.shape, True))
    o_w = _dot_nt(_bf(p_w), _bf(win[1]))

    gates = _sigmoid(sm_ref[0])
    o_c = oc_ref[0]
    outs = []
    for h in range(N_HEADS):
        g = lambda t: gates[:, SM_NG + 3 * h + t:SM_NG + 3 * h + t + 1]
        outs.append(g(0) * _head(o_c, h) + g(1) * o_s[h:h + 1, :] + g(2) * o_w[h:h + 1, :])
    y_ref[0] = jnp.concatenate(outs, axis=1)


def _nsa_attend(layer, page_table, top, cache_t, nq, small, nkv, o_c, swa_t, tabs, tabw, past):
    b = nq.shape[0]
    last_page = past // PAGE_SIZE - 1
    blocks_per_page = PAGE_SIZE // SLC_BLOCK

    n_pool = cache_t.shape[1]

    def page_spec(s):
        def imap(i, pt, tp):
            ii = jnp.minimum(i, b - 1)
            pg = jnp.clip(tp[ii, s] // blocks_per_page, 0, last_page)
            return (layer, jnp.clip(pt[ii, pg], 0, n_pool - 1), 1, 0, 0)
        return pl.BlockSpec((None, None, 2, HEAD_DIM, PAGE_SIZE), imap)

    row = lambda n: pl.BlockSpec((1, 1, n), lambda i, pt, tp: (i, 0, 0))
    const = lambda shape: pl.BlockSpec(shape, lambda i, pt, tp: (0,) * len(shape))
    grid_spec = pltpu.PrefetchScalarGridSpec(
        num_scalar_prefetch=2,
        grid=(b,),
        in_specs=[page_spec(s) for s in range(SLC_TOPK)]
        + [row(D_HEADS), row(LANES), row(6 * HEAD_DIM), row(D_HEADS),
           pl.BlockSpec((None, None, 2, HEAD_DIM, WINDOW), lambda i, pt, tp: (layer, i, 0, 0, 0)),
           const(tabs.shape), const(tabw.shape)],
        out_specs=[row(D_HEADS), pl.BlockSpec((None, 2, HEAD_DIM, WINDOW), lambda i, pt, tp: (i, 0, 0, 0))],
    )
    return pl.pallas_call(
        functools.partial(_nsa_attend_body, past=past),
        grid_spec=grid_spec,
        out_shape=[jax.ShapeDtypeStruct((b, 1, D_HEADS), F32),
                   jax.ShapeDtypeStruct((b, 2, HEAD_DIM, WINDOW), F32)],
        compiler_params=_cparams("parallel"),
        name="nsa_attend",
    )(page_table, top, *([cache_t] * SLC_TOPK), nq, small, nkv, o_c, swa_t, tabs, tabw)


SEQ_CHUNK = 64
GDN_CHUNKS = 1
SCAN_CHUNK = 256
ROW_TILE = 512
FFN_TILE = 512


def _lane_vec(pairs):
    v = jnp.zeros((1, LANES), F32)
    for off, vals in pairs:
        v = v.at[0, off:off + vals.shape[0]].set(vals.astype(F32))
    return v


def kernel(x_prompt, x_sample, cache_nsa_kv, cache_swa_kv, state_ret, state_mlstm_C, state_mlstm_n, state_mlstm_m, state_gdn, state_gdn_conv, state_ffn_conv, page_table, rel_bias, norm_pre_mix, w_in, mlstm_b_i, mlstm_b_f, gdn_conv_w, gdn_A_log, gdn_dt_bias, nsa_cmp_pe, nsa_cmp_w, ret_norm, mlstm_norm, gdn_norm, w_out, norm_post_mix, norm_pre_ffn, w_ffn_gate, w_ffn_up, ffn_conv_w, w_ffn_down, norm_post_ffn):
    depth = w_in.shape[0]
    bp, seq, _ = x_prompt.shape
    bs, dec_seq, _ = x_sample.shape
    assert dec_seq == 1 and seq % SLC_KC == 0 and seq >= WINDOW + Q_BLOCK
    n_pool = cache_nsa_kv.shape[1]
    past = page_table.shape[1] * PAGE_SIZE
    assert past >= WINDOW and past % SLC_BLOCK == 0 and page_table.shape[1] % PAGES_PER_STEP == 0

    p0, d_seq = _nsa_tab_seq_dists(seq)
    n16 = past // CMP_STRIDE
    n_slc = past // SLC_BLOCK + 1
    d_cmp = np.maximum(past - (np.arange(n16) * CMP_STRIDE + CMP_LEN - 1), 0)
    d_slc = np.maximum(past - np.arange(n_slc * SLC_BLOCK), 0)
    d_win = WINDOW - 1 - np.arange(WINDOW)
    tab = _bias_tables(rel_bias, np.concatenate([d_seq, d_cmp, d_slc, d_win]))
    o1, o2, o3 = len(d_seq), len(d_seq) + len(d_cmp), len(d_seq) + len(d_cmp) + len(d_slc)
    tab_seq = tab[:, :o1].reshape(SUBLANES, o1 // LANES, LANES)
    tabc = tab[:, o1:o2]
    tabs = jnp.transpose(tab[:, o2:o3].reshape(SUBLANES, n_slc, SLC_BLOCK), (1, 0, 2))
    tabs = jnp.concatenate([tabs, tabs], axis=2)
    tabw = tab[:, o3:]

    cache_t = jnp.transpose(cache_nsa_kv, (0, 1, 3, 4, 2))
    swa_t = jnp.transpose(cache_swa_kv, (0, 1, 3, 4, 2))
    cos_p, sin_p = _rope_tables(jnp.arange(seq))
    cos_s, sin_s = _rope_tables(jnp.arange(past, past + 1))

    xp = x_prompt
    xs = x_sample.reshape(bs, D_MODEL)
    p_states, s_states = [], []
    for l in range(depth):
        w_cat = _prep_w_in(w_in[l])
        w_out_bf = _bf(w_out[l])
        wg, wu, wd = _bf(w_ffn_gate[l]), _bf(w_ffn_up[l]), _bf(w_ffn_down[l])
        wab, pe = _prep_cmp_w(nsa_cmp_w[l], nsa_cmp_pe[l])
        ml_bias = _lane_vec([(SM_MI, mlstm_b_i[l]), (SM_MF, mlstm_b_f[l])])
        dtb = _lane_vec([(SM_GA, gdn_dt_bias[l])])
        alog = _lane_vec([(SM_GA, gdn_A_log[l])])

        z = _in_proj(xp.reshape(bp * seq, D_MODEL), norm_pre_mix[l], w_cat, ROW_TILE, seq=seq)
        z_ret, z_ml, z_gdn, z_nq, z_nkv, z_sm = [t.reshape(bp, seq, -1) for t in z[:-1]]
        nkv_t = z[-1].reshape(bp, 6, HEAD_DIM, seq)
        y_ret, st_ret = _retention_seq(z_ret, cos_p, sin_p, ret_norm[l], SCAN_CHUNK, bp)
        y_ml, st_c, st_n, st_m = _mlstm_seq(z_ml, z_sm, ml_bias, mlstm_norm[l], SCAN_CHUNK, bp)
        y_gdn, st_g = _gdn_seq(z_gdn, z_sm, gdn_conv_w[l], dtb, alog, gdn_norm[l], SEQ_CHUNK, bp, GDN_CHUNKS)
        rows = z_nkv[..., 0:2 * HEAD_DIM].reshape(bp, seq // CMP_STRIDE, CMP_STRIDE * 2 * HEAD_DIM)
        kvc = _compress_seq(rows, wab, pe)
        y_nsa = _nsa_seq(rel_bias, z_nq, z_sm, z_nkv, kvc, tab_seq, p0)
        xp, tail = _ffn_seq(xp, (y_ret, y_ml, y_gdn, y_nsa), w_out_bf, norm_post_mix[l], norm_pre_ffn[l],
                            wg, wu, wd, ffn_conv_w[l], norm_post_ffn[l], min(seq, FFN_TILE))
        p_states.append((
            jnp.transpose(nkv_t[:, 0:4], (0, 3, 1, 2)),
            jnp.transpose(nkv_t[:, 4:6, :, seq - WINDOW:], (0, 3, 1, 2)),
            st_ret, st_c, st_n, st_m[:, 0, :N_HEADS], st_g,
            z_gdn[:, seq - (GDN_CONV - 1):, 0:3 * D_HEADS],
            tail[:, SUBLANES - (FFN_CONV - 1):, :]))

        z = _in_proj(xs, norm_pre_mix[l], w_cat, bs)
        z_ret, z_ml, z_gdn, z_nq, z_nkv, z_sm = [t.reshape(bs, 1, -1) for t in z]
        m_pad = jnp.pad(state_mlstm_m[l], ((0, 0), (0, LANES - N_HEADS))).reshape(bs, 1, LANES)
        (y_ret, y_ml, y_gdn, st_ret, st_c, st_n, st_m, st_g, st_gc) = _mixers_step(
            z_ret, z_ml, z_gdn, z_sm, cos_s, sin_s, ml_bias, dtb, alog, gdn_conv_w[l],
            ret_norm[l], mlstm_norm[l], gdn_norm[l],
            state_ret[l], state_mlstm_C[l], state_mlstm_n[l], m_pad, state_gdn[l], state_gdn_conv[l])
        o_c, top = _nsa_select(l, page_table, cache_t, _bf(wab), pe, z_nq, tabc, past)
        y_nsa, swa_new = _nsa_attend(l, page_table, top[:, 0, :SLC_TOPK], cache_t, z_nq, z_sm, z_nkv, o_c,
                                     swa_t, tabs, tabw, past)
        ys = [t.reshape(bs, D_HEADS) for t in (y_ret, y_ml, y_gdn, y_nsa)]
        x1, h2 = _out_proj(xs, ys, w_out_bf, norm_post_mix[l], norm_pre_ffn[l], bs)
        xs, g_new = _ffn_step(x1, h2, state_ffn_conv[l][:, 0], state_ffn_conv[l][:, 1], wg, wu, wd,
                              ffn_conv_w[l], norm_post_ffn[l])
        s_states.append((
            z_nkv[..., 0:4 * HEAD_DIM].reshape(bs, 1, 4, HEAD_DIM),
            jnp.transpose(swa_new, (0, 3, 1, 2)),
            st_ret, st_c, st_n, st_m[:, 0, :N_HEADS], st_g, st_gc,
            jnp.concatenate([state_ffn_conv[l][:, 1:], g_new[:, None, :]], axis=1)))

    stack = lambda states: tuple(jnp.stack([s[k] for s in states]) for k in range(len(states[0])))
    return (xp, xs.reshape(bs, 1, D_MODEL)) + stack(p_states) + stack(s_states)
```

```python
import functools
import math

import jax
import jax.numpy as jnp
import numpy as np
from jax import lax
from jax.experimental import pallas as pl
from jax.experimental.pallas import tpu as pltpu

F32 = jnp.float32
BF16 = jnp.bfloat16
HI = lax.Precision.HIGHEST

D_MODEL = 1024
HEAD_DIM = 64
N_HEADS = 4
D_HEADS = N_HEADS * HEAD_DIM
D_FF = 2816
GDN_CONV = 4
FFN_CONV = 3
PAGE_SIZE = 128
CMP_STRIDE = 16
CMP_LEN = 32
SLC_BLOCK = 64
SLC_TOPK = 16
WINDOW = 512
Q_BLOCK = 128
T5_BUCKETS = 32
T5_MAX_DIST = 1024
ROPE_BASE = 10000.0
EPS = 1e-6
SCALE = HEAD_DIM ** -0.5
IN_WIDTHS = (256, 256, 256, 256, 256, 256, 256, 4, 4, 256, 768, 4, 4, 256, 256, 384, 12)
LANES = 128
SUBLANES = 8
VMEM_LIMIT = 56 * 1024 * 1024
NEG_INF = float("-inf")

SM_MI, SM_MF, SM_GA, SM_GB, SM_NG = 0, 4, 8, 12, 16


def _cparams(*sem):
    return pltpu.CompilerParams(dimension_semantics=sem, vmem_limit_bytes=VMEM_LIMIT)


def _rms(x, g):
    return x * lax.rsqrt(jnp.mean(x * x, axis=-1, keepdims=True) + EPS) * g


def _dot(a, b, **kw):
    return jnp.dot(a, b, preferred_element_type=F32, **kw)


def _dot_nt(a, b, **kw):
    return lax.dot_general(a, b, (((1,), (1,)), ((), ())), preferred_element_type=F32, **kw)


def _dot_tn(a, b, **kw):
    return lax.dot_general(a, b, (((0,), (0,)), ((), ())), preferred_element_type=F32, **kw)


def _bf(x):
    return x.astype(BF16)


def _sigmoid(x):
    return 1.0 / (1.0 + jnp.exp(-x))


def _silu(x):
    return x * _sigmoid(x)


def _softplus(x):
    return jnp.maximum(x, 0.0) + jnp.log(1.0 + jnp.exp(-jnp.abs(x)))


def _head_rms(o):
    return o * lax.rsqrt(jnp.mean(o * o, axis=-1, keepdims=True) + EPS)


PROJ_WIDTHS = (1024, 1024, 1024, 256, 384, 128)


def _prep_w_in(w):
    wt = w.T
    parts, off = [], 0
    for wd in IN_WIDTHS:
        parts.append(wt[off:off + wd])
        off += wd
    (rq, rk, rv, rg, mq, mk, mv, mi, mf, mo, gqkv, ga, gbeta, gg, nq, nkv, ngate) = parts
    small = jnp.concatenate([mi, mf, ga, gbeta, ngate], axis=0)
    small = jnp.pad(small, ((0, LANES - small.shape[0]), (0, 0)))
    cat = jnp.concatenate([rq, rk, rv, rg, mq, mk, mv, mo, gqkv, gg, nq, nkv, small], axis=0)
    return cat.astype(BF16)


NKV_OFF = sum(PROJ_WIDTHS[:4])


def _in_proj_body(x_ref, g_ref, wt_ref, *out_refs, seq):
    h = _bf(_rms(x_ref[...], g_ref[...]))
    off = 0
    for ref in out_refs[:len(PROJ_WIDTHS)]:
        n = ref.shape[-1]
        ref[...] = _dot_nt(h, wt_ref[off:off + n, :])
        off += n
    if seq is not None:
        kvt_ref, kx_ref, vs_ref, kvw_ref = out_refs[len(PROJ_WIDTHS):]
        kvt_ref[0] = _dot_nt(wt_ref[NKV_OFF:NKV_OFF + PROJ_WIDTHS[4], :], h)
        nkv = out_refs[4][...]
        tm = nkv.shape[0]
        nblk = kx_ref.shape[-1] - HEAD_DIM
        pos = (pl.program_id(0) * tm) % seq + lax.broadcasted_iota(jnp.int32, (tm, nblk), 0)
        block_id = jnp.where(pos // SLC_BLOCK == lax.broadcasted_iota(jnp.int32, (tm, nblk), 1), 1.0, 0.0)
        kx_ref[...] = _bf(jnp.concatenate([nkv[:, 2 * HEAD_DIM:3 * HEAD_DIM], block_id], axis=1))
        vs_ref[...] = _bf(nkv[:, 3 * HEAD_DIM:4 * HEAD_DIM])
        kvw_ref[...] = _bf(nkv[:, 4 * HEAD_DIM:6 * HEAD_DIM])


def _in_proj(x2d, g, w_cat_t, tm, seq=None):
    m = x2d.shape[0]
    ntot = sum(PROJ_WIDTHS)
    const = lambda shape: pl.BlockSpec(shape, lambda i: (0,) * len(shape), pipeline_mode=pl.Buffered(1))
    out_specs = [pl.BlockSpec((tm, n), lambda i: (i, 0)) for n in PROJ_WIDTHS]
    out_shape = [jax.ShapeDtypeStruct((m, n), F32) for n in PROJ_WIDTHS]
    if seq is not None:
        per_seq = seq // tm
        nkv = PROJ_WIDTHS[4]
        out_specs.append(pl.BlockSpec((1, nkv, tm), lambda i: (i // per_seq, 0, i % per_seq)))
        out_shape.append(jax.ShapeDtypeStruct((m // seq, nkv, seq), F32))
        for n in (HEAD_DIM + seq // SLC_BLOCK, HEAD_DIM, 2 * HEAD_DIM):
            out_specs.append(pl.BlockSpec((tm, n), lambda i: (i, 0)))
            out_shape.append(jax.ShapeDtypeStruct((m, n), BF16))
    return pl.pallas_call(
        functools.partial(_in_proj_body, seq=seq),
        grid=(m // tm,),
        in_specs=[pl.BlockSpec((tm, D_MODEL), lambda i: (i, 0)), const((1, D_MODEL)), const((ntot, D_MODEL))],
        out_specs=out_specs,
        out_shape=out_shape,
        compiler_params=_cparams("parallel"),
        name="in_proj",
    )(x2d, g.reshape(1, D_MODEL), w_cat_t)


def _out_proj_body(x_ref, y0, y1, y2, y3, w_ref, gpost_ref, gpre_ref, x1_ref, h2_ref):
    acc = None
    for k, y in enumerate((y0, y1, y2, y3)):
        t = _dot(_bf(y[...]), w_ref[k * D_HEADS:(k + 1) * D_HEADS, :])
        acc = t if acc is None else acc + t
    x1 = x_ref[...] + _rms(acc, gpost_ref[...])
    x1_ref[...] = x1
    h2_ref[...] = _bf(_rms(x1, gpre_ref[...]))


def _out_proj(x2d, ys, w_out_bf, gpost, gpre, tm):
    m = x2d.shape[0]
    row = lambda n: pl.BlockSpec((tm, n), lambda i: (i, 0))
    vec = pl.BlockSpec((1, D_MODEL), lambda i: (0, 0))
    return pl.pallas_call(
        _out_proj_body,
        grid=(m // tm,),
        in_specs=[row(D_MODEL)] + [row(D_HEADS)] * 4
        + [pl.BlockSpec((D_MODEL, D_MODEL), lambda i: (0, 0)), vec, vec],
        out_specs=[row(D_MODEL), row(D_MODEL)],
        out_shape=[jax.ShapeDtypeStruct((m, D_MODEL), F32), jax.ShapeDtypeStruct((m, D_MODEL), BF16)],
        compiler_params=_cparams("parallel"),
        name="out_proj",
    )(x2d, *ys, w_out_bf, gpost.reshape(1, -1), gpre.reshape(1, -1))


FFN_TN = 256


def _gelu_tanh(x):
    c = 0.7978845608028654
    hx = 0.5 * x
    return hx + hx * jnp.tanh(x * (c + (0.044715 * c) * (x * x)))


def _ffn_seq_body(x_ref, y0, y1, y2, y3, wo_ref, gmix_ref, gpre_ref, wg_ref, wu_ref, wd_ref, cw_ref, gpost_ref,
                  x2_ref, tail_ref, carry_ref, act_ref, *, tm):
    @pl.when(pl.program_id(1) == 0)
    def _():
        carry_ref[...] = jnp.zeros_like(carry_ref)

    mix = None
    for k, y in enumerate((y0, y1, y2, y3)):
        t = _dot(_bf(y[0]), wo_ref[k * D_HEADS:(k + 1) * D_HEADS, :])
        mix = t if mix is None else mix + t
    x1 = x_ref[0] + _rms(mix, gmix_ref[...])
    h2 = _bf(_rms(x1, gpre_ref[...]))
    row8 = lax.broadcasted_iota(jnp.int32, (SUBLANES, FFN_TN), 0)
    starts = list(range(0, D_FF, FFN_TN))
    up = lambda n0: (_dot(h2, wg_ref[:, n0:n0 + FFN_TN]), _dot(h2, wu_ref[:, n0:n0 + FFN_TN]))
    nxt = up(starts[0])
    for i, n0 in enumerate(starts):
        g, u = nxt
        if i + 1 < len(starts):
            nxt = up(starts[i + 1])
        prev = carry_ref[:, n0:n0 + FFN_TN]
        g1, g2 = pltpu.roll(g, 1, 0), pltpu.roll(g, 2, 0)
        top1 = jnp.where(row8 == 0, prev[7:8, :], g1[0:SUBLANES])
        top2 = jnp.where(row8 == 0, prev[6:7, :], jnp.where(row8 == 1, prev[7:8, :], g2[0:SUBLANES]))
        g1 = jnp.concatenate([top1, g1[SUBLANES:]], axis=0)
        g2 = jnp.concatenate([top2, g2[SUBLANES:]], axis=0)
        cw = cw_ref[:, n0:n0 + FFN_TN]
        gate = cw[0:1, :] * g2 + cw[1:2, :] * g1 + cw[2:3, :] * g
        a = _gelu_tanh(gate) * u
        act_ref[:, n0:n0 + FFN_TN] = _bf(a)
        carry_ref[:, n0:n0 + FFN_TN] = g[tm - SUBLANES:tm, :]
    acc = _dot(act_ref[...], wd_ref[...])
    x2_ref[0] = x1 + _rms(acc, gpost_ref[...])
    tail_ref[0] = carry_ref[...]


def _ffn_seq(x, ys, w_out_bf, gmix, gpre, wg, wu, wd, cw, gpost, tm):
    b, l, _ = x.shape
    const = lambda shape: pl.BlockSpec(shape, lambda i, j: (0,) * len(shape), pipeline_mode=pl.Buffered(1))
    row = lambda n: pl.BlockSpec((1, tm, n), lambda i, j: (i, j, 0))
    return pl.pallas_call(
        functools.partial(_ffn_seq_body, tm=tm),
        grid=(b, l // tm),
        in_specs=[row(D_MODEL)] + [row(D_HEADS)] * 4
        + [const((D_MODEL, D_MODEL)), const((1, D_MODEL)), const((1, D_MODEL)),
           const((D_MODEL, D_FF)), const((D_MODEL, D_FF)), const((D_FF, D_MODEL)),
           const((FFN_CONV, D_FF)), const((1, D_MODEL))],
        out_specs=[pl.BlockSpec((1, tm, D_MODEL), lambda i, j: (i, j, 0)),
                   pl.BlockSpec((1, SUBLANES, D_FF), lambda i, j: (i, 0, 0))],
        out_shape=[jax.ShapeDtypeStruct((b, l, D_MODEL), F32),
                   jax.ShapeDtypeStruct((b, SUBLANES, D_FF), F32)],
        scratch_shapes=[pltpu.VMEM((SUBLANES, D_FF), F32), pltpu.VMEM((tm, D_FF), BF16)],
        compiler_params=_cparams("parallel", "arbitrary"),
        name="ffn_seq",
    )(x, *ys, w_out_bf, gmix.reshape(1, -1), gpre.reshape(1, -1), wg, wu, wd, cw, gpost.reshape(1, -1))


def _ffn_step_body(x1_ref, h2_ref, b0_ref, b1_ref, wg_ref, wu_ref, wd_ref, cw_ref, gpost_ref, x2_ref, g_ref):
    h2 = h2_ref[...]
    acc = jnp.zeros(x1_ref.shape, F32)
    for n0 in range(0, D_FF, FFN_TN):
        sl = slice(n0, n0 + FFN_TN)
        g = _dot(h2, wg_ref[:, sl])
        cw = cw_ref[:, sl]
        gate = cw[0:1, :] * b0_ref[:, sl] + cw[1:2, :] * b1_ref[:, sl] + cw[2:3, :] * g
        u = _dot(h2, wu_ref[:, sl])
        acc = acc + _dot(_bf(_gelu_tanh(gate) * u), wd_ref[sl, :])
        g_ref[:, sl] = g
    x2_ref[...] = x1_ref[...] + _rms(acc, gpost_ref[...])


def _ffn_step(x1, h2, b0, b1, wg, wu, wd, cw, gpost):
    m = x1.shape[0]
    return pl.pallas_call(
        _ffn_step_body,
        out_shape=[jax.ShapeDtypeStruct((m, D_MODEL), F32), jax.ShapeDtypeStruct((m, D_FF), F32)],
        compiler_params=pltpu.CompilerParams(vmem_limit_bytes=VMEM_LIMIT),
        name="ffn_step",
    )(x1, h2, b0, b1, wg, wu, wd, cw, gpost.reshape(1, -1))


def _rope_tables(pos):
    half = HEAD_DIM // 2
    inv = ROPE_BASE ** (-jnp.linspace(0.0, 1.0, half, dtype=F32))
    ang = pos.astype(F32)[:, None] * inv[None, :]
    return jnp.tile(jnp.cos(ang), (1, LANES // half)), jnp.tile(jnp.sin(ang), (1, LANES // half))


def _rope128(x, cos, sin):
    lane = lax.broadcasted_iota(jnp.int32, x.shape, 1)
    first = (lane % HEAD_DIM) < (HEAD_DIM // 2)
    other = jnp.where(first, -pltpu.roll(x, LANES - HEAD_DIM // 2, 1), pltpu.roll(x, HEAD_DIM // 2, 1))
    return x * cos + other * sin


def _rope256(x, cos, sin):
    return jnp.concatenate([_rope128(x[:, :LANES], cos, sin), _rope128(x[:, LANES:], cos, sin)], axis=1)


def _ret_log_decay(h):
    return math.log(1.0 - 2.0 ** (-5.0 - h))


def _head(x, h):
    return x[:, h * HEAD_DIM:(h + 1) * HEAD_DIM]


def _ret_body(z_ref, cos_ref, sin_ref, nw_ref, y_ref, s_ref, *, chunk):
    @pl.when(pl.program_id(1) == 0)
    def _():
        s_ref[...] = jnp.zeros_like(s_ref)

    nb = z_ref.shape[0]
    cos, sin = cos_ref[...], sin_ref[...]
    t = lax.broadcasted_iota(jnp.int32, (chunk, chunk), 0)
    s = lax.broadcasted_iota(jnp.int32, (chunk, chunk), 1)
    causal = t >= s
    diff = jnp.where(causal, t - s, 0).astype(F32)
    tcol = lax.broadcasted_iota(jnp.int32, (chunk, 1), 0).astype(F32)
    lgs = [_ret_log_decay(h) for h in range(N_HEADS)]
    dmat = [jnp.where(causal, jnp.exp(lg * diff), 0.0) for lg in lgs]
    xi = [jnp.exp(lg * (tcol + 1.0)) for lg in lgs]
    zeta = [jnp.exp(lg * (chunk - 1.0 - tcol)) for lg in lgs]
    probs = [(bb, h) for bb in range(nb) for h in range(N_HEADS)]
    qkv = []
    for bb in range(nb):
        q = _rope256(z_ref[bb, :, 0:256], cos, sin)
        k = _rope256(z_ref[bb, :, 256:512], cos, sin) * SCALE
        qkv.append((q, k, z_ref[bb, :, 512:768]))
    qh = [_head(qkv[bb][0], h) for bb, h in probs]
    kh = [_head(qkv[bb][1], h) for bb, h in probs]
    vh = [_bf(_head(qkv[bb][2], h)) for bb, h in probs]
    st = [s_ref[bb, h] for bb, h in probs]
    sc = [_dot_nt(_bf(q), _bf(k)) for q, k in zip(qh, kh)]
    cross = [_dot(_bf(q * xi[h]), _bf(s0)) for q, s0, (_, h) in zip(qh, st, probs)]
    upd = [_dot_tn(_bf(k * zeta[h]), v) for k, v, (_, h) in zip(kh, vh, probs)]
    o = [_dot(_bf(x * dmat[h]), v) + c for x, v, c, (_, h) in zip(sc, vh, cross, probs)]
    for i, (bb, h) in enumerate(probs):
        s_ref[bb, h] = st[i] * math.exp(lgs[h] * chunk) + upd[i]
    for bb in range(nb):
        outs = [_head_rms(o[bb * N_HEADS + h]) for h in range(N_HEADS)]
        y_ref[bb] = jnp.concatenate(outs, axis=1) * nw_ref[...] * _silu(z_ref[bb, :, 768:1024])


def _retention_seq(z_ret, cos, sin, nw, chunk, nb):
    b, l, _ = z_ret.shape
    return pl.pallas_call(
        functools.partial(_ret_body, chunk=chunk),
        grid=(b // nb, l // chunk),
        in_specs=[pl.BlockSpec((nb, chunk, 1024), lambda i, j: (i, j, 0)),
                  pl.BlockSpec((chunk, LANES), lambda i, j: (j, 0)),
                  pl.BlockSpec((chunk, LANES), lambda i, j: (j, 0)),
                  pl.BlockSpec((1, D_HEADS), lambda i, j: (0, 0))],
        out_specs=[pl.BlockSpec((nb, chunk, D_HEADS), lambda i, j: (i, j, 0)),
                   pl.BlockSpec((nb, N_HEADS, HEAD_DIM, HEAD_DIM), lambda i, j: (i, 0, 0, 0))],
        out_shape=[jax.ShapeDtypeStruct((b, l, D_HEADS), F32),
                   jax.ShapeDtypeStruct((b, N_HEADS, HEAD_DIM, HEAD_DIM), F32)],
        compiler_params=_cparams("parallel", "arbitrary"),
        name="retention_seq",
    )(z_ret, cos, sin, nw.reshape(1, -1))


def _lanes_to_rows(x, n=2 * SUBLANES):
    sel = (lax.broadcasted_iota(jnp.int32, (n, x.shape[1]), 0)
           == lax.broadcasted_iota(jnp.int32, (n, x.shape[1]), 1)).astype(F32)
    return _dot_nt(sel, x, precision=HI)


def _tri_incl(n):
    t = lax.broadcasted_iota(jnp.int32, (n, n), 0)
    s = lax.broadcasted_iota(jnp.int32, (n, n), 1)
    return t >= s


def _ml_body(z_ref, sm_ref, bias_ref, nw_ref, y_ref, c_ref, n_ref, m_ref, *, chunk):
    @pl.when(pl.program_id(1) == 0)
    def _():
        c_ref[...] = jnp.zeros_like(c_ref)
        n_ref[...] = jnp.zeros_like(n_ref)
        m_ref[...] = jnp.zeros_like(m_ref)

    nb = z_ref.shape[0]
    incl = _tri_incl(chunk)
    tri = incl.astype(F32)
    probs = [(bb, h) for bb in range(nb) for h in range(N_HEADS)]
    gate = []
    for bb in range(nb):
        pre = sm_ref[bb] + bias_ref[...]
        logf = jnp.minimum(pre, 0.0) - jnp.log(1.0 + jnp.exp(-jnp.abs(pre)))
        bcum = _dot(tri, logf, precision=HI)
        gate.append((pre, bcum, _lanes_to_rows(pre), _lanes_to_rows(bcum), m_ref[bb]))
    qh = [_head(z_ref[bb, :, 0:256], h) for bb, h in probs]
    kh = [_head(z_ref[bb, :, 256:512], h) * SCALE for bb, h in probs]
    vh = [_bf(_head(z_ref[bb, :, 512:768], h)) for bb, h in probs]
    cm = [c_ref[bb, h] for bb, h in probs]
    nv = [n_ref[bb, h:h + 1, :] for bb, h in probs]
    qk_raw = [_dot_nt(_bf(q), _bf(k)) for q, k in zip(qh, kh)]
    q_c = [_dot(_bf(q), _bf(c)) for q, c in zip(qh, cm)]
    i_col, b_col, inter, dlog = [], [], [], []
    for bb, h in probs:
        pre, bcum, pre_t, bcum_t, m_all = gate[bb]
        i_col.append(pre[:, SM_MI + h:SM_MI + h + 1])
        b_col.append(bcum[:, SM_MF + h:SM_MF + h + 1])
        inter.append(b_col[-1] + m_all[:, h:h + 1])
        dlog.append(jnp.where(incl, b_col[-1] - bcum_t[SM_MF + h:SM_MF + h + 1, :]
                              + pre_t[SM_MI + h:SM_MI + h + 1, :], NEG_INF))
    dmax = [jnp.max(x, axis=1, keepdims=True) for x in dlog]
    q_n = [jnp.sum(q * n, axis=1, keepdims=True) for q, n in zip(qh, nv)]
    stab = []
    for i in range(len(probs)):
        m_t = jnp.maximum(inter[i], dmax[i])
        m_end = m_t[chunk - 1:chunk, :]
        stab.append(dict(m_t=m_t, w=jnp.exp(dlog[i] - m_t), g_in=jnp.exp(inter[i] - m_t), m_end=m_end,
                         w_end=jnp.exp(b_col[i][chunk - 1:chunk, :] - b_col[i] + i_col[i] - m_end),
                         g_end=jnp.exp(inter[i][chunk - 1:chunk, :] - m_end)))
    qk = [x * d["w"] for x, d in zip(qk_raw, stab)]
    num = [_dot(_bf(x), v) + d["g_in"] * c for x, v, d, c in zip(qk, vh, stab, q_c)]
    upd = [_dot_tn(_bf(k * d["w_end"]), v) for k, d, v in zip(kh, stab, vh)]
    qk_sum = [jnp.sum(x, axis=1, keepdims=True) for x in qk]
    k_sum = [jnp.sum(d["w_end"] * k, axis=0, keepdims=True) for d, k in zip(stab, kh)]
    hh = []
    for i, (bb, h) in enumerate(probs):
        d = stab[i]
        den = qk_sum[i] + d["g_in"] * q_n[i]
        hh.append(num[i] / jnp.maximum(jnp.abs(den), jnp.exp(-d["m_t"])))
        c_ref[bb, h] = d["g_end"] * cm[i] + upd[i]
        n_ref[bb, h:h + 1, :] = d["g_end"] * nv[i] + k_sum[i]
    lane = lax.broadcasted_iota(jnp.int32, (1, LANES), 1)
    for bb in range(nb):
        m_out = jnp.zeros((1, LANES), F32)
        outs = []
        for h in range(N_HEADS):
            i = bb * N_HEADS + h
            m_out = jnp.where(lane == h, stab[i]["m_end"], m_out)
            outs.append(_head_rms(_sigmoid(_head(z_ref[bb, :, 768:1024], h)) * hh[i]))
        m_ref[bb] = m_out
        y_ref[bb] = jnp.concatenate(outs, axis=1) * nw_ref[...]


def _mlstm_seq(z_ml, small, bias_vec, nw, chunk, nb):
    b, l, _ = z_ml.shape
    return pl.pallas_call(
        functools.partial(_ml_body, chunk=chunk),
        grid=(b // nb, l // chunk),
        in_specs=[pl.BlockSpec((nb, chunk, 1024), lambda i, j: (i, j, 0)),
                  pl.BlockSpec((nb, chunk, LANES), lambda i, j: (i, j, 0)),
                  pl.BlockSpec((1, LANES), lambda i, j: (0, 0)),
                  pl.BlockSpec((1, D_HEADS), lambda i, j: (0, 0))],
        out_specs=[pl.BlockSpec((nb, chunk, D_HEADS), lambda i, j: (i, j, 0)),
                   pl.BlockSpec((nb, N_HEADS, HEAD_DIM, HEAD_DIM), lambda i, j: (i, 0, 0, 0)),
                   pl.BlockSpec((nb, N_HEADS, HEAD_DIM), lambda i, j: (i, 0, 0)),
                   pl.BlockSpec((nb, 1, LANES), lambda i, j: (i, 0, 0))],
        out_shape=[jax.ShapeDtypeStruct((b, l, D_HEADS), F32),
                   jax.ShapeDtypeStruct((b, N_HEADS, HEAD_DIM, HEAD_DIM), F32),
                   jax.ShapeDtypeStruct((b, N_HEADS, HEAD_DIM), F32),
                   jax.ShapeDtypeStruct((b, 1, LANES), F32)],
        compiler_params=_cparams("parallel", "arbitrary"),
        name="mlstm_seq",
    )(z_ml, small, bias_vec, nw.reshape(1, -1))


def _split_bf16(x):
    hi = _bf(x)
    return hi, _bf(x - hi.astype(F32))


def _dot3(a, b):
    (ah, al), (bh, bl) = a, b
    return _dot(ah, bh) + (_dot(ah, bl) + _dot(al, bh))


def _l2norm(x):
    return x * lax.rsqrt(jnp.sum(x * x, axis=-1, keepdims=True) + EPS)


def _gdn_body(z_ref, sm_ref, cw_ref, dtb_ref, alog_ref, nw_ref, y_ref, s_ref, buf_ref, *, chunk):
    @pl.when(pl.program_id(1) == 0)
    def _():
        s_ref[...] = jnp.zeros_like(s_ref)
        buf_ref[:, 0:SUBLANES, :] = jnp.zeros((buf_ref.shape[0], SUBLANES, 3 * D_HEADS), F32)

    nb = z_ref.shape[0]
    rows = z_ref.shape[1]
    nc = rows // chunk
    incl = _tri_incl(chunk)
    strict = lax.broadcasted_iota(jnp.int32, (chunk, chunk), 0) > lax.broadcasted_iota(jnp.int32, (chunk, chunk), 1)
    eye = (lax.broadcasted_iota(jnp.int32, (chunk, chunk), 0)
           == lax.broadcasted_iota(jnp.int32, (chunk, chunk), 1)).astype(F32)
    tri = incl.astype(F32)
    gates, acts = {}, []
    for bb in range(nb):
        buf_ref[bb, SUBLANES:SUBLANES + rows, :] = z_ref[bb, :, 0:768]
        conv = None
        for j in range(GDN_CONV):
            term = buf_ref[bb, pl.ds(SUBLANES - (GDN_CONV - 1) + j, rows), :] * cw_ref[j:j + 1, :]
            conv = term if conv is None else conv + term
        buf_ref[bb, 0:SUBLANES, :] = buf_ref[bb, rows:rows + SUBLANES, :]
        acts.append(_silu(conv))
        sm = sm_ref[bb]
        g_all = -jnp.exp(alog_ref[...]) * _softplus(sm + dtb_ref[...])
        beta_all = _sigmoid(sm)
        for cc in range(nc):
            rs = slice(cc * chunk, (cc + 1) * chunk)
            gcum = _dot(tri, g_all[rs], precision=HI)
            gates[bb, cc] = (gcum, _lanes_to_rows(gcum), beta_all[rs])

    probs = [(bb, cc, h) for cc in range(nc) for bb in range(nb) for h in range(N_HEADS)]
    pre = []
    for bb, cc, h in probs:
        rs = slice(cc * chunk, (cc + 1) * chunk)
        gcum, gcum_t, beta_all = gates[bb, cc]
        g_col = gcum[:, SM_GA + h:SM_GA + h + 1]
        g_row = gcum_t[SM_GA + h:SM_GA + h + 1, :]
        beta = beta_all[:, SM_GB + h:SM_GB + h + 1]
        decay = jnp.where(incl, jnp.exp(jnp.where(incl, g_col - g_row, 0.0)), 0.0)
        act = acts[bb]
        qh = _l2norm(_head(act[rs, 0:256], h)) * SCALE
        kh = _l2norm(_head(act[rs, 256:512], h))
        vh = _head(act[rs, 512:768], h)
        eg = jnp.exp(g_col)
        pre.append(dict(g_col=g_col, beta=beta, decay=decay, qh=qh, kh=kh, kb=_bf(kh), eg=eg,
                        rhs=jnp.concatenate([vh * beta, kh * (beta * eg)], axis=1)))
    a_l = [jnp.where(strict, d["beta"] * _dot_nt(d["kb"], d["kb"]) * d["decay"], 0.0) for d in pre]
    p_l = [eye - a for a in a_l]
    pw_s = [_split_bf16(a) for a in a_l]
    lvl = 2
    while lvl < chunk:
        pw_s = [_split_bf16(_dot3(x, x)) for x in pw_s]
        p_l = [p + _dot3(_split_bf16(p), x) for p, x in zip(p_l, pw_s)]
        lvl *= 2
    uw_l = [_dot3(_split_bf16(p), _split_bf16(d["rhs"])) for p, d in zip(p_l, pre)]
    qk_l = [_dot_nt(_bf(d["qh"]), d["kb"]) * d["decay"] for d in pre]
    per_chunk = nb * N_HEADS
    st_l = [s_ref[bb, h] for bb, cc, h in probs[:per_chunk]]
    o_all = []
    for cc in range(nc):
        sl = slice(cc * per_chunk, (cc + 1) * per_chunk)
        stb_l = [_bf(st) for st in st_l]
        delta_l = [uw[:, 0:HEAD_DIM] - _dot(_bf(uw[:, HEAD_DIM:]), stb) for uw, stb in zip(uw_l[sl], stb_l)]
        o_all += [_dot(_bf(qk), _bf(delta)) + _dot(_bf(d["qh"] * d["eg"]), stb)
                  for qk, delta, d, stb in zip(qk_l[sl], delta_l, pre[sl], stb_l)]
        new_st = []
        for d, st, delta in zip(pre[sl], st_l, delta_l):
            g_end = d["g_col"][chunk - 1:chunk, :]
            w_end = jnp.exp(g_end - d["g_col"])
            new_st.append(jnp.exp(g_end) * st + _dot_tn(_bf(d["kh"] * w_end), _bf(delta)))
        st_l = new_st
    for (bb, cc, h), st in zip(probs[:per_chunk], st_l):
        s_ref[bb, h] = st
    for bb in range(nb):
        for cc in range(nc):
            outs = [_head_rms(o_all[cc * per_chunk + bb * N_HEADS + h]) for h in range(N_HEADS)]
            rs = slice(cc * chunk, (cc + 1) * chunk)
            y_ref[bb, rs, :] = jnp.concatenate(outs, axis=1) * nw_ref[...] * _silu(z_ref[bb, rs, 768:1024])


def _gdn_seq(z_gdn, small, conv_w, dtb_vec, alog_vec, nw, chunk, nb, nc):
    b, l, _ = z_gdn.shape
    rows = nc * chunk
    return pl.pallas_call(
        functools.partial(_gdn_body, chunk=chunk),
        grid=(b // nb, l // rows),
        in_specs=[pl.BlockSpec((nb, rows, 1024), lambda i, j: (i, j, 0)),
                  pl.BlockSpec((nb, rows, LANES), lambda i, j: (i, j, 0)),
                  pl.BlockSpec((GDN_CONV, 3 * D_HEADS), lambda i, j: (0, 0)),
                  pl.BlockSpec((1, LANES), lambda i, j: (0, 0)),
                  pl.BlockSpec((1, LANES), lambda i, j: (0, 0)),
                  pl.BlockSpec((1, D_HEADS), lambda i, j: (0, 0))],
        out_specs=[pl.BlockSpec((nb, rows, D_HEADS), lambda i, j: (i, j, 0)),
                   pl.BlockSpec((nb, N_HEADS, HEAD_DIM, HEAD_DIM), lambda i, j: (i, 0, 0, 0))],
        out_shape=[jax.ShapeDtypeStruct((b, l, D_HEADS), F32),
                   jax.ShapeDtypeStruct((b, N_HEADS, HEAD_DIM, HEAD_DIM), F32)],
        scratch_shapes=[pltpu.VMEM((nb, SUBLANES + rows, 3 * D_HEADS), F32)],
        compiler_params=_cparams("parallel", "arbitrary"),
        name="gdn_seq",
    )(z_gdn, small, conv_w, dtb_vec, alog_vec, nw.reshape(1, -1))


STEP_SEQS = 4


def _row_to_col(row):
    n = row.shape[1]
    eye = lax.broadcasted_iota(jnp.int32, (n, n), 0) == lax.broadcasted_iota(jnp.int32, (n, n), 1)
    return jnp.sum(jnp.where(eye, jnp.broadcast_to(row, (n, n)), 0.0), axis=1, keepdims=True)


def _vec_mat(col, mat):
    return jnp.sum(col * mat, axis=0, keepdims=True)


def _step_body(zr_ref, zm_ref, zg_ref, sm_ref, cos_ref, sin_ref, mlb_ref, dtb_ref, alog_ref, cw_ref,
               nwr_ref, nwm_ref, nwg_ref, sr_ref, mc_ref, mn_ref, mm_ref, gs_ref, gc_ref,
               yr_ref, ym_ref, yg_ref, sr_o, mc_o, mn_o, mm_o, gs_o, gc_o):
    nb = zr_ref.shape[0]
    d = HEAD_DIM
    eye = lax.broadcasted_iota(jnp.int32, (d, d), 0) == lax.broadcasted_iota(jnp.int32, (d, d), 1)
    lane = lax.broadcasted_iota(jnp.int32, (1, LANES), 1)
    cos, sin = cos_ref[...], sin_ref[...]
    rowsum = lambda x: jnp.sum(x, axis=1, keepdims=True)

    def vec_mat(row, mat):
        return _dot(_pad_rows(row), mat, precision=HI)[0:1, :]

    def outer(a_row, b_row):
        diag = jnp.where(eye, jnp.broadcast_to(a_row, (d, d)), 0.0)
        return _dot(diag, jnp.broadcast_to(b_row, (d, d)), precision=HI)

    ret, ml, gdn, seqs = [], [], [], []
    for bb in range(nb):
        sm, zr, zm, zg = sm_ref[bb], zr_ref[bb], zm_ref[bb], zg_ref[bb]
        rq = _rope256(zr[:, 0:256], cos, sin)
        rk = _rope256(zr[:, 256:512], cos, sin) * SCALE
        pre = sm + mlb_ref[...]
        logf = jnp.minimum(pre, 0.0) - jnp.log(1.0 + jnp.exp(-jnp.abs(pre)))
        m_all = mm_ref[bb]
        x = zg[:, 0:768]
        buf = gc_ref[bb]
        conv = x * cw_ref[GDN_CONV - 1:GDN_CONV, :]
        for j in range(GDN_CONV - 1):
            conv = conv + buf[j:j + 1, :] * cw_ref[j:j + 1, :]
        act = _silu(conv)
        g_all = -jnp.exp(alog_ref[...]) * _softplus(sm + dtb_ref[...])
        beta_all = _sigmoid(sm)
        seqs.append(dict(zr=zr, zm=zm, zg=zg, x=x, buf=buf))
        for h in range(N_HEADS):
            ret.append(dict(q=_head(rq, h), k=_head(rk, h), v=_head(zr[:, 512:768], h), st=sr_ref[bb, h],
                            gam=math.exp(_ret_log_decay(h))))
            i_g = pre[:, SM_MI + h:SM_MI + h + 1]
            inter = logf[:, SM_MF + h:SM_MF + h + 1] + m_all[:, h:h + 1]
            m_t = jnp.maximum(inter, i_g)
            ml.append(dict(q=_head(zm[:, 0:256], h), k=_head(zm[:, 256:512], h) * SCALE, v=_head(zm[:, 512:768], h),
                           cm=mc_ref[bb, h], nv=mn_ref[bb, h:h + 1, :], m_t=m_t, w=jnp.exp(i_g - m_t),
                           g_in=jnp.exp(inter - m_t)))
            gdn.append(dict(q=_head(act[:, 0:256], h), k=_head(act[:, 256:512], h), v=_head(act[:, 512:768], h),
                            st=gs_ref[bb, h], eg=jnp.exp(g_all[:, SM_GA + h:SM_GA + h + 1]),
                            beta=beta_all[:, SM_GB + h:SM_GB + h + 1]))
    for p in gdn:
        p["q"], p["k"] = _l2norm(p["q"]) * SCALE, _l2norm(p["k"])
    for p in ret + ml + gdn:
        p["qk"] = rowsum(p["q"] * p["k"])
    for p in ml:
        p["qn"] = rowsum(p["q"] * p["nv"])
    for p in ret:
        p["qs"] = vec_mat(p["q"] * p["gam"], p["st"])
    for p in ml:
        p["qc"] = vec_mat(p["q"], p["cm"])
    for p in gdn:
        p["ks"] = vec_mat(p["k"] * (p["beta"] * p["eg"]), p["st"])
        p["qs"] = vec_mat(p["q"] * p["eg"], p["st"])
    for p in ret:
        p["o"] = p["qk"] * p["v"] + p["qs"]
    for p in ml:
        qk = p["qk"] * p["w"]
        den = qk + p["g_in"] * p["qn"]
        p["o"] = (qk * p["v"] + p["g_in"] * p["qc"]) / jnp.maximum(jnp.abs(den), jnp.exp(-p["m_t"]))
    for p in gdn:
        p["delta"] = p["v"] * p["beta"] - p["ks"]
        p["o"] = p["qk"] * p["delta"] + p["qs"]
    for p in ret:
        p["new"] = p["st"] * p["gam"] + outer(p["k"], p["v"])
    for p in ml:
        p["new"] = p["g_in"] * p["cm"] + outer(p["k"] * p["w"], p["v"])
    for p in gdn:
        p["new"] = p["eg"] * p["st"] + outer(p["k"], p["delta"])
    for bb in range(nb):
        s = seqs[bb]
        ps = slice(bb * N_HEADS, (bb + 1) * N_HEADS)
        m_out = jnp.zeros((1, LANES), F32)
        for h, (pr, pm, pg) in enumerate(zip(ret[ps], ml[ps], gdn[ps])):
            sr_o[bb, h] = pr["new"]
            mc_o[bb, h] = pm["new"]
            mn_o[bb, h:h + 1, :] = pm["g_in"] * pm["nv"] + pm["w"] * pm["k"]
            gs_o[bb, h] = pg["new"]
            m_out = jnp.where(lane == h, pm["m_t"], m_out)
        mm_o[bb] = m_out
        gc_o[bb, 0:GDN_CONV - 2, :] = s["buf"][1:GDN_CONV - 1, :]
        gc_o[bb, GDN_CONV - 2:GDN_CONV - 1, :] = s["x"]
        yr_ref[bb] = (jnp.concatenate([_head_rms(p["o"]) for p in ret[ps]], axis=1) * nwr_ref[...]
                      * _silu(s["zr"][:, 768:1024]))
        og = s["zm"][:, 768:1024]
        ym_ref[bb] = jnp.concatenate([_head_rms(_sigmoid(_head(og, h)) * p["o"])
                                      for h, p in enumerate(ml[ps])], axis=1) * nwm_ref[...]
        yg_ref[bb] = (jnp.concatenate([_head_rms(p["o"]) for p in gdn[ps]], axis=1) * nwg_ref[...]
                      * _silu(s["zg"][:, 768:1024]))


def _mixers_step(z_ret, z_ml, z_gdn, small, cos, sin, mlb, dtb, alog, conv_w, nwr, nwm, nwg,
                 s_ret, ml_c, ml_n, ml_m, gdn_s, gdn_conv):
    b = z_ret.shape[0]
    nb = STEP_SEQS if b % STEP_SEQS == 0 else 1
    row = lambda n: pl.BlockSpec((nb, 1, n), lambda i: (i, 0, 0))
    vec = lambda n: pl.BlockSpec((1, n), lambda i: (0, 0))
    mat = pl.BlockSpec((nb, N_HEADS, HEAD_DIM, HEAD_DIM), lambda i: (i, 0, 0, 0))
    nsp = pl.BlockSpec((nb, N_HEADS, HEAD_DIM), lambda i: (i, 0, 0))
    csp = pl.BlockSpec((nb, GDN_CONV - 1, 3 * D_HEADS), lambda i: (i, 0, 0))
    mshape = jax.ShapeDtypeStruct((b, N_HEADS, HEAD_DIM, HEAD_DIM), F32)
    yshape = jax.ShapeDtypeStruct((b, 1, D_HEADS), F32)
    return pl.pallas_call(
        _step_body,
        grid=(b // nb,),
        in_specs=[row(1024), row(1024), row(1024), row(LANES), vec(LANES), vec(LANES), vec(LANES), vec(LANES), vec(LANES),
                  pl.BlockSpec((GDN_CONV, 3 * D_HEADS), lambda i: (0, 0)), vec(D_HEADS), vec(D_HEADS), vec(D_HEADS),
                  mat, mat, nsp, row(LANES), mat, csp],
        out_specs=[row(D_HEADS), row(D_HEADS), row(D_HEADS), mat, mat, nsp, row(LANES), mat, csp],
        out_shape=[yshape, yshape, yshape, mshape, mshape,
                   jax.ShapeDtypeStruct((b, N_HEADS, HEAD_DIM), F32), jax.ShapeDtypeStruct((b, 1, LANES), F32),
                   mshape, jax.ShapeDtypeStruct((b, GDN_CONV - 1, 3 * D_HEADS), F32)],
        compiler_params=_cparams("parallel"),
        name="mixers_step",
    )(z_ret, z_ml, z_gdn, small, cos, sin, mlb, dtb, alog, conv_w,
      nwr.reshape(1, -1), nwm.reshape(1, -1), nwg.reshape(1, -1), s_ret, ml_c, ml_n, ml_m, gdn_s, gdn_conv)


def _t5_bucket_np(n):
    exact = T5_BUCKETS // 2
    n = np.maximum(np.asarray(n, np.int64), 0)
    x = np.maximum(n, 1).astype(np.float32) / np.float32(exact)
    large = exact + (np.log(x) / np.float32(math.log(T5_MAX_DIST / exact)) * np.float32(T5_BUCKETS - exact)).astype(np.int32)
    return np.where(n < exact, n, np.minimum(large, T5_BUCKETS - 1)).astype(np.int32)


def _bucket_thresholds():
    b = _t5_bucket_np(np.arange(4 * T5_MAX_DIST))
    return [int(np.argmax(b >= k)) for k in range(T5_BUCKETS // 2 + 1, T5_BUCKETS)]


def _bias_table_body(rbt_ref, oh_ref, o_ref):
    o_ref[...] = _dot(rbt_ref[...], oh_ref[...].astype(F32), precision=HI)


def _bias_tables(rel_bias, dists):
    n = len(dists)
    tn = 4096
    npad = -(-n // tn) * tn
    onehot = np.zeros((T5_BUCKETS, npad), np.float32)
    onehot[_t5_bucket_np(dists), np.arange(n)] = 1.0
    rbt = jnp.pad(rel_bias.T, ((0, SUBLANES - N_HEADS), (0, 0)))
    out = pl.pallas_call(
        _bias_table_body,
        grid=(npad // tn,),
        in_specs=[pl.BlockSpec((SUBLANES, T5_BUCKETS), lambda i: (0, 0)),
                  pl.BlockSpec((T5_BUCKETS, tn), lambda i: (0, i))],
        out_specs=pl.BlockSpec((SUBLANES, tn), lambda i: (0, i)),
        out_shape=jax.ShapeDtypeStruct((SUBLANES, npad), F32),
        name="bias_tables",
    )(rbt, jnp.asarray(onehot, BF16))
    return out[:, :n]


def _prep_cmp_w(cmp_w, cmp_pe):
    eye2 = jnp.eye(2, dtype=F32)

    def half(w):
        full = jnp.einsum('cjde,cf->jcdfe', w, eye2)
        return full.reshape(CMP_STRIDE * 2 * HEAD_DIM, 2 * HEAD_DIM)

    wab = jnp.concatenate([half(cmp_w[:, :CMP_STRIDE]), half(cmp_w[:, CMP_STRIDE:])], axis=1)
    pe = jnp.concatenate([cmp_pe[:CMP_STRIDE].reshape(1, -1), cmp_pe[CMP_STRIDE:].reshape(1, -1)], axis=1)
    return wab, jnp.pad(pe, ((0, SUBLANES - 1), (0, 0)))


def _pe_term(pe_ref, wab_ref):
    kin = CMP_STRIDE * 2 * HEAD_DIM
    t = (_dot(pe_ref[:, 0:kin], wab_ref[:, 0:LANES].astype(F32), precision=HI)
         + _dot(pe_ref[:, kin:2 * kin], wab_ref[:, LANES:2 * LANES].astype(F32), precision=HI))
    return t[0:1, :]


def _combine_cmp(parts, pe_term):
    rows = parts.shape[0]
    nxt = pltpu.roll(parts[:, LANES:2 * LANES], rows - 1, 0)
    r = lax.broadcasted_iota(jnp.int32, (rows, LANES), 0)
    return jnp.where(r < rows - 1, parts[:, 0:LANES] + nxt + pe_term, 0.0)


def _compress_seq_body(r_ref, wab_ref, pe_ref, o_ref):
    parts = _dot(_bf(r_ref[0]), _bf(wab_ref[...]))
    o_ref[0] = _combine_cmp(parts, _pe_term(pe_ref, wab_ref))


def _compress_seq(rows, wab, pe):
    b, n16, kin = rows.shape
    return pl.pallas_call(
        _compress_seq_body,
        grid=(b,),
        in_specs=[pl.BlockSpec((1, n16, kin), lambda i: (i, 0, 0)),
                  pl.BlockSpec(wab.shape, lambda i: (0, 0)),
                  pl.BlockSpec(pe.shape, lambda i: (0, 0))],
        out_specs=pl.BlockSpec((1, n16, LANES), lambda i: (i, 0, 0)),
        out_shape=jax.ShapeDtypeStruct((b, n16, LANES), F32),
        compiler_params=_cparams("parallel"),
        name="nsa_compress_seq",
    )(rows, wab, pe)


def _bias_from_dist(dist, rb_ref, h):
    n = jnp.maximum(dist, 0)
    exact = T5_BUCKETS // 2
    out = jnp.full(dist.shape, rb_ref[exact, h], F32)
    for k, thr in enumerate(_bucket_thresholds()):
        out = jnp.where(n >= thr, rb_ref[exact + 1 + k, h], out)
    for j in range(exact):
        out = jnp.where(n == j, rb_ref[j, h], out)
    return out


def _softmax_rows(s, valid):
    s = jnp.where(valid, s, NEG_INF)
    m = jnp.max(s, axis=1, keepdims=True)
    m = jnp.where(m == NEG_INF, 0.0, m)
    p = jnp.exp(s - m)
    den = jnp.sum(p, axis=1, keepdims=True)
    return p / jnp.where(den > 0, den, 1.0)


def _top_blocks(score, k):
    lane = lax.broadcasted_iota(jnp.int32, score.shape, 1)
    width = score.shape[1]
    sel = jnp.zeros(score.shape, jnp.bool_)
    for _ in range(k):
        m = jnp.max(score, axis=1, keepdims=True)
        idx = jnp.min(jnp.where(score == m, lane, width), axis=1, keepdims=True)
        pick = (lane == idx) & (m > NEG_INF)
        sel = sel | pick
        score = jnp.where(pick, NEG_INF, score)
    return sel


def _tile_mask(ok):
    return jnp.concatenate([jnp.where(ok, 1.0, 0.0)] * N_HEADS, axis=0) > 0.5


def _stack_heads(x):
    return jnp.concatenate([_head(x, h) for h in range(N_HEADS)], axis=0)


def _toeplitz(tab_ref, h, r0, rows, ncol):
    nrow = (rows + ncol) // LANES
    strip = tab_ref[h, pl.ds(r0, nrow), :]
    flat = jnp.concatenate([strip[r:r + 1, :] for r in range(nrow)], axis=1)
    rolled = pltpu.roll(jnp.broadcast_to(flat, (rows, nrow * LANES)), 0, 1, stride=1, stride_axis=0)
    return rolled[:, rows:]


SLC_KC = 512


MASK_BIG = 1e30


def _top_blocks_t(score, k):
    row = lax.broadcasted_iota(jnp.int32, score.shape, 0)
    n = score.shape[0]
    sel = jnp.zeros(score.shape, jnp.bool_)
    for _ in range(k):
        m = jnp.max(score, axis=0, keepdims=True)
        idx = jnp.min(jnp.where(score == m, row, n), axis=0, keepdims=True)
        pick = (row == idx) & (m > NEG_INF)
        sel = sel | pick
        score = jnp.where(pick, NEG_INF, score)
    return sel


def _bucket_index(dist):
    n = jnp.maximum(dist, 0)
    exact = T5_BUCKETS // 2
    big = jnp.full(dist.shape, exact, jnp.int32)
    for thr in _bucket_thresholds():
        big = big + (n >= thr).astype(jnp.int32)
    return jnp.where(n < exact, n, big)


def _gather_bias(rbt_ref, h, bucket):
    rows, width = bucket.shape
    piece = min(width, LANES)
    tab = jnp.broadcast_to(rbt_ref[h:h + 1, 0:piece], (rows, piece))
    parts = [jnp.take_along_axis(tab, bucket[:, s:s + piece], axis=1) for s in range(0, width, piece)]
    return parts[0] if len(parts) == 1 else jnp.concatenate(parts, axis=1)


def _tile_rows(x, n=N_HEADS):
    return jnp.concatenate([x] * n, axis=0)


def _near_chunks():
    return (_bucket_thresholds()[-1] + SLC_KC - 2) // SLC_KC + 1


def _bias_tile(tab_ref, base, ncol, p0, lo, hi):
    r0 = (p0 - base - Q_BLOCK) // LANES
    bias = jnp.concatenate([_toeplitz(tab_ref, h, r0, Q_BLOCK, ncol) for h in range(N_HEADS)], axis=0)
    dist = (base + lax.broadcasted_iota(jnp.int32, (Q_BLOCK, ncol), 0)
            - lax.broadcasted_iota(jnp.int32, (Q_BLOCK, ncol), 1))
    return bias + _tile_rows(jnp.where((dist >= lo) & (dist < hi), 0.0, -MASK_BIG))


def _nsa_seq_body(rb_ref, q_ref, sm_ref, kx_ref, vs_ref, kvw_ref, kvc_ref, tab_ref, rbt_ref, ovlt_ref, y_ref,
                  stile_ref, wtile_ref, *, seq, p0):
    qb = Q_BLOCK
    rows = N_HEADS * qb
    n16 = seq // CMP_STRIDE
    nblk = seq // SLC_BLOCK
    bi = pl.program_id(1)
    q0 = bi * qb

    @pl.when((pl.program_id(0) == 0) & (bi == 0))
    def _():
        per_chunk = SLC_KC // qb

        def slc_tile(t, carry):
            base = (t // _near_chunks()) * qb + (t % _near_chunks()) * SLC_KC
            stile_ref[t] = _bias_tile(tab_ref, base, SLC_KC, p0, 0, seq + SLC_KC)
            return carry

        def win_tile(t, carry):
            wtile_ref[t] = _bias_tile(tab_ref, t * qb, WINDOW + qb, p0, 0, WINDOW)
            return carry

        lax.fori_loop(0, per_chunk * _near_chunks(), slc_tile, 0)
        lax.fori_loop(0, WINDOW // qb + 1, win_tile, 0)
    qs = _stack_heads(q_ref[0]) * SCALE
    qsb = _bf(qs)
    qpos = q0 + lax.broadcasted_iota(jnp.int32, (qb, 1), 0)

    kvc = kvc_ref[0]
    s_c = _dot_nt(qsb, _bf(kvc[:, 0:HEAD_DIM]))
    band = WINDOW + qb
    sw = pl.multiple_of(jnp.maximum(q0 - WINDOW, 0), qb)
    kw = kvw_ref[0, pl.ds(sw, band), 0:HEAD_DIM]
    vw = kvw_ref[0, pl.ds(sw, band), HEAD_DIM:2 * HEAD_DIM]
    s_w = _dot_nt(qsb, kw)

    cmp_end = lax.broadcasted_iota(jnp.int32, (qb, n16), 1) * CMP_STRIDE + (CMP_LEN - 1)
    dist_c = qpos - cmp_end
    bucket_c = _bucket_index(dist_c)
    bias_c = jnp.concatenate([_gather_bias(rbt_ref, h, bucket_c) for h in range(N_HEADS)], axis=0)
    p_c = _softmax_rows(s_c + bias_c, _tile_mask(dist_c >= 0))
    p_cb = _bf(p_c)
    o_c = _dot(p_cb, _bf(kvc[:, HEAD_DIM:2 * HEAD_DIM]))
    p_r = p_cb.astype(F32)
    p_sum = p_r[0:qb] + p_r[qb:2 * qb] + p_r[2 * qb:3 * qb] + p_r[3 * qb:4 * qb]
    imp_t = _dot_nt(ovlt_ref[...], p_sum, precision=HI)

    base_w = q0 - sw
    sb_w = s_w + wtile_ref[base_w // qb]
    e_w = jnp.exp(sb_w - jnp.max(sb_w, axis=1, keepdims=True))
    o_w = _dot(_bf(e_w / jnp.sum(e_w, axis=1, keepdims=True)), vw)

    qlane = q0 + lax.broadcasted_iota(jnp.int32, (1, qb), 1)
    cur = qlane // SLC_BLOCK
    rowb = lax.broadcasted_iota(jnp.int32, (nblk, qb), 0)
    forced = (rowb == 0) | (rowb == cur) | (rowb == cur - 1)
    score = jnp.where(forced | (rowb > cur), NEG_INF, imp_t)
    sel = jnp.where(forced | _top_blocks_t(score, SLC_TOPK - 3), 1.0, 0.0).T
    negm = _bf(_tile_rows((sel - 1.0) * MASK_BIG))
    q_ext = jnp.concatenate([qsb, negm], axis=1)

    thr_far = _bucket_thresholds()[-1] + SLC_KC - 1
    n_chunks = q0 // SLC_KC + 1
    n_far = jnp.maximum(q0 - thr_far + SLC_KC, 0) // SLC_KC
    n_near = _near_chunks()

    far_bias = jnp.concatenate([jnp.full((qb, 1), rb_ref[T5_BUCKETS - 1, h], F32) for h in range(N_HEADS)], axis=0)

    def scores(c):
        k0 = pl.multiple_of(c * SLC_KC, SLC_KC)
        kx = kx_ref[0, pl.ds(k0, SLC_KC), :]
        return q0 - k0, _dot_nt(q_ext, kx), vs_ref[0, pl.ds(k0, SLC_KC), :]

    def online(carry, s, vs):
        m, l, acc = carry
        m_new = jnp.maximum(m, jnp.max(s, axis=1, keepdims=True))
        alpha = jnp.exp(m - m_new)
        p = jnp.exp(s - m_new)
        return m_new, alpha * l + jnp.sum(p, axis=1, keepdims=True), alpha * acc + _dot(_bf(p), vs)

    def far_step(c, carry):
        _, s, vs = scores(c)
        return online(carry, s + far_bias, vs)

    def near_step(c, carry):
        base, s, vs = scores(c)
        tile = ((q0 % SLC_KC) // qb) * n_near + base // SLC_KC
        return online(carry, s + stile_ref[tile], vs)

    init = (jnp.full((rows, 1), -MASK_BIG / 10, F32), jnp.zeros((rows, 1), F32), jnp.zeros((rows, HEAD_DIM), F32))
    carry = lax.fori_loop(0, n_far, far_step, init)
    _, l_s, acc_s = lax.fori_loop(n_far, n_chunks, near_step, carry)
    o_s = acc_s / jnp.where(l_s > 0, l_s, 1.0)

    gates = _sigmoid(sm_ref[0])
    outs = []
    for h in range(N_HEADS):
        rs = slice(h * qb, (h + 1) * qb)
        g = lambda t: gates[:, SM_NG + 3 * h + t:SM_NG + 3 * h + t + 1]
        outs.append(g(0) * o_c[rs] + g(1) * o_s[rs] + g(2) * o_w[rs])
    y_ref[0] = jnp.concatenate(outs, axis=1)


def _nsa_seq(rel_bias, nq, small, kx, vs, kvw, kvc, tab, seq_p0):
    b, l, _ = nq.shape
    n16 = l // CMP_STRIDE
    nblk = l // SLC_BLOCK
    cmp_start = np.arange(n16)[None, :] * CMP_STRIDE
    slc_start = np.arange(nblk)[:, None] * SLC_BLOCK
    ovlt = ((cmp_start < slc_start + SLC_BLOCK) & (cmp_start + CMP_LEN > slc_start)
            & (np.arange(n16)[None, :] < n16 - 1)).astype(np.float32)
    rbt = jnp.pad(rel_bias.T, ((0, SUBLANES - N_HEADS), (0, LANES - T5_BUCKETS)))
    nrows = tab.shape[1]
    const = lambda shape: pl.BlockSpec(shape, lambda i, j: (0,) * len(shape))
    seq_spec = lambda n: pl.BlockSpec((1, l, n), lambda i, j: (i, 0, 0))
    return pl.pallas_call(
        functools.partial(_nsa_seq_body, seq=l, p0=seq_p0),
        grid=(b, l // Q_BLOCK),
        in_specs=[pl.BlockSpec(memory_space=pltpu.SMEM),
                  pl.BlockSpec((1, Q_BLOCK, D_HEADS), lambda i, j: (i, j, 0)),
                  pl.BlockSpec((1, Q_BLOCK, LANES), lambda i, j: (i, j, 0)),
                  seq_spec(HEAD_DIM + nblk), seq_spec(HEAD_DIM), seq_spec(2 * HEAD_DIM),
                  pl.BlockSpec((1, n16, LANES), lambda i, j: (i, 0, 0)),
                  const((SUBLANES, nrows, LANES)), const((SUBLANES, LANES)), const((nblk, n16))],
        out_specs=pl.BlockSpec((1, Q_BLOCK, D_HEADS), lambda i, j: (i, j, 0)),
        out_shape=jax.ShapeDtypeStruct((b, l, D_HEADS), F32),
        scratch_shapes=[pltpu.VMEM(((SLC_KC // Q_BLOCK) * _near_chunks(), N_HEADS * Q_BLOCK, SLC_KC), F32),
                        pltpu.VMEM((WINDOW // Q_BLOCK + 1, N_HEADS * Q_BLOCK, WINDOW + Q_BLOCK), F32)],
        compiler_params=_cparams("arbitrary", "arbitrary"),
        name="nsa_seq",
    )(rel_bias, nq, small, kx, vs, kvw, kvc, tab, rbt, jnp.asarray(ovlt))


def _nsa_tab_seq_dists(seq):
    max_base = max(seq, SLC_KC + (_near_chunks() - 1) * SLC_KC)
    p0 = max_base + Q_BLOCK
    length = p0 + WINDOW + Q_BLOCK + LANES
    return p0, np.maximum(p0 - np.arange(length), 0)


PAGE_GROUP = 16
PAGES_PER_STEP = 64


def _pad_rows(x, rows=SUBLANES):
    return jnp.concatenate([x, jnp.zeros((rows - x.shape[0], x.shape[1]), x.dtype)], axis=0)


def _nsa_select_body(pt_ref, *refs, past, n_steps):
    del pt_ref
    pages = refs[:PAGES_PER_STEP]
    wab_ref, pe_ref, q_ref, tabc_ref, ovlt_ref, oc_ref, top_ref, parts_ref, rows_ref = refs[PAGES_PER_STEP:]
    step = pl.program_id(1)
    blocks_per_page = PAGE_SIZE // CMP_STRIDE
    group_rows = PAGE_GROUP * blocks_per_page
    n_groups = PAGES_PER_STEP // PAGE_GROUP
    for p, pg in enumerate(pages):
        g, pp = divmod(p, PAGE_GROUP)
        rows_ref[g, pp * PAGE_SIZE:(pp + 1) * PAGE_SIZE, :] = pg[...].reshape(2 * HEAD_DIM, PAGE_SIZE).T
    for g in range(n_groups):
        acc = jnp.zeros((group_rows, 2 * LANES), F32)
        for j in range(CMP_STRIDE):
            xj = rows_ref[g, pl.ds(j, group_rows, stride=CMP_STRIDE), :]
            acc = acc + _dot(_bf(xj), wab_ref[j * LANES:(j + 1) * LANES, :])
        row0 = pl.multiple_of((step * n_groups + g) * group_rows, group_rows)
        parts_ref[pl.ds(row0, group_rows), :] = acc

    @pl.when(step == n_steps - 1)
    def _():
        n16 = past // CMP_STRIDE
        n_cmp = n16 - 1
        cur = past // SLC_BLOCK
        pe_term = _pe_term(pe_ref, wab_ref)
        kvc = _combine_cmp(parts_ref[...], pe_term)
        qs = _pad_rows(_stack_heads(q_ref[0]) * SCALE)
        s_c = _dot_nt(_bf(qs), _bf(kvc[:, 0:HEAD_DIM])) + tabc_ref[...]
        lane = lax.broadcasted_iota(jnp.int32, (SUBLANES, n16), 1)
        p_c = _softmax_rows(s_c, lane < n_cmp)
        p_cb = _bf(p_c)
        o_c = _dot(p_cb, _bf(kvc[:, HEAD_DIM:2 * HEAD_DIM]))
        oc_ref[0] = jnp.concatenate([o_c[h:h + 1, :] for h in range(N_HEADS)], axis=1)
        imp_h = _dot_nt(ovlt_ref[...], p_cb)
        imp = imp_h[:, 0:1] + imp_h[:, 1:2] + imp_h[:, 2:3] + imp_h[:, 3:4]
        nrow = imp.shape[0]
        rowb = lax.broadcasted_iota(jnp.int32, (nrow, 1), 0)
        forced_ids = (0, cur - 1, cur)
        forced = (rowb == 0) | (rowb == cur) | (rowb == cur - 1)
        score = jnp.where(forced | (rowb > cur), NEG_INF, imp)
        out_lane = lax.broadcasted_iota(jnp.int32, (1, LANES), 1)
        top = jnp.zeros((1, LANES), jnp.int32)
        for r, blk in enumerate(forced_ids):
            top = jnp.where(out_lane == r, blk, top)
        for r in range(len(forced_ids), SLC_TOPK):
            m = jnp.max(score, axis=0, keepdims=True)
            idx = jnp.min(jnp.where(score == m, rowb, nrow), axis=0, keepdims=True)
            top = jnp.where(out_lane == r, idx, top)
            score = jnp.where(rowb == idx, NEG_INF, score)
        top_ref[0] = top


def _nsa_select(layer, page_table, cache_t, wab_bf, pe, nq, tabc, past):
    b, n_pages = page_table.shape
    n_steps = n_pages // PAGES_PER_STEP
    n16 = past // CMP_STRIDE
    n_slc = past // SLC_BLOCK + 1
    nrow = -(-n_slc // SUBLANES) * SUBLANES
    cmp_start = np.arange(n16)[None, :] * CMP_STRIDE
    slc_start = np.arange(nrow)[:, None] * SLC_BLOCK
    ovlt = ((cmp_start < slc_start + SLC_BLOCK) & (cmp_start + CMP_LEN > slc_start)
            & (np.arange(n16)[None, :] < n16 - 1) & (np.arange(nrow)[:, None] < n_slc)).astype(np.float32)

    n_pool = cache_t.shape[1]

    def page_spec(p):
        def imap(i, s, pt):
            pg = pt[jnp.minimum(i, b - 1), jnp.minimum(s, n_steps - 1) * PAGES_PER_STEP + p]
            return (layer, jnp.minimum(pg, n_pool - 1), 0, 0, 0)
        return pl.BlockSpec((None, None, 2, HEAD_DIM, PAGE_SIZE), imap)

    const = lambda shape: pl.BlockSpec(shape, lambda i, s, pt: (0,) * len(shape))
    grid_spec = pltpu.PrefetchScalarGridSpec(
        num_scalar_prefetch=1,
        grid=(b, n_steps),
        in_specs=[page_spec(p) for p in range(PAGES_PER_STEP)]
        + [const(wab_bf.shape), const(pe.shape), pl.BlockSpec((1, 1, D_HEADS), lambda i, s, pt: (i, 0, 0)),
           const(tabc.shape), const(ovlt.shape)],
        out_specs=[pl.BlockSpec((1, 1, D_HEADS), lambda i, s, pt: (i, 0, 0)),
                   pl.BlockSpec((1, 1, LANES), lambda i, s, pt: (i, 0, 0))],
        scratch_shapes=[pltpu.VMEM((n16, 2 * LANES), F32),
                        pltpu.VMEM((PAGES_PER_STEP // PAGE_GROUP, PAGE_GROUP * PAGE_SIZE, 2 * HEAD_DIM), F32)],
    )
    return pl.pallas_call(
        functools.partial(_nsa_select_body, past=past, n_steps=n_steps),
        grid_spec=grid_spec,
        out_shape=[jax.ShapeDtypeStruct((b, 1, D_HEADS), F32), jax.ShapeDtypeStruct((b, 1, LANES), jnp.int32)],
        compiler_params=_cparams("parallel", "arbitrary"),
        name="nsa_select",
    )(page_table, *([cache_t] * PAGES_PER_STEP), wab_bf, pe, nq, tabc, jnp.asarray(ovlt, BF16))


def _nsa_attend_body(pt_ref, top_ref, *refs, past):
    del pt_ref
    pages = refs[:SLC_TOPK]
    (q_ref, sm_ref, new_ref, oc_ref, swa_ref, tabs_ref, tabw_ref, y_ref, swa_out) = refs[SLC_TOPK:]
    b = pl.program_id(0)
    new_blk = past // SLC_BLOCK
    blocks_per_page = PAGE_SIZE // SLC_BLOCK
    qs = _bf(_pad_rows(_stack_heads(q_ref[0]) * SCALE))
    new = new_ref[0]
    lane_t = lax.broadcasted_iota(jnp.int32, (HEAD_DIM, PAGE_SIZE), 1)
    lane = lax.broadcasted_iota(jnp.int32, (SUBLANES, PAGE_SIZE), 1)
    new_kt = jnp.where(lane_t == 0, _row_to_col(new[:, 2 * HEAD_DIM:3 * HEAD_DIM]), 0.0)
    new_vt = jnp.where(lane_t == 0, _row_to_col(new[:, 3 * HEAD_DIM:4 * HEAD_DIM]), 0.0)
    scores, oks, vts = [], [], []
    for s in range(SLC_TOPK):
        ti = jnp.clip(top_ref[b, s], 0, new_blk)
        is_new = ti == new_blk
        kt = jnp.where(is_new, new_kt, pages[s][0])
        vts.append(jnp.where(is_new, new_vt, pages[s][1]))
        scores.append(_dot(qs, _bf(kt)) + tabs_ref[ti])
        kpos = ti * SLC_BLOCK + lane % SLC_BLOCK
        ok = (lane // SLC_BLOCK == ti % blocks_per_page) & (kpos <= past)
        oks.append(jnp.where(ok, 1.0, 0.0))
    p_s = _softmax_rows(jnp.concatenate(scores, axis=1), jnp.concatenate(oks, axis=1) > 0.5)
    o_s = None
    for s in range(SLC_TOPK):
        t = _dot_nt(_bf(p_s[:, s * PAGE_SIZE:(s + 1) * PAGE_SIZE]), _bf(vts[s]))
        o_s = t if o_s is None else o_s + t

    wlane = lax.broadcasted_iota(jnp.int32, (HEAD_DIM, WINDOW), 1)
    win = []
    for c in range(2):
        col = _row_to_col(new[:, (4 + c) * HEAD_DIM:(5 + c) * HEAD_DIM])
        win.append(jnp.where(wlane == WINDOW - 1, col, pltpu.roll(swa_ref[c], WINDOW - 1, 1)))
        swa_out[c] = win[c]
    s_w = _dot(qs, _bf(win[0])) + tabw_ref[...]
    p_w = _softmax_rows(s_w, jnp.full(s_w.shape, True))
    o_w = _dot_nt(_bf(p_w), _bf(win[1]))

    gates = _sigmoid(sm_ref[0])
    o_c = oc_ref[0]
    outs = []
    for h in range(N_HEADS):
        g = lambda t: gates[:, SM_NG + 3 * h + t:SM_NG + 3 * h + t + 1]
        outs.append(g(0) * _head(o_c, h) + g(1) * o_s[h:h + 1, :] + g(2) * o_w[h:h + 1, :])
    y_ref[0] = jnp.concatenate(outs, axis=1)


def _nsa_attend(layer, page_table, top, cache_t, nq, small, nkv, o_c, swa_t, tabs, tabw, past):
    b = nq.shape[0]
    last_page = past // PAGE_SIZE - 1
    blocks_per_page = PAGE_SIZE // SLC_BLOCK

    n_pool = cache_t.shape[1]

    def page_spec(s):
        def imap(i, pt, tp):
            ii = jnp.minimum(i, b - 1)
            pg = jnp.clip(tp[ii, s] // blocks_per_page, 0, last_page)
            return (layer, jnp.clip(pt[ii, pg], 0, n_pool - 1), 1, 0, 0)
        return pl.BlockSpec((None, None, 2, HEAD_DIM, PAGE_SIZE), imap)

    row = lambda n: pl.BlockSpec((1, 1, n), lambda i, pt, tp: (i, 0, 0))
    const = lambda shape: pl.BlockSpec(shape, lambda i, pt, tp: (0,) * len(shape))
    grid_spec = pltpu.PrefetchScalarGridSpec(
        num_scalar_prefetch=2,
        grid=(b,),
        in_specs=[page_spec(s) for s in range(SLC_TOPK)]
        + [row(D_HEADS), row(LANES), row(6 * HEAD_DIM), row(D_HEADS),
           pl.BlockSpec((None, None, 2, HEAD_DIM, WINDOW), lambda i, pt, tp: (layer, i, 0, 0, 0)),
           const(tabs.shape), const(tabw.shape)],
        out_specs=[row(D_HEADS), pl.BlockSpec((None, 2, HEAD_DIM, WINDOW), lambda i, pt, tp: (i, 0, 0, 0))],
    )
    return pl.pallas_call(
        functools.partial(_nsa_attend_body, past=past),
        grid_spec=grid_spec,
        out_shape=[jax.ShapeDtypeStruct((b, 1, D_HEADS), F32),
                   jax.ShapeDtypeStruct((b, 2, HEAD_DIM, WINDOW), F32)],
        compiler_params=_cparams("parallel"),
        name="nsa_attend",
    )(page_table, top, *([cache_t] * SLC_TOPK), nq, small, nkv, o_c, swa_t, tabs, tabw)


SEQ_CHUNK = 64
GDN_CHUNKS = 1
SCAN_CHUNK = 256
ROW_TILE = 512
FFN_TILE = 512


def _lane_vec(pairs):
    v = jnp.zeros((1, LANES), F32)
    for off, vals in pairs:
        v = v.at[0, off:off + vals.shape[0]].set(vals.astype(F32))
    return v


def kernel(x_prompt, x_sample, cache_nsa_kv, cache_swa_kv, state_ret, state_mlstm_C, state_mlstm_n, state_mlstm_m, state_gdn, state_gdn_conv, state_ffn_conv, page_table, rel_bias, norm_pre_mix, w_in, mlstm_b_i, mlstm_b_f, gdn_conv_w, gdn_A_log, gdn_dt_bias, nsa_cmp_pe, nsa_cmp_w, ret_norm, mlstm_norm, gdn_norm, w_out, norm_post_mix, norm_pre_ffn, w_ffn_gate, w_ffn_up, ffn_conv_w, w_ffn_down, norm_post_ffn):
    depth = w_in.shape[0]
    bp, seq, _ = x_prompt.shape
    bs, dec_seq, _ = x_sample.shape
    assert dec_seq == 1 and seq % SLC_KC == 0 and seq >= WINDOW + Q_BLOCK
    n_pool = cache_nsa_kv.shape[1]
    past = page_table.shape[1] * PAGE_SIZE
    assert past >= WINDOW and past % SLC_BLOCK == 0 and page_table.shape[1] % PAGES_PER_STEP == 0
    assert past // SLC_BLOCK + 1 >= SLC_TOPK

    p0, d_seq = _nsa_tab_seq_dists(seq)
    n16 = past // CMP_STRIDE
    n_slc = past // SLC_BLOCK + 1
    d_cmp = np.maximum(past - (np.arange(n16) * CMP_STRIDE + CMP_LEN - 1), 0)
    d_slc = np.maximum(past - np.arange(n_slc * SLC_BLOCK), 0)
    d_win = WINDOW - 1 - np.arange(WINDOW)
    tab = _bias_tables(rel_bias, np.concatenate([d_seq, d_cmp, d_slc, d_win]))
    o1, o2, o3 = len(d_seq), len(d_seq) + len(d_cmp), len(d_seq) + len(d_cmp) + len(d_slc)
    tab_seq = tab[:, :o1].reshape(SUBLANES, o1 // LANES, LANES)
    tabc = tab[:, o1:o2]
    tabs = jnp.transpose(tab[:, o2:o3].reshape(SUBLANES, n_slc, SLC_BLOCK), (1, 0, 2))
    tabs = jnp.concatenate([tabs, tabs], axis=2)
    tabw = tab[:, o3:]

    cache_t = jnp.transpose(cache_nsa_kv, (0, 1, 3, 4, 2))
    swa_t = jnp.transpose(cache_swa_kv, (0, 1, 3, 4, 2))
    cos_p, sin_p = _rope_tables(jnp.arange(seq))
    cos_s, sin_s = _rope_tables(jnp.arange(past, past + 1))

    xp = x_prompt
    xs = x_sample.reshape(bs, D_MODEL)
    p_states, s_states = [], []
    for l in range(depth):
        w_cat = _prep_w_in(w_in[l])
        w_out_bf = _bf(w_out[l])
        wg, wu, wd = _bf(w_ffn_gate[l]), _bf(w_ffn_up[l]), _bf(w_ffn_down[l])
        wab, pe = _prep_cmp_w(nsa_cmp_w[l], nsa_cmp_pe[l])
        ml_bias = _lane_vec([(SM_MI, mlstm_b_i[l]), (SM_MF, mlstm_b_f[l])])
        dtb = _lane_vec([(SM_GA, gdn_dt_bias[l])])
        alog = _lane_vec([(SM_GA, gdn_A_log[l])])

        z = _in_proj(xp.reshape(bp * seq, D_MODEL), norm_pre_mix[l], w_cat, ROW_TILE, seq=seq)
        z_ret, z_ml, z_gdn, z_nq, z_nkv, z_sm = [t.reshape(bp, seq, -1) for t in z[:6]]
        nkv_t = z[6].reshape(bp, 6, HEAD_DIM, seq)
        kx, vs, kvw = [t.reshape(bp, seq, -1) for t in z[7:]]
        y_ret, st_ret = _retention_seq(z_ret, cos_p, sin_p, ret_norm[l], SCAN_CHUNK, bp)
        y_ml, st_c, st_n, st_m = _mlstm_seq(z_ml, z_sm, ml_bias, mlstm_norm[l], SCAN_CHUNK, bp)
        y_gdn, st_g = _gdn_seq(z_gdn, z_sm, gdn_conv_w[l], dtb, alog, gdn_norm[l], SEQ_CHUNK, bp, GDN_CHUNKS)
        rows = z_nkv[..., 0:2 * HEAD_DIM].reshape(bp, seq // CMP_STRIDE, CMP_STRIDE * 2 * HEAD_DIM)
        kvc = _compress_seq(rows, wab, pe)
        y_nsa = _nsa_seq(rel_bias, z_nq, z_sm, kx, vs, kvw, kvc, tab_seq, p0)
        xp, tail = _ffn_seq(xp, (y_ret, y_ml, y_gdn, y_nsa), w_out_bf, norm_post_mix[l], norm_pre_ffn[l],
                            wg, wu, wd, ffn_conv_w[l], norm_post_ffn[l], min(seq, FFN_TILE))
        p_states.append((
            jnp.transpose(nkv_t[:, 0:4], (0, 3, 1, 2)),
            jnp.transpose(nkv_t[:, 4:6, :, seq - WINDOW:], (0, 3, 1, 2)),
            st_ret, st_c, st_n, st_m[:, 0, :N_HEADS], st_g,
            z_gdn[:, seq - (GDN_CONV - 1):, 0:3 * D_HEADS],
            tail[:, SUBLANES - (FFN_CONV - 1):, :]))

        z = _in_proj(xs, norm_pre_mix[l], w_cat, bs)
        z_ret, z_ml, z_gdn, z_nq, z_nkv, z_sm = [t.reshape(bs, 1, -1) for t in z]
        m_pad = jnp.pad(state_mlstm_m[l], ((0, 0), (0, LANES - N_HEADS))).reshape(bs, 1, LANES)
        (y_ret, y_ml, y_gdn, st_ret, st_c, st_n, st_m, st_g, st_gc) = _mixers_step(
            z_ret, z_ml, z_gdn, z_sm, cos_s, sin_s, ml_bias, dtb, alog, gdn_conv_w[l],
            ret_norm[l], mlstm_norm[l], gdn_norm[l],
            state_ret[l], state_mlstm_C[l], state_mlstm_n[l], m_pad, state_gdn[l], state_gdn_conv[l])
        o_c, top = _nsa_select(l, page_table, cache_t, _bf(wab), pe, z_nq, tabc, past)
        y_nsa, swa_new = _nsa_attend(l, page_table, top[:, 0, :SLC_TOPK], cache_t, z_nq, z_sm, z_nkv, o_c,
                                     swa_t, tabs, tabw, past)
        ys = [t.reshape(bs, D_HEADS) for t in (y_ret, y_ml, y_gdn, y_nsa)]
        x1, h2 = _out_proj(xs, ys, w_out_bf, norm_post_mix[l], norm_pre_ffn[l], bs)
        xs, g_new = _ffn_step(x1, h2, state_ffn_conv[l][:, 0], state_ffn_conv[l][:, 1], wg, wu, wd,
                              ffn_conv_w[l], norm_post_ffn[l])
        s_states.append((
            z_nkv[..., 0:4 * HEAD_DIM].reshape(bs, 1, 4, HEAD_DIM),
            jnp.transpose(swa_new, (0, 3, 1, 2)),
            st_ret, st_c, st_n, st_m[:, 0, :N_HEADS], st_g, st_gc,
            jnp.concatenate([state_ffn_conv[l][:, 1:], g_new[:, None, :]], axis=1)))

    stack = lambda states: tuple(jnp.stack([s[k] for s in states]) for k in range(len(states[0])))
    return (xp, xs.reshape(bs, 1, D_MODEL)) + stack(p_states) + stack(s_states)
```

```python
import functools
import math

import jax
import jax.numpy as jnp
import numpy as np
from jax import lax
from jax.experimental import pallas as pl
from jax.experimental.pallas import tpu as pltpu

F32 = jnp.float32
BF16 = jnp.bfloat16
HI = lax.Precision.HIGHEST

D_MODEL = 1024
HEAD_DIM = 64
N_HEADS = 4
D_HEADS = N_HEADS * HEAD_DIM
D_FF = 2816
GDN_CONV = 4
FFN_CONV = 3
PAGE_SIZE = 128
CMP_STRIDE = 16
CMP_LEN = 32
SLC_BLOCK = 64
SLC_TOPK = 16
WINDOW = 512
Q_BLOCK = 128
T5_BUCKETS = 32
T5_MAX_DIST = 1024
ROPE_BASE = 10000.0
EPS = 1e-6
SCALE = HEAD_DIM ** -0.5
IN_WIDTHS = (256, 256, 256, 256, 256, 256, 256, 4, 4, 256, 768, 4, 4, 256, 256, 384, 12)
LANES = 128
SUBLANES = 8
VMEM_LIMIT = 56 * 1024 * 1024
NEG_INF = float("-inf")

SM_MI, SM_MF, SM_GA, SM_GB, SM_NG = 0, 4, 8, 12, 16


def _cparams(*sem):
    return pltpu.CompilerParams(dimension_semantics=sem, vmem_limit_bytes=VMEM_LIMIT)


def _rms(x, g):
    return x * lax.rsqrt(jnp.mean(x * x, axis=-1, keepdims=True) + EPS) * g


def _dot(a, b, **kw):
    return jnp.dot(a, b, preferred_element_type=F32, **kw)


def _dot_nt(a, b, **kw):
    return lax.dot_general(a, b, (((1,), (1,)), ((), ())), preferred_element_type=F32, **kw)


def _dot_tn(a, b, **kw):
    return lax.dot_general(a, b, (((0,), (0,)), ((), ())), preferred_element_type=F32, **kw)


def _bf(x):
    return x.astype(BF16)


def _sigmoid(x):
    return 1.0 / (1.0 + jnp.exp(-x))


def _silu(x):
    return x * _sigmoid(x)


def _softplus(x):
    return jnp.maximum(x, 0.0) + jnp.log(1.0 + jnp.exp(-jnp.abs(x)))


def _head_rms(o):
    return o * lax.rsqrt(jnp.mean(o * o, axis=-1, keepdims=True) + EPS)


PROJ_WIDTHS = (1024, 1024, 1024, 256, 384, 128)


def _prep_w_in(w):
    wt = w.T
    parts, off = [], 0
    for wd in IN_WIDTHS:
        parts.append(wt[off:off + wd])
        off += wd
    (rq, rk, rv, rg, mq, mk, mv, mi, mf, mo, gqkv, ga, gbeta, gg, nq, nkv, ngate) = parts
    small = jnp.concatenate([mi, mf, ga, gbeta, ngate], axis=0)
    small = jnp.pad(small, ((0, LANES - small.shape[0]), (0, 0)))
    cat = jnp.concatenate([rq, rk, rv, rg, mq, mk, mv, mo, gqkv, gg, nq, nkv, small], axis=0)
    return cat.astype(BF16)


NKV_OFF = sum(PROJ_WIDTHS[:4])


def _in_proj_body(x_ref, g_ref, wt_ref, *out_refs, seq):
    h = _bf(_rms(x_ref[...], g_ref[...]))
    off = 0
    for ref in out_refs[:len(PROJ_WIDTHS)]:
        n = ref.shape[-1]
        ref[...] = _dot_nt(h, wt_ref[off:off + n, :])
        off += n
    if seq is not None:
        kvt_ref, kx_ref, vs_ref, kvw_ref = out_refs[len(PROJ_WIDTHS):]
        kvt_ref[0] = _dot_nt(wt_ref[NKV_OFF:NKV_OFF + PROJ_WIDTHS[4], :], h)
        nkv = out_refs[4][...]
        tm = nkv.shape[0]
        nblk = kx_ref.shape[-1] - HEAD_DIM
        pos = (pl.program_id(0) * tm) % seq + lax.broadcasted_iota(jnp.int32, (tm, nblk), 0)
        block_id = jnp.where(pos // SLC_BLOCK == lax.broadcasted_iota(jnp.int32, (tm, nblk), 1), 1.0, 0.0)
        kx_ref[...] = _bf(jnp.concatenate([nkv[:, 2 * HEAD_DIM:3 * HEAD_DIM], block_id], axis=1))
        vs_ref[...] = _bf(nkv[:, 3 * HEAD_DIM:4 * HEAD_DIM])
        kvw_ref[...] = _bf(nkv[:, 4 * HEAD_DIM:6 * HEAD_DIM])


def _in_proj(x2d, g, w_cat_t, tm, seq=None):
    m = x2d.shape[0]
    ntot = sum(PROJ_WIDTHS)
    const = lambda shape: pl.BlockSpec(shape, lambda i: (0,) * len(shape), pipeline_mode=pl.Buffered(1))
    out_specs = [pl.BlockSpec((tm, n), lambda i: (i, 0)) for n in PROJ_WIDTHS]
    out_shape = [jax.ShapeDtypeStruct((m, n), F32) for n in PROJ_WIDTHS]
    if seq is not None:
        per_seq = seq // tm
        nkv = PROJ_WIDTHS[4]
        out_specs.append(pl.BlockSpec((1, nkv, tm), lambda i: (i // per_seq, 0, i % per_seq)))
        out_shape.append(jax.ShapeDtypeStruct((m // seq, nkv, seq), F32))
        for n in (HEAD_DIM + seq // SLC_BLOCK, HEAD_DIM, 2 * HEAD_DIM):
            out_specs.append(pl.BlockSpec((tm, n), lambda i: (i, 0)))
            out_shape.append(jax.ShapeDtypeStruct((m, n), BF16))
    return pl.pallas_call(
        functools.partial(_in_proj_body, seq=seq),
        grid=(m // tm,),
        in_specs=[pl.BlockSpec((tm, D_MODEL), lambda i: (i, 0)), const((1, D_MODEL)), const((ntot, D_MODEL))],
        out_specs=out_specs,
        out_shape=out_shape,
        compiler_params=_cparams("parallel"),
        name="in_proj",
    )(x2d, g.reshape(1, D_MODEL), w_cat_t)


def _out_proj_body(x_ref, y0, y1, y2, y3, w_ref, gpost_ref, gpre_ref, x1_ref, h2_ref):
    acc = None
    for k, y in enumerate((y0, y1, y2, y3)):
        t = _dot(_bf(y[...]), w_ref[k * D_HEADS:(k + 1) * D_HEADS, :])
        acc = t if acc is None else acc + t
    x1 = x_ref[...] + _rms(acc, gpost_ref[...])
    x1_ref[...] = x1
    h2_ref[...] = _bf(_rms(x1, gpre_ref[...]))


def _out_proj(x2d, ys, w_out_bf, gpost, gpre, tm):
    m = x2d.shape[0]
    row = lambda n: pl.BlockSpec((tm, n), lambda i: (i, 0))
    vec = pl.BlockSpec((1, D_MODEL), lambda i: (0, 0))
    return pl.pallas_call(
        _out_proj_body,
        grid=(m // tm,),
        in_specs=[row(D_MODEL)] + [row(D_HEADS)] * 4
        + [pl.BlockSpec((D_MODEL, D_MODEL), lambda i: (0, 0)), vec, vec],
        out_specs=[row(D_MODEL), row(D_MODEL)],
        out_shape=[jax.ShapeDtypeStruct((m, D_MODEL), F32), jax.ShapeDtypeStruct((m, D_MODEL), BF16)],
        compiler_params=_cparams("parallel"),
        name="out_proj",
    )(x2d, *ys, w_out_bf, gpost.reshape(1, -1), gpre.reshape(1, -1))


FFN_TN = 256


def _gelu_tanh(x):
    c = 0.7978845608028654
    hx = 0.5 * x
    return hx + hx * jnp.tanh(x * (c + (0.044715 * c) * (x * x)))


def _ffn_seq_body(x_ref, y0, y1, y2, y3, wo_ref, gmix_ref, gpre_ref, wg_ref, wu_ref, wd_ref, cw_ref, gpost_ref,
                  x2_ref, tail_ref, carry_ref, act_ref, *, tm):
    @pl.when(pl.program_id(1) == 0)
    def _():
        carry_ref[...] = jnp.zeros_like(carry_ref)

    mix = None
    for k, y in enumerate((y0, y1, y2, y3)):
        t = _dot(_bf(y[0]), wo_ref[k * D_HEADS:(k + 1) * D_HEADS, :])
        mix = t if mix is None else mix + t
    x1 = x_ref[0] + _rms(mix, gmix_ref[...])
    h2 = _bf(_rms(x1, gpre_ref[...]))
    row8 = lax.broadcasted_iota(jnp.int32, (SUBLANES, FFN_TN), 0)
    starts = list(range(0, D_FF, FFN_TN))
    up = lambda n0: (_dot(h2, wg_ref[:, n0:n0 + FFN_TN]), _dot(h2, wu_ref[:, n0:n0 + FFN_TN]))
    nxt = up(starts[0])
    for i, n0 in enumerate(starts):
        g, u = nxt
        if i + 1 < len(starts):
            nxt = up(starts[i + 1])
        prev = carry_ref[:, n0:n0 + FFN_TN]
        g1, g2 = pltpu.roll(g, 1, 0), pltpu.roll(g, 2, 0)
        top1 = jnp.where(row8 == 0, prev[7:8, :], g1[0:SUBLANES])
        top2 = jnp.where(row8 == 0, prev[6:7, :], jnp.where(row8 == 1, prev[7:8, :], g2[0:SUBLANES]))
        g1 = jnp.concatenate([top1, g1[SUBLANES:]], axis=0)
        g2 = jnp.concatenate([top2, g2[SUBLANES:]], axis=0)
        cw = cw_ref[:, n0:n0 + FFN_TN]
        gate = cw[0:1, :] * g2 + cw[1:2, :] * g1 + cw[2:3, :] * g
        a = _gelu_tanh(gate) * u
        act_ref[:, n0:n0 + FFN_TN] = _bf(a)
        carry_ref[:, n0:n0 + FFN_TN] = g[tm - SUBLANES:tm, :]
    acc = _dot(act_ref[...], wd_ref[...])
    x2_ref[0] = x1 + _rms(acc, gpost_ref[...])
    tail_ref[0] = carry_ref[...]


def _ffn_seq(x, ys, w_out_bf, gmix, gpre, wg, wu, wd, cw, gpost, tm):
    b, l, _ = x.shape
    const = lambda shape: pl.BlockSpec(shape, lambda i, j: (0,) * len(shape), pipeline_mode=pl.Buffered(1))
    row = lambda n: pl.BlockSpec((1, tm, n), lambda i, j: (i, j, 0))
    return pl.pallas_call(
        functools.partial(_ffn_seq_body, tm=tm),
        grid=(b, l // tm),
        in_specs=[row(D_MODEL)] + [row(D_HEADS)] * 4
        + [const((D_MODEL, D_MODEL)), const((1, D_MODEL)), const((1, D_MODEL)),
           const((D_MODEL, D_FF)), const((D_MODEL, D_FF)), const((D_FF, D_MODEL)),
           const((FFN_CONV, D_FF)), const((1, D_MODEL))],
        out_specs=[pl.BlockSpec((1, tm, D_MODEL), lambda i, j: (i, j, 0)),
                   pl.BlockSpec((1, SUBLANES, D_FF), lambda i, j: (i, 0, 0))],
        out_shape=[jax.ShapeDtypeStruct((b, l, D_MODEL), F32),
                   jax.ShapeDtypeStruct((b, SUBLANES, D_FF), F32)],
        scratch_shapes=[pltpu.VMEM((SUBLANES, D_FF), F32), pltpu.VMEM((tm, D_FF), BF16)],
        compiler_params=_cparams("parallel", "arbitrary"),
        name="ffn_seq",
    )(x, *ys, w_out_bf, gmix.reshape(1, -1), gpre.reshape(1, -1), wg, wu, wd, cw, gpost.reshape(1, -1))


def _ffn_step_body(x1_ref, h2_ref, b0_ref, b1_ref, wg_ref, wu_ref, wd_ref, cw_ref, gpost_ref, x2_ref, g_ref):
    h2 = h2_ref[...]
    acc = jnp.zeros(x1_ref.shape, F32)
    for n0 in range(0, D_FF, FFN_TN):
        sl = slice(n0, n0 + FFN_TN)
        g = _dot(h2, wg_ref[:, sl])
        cw = cw_ref[:, sl]
        gate = cw[0:1, :] * b0_ref[:, sl] + cw[1:2, :] * b1_ref[:, sl] + cw[2:3, :] * g
        u = _dot(h2, wu_ref[:, sl])
        acc = acc + _dot(_bf(_gelu_tanh(gate) * u), wd_ref[sl, :])
        g_ref[:, sl] = g
    x2_ref[...] = x1_ref[...] + _rms(acc, gpost_ref[...])


def _ffn_step(x1, h2, b0, b1, wg, wu, wd, cw, gpost):
    m = x1.shape[0]
    return pl.pallas_call(
        _ffn_step_body,
        out_shape=[jax.ShapeDtypeStruct((m, D_MODEL), F32), jax.ShapeDtypeStruct((m, D_FF), F32)],
        compiler_params=pltpu.CompilerParams(vmem_limit_bytes=VMEM_LIMIT),
        name="ffn_step",
    )(x1, h2, b0, b1, wg, wu, wd, cw, gpost.reshape(1, -1))


def _rope_tables(pos):
    half = HEAD_DIM // 2
    inv = ROPE_BASE ** (-jnp.linspace(0.0, 1.0, half, dtype=F32))
    ang = pos.astype(F32)[:, None] * inv[None, :]
    return jnp.tile(jnp.cos(ang), (1, LANES // half)), jnp.tile(jnp.sin(ang), (1, LANES // half))


def _rope128(x, cos, sin):
    lane = lax.broadcasted_iota(jnp.int32, x.shape, 1)
    first = (lane % HEAD_DIM) < (HEAD_DIM // 2)
    other = jnp.where(first, -pltpu.roll(x, LANES - HEAD_DIM // 2, 1), pltpu.roll(x, HEAD_DIM // 2, 1))
    return x * cos + other * sin


def _rope256(x, cos, sin):
    return jnp.concatenate([_rope128(x[:, :LANES], cos, sin), _rope128(x[:, LANES:], cos, sin)], axis=1)


def _ret_log_decay(h):
    return math.log(1.0 - 2.0 ** (-5.0 - h))


def _head(x, h):
    return x[:, h * HEAD_DIM:(h + 1) * HEAD_DIM]


def _ret_body(z_ref, cos_ref, sin_ref, nw_ref, y_ref, s_ref, *, chunk):
    @pl.when(pl.program_id(1) == 0)
    def _():
        s_ref[...] = jnp.zeros_like(s_ref)

    nb = z_ref.shape[0]
    cos, sin = cos_ref[...], sin_ref[...]
    t = lax.broadcasted_iota(jnp.int32, (chunk, chunk), 0)
    s = lax.broadcasted_iota(jnp.int32, (chunk, chunk), 1)
    causal = t >= s
    diff = jnp.where(causal, t - s, 0).astype(F32)
    tcol = lax.broadcasted_iota(jnp.int32, (chunk, 1), 0).astype(F32)
    lgs = [_ret_log_decay(h) for h in range(N_HEADS)]
    dmat = [jnp.where(causal, jnp.exp(lg * diff), 0.0) for lg in lgs]
    xi = [jnp.exp(lg * (tcol + 1.0)) for lg in lgs]
    zeta = [jnp.exp(lg * (chunk - 1.0 - tcol)) for lg in lgs]
    probs = [(bb, h) for bb in range(nb) for h in range(N_HEADS)]
    qkv = []
    for bb in range(nb):
        q = _rope256(z_ref[bb, :, 0:256], cos, sin)
        k = _rope256(z_ref[bb, :, 256:512], cos, sin) * SCALE
        qkv.append((q, k, z_ref[bb, :, 512:768]))
    qh = [_head(qkv[bb][0], h) for bb, h in probs]
    kh = [_head(qkv[bb][1], h) for bb, h in probs]
    vh = [_bf(_head(qkv[bb][2], h)) for bb, h in probs]
    st = [s_ref[bb, h] for bb, h in probs]
    sc = [_dot_nt(_bf(q), _bf(k)) for q, k in zip(qh, kh)]
    cross = [_dot(_bf(q * xi[h]), _bf(s0)) for q, s0, (_, h) in zip(qh, st, probs)]
    upd = [_dot_tn(_bf(k * zeta[h]), v) for k, v, (_, h) in zip(kh, vh, probs)]
    o = [_dot(_bf(x * dmat[h]), v) + c for x, v, c, (_, h) in zip(sc, vh, cross, probs)]
    for i, (bb, h) in enumerate(probs):
        s_ref[bb, h] = st[i] * math.exp(lgs[h] * chunk) + upd[i]
    for bb in range(nb):
        outs = [_head_rms(o[bb * N_HEADS + h]) for h in range(N_HEADS)]
        y_ref[bb] = jnp.concatenate(outs, axis=1) * nw_ref[...] * _silu(z_ref[bb, :, 768:1024])


def _retention_seq(z_ret, cos, sin, nw, chunk, nb):
    b, l, _ = z_ret.shape
    return pl.pallas_call(
        functools.partial(_ret_body, chunk=chunk),
        grid=(b // nb, l // chunk),
        in_specs=[pl.BlockSpec((nb, chunk, 1024), lambda i, j: (i, j, 0)),
                  pl.BlockSpec((chunk, LANES), lambda i, j: (j, 0)),
                  pl.BlockSpec((chunk, LANES), lambda i, j: (j, 0)),
                  pl.BlockSpec((1, D_HEADS), lambda i, j: (0, 0))],
        out_specs=[pl.BlockSpec((nb, chunk, D_HEADS), lambda i, j: (i, j, 0)),
                   pl.BlockSpec((nb, N_HEADS, HEAD_DIM, HEAD_DIM), lambda i, j: (i, 0, 0, 0))],
        out_shape=[jax.ShapeDtypeStruct((b, l, D_HEADS), F32),
                   jax.ShapeDtypeStruct((b, N_HEADS, HEAD_DIM, HEAD_DIM), F32)],
        compiler_params=_cparams("parallel", "arbitrary"),
        name="retention_seq",
    )(z_ret, cos, sin, nw.reshape(1, -1))


def _lanes_to_rows(x, n=2 * SUBLANES):
    sel = (lax.broadcasted_iota(jnp.int32, (n, x.shape[1]), 0)
           == lax.broadcasted_iota(jnp.int32, (n, x.shape[1]), 1)).astype(F32)
    return _dot_nt(sel, x, precision=HI)


def _tri_incl(n):
    t = lax.broadcasted_iota(jnp.int32, (n, n), 0)
    s = lax.broadcasted_iota(jnp.int32, (n, n), 1)
    return t >= s


def _ml_body(z_ref, sm_ref, bias_ref, nw_ref, y_ref, c_ref, n_ref, m_ref, *, chunk):
    @pl.when(pl.program_id(1) == 0)
    def _():
        c_ref[...] = jnp.zeros_like(c_ref)
        n_ref[...] = jnp.zeros_like(n_ref)
        m_ref[...] = jnp.zeros_like(m_ref)

    nb = z_ref.shape[0]
    incl = _tri_incl(chunk)
    tri = incl.astype(F32)
    probs = [(bb, h) for bb in range(nb) for h in range(N_HEADS)]
    gate = []
    for bb in range(nb):
        pre = sm_ref[bb] + bias_ref[...]
        logf = jnp.minimum(pre, 0.0) - jnp.log(1.0 + jnp.exp(-jnp.abs(pre)))
        bcum = _dot(tri, logf, precision=HI)
        gate.append((pre, bcum, _lanes_to_rows(pre), _lanes_to_rows(bcum), m_ref[bb]))
    qh = [_head(z_ref[bb, :, 0:256], h) for bb, h in probs]
    kh = [_head(z_ref[bb, :, 256:512], h) * SCALE for bb, h in probs]
    vh = [_bf(_head(z_ref[bb, :, 512:768], h)) for bb, h in probs]
    cm = [c_ref[bb, h] for bb, h in probs]
    nv = [n_ref[bb, h:h + 1, :] for bb, h in probs]
    qk_raw = [_dot_nt(_bf(q), _bf(k)) for q, k in zip(qh, kh)]
    q_c = [_dot(_bf(q), _bf(c)) for q, c in zip(qh, cm)]
    i_col, b_col, inter, dlog = [], [], [], []
    for bb, h in probs:
        pre, bcum, pre_t, bcum_t, m_all = gate[bb]
        i_col.append(pre[:, SM_MI + h:SM_MI + h + 1])
        b_col.append(bcum[:, SM_MF + h:SM_MF + h + 1])
        inter.append(b_col[-1] + m_all[:, h:h + 1])
        dlog.append(jnp.where(incl, b_col[-1] - bcum_t[SM_MF + h:SM_MF + h + 1, :]
                              + pre_t[SM_MI + h:SM_MI + h + 1, :], NEG_INF))
    dmax = [jnp.max(x, axis=1, keepdims=True) for x in dlog]
    q_n = [jnp.sum(q * n, axis=1, keepdims=True) for q, n in zip(qh, nv)]
    stab = []
    for i in range(len(probs)):
        m_t = jnp.maximum(inter[i], dmax[i])
        m_end = m_t[chunk - 1:chunk, :]
        stab.append(dict(m_t=m_t, w=jnp.exp(dlog[i] - m_t), g_in=jnp.exp(inter[i] - m_t), m_end=m_end,
                         w_end=jnp.exp(b_col[i][chunk - 1:chunk, :] - b_col[i] + i_col[i] - m_end),
                         g_end=jnp.exp(inter[i][chunk - 1:chunk, :] - m_end)))
    qk = [x * d["w"] for x, d in zip(qk_raw, stab)]
    num = [_dot(_bf(x), v) + d["g_in"] * c for x, v, d, c in zip(qk, vh, stab, q_c)]
    upd = [_dot_tn(_bf(k * d["w_end"]), v) for k, d, v in zip(kh, stab, vh)]
    qk_sum = [jnp.sum(x, axis=1, keepdims=True) for x in qk]
    k_sum = [jnp.sum(d["w_end"] * k, axis=0, keepdims=True) for d, k in zip(stab, kh)]
    hh = []
    for i, (bb, h) in enumerate(probs):
        d = stab[i]
        den = qk_sum[i] + d["g_in"] * q_n[i]
        hh.append(num[i] / jnp.maximum(jnp.abs(den), jnp.exp(-d["m_t"])))
        c_ref[bb, h] = d["g_end"] * cm[i] + upd[i]
        n_ref[bb, h:h + 1, :] = d["g_end"] * nv[i] + k_sum[i]
    lane = lax.broadcasted_iota(jnp.int32, (1, LANES), 1)
    for bb in range(nb):
        m_out = jnp.zeros((1, LANES), F32)
        outs = []
        for h in range(N_HEADS):
            i = bb * N_HEADS + h
            m_out = jnp.where(lane == h, stab[i]["m_end"], m_out)
            outs.append(_head_rms(_sigmoid(_head(z_ref[bb, :, 768:1024], h)) * hh[i]))
        m_ref[bb] = m_out
        y_ref[bb] = jnp.concatenate(outs, axis=1) * nw_ref[...]


def _mlstm_seq(z_ml, small, bias_vec, nw, chunk, nb):
    b, l, _ = z_ml.shape
    return pl.pallas_call(
        functools.partial(_ml_body, chunk=chunk),
        grid=(b // nb, l // chunk),
        in_specs=[pl.BlockSpec((nb, chunk, 1024), lambda i, j: (i, j, 0)),
                  pl.BlockSpec((nb, chunk, LANES), lambda i, j: (i, j, 0)),
                  pl.BlockSpec((1, LANES), lambda i, j: (0, 0)),
                  pl.BlockSpec((1, D_HEADS), lambda i, j: (0, 0))],
        out_specs=[pl.BlockSpec((nb, chunk, D_HEADS), lambda i, j: (i, j, 0)),
                   pl.BlockSpec((nb, N_HEADS, HEAD_DIM, HEAD_DIM), lambda i, j: (i, 0, 0, 0)),
                   pl.BlockSpec((nb, N_HEADS, HEAD_DIM), lambda i, j: (i, 0, 0)),
                   pl.BlockSpec((nb, 1, LANES), lambda i, j: (i, 0, 0))],
        out_shape=[jax.ShapeDtypeStruct((b, l, D_HEADS), F32),
                   jax.ShapeDtypeStruct((b, N_HEADS, HEAD_DIM, HEAD_DIM), F32),
                   jax.ShapeDtypeStruct((b, N_HEADS, HEAD_DIM), F32),
                   jax.ShapeDtypeStruct((b, 1, LANES), F32)],
        compiler_params=_cparams("parallel", "arbitrary"),
        name="mlstm_seq",
    )(z_ml, small, bias_vec, nw.reshape(1, -1))


def _split_bf16(x):
    hi = _bf(x)
    return hi, _bf(x - hi.astype(F32))


def _dot3(a, b):
    (ah, al), (bh, bl) = a, b
    return _dot(ah, bh) + (_dot(ah, bl) + _dot(al, bh))


def _l2norm(x):
    return x * lax.rsqrt(jnp.sum(x * x, axis=-1, keepdims=True) + EPS)


def _gdn_body(z_ref, sm_ref, cw_ref, dtb_ref, alog_ref, nw_ref, y_ref, s_ref, buf_ref, *, chunk):
    @pl.when(pl.program_id(1) == 0)
    def _():
        s_ref[...] = jnp.zeros_like(s_ref)
        buf_ref[:, 0:SUBLANES, :] = jnp.zeros((buf_ref.shape[0], SUBLANES, 3 * D_HEADS), F32)

    nb = z_ref.shape[0]
    rows = z_ref.shape[1]
    nc = rows // chunk
    incl = _tri_incl(chunk)
    strict = lax.broadcasted_iota(jnp.int32, (chunk, chunk), 0) > lax.broadcasted_iota(jnp.int32, (chunk, chunk), 1)
    eye = (lax.broadcasted_iota(jnp.int32, (chunk, chunk), 0)
           == lax.broadcasted_iota(jnp.int32, (chunk, chunk), 1)).astype(F32)
    tri = incl.astype(F32)
    gates, acts = {}, []
    for bb in range(nb):
        buf_ref[bb, SUBLANES:SUBLANES + rows, :] = z_ref[bb, :, 0:768]
        conv = None
        for j in range(GDN_CONV):
            term = buf_ref[bb, pl.ds(SUBLANES - (GDN_CONV - 1) + j, rows), :] * cw_ref[j:j + 1, :]
            conv = term if conv is None else conv + term
        buf_ref[bb, 0:SUBLANES, :] = buf_ref[bb, rows:rows + SUBLANES, :]
        acts.append(_silu(conv))
        sm = sm_ref[bb]
        g_all = -jnp.exp(alog_ref[...]) * _softplus(sm + dtb_ref[...])
        beta_all = _sigmoid(sm)
        for cc in range(nc):
            rs = slice(cc * chunk, (cc + 1) * chunk)
            gcum = _dot(tri, g_all[rs], precision=HI)
            gates[bb, cc] = (gcum, _lanes_to_rows(gcum), beta_all[rs])

    probs = [(bb, cc, h) for cc in range(nc) for bb in range(nb) for h in range(N_HEADS)]
    pre = []
    for bb, cc, h in probs:
        rs = slice(cc * chunk, (cc + 1) * chunk)
        gcum, gcum_t, beta_all = gates[bb, cc]
        g_col = gcum[:, SM_GA + h:SM_GA + h + 1]
        g_row = gcum_t[SM_GA + h:SM_GA + h + 1, :]
        beta = beta_all[:, SM_GB + h:SM_GB + h + 1]
        decay = jnp.where(incl, jnp.exp(jnp.where(incl, g_col - g_row, 0.0)), 0.0)
        act = acts[bb]
        qh = _l2norm(_head(act[rs, 0:256], h)) * SCALE
        kh = _l2norm(_head(act[rs, 256:512], h))
        vh = _head(act[rs, 512:768], h)
        eg = jnp.exp(g_col)
        pre.append(dict(g_col=g_col, beta=beta, decay=decay, qh=qh, kh=kh, kb=_bf(kh), eg=eg,
                        rhs=jnp.concatenate([vh * beta, kh * (beta * eg)], axis=1)))
    a_l = [jnp.where(strict, d["beta"] * _dot_nt(d["kb"], d["kb"]) * d["decay"], 0.0) for d in pre]
    p_l = [eye - a for a in a_l]
    pw_s = [_split_bf16(a) for a in a_l]
    lvl = 2
    while lvl < chunk:
        pw_s = [_split_bf16(_dot3(x, x)) for x in pw_s]
        p_l = [p + _dot3(_split_bf16(p), x) for p, x in zip(p_l, pw_s)]
        lvl *= 2
    uw_l = [_dot3(_split_bf16(p), _split_bf16(d["rhs"])) for p, d in zip(p_l, pre)]
    qk_l = [_dot_nt(_bf(d["qh"]), d["kb"]) * d["decay"] for d in pre]
    per_chunk = nb * N_HEADS
    st_l = [s_ref[bb, h] for bb, cc, h in probs[:per_chunk]]
    o_all = []
    for cc in range(nc):
        sl = slice(cc * per_chunk, (cc + 1) * per_chunk)
        stb_l = [_bf(st) for st in st_l]
        delta_l = [uw[:, 0:HEAD_DIM] - _dot(_bf(uw[:, HEAD_DIM:]), stb) for uw, stb in zip(uw_l[sl], stb_l)]
        o_all += [_dot(_bf(qk), _bf(delta)) + _dot(_bf(d["qh"] * d["eg"]), stb)
                  for qk, delta, d, stb in zip(qk_l[sl], delta_l, pre[sl], stb_l)]
        new_st = []
        for d, st, delta in zip(pre[sl], st_l, delta_l):
            g_end = d["g_col"][chunk - 1:chunk, :]
            w_end = jnp.exp(g_end - d["g_col"])
            new_st.append(jnp.exp(g_end) * st + _dot_tn(_bf(d["kh"] * w_end), _bf(delta)))
        st_l = new_st
    for (bb, cc, h), st in zip(probs[:per_chunk], st_l):
        s_ref[bb, h] = st
    for bb in range(nb):
        for cc in range(nc):
            outs = [_head_rms(o_all[cc * per_chunk + bb * N_HEADS + h]) for h in range(N_HEADS)]
            rs = slice(cc * chunk, (cc + 1) * chunk)
            y_ref[bb, rs, :] = jnp.concatenate(outs, axis=1) * nw_ref[...] * _silu(z_ref[bb, rs, 768:1024])


def _gdn_seq(z_gdn, small, conv_w, dtb_vec, alog_vec, nw, chunk, nb, nc):
    b, l, _ = z_gdn.shape
    rows = nc * chunk
    return pl.pallas_call(
        functools.partial(_gdn_body, chunk=chunk),
        grid=(b // nb, l // rows),
        in_specs=[pl.BlockSpec((nb, rows, 1024), lambda i, j: (i, j, 0)),
                  pl.BlockSpec((nb, rows, LANES), lambda i, j: (i, j, 0)),
                  pl.BlockSpec((GDN_CONV, 3 * D_HEADS), lambda i, j: (0, 0)),
                  pl.BlockSpec((1, LANES), lambda i, j: (0, 0)),
                  pl.BlockSpec((1, LANES), lambda i, j: (0, 0)),
                  pl.BlockSpec((1, D_HEADS), lambda i, j: (0, 0))],
        out_specs=[pl.BlockSpec((nb, rows, D_HEADS), lambda i, j: (i, j, 0)),
                   pl.BlockSpec((nb, N_HEADS, HEAD_DIM, HEAD_DIM), lambda i, j: (i, 0, 0, 0))],
        out_shape=[jax.ShapeDtypeStruct((b, l, D_HEADS), F32),
                   jax.ShapeDtypeStruct((b, N_HEADS, HEAD_DIM, HEAD_DIM), F32)],
        scratch_shapes=[pltpu.VMEM((nb, SUBLANES + rows, 3 * D_HEADS), F32)],
        compiler_params=_cparams("parallel", "arbitrary"),
        name="gdn_seq",
    )(z_gdn, small, conv_w, dtb_vec, alog_vec, nw.reshape(1, -1))


STEP_SEQS = 4


def _row_to_col(row):
    n = row.shape[1]
    eye = lax.broadcasted_iota(jnp.int32, (n, n), 0) == lax.broadcasted_iota(jnp.int32, (n, n), 1)
    return jnp.sum(jnp.where(eye, jnp.broadcast_to(row, (n, n)), 0.0), axis=1, keepdims=True)


def _step_body(zr_ref, zm_ref, zg_ref, sm_ref, cos_ref, sin_ref, mlb_ref, dtb_ref, alog_ref, cw_ref,
               nwr_ref, nwm_ref, nwg_ref, sr_ref, mc_ref, mn_ref, mm_ref, gs_ref, gc_ref,
               yr_ref, ym_ref, yg_ref, sr_o, mc_o, mn_o, mm_o, gs_o, gc_o):
    nb = zr_ref.shape[0]
    d = HEAD_DIM
    eye = lax.broadcasted_iota(jnp.int32, (d, d), 0) == lax.broadcasted_iota(jnp.int32, (d, d), 1)
    lane = lax.broadcasted_iota(jnp.int32, (1, LANES), 1)
    cos, sin = cos_ref[...], sin_ref[...]
    rowsum = lambda x: jnp.sum(x, axis=1, keepdims=True)

    def vec_mat(row, mat):
        return _dot(_pad_rows(row), mat, precision=HI)[0:1, :]

    def outer(a_row, b_row):
        diag = jnp.where(eye, jnp.broadcast_to(a_row, (d, d)), 0.0)
        return _dot(diag, jnp.broadcast_to(b_row, (d, d)), precision=HI)

    ret, ml, gdn, seqs = [], [], [], []
    for bb in range(nb):
        sm, zr, zm, zg = sm_ref[bb], zr_ref[bb], zm_ref[bb], zg_ref[bb]
        rq = _rope256(zr[:, 0:256], cos, sin)
        rk = _rope256(zr[:, 256:512], cos, sin) * SCALE
        pre = sm + mlb_ref[...]
        logf = jnp.minimum(pre, 0.0) - jnp.log(1.0 + jnp.exp(-jnp.abs(pre)))
        m_all = mm_ref[bb]
        x = zg[:, 0:768]
        buf = gc_ref[bb]
        conv = x * cw_ref[GDN_CONV - 1:GDN_CONV, :]
        for j in range(GDN_CONV - 1):
            conv = conv + buf[j:j + 1, :] * cw_ref[j:j + 1, :]
        act = _silu(conv)
        g_all = -jnp.exp(alog_ref[...]) * _softplus(sm + dtb_ref[...])
        beta_all = _sigmoid(sm)
        seqs.append(dict(zr=zr, zm=zm, zg=zg, x=x, buf=buf))
        for h in range(N_HEADS):
            ret.append(dict(q=_head(rq, h), k=_head(rk, h), v=_head(zr[:, 512:768], h), st=sr_ref[bb, h],
                            gam=math.exp(_ret_log_decay(h))))
            i_g = pre[:, SM_MI + h:SM_MI + h + 1]
            inter = logf[:, SM_MF + h:SM_MF + h + 1] + m_all[:, h:h + 1]
            m_t = jnp.maximum(inter, i_g)
            ml.append(dict(q=_head(zm[:, 0:256], h), k=_head(zm[:, 256:512], h) * SCALE, v=_head(zm[:, 512:768], h),
                           cm=mc_ref[bb, h], nv=mn_ref[bb, h:h + 1, :], m_t=m_t, w=jnp.exp(i_g - m_t),
                           g_in=jnp.exp(inter - m_t)))
            gdn.append(dict(q=_head(act[:, 0:256], h), k=_head(act[:, 256:512], h), v=_head(act[:, 512:768], h),
                            st=gs_ref[bb, h], eg=jnp.exp(g_all[:, SM_GA + h:SM_GA + h + 1]),
                            beta=beta_all[:, SM_GB + h:SM_GB + h + 1]))
    for p in gdn:
        p["q"], p["k"] = _l2norm(p["q"]) * SCALE, _l2norm(p["k"])
    for p in ret + ml + gdn:
        p["qk"] = rowsum(p["q"] * p["k"])
    for p in ml:
        p["qn"] = rowsum(p["q"] * p["nv"])
    for p in ret:
        p["qs"] = vec_mat(p["q"] * p["gam"], p["st"])
    for p in ml:
        p["qc"] = vec_mat(p["q"], p["cm"])
    for p in gdn:
        p["ks"] = vec_mat(p["k"] * (p["beta"] * p["eg"]), p["st"])
        p["qs"] = vec_mat(p["q"] * p["eg"], p["st"])
    for p in ret:
        p["o"] = p["qk"] * p["v"] + p["qs"]
    for p in ml:
        qk = p["qk"] * p["w"]
        den = qk + p["g_in"] * p["qn"]
        p["o"] = (qk * p["v"] + p["g_in"] * p["qc"]) / jnp.maximum(jnp.abs(den), jnp.exp(-p["m_t"]))
    for p in gdn:
        p["delta"] = p["v"] * p["beta"] - p["ks"]
        p["o"] = p["qk"] * p["delta"] + p["qs"]
    for p in ret:
        p["new"] = p["st"] * p["gam"] + outer(p["k"], p["v"])
    for p in ml:
        p["new"] = p["g_in"] * p["cm"] + outer(p["k"] * p["w"], p["v"])
    for p in gdn:
        p["new"] = p["eg"] * p["st"] + outer(p["k"], p["delta"])
    for bb in range(nb):
        s = seqs[bb]
        ps = slice(bb * N_HEADS, (bb + 1) * N_HEADS)
        m_out = jnp.zeros((1, LANES), F32)
        for h, (pr, pm, pg) in enumerate(zip(ret[ps], ml[ps], gdn[ps])):
            sr_o[bb, h] = pr["new"]
            mc_o[bb, h] = pm["new"]
            mn_o[bb, h:h + 1, :] = pm["g_in"] * pm["nv"] + pm["w"] * pm["k"]
            gs_o[bb, h] = pg["new"]
            m_out = jnp.where(lane == h, pm["m_t"], m_out)
        mm_o[bb] = m_out
        gc_o[bb, 0:GDN_CONV - 2, :] = s["buf"][1:GDN_CONV - 1, :]
        gc_o[bb, GDN_CONV - 2:GDN_CONV - 1, :] = s["x"]
        yr_ref[bb] = (jnp.concatenate([_head_rms(p["o"]) for p in ret[ps]], axis=1) * nwr_ref[...]
                      * _silu(s["zr"][:, 768:1024]))
        og = s["zm"][:, 768:1024]
        ym_ref[bb] = jnp.concatenate([_head_rms(_sigmoid(_head(og, h)) * p["o"])
                                      for h, p in enumerate(ml[ps])], axis=1) * nwm_ref[...]
        yg_ref[bb] = (jnp.concatenate([_head_rms(p["o"]) for p in gdn[ps]], axis=1) * nwg_ref[...]
                      * _silu(s["zg"][:, 768:1024]))


def _mixers_step(z_ret, z_ml, z_gdn, small, cos, sin, mlb, dtb, alog, conv_w, nwr, nwm, nwg,
                 s_ret, ml_c, ml_n, ml_m, gdn_s, gdn_conv):
    b = z_ret.shape[0]
    nb = STEP_SEQS if b % STEP_SEQS == 0 else 1
    row = lambda n: pl.BlockSpec((nb, 1, n), lambda i: (i, 0, 0))
    vec = lambda n: pl.BlockSpec((1, n), lambda i: (0, 0))
    mat = pl.BlockSpec((nb, N_HEADS, HEAD_DIM, HEAD_DIM), lambda i: (i, 0, 0, 0))
    nsp = pl.BlockSpec((nb, N_HEADS, HEAD_DIM), lambda i: (i, 0, 0))
    csp = pl.BlockSpec((nb, GDN_CONV - 1, 3 * D_HEADS), lambda i: (i, 0, 0))
    mshape = jax.ShapeDtypeStruct((b, N_HEADS, HEAD_DIM, HEAD_DIM), F32)
    yshape = jax.ShapeDtypeStruct((b, 1, D_HEADS), F32)
    return pl.pallas_call(
        _step_body,
        grid=(b // nb,),
        in_specs=[row(1024), row(1024), row(1024), row(LANES), vec(LANES), vec(LANES), vec(LANES), vec(LANES), vec(LANES),
                  pl.BlockSpec((GDN_CONV, 3 * D_HEADS), lambda i: (0, 0)), vec(D_HEADS), vec(D_HEADS), vec(D_HEADS),
                  mat, mat, nsp, row(LANES), mat, csp],
        out_specs=[row(D_HEADS), row(D_HEADS), row(D_HEADS), mat, mat, nsp, row(LANES), mat, csp],
        out_shape=[yshape, yshape, yshape, mshape, mshape,
                   jax.ShapeDtypeStruct((b, N_HEADS, HEAD_DIM), F32), jax.ShapeDtypeStruct((b, 1, LANES), F32),
                   mshape, jax.ShapeDtypeStruct((b, GDN_CONV - 1, 3 * D_HEADS), F32)],
        compiler_params=_cparams("parallel"),
        name="mixers_step",
    )(z_ret, z_ml, z_gdn, small, cos, sin, mlb, dtb, alog, conv_w,
      nwr.reshape(1, -1), nwm.reshape(1, -1), nwg.reshape(1, -1), s_ret, ml_c, ml_n, ml_m, gdn_s, gdn_conv)


def _t5_bucket_np(n):
    exact = T5_BUCKETS // 2
    n = np.maximum(np.asarray(n, np.int64), 0)
    x = np.maximum(n, 1).astype(np.float32) / np.float32(exact)
    large = exact + (np.log(x) / np.float32(math.log(T5_MAX_DIST / exact)) * np.float32(T5_BUCKETS - exact)).astype(np.int32)
    return np.where(n < exact, n, np.minimum(large, T5_BUCKETS - 1)).astype(np.int32)


def _bucket_thresholds():
    b = _t5_bucket_np(np.arange(4 * T5_MAX_DIST))
    return [int(np.argmax(b >= k)) for k in range(T5_BUCKETS // 2 + 1, T5_BUCKETS)]


def _bias_table_body(rbt_ref, oh_ref, o_ref):
    o_ref[...] = _dot(rbt_ref[...], oh_ref[...].astype(F32), precision=HI)


def _bias_tables(rel_bias, dists):
    n = len(dists)
    tn = 4096
    npad = -(-n // tn) * tn
    onehot = np.zeros((T5_BUCKETS, npad), np.float32)
    onehot[_t5_bucket_np(dists), np.arange(n)] = 1.0
    rbt = jnp.pad(rel_bias.T, ((0, SUBLANES - N_HEADS), (0, 0)))
    out = pl.pallas_call(
        _bias_table_body,
        grid=(npad // tn,),
        in_specs=[pl.BlockSpec((SUBLANES, T5_BUCKETS), lambda i: (0, 0)),
                  pl.BlockSpec((T5_BUCKETS, tn), lambda i: (0, i))],
        out_specs=pl.BlockSpec((SUBLANES, tn), lambda i: (0, i)),
        out_shape=jax.ShapeDtypeStruct((SUBLANES, npad), F32),
        name="bias_tables",
    )(rbt, jnp.asarray(onehot, BF16))
    return out[:, :n]


def _prep_cmp_w(cmp_w, cmp_pe):
    eye2 = jnp.eye(2, dtype=F32)

    def half(w):
        full = jnp.einsum('cjde,cf->jcdfe', w, eye2)
        return full.reshape(CMP_STRIDE * 2 * HEAD_DIM, 2 * HEAD_DIM)

    wab = jnp.concatenate([half(cmp_w[:, :CMP_STRIDE]), half(cmp_w[:, CMP_STRIDE:])], axis=1)
    pe = jnp.concatenate([cmp_pe[:CMP_STRIDE].reshape(1, -1), cmp_pe[CMP_STRIDE:].reshape(1, -1)], axis=1)
    return wab, jnp.pad(pe, ((0, SUBLANES - 1), (0, 0)))


def _pe_term(pe_ref, wab_ref):
    kin = CMP_STRIDE * 2 * HEAD_DIM
    t = (_dot(pe_ref[:, 0:kin], wab_ref[:, 0:LANES].astype(F32), precision=HI)
         + _dot(pe_ref[:, kin:2 * kin], wab_ref[:, LANES:2 * LANES].astype(F32), precision=HI))
    return t[0:1, :]


def _combine_cmp(parts, pe_term):
    rows = parts.shape[0]
    nxt = pltpu.roll(parts[:, LANES:2 * LANES], rows - 1, 0)
    r = lax.broadcasted_iota(jnp.int32, (rows, LANES), 0)
    return jnp.where(r < rows - 1, parts[:, 0:LANES] + nxt + pe_term, 0.0)


def _compress_seq_body(kv_ref, wab_ref, pe_ref, o_ref):
    n16 = o_ref.shape[1]
    parts = jnp.zeros((n16, 2 * LANES), F32)
    for j in range(CMP_STRIDE):
        xj = kv_ref[0, pl.ds(j, n16, stride=CMP_STRIDE), :]
        parts = parts + _dot(_bf(xj), _bf(wab_ref[j * LANES:(j + 1) * LANES, :]))
    o_ref[0] = _combine_cmp(parts, _pe_term(pe_ref, wab_ref))


def _compress_seq(nkv, wab, pe):
    b, l, _ = nkv.shape
    n16 = l // CMP_STRIDE
    return pl.pallas_call(
        _compress_seq_body,
        grid=(b,),
        in_specs=[pl.BlockSpec((1, l, LANES), lambda i: (i, 0, 0)),
                  pl.BlockSpec(wab.shape, lambda i: (0, 0)),
                  pl.BlockSpec(pe.shape, lambda i: (0, 0))],
        out_specs=pl.BlockSpec((1, n16, LANES), lambda i: (i, 0, 0)),
        out_shape=jax.ShapeDtypeStruct((b, n16, LANES), F32),
        compiler_params=_cparams("parallel"),
        name="nsa_compress_seq",
    )(nkv, wab, pe)


def _softmax_rows(s, valid):
    s = jnp.where(valid, s, NEG_INF)
    m = jnp.max(s, axis=1, keepdims=True)
    m = jnp.where(m == NEG_INF, 0.0, m)
    p = jnp.exp(s - m)
    den = jnp.sum(p, axis=1, keepdims=True)
    return p / jnp.where(den > 0, den, 1.0)


def _tile_mask(ok):
    return jnp.concatenate([jnp.where(ok, 1.0, 0.0)] * N_HEADS, axis=0) > 0.5


def _stack_heads(x):
    return jnp.concatenate([_head(x, h) for h in range(N_HEADS)], axis=0)


def _toeplitz(tab_ref, h, r0, rows, ncol):
    nrow = (rows + ncol) // LANES
    strip = tab_ref[h, pl.ds(r0, nrow), :]
    flat = jnp.concatenate([strip[r:r + 1, :] for r in range(nrow)], axis=1)
    rolled = pltpu.roll(jnp.broadcast_to(flat, (rows, nrow * LANES)), 0, 1, stride=1, stride_axis=0)
    return rolled[:, rows:]


SLC_KC = 512


MASK_BIG = 1e30


def _top_blocks_t(score, k):
    row = lax.broadcasted_iota(jnp.int32, score.shape, 0)
    n = score.shape[0]
    sel = jnp.zeros(score.shape, jnp.bool_)
    for _ in range(k):
        m = jnp.max(score, axis=0, keepdims=True)
        idx = jnp.min(jnp.where(score == m, row, n), axis=0, keepdims=True)
        pick = (row == idx) & (m > NEG_INF)
        sel = sel | pick
        score = jnp.where(pick, NEG_INF, score)
    return sel


def _bucket_index(dist):
    n = jnp.maximum(dist, 0)
    exact = T5_BUCKETS // 2
    big = jnp.full(dist.shape, exact, jnp.int32)
    for thr in _bucket_thresholds():
        big = big + (n >= thr).astype(jnp.int32)
    return jnp.where(n < exact, n, big)


def _gather_bias(rbt_ref, h, bucket):
    rows, width = bucket.shape
    piece = min(width, LANES)
    tab = jnp.broadcast_to(rbt_ref[h:h + 1, 0:piece], (rows, piece))
    parts = [jnp.take_along_axis(tab, bucket[:, s:s + piece], axis=1) for s in range(0, width, piece)]
    return parts[0] if len(parts) == 1 else jnp.concatenate(parts, axis=1)


def _tile_rows(x, n=N_HEADS):
    return jnp.concatenate([x] * n, axis=0)


def _near_chunks():
    return (_bucket_thresholds()[-1] + SLC_KC - 2) // SLC_KC + 1


def _bias_tile(tab_ref, base, ncol, p0, lo, hi):
    r0 = (p0 - base - Q_BLOCK) // LANES
    bias = jnp.concatenate([_toeplitz(tab_ref, h, r0, Q_BLOCK, ncol) for h in range(N_HEADS)], axis=0)
    dist = (base + lax.broadcasted_iota(jnp.int32, (Q_BLOCK, ncol), 0)
            - lax.broadcasted_iota(jnp.int32, (Q_BLOCK, ncol), 1))
    return bias + _tile_rows(jnp.where((dist >= lo) & (dist < hi), 0.0, -MASK_BIG))


def _nsa_seq_body(rb_ref, q_ref, sm_ref, kx_ref, vs_ref, kvw_ref, kvc_ref, tab_ref, rbt_ref, ovlt_ref, y_ref,
                  stile_ref, wtile_ref, *, seq, p0):
    qb = Q_BLOCK
    rows = N_HEADS * qb
    n16 = seq // CMP_STRIDE
    nblk = seq // SLC_BLOCK
    bi = pl.program_id(1)
    q0 = bi * qb

    @pl.when((pl.program_id(0) == 0) & (bi == 0))
    def _():
        per_chunk = SLC_KC // qb

        def slc_tile(t, carry):
            base = (t // _near_chunks()) * qb + (t % _near_chunks()) * SLC_KC
            stile_ref[t] = _bias_tile(tab_ref, base, SLC_KC, p0, 0, seq + SLC_KC)
            return carry

        def win_tile(t, carry):
            wtile_ref[t] = _bias_tile(tab_ref, t * qb, WINDOW + qb, p0, 0, WINDOW)
            return carry

        lax.fori_loop(0, per_chunk * _near_chunks(), slc_tile, 0)
        lax.fori_loop(0, WINDOW // qb + 1, win_tile, 0)
    qs = _stack_heads(q_ref[0]) * SCALE
    qsb = _bf(qs)
    qpos = q0 + lax.broadcasted_iota(jnp.int32, (qb, 1), 0)

    kvc = kvc_ref[0]
    s_c = _dot_nt(qsb, _bf(kvc[:, 0:HEAD_DIM]))
    band = WINDOW + qb
    sw = pl.multiple_of(jnp.maximum(q0 - WINDOW, 0), qb)
    kw = kvw_ref[0, pl.ds(sw, band), 0:HEAD_DIM]
    vw = kvw_ref[0, pl.ds(sw, band), HEAD_DIM:2 * HEAD_DIM]
    s_w = _dot_nt(qsb, kw)

    cmp_end = lax.broadcasted_iota(jnp.int32, (qb, n16), 1) * CMP_STRIDE + (CMP_LEN - 1)
    dist_c = qpos - cmp_end
    bucket_c = _bucket_index(dist_c)
    bias_c = jnp.concatenate([_gather_bias(rbt_ref, h, bucket_c) for h in range(N_HEADS)], axis=0)
    p_c = _softmax_rows(s_c + bias_c, _tile_mask(dist_c >= 0))
    p_cb = _bf(p_c)
    o_c = _dot(p_cb, _bf(kvc[:, HEAD_DIM:2 * HEAD_DIM]))
    p_r = p_cb.astype(F32)
    p_sum = p_r[0:qb] + p_r[qb:2 * qb] + p_r[2 * qb:3 * qb] + p_r[3 * qb:4 * qb]
    imp_t = _dot_nt(ovlt_ref[...], p_sum, precision=HI)

    base_w = q0 - sw
    sb_w = s_w + wtile_ref[base_w // qb]
    e_w = jnp.exp(sb_w - jnp.max(sb_w, axis=1, keepdims=True))
    o_w = _dot(_bf(e_w / jnp.sum(e_w, axis=1, keepdims=True)), vw)

    qlane = q0 + lax.broadcasted_iota(jnp.int32, (1, qb), 1)
    cur = qlane // SLC_BLOCK
    rowb = lax.broadcasted_iota(jnp.int32, (nblk, qb), 0)
    forced = (rowb == 0) | (rowb == cur) | (rowb == cur - 1)
    score = jnp.where(forced | (rowb > cur), NEG_INF, imp_t)
    sel = jnp.where(forced | _top_blocks_t(score, SLC_TOPK - 3), 1.0, 0.0).T
    negm = _bf(_tile_rows((sel - 1.0) * MASK_BIG))
    q_ext = jnp.concatenate([qsb, negm], axis=1)

    thr_far = _bucket_thresholds()[-1] + SLC_KC - 1
    n_chunks = q0 // SLC_KC + 1
    n_far = jnp.maximum(q0 - thr_far + SLC_KC, 0) // SLC_KC
    n_near = _near_chunks()

    far_bias = jnp.concatenate([jnp.full((qb, 1), rb_ref[T5_BUCKETS - 1, h], F32) for h in range(N_HEADS)], axis=0)

    def scores(c):
        k0 = pl.multiple_of(c * SLC_KC, SLC_KC)
        kx = kx_ref[0, pl.ds(k0, SLC_KC), :]
        return q0 - k0, _dot_nt(q_ext, kx), vs_ref[0, pl.ds(k0, SLC_KC), :]

    def online(carry, s, vs):
        m, l, acc = carry
        m_new = jnp.maximum(m, jnp.max(s, axis=1, keepdims=True))
        alpha = jnp.exp(m - m_new)
        p = jnp.exp(s - m_new)
        return m_new, alpha * l + jnp.sum(p, axis=1, keepdims=True), alpha * acc + _dot(_bf(p), vs)

    def far_step(c, carry):
        _, s, vs = scores(c)
        return online(carry, s + far_bias, vs)

    def near_step(c, carry):
        base, s, vs = scores(c)
        tile = ((q0 % SLC_KC) // qb) * n_near + base // SLC_KC
        return online(carry, s + stile_ref[tile], vs)

    init = (jnp.full((rows, 1), -MASK_BIG / 10, F32), jnp.zeros((rows, 1), F32), jnp.zeros((rows, HEAD_DIM), F32))
    carry = lax.fori_loop(0, n_far, far_step, init)
    _, l_s, acc_s = lax.fori_loop(n_far, n_chunks, near_step, carry)
    o_s = acc_s / jnp.where(l_s > 0, l_s, 1.0)

    gates = _sigmoid(sm_ref[0])
    outs = []
    for h in range(N_HEADS):
        rs = slice(h * qb, (h + 1) * qb)
        g = lambda t: gates[:, SM_NG + 3 * h + t:SM_NG + 3 * h + t + 1]
        outs.append(g(0) * o_c[rs] + g(1) * o_s[rs] + g(2) * o_w[rs])
    y_ref[0] = jnp.concatenate(outs, axis=1)


def _nsa_seq(rel_bias, nq, small, kx, vs, kvw, kvc, tab, seq_p0):
    b, l, _ = nq.shape
    n16 = l // CMP_STRIDE
    nblk = l // SLC_BLOCK
    cmp_start = np.arange(n16)[None, :] * CMP_STRIDE
    slc_start = np.arange(nblk)[:, None] * SLC_BLOCK
    ovlt = ((cmp_start < slc_start + SLC_BLOCK) & (cmp_start + CMP_LEN > slc_start)
            & (np.arange(n16)[None, :] < n16 - 1)).astype(np.float32)
    rbt = jnp.pad(rel_bias.T, ((0, SUBLANES - N_HEADS), (0, LANES - T5_BUCKETS)))
    nrows = tab.shape[1]
    const = lambda shape: pl.BlockSpec(shape, lambda i, j: (0,) * len(shape))
    seq_spec = lambda n: pl.BlockSpec((1, l, n), lambda i, j: (i, 0, 0))
    return pl.pallas_call(
        functools.partial(_nsa_seq_body, seq=l, p0=seq_p0),
        grid=(b, l // Q_BLOCK),
        in_specs=[pl.BlockSpec(memory_space=pltpu.SMEM),
                  pl.BlockSpec((1, Q_BLOCK, D_HEADS), lambda i, j: (i, j, 0)),
                  pl.BlockSpec((1, Q_BLOCK, LANES), lambda i, j: (i, j, 0)),
                  seq_spec(HEAD_DIM + nblk), seq_spec(HEAD_DIM), seq_spec(2 * HEAD_DIM),
                  pl.BlockSpec((1, n16, LANES), lambda i, j: (i, 0, 0)),
                  const((SUBLANES, nrows, LANES)), const((SUBLANES, LANES)), const((nblk, n16))],
        out_specs=pl.BlockSpec((1, Q_BLOCK, D_HEADS), lambda i, j: (i, j, 0)),
        out_shape=jax.ShapeDtypeStruct((b, l, D_HEADS), F32),
        scratch_shapes=[pltpu.VMEM(((SLC_KC // Q_BLOCK) * _near_chunks(), N_HEADS * Q_BLOCK, SLC_KC), F32),
                        pltpu.VMEM((WINDOW // Q_BLOCK + 1, N_HEADS * Q_BLOCK, WINDOW + Q_BLOCK), F32)],
        compiler_params=_cparams("arbitrary", "arbitrary"),
        name="nsa_seq",
    )(rel_bias, nq, small, kx, vs, kvw, kvc, tab, rbt, jnp.asarray(ovlt))


def _nsa_tab_seq_dists(seq):
    max_base = max(seq, SLC_KC + (_near_chunks() - 1) * SLC_KC)
    p0 = max_base + Q_BLOCK
    length = p0 + WINDOW + Q_BLOCK + LANES
    return p0, np.maximum(p0 - np.arange(length), 0)


PAGE_GROUP = 16
PAGES_PER_STEP = 64


def _pad_rows(x, rows=SUBLANES):
    return jnp.concatenate([x, jnp.zeros((rows - x.shape[0], x.shape[1]), x.dtype)], axis=0)


def _nsa_select_body(pt_ref, *refs, past, n_steps):
    del pt_ref
    pages = refs[:PAGES_PER_STEP]
    wab_ref, pe_ref, q_ref, tabc_ref, ovlt_ref, oc_ref, top_ref, parts_ref, rows_ref = refs[PAGES_PER_STEP:]
    step = pl.program_id(1)
    blocks_per_page = PAGE_SIZE // CMP_STRIDE
    group_rows = PAGE_GROUP * blocks_per_page
    n_groups = PAGES_PER_STEP // PAGE_GROUP
    for p, pg in enumerate(pages):
        g, pp = divmod(p, PAGE_GROUP)
        rows_ref[g, pp * PAGE_SIZE:(pp + 1) * PAGE_SIZE, :] = pg[...].reshape(2 * HEAD_DIM, PAGE_SIZE).T
    for g in range(n_groups):
        acc = jnp.zeros((group_rows, 2 * LANES), F32)
        for j in range(CMP_STRIDE):
            xj = rows_ref[g, pl.ds(j, group_rows, stride=CMP_STRIDE), :]
            acc = acc + _dot(_bf(xj), wab_ref[j * LANES:(j + 1) * LANES, :])
        row0 = pl.multiple_of((step * n_groups + g) * group_rows, group_rows)
        parts_ref[pl.ds(row0, group_rows), :] = acc

    @pl.when(step == n_steps - 1)
    def _():
        n16 = past // CMP_STRIDE
        n_cmp = n16 - 1
        cur = past // SLC_BLOCK
        pe_term = _pe_term(pe_ref, wab_ref)
        kvc = _combine_cmp(parts_ref[...], pe_term)
        qs = _pad_rows(_stack_heads(q_ref[0]) * SCALE)
        s_c = _dot_nt(_bf(qs), _bf(kvc[:, 0:HEAD_DIM])) + tabc_ref[...]
        lane = lax.broadcasted_iota(jnp.int32, (SUBLANES, n16), 1)
        p_c = _softmax_rows(s_c, lane < n_cmp)
        p_cb = _bf(p_c)
        o_c = _dot(p_cb, _bf(kvc[:, HEAD_DIM:2 * HEAD_DIM]))
        oc_ref[0] = jnp.concatenate([o_c[h:h + 1, :] for h in range(N_HEADS)], axis=1)
        imp_h = _dot_nt(ovlt_ref[...], p_cb)
        imp = imp_h[:, 0:1] + imp_h[:, 1:2] + imp_h[:, 2:3] + imp_h[:, 3:4]
        nrow = imp.shape[0]
        rowb = lax.broadcasted_iota(jnp.int32, (nrow, 1), 0)
        forced_ids = (0, cur - 1, cur)
        forced = (rowb == 0) | (rowb == cur) | (rowb == cur - 1)
        score = jnp.where(forced | (rowb > cur), NEG_INF, imp)
        out_lane = lax.broadcasted_iota(jnp.int32, (1, LANES), 1)
        top = jnp.zeros((1, LANES), jnp.int32)
        for r, blk in enumerate(forced_ids):
            top = jnp.where(out_lane == r, blk, top)
        for r in range(len(forced_ids), SLC_TOPK):
            m = jnp.max(score, axis=0, keepdims=True)
            idx = jnp.min(jnp.where(score == m, rowb, nrow), axis=0, keepdims=True)
            top = jnp.where(out_lane == r, idx, top)
            score = jnp.where(rowb == idx, NEG_INF, score)
        top_ref[0] = top


def _nsa_select(layer, page_table, cache_t, wab_bf, pe, nq, tabc, past):
    b, n_pages = page_table.shape
    n_steps = n_pages // PAGES_PER_STEP
    n16 = past // CMP_STRIDE
    n_slc = past // SLC_BLOCK + 1
    nrow = -(-n_slc // SUBLANES) * SUBLANES
    cmp_start = np.arange(n16)[None, :] * CMP_STRIDE
    slc_start = np.arange(nrow)[:, None] * SLC_BLOCK
    ovlt = ((cmp_start < slc_start + SLC_BLOCK) & (cmp_start + CMP_LEN > slc_start)
            & (np.arange(n16)[None, :] < n16 - 1) & (np.arange(nrow)[:, None] < n_slc)).astype(np.float32)

    n_pool = cache_t.shape[1]

    def page_spec(p):
        def imap(i, s, pt):
            pg = pt[jnp.minimum(i, b - 1), jnp.minimum(s, n_steps - 1) * PAGES_PER_STEP + p]
            return (layer, jnp.minimum(pg, n_pool - 1), 0, 0, 0)
        return pl.BlockSpec((None, None, 2, HEAD_DIM, PAGE_SIZE), imap)

    const = lambda shape: pl.BlockSpec(shape, lambda i, s, pt: (0,) * len(shape))
    grid_spec = pltpu.PrefetchScalarGridSpec(
        num_scalar_prefetch=1,
        grid=(b, n_steps),
        in_specs=[page_spec(p) for p in range(PAGES_PER_STEP)]
        + [const(wab_bf.shape), const(pe.shape), pl.BlockSpec((1, 1, D_HEADS), lambda i, s, pt: (i, 0, 0)),
           const(tabc.shape), const(ovlt.shape)],
        out_specs=[pl.BlockSpec((1, 1, D_HEADS), lambda i, s, pt: (i, 0, 0)),
                   pl.BlockSpec((1, 1, LANES), lambda i, s, pt: (i, 0, 0))],
        scratch_shapes=[pltpu.VMEM((n16, 2 * LANES), F32),
                        pltpu.VMEM((PAGES_PER_STEP // PAGE_GROUP, PAGE_GROUP * PAGE_SIZE, 2 * HEAD_DIM), F32)],
    )
    return pl.pallas_call(
        functools.partial(_nsa_select_body, past=past, n_steps=n_steps),
        grid_spec=grid_spec,
        out_shape=[jax.ShapeDtypeStruct((b, 1, D_HEADS), F32), jax.ShapeDtypeStruct((b, 1, LANES), jnp.int32)],
        compiler_params=_cparams("parallel", "arbitrary"),
        name="nsa_select",
    )(page_table, *([cache_t] * PAGES_PER_STEP), wab_bf, pe, nq, tabc, jnp.asarray(ovlt, BF16))


def _nsa_attend_body(pt_ref, top_ref, *refs, past):
    del pt_ref
    pages = refs[:SLC_TOPK]
    (q_ref, sm_ref, new_ref, oc_ref, swa_ref, tabs_ref, tabw_ref, y_ref, swa_out) = refs[SLC_TOPK:]
    b = pl.program_id(0)
    new_blk = past // SLC_BLOCK
    blocks_per_page = PAGE_SIZE // SLC_BLOCK
    qs = _bf(_pad_rows(_stack_heads(q_ref[0]) * SCALE))
    new = new_ref[0]
    lane_t = lax.broadcasted_iota(jnp.int32, (HEAD_DIM, PAGE_SIZE), 1)
    lane = lax.broadcasted_iota(jnp.int32, (SUBLANES, PAGE_SIZE), 1)
    new_kt = jnp.where(lane_t == 0, _row_to_col(new[:, 2 * HEAD_DIM:3 * HEAD_DIM]), 0.0)
    new_vt = jnp.where(lane_t == 0, _row_to_col(new[:, 3 * HEAD_DIM:4 * HEAD_DIM]), 0.0)
    scores, oks, vts = [], [], []
    for s in range(SLC_TOPK):
        ti = jnp.clip(top_ref[b, s], 0, new_blk)
        is_new = ti == new_blk
        kt = jnp.where(is_new, new_kt, pages[s][0])
        vts.append(jnp.where(is_new, new_vt, pages[s][1]))
        scores.append(_dot(qs, _bf(kt)) + tabs_ref[ti])
        kpos = ti * SLC_BLOCK + lane % SLC_BLOCK
        ok = (lane // SLC_BLOCK == ti % blocks_per_page) & (kpos <= past)
        oks.append(jnp.where(ok, 1.0, 0.0))
    p_s = _softmax_rows(jnp.concatenate(scores, axis=1), jnp.concatenate(oks, axis=1) > 0.5)
    o_s = None
    for s in range(SLC_TOPK):
        t = _dot_nt(_bf(p_s[:, s * PAGE_SIZE:(s + 1) * PAGE_SIZE]), _bf(vts[s]))
        o_s = t if o_s is None else o_s + t

    wlane = lax.broadcasted_iota(jnp.int32, (HEAD_DIM, WINDOW), 1)
    win = []
    for c in range(2):
        col = _row_to_col(new[:, (4 + c) * HEAD_DIM:(5 + c) * HEAD_DIM])
        win.append(jnp.where(wlane == WINDOW - 1, col, pltpu.roll(swa_ref[c], WINDOW - 1, 1)))
        swa_out[c] = win[c]
    s_w = _dot(qs, _bf(win[0])) + tabw_ref[...]
    p_w = _softmax_rows(s_w, jnp.full(s_w.shape, True))
    o_w = _dot_nt(_bf(p_w), _bf(win[1]))

    gates = _sigmoid(sm_ref[0])
    o_c = oc_ref[0]
    outs = []
    for h in range(N_HEADS):
        g = lambda t: gates[:, SM_NG + 3 * h + t:SM_NG + 3 * h + t + 1]
        outs.append(g(0) * _head(o_c, h) + g(1) * o_s[h:h + 1, :] + g(2) * o_w[h:h + 1, :])
    y_ref[0] = jnp.concatenate(outs, axis=1)


def _nsa_attend(layer, page_table, top, cache_t, nq, small, nkv, o_c, swa_t, tabs, tabw, past):
    b = nq.shape[0]
    last_page = past // PAGE_SIZE - 1
    blocks_per_page = PAGE_SIZE // SLC_BLOCK

    n_pool = cache_t.shape[1]

    def page_spec(s):
        def imap(i, pt, tp):
            ii = jnp.minimum(i, b - 1)
            pg = jnp.clip(tp[ii, s] // blocks_per_page, 0, last_page)
            return (layer, jnp.clip(pt[ii, pg], 0, n_pool - 1), 1, 0, 0)
        return pl.BlockSpec((None, None, 2, HEAD_DIM, PAGE_SIZE), imap)

    row = lambda n: pl.BlockSpec((1, 1, n), lambda i, pt, tp: (i, 0, 0))
    const = lambda shape: pl.BlockSpec(shape, lambda i, pt, tp: (0,) * len(shape))
    grid_spec = pltpu.PrefetchScalarGridSpec(
        num_scalar_prefetch=2,
        grid=(b,),
        in_specs=[page_spec(s) for s in range(SLC_TOPK)]
        + [row(D_HEADS), row(LANES), row(6 * HEAD_DIM), row(D_HEADS),
           pl.BlockSpec((None, None, 2, HEAD_DIM, WINDOW), lambda i, pt, tp: (layer, i, 0, 0, 0)),
           const(tabs.shape), const(tabw.shape)],
        out_specs=[row(D_HEADS), pl.BlockSpec((None, 2, HEAD_DIM, WINDOW), lambda i, pt, tp: (i, 0, 0, 0))],
    )
    return pl.pallas_call(
        functools.partial(_nsa_attend_body, past=past),
        grid_spec=grid_spec,
        out_shape=[jax.ShapeDtypeStruct((b, 1, D_HEADS), F32),
                   jax.ShapeDtypeStruct((b, 2, HEAD_DIM, WINDOW), F32)],
        compiler_params=_cparams("parallel"),
        name="nsa_attend",
    )(page_table, top, *([cache_t] * SLC_TOPK), nq, small, nkv, o_c, swa_t, tabs, tabw)


SEQ_CHUNK = 64
GDN_CHUNKS = 1
SCAN_CHUNK = 256
ROW_TILE = 512
FFN_TILE = 512


def _lane_vec(pairs):
    v = jnp.zeros((1, LANES), F32)
    for off, vals in pairs:
        v = v.at[0, off:off + vals.shape[0]].set(vals.astype(F32))
    return v


def kernel(x_prompt, x_sample, cache_nsa_kv, cache_swa_kv, state_ret, state_mlstm_C, state_mlstm_n, state_mlstm_m, state_gdn, state_gdn_conv, state_ffn_conv, page_table, rel_bias, norm_pre_mix, w_in, mlstm_b_i, mlstm_b_f, gdn_conv_w, gdn_A_log, gdn_dt_bias, nsa_cmp_pe, nsa_cmp_w, ret_norm, mlstm_norm, gdn_norm, w_out, norm_post_mix, norm_pre_ffn, w_ffn_gate, w_ffn_up, ffn_conv_w, w_ffn_down, norm_post_ffn):
    depth = w_in.shape[0]
    bp, seq, _ = x_prompt.shape
    bs, dec_seq, _ = x_sample.shape
    assert dec_seq == 1 and seq % SLC_KC == 0 and seq >= WINDOW + Q_BLOCK
    n_pool = cache_nsa_kv.shape[1]
    past = page_table.shape[1] * PAGE_SIZE
    assert past >= WINDOW and past % SLC_BLOCK == 0 and page_table.shape[1] % PAGES_PER_STEP == 0
    assert past // SLC_BLOCK + 1 >= SLC_TOPK

    p0, d_seq = _nsa_tab_seq_dists(seq)
    n16 = past // CMP_STRIDE
    n_slc = past // SLC_BLOCK + 1
    d_cmp = np.maximum(past - (np.arange(n16) * CMP_STRIDE + CMP_LEN - 1), 0)
    d_slc = np.maximum(past - np.arange(n_slc * SLC_BLOCK), 0)
    d_win = WINDOW - 1 - np.arange(WINDOW)
    tab = _bias_tables(rel_bias, np.concatenate([d_seq, d_cmp, d_slc, d_win]))
    o1, o2, o3 = len(d_seq), len(d_seq) + len(d_cmp), len(d_seq) + len(d_cmp) + len(d_slc)
    tab_seq = tab[:, :o1].reshape(SUBLANES, o1 // LANES, LANES)
    tabc = tab[:, o1:o2]
    tabs = jnp.transpose(tab[:, o2:o3].reshape(SUBLANES, n_slc, SLC_BLOCK), (1, 0, 2))
    tabs = jnp.concatenate([tabs, tabs], axis=2)
    tabw = tab[:, o3:]

    cache_t = jnp.transpose(cache_nsa_kv, (0, 1, 3, 4, 2))
    swa_t = jnp.transpose(cache_swa_kv, (0, 1, 3, 4, 2))
    cos_p, sin_p = _rope_tables(jnp.arange(seq))
    cos_s, sin_s = _rope_tables(jnp.arange(past, past + 1))

    xp = x_prompt
    xs = x_sample.reshape(bs, D_MODEL)
    p_states, s_states = [], []
    for l in range(depth):
        w_cat = _prep_w_in(w_in[l])
        w_out_bf = _bf(w_out[l])
        wg, wu, wd = _bf(w_ffn_gate[l]), _bf(w_ffn_up[l]), _bf(w_ffn_down[l])
        wab, pe = _prep_cmp_w(nsa_cmp_w[l], nsa_cmp_pe[l])
        ml_bias = _lane_vec([(SM_MI, mlstm_b_i[l]), (SM_MF, mlstm_b_f[l])])
        dtb = _lane_vec([(SM_GA, gdn_dt_bias[l])])
        alog = _lane_vec([(SM_GA, gdn_A_log[l])])

        z = _in_proj(xp.reshape(bp * seq, D_MODEL), norm_pre_mix[l], w_cat, ROW_TILE, seq=seq)
        z_ret, z_ml, z_gdn, z_nq, z_nkv, z_sm = [t.reshape(bp, seq, -1) for t in z[:6]]
        nkv_t = z[6].reshape(bp, 6, HEAD_DIM, seq)
        kx, vs, kvw = [t.reshape(bp, seq, -1) for t in z[7:]]
        y_ret, st_ret = _retention_seq(z_ret, cos_p, sin_p, ret_norm[l], SCAN_CHUNK, bp)
        y_ml, st_c, st_n, st_m = _mlstm_seq(z_ml, z_sm, ml_bias, mlstm_norm[l], SCAN_CHUNK, bp)
        y_gdn, st_g = _gdn_seq(z_gdn, z_sm, gdn_conv_w[l], dtb, alog, gdn_norm[l], SEQ_CHUNK, bp, GDN_CHUNKS)
        kvc = _compress_seq(z_nkv, wab, pe)
        y_nsa = _nsa_seq(rel_bias, z_nq, z_sm, kx, vs, kvw, kvc, tab_seq, p0)
        xp, tail = _ffn_seq(xp, (y_ret, y_ml, y_gdn, y_nsa), w_out_bf, norm_post_mix[l], norm_pre_ffn[l],
                            wg, wu, wd, ffn_conv_w[l], norm_post_ffn[l], min(seq, FFN_TILE))
        p_states.append((
            jnp.transpose(nkv_t[:, 0:4], (0, 3, 1, 2)),
            jnp.transpose(nkv_t[:, 4:6, :, seq - WINDOW:], (0, 3, 1, 2)),
            st_ret, st_c, st_n, st_m[:, 0, :N_HEADS], st_g,
            z_gdn[:, seq - (GDN_CONV - 1):, 0:3 * D_HEADS],
            tail[:, SUBLANES - (FFN_CONV - 1):, :]))

        z = _in_proj(xs, norm_pre_mix[l], w_cat, bs)
        z_ret, z_ml, z_gdn, z_nq, z_nkv, z_sm = [t.reshape(bs, 1, -1) for t in z]
        m_pad = jnp.pad(state_mlstm_m[l], ((0, 0), (0, LANES - N_HEADS))).reshape(bs, 1, LANES)
        (y_ret, y_ml, y_gdn, st_ret, st_c, st_n, st_m, st_g, st_gc) = _mixers_step(
            z_ret, z_ml, z_gdn, z_sm, cos_s, sin_s, ml_bias, dtb, alog, gdn_conv_w[l],
            ret_norm[l], mlstm_norm[l], gdn_norm[l],
            state_ret[l], state_mlstm_C[l], state_mlstm_n[l], m_pad, state_gdn[l], state_gdn_conv[l])
        o_c, top = _nsa_select(l, page_table, cache_t, _bf(wab), pe, z_nq, tabc, past)
        y_nsa, swa_new = _nsa_attend(l, page_table, top[:, 0, :SLC_TOPK], cache_t, z_nq, z_sm, z_nkv, o_c,
                                     swa_t, tabs, tabw, past)
        ys = [t.reshape(bs, D_HEADS) for t in (y_ret, y_ml, y_gdn, y_nsa)]
        x1, h2 = _out_proj(xs, ys, w_out_bf, norm_post_mix[l], norm_pre_ffn[l], bs)
        xs, g_new = _ffn_step(x1, h2, state_ffn_conv[l][:, 0], state_ffn_conv[l][:, 1], wg, wu, wd,
                              ffn_conv_w[l], norm_post_ffn[l])
        s_states.append((
            z_nkv[..., 0:4 * HEAD_DIM].reshape(bs, 1, 4, HEAD_DIM),
            jnp.transpose(swa_new, (0, 3, 1, 2)),
            st_ret, st_c, st_n, st_m[:, 0, :N_HEADS], st_g, st_gc,
            jnp.concatenate([state_ffn_conv[l][:, 1:], g_new[:, None, :]], axis=1)))

    stack = lambda states: tuple(jnp.stack([s[k] for s in states]) for k in range(len(states[0])))
    return (xp, xs.reshape(bs, 1, D_MODEL)) + stack(p_states) + stack(s_states)
```

```python
import functools
import math

import jax
import jax.numpy as jnp
import numpy as np
from jax import lax
from jax.experimental import pallas as pl
from jax.experimental.pallas import tpu as pltpu

F32 = jnp.float32
BF16 = jnp.bfloat16
HI = lax.Precision.HIGHEST

D_MODEL = 1024
HEAD_DIM = 64
N_HEADS = 4
D_HEADS = N_HEADS * HEAD_DIM
D_FF = 2816
GDN_CONV = 4
FFN_CONV = 3
PAGE_SIZE = 128
CMP_STRIDE = 16
CMP_LEN = 32
SLC_BLOCK = 64
SLC_TOPK = 16
WINDOW = 512
Q_BLOCK = 128
T5_BUCKETS = 32
T5_MAX_DIST = 1024
ROPE_BASE = 10000.0
EPS = 1e-6
SCALE = HEAD_DIM ** -0.5
IN_WIDTHS = (256, 256, 256, 256, 256, 256, 256, 4, 4, 256, 768, 4, 4, 256, 256, 384, 12)
LANES = 128
SUBLANES = 8
VMEM_LIMIT = 56 * 1024 * 1024
NEG_INF = float("-inf")

SM_MI, SM_MF, SM_GA, SM_GB, SM_NG = 0, 4, 8, 12, 16


def _cparams(*sem):
    return pltpu.CompilerParams(dimension_semantics=sem, vmem_limit_bytes=VMEM_LIMIT)


def _rms(x, g):
    return x * lax.rsqrt(jnp.mean(x * x, axis=-1, keepdims=True) + EPS) * g


def _dot(a, b, **kw):
    return jnp.dot(a, b, preferred_element_type=F32, **kw)


def _dot_nt(a, b, **kw):
    return lax.dot_general(a, b, (((1,), (1,)), ((), ())), preferred_element_type=F32, **kw)


def _dot_tn(a, b, **kw):
    return lax.dot_general(a, b, (((0,), (0,)), ((), ())), preferred_element_type=F32, **kw)


def _bf(x):
    return x.astype(BF16)


def _sigmoid(x):
    return 1.0 / (1.0 + jnp.exp(-x))


def _silu(x):
    return x * _sigmoid(x)


def _softplus(x):
    return jnp.maximum(x, 0.0) + jnp.log(1.0 + jnp.exp(-jnp.abs(x)))


def _head_rms(o):
    return o * lax.rsqrt(jnp.mean(o * o, axis=-1, keepdims=True) + EPS)


PROJ_WIDTHS = (1024, 1024, 1024, 256, 384, 128)


def _prep_w_in(w):
    wt = w.T
    parts, off = [], 0
    for wd in IN_WIDTHS:
        parts.append(wt[off:off + wd])
        off += wd
    (rq, rk, rv, rg, mq, mk, mv, mi, mf, mo, gqkv, ga, gbeta, gg, nq, nkv, ngate) = parts
    small = jnp.concatenate([mi, mf, ga, gbeta, ngate], axis=0)
    small = jnp.pad(small, ((0, LANES - small.shape[0]), (0, 0)))
    cat = jnp.concatenate([rq, rk, rv, rg, mq, mk, mv, mo, gqkv, gg, nq, nkv, small], axis=0)
    return cat.astype(BF16)


NKV_OFF = sum(PROJ_WIDTHS[:4])


def _in_proj_body(x_ref, g_ref, wt_ref, *out_refs, seq):
    h = _bf(_rms(x_ref[...], g_ref[...]))
    off = 0
    for ref in out_refs[:len(PROJ_WIDTHS)]:
        n = ref.shape[-1]
        ref[...] = _dot_nt(h, wt_ref[off:off + n, :])
        off += n
    if seq is not None:
        kvt_ref, kx_ref, vs_ref, kvw_ref = out_refs[len(PROJ_WIDTHS):]
        kvt_ref[0] = _dot_nt(wt_ref[NKV_OFF:NKV_OFF + PROJ_WIDTHS[4], :], h)
        nkv = out_refs[4][...]
        tm = nkv.shape[0]
        nblk = kx_ref.shape[-1] - HEAD_DIM
        pos = (pl.program_id(0) * tm) % seq + lax.broadcasted_iota(jnp.int32, (tm, nblk), 0)
        block_id = jnp.where(pos // SLC_BLOCK == lax.broadcasted_iota(jnp.int32, (tm, nblk), 1), 1.0, 0.0)
        kx_ref[...] = _bf(jnp.concatenate([nkv[:, 2 * HEAD_DIM:3 * HEAD_DIM], block_id], axis=1))
        vs_ref[...] = _bf(nkv[:, 3 * HEAD_DIM:4 * HEAD_DIM])
        kvw_ref[...] = _bf(nkv[:, 4 * HEAD_DIM:6 * HEAD_DIM])


def _in_proj(x2d, g, w_cat_t, tm, seq=None):
    m = x2d.shape[0]
    ntot = sum(PROJ_WIDTHS)
    const = lambda shape: pl.BlockSpec(shape, lambda i: (0,) * len(shape), pipeline_mode=pl.Buffered(1))
    out_specs = [pl.BlockSpec((tm, n), lambda i: (i, 0)) for n in PROJ_WIDTHS]
    out_shape = [jax.ShapeDtypeStruct((m, n), F32) for n in PROJ_WIDTHS]
    if seq is not None:
        per_seq = seq // tm
        nkv = PROJ_WIDTHS[4]
        out_specs.append(pl.BlockSpec((1, nkv, tm), lambda i: (i // per_seq, 0, i % per_seq)))
        out_shape.append(jax.ShapeDtypeStruct((m // seq, nkv, seq), F32))
        for n in (HEAD_DIM + seq // SLC_BLOCK, HEAD_DIM, 2 * HEAD_DIM):
            out_specs.append(pl.BlockSpec((tm, n), lambda i: (i, 0)))
            out_shape.append(jax.ShapeDtypeStruct((m, n), BF16))
    return pl.pallas_call(
        functools.partial(_in_proj_body, seq=seq),
        grid=(m // tm,),
        in_specs=[pl.BlockSpec((tm, D_MODEL), lambda i: (i, 0)), const((1, D_MODEL)), const((ntot, D_MODEL))],
        out_specs=out_specs,
        out_shape=out_shape,
        compiler_params=_cparams("parallel"),
        name="in_proj",
    )(x2d, g.reshape(1, D_MODEL), w_cat_t)


def _out_proj_body(x_ref, y0, y1, y2, y3, w_ref, gpost_ref, gpre_ref, x1_ref, h2_ref):
    acc = None
    for k, y in enumerate((y0, y1, y2, y3)):
        t = _dot(_bf(y[...]), w_ref[k * D_HEADS:(k + 1) * D_HEADS, :])
        acc = t if acc is None else acc + t
    x1 = x_ref[...] + _rms(acc, gpost_ref[...])
    x1_ref[...] = x1
    h2_ref[...] = _bf(_rms(x1, gpre_ref[...]))


def _out_proj(x2d, ys, w_out_bf, gpost, gpre, tm):
    m = x2d.shape[0]
    row = lambda n: pl.BlockSpec((tm, n), lambda i: (i, 0))
    vec = pl.BlockSpec((1, D_MODEL), lambda i: (0, 0))
    return pl.pallas_call(
        _out_proj_body,
        grid=(m // tm,),
        in_specs=[row(D_MODEL)] + [row(D_HEADS)] * 4
        + [pl.BlockSpec((D_MODEL, D_MODEL), lambda i: (0, 0)), vec, vec],
        out_specs=[row(D_MODEL), row(D_MODEL)],
        out_shape=[jax.ShapeDtypeStruct((m, D_MODEL), F32), jax.ShapeDtypeStruct((m, D_MODEL), BF16)],
        compiler_params=_cparams("parallel"),
        name="out_proj",
    )(x2d, *ys, w_out_bf, gpost.reshape(1, -1), gpre.reshape(1, -1))


FFN_TN = 256


def _gelu_tanh(x):
    c = 0.7978845608028654
    hx = 0.5 * x
    return hx + hx * jnp.tanh(x * (c + (0.044715 * c) * (x * x)))


def _ffn_seq_body(x_ref, y0, y1, y2, y3, wo_ref, gmix_ref, gpre_ref, wg_ref, wu_ref, wd_ref, cw_ref, gpost_ref,
                  x2_ref, tail_ref, carry_ref, act_ref, *, tm):
    @pl.when(pl.program_id(1) == 0)
    def _():
        carry_ref[...] = jnp.zeros_like(carry_ref)

    mix = None
    for k, y in enumerate((y0, y1, y2, y3)):
        t = _dot(_bf(y[0]), wo_ref[k * D_HEADS:(k + 1) * D_HEADS, :])
        mix = t if mix is None else mix + t
    x1 = x_ref[0] + _rms(mix, gmix_ref[...])
    h2 = _bf(_rms(x1, gpre_ref[...]))
    row8 = lax.broadcasted_iota(jnp.int32, (SUBLANES, FFN_TN), 0)
    starts = list(range(0, D_FF, FFN_TN))
    up = lambda n0: (_dot(h2, wg_ref[:, n0:n0 + FFN_TN]), _dot(h2, wu_ref[:, n0:n0 + FFN_TN]))
    nxt = up(starts[0])
    for i, n0 in enumerate(starts):
        g, u = nxt
        if i + 1 < len(starts):
            nxt = up(starts[i + 1])
        prev = carry_ref[:, n0:n0 + FFN_TN]
        g1, g2 = pltpu.roll(g, 1, 0), pltpu.roll(g, 2, 0)
        top1 = jnp.where(row8 == 0, prev[7:8, :], g1[0:SUBLANES])
        top2 = jnp.where(row8 == 0, prev[6:7, :], jnp.where(row8 == 1, prev[7:8, :], g2[0:SUBLANES]))
        g1 = jnp.concatenate([top1, g1[SUBLANES:]], axis=0)
        g2 = jnp.concatenate([top2, g2[SUBLANES:]], axis=0)
        cw = cw_ref[:, n0:n0 + FFN_TN]
        gate = cw[0:1, :] * g2 + cw[1:2, :] * g1 + cw[2:3, :] * g
        a = _gelu_tanh(gate) * u
        act_ref[:, n0:n0 + FFN_TN] = _bf(a)
        carry_ref[:, n0:n0 + FFN_TN] = g[tm - SUBLANES:tm, :]
    acc = _dot(act_ref[...], wd_ref[...])
    x2_ref[0] = x1 + _rms(acc, gpost_ref[...])
    tail_ref[0] = carry_ref[...]


def _ffn_seq(x, ys, w_out_bf, gmix, gpre, wg, wu, wd, cw, gpost, tm):
    b, l, _ = x.shape
    const = lambda shape: pl.BlockSpec(shape, lambda i, j: (0,) * len(shape), pipeline_mode=pl.Buffered(1))
    row = lambda n: pl.BlockSpec((1, tm, n), lambda i, j: (i, j, 0))
    return pl.pallas_call(
        functools.partial(_ffn_seq_body, tm=tm),
        grid=(b, l // tm),
        in_specs=[row(D_MODEL)] + [row(D_HEADS)] * 4
        + [const((D_MODEL, D_MODEL)), const((1, D_MODEL)), const((1, D_MODEL)),
           const((D_MODEL, D_FF)), const((D_MODEL, D_FF)), const((D_FF, D_MODEL)),
           const((FFN_CONV, D_FF)), const((1, D_MODEL))],
        out_specs=[pl.BlockSpec((1, tm, D_MODEL), lambda i, j: (i, j, 0)),
                   pl.BlockSpec((1, SUBLANES, D_FF), lambda i, j: (i, 0, 0))],
        out_shape=[jax.ShapeDtypeStruct((b, l, D_MODEL), F32),
                   jax.ShapeDtypeStruct((b, SUBLANES, D_FF), F32)],
        scratch_shapes=[pltpu.VMEM((SUBLANES, D_FF), F32), pltpu.VMEM((tm, D_FF), BF16)],
        compiler_params=_cparams("parallel", "arbitrary"),
        name="ffn_seq",
    )(x, *ys, w_out_bf, gmix.reshape(1, -1), gpre.reshape(1, -1), wg, wu, wd, cw, gpost.reshape(1, -1))


def _ffn_step_body(x1_ref, h2_ref, b0_ref, b1_ref, wg_ref, wu_ref, wd_ref, cw_ref, gpost_ref, x2_ref, g_ref):
    h2 = h2_ref[...]
    acc = jnp.zeros(x1_ref.shape, F32)
    for n0 in range(0, D_FF, FFN_TN):
        sl = slice(n0, n0 + FFN_TN)
        g = _dot(h2, wg_ref[:, sl])
        cw = cw_ref[:, sl]
        gate = cw[0:1, :] * b0_ref[:, sl] + cw[1:2, :] * b1_ref[:, sl] + cw[2:3, :] * g
        u = _dot(h2, wu_ref[:, sl])
        acc = acc + _dot(_bf(_gelu_tanh(gate) * u), wd_ref[sl, :])
        g_ref[:, sl] = g
    x2_ref[...] = x1_ref[...] + _rms(acc, gpost_ref[...])


def _ffn_step(x1, h2, b0, b1, wg, wu, wd, cw, gpost):
    m = x1.shape[0]
    return pl.pallas_call(
        _ffn_step_body,
        out_shape=[jax.ShapeDtypeStruct((m, D_MODEL), F32), jax.ShapeDtypeStruct((m, D_FF), F32)],
        compiler_params=pltpu.CompilerParams(vmem_limit_bytes=VMEM_LIMIT),
        name="ffn_step",
    )(x1, h2, b0, b1, wg, wu, wd, cw, gpost.reshape(1, -1))


def _rope_tables(pos):
    half = HEAD_DIM // 2
    inv = ROPE_BASE ** (-jnp.linspace(0.0, 1.0, half, dtype=F32))
    ang = pos.astype(F32)[:, None] * inv[None, :]
    return jnp.tile(jnp.cos(ang), (1, LANES // half)), jnp.tile(jnp.sin(ang), (1, LANES // half))


def _rope128(x, cos, sin):
    lane = lax.broadcasted_iota(jnp.int32, x.shape, 1)
    first = (lane % HEAD_DIM) < (HEAD_DIM // 2)
    other = jnp.where(first, -pltpu.roll(x, LANES - HEAD_DIM // 2, 1), pltpu.roll(x, HEAD_DIM // 2, 1))
    return x * cos + other * sin


def _rope256(x, cos, sin):
    return jnp.concatenate([_rope128(x[:, :LANES], cos, sin), _rope128(x[:, LANES:], cos, sin)], axis=1)


def _ret_log_decay(h):
    return math.log(1.0 - 2.0 ** (-5.0 - h))


def _head(x, h):
    return x[:, h * HEAD_DIM:(h + 1) * HEAD_DIM]


def _ret_body(z_ref, cos_ref, sin_ref, nw_ref, y_ref, s_ref, *, chunk):
    @pl.when(pl.program_id(1) == 0)
    def _():
        s_ref[...] = jnp.zeros_like(s_ref)

    nb = z_ref.shape[0]
    cos, sin = cos_ref[...], sin_ref[...]
    t = lax.broadcasted_iota(jnp.int32, (chunk, chunk), 0)
    s = lax.broadcasted_iota(jnp.int32, (chunk, chunk), 1)
    causal = t >= s
    diff = jnp.where(causal, t - s, 0).astype(F32)
    tcol = lax.broadcasted_iota(jnp.int32, (chunk, 1), 0).astype(F32)
    lgs = [_ret_log_decay(h) for h in range(N_HEADS)]
    dmat = [jnp.where(causal, jnp.exp(lg * diff), 0.0) for lg in lgs]
    xi = [jnp.exp(lg * (tcol + 1.0)) for lg in lgs]
    zeta = [jnp.exp(lg * (chunk - 1.0 - tcol)) for lg in lgs]
    probs = [(bb, h) for bb in range(nb) for h in range(N_HEADS)]
    qkv = []
    for bb in range(nb):
        q = _rope256(z_ref[bb, :, 0:256], cos, sin)
        k = _rope256(z_ref[bb, :, 256:512], cos, sin) * SCALE
        qkv.append((q, k, z_ref[bb, :, 512:768]))
    qh = [_head(qkv[bb][0], h) for bb, h in probs]
    kh = [_head(qkv[bb][1], h) for bb, h in probs]
    vh = [_bf(_head(qkv[bb][2], h)) for bb, h in probs]
    st = [s_ref[bb, h] for bb, h in probs]
    sc = [_dot_nt(_bf(q), _bf(k)) for q, k in zip(qh, kh)]
    cross = [_dot(_bf(q * xi[h]), _bf(s0)) for q, s0, (_, h) in zip(qh, st, probs)]
    upd = [_dot_tn(_bf(k * zeta[h]), v) for k, v, (_, h) in zip(kh, vh, probs)]
    o = [_dot(_bf(x * dmat[h]), v) + c for x, v, c, (_, h) in zip(sc, vh, cross, probs)]
    for i, (bb, h) in enumerate(probs):
        s_ref[bb, h] = st[i] * math.exp(lgs[h] * chunk) + upd[i]
    for bb in range(nb):
        outs = [_head_rms(o[bb * N_HEADS + h]) for h in range(N_HEADS)]
        y_ref[bb] = jnp.concatenate(outs, axis=1) * nw_ref[...] * _silu(z_ref[bb, :, 768:1024])


def _retention_seq(z_ret, cos, sin, nw, chunk, nb):
    b, l, _ = z_ret.shape
    return pl.pallas_call(
        functools.partial(_ret_body, chunk=chunk),
        grid=(b // nb, l // chunk),
        in_specs=[pl.BlockSpec((nb, chunk, 1024), lambda i, j: (i, j, 0)),
                  pl.BlockSpec((chunk, LANES), lambda i, j: (j, 0)),
                  pl.BlockSpec((chunk, LANES), lambda i, j: (j, 0)),
                  pl.BlockSpec((1, D_HEADS), lambda i, j: (0, 0))],
        out_specs=[pl.BlockSpec((nb, chunk, D_HEADS), lambda i, j: (i, j, 0)),
                   pl.BlockSpec((nb, N_HEADS, HEAD_DIM, HEAD_DIM), lambda i, j: (i, 0, 0, 0))],
        out_shape=[jax.ShapeDtypeStruct((b, l, D_HEADS), F32),
                   jax.ShapeDtypeStruct((b, N_HEADS, HEAD_DIM, HEAD_DIM), F32)],
        compiler_params=_cparams("parallel", "arbitrary"),
        name="retention_seq",
    )(z_ret, cos, sin, nw.reshape(1, -1))


def _lanes_to_rows(x, n=2 * SUBLANES):
    sel = (lax.broadcasted_iota(jnp.int32, (n, x.shape[1]), 0)
           == lax.broadcasted_iota(jnp.int32, (n, x.shape[1]), 1)).astype(F32)
    return _dot_nt(sel, x, precision=HI)


def _tri_incl(n):
    t = lax.broadcasted_iota(jnp.int32, (n, n), 0)
    s = lax.broadcasted_iota(jnp.int32, (n, n), 1)
    return t >= s


def _ml_body(z_ref, sm_ref, bias_ref, nw_ref, y_ref, c_ref, n_ref, m_ref, *, chunk):
    @pl.when(pl.program_id(1) == 0)
    def _():
        c_ref[...] = jnp.zeros_like(c_ref)
        n_ref[...] = jnp.zeros_like(n_ref)
        m_ref[...] = jnp.zeros_like(m_ref)

    nb = z_ref.shape[0]
    incl = _tri_incl(chunk)
    tri = incl.astype(F32)
    probs = [(bb, h) for bb in range(nb) for h in range(N_HEADS)]
    gate = []
    for bb in range(nb):
        pre = sm_ref[bb] + bias_ref[...]
        logf = jnp.minimum(pre, 0.0) - jnp.log(1.0 + jnp.exp(-jnp.abs(pre)))
        bcum = _dot(tri, logf, precision=HI)
        gate.append((pre, bcum, _lanes_to_rows(pre), _lanes_to_rows(bcum), m_ref[bb]))
    qh = [_head(z_ref[bb, :, 0:256], h) for bb, h in probs]
    kh = [_head(z_ref[bb, :, 256:512], h) * SCALE for bb, h in probs]
    vh = [_bf(_head(z_ref[bb, :, 512:768], h)) for bb, h in probs]
    cm = [c_ref[bb, h] for bb, h in probs]
    nv = [n_ref[bb, h:h + 1, :] for bb, h in probs]
    qk_raw = [_dot_nt(_bf(q), _bf(k)) for q, k in zip(qh, kh)]
    q_c = [_dot(_bf(q), _bf(c)) for q, c in zip(qh, cm)]
    i_col, b_col, inter, dlog = [], [], [], []
    for bb, h in probs:
        pre, bcum, pre_t, bcum_t, m_all = gate[bb]
        i_col.append(pre[:, SM_MI + h:SM_MI + h + 1])
        b_col.append(bcum[:, SM_MF + h:SM_MF + h + 1])
        inter.append(b_col[-1] + m_all[:, h:h + 1])
        dlog.append(jnp.where(incl, b_col[-1] - bcum_t[SM_MF + h:SM_MF + h + 1, :]
                              + pre_t[SM_MI + h:SM_MI + h + 1, :], NEG_INF))
    dmax = [jnp.max(x, axis=1, keepdims=True) for x in dlog]
    q_n = [jnp.sum(q * n, axis=1, keepdims=True) for q, n in zip(qh, nv)]
    stab = []
    for i in range(len(probs)):
        m_t = jnp.maximum(inter[i], dmax[i])
        m_end = m_t[chunk - 1:chunk, :]
        stab.append(dict(m_t=m_t, w=jnp.exp(dlog[i] - m_t), g_in=jnp.exp(inter[i] - m_t), m_end=m_end,
                         w_end=jnp.exp(b_col[i][chunk - 1:chunk, :] - b_col[i] + i_col[i] - m_end),
                         g_end=jnp.exp(inter[i][chunk - 1:chunk, :] - m_end)))
    qk = [x * d["w"] for x, d in zip(qk_raw, stab)]
    num = [_dot(_bf(x), v) + d["g_in"] * c for x, v, d, c in zip(qk, vh, stab, q_c)]
    upd = [_dot_tn(_bf(k * d["w_end"]), v) for k, d, v in zip(kh, stab, vh)]
    qk_sum = [jnp.sum(x, axis=1, keepdims=True) for x in qk]
    k_sum = [jnp.sum(d["w_end"] * k, axis=0, keepdims=True) for d, k in zip(stab, kh)]
    hh = []
    for i, (bb, h) in enumerate(probs):
        d = stab[i]
        den = qk_sum[i] + d["g_in"] * q_n[i]
        hh.append(num[i] / jnp.maximum(jnp.abs(den), jnp.exp(-d["m_t"])))
        c_ref[bb, h] = d["g_end"] * cm[i] + upd[i]
        n_ref[bb, h:h + 1, :] = d["g_end"] * nv[i] + k_sum[i]
    lane = lax.broadcasted_iota(jnp.int32, (1, LANES), 1)
    for bb in range(nb):
        m_out = jnp.zeros((1, LANES), F32)
        outs = []
        for h in range(N_HEADS):
            i = bb * N_HEADS + h
            m_out = jnp.where(lane == h, stab[i]["m_end"], m_out)
            outs.append(_head_rms(_sigmoid(_head(z_ref[bb, :, 768:1024], h)) * hh[i]))
        m_ref[bb] = m_out
        y_ref[bb] = jnp.concatenate(outs, axis=1) * nw_ref[...]


def _mlstm_seq(z_ml, small, bias_vec, nw, chunk, nb):
    b, l, _ = z_ml.shape
    return pl.pallas_call(
        functools.partial(_ml_body, chunk=chunk),
        grid=(b // nb, l // chunk),
        in_specs=[pl.BlockSpec((nb, chunk, 1024), lambda i, j: (i, j, 0)),
                  pl.BlockSpec((nb, chunk, LANES), lambda i, j: (i, j, 0)),
                  pl.BlockSpec((1, LANES), lambda i, j: (0, 0)),
                  pl.BlockSpec((1, D_HEADS), lambda i, j: (0, 0))],
        out_specs=[pl.BlockSpec((nb, chunk, D_HEADS), lambda i, j: (i, j, 0)),
                   pl.BlockSpec((nb, N_HEADS, HEAD_DIM, HEAD_DIM), lambda i, j: (i, 0, 0, 0)),
                   pl.BlockSpec((nb, N_HEADS, HEAD_DIM), lambda i, j: (i, 0, 0)),
                   pl.BlockSpec((nb, 1, LANES), lambda i, j: (i, 0, 0))],
        out_shape=[jax.ShapeDtypeStruct((b, l, D_HEADS), F32),
                   jax.ShapeDtypeStruct((b, N_HEADS, HEAD_DIM, HEAD_DIM), F32),
                   jax.ShapeDtypeStruct((b, N_HEADS, HEAD_DIM), F32),
                   jax.ShapeDtypeStruct((b, 1, LANES), F32)],
        compiler_params=_cparams("parallel", "arbitrary"),
        name="mlstm_seq",
    )(z_ml, small, bias_vec, nw.reshape(1, -1))


def _split_bf16(x):
    hi = _bf(x)
    return hi, _bf(x - hi.astype(F32))


def _dot3(a, b):
    (ah, al), (bh, bl) = a, b
    return _dot(ah, bh) + (_dot(ah, bl) + _dot(al, bh))


def _l2norm(x):
    return x * lax.rsqrt(jnp.sum(x * x, axis=-1, keepdims=True) + EPS)


def _gdn_body(z_ref, sm_ref, cw_ref, dtb_ref, alog_ref, nw_ref, y_ref, s_ref, buf_ref, *, chunk):
    @pl.when(pl.program_id(1) == 0)
    def _():
        s_ref[...] = jnp.zeros_like(s_ref)
        buf_ref[:, 0:SUBLANES, :] = jnp.zeros((buf_ref.shape[0], SUBLANES, 3 * D_HEADS), F32)

    nb = z_ref.shape[0]
    rows = z_ref.shape[1]
    nc = rows // chunk
    incl = _tri_incl(chunk)
    strict = lax.broadcasted_iota(jnp.int32, (chunk, chunk), 0) > lax.broadcasted_iota(jnp.int32, (chunk, chunk), 1)
    eye = (lax.broadcasted_iota(jnp.int32, (chunk, chunk), 0)
           == lax.broadcasted_iota(jnp.int32, (chunk, chunk), 1)).astype(F32)
    tri = incl.astype(F32)
    gates, acts = {}, []
    for bb in range(nb):
        buf_ref[bb, SUBLANES:SUBLANES + rows, :] = z_ref[bb, :, 0:768]
        conv = None
        for j in range(GDN_CONV):
            term = buf_ref[bb, pl.ds(SUBLANES - (GDN_CONV - 1) + j, rows), :] * cw_ref[j:j + 1, :]
            conv = term if conv is None else conv + term
        buf_ref[bb, 0:SUBLANES, :] = buf_ref[bb, rows:rows + SUBLANES, :]
        acts.append(_silu(conv))
        sm = sm_ref[bb]
        g_all = -jnp.exp(alog_ref[...]) * _softplus(sm + dtb_ref[...])
        beta_all = _sigmoid(sm)
        for cc in range(nc):
            rs = slice(cc * chunk, (cc + 1) * chunk)
            gcum = _dot(tri, g_all[rs], precision=HI)
            gates[bb, cc] = (gcum, _lanes_to_rows(gcum), beta_all[rs])

    probs = [(bb, cc, h) for cc in range(nc) for bb in range(nb) for h in range(N_HEADS)]
    pre = []
    for bb, cc, h in probs:
        rs = slice(cc * chunk, (cc + 1) * chunk)
        gcum, gcum_t, beta_all = gates[bb, cc]
        g_col = gcum[:, SM_GA + h:SM_GA + h + 1]
        g_row = gcum_t[SM_GA + h:SM_GA + h + 1, :]
        beta = beta_all[:, SM_GB + h:SM_GB + h + 1]
        decay = jnp.where(incl, jnp.exp(jnp.where(incl, g_col - g_row, 0.0)), 0.0)
        act = acts[bb]
        qh = _l2norm(_head(act[rs, 0:256], h)) * SCALE
        kh = _l2norm(_head(act[rs, 256:512], h))
        vh = _head(act[rs, 512:768], h)
        eg = jnp.exp(g_col)
        pre.append(dict(g_col=g_col, beta=beta, decay=decay, qh=qh, kh=kh, kb=_bf(kh), eg=eg,
                        rhs=jnp.concatenate([vh * beta, kh * (beta * eg)], axis=1)))
    a_l = [jnp.where(strict, d["beta"] * _dot_nt(d["kb"], d["kb"]) * d["decay"], 0.0) for d in pre]
    p_l = [eye - a for a in a_l]
    pw_s = [_split_bf16(a) for a in a_l]
    lvl = 2
    while lvl < chunk:
        pw_s = [_split_bf16(_dot3(x, x)) for x in pw_s]
        p_l = [p + _dot3(_split_bf16(p), x) for p, x in zip(p_l, pw_s)]
        lvl *= 2
    uw_l = [_dot3(_split_bf16(p), _split_bf16(d["rhs"])) for p, d in zip(p_l, pre)]
    qk_l = [_dot_nt(_bf(d["qh"]), d["kb"]) * d["decay"] for d in pre]
    per_chunk = nb * N_HEADS
    st_l = [s_ref[bb, h] for bb, cc, h in probs[:per_chunk]]
    o_all = []
    for cc in range(nc):
        sl = slice(cc * per_chunk, (cc + 1) * per_chunk)
        stb_l = [_bf(st) for st in st_l]
        delta_l = [uw[:, 0:HEAD_DIM] - _dot(_bf(uw[:, HEAD_DIM:]), stb) for uw, stb in zip(uw_l[sl], stb_l)]
        o_all += [_dot(_bf(qk), _bf(delta)) + _dot(_bf(d["qh"] * d["eg"]), stb)
                  for qk, delta, d, stb in zip(qk_l[sl], delta_l, pre[sl], stb_l)]
        new_st = []
        for d, st, delta in zip(pre[sl], st_l, delta_l):
            g_end = d["g_col"][chunk - 1:chunk, :]
            w_end = jnp.exp(g_end - d["g_col"])
            new_st.append(jnp.exp(g_end) * st + _dot_tn(_bf(d["kh"] * w_end), _bf(delta)))
        st_l = new_st
    for (bb, cc, h), st in zip(probs[:per_chunk], st_l):
        s_ref[bb, h] = st
    for bb in range(nb):
        for cc in range(nc):
            outs = [_head_rms(o_all[cc * per_chunk + bb * N_HEADS + h]) for h in range(N_HEADS)]
            rs = slice(cc * chunk, (cc + 1) * chunk)
            y_ref[bb, rs, :] = jnp.concatenate(outs, axis=1) * nw_ref[...] * _silu(z_ref[bb, rs, 768:1024])


def _gdn_seq(z_gdn, small, conv_w, dtb_vec, alog_vec, nw, chunk, nb, nc):
    b, l, _ = z_gdn.shape
    rows = nc * chunk
    return pl.pallas_call(
        functools.partial(_gdn_body, chunk=chunk),
        grid=(b // nb, l // rows),
        in_specs=[pl.BlockSpec((nb, rows, 1024), lambda i, j: (i, j, 0)),
                  pl.BlockSpec((nb, rows, LANES), lambda i, j: (i, j, 0)),
                  pl.BlockSpec((GDN_CONV, 3 * D_HEADS), lambda i, j: (0, 0)),
                  pl.BlockSpec((1, LANES), lambda i, j: (0, 0)),
                  pl.BlockSpec((1, LANES), lambda i, j: (0, 0)),
                  pl.BlockSpec((1, D_HEADS), lambda i, j: (0, 0))],
        out_specs=[pl.BlockSpec((nb, rows, D_HEADS), lambda i, j: (i, j, 0)),
                   pl.BlockSpec((nb, N_HEADS, HEAD_DIM, HEAD_DIM), lambda i, j: (i, 0, 0, 0))],
        out_shape=[jax.ShapeDtypeStruct((b, l, D_HEADS), F32),
                   jax.ShapeDtypeStruct((b, N_HEADS, HEAD_DIM, HEAD_DIM), F32)],
        scratch_shapes=[pltpu.VMEM((nb, SUBLANES + rows, 3 * D_HEADS), F32)],
        compiler_params=_cparams("parallel", "arbitrary"),
        name="gdn_seq",
    )(z_gdn, small, conv_w, dtb_vec, alog_vec, nw.reshape(1, -1))


STEP_SEQS = 4


def _row_to_col(row):
    n = row.shape[1]
    eye = lax.broadcasted_iota(jnp.int32, (n, n), 0) == lax.broadcasted_iota(jnp.int32, (n, n), 1)
    return jnp.sum(jnp.where(eye, jnp.broadcast_to(row, (n, n)), 0.0), axis=1, keepdims=True)


def _step_body(zr_ref, zm_ref, zg_ref, sm_ref, cos_ref, sin_ref, mlb_ref, dtb_ref, alog_ref, cw_ref,
               nwr_ref, nwm_ref, nwg_ref, sr_ref, mc_ref, mn_ref, mm_ref, gs_ref, gc_ref,
               yr_ref, ym_ref, yg_ref, sr_o, mc_o, mn_o, mm_o, gs_o, gc_o):
    nb = zr_ref.shape[0]
    d = HEAD_DIM
    eye = lax.broadcasted_iota(jnp.int32, (d, d), 0) == lax.broadcasted_iota(jnp.int32, (d, d), 1)
    lane = lax.broadcasted_iota(jnp.int32, (1, LANES), 1)
    cos, sin = cos_ref[...], sin_ref[...]
    rowsum = lambda x: jnp.sum(x, axis=1, keepdims=True)

    def vec_mat(row, mat):
        return _dot(_pad_rows(row), mat, precision=HI)[0:1, :]

    def outer(a_row, b_row):
        diag = jnp.where(eye, jnp.broadcast_to(a_row, (d, d)), 0.0)
        return _dot(diag, jnp.broadcast_to(b_row, (d, d)), precision=HI)

    ret, ml, gdn, seqs = [], [], [], []
    for bb in range(nb):
        sm, zr, zm, zg = sm_ref[bb], zr_ref[bb], zm_ref[bb], zg_ref[bb]
        rq = _rope256(zr[:, 0:256], cos, sin)
        rk = _rope256(zr[:, 256:512], cos, sin) * SCALE
        pre = sm + mlb_ref[...]
        logf = jnp.minimum(pre, 0.0) - jnp.log(1.0 + jnp.exp(-jnp.abs(pre)))
        m_all = mm_ref[bb]
        x = zg[:, 0:768]
        buf = gc_ref[bb]
        conv = x * cw_ref[GDN_CONV - 1:GDN_CONV, :]
        for j in range(GDN_CONV - 1):
            conv = conv + buf[j:j + 1, :] * cw_ref[j:j + 1, :]
        act = _silu(conv)
        g_all = -jnp.exp(alog_ref[...]) * _softplus(sm + dtb_ref[...])
        beta_all = _sigmoid(sm)
        seqs.append(dict(zr=zr, zm=zm, zg=zg, x=x, buf=buf))
        for h in range(N_HEADS):
            ret.append(dict(q=_head(rq, h), k=_head(rk, h), v=_head(zr[:, 512:768], h), st=sr_ref[bb, h],
                            gam=math.exp(_ret_log_decay(h))))
            i_g = pre[:, SM_MI + h:SM_MI + h + 1]
            inter = logf[:, SM_MF + h:SM_MF + h + 1] + m_all[:, h:h + 1]
            m_t = jnp.maximum(inter, i_g)
            ml.append(dict(q=_head(zm[:, 0:256], h), k=_head(zm[:, 256:512], h) * SCALE, v=_head(zm[:, 512:768], h),
                           cm=mc_ref[bb, h], nv=mn_ref[bb, h:h + 1, :], m_t=m_t, w=jnp.exp(i_g - m_t),
                           g_in=jnp.exp(inter - m_t)))
            gdn.append(dict(q=_head(act[:, 0:256], h), k=_head(act[:, 256:512], h), v=_head(act[:, 512:768], h),
                            st=gs_ref[bb, h], eg=jnp.exp(g_all[:, SM_GA + h:SM_GA + h + 1]),
                            beta=beta_all[:, SM_GB + h:SM_GB + h + 1]))
    for p in gdn:
        p["q"], p["k"] = _l2norm(p["q"]) * SCALE, _l2norm(p["k"])
    for p in ret + ml + gdn:
        p["qk"] = rowsum(p["q"] * p["k"])
    for p in ml:
        p["qn"] = rowsum(p["q"] * p["nv"])
    for p in ret:
        p["qs"] = vec_mat(p["q"] * p["gam"], p["st"])
    for p in ml:
        p["qc"] = vec_mat(p["q"], p["cm"])
    for p in gdn:
        p["ks"] = vec_mat(p["k"] * (p["beta"] * p["eg"]), p["st"])
        p["qs"] = vec_mat(p["q"] * p["eg"], p["st"])
    for p in ret:
        p["o"] = p["qk"] * p["v"] + p["qs"]
    for p in ml:
        qk = p["qk"] * p["w"]
        den = qk + p["g_in"] * p["qn"]
        p["o"] = (qk * p["v"] + p["g_in"] * p["qc"]) / jnp.maximum(jnp.abs(den), jnp.exp(-p["m_t"]))
    for p in gdn:
        p["delta"] = p["v"] * p["beta"] - p["ks"]
        p["o"] = p["qk"] * p["delta"] + p["qs"]
    for p in ret:
        p["new"] = p["st"] * p["gam"] + outer(p["k"], p["v"])
    for p in ml:
        p["new"] = p["g_in"] * p["cm"] + outer(p["k"] * p["w"], p["v"])
    for p in gdn:
        p["new"] = p["eg"] * p["st"] + outer(p["k"], p["delta"])
    for bb in range(nb):
        s = seqs[bb]
        ps = slice(bb * N_HEADS, (bb + 1) * N_HEADS)
        m_out = jnp.zeros((1, LANES), F32)
        for h, (pr, pm, pg) in enumerate(zip(ret[ps], ml[ps], gdn[ps])):
            sr_o[bb, h] = pr["new"]
            mc_o[bb, h] = pm["new"]
            mn_o[bb, h:h + 1, :] = pm["g_in"] * pm["nv"] + pm["w"] * pm["k"]
            gs_o[bb, h] = pg["new"]
            m_out = jnp.where(lane == h, pm["m_t"], m_out)
        mm_o[bb] = m_out
        gc_o[bb, 0:GDN_CONV - 2, :] = s["buf"][1:GDN_CONV - 1, :]
        gc_o[bb, GDN_CONV - 2:GDN_CONV - 1, :] = s["x"]
        yr_ref[bb] = (jnp.concatenate([_head_rms(p["o"]) for p in ret[ps]], axis=1) * nwr_ref[...]
                      * _silu(s["zr"][:, 768:1024]))
        og = s["zm"][:, 768:1024]
        ym_ref[bb] = jnp.concatenate([_head_rms(_sigmoid(_head(og, h)) * p["o"])
                                      for h, p in enumerate(ml[ps])], axis=1) * nwm_ref[...]
        yg_ref[bb] = (jnp.concatenate([_head_rms(p["o"]) for p in gdn[ps]], axis=1) * nwg_ref[...]
                      * _silu(s["zg"][:, 768:1024]))


def _mixers_step(z_ret, z_ml, z_gdn, small, cos, sin, mlb, dtb, alog, conv_w, nwr, nwm, nwg,
                 s_ret, ml_c, ml_n, ml_m, gdn_s, gdn_conv):
    b = z_ret.shape[0]
    nb = STEP_SEQS if b % STEP_SEQS == 0 else 1
    row = lambda n: pl.BlockSpec((nb, 1, n), lambda i: (i, 0, 0))
    vec = lambda n: pl.BlockSpec((1, n), lambda i: (0, 0))
    mat = pl.BlockSpec((nb, N_HEADS, HEAD_DIM, HEAD_DIM), lambda i: (i, 0, 0, 0))
    nsp = pl.BlockSpec((nb, N_HEADS, HEAD_DIM), lambda i: (i, 0, 0))
    csp = pl.BlockSpec((nb, GDN_CONV - 1, 3 * D_HEADS), lambda i: (i, 0, 0))
    mshape = jax.ShapeDtypeStruct((b, N_HEADS, HEAD_DIM, HEAD_DIM), F32)
    yshape = jax.ShapeDtypeStruct((b, 1, D_HEADS), F32)
    return pl.pallas_call(
        _step_body,
        grid=(b // nb,),
        in_specs=[row(1024), row(1024), row(1024), row(LANES), vec(LANES), vec(LANES), vec(LANES), vec(LANES), vec(LANES),
                  pl.BlockSpec((GDN_CONV, 3 * D_HEADS), lambda i: (0, 0)), vec(D_HEADS), vec(D_HEADS), vec(D_HEADS),
                  mat, mat, nsp, row(LANES), mat, csp],
        out_specs=[row(D_HEADS), row(D_HEADS), row(D_HEADS), mat, mat, nsp, row(LANES), mat, csp],
        out_shape=[yshape, yshape, yshape, mshape, mshape,
                   jax.ShapeDtypeStruct((b, N_HEADS, HEAD_DIM), F32), jax.ShapeDtypeStruct((b, 1, LANES), F32),
                   mshape, jax.ShapeDtypeStruct((b, GDN_CONV - 1, 3 * D_HEADS), F32)],
        compiler_params=_cparams("parallel"),
        name="mixers_step",
    )(z_ret, z_ml, z_gdn, small, cos, sin, mlb, dtb, alog, conv_w,
      nwr.reshape(1, -1), nwm.reshape(1, -1), nwg.reshape(1, -1), s_ret, ml_c, ml_n, ml_m, gdn_s, gdn_conv)


def _t5_bucket_np(n):
    exact = T5_BUCKETS // 2
    n = np.maximum(np.asarray(n, np.int64), 0)
    x = np.maximum(n, 1).astype(np.float32) / np.float32(exact)
    large = exact + (np.log(x) / np.float32(math.log(T5_MAX_DIST / exact)) * np.float32(T5_BUCKETS - exact)).astype(np.int32)
    return np.where(n < exact, n, np.minimum(large, T5_BUCKETS - 1)).astype(np.int32)


def _bucket_thresholds():
    b = _t5_bucket_np(np.arange(4 * T5_MAX_DIST))
    return [int(np.argmax(b >= k)) for k in range(T5_BUCKETS // 2 + 1, T5_BUCKETS)]


def _bias_table_body(rbt_ref, oh_ref, o_ref):
    o_ref[...] = _dot(rbt_ref[...], oh_ref[...].astype(F32), precision=HI)


def _bias_tables(rel_bias, dists):
    n = len(dists)
    tn = 4096
    npad = -(-n // tn) * tn
    onehot = np.zeros((T5_BUCKETS, npad), np.float32)
    onehot[_t5_bucket_np(dists), np.arange(n)] = 1.0
    rbt = jnp.pad(rel_bias.T, ((0, SUBLANES - N_HEADS), (0, 0)))
    out = pl.pallas_call(
        _bias_table_body,
        grid=(npad // tn,),
        in_specs=[pl.BlockSpec((SUBLANES, T5_BUCKETS), lambda i: (0, 0)),
                  pl.BlockSpec((T5_BUCKETS, tn), lambda i: (0, i))],
        out_specs=pl.BlockSpec((SUBLANES, tn), lambda i: (0, i)),
        out_shape=jax.ShapeDtypeStruct((SUBLANES, npad), F32),
        name="bias_tables",
    )(rbt, jnp.asarray(onehot, BF16))
    return out[:, :n]


def _prep_cmp_w(cmp_w, cmp_pe):
    eye2 = jnp.eye(2, dtype=F32)

    def half(w):
        full = jnp.einsum('cjde,cf->jcdfe', w, eye2)
        return full.reshape(CMP_STRIDE * 2 * HEAD_DIM, 2 * HEAD_DIM)

    wab = jnp.concatenate([half(cmp_w[:, :CMP_STRIDE]), half(cmp_w[:, CMP_STRIDE:])], axis=1)
    pe = jnp.concatenate([cmp_pe[:CMP_STRIDE].reshape(1, -1), cmp_pe[CMP_STRIDE:].reshape(1, -1)], axis=1)
    return wab, jnp.pad(pe, ((0, SUBLANES - 1), (0, 0)))


def _pe_term(pe_ref, wab_ref):
    kin = CMP_STRIDE * 2 * HEAD_DIM
    t = (_dot(pe_ref[:, 0:kin], wab_ref[:, 0:LANES].astype(F32), precision=HI)
         + _dot(pe_ref[:, kin:2 * kin], wab_ref[:, LANES:2 * LANES].astype(F32), precision=HI))
    return t[0:1, :]


def _combine_cmp(parts, pe_term):
    rows = parts.shape[0]
    nxt = pltpu.roll(parts[:, LANES:2 * LANES], rows - 1, 0)
    r = lax.broadcasted_iota(jnp.int32, (rows, LANES), 0)
    return jnp.where(r < rows - 1, parts[:, 0:LANES] + nxt + pe_term, 0.0)


def _compress_seq_body(kv_ref, wab_ref, pe_ref, o_ref):
    n16 = o_ref.shape[1]
    parts = jnp.zeros((n16, 2 * LANES), F32)
    for j in range(CMP_STRIDE):
        xj = kv_ref[0, pl.ds(j, n16, stride=CMP_STRIDE), :]
        parts = parts + _dot(_bf(xj), _bf(wab_ref[j * LANES:(j + 1) * LANES, :]))
    o_ref[0] = _combine_cmp(parts, _pe_term(pe_ref, wab_ref))


def _compress_seq(nkv, wab, pe):
    b, l, _ = nkv.shape
    n16 = l // CMP_STRIDE
    return pl.pallas_call(
        _compress_seq_body,
        grid=(b,),
        in_specs=[pl.BlockSpec((1, l, LANES), lambda i: (i, 0, 0)),
                  pl.BlockSpec(wab.shape, lambda i: (0, 0)),
                  pl.BlockSpec(pe.shape, lambda i: (0, 0))],
        out_specs=pl.BlockSpec((1, n16, LANES), lambda i: (i, 0, 0)),
        out_shape=jax.ShapeDtypeStruct((b, n16, LANES), F32),
        compiler_params=_cparams("parallel"),
        name="nsa_compress_seq",
    )(nkv, wab, pe)


def _softmax_rows(s, valid):
    s = jnp.where(valid, s, NEG_INF)
    m = jnp.max(s, axis=1, keepdims=True)
    m = jnp.where(m == NEG_INF, 0.0, m)
    p = jnp.exp(s - m)
    den = jnp.sum(p, axis=1, keepdims=True)
    return p / jnp.where(den > 0, den, 1.0)


def _tile_mask(ok):
    return jnp.concatenate([jnp.where(ok, 1.0, 0.0)] * N_HEADS, axis=0) > 0.5


def _stack_heads(x):
    return jnp.concatenate([_head(x, h) for h in range(N_HEADS)], axis=0)


def _toeplitz(tab_ref, h, r0, rows, ncol):
    nrow = (rows + ncol) // LANES
    strip = tab_ref[h, pl.ds(r0, nrow), :]
    flat = jnp.concatenate([strip[r:r + 1, :] for r in range(nrow)], axis=1)
    rolled = pltpu.roll(jnp.broadcast_to(flat, (rows, nrow * LANES)), 0, 1, stride=1, stride_axis=0)
    return rolled[:, rows:]


SLC_KC = 512


MASK_BIG = 1e30


def _top_blocks_t(score, k):
    row = lax.broadcasted_iota(jnp.int32, score.shape, 0)
    n = score.shape[0]
    sel = jnp.zeros(score.shape, jnp.bool_)
    for _ in range(k):
        m = jnp.max(score, axis=0, keepdims=True)
        idx = jnp.min(jnp.where(score == m, row, n), axis=0, keepdims=True)
        pick = (row == idx) & (m > NEG_INF)
        sel = sel | pick
        score = jnp.where(pick, NEG_INF, score)
    return sel


def _bucket_index(dist):
    n = jnp.maximum(dist, 0)
    exact = T5_BUCKETS // 2
    big = jnp.full(dist.shape, exact, jnp.int32)
    for thr in _bucket_thresholds():
        big = big + (n >= thr).astype(jnp.int32)
    return jnp.where(n < exact, n, big)


def _gather_bias(rbt_ref, h, bucket):
    rows, width = bucket.shape
    piece = min(width, LANES)
    tab = jnp.broadcast_to(rbt_ref[h:h + 1, 0:piece], (rows, piece))
    parts = [jnp.take_along_axis(tab, bucket[:, s:s + piece], axis=1) for s in range(0, width, piece)]
    return parts[0] if len(parts) == 1 else jnp.concatenate(parts, axis=1)


def _tile_rows(x, n=N_HEADS):
    return jnp.concatenate([x] * n, axis=0)


def _near_chunks():
    return (_bucket_thresholds()[-1] + SLC_KC - 2) // SLC_KC + 1


def _bias_tile(tab_ref, base, ncol, p0, lo, hi):
    r0 = (p0 - base - Q_BLOCK) // LANES
    bias = jnp.concatenate([_toeplitz(tab_ref, h, r0, Q_BLOCK, ncol) for h in range(N_HEADS)], axis=0)
    dist = (base + lax.broadcasted_iota(jnp.int32, (Q_BLOCK, ncol), 0)
            - lax.broadcasted_iota(jnp.int32, (Q_BLOCK, ncol), 1))
    return bias + _tile_rows(jnp.where((dist >= lo) & (dist < hi), 0.0, -MASK_BIG))


def _nsa_seq_body(rb_ref, q_ref, sm_ref, kx_ref, vs_ref, kvw_ref, kvc_ref, tab_ref, rbt_ref, ovlt_ref, y_ref,
                  stile_ref, wtile_ref, *, seq, p0):
    qb = Q_BLOCK
    rows = N_HEADS * qb
    n16 = seq // CMP_STRIDE
    nblk = seq // SLC_BLOCK
    bi = pl.program_id(1)
    q0 = bi * qb

    @pl.when((pl.program_id(0) == 0) & (bi == 0))
    def _():
        per_chunk = SLC_KC // qb

        def slc_tile(t, carry):
            base = (t // _near_chunks()) * qb + (t % _near_chunks()) * SLC_KC
            stile_ref[t] = _bias_tile(tab_ref, base, SLC_KC, p0, 0, seq + SLC_KC)
            return carry

        def win_tile(t, carry):
            wtile_ref[t] = _bias_tile(tab_ref, t * qb, WINDOW + qb, p0, 0, WINDOW)
            return carry

        lax.fori_loop(0, per_chunk * _near_chunks(), slc_tile, 0)
        lax.fori_loop(0, WINDOW // qb + 1, win_tile, 0)
    qs = _stack_heads(q_ref[0]) * SCALE
    qsb = _bf(qs)
    qpos = q0 + lax.broadcasted_iota(jnp.int32, (qb, 1), 0)

    kvc = kvc_ref[0]
    s_c = _dot_nt(qsb, _bf(kvc[:, 0:HEAD_DIM]))
    band = WINDOW + qb
    sw = pl.multiple_of(jnp.maximum(q0 - WINDOW, 0), qb)
    kw = kvw_ref[0, pl.ds(sw, band), 0:HEAD_DIM]
    vw = kvw_ref[0, pl.ds(sw, band), HEAD_DIM:2 * HEAD_DIM]
    s_w = _dot_nt(qsb, kw)

    cmp_end = lax.broadcasted_iota(jnp.int32, (qb, n16), 1) * CMP_STRIDE + (CMP_LEN - 1)
    dist_c = qpos - cmp_end
    bucket_c = _bucket_index(dist_c)
    bias_c = jnp.concatenate([_gather_bias(rbt_ref, h, bucket_c) for h in range(N_HEADS)], axis=0)
    p_c = _softmax_rows(s_c + bias_c, _tile_mask(dist_c >= 0))
    p_cb = _bf(p_c)
    o_c = _dot(p_cb, _bf(kvc[:, HEAD_DIM:2 * HEAD_DIM]))
    p_r = p_cb.astype(F32)
    p_sum = p_r[0:qb] + p_r[qb:2 * qb] + p_r[2 * qb:3 * qb] + p_r[3 * qb:4 * qb]
    imp_t = _dot_nt(ovlt_ref[...], p_sum, precision=HI)

    base_w = q0 - sw
    sb_w = s_w + wtile_ref[base_w // qb]
    e_w = jnp.exp(sb_w - jnp.max(sb_w, axis=1, keepdims=True))
    o_w = _dot(_bf(e_w / jnp.sum(e_w, axis=1, keepdims=True)), vw)

    qlane = q0 + lax.broadcasted_iota(jnp.int32, (1, qb), 1)
    cur = qlane // SLC_BLOCK
    rowb = lax.broadcasted_iota(jnp.int32, (nblk, qb), 0)
    forced = (rowb == 0) | (rowb == cur) | (rowb == cur - 1)
    score = jnp.where(forced | (rowb > cur), NEG_INF, imp_t)
    sel = jnp.where(forced | _top_blocks_t(score, SLC_TOPK - 3), 1.0, 0.0).T
    negm = _bf(_tile_rows((sel - 1.0) * MASK_BIG))
    q_ext = jnp.concatenate([qsb, negm], axis=1)

    thr_far = _bucket_thresholds()[-1] + SLC_KC - 1
    n_chunks = q0 // SLC_KC + 1
    n_far = jnp.maximum(q0 - thr_far + SLC_KC, 0) // SLC_KC
    n_near = _near_chunks()

    far_bias = jnp.concatenate([jnp.full((qb, 1), rb_ref[T5_BUCKETS - 1, h], F32) for h in range(N_HEADS)], axis=0)

    def scores(c):
        k0 = pl.multiple_of(c * SLC_KC, SLC_KC)
        kx = kx_ref[0, pl.ds(k0, SLC_KC), :]
        return q0 - k0, _dot_nt(q_ext, kx), vs_ref[0, pl.ds(k0, SLC_KC), :]

    def online(carry, s, vs):
        m, l, acc = carry
        m_new = jnp.maximum(m, jnp.max(s, axis=1, keepdims=True))
        alpha = jnp.exp(m - m_new)
        p = jnp.exp(s - m_new)
        return m_new, alpha * l + jnp.sum(p, axis=1, keepdims=True), alpha * acc + _dot(_bf(p), vs)

    def far_step(c, carry):
        _, s, vs = scores(c)
        return online(carry, s + far_bias, vs)

    def near_step(c, carry):
        base, s, vs = scores(c)
        tile = ((q0 % SLC_KC) // qb) * n_near + base // SLC_KC
        return online(carry, s + stile_ref[tile], vs)

    init = (jnp.full((rows, 1), -MASK_BIG / 10, F32), jnp.zeros((rows, 1), F32), jnp.zeros((rows, HEAD_DIM), F32))
    carry = lax.fori_loop(0, n_far, far_step, init)
    _, l_s, acc_s = lax.fori_loop(n_far, n_chunks, near_step, carry)
    o_s = acc_s / jnp.where(l_s > 0, l_s, 1.0)

    gates = _sigmoid(sm_ref[0])
    outs = []
    for h in range(N_HEADS):
        rs = slice(h * qb, (h + 1) * qb)
        g = lambda t: gates[:, SM_NG + 3 * h + t:SM_NG + 3 * h + t + 1]
        outs.append(g(0) * o_c[rs] + g(1) * o_s[rs] + g(2) * o_w[rs])
    y_ref[0] = jnp.concatenate(outs, axis=1)


def _nsa_seq(rel_bias, nq, small, kx, vs, kvw, kvc, tab, seq_p0):
    b, l, _ = nq.shape
    n16 = l // CMP_STRIDE
    nblk = l // SLC_BLOCK
    cmp_start = np.arange(n16)[None, :] * CMP_STRIDE
    slc_start = np.arange(nblk)[:, None] * SLC_BLOCK
    ovlt = ((cmp_start < slc_start + SLC_BLOCK) & (cmp_start + CMP_LEN > slc_start)
            & (np.arange(n16)[None, :] < n16 - 1)).astype(np.float32)
    rbt = jnp.pad(rel_bias.T, ((0, SUBLANES - N_HEADS), (0, LANES - T5_BUCKETS)))
    nrows = tab.shape[1]
    const = lambda shape: pl.BlockSpec(shape, lambda i, j: (0,) * len(shape))
    seq_spec = lambda n: pl.BlockSpec((1, l, n), lambda i, j: (i, 0, 0))
    return pl.pallas_call(
        functools.partial(_nsa_seq_body, seq=l, p0=seq_p0),
        grid=(b, l // Q_BLOCK),
        in_specs=[pl.BlockSpec(memory_space=pltpu.SMEM),
                  pl.BlockSpec((1, Q_BLOCK, D_HEADS), lambda i, j: (i, j, 0)),
                  pl.BlockSpec((1, Q_BLOCK, LANES), lambda i, j: (i, j, 0)),
                  seq_spec(HEAD_DIM + nblk), seq_spec(HEAD_DIM), seq_spec(2 * HEAD_DIM),
                  pl.BlockSpec((1, n16, LANES), lambda i, j: (i, 0, 0)),
                  const((SUBLANES, nrows, LANES)), const((SUBLANES, LANES)), const((nblk, n16))],
        out_specs=pl.BlockSpec((1, Q_BLOCK, D_HEADS), lambda i, j: (i, j, 0)),
        out_shape=jax.ShapeDtypeStruct((b, l, D_HEADS), F32),
        scratch_shapes=[pltpu.VMEM(((SLC_KC // Q_BLOCK) * _near_chunks(), N_HEADS * Q_BLOCK, SLC_KC), F32),
                        pltpu.VMEM((WINDOW // Q_BLOCK + 1, N_HEADS * Q_BLOCK, WINDOW + Q_BLOCK), F32)],
        compiler_params=_cparams("arbitrary", "arbitrary"),
        name="nsa_seq",
    )(rel_bias, nq, small, kx, vs, kvw, kvc, tab, rbt, jnp.asarray(ovlt))


def _nsa_tab_seq_dists(seq):
    max_base = max(seq, SLC_KC + (_near_chunks() - 1) * SLC_KC)
    p0 = max_base + Q_BLOCK
    length = p0 + WINDOW + Q_BLOCK + LANES
    return p0, np.maximum(p0 - np.arange(length), 0)


PAGE_GROUP = 16
PAGES_PER_STEP = 64


def _pad_rows(x, rows=SUBLANES):
    return jnp.concatenate([x, jnp.zeros((rows - x.shape[0], x.shape[1]), x.dtype)], axis=0)


def _nsa_select_body(pt_ref, *refs, past, n_steps):
    del pt_ref
    pages = refs[:PAGES_PER_STEP]
    wab_ref, pe_ref, q_ref, tabc_ref, ovlt_ref, oc_ref, top_ref, parts_ref, rows_ref = refs[PAGES_PER_STEP:]
    step = pl.program_id(1)
    blocks_per_page = PAGE_SIZE // CMP_STRIDE
    group_rows = PAGE_GROUP * blocks_per_page
    n_groups = PAGES_PER_STEP // PAGE_GROUP
    for p, pg in enumerate(pages):
        g, pp = divmod(p, PAGE_GROUP)
        rows_ref[g, pp * PAGE_SIZE:(pp + 1) * PAGE_SIZE, :] = pg[...].reshape(2 * HEAD_DIM, PAGE_SIZE).T
    for g in range(n_groups):
        acc = jnp.zeros((group_rows, 2 * LANES), F32)
        for j in range(CMP_STRIDE):
            xj = rows_ref[g, pl.ds(j, group_rows, stride=CMP_STRIDE), :]
            acc = acc + _dot(_bf(xj), wab_ref[j * LANES:(j + 1) * LANES, :])
        row0 = pl.multiple_of((step * n_groups + g) * group_rows, group_rows)
        parts_ref[pl.ds(row0, group_rows), :] = acc

    @pl.when(step == n_steps - 1)
    def _():
        n16 = past // CMP_STRIDE
        n_cmp = n16 - 1
        cur = past // SLC_BLOCK
        pe_term = _pe_term(pe_ref, wab_ref)
        kvc = _combine_cmp(parts_ref[...], pe_term)
        qs = _pad_rows(_stack_heads(q_ref[0]) * SCALE)
        s_c = _dot_nt(_bf(qs), _bf(kvc[:, 0:HEAD_DIM])) + tabc_ref[...]
        lane = lax.broadcasted_iota(jnp.int32, (SUBLANES, n16), 1)
        p_c = _softmax_rows(s_c, lane < n_cmp)
        p_cb = _bf(p_c)
        o_c = _dot(p_cb, _bf(kvc[:, HEAD_DIM:2 * HEAD_DIM]))
        oc_ref[0] = jnp.concatenate([o_c[h:h + 1, :] for h in range(N_HEADS)], axis=1)
        imp_h = _dot_nt(ovlt_ref[...], p_cb)
        imp = imp_h[:, 0:1] + imp_h[:, 1:2] + imp_h[:, 2:3] + imp_h[:, 3:4]
        nrow = imp.shape[0]
        rowb = lax.broadcasted_iota(jnp.int32, (nrow, 1), 0)
        forced_ids = (0, cur - 1, cur)
        forced = (rowb == 0) | (rowb == cur) | (rowb == cur - 1)
        score = jnp.where(forced | (rowb > cur), NEG_INF, imp)
        out_lane = lax.broadcasted_iota(jnp.int32, (1, LANES), 1)
        top = jnp.zeros((1, LANES), jnp.int32)
        for r, blk in enumerate(forced_ids):
            top = jnp.where(out_lane == r, blk, top)
        for r in range(len(forced_ids), SLC_TOPK):
            m = jnp.max(score, axis=0, keepdims=True)
            idx = jnp.min(jnp.where(score == m, rowb, nrow), axis=0, keepdims=True)
            top = jnp.where(out_lane == r, idx, top)
            score = jnp.where(rowb == idx, NEG_INF, score)
        top_ref[0] = top


def _nsa_select(layer, page_table, cache_t, wab_bf, pe, nq, tabc, past):
    b, n_pages = page_table.shape
    n_steps = n_pages // PAGES_PER_STEP
    n16 = past // CMP_STRIDE
    n_slc = past // SLC_BLOCK + 1
    nrow = -(-n_slc // SUBLANES) * SUBLANES
    cmp_start = np.arange(n16)[None, :] * CMP_STRIDE
    slc_start = np.arange(nrow)[:, None] * SLC_BLOCK
    ovlt = ((cmp_start < slc_start + SLC_BLOCK) & (cmp_start + CMP_LEN > slc_start)
            & (np.arange(n16)[None, :] < n16 - 1) & (np.arange(nrow)[:, None] < n_slc)).astype(np.float32)

    n_pool = cache_t.shape[1]

    def page_spec(p):
        def imap(i, s, pt):
            pg = pt[jnp.minimum(i, b - 1), jnp.minimum(s, n_steps - 1) * PAGES_PER_STEP + p]
            return (layer, jnp.minimum(pg, n_pool - 1), 0, 0, 0)
        return pl.BlockSpec((None, None, 2, HEAD_DIM, PAGE_SIZE), imap)

    const = lambda shape: pl.BlockSpec(shape, lambda i, s, pt: (0,) * len(shape))
    grid_spec = pltpu.PrefetchScalarGridSpec(
        num_scalar_prefetch=1,
        grid=(b, n_steps),
        in_specs=[page_spec(p) for p in range(PAGES_PER_STEP)]
        + [const(wab_bf.shape), const(pe.shape), pl.BlockSpec((1, 1, D_HEADS), lambda i, s, pt: (i, 0, 0)),
           const(tabc.shape), const(ovlt.shape)],
        out_specs=[pl.BlockSpec((1, 1, D_HEADS), lambda i, s, pt: (i, 0, 0)),
                   pl.BlockSpec((1, 1, LANES), lambda i, s, pt: (i, 0, 0))],
        scratch_shapes=[pltpu.VMEM((n16, 2 * LANES), F32),
                        pltpu.VMEM((PAGES_PER_STEP // PAGE_GROUP, PAGE_GROUP * PAGE_SIZE, 2 * HEAD_DIM), F32)],
    )
    return pl.pallas_call(
        functools.partial(_nsa_select_body, past=past, n_steps=n_steps),
        grid_spec=grid_spec,
        out_shape=[jax.ShapeDtypeStruct((b, 1, D_HEADS), F32), jax.ShapeDtypeStruct((b, 1, LANES), jnp.int32)],
        compiler_params=_cparams("parallel", "arbitrary"),
        name="nsa_select",
    )(page_table, *([cache_t] * PAGES_PER_STEP), wab_bf, pe, nq, tabc, jnp.asarray(ovlt, BF16))


def _nsa_attend_body(pt_ref, top_ref, *refs, past):
    del pt_ref
    pages = refs[:SLC_TOPK]
    (q_ref, sm_ref, new_ref, oc_ref, swa_ref, tabs_ref, tabw_ref, y_ref, swa_out) = refs[SLC_TOPK:]
    b = pl.program_id(0)
    new_blk = past // SLC_BLOCK
    blocks_per_page = PAGE_SIZE // SLC_BLOCK
    qs = _bf(_pad_rows(_stack_heads(q_ref[0]) * SCALE))
    new = new_ref[0]
    lane_t = lax.broadcasted_iota(jnp.int32, (HEAD_DIM, PAGE_SIZE), 1)
    lane = lax.broadcasted_iota(jnp.int32, (SUBLANES, PAGE_SIZE), 1)
    new_kt = jnp.where(lane_t == 0, _row_to_col(new[:, 2 * HEAD_DIM:3 * HEAD_DIM]), 0.0)
    new_vt = jnp.where(lane_t == 0, _row_to_col(new[:, 3 * HEAD_DIM:4 * HEAD_DIM]), 0.0)
    kts, biases, oks, vts = [], [], [], []
    for s in range(SLC_TOPK):
        ti = jnp.clip(top_ref[b, s], 0, new_blk)
        is_new = ti == new_blk
        kts.append(jnp.where(is_new, new_kt, pages[s][0]))
        vts.append(jnp.where(is_new, new_vt, pages[s][1]))
        biases.append(tabs_ref[ti])
        kpos = ti * SLC_BLOCK + lane % SLC_BLOCK
        ok = (lane // SLC_BLOCK == ti % blocks_per_page) & (kpos <= past)
        oks.append(jnp.where(ok, 1.0, 0.0))
    wlane = lax.broadcasted_iota(jnp.int32, (HEAD_DIM, WINDOW), 1)
    win = []
    for c in range(2):
        col = _row_to_col(new[:, (4 + c) * HEAD_DIM:(5 + c) * HEAD_DIM])
        win.append(jnp.where(wlane == WINDOW - 1, col, pltpu.roll(swa_ref[c], WINDOW - 1, 1)))
    s_w = _dot(qs, _bf(win[0])) + tabw_ref[...]

    s_s = _dot(qs, _bf(jnp.concatenate(kts, axis=1))) + jnp.concatenate(biases, axis=1)
    p_s = _softmax_rows(s_s, jnp.concatenate(oks, axis=1) > 0.5)
    p_w = _softmax_rows(s_w, jnp.full(s_w.shape, True))
    o_s = _dot_nt(_bf(p_s), _bf(jnp.concatenate(vts, axis=1)))
    o_w = _dot_nt(_bf(p_w), _bf(win[1]))
    for c in range(2):
        swa_out[c] = win[c]

    gates = _sigmoid(sm_ref[0])
    o_c = oc_ref[0]
    outs = []
    for h in range(N_HEADS):
        g = lambda t: gates[:, SM_NG + 3 * h + t:SM_NG + 3 * h + t + 1]
        outs.append(g(0) * _head(o_c, h) + g(1) * o_s[h:h + 1, :] + g(2) * o_w[h:h + 1, :])
    y_ref[0] = jnp.concatenate(outs, axis=1)


def _nsa_attend(layer, page_table, top, cache_t, nq, small, nkv, o_c, swa_t, tabs, tabw, past):
    b = nq.shape[0]
    last_page = past // PAGE_SIZE - 1
    blocks_per_page = PAGE_SIZE // SLC_BLOCK

    n_pool = cache_t.shape[1]

    def page_spec(s):
        def imap(i, pt, tp):
            ii = jnp.minimum(i, b - 1)
            pg = jnp.clip(tp[ii, s] // blocks_per_page, 0, last_page)
            return (layer, jnp.clip(pt[ii, pg], 0, n_pool - 1), 1, 0, 0)
        return pl.BlockSpec((None, None, 2, HEAD_DIM, PAGE_SIZE), imap)

    row = lambda n: pl.BlockSpec((1, 1, n), lambda i, pt, tp: (i, 0, 0))
    const = lambda shape: pl.BlockSpec(shape, lambda i, pt, tp: (0,) * len(shape))
    grid_spec = pltpu.PrefetchScalarGridSpec(
        num_scalar_prefetch=2,
        grid=(b,),
        in_specs=[page_spec(s) for s in range(SLC_TOPK)]
        + [row(D_HEADS), row(LANES), row(6 * HEAD_DIM), row(D_HEADS),
           pl.BlockSpec((None, None, 2, HEAD_DIM, WINDOW), lambda i, pt, tp: (layer, i, 0, 0, 0)),
           const(tabs.shape), const(tabw.shape)],
        out_specs=[row(D_HEADS), pl.BlockSpec((None, 2, HEAD_DIM, WINDOW), lambda i, pt, tp: (i, 0, 0, 0))],
    )
    return pl.pallas_call(
        functools.partial(_nsa_attend_body, past=past),
        grid_spec=grid_spec,
        out_shape=[jax.ShapeDtypeStruct((b, 1, D_HEADS), F32),
                   jax.ShapeDtypeStruct((b, 2, HEAD_DIM, WINDOW), F32)],
        compiler_params=_cparams("parallel"),
        name="nsa_attend",
    )(page_table, top, *([cache_t] * SLC_TOPK), nq, small, nkv, o_c, swa_t, tabs, tabw)


SEQ_CHUNK = 64
GDN_CHUNKS = 1
SCAN_CHUNK = 256
ROW_TILE = 512
FFN_TILE = 512


def _lane_vec(pairs):
    v = jnp.zeros((1, LANES), F32)
    for off, vals in pairs:
        v = v.at[0, off:off + vals.shape[0]].set(vals.astype(F32))
    return v


def kernel(x_prompt, x_sample, cache_nsa_kv, cache_swa_kv, state_ret, state_mlstm_C, state_mlstm_n, state_mlstm_m, state_gdn, state_gdn_conv, state_ffn_conv, page_table, rel_bias, norm_pre_mix, w_in, mlstm_b_i, mlstm_b_f, gdn_conv_w, gdn_A_log, gdn_dt_bias, nsa_cmp_pe, nsa_cmp_w, ret_norm, mlstm_norm, gdn_norm, w_out, norm_post_mix, norm_pre_ffn, w_ffn_gate, w_ffn_up, ffn_conv_w, w_ffn_down, norm_post_ffn):
    depth = w_in.shape[0]
    bp, seq, _ = x_prompt.shape
    bs, dec_seq, _ = x_sample.shape
    assert dec_seq == 1 and seq % SLC_KC == 0 and seq >= WINDOW + Q_BLOCK
    n_pool = cache_nsa_kv.shape[1]
    past = page_table.shape[1] * PAGE_SIZE
    assert past >= WINDOW and past % SLC_BLOCK == 0 and page_table.shape[1] % PAGES_PER_STEP == 0
    assert past // SLC_BLOCK + 1 >= SLC_TOPK

    p0, d_seq = _nsa_tab_seq_dists(seq)
    n16 = past // CMP_STRIDE
    n_slc = past // SLC_BLOCK + 1
    d_cmp = np.maximum(past - (np.arange(n16) * CMP_STRIDE + CMP_LEN - 1), 0)
    d_slc = np.maximum(past - np.arange(n_slc * SLC_BLOCK), 0)
    d_win = WINDOW - 1 - np.arange(WINDOW)
    tab = _bias_tables(rel_bias, np.concatenate([d_seq, d_cmp, d_slc, d_win]))
    o1, o2, o3 = len(d_seq), len(d_seq) + len(d_cmp), len(d_seq) + len(d_cmp) + len(d_slc)
    tab_seq = tab[:, :o1].reshape(SUBLANES, o1 // LANES, LANES)
    tabc = tab[:, o1:o2]
    tabs = jnp.transpose(tab[:, o2:o3].reshape(SUBLANES, n_slc, SLC_BLOCK), (1, 0, 2))
    tabs = jnp.concatenate([tabs, tabs], axis=2)
    tabw = tab[:, o3:]

    cache_t = jnp.transpose(cache_nsa_kv, (0, 1, 3, 4, 2))
    swa_t = jnp.transpose(cache_swa_kv, (0, 1, 3, 4, 2))
    cos_p, sin_p = _rope_tables(jnp.arange(seq))
    cos_s, sin_s = _rope_tables(jnp.arange(past, past + 1))

    xp = x_prompt
    xs = x_sample.reshape(bs, D_MODEL)
    p_states, s_states = [], []
    for l in range(depth):
        w_cat = _prep_w_in(w_in[l])
        w_out_bf = _bf(w_out[l])
        wg, wu, wd = _bf(w_ffn_gate[l]), _bf(w_ffn_up[l]), _bf(w_ffn_down[l])
        wab, pe = _prep_cmp_w(nsa_cmp_w[l], nsa_cmp_pe[l])
        ml_bias = _lane_vec([(SM_MI, mlstm_b_i[l]), (SM_MF, mlstm_b_f[l])])
        dtb = _lane_vec([(SM_GA, gdn_dt_bias[l])])
        alog = _lane_vec([(SM_GA, gdn_A_log[l])])

        z = _in_proj(xp.reshape(bp * seq, D_MODEL), norm_pre_mix[l], w_cat, ROW_TILE, seq=seq)
        z_ret, z_ml, z_gdn, z_nq, z_nkv, z_sm = [t.reshape(bp, seq, -1) for t in z[:6]]
        nkv_t = z[6].reshape(bp, 6, HEAD_DIM, seq)
        kx, vs, kvw = [t.reshape(bp, seq, -1) for t in z[7:]]
        y_ret, st_ret = _retention_seq(z_ret, cos_p, sin_p, ret_norm[l], SCAN_CHUNK, bp)
        y_ml, st_c, st_n, st_m = _mlstm_seq(z_ml, z_sm, ml_bias, mlstm_norm[l], SCAN_CHUNK, bp)
        y_gdn, st_g = _gdn_seq(z_gdn, z_sm, gdn_conv_w[l], dtb, alog, gdn_norm[l], SEQ_CHUNK, bp, GDN_CHUNKS)
        kvc = _compress_seq(z_nkv, wab, pe)
        y_nsa = _nsa_seq(rel_bias, z_nq, z_sm, kx, vs, kvw, kvc, tab_seq, p0)
        xp, tail = _ffn_seq(xp, (y_ret, y_ml, y_gdn, y_nsa), w_out_bf, norm_post_mix[l], norm_pre_ffn[l],
                            wg, wu, wd, ffn_conv_w[l], norm_post_ffn[l], min(seq, FFN_TILE))
        p_states.append((
            jnp.transpose(nkv_t[:, 0:4], (0, 3, 1, 2)),
            jnp.transpose(nkv_t[:, 4:6, :, seq - WINDOW:], (0, 3, 1, 2)),
            st_ret, st_c, st_n, st_m[:, 0, :N_HEADS], st_g,
            z_gdn[:, seq - (GDN_CONV - 1):, 0:3 * D_HEADS],
            tail[:, SUBLANES - (FFN_CONV - 1):, :]))

        z = _in_proj(xs, norm_pre_mix[l], w_cat, bs)
        z_ret, z_ml, z_gdn, z_nq, z_nkv, z_sm = [t.reshape(bs, 1, -1) for t in z]
        m_pad = jnp.pad(state_mlstm_m[l], ((0, 0), (0, LANES - N_HEADS))).reshape(bs, 1, LANES)
        (y_ret, y_ml, y_gdn, st_ret, st_c, st_n, st_m, st_g, st_gc) = _mixers_step(
            z_ret, z_ml, z_gdn, z_sm, cos_s, sin_s, ml_bias, dtb, alog, gdn_conv_w[l],
            ret_norm[l], mlstm_norm[l], gdn_norm[l],
            state_ret[l], state_mlstm_C[l], state_mlstm_n[l], m_pad, state_gdn[l], state_gdn_conv[l])
        o_c, top = _nsa_select(l, page_table, cache_t, _bf(wab), pe, z_nq, tabc, past)
        y_nsa, swa_new = _nsa_attend(l, page_table, top[:, 0, :SLC_TOPK], cache_t, z_nq, z_sm, z_nkv, o_c,
                                     swa_t, tabs, tabw, past)
        ys = [t.reshape(bs, D_HEADS) for t in (y_ret, y_ml, y_gdn, y_nsa)]
        x1, h2 = _out_proj(xs, ys, w_out_bf, norm_post_mix[l], norm_pre_ffn[l], bs)
        xs, g_new = _ffn_step(x1, h2, state_ffn_conv[l][:, 0], state_ffn_conv[l][:, 1], wg, wu, wd,
                              ffn_conv_w[l], norm_post_ffn[l])
        s_states.append((
            z_nkv[..., 0:4 * HEAD_DIM].reshape(bs, 1, 4, HEAD_DIM),
            jnp.transpose(swa_new, (0, 3, 1, 2)),
            st_ret, st_c, st_n, st_m[:, 0, :N_HEADS], st_g, st_gc,
            jnp.concatenate([state_ffn_conv[l][:, 1:], g_new[:, None, :]], axis=1)))

    stack = lambda states: tuple(jnp.stack([s[k] for s in states]) for k in range(len(states[0])))
    return (xp, xs.reshape(bs, 1, D_MODEL)) + stack(p_states) + stack(s_states)
```

```python
import functools
import math

import jax
import jax.numpy as jnp
import numpy as np
from jax import lax
from jax.experimental import pallas as pl
from jax.experimental.pallas import tpu as pltpu

F32 = jnp.float32
BF16 = jnp.bfloat16
HI = lax.Precision.HIGHEST

D_MODEL = 1024
HEAD_DIM = 64
N_HEADS = 4
D_HEADS = N_HEADS * HEAD_DIM
D_FF = 2816
GDN_CONV = 4
FFN_CONV = 3
PAGE_SIZE = 128
CMP_STRIDE = 16
CMP_LEN = 32
SLC_BLOCK = 64
SLC_TOPK = 16
WINDOW = 512
Q_BLOCK = 128
T5_BUCKETS = 32
T5_MAX_DIST = 1024
ROPE_BASE = 10000.0
EPS = 1e-6
SCALE = HEAD_DIM ** -0.5
IN_WIDTHS = (256, 256, 256, 256, 256, 256, 256, 4, 4, 256, 768, 4, 4, 256, 256, 384, 12)
LANES = 128
SUBLANES = 8
VMEM_LIMIT = 56 * 1024 * 1024
NEG_INF = float("-inf")

SM_MI, SM_MF, SM_GA, SM_GB, SM_NG = 0, 4, 8, 12, 16


def _cparams(*sem):
    return pltpu.CompilerParams(dimension_semantics=sem, vmem_limit_bytes=VMEM_LIMIT)


def _rms(x, g):
    return x * lax.rsqrt(jnp.mean(x * x, axis=-1, keepdims=True) + EPS) * g


def _dot(a, b, **kw):
    return jnp.dot(a, b, preferred_element_type=F32, **kw)


def _dot_nt(a, b, **kw):
    return lax.dot_general(a, b, (((1,), (1,)), ((), ())), preferred_element_type=F32, **kw)


def _dot_tn(a, b, **kw):
    return lax.dot_general(a, b, (((0,), (0,)), ((), ())), preferred_element_type=F32, **kw)


def _bf(x):
    return x.astype(BF16)


def _sigmoid(x):
    return 1.0 / (1.0 + jnp.exp(-x))


def _silu(x):
    return x * _sigmoid(x)


def _softplus(x):
    return jnp.maximum(x, 0.0) + jnp.log(1.0 + jnp.exp(-jnp.abs(x)))


def _head_rms(o):
    return o * lax.rsqrt(jnp.mean(o * o, axis=-1, keepdims=True) + EPS)


PROJ_WIDTHS = (1024, 1024, 1024, 256, 384, 128)


def _prep_w_in(w):
    wt = w.T
    parts, off = [], 0
    for wd in IN_WIDTHS:
        parts.append(wt[off:off + wd])
        off += wd
    (rq, rk, rv, rg, mq, mk, mv, mi, mf, mo, gqkv, ga, gbeta, gg, nq, nkv, ngate) = parts
    small = jnp.concatenate([mi, mf, ga, gbeta, ngate], axis=0)
    small = jnp.pad(small, ((0, LANES - small.shape[0]), (0, 0)))
    cat = jnp.concatenate([rq, rk, rv, rg, mq, mk, mv, mo, gqkv, gg, nq, nkv, small], axis=0)
    return cat.astype(BF16)


NKV_OFF = sum(PROJ_WIDTHS[:4])


def _in_proj_body(x_ref, g_ref, wt_ref, *out_refs, seq):
    h = _bf(_rms(x_ref[...], g_ref[...]))
    off = 0
    for ref in out_refs[:len(PROJ_WIDTHS)]:
        n = ref.shape[-1]
        ref[...] = _dot_nt(h, wt_ref[off:off + n, :])
        off += n
    if seq is not None:
        kvt_ref, kx_ref, vs_ref, kvw_ref = out_refs[len(PROJ_WIDTHS):]
        kvt_ref[0] = _dot_nt(wt_ref[NKV_OFF:NKV_OFF + PROJ_WIDTHS[4], :], h)
        nkv = out_refs[4][...]
        tm = nkv.shape[0]
        nblk = kx_ref.shape[-1] - HEAD_DIM
        pos = (pl.program_id(0) * tm) % seq + lax.broadcasted_iota(jnp.int32, (tm, nblk), 0)
        block_id = jnp.where(pos // SLC_BLOCK == lax.broadcasted_iota(jnp.int32, (tm, nblk), 1), 1.0, 0.0)
        kx_ref[...] = _bf(jnp.concatenate([nkv[:, 2 * HEAD_DIM:3 * HEAD_DIM], block_id], axis=1))
        vs_ref[...] = _bf(nkv[:, 3 * HEAD_DIM:4 * HEAD_DIM])
        kvw_ref[...] = _bf(nkv[:, 4 * HEAD_DIM:6 * HEAD_DIM])


def _in_proj(x2d, g, w_cat_t, tm, seq=None):
    m = x2d.shape[0]
    ntot = sum(PROJ_WIDTHS)
    const = lambda shape: pl.BlockSpec(shape, lambda i: (0,) * len(shape), pipeline_mode=pl.Buffered(1))
    out_specs = [pl.BlockSpec((tm, n), lambda i: (i, 0)) for n in PROJ_WIDTHS]
    out_shape = [jax.ShapeDtypeStruct((m, n), F32) for n in PROJ_WIDTHS]
    if seq is not None:
        per_seq = seq // tm
        nkv = PROJ_WIDTHS[4]
        out_specs.append(pl.BlockSpec((1, nkv, tm), lambda i: (i // per_seq, 0, i % per_seq)))
        out_shape.append(jax.ShapeDtypeStruct((m // seq, nkv, seq), F32))
        for n in (HEAD_DIM + seq // SLC_BLOCK, HEAD_DIM, 2 * HEAD_DIM):
            out_specs.append(pl.BlockSpec((tm, n), lambda i: (i, 0)))
            out_shape.append(jax.ShapeDtypeStruct((m, n), BF16))
    return pl.pallas_call(
        functools.partial(_in_proj_body, seq=seq),
        grid=(m // tm,),
        in_specs=[pl.BlockSpec((tm, D_MODEL), lambda i: (i, 0)), const((1, D_MODEL)), const((ntot, D_MODEL))],
        out_specs=out_specs,
        out_shape=out_shape,
        compiler_params=_cparams("parallel"),
        name="in_proj",
    )(x2d, g.reshape(1, D_MODEL), w_cat_t)


def _out_proj_body(x_ref, y0, y1, y2, y3, w_ref, gpost_ref, gpre_ref, x1_ref, h2_ref):
    acc = None
    for k, y in enumerate((y0, y1, y2, y3)):
        t = _dot(_bf(y[...]), w_ref[k * D_HEADS:(k + 1) * D_HEADS, :])
        acc = t if acc is None else acc + t
    x1 = x_ref[...] + _rms(acc, gpost_ref[...])
    x1_ref[...] = x1
    h2_ref[...] = _bf(_rms(x1, gpre_ref[...]))


def _out_proj(x2d, ys, w_out_bf, gpost, gpre, tm):
    m = x2d.shape[0]
    row = lambda n: pl.BlockSpec((tm, n), lambda i: (i, 0))
    vec = pl.BlockSpec((1, D_MODEL), lambda i: (0, 0))
    return pl.pallas_call(
        _out_proj_body,
        grid=(m // tm,),
        in_specs=[row(D_MODEL)] + [row(D_HEADS)] * 4
        + [pl.BlockSpec((D_MODEL, D_MODEL), lambda i: (0, 0)), vec, vec],
        out_specs=[row(D_MODEL), row(D_MODEL)],
        out_shape=[jax.ShapeDtypeStruct((m, D_MODEL), F32), jax.ShapeDtypeStruct((m, D_MODEL), BF16)],
        compiler_params=_cparams("parallel"),
        name="out_proj",
    )(x2d, *ys, w_out_bf, gpost.reshape(1, -1), gpre.reshape(1, -1))


FFN_TN = 256


def _gelu_tanh(x):
    c = 0.7978845608028654
    hx = 0.5 * x
    return hx + hx * jnp.tanh(x * (c + (0.044715 * c) * (x * x)))


def _ffn_seq_body(x_ref, y0, y1, y2, y3, wo_ref, gmix_ref, gpre_ref, wg_ref, wu_ref, wd_ref, cw_ref, gpost_ref,
                  x2_ref, tail_ref, carry_ref, act_ref, *, tm):
    @pl.when(pl.program_id(1) == 0)
    def _():
        carry_ref[...] = jnp.zeros_like(carry_ref)

    mix = None
    for k, y in enumerate((y0, y1, y2, y3)):
        t = _dot(_bf(y[0]), wo_ref[k * D_HEADS:(k + 1) * D_HEADS, :])
        mix = t if mix is None else mix + t
    x1 = x_ref[0] + _rms(mix, gmix_ref[...])
    h2 = _bf(_rms(x1, gpre_ref[...]))
    row8 = lax.broadcasted_iota(jnp.int32, (SUBLANES, FFN_TN), 0)
    starts = list(range(0, D_FF, FFN_TN))
    up = lambda n0: (_dot(h2, wg_ref[:, n0:n0 + FFN_TN]), _dot(h2, wu_ref[:, n0:n0 + FFN_TN]))
    nxt = up(starts[0])
    for i, n0 in enumerate(starts):
        g, u = nxt
        if i + 1 < len(starts):
            nxt = up(starts[i + 1])
        prev = carry_ref[:, n0:n0 + FFN_TN]
        g1, g2 = pltpu.roll(g, 1, 0), pltpu.roll(g, 2, 0)
        top1 = jnp.where(row8 == 0, prev[7:8, :], g1[0:SUBLANES])
        top2 = jnp.where(row8 == 0, prev[6:7, :], jnp.where(row8 == 1, prev[7:8, :], g2[0:SUBLANES]))
        g1 = jnp.concatenate([top1, g1[SUBLANES:]], axis=0)
        g2 = jnp.concatenate([top2, g2[SUBLANES:]], axis=0)
        cw = cw_ref[:, n0:n0 + FFN_TN]
        gate = cw[0:1, :] * g2 + cw[1:2, :] * g1 + cw[2:3, :] * g
        a = _gelu_tanh(gate) * u
        act_ref[:, n0:n0 + FFN_TN] = _bf(a)
        carry_ref[:, n0:n0 + FFN_TN] = g[tm - SUBLANES:tm, :]
    acc = _dot(act_ref[...], wd_ref[...])
    x2_ref[0] = x1 + _rms(acc, gpost_ref[...])
    tail_ref[0] = carry_ref[...]


def _ffn_seq(x, ys, w_out_bf, gmix, gpre, wg, wu, wd, cw, gpost, tm):
    b, l, _ = x.shape
    const = lambda shape: pl.BlockSpec(shape, lambda i, j: (0,) * len(shape), pipeline_mode=pl.Buffered(1))
    row = lambda n: pl.BlockSpec((1, tm, n), lambda i, j: (i, j, 0))
    return pl.pallas_call(
        functools.partial(_ffn_seq_body, tm=tm),
        grid=(b, l // tm),
        in_specs=[row(D_MODEL)] + [row(D_HEADS)] * 4
        + [const((D_MODEL, D_MODEL)), const((1, D_MODEL)), const((1, D_MODEL)),
           const((D_MODEL, D_FF)), const((D_MODEL, D_FF)), const((D_FF, D_MODEL)),
           const((FFN_CONV, D_FF)), const((1, D_MODEL))],
        out_specs=[pl.BlockSpec((1, tm, D_MODEL), lambda i, j: (i, j, 0)),
                   pl.BlockSpec((1, SUBLANES, D_FF), lambda i, j: (i, 0, 0))],
        out_shape=[jax.ShapeDtypeStruct((b, l, D_MODEL), F32),
                   jax.ShapeDtypeStruct((b, SUBLANES, D_FF), F32)],
        scratch_shapes=[pltpu.VMEM((SUBLANES, D_FF), F32), pltpu.VMEM((tm, D_FF), BF16)],
        compiler_params=_cparams("parallel", "arbitrary"),
        name="ffn_seq",
    )(x, *ys, w_out_bf, gmix.reshape(1, -1), gpre.reshape(1, -1), wg, wu, wd, cw, gpost.reshape(1, -1))


def _ffn_step_body(x1_ref, h2_ref, b0_ref, b1_ref, wg_ref, wu_ref, wd_ref, cw_ref, gpost_ref, x2_ref, g_ref):
    h2 = h2_ref[...]
    acc = jnp.zeros(x1_ref.shape, F32)
    for n0 in range(0, D_FF, FFN_TN):
        sl = slice(n0, n0 + FFN_TN)
        g = _dot(h2, wg_ref[:, sl])
        cw = cw_ref[:, sl]
        gate = cw[0:1, :] * b0_ref[:, sl] + cw[1:2, :] * b1_ref[:, sl] + cw[2:3, :] * g
        u = _dot(h2, wu_ref[:, sl])
        acc = acc + _dot(_bf(_gelu_tanh(gate) * u), wd_ref[sl, :])
        g_ref[:, sl] = g
    x2_ref[...] = x1_ref[...] + _rms(acc, gpost_ref[...])


def _ffn_step(x1, h2, b0, b1, wg, wu, wd, cw, gpost):
    m = x1.shape[0]
    return pl.pallas_call(
        _ffn_step_body,
        out_shape=[jax.ShapeDtypeStruct((m, D_MODEL), F32), jax.ShapeDtypeStruct((m, D_FF), F32)],
        compiler_params=pltpu.CompilerParams(vmem_limit_bytes=VMEM_LIMIT),
        name="ffn_step",
    )(x1, h2, b0, b1, wg, wu, wd, cw, gpost.reshape(1, -1))


def _rope_tables(pos):
    half = HEAD_DIM // 2
    inv = ROPE_BASE ** (-jnp.linspace(0.0, 1.0, half, dtype=F32))
    ang = pos.astype(F32)[:, None] * inv[None, :]
    return jnp.tile(jnp.cos(ang), (1, LANES // half)), jnp.tile(jnp.sin(ang), (1, LANES // half))


def _rope128(x, cos, sin):
    lane = lax.broadcasted_iota(jnp.int32, x.shape, 1)
    first = (lane % HEAD_DIM) < (HEAD_DIM // 2)
    other = jnp.where(first, -pltpu.roll(x, LANES - HEAD_DIM // 2, 1), pltpu.roll(x, HEAD_DIM // 2, 1))
    return x * cos + other * sin


def _rope256(x, cos, sin):
    return jnp.concatenate([_rope128(x[:, :LANES], cos, sin), _rope128(x[:, LANES:], cos, sin)], axis=1)


def _ret_log_decay(h):
    return math.log(1.0 - 2.0 ** (-5.0 - h))


def _head(x, h):
    return x[:, h * HEAD_DIM:(h + 1) * HEAD_DIM]


def _ret_body(z_ref, cos_ref, sin_ref, nw_ref, y_ref, s_ref, *, chunk):
    @pl.when(pl.program_id(1) == 0)
    def _():
        s_ref[...] = jnp.zeros_like(s_ref)

    nb = z_ref.shape[0]
    cos, sin = cos_ref[...], sin_ref[...]
    t = lax.broadcasted_iota(jnp.int32, (chunk, chunk), 0)
    s = lax.broadcasted_iota(jnp.int32, (chunk, chunk), 1)
    causal = t >= s
    diff = jnp.where(causal, t - s, 0).astype(F32)
    tcol = lax.broadcasted_iota(jnp.int32, (chunk, 1), 0).astype(F32)
    lgs = [_ret_log_decay(h) for h in range(N_HEADS)]
    dmat = [jnp.where(causal, jnp.exp(lg * diff), 0.0) for lg in lgs]
    xi = [jnp.exp(lg * (tcol + 1.0)) for lg in lgs]
    zeta = [jnp.exp(lg * (chunk - 1.0 - tcol)) for lg in lgs]
    probs = [(bb, h) for bb in range(nb) for h in range(N_HEADS)]
    qkv = []
    for bb in range(nb):
        q = _rope256(z_ref[bb, :, 0:256], cos, sin)
        k = _rope256(z_ref[bb, :, 256:512], cos, sin) * SCALE
        qkv.append((q, k, z_ref[bb, :, 512:768]))
    qh = [_head(qkv[bb][0], h) for bb, h in probs]
    kh = [_head(qkv[bb][1], h) for bb, h in probs]
    vh = [_bf(_head(qkv[bb][2], h)) for bb, h in probs]
    st = [s_ref[bb, h] for bb, h in probs]
    sc = [_dot_nt(_bf(q), _bf(k)) for q, k in zip(qh, kh)]
    cross = [_dot(_bf(q * xi[h]), _bf(s0)) for q, s0, (_, h) in zip(qh, st, probs)]
    upd = [_dot_tn(_bf(k * zeta[h]), v) for k, v, (_, h) in zip(kh, vh, probs)]
    o = [_dot(_bf(x * dmat[h]), v) + c for x, v, c, (_, h) in zip(sc, vh, cross, probs)]
    for i, (bb, h) in enumerate(probs):
        s_ref[bb, h] = st[i] * math.exp(lgs[h] * chunk) + upd[i]
    for bb in range(nb):
        outs = [_head_rms(o[bb * N_HEADS + h]) for h in range(N_HEADS)]
        y_ref[bb] = jnp.concatenate(outs, axis=1) * nw_ref[...] * _silu(z_ref[bb, :, 768:1024])


def _retention_seq(z_ret, cos, sin, nw, chunk, nb):
    b, l, _ = z_ret.shape
    return pl.pallas_call(
        functools.partial(_ret_body, chunk=chunk),
        grid=(b // nb, l // chunk),
        in_specs=[pl.BlockSpec((nb, chunk, 1024), lambda i, j: (i, j, 0)),
                  pl.BlockSpec((chunk, LANES), lambda i, j: (j, 0)),
                  pl.BlockSpec((chunk, LANES), lambda i, j: (j, 0)),
                  pl.BlockSpec((1, D_HEADS), lambda i, j: (0, 0))],
        out_specs=[pl.BlockSpec((nb, chunk, D_HEADS), lambda i, j: (i, j, 0)),
                   pl.BlockSpec((nb, N_HEADS, HEAD_DIM, HEAD_DIM), lambda i, j: (i, 0, 0, 0))],
        out_shape=[jax.ShapeDtypeStruct((b, l, D_HEADS), F32),
                   jax.ShapeDtypeStruct((b, N_HEADS, HEAD_DIM, HEAD_DIM), F32)],
        compiler_params=_cparams("parallel", "arbitrary"),
        name="retention_seq",
    )(z_ret, cos, sin, nw.reshape(1, -1))


def _lanes_to_rows(x, n=2 * SUBLANES):
    sel = (lax.broadcasted_iota(jnp.int32, (n, x.shape[1]), 0)
           == lax.broadcasted_iota(jnp.int32, (n, x.shape[1]), 1)).astype(F32)
    return _dot_nt(sel, x, precision=HI)


def _tri_incl(n):
    t = lax.broadcasted_iota(jnp.int32, (n, n), 0)
    s = lax.broadcasted_iota(jnp.int32, (n, n), 1)
    return t >= s


def _ml_body(z_ref, sm_ref, bias_ref, nw_ref, y_ref, c_ref, n_ref, m_ref, *, chunk):
    @pl.when(pl.program_id(1) == 0)
    def _():
        c_ref[...] = jnp.zeros_like(c_ref)
        n_ref[...] = jnp.zeros_like(n_ref)
        m_ref[...] = jnp.zeros_like(m_ref)

    nb = z_ref.shape[0]
    incl = _tri_incl(chunk)
    tri = incl.astype(F32)
    probs = [(bb, h) for bb in range(nb) for h in range(N_HEADS)]
    gate = []
    for bb in range(nb):
        pre = sm_ref[bb] + bias_ref[...]
        logf = jnp.minimum(pre, 0.0) - jnp.log(1.0 + jnp.exp(-jnp.abs(pre)))
        bcum = _dot(tri, logf, precision=HI)
        gate.append((pre, bcum, _lanes_to_rows(pre), _lanes_to_rows(bcum), m_ref[bb]))
    qh = [_head(z_ref[bb, :, 0:256], h) for bb, h in probs]
    kh = [_head(z_ref[bb, :, 256:512], h) * SCALE for bb, h in probs]
    vh = [_bf(_head(z_ref[bb, :, 512:768], h)) for bb, h in probs]
    cm = [c_ref[bb, h] for bb, h in probs]
    nv = [n_ref[bb, h:h + 1, :] for bb, h in probs]
    qk_raw = [_dot_nt(_bf(q), _bf(k)) for q, k in zip(qh, kh)]
    q_c = [_dot(_bf(q), _bf(c)) for q, c in zip(qh, cm)]
    i_col, b_col, inter, dlog = [], [], [], []
    for bb, h in probs:
        pre, bcum, pre_t, bcum_t, m_all = gate[bb]
        i_col.append(pre[:, SM_MI + h:SM_MI + h + 1])
        b_col.append(bcum[:, SM_MF + h:SM_MF + h + 1])
        inter.append(b_col[-1] + m_all[:, h:h + 1])
        dlog.append(jnp.where(incl, b_col[-1] - bcum_t[SM_MF + h:SM_MF + h + 1, :]
                              + pre_t[SM_MI + h:SM_MI + h + 1, :], NEG_INF))
    dmax = [jnp.max(x, axis=1, keepdims=True) for x in dlog]
    q_n = [jnp.sum(q * n, axis=1, keepdims=True) for q, n in zip(qh, nv)]
    stab = []
    for i in range(len(probs)):
        m_t = jnp.maximum(inter[i], dmax[i])
        m_end = m_t[chunk - 1:chunk, :]
        stab.append(dict(m_t=m_t, w=jnp.exp(dlog[i] - m_t), g_in=jnp.exp(inter[i] - m_t), m_end=m_end,
                         w_end=jnp.exp(b_col[i][chunk - 1:chunk, :] - b_col[i] + i_col[i] - m_end),
                         g_end=jnp.exp(inter[i][chunk - 1:chunk, :] - m_end)))
    qk = [x * d["w"] for x, d in zip(qk_raw, stab)]
    num = [_dot(_bf(x), v) + d["g_in"] * c for x, v, d, c in zip(qk, vh, stab, q_c)]
    upd = [_dot_tn(_bf(k * d["w_end"]), v) for k, d, v in zip(kh, stab, vh)]
    qk_sum = [jnp.sum(x, axis=1, keepdims=True) for x in qk]
    k_sum = [jnp.sum(d["w_end"] * k, axis=0, keepdims=True) for d, k in zip(stab, kh)]
    hh = []
    for i, (bb, h) in enumerate(probs):
        d = stab[i]
        den = qk_sum[i] + d["g_in"] * q_n[i]
        hh.append(num[i] / jnp.maximum(jnp.abs(den), jnp.exp(-d["m_t"])))
        c_ref[bb, h] = d["g_end"] * cm[i] + upd[i]
        n_ref[bb, h:h + 1, :] = d["g_end"] * nv[i] + k_sum[i]
    lane = lax.broadcasted_iota(jnp.int32, (1, LANES), 1)
    for bb in range(nb):
        m_out = jnp.zeros((1, LANES), F32)
        outs = []
        for h in range(N_HEADS):
            i = bb * N_HEADS + h
            m_out = jnp.where(lane == h, stab[i]["m_end"], m_out)
            outs.append(_head_rms(_sigmoid(_head(z_ref[bb, :, 768:1024], h)) * hh[i]))
        m_ref[bb] = m_out
        y_ref[bb] = jnp.concatenate(outs, axis=1) * nw_ref[...]


def _mlstm_seq(z_ml, small, bias_vec, nw, chunk, nb):
    b, l, _ = z_ml.shape
    return pl.pallas_call(
        functools.partial(_ml_body, chunk=chunk),
        grid=(b // nb, l // chunk),
        in_specs=[pl.BlockSpec((nb, chunk, 1024), lambda i, j: (i, j, 0)),
                  pl.BlockSpec((nb, chunk, LANES), lambda i, j: (i, j, 0)),
                  pl.BlockSpec((1, LANES), lambda i, j: (0, 0)),
                  pl.BlockSpec((1, D_HEADS), lambda i, j: (0, 0))],
        out_specs=[pl.BlockSpec((nb, chunk, D_HEADS), lambda i, j: (i, j, 0)),
                   pl.BlockSpec((nb, N_HEADS, HEAD_DIM, HEAD_DIM), lambda i, j: (i, 0, 0, 0)),
                   pl.BlockSpec((nb, N_HEADS, HEAD_DIM), lambda i, j: (i, 0, 0)),
                   pl.BlockSpec((nb, 1, LANES), lambda i, j: (i, 0, 0))],
        out_shape=[jax.ShapeDtypeStruct((b, l, D_HEADS), F32),
                   jax.ShapeDtypeStruct((b, N_HEADS, HEAD_DIM, HEAD_DIM), F32),
                   jax.ShapeDtypeStruct((b, N_HEADS, HEAD_DIM), F32),
                   jax.ShapeDtypeStruct((b, 1, LANES), F32)],
        compiler_params=_cparams("parallel", "arbitrary"),
        name="mlstm_seq",
    )(z_ml, small, bias_vec, nw.reshape(1, -1))


def _split_bf16(x):
    hi = _bf(x)
    return hi, _bf(x - hi.astype(F32))


def _dot3(a, b):
    (ah, al), (bh, bl) = a, b
    return _dot(ah, bh) + (_dot(ah, bl) + _dot(al, bh))


def _l2norm(x):
    return x * lax.rsqrt(jnp.sum(x * x, axis=-1, keepdims=True) + EPS)


def _gdn_body(z_ref, sm_ref, cw_ref, dtb_ref, alog_ref, nw_ref, y_ref, s_ref, buf_ref, *, chunk):
    @pl.when(pl.program_id(1) == 0)
    def _():
        s_ref[...] = jnp.zeros_like(s_ref)
        buf_ref[:, 0:SUBLANES, :] = jnp.zeros((buf_ref.shape[0], SUBLANES, 3 * D_HEADS), F32)

    nb = z_ref.shape[0]
    rows = z_ref.shape[1]
    nc = rows // chunk
    incl = _tri_incl(chunk)
    strict = lax.broadcasted_iota(jnp.int32, (chunk, chunk), 0) > lax.broadcasted_iota(jnp.int32, (chunk, chunk), 1)
    eye = (lax.broadcasted_iota(jnp.int32, (chunk, chunk), 0)
           == lax.broadcasted_iota(jnp.int32, (chunk, chunk), 1)).astype(F32)
    tri = incl.astype(F32)
    gates, acts = {}, []
    for bb in range(nb):
        buf_ref[bb, SUBLANES:SUBLANES + rows, :] = z_ref[bb, :, 0:768]
        conv = None
        for j in range(GDN_CONV):
            term = buf_ref[bb, pl.ds(SUBLANES - (GDN_CONV - 1) + j, rows), :] * cw_ref[j:j + 1, :]
            conv = term if conv is None else conv + term
        buf_ref[bb, 0:SUBLANES, :] = buf_ref[bb, rows:rows + SUBLANES, :]
        acts.append(_silu(conv))
        sm = sm_ref[bb]
        g_all = -jnp.exp(alog_ref[...]) * _softplus(sm + dtb_ref[...])
        beta_all = _sigmoid(sm)
        for cc in range(nc):
            rs = slice(cc * chunk, (cc + 1) * chunk)
            gcum = _dot(tri, g_all[rs], precision=HI)
            gates[bb, cc] = (gcum, _lanes_to_rows(gcum), beta_all[rs])

    probs = [(bb, cc, h) for cc in range(nc) for bb in range(nb) for h in range(N_HEADS)]
    pre = []
    for bb, cc, h in probs:
        rs = slice(cc * chunk, (cc + 1) * chunk)
        gcum, gcum_t, beta_all = gates[bb, cc]
        g_col = gcum[:, SM_GA + h:SM_GA + h + 1]
        g_row = gcum_t[SM_GA + h:SM_GA + h + 1, :]
        beta = beta_all[:, SM_GB + h:SM_GB + h + 1]
        decay = jnp.where(incl, jnp.exp(jnp.where(incl, g_col - g_row, 0.0)), 0.0)
        act = acts[bb]
        qh = _l2norm(_head(act[rs, 0:256], h)) * SCALE
        kh = _l2norm(_head(act[rs, 256:512], h))
        vh = _head(act[rs, 512:768], h)
        eg = jnp.exp(g_col)
        pre.append(dict(g_col=g_col, beta=beta, decay=decay, qh=qh, kh=kh, kb=_bf(kh), eg=eg,
                        rhs=jnp.concatenate([vh * beta, kh * (beta * eg)], axis=1)))
    a_l = [jnp.where(strict, d["beta"] * _dot_nt(d["kb"], d["kb"]) * d["decay"], 0.0) for d in pre]
    p_l = [eye - a for a in a_l]
    pw_s = [_split_bf16(a) for a in a_l]
    lvl = 2
    while lvl < chunk:
        pw_s = [_split_bf16(_dot3(x, x)) for x in pw_s]
        p_l = [p + _dot3(_split_bf16(p), x) for p, x in zip(p_l, pw_s)]
        lvl *= 2
    uw_l = [_dot3(_split_bf16(p), _split_bf16(d["rhs"])) for p, d in zip(p_l, pre)]
    qk_l = [_dot_nt(_bf(d["qh"]), d["kb"]) * d["decay"] for d in pre]
    per_chunk = nb * N_HEADS
    st_l = [s_ref[bb, h] for bb, cc, h in probs[:per_chunk]]
    o_all = []
    for cc in range(nc):
        sl = slice(cc * per_chunk, (cc + 1) * per_chunk)
        stb_l = [_bf(st) for st in st_l]
        delta_l = [uw[:, 0:HEAD_DIM] - _dot(_bf(uw[:, HEAD_DIM:]), stb) for uw, stb in zip(uw_l[sl], stb_l)]
        o_all += [_dot(_bf(qk), _bf(delta)) + _dot(_bf(d["qh"] * d["eg"]), stb)
                  for qk, delta, d, stb in zip(qk_l[sl], delta_l, pre[sl], stb_l)]
        new_st = []
        for d, st, delta in zip(pre[sl], st_l, delta_l):
            g_end = d["g_col"][chunk - 1:chunk, :]
            w_end = jnp.exp(g_end - d["g_col"])
            new_st.append(jnp.exp(g_end) * st + _dot_tn(_bf(d["kh"] * w_end), _bf(delta)))
        st_l = new_st
    for (bb, cc, h), st in zip(probs[:per_chunk], st_l):
        s_ref[bb, h] = st
    for bb in range(nb):
        for cc in range(nc):
            outs = [_head_rms(o_all[cc * per_chunk + bb * N_HEADS + h]) for h in range(N_HEADS)]
            rs = slice(cc * chunk, (cc + 1) * chunk)
            y_ref[bb, rs, :] = jnp.concatenate(outs, axis=1) * nw_ref[...] * _silu(z_ref[bb, rs, 768:1024])


def _gdn_seq(z_gdn, small, conv_w, dtb_vec, alog_vec, nw, chunk, nb, nc):
    b, l, _ = z_gdn.shape
    rows = nc * chunk
    return pl.pallas_call(
        functools.partial(_gdn_body, chunk=chunk),
        grid=(b // nb, l // rows),
        in_specs=[pl.BlockSpec((nb, rows, 1024), lambda i, j: (i, j, 0)),
                  pl.BlockSpec((nb, rows, LANES), lambda i, j: (i, j, 0)),
                  pl.BlockSpec((GDN_CONV, 3 * D_HEADS), lambda i, j: (0, 0)),
                  pl.BlockSpec((1, LANES), lambda i, j: (0, 0)),
                  pl.BlockSpec((1, LANES), lambda i, j: (0, 0)),
                  pl.BlockSpec((1, D_HEADS), lambda i, j: (0, 0))],
        out_specs=[pl.BlockSpec((nb, rows, D_HEADS), lambda i, j: (i, j, 0)),
                   pl.BlockSpec((nb, N_HEADS, HEAD_DIM, HEAD_DIM), lambda i, j: (i, 0, 0, 0))],
        out_shape=[jax.ShapeDtypeStruct((b, l, D_HEADS), F32),
                   jax.ShapeDtypeStruct((b, N_HEADS, HEAD_DIM, HEAD_DIM), F32)],
        scratch_shapes=[pltpu.VMEM((nb, SUBLANES + rows, 3 * D_HEADS), F32)],
        compiler_params=_cparams("parallel", "arbitrary"),
        name="gdn_seq",
    )(z_gdn, small, conv_w, dtb_vec, alog_vec, nw.reshape(1, -1))


STEP_SEQS = 4


def _row_to_col(row):
    n = row.shape[1]
    eye = lax.broadcasted_iota(jnp.int32, (n, n), 0) == lax.broadcasted_iota(jnp.int32, (n, n), 1)
    return jnp.sum(jnp.where(eye, jnp.broadcast_to(row, (n, n)), 0.0), axis=1, keepdims=True)


def _step_body(zr_ref, zm_ref, zg_ref, sm_ref, cos_ref, sin_ref, mlb_ref, dtb_ref, alog_ref, cw_ref,
               nwr_ref, nwm_ref, nwg_ref, sr_ref, mc_ref, mn_ref, mm_ref, gs_ref, gc_ref,
               yr_ref, ym_ref, yg_ref, sr_o, mc_o, mn_o, mm_o, gs_o, gc_o):
    nb = zr_ref.shape[0]
    d = HEAD_DIM
    eye = lax.broadcasted_iota(jnp.int32, (d, d), 0) == lax.broadcasted_iota(jnp.int32, (d, d), 1)
    lane = lax.broadcasted_iota(jnp.int32, (1, LANES), 1)
    cos, sin = cos_ref[...], sin_ref[...]
    rowsum = lambda x: jnp.sum(x, axis=1, keepdims=True)

    def vec_mat(row, mat):
        return _dot(_pad_rows(row), mat, precision=HI)[0:1, :]

    def outer(a_row, b_row):
        diag = jnp.where(eye, jnp.broadcast_to(a_row, (d, d)), 0.0)
        return _dot(diag, jnp.broadcast_to(b_row, (d, d)), precision=HI)

    ret, ml, gdn, seqs = [], [], [], []
    for bb in range(nb):
        sm, zr, zm, zg = sm_ref[bb], zr_ref[bb], zm_ref[bb], zg_ref[bb]
        rq = _rope256(zr[:, 0:256], cos, sin)
        rk = _rope256(zr[:, 256:512], cos, sin) * SCALE
        pre = sm + mlb_ref[...]
        logf = jnp.minimum(pre, 0.0) - jnp.log(1.0 + jnp.exp(-jnp.abs(pre)))
        m_all = mm_ref[bb]
        x = zg[:, 0:768]
        buf = gc_ref[bb]
        conv = x * cw_ref[GDN_CONV - 1:GDN_CONV, :]
        for j in range(GDN_CONV - 1):
            conv = conv + buf[j:j + 1, :] * cw_ref[j:j + 1, :]
        act = _silu(conv)
        g_all = -jnp.exp(alog_ref[...]) * _softplus(sm + dtb_ref[...])
        beta_all = _sigmoid(sm)
        seqs.append(dict(zr=zr, zm=zm, zg=zg, x=x, buf=buf))
        for h in range(N_HEADS):
            ret.append(dict(q=_head(rq, h), k=_head(rk, h), v=_head(zr[:, 512:768], h), st=sr_ref[bb, h],
                            gam=math.exp(_ret_log_decay(h))))
            i_g = pre[:, SM_MI + h:SM_MI + h + 1]
            inter = logf[:, SM_MF + h:SM_MF + h + 1] + m_all[:, h:h + 1]
            m_t = jnp.maximum(inter, i_g)
            ml.append(dict(q=_head(zm[:, 0:256], h), k=_head(zm[:, 256:512], h) * SCALE, v=_head(zm[:, 512:768], h),
                           cm=mc_ref[bb, h], nv=mn_ref[bb, h:h + 1, :], m_t=m_t, w=jnp.exp(i_g - m_t),
                           g_in=jnp.exp(inter - m_t)))
            gdn.append(dict(q=_head(act[:, 0:256], h), k=_head(act[:, 256:512], h), v=_head(act[:, 512:768], h),
                            st=gs_ref[bb, h], eg=jnp.exp(g_all[:, SM_GA + h:SM_GA + h + 1]),
                            beta=beta_all[:, SM_GB + h:SM_GB + h + 1]))
    for p in gdn:
        p["q"], p["k"] = _l2norm(p["q"]) * SCALE, _l2norm(p["k"])
    for p in ret + ml + gdn:
        p["qk"] = rowsum(p["q"] * p["k"])
    for p in ml:
        p["qn"] = rowsum(p["q"] * p["nv"])
    for p in ret:
        p["qs"] = vec_mat(p["q"] * p["gam"], p["st"])
    for p in ml:
        p["qc"] = vec_mat(p["q"], p["cm"])
    for p in gdn:
        p["ks"] = vec_mat(p["k"] * (p["beta"] * p["eg"]), p["st"])
        p["qs"] = vec_mat(p["q"] * p["eg"], p["st"])
    for p in ret:
        p["o"] = p["qk"] * p["v"] + p["qs"]
    for p in ml:
        qk = p["qk"] * p["w"]
        den = qk + p["g_in"] * p["qn"]
        p["o"] = (qk * p["v"] + p["g_in"] * p["qc"]) / jnp.maximum(jnp.abs(den), jnp.exp(-p["m_t"]))
    for p in gdn:
        p["delta"] = p["v"] * p["beta"] - p["ks"]
        p["o"] = p["qk"] * p["delta"] + p["qs"]
    for p in ret:
        p["new"] = p["st"] * p["gam"] + outer(p["k"], p["v"])
    for p in ml:
        p["new"] = p["g_in"] * p["cm"] + outer(p["k"] * p["w"], p["v"])
    for p in gdn:
        p["new"] = p["eg"] * p["st"] + outer(p["k"], p["delta"])
    for bb in range(nb):
        s = seqs[bb]
        ps = slice(bb * N_HEADS, (bb + 1) * N_HEADS)
        m_out = jnp.zeros((1, LANES), F32)
        for h, (pr, pm, pg) in enumerate(zip(ret[ps], ml[ps], gdn[ps])):
            sr_o[bb, h] = pr["new"]
            mc_o[bb, h] = pm["new"]
            mn_o[bb, h:h + 1, :] = pm["g_in"] * pm["nv"] + pm["w"] * pm["k"]
            gs_o[bb, h] = pg["new"]
            m_out = jnp.where(lane == h, pm["m_t"], m_out)
        mm_o[bb] = m_out
        gc_o[bb, 0:GDN_CONV - 2, :] = s["buf"][1:GDN_CONV - 1, :]
        gc_o[bb, GDN_CONV - 2:GDN_CONV - 1, :] = s["x"]
        yr_ref[bb] = (jnp.concatenate([_head_rms(p["o"]) for p in ret[ps]], axis=1) * nwr_ref[...]
                      * _silu(s["zr"][:, 768:1024]))
        og = s["zm"][:, 768:1024]
        ym_ref[bb] = jnp.concatenate([_head_rms(_sigmoid(_head(og, h)) * p["o"])
                                      for h, p in enumerate(ml[ps])], axis=1) * nwm_ref[...]
        yg_ref[bb] = (jnp.concatenate([_head_rms(p["o"]) for p in gdn[ps]], axis=1) * nwg_ref[...]
                      * _silu(s["zg"][:, 768:1024]))


def _mixers_step(z_ret, z_ml, z_gdn, small, cos, sin, mlb, dtb, alog, conv_w, nwr, nwm, nwg,
                 s_ret, ml_c, ml_n, ml_m, gdn_s, gdn_conv):
    b = z_ret.shape[0]
    nb = STEP_SEQS if b % STEP_SEQS == 0 else 1
    row = lambda n: pl.BlockSpec((nb, 1, n), lambda i: (i, 0, 0))
    vec = lambda n: pl.BlockSpec((1, n), lambda i: (0, 0))
    mat = pl.BlockSpec((nb, N_HEADS, HEAD_DIM, HEAD_DIM), lambda i: (i, 0, 0, 0))
    nsp = pl.BlockSpec((nb, N_HEADS, HEAD_DIM), lambda i: (i, 0, 0))
    csp = pl.BlockSpec((nb, GDN_CONV - 1, 3 * D_HEADS), lambda i: (i, 0, 0))
    mshape = jax.ShapeDtypeStruct((b, N_HEADS, HEAD_DIM, HEAD_DIM), F32)
    yshape = jax.ShapeDtypeStruct((b, 1, D_HEADS), F32)
    return pl.pallas_call(
        _step_body,
        grid=(b // nb,),
        in_specs=[row(1024), row(1024), row(1024), row(LANES), vec(LANES), vec(LANES), vec(LANES), vec(LANES), vec(LANES),
                  pl.BlockSpec((GDN_CONV, 3 * D_HEADS), lambda i: (0, 0)), vec(D_HEADS), vec(D_HEADS), vec(D_HEADS),
                  mat, mat, nsp, row(LANES), mat, csp],
        out_specs=[row(D_HEADS), row(D_HEADS), row(D_HEADS), mat, mat, nsp, row(LANES), mat, csp],
        out_shape=[yshape, yshape, yshape, mshape, mshape,
                   jax.ShapeDtypeStruct((b, N_HEADS, HEAD_DIM), F32), jax.ShapeDtypeStruct((b, 1, LANES), F32),
                   mshape, jax.ShapeDtypeStruct((b, GDN_CONV - 1, 3 * D_HEADS), F32)],
        compiler_params=_cparams("parallel"),
        name="mixers_step",
    )(z_ret, z_ml, z_gdn, small, cos, sin, mlb, dtb, alog, conv_w,
      nwr.reshape(1, -1), nwm.reshape(1, -1), nwg.reshape(1, -1), s_ret, ml_c, ml_n, ml_m, gdn_s, gdn_conv)


def _t5_bucket_np(n):
    exact = T5_BUCKETS // 2
    n = np.maximum(np.asarray(n, np.int64), 0)
    x = np.maximum(n, 1).astype(np.float32) / np.float32(exact)
    large = exact + (np.log(x) / np.float32(math.log(T5_MAX_DIST / exact)) * np.float32(T5_BUCKETS - exact)).astype(np.int32)
    return np.where(n < exact, n, np.minimum(large, T5_BUCKETS - 1)).astype(np.int32)


def _bucket_thresholds():
    b = _t5_bucket_np(np.arange(4 * T5_MAX_DIST))
    return [int(np.argmax(b >= k)) for k in range(T5_BUCKETS // 2 + 1, T5_BUCKETS)]


def _bias_table_body(rbt_ref, oh_ref, o_ref):
    o_ref[...] = _dot(rbt_ref[...], oh_ref[...].astype(F32), precision=HI)


def _bias_tables(rel_bias, dists):
    n = len(dists)
    tn = 4096
    npad = -(-n // tn) * tn
    onehot = np.zeros((T5_BUCKETS, npad), np.float32)
    onehot[_t5_bucket_np(dists), np.arange(n)] = 1.0
    rbt = jnp.pad(rel_bias.T, ((0, SUBLANES - N_HEADS), (0, 0)))
    out = pl.pallas_call(
        _bias_table_body,
        grid=(npad // tn,),
        in_specs=[pl.BlockSpec((SUBLANES, T5_BUCKETS), lambda i: (0, 0)),
                  pl.BlockSpec((T5_BUCKETS, tn), lambda i: (0, i))],
        out_specs=pl.BlockSpec((SUBLANES, tn), lambda i: (0, i)),
        out_shape=jax.ShapeDtypeStruct((SUBLANES, npad), F32),
        name="bias_tables",
    )(rbt, jnp.asarray(onehot, BF16))
    return out[:, :n]


def _prep_cmp_w(cmp_w, cmp_pe):
    eye2 = jnp.eye(2, dtype=F32)

    def half(w):
        full = jnp.einsum('cjde,cf->jcdfe', w, eye2)
        return full.reshape(CMP_STRIDE * 2 * HEAD_DIM, 2 * HEAD_DIM)

    wab = jnp.concatenate([half(cmp_w[:, :CMP_STRIDE]), half(cmp_w[:, CMP_STRIDE:])], axis=1)
    pe = jnp.concatenate([cmp_pe[:CMP_STRIDE].reshape(1, -1), cmp_pe[CMP_STRIDE:].reshape(1, -1)], axis=1)
    return wab, jnp.pad(pe, ((0, SUBLANES - 1), (0, 0)))


def _pe_term(pe_ref, wab_ref):
    kin = CMP_STRIDE * 2 * HEAD_DIM
    t = (_dot(pe_ref[:, 0:kin], wab_ref[:, 0:LANES].astype(F32), precision=HI)
         + _dot(pe_ref[:, kin:2 * kin], wab_ref[:, LANES:2 * LANES].astype(F32), precision=HI))
    return t[0:1, :]


def _combine_cmp(parts, pe_term):
    rows = parts.shape[0]
    nxt = pltpu.roll(parts[:, LANES:2 * LANES], rows - 1, 0)
    r = lax.broadcasted_iota(jnp.int32, (rows, LANES), 0)
    return jnp.where(r < rows - 1, parts[:, 0:LANES] + nxt + pe_term, 0.0)


def _compress_seq_body(kv_ref, wab_ref, pe_ref, o_ref):
    n16 = o_ref.shape[1]
    parts = jnp.zeros((n16, 2 * LANES), F32)
    for j in range(CMP_STRIDE):
        xj = kv_ref[0, pl.ds(j, n16, stride=CMP_STRIDE), :]
        parts = parts + _dot(_bf(xj), _bf(wab_ref[j * LANES:(j + 1) * LANES, :]))
    o_ref[0] = _combine_cmp(parts, _pe_term(pe_ref, wab_ref))


def _compress_seq(nkv, wab, pe):
    b, l, _ = nkv.shape
    n16 = l // CMP_STRIDE
    return pl.pallas_call(
        _compress_seq_body,
        grid=(b,),
        in_specs=[pl.BlockSpec((1, l, LANES), lambda i: (i, 0, 0)),
                  pl.BlockSpec(wab.shape, lambda i: (0, 0)),
                  pl.BlockSpec(pe.shape, lambda i: (0, 0))],
        out_specs=pl.BlockSpec((1, n16, LANES), lambda i: (i, 0, 0)),
        out_shape=jax.ShapeDtypeStruct((b, n16, LANES), F32),
        compiler_params=_cparams("parallel"),
        name="nsa_compress_seq",
    )(nkv, wab, pe)


def _softmax_rows(s, valid):
    s = jnp.where(valid, s, NEG_INF)
    m = jnp.max(s, axis=1, keepdims=True)
    m = jnp.where(m == NEG_INF, 0.0, m)
    p = jnp.exp(s - m)
    den = jnp.sum(p, axis=1, keepdims=True)
    return p / jnp.where(den > 0, den, 1.0)


def _tile_mask(ok):
    return jnp.concatenate([jnp.where(ok, 1.0, 0.0)] * N_HEADS, axis=0) > 0.5


def _stack_heads(x):
    return jnp.concatenate([_head(x, h) for h in range(N_HEADS)], axis=0)


def _toeplitz(tab_ref, h, r0, rows, ncol):
    nrow = (rows + ncol) // LANES
    strip = tab_ref[h, pl.ds(r0, nrow), :]
    flat = jnp.concatenate([strip[r:r + 1, :] for r in range(nrow)], axis=1)
    rolled = pltpu.roll(jnp.broadcast_to(flat, (rows, nrow * LANES)), 0, 1, stride=1, stride_axis=0)
    return rolled[:, rows:]


SLC_KC = 512


MASK_BIG = 1e30


def _top_blocks_t(score, k):
    row = lax.broadcasted_iota(jnp.int32, score.shape, 0)
    n = score.shape[0]
    sel = jnp.zeros(score.shape, jnp.bool_)
    for _ in range(k):
        m = jnp.max(score, axis=0, keepdims=True)
        idx = jnp.min(jnp.where(score == m, row, n), axis=0, keepdims=True)
        pick = (row == idx) & (m > NEG_INF)
        sel = sel | pick
        score = jnp.where(pick, NEG_INF, score)
    return sel


def _bucket_index(dist):
    n = jnp.maximum(dist, 0)
    exact = T5_BUCKETS // 2
    big = jnp.full(dist.shape, exact, jnp.int32)
    for thr in _bucket_thresholds():
        big = big + (n >= thr).astype(jnp.int32)
    return jnp.where(n < exact, n, big)


def _gather_bias(rbt_ref, h, bucket):
    rows, width = bucket.shape
    piece = min(width, LANES)
    tab = jnp.broadcast_to(rbt_ref[h:h + 1, 0:piece], (rows, piece))
    parts = [jnp.take_along_axis(tab, bucket[:, s:s + piece], axis=1) for s in range(0, width, piece)]
    return parts[0] if len(parts) == 1 else jnp.concatenate(parts, axis=1)


def _tile_rows(x, n=N_HEADS):
    return jnp.concatenate([x] * n, axis=0)


def _near_chunks():
    return (_bucket_thresholds()[-1] + SLC_KC - 2) // SLC_KC + 1


def _bias_tile(tab_ref, base, ncol, p0, lo, hi):
    r0 = (p0 - base - Q_BLOCK) // LANES
    bias = jnp.concatenate([_toeplitz(tab_ref, h, r0, Q_BLOCK, ncol) for h in range(N_HEADS)], axis=0)
    dist = (base + lax.broadcasted_iota(jnp.int32, (Q_BLOCK, ncol), 0)
            - lax.broadcasted_iota(jnp.int32, (Q_BLOCK, ncol), 1))
    return bias + _tile_rows(jnp.where((dist >= lo) & (dist < hi), 0.0, -MASK_BIG))


def _nsa_seq_body(rb_ref, q_ref, sm_ref, kx_ref, vs_ref, kvw_ref, kvc_ref, tab_ref, rbt_ref, ovlt_ref, y_ref,
                  stile_ref, wtile_ref, *, seq, p0):
    qb = Q_BLOCK
    rows = N_HEADS * qb
    n16 = seq // CMP_STRIDE
    nblk = seq // SLC_BLOCK
    bi = pl.program_id(1)
    q0 = bi * qb

    @pl.when((pl.program_id(0) == 0) & (bi == 0))
    def _():
        per_chunk = SLC_KC // qb

        def slc_tile(t, carry):
            base = (t // _near_chunks()) * qb + (t % _near_chunks()) * SLC_KC
            stile_ref[t] = _bias_tile(tab_ref, base, SLC_KC, p0, 0, seq + SLC_KC)
            return carry

        def win_tile(t, carry):
            wtile_ref[t] = _bias_tile(tab_ref, t * qb, WINDOW + qb, p0, 0, WINDOW)
            return carry

        lax.fori_loop(0, per_chunk * _near_chunks(), slc_tile, 0)
        lax.fori_loop(0, WINDOW // qb + 1, win_tile, 0)
    qs = _stack_heads(q_ref[0]) * SCALE
    qsb = _bf(qs)
    qpos = q0 + lax.broadcasted_iota(jnp.int32, (qb, 1), 0)

    kvc = kvc_ref[0]
    s_c = _dot_nt(qsb, _bf(kvc[:, 0:HEAD_DIM]))
    band = WINDOW + qb
    sw = pl.multiple_of(jnp.maximum(q0 - WINDOW, 0), qb)
    kw = kvw_ref[0, pl.ds(sw, band), 0:HEAD_DIM]
    vw = kvw_ref[0, pl.ds(sw, band), HEAD_DIM:2 * HEAD_DIM]
    s_w = _dot_nt(qsb, kw)

    cmp_end = lax.broadcasted_iota(jnp.int32, (qb, n16), 1) * CMP_STRIDE + (CMP_LEN - 1)
    dist_c = qpos - cmp_end
    bucket_c = _bucket_index(dist_c)
    bias_c = jnp.concatenate([_gather_bias(rbt_ref, h, bucket_c) for h in range(N_HEADS)], axis=0)
    p_c = _softmax_rows(s_c + bias_c, _tile_mask(dist_c >= 0))
    p_cb = _bf(p_c)
    o_c = _dot(p_cb, _bf(kvc[:, HEAD_DIM:2 * HEAD_DIM]))
    p_r = p_cb.astype(F32)
    p_sum = p_r[0:qb] + p_r[qb:2 * qb] + p_r[2 * qb:3 * qb] + p_r[3 * qb:4 * qb]
    imp_t = _dot_nt(ovlt_ref[...], p_sum, precision=HI)

    base_w = q0 - sw
    sb_w = s_w + wtile_ref[base_w // qb]
    e_w = jnp.exp(sb_w - jnp.max(sb_w, axis=1, keepdims=True))
    o_w = _dot(_bf(e_w / jnp.sum(e_w, axis=1, keepdims=True)), vw)

    qlane = q0 + lax.broadcasted_iota(jnp.int32, (1, qb), 1)
    cur = qlane // SLC_BLOCK
    rowb = lax.broadcasted_iota(jnp.int32, (nblk, qb), 0)
    forced = (rowb == 0) | (rowb == cur) | (rowb == cur - 1)
    score = jnp.where(forced | (rowb > cur), NEG_INF, imp_t)
    sel = jnp.where(forced | _top_blocks_t(score, SLC_TOPK - 3), 1.0, 0.0).T
    negm = _bf(_tile_rows((sel - 1.0) * MASK_BIG))
    q_ext = jnp.concatenate([qsb, negm], axis=1)

    thr_far = _bucket_thresholds()[-1] + SLC_KC - 1
    n_chunks = q0 // SLC_KC + 1
    n_far = jnp.maximum(q0 - thr_far + SLC_KC, 0) // SLC_KC
    n_near = _near_chunks()

    far_bias = jnp.concatenate([jnp.full((qb, 1), rb_ref[T5_BUCKETS - 1, h], F32) for h in range(N_HEADS)], axis=0)

    def scores(c):
        k0 = pl.multiple_of(c * SLC_KC, SLC_KC)
        kx = kx_ref[0, pl.ds(k0, SLC_KC), :]
        return q0 - k0, _dot_nt(q_ext, kx), vs_ref[0, pl.ds(k0, SLC_KC), :]

    def online(carry, s, vs):
        m, l, acc = carry
        m_new = jnp.maximum(m, jnp.max(s, axis=1, keepdims=True))
        alpha = jnp.exp(m - m_new)
        p = jnp.exp(s - m_new)
        return m_new, alpha * l + jnp.sum(p, axis=1, keepdims=True), alpha * acc + _dot(_bf(p), vs)

    def far_step(c, carry):
        _, s, vs = scores(c)
        return online(carry, s + far_bias, vs)

    def near_step(c, carry):
        base, s, vs = scores(c)
        tile = ((q0 % SLC_KC) // qb) * n_near + base // SLC_KC
        return online(carry, s + stile_ref[tile], vs)

    init = (jnp.full((rows, 1), -MASK_BIG / 10, F32), jnp.zeros((rows, 1), F32), jnp.zeros((rows, HEAD_DIM), F32))
    carry = lax.fori_loop(0, n_far, far_step, init)
    _, l_s, acc_s = lax.fori_loop(n_far, n_chunks, near_step, carry)
    o_s = acc_s / jnp.where(l_s > 0, l_s, 1.0)

    gates = _sigmoid(sm_ref[0])
    outs = []
    for h in range(N_HEADS):
        rs = slice(h * qb, (h + 1) * qb)
        g = lambda t: gates[:, SM_NG + 3 * h + t:SM_NG + 3 * h + t + 1]
        outs.append(g(0) * o_c[rs] + g(1) * o_s[rs] + g(2) * o_w[rs])
    y_ref[0] = jnp.concatenate(outs, axis=1)


def _nsa_seq(rel_bias, nq, small, kx, vs, kvw, kvc, tab, seq_p0):
    b, l, _ = nq.shape
    n16 = l // CMP_STRIDE
    nblk = l // SLC_BLOCK
    cmp_start = np.arange(n16)[None, :] * CMP_STRIDE
    slc_start = np.arange(nblk)[:, None] * SLC_BLOCK
    ovlt = ((cmp_start < slc_start + SLC_BLOCK) & (cmp_start + CMP_LEN > slc_start)
            & (np.arange(n16)[None, :] < n16 - 1)).astype(np.float32)
    rbt = jnp.pad(rel_bias.T, ((0, SUBLANES - N_HEADS), (0, LANES - T5_BUCKETS)))
    nrows = tab.shape[1]
    const = lambda shape: pl.BlockSpec(shape, lambda i, j: (0,) * len(shape))
    seq_spec = lambda n: pl.BlockSpec((1, l, n), lambda i, j: (i, 0, 0))
    return pl.pallas_call(
        functools.partial(_nsa_seq_body, seq=l, p0=seq_p0),
        grid=(b, l // Q_BLOCK),
        in_specs=[pl.BlockSpec(memory_space=pltpu.SMEM),
                  pl.BlockSpec((1, Q_BLOCK, D_HEADS), lambda i, j: (i, j, 0)),
                  pl.BlockSpec((1, Q_BLOCK, LANES), lambda i, j: (i, j, 0)),
                  seq_spec(HEAD_DIM + nblk), seq_spec(HEAD_DIM), seq_spec(2 * HEAD_DIM),
                  pl.BlockSpec((1, n16, LANES), lambda i, j: (i, 0, 0)),
                  const((SUBLANES, nrows, LANES)), const((SUBLANES, LANES)), const((nblk, n16))],
        out_specs=pl.BlockSpec((1, Q_BLOCK, D_HEADS), lambda i, j: (i, j, 0)),
        out_shape=jax.ShapeDtypeStruct((b, l, D_HEADS), F32),
        scratch_shapes=[pltpu.VMEM(((SLC_KC // Q_BLOCK) * _near_chunks(), N_HEADS * Q_BLOCK, SLC_KC), F32),
                        pltpu.VMEM((WINDOW // Q_BLOCK + 1, N_HEADS * Q_BLOCK, WINDOW + Q_BLOCK), F32)],
        compiler_params=_cparams("arbitrary", "arbitrary"),
        name="nsa_seq",
    )(rel_bias, nq, small, kx, vs, kvw, kvc, tab, rbt, jnp.asarray(ovlt))


def _nsa_tab_seq_dists(seq):
    max_base = max(seq, SLC_KC + (_near_chunks() - 1) * SLC_KC)
    p0 = max_base + Q_BLOCK
    length = p0 + WINDOW + Q_BLOCK + LANES
    return p0, np.maximum(p0 - np.arange(length), 0)


PAGE_GROUP = 16
PAGES_PER_STEP = 128


def _pad_rows(x, rows=SUBLANES):
    return jnp.concatenate([x, jnp.zeros((rows - x.shape[0], x.shape[1]), x.dtype)], axis=0)


def _nsa_select_body(pt_ref, *refs, past, n_steps):
    del pt_ref
    pages = refs[:PAGES_PER_STEP]
    wab_ref, pe_ref, q_ref, tabc_ref, ovlt_ref, oc_ref, top_ref, parts_ref, rows_ref = refs[PAGES_PER_STEP:]
    step = pl.program_id(1)
    blocks_per_page = PAGE_SIZE // CMP_STRIDE
    group_rows = PAGE_GROUP * blocks_per_page
    n_groups = PAGES_PER_STEP // PAGE_GROUP
    for p, pg in enumerate(pages):
        g, pp = divmod(p, PAGE_GROUP)
        rows_ref[g, pp * PAGE_SIZE:(pp + 1) * PAGE_SIZE, :] = pg[...].reshape(2 * HEAD_DIM, PAGE_SIZE).T
    for g in range(n_groups):
        acc = jnp.zeros((group_rows, 2 * LANES), F32)
        for j in range(CMP_STRIDE):
            xj = rows_ref[g, pl.ds(j, group_rows, stride=CMP_STRIDE), :]
            acc = acc + _dot(_bf(xj), wab_ref[j * LANES:(j + 1) * LANES, :])
        row0 = pl.multiple_of((step * n_groups + g) * group_rows, group_rows)
        parts_ref[pl.ds(row0, group_rows), :] = acc

    @pl.when(step == n_steps - 1)
    def _():
        n16 = past // CMP_STRIDE
        n_cmp = n16 - 1
        cur = past // SLC_BLOCK
        pe_term = _pe_term(pe_ref, wab_ref)
        kvc = _combine_cmp(parts_ref[...], pe_term)
        qs = _pad_rows(_stack_heads(q_ref[0]) * SCALE)
        s_c = _dot_nt(_bf(qs), _bf(kvc[:, 0:HEAD_DIM])) + tabc_ref[...]
        lane = lax.broadcasted_iota(jnp.int32, (SUBLANES, n16), 1)
        p_c = _softmax_rows(s_c, lane < n_cmp)
        p_cb = _bf(p_c)
        o_c = _dot(p_cb, _bf(kvc[:, HEAD_DIM:2 * HEAD_DIM]))
        oc_ref[0] = jnp.concatenate([o_c[h:h + 1, :] for h in range(N_HEADS)], axis=1)
        imp_h = _dot_nt(ovlt_ref[...], p_cb)
        imp = imp_h[:, 0:1] + imp_h[:, 1:2] + imp_h[:, 2:3] + imp_h[:, 3:4]
        nrow = imp.shape[0]
        rowb = lax.broadcasted_iota(jnp.int32, (nrow, 1), 0)
        forced_ids = (0, cur - 1, cur)
        forced = (rowb == 0) | (rowb == cur) | (rowb == cur - 1)
        score = jnp.where(forced | (rowb > cur), NEG_INF, imp)
        out_lane = lax.broadcasted_iota(jnp.int32, (1, LANES), 1)
        top = jnp.zeros((1, LANES), jnp.int32)
        for r, blk in enumerate(forced_ids):
            top = jnp.where(out_lane == r, blk, top)
        for r in range(len(forced_ids), SLC_TOPK):
            m = jnp.max(score, axis=0, keepdims=True)
            idx = jnp.min(jnp.where(score == m, rowb, nrow), axis=0, keepdims=True)
            top = jnp.where(out_lane == r, idx, top)
            score = jnp.where(rowb == idx, NEG_INF, score)
        top_ref[0] = top


def _nsa_select(layer, page_table, cache_t, wab_bf, pe, nq, tabc, past):
    b, n_pages = page_table.shape
    n_steps = n_pages // PAGES_PER_STEP
    n16 = past // CMP_STRIDE
    n_slc = past // SLC_BLOCK + 1
    nrow = -(-n_slc // SUBLANES) * SUBLANES
    cmp_start = np.arange(n16)[None, :] * CMP_STRIDE
    slc_start = np.arange(nrow)[:, None] * SLC_BLOCK
    ovlt = ((cmp_start < slc_start + SLC_BLOCK) & (cmp_start + CMP_LEN > slc_start)
            & (np.arange(n16)[None, :] < n16 - 1) & (np.arange(nrow)[:, None] < n_slc)).astype(np.float32)

    n_pool = cache_t.shape[1]

    def page_spec(p):
        def imap(i, s, pt):
            pg = pt[jnp.minimum(i, b - 1), jnp.minimum(s, n_steps - 1) * PAGES_PER_STEP + p]
            return (layer, jnp.minimum(pg, n_pool - 1), 0, 0, 0)
        return pl.BlockSpec((None, None, 2, HEAD_DIM, PAGE_SIZE), imap)

    const = lambda shape: pl.BlockSpec(shape, lambda i, s, pt: (0,) * len(shape))
    grid_spec = pltpu.PrefetchScalarGridSpec(
        num_scalar_prefetch=1,
        grid=(b, n_steps),
        in_specs=[page_spec(p) for p in range(PAGES_PER_STEP)]
        + [const(wab_bf.shape), const(pe.shape), pl.BlockSpec((1, 1, D_HEADS), lambda i, s, pt: (i, 0, 0)),
           const(tabc.shape), const(ovlt.shape)],
        out_specs=[pl.BlockSpec((1, 1, D_HEADS), lambda i, s, pt: (i, 0, 0)),
                   pl.BlockSpec((1, 1, LANES), lambda i, s, pt: (i, 0, 0))],
        scratch_shapes=[pltpu.VMEM((n16, 2 * LANES), F32),
                        pltpu.VMEM((PAGES_PER_STEP // PAGE_GROUP, PAGE_GROUP * PAGE_SIZE, 2 * HEAD_DIM), F32)],
    )
    return pl.pallas_call(
        functools.partial(_nsa_select_body, past=past, n_steps=n_steps),
        grid_spec=grid_spec,
        out_shape=[jax.ShapeDtypeStruct((b, 1, D_HEADS), F32), jax.ShapeDtypeStruct((b, 1, LANES), jnp.int32)],
        compiler_params=_cparams("parallel", "arbitrary"),
        name="nsa_select",
    )(page_table, *([cache_t] * PAGES_PER_STEP), wab_bf, pe, nq, tabc, jnp.asarray(ovlt, BF16))


def _nsa_attend_body(pt_ref, top_ref, *refs, past):
    del pt_ref
    pages = refs[:SLC_TOPK]
    (q_ref, sm_ref, new_ref, oc_ref, swa_ref, tabs_ref, tabw_ref, y_ref, swa_out) = refs[SLC_TOPK:]
    b = pl.program_id(0)
    new_blk = past // SLC_BLOCK
    blocks_per_page = PAGE_SIZE // SLC_BLOCK
    qs = _bf(_pad_rows(_stack_heads(q_ref[0]) * SCALE))
    new = new_ref[0]
    lane_t = lax.broadcasted_iota(jnp.int32, (HEAD_DIM, PAGE_SIZE), 1)
    lane = lax.broadcasted_iota(jnp.int32, (SUBLANES, PAGE_SIZE), 1)
    new_kt = jnp.where(lane_t == 0, _row_to_col(new[:, 2 * HEAD_DIM:3 * HEAD_DIM]), 0.0)
    new_vt = jnp.where(lane_t == 0, _row_to_col(new[:, 3 * HEAD_DIM:4 * HEAD_DIM]), 0.0)
    kts, biases, oks, vts = [], [], [], []
    for s in range(SLC_TOPK):
        ti = jnp.clip(top_ref[b, s], 0, new_blk)
        is_new = ti == new_blk
        kts.append(jnp.where(is_new, new_kt, pages[s][0]))
        vts.append(jnp.where(is_new, new_vt, pages[s][1]))
        biases.append(tabs_ref[ti])
        kpos = ti * SLC_BLOCK + lane % SLC_BLOCK
        ok = (lane // SLC_BLOCK == ti % blocks_per_page) & (kpos <= past)
        oks.append(jnp.where(ok, 1.0, 0.0))
    wlane = lax.broadcasted_iota(jnp.int32, (HEAD_DIM, WINDOW), 1)
    win = []
    for c in range(2):
        col = _row_to_col(new[:, (4 + c) * HEAD_DIM:(5 + c) * HEAD_DIM])
        win.append(jnp.where(wlane == WINDOW - 1, col, pltpu.roll(swa_ref[c], WINDOW - 1, 1)))
    s_w = _dot(qs, _bf(win[0])) + tabw_ref[...]

    s_s = _dot(qs, _bf(jnp.concatenate(kts, axis=1))) + jnp.concatenate(biases, axis=1)
    p_s = _softmax_rows(s_s, jnp.concatenate(oks, axis=1) > 0.5)
    p_w = _softmax_rows(s_w, jnp.full(s_w.shape, True))
    o_s = _dot_nt(_bf(p_s), _bf(jnp.concatenate(vts, axis=1)))
    o_w = _dot_nt(_bf(p_w), _bf(win[1]))
    for c in range(2):
        swa_out[c] = win[c]

    gates = _sigmoid(sm_ref[0])
    o_c = oc_ref[0]
    outs = []
    for h in range(N_HEADS):
        g = lambda t: gates[:, SM_NG + 3 * h + t:SM_NG + 3 * h + t + 1]
        outs.append(g(0) * _head(o_c, h) + g(1) * o_s[h:h + 1, :] + g(2) * o_w[h:h + 1, :])
    y_ref[0] = jnp.concatenate(outs, axis=1)


def _nsa_attend(layer, page_table, top, cache_t, nq, small, nkv, o_c, swa_t, tabs, tabw, past):
    b = nq.shape[0]
    last_page = past // PAGE_SIZE - 1
    blocks_per_page = PAGE_SIZE // SLC_BLOCK

    n_pool = cache_t.shape[1]

    def page_spec(s):
        def imap(i, pt, tp):
            ii = jnp.minimum(i, b - 1)
            pg = jnp.clip(tp[ii, s] // blocks_per_page, 0, last_page)
            return (layer, jnp.clip(pt[ii, pg], 0, n_pool - 1), 1, 0, 0)
        return pl.BlockSpec((None, None, 2, HEAD_DIM, PAGE_SIZE), imap)

    row = lambda n: pl.BlockSpec((1, 1, n), lambda i, pt, tp: (i, 0, 0))
    const = lambda shape: pl.BlockSpec(shape, lambda i, pt, tp: (0,) * len(shape))
    grid_spec = pltpu.PrefetchScalarGridSpec(
        num_scalar_prefetch=2,
        grid=(b,),
        in_specs=[page_spec(s) for s in range(SLC_TOPK)]
        + [row(D_HEADS), row(LANES), row(6 * HEAD_DIM), row(D_HEADS),
           pl.BlockSpec((None, None, 2, HEAD_DIM, WINDOW), lambda i, pt, tp: (layer, i, 0, 0, 0)),
           const(tabs.shape), const(tabw.shape)],
        out_specs=[row(D_HEADS), pl.BlockSpec((None, 2, HEAD_DIM, WINDOW), lambda i, pt, tp: (i, 0, 0, 0))],
    )
    return pl.pallas_call(
        functools.partial(_nsa_attend_body, past=past),
        grid_spec=grid_spec,
        out_shape=[jax.ShapeDtypeStruct((b, 1, D_HEADS), F32),
                   jax.ShapeDtypeStruct((b, 2, HEAD_DIM, WINDOW), F32)],
        compiler_params=_cparams("parallel"),
        name="nsa_attend",
    )(page_table, top, *([cache_t] * SLC_TOPK), nq, small, nkv, o_c, swa_t, tabs, tabw)


SEQ_CHUNK = 64
GDN_CHUNKS = 1
SCAN_CHUNK = 256
ROW_TILE = 512
FFN_TILE = 512


def _lane_vec(pairs):
    v = jnp.zeros((1, LANES), F32)
    for off, vals in pairs:
        v = v.at[0, off:off + vals.shape[0]].set(vals.astype(F32))
    return v


def kernel(x_prompt, x_sample, cache_nsa_kv, cache_swa_kv, state_ret, state_mlstm_C, state_mlstm_n, state_mlstm_m, state_gdn, state_gdn_conv, state_ffn_conv, page_table, rel_bias, norm_pre_mix, w_in, mlstm_b_i, mlstm_b_f, gdn_conv_w, gdn_A_log, gdn_dt_bias, nsa_cmp_pe, nsa_cmp_w, ret_norm, mlstm_norm, gdn_norm, w_out, norm_post_mix, norm_pre_ffn, w_ffn_gate, w_ffn_up, ffn_conv_w, w_ffn_down, norm_post_ffn):
    depth = w_in.shape[0]
    bp, seq, _ = x_prompt.shape
    bs, dec_seq, _ = x_sample.shape
    assert dec_seq == 1 and seq % SLC_KC == 0 and seq >= WINDOW + Q_BLOCK
    n_pool = cache_nsa_kv.shape[1]
    past = page_table.shape[1] * PAGE_SIZE
    assert past >= WINDOW and past % SLC_BLOCK == 0 and page_table.shape[1] % PAGES_PER_STEP == 0
    assert past // SLC_BLOCK + 1 >= SLC_TOPK

    p0, d_seq = _nsa_tab_seq_dists(seq)
    n16 = past // CMP_STRIDE
    n_slc = past // SLC_BLOCK + 1
    d_cmp = np.maximum(past - (np.arange(n16) * CMP_STRIDE + CMP_LEN - 1), 0)
    d_slc = np.maximum(past - np.arange(n_slc * SLC_BLOCK), 0)
    d_win = WINDOW - 1 - np.arange(WINDOW)
    tab = _bias_tables(rel_bias, np.concatenate([d_seq, d_cmp, d_slc, d_win]))
    o1, o2, o3 = len(d_seq), len(d_seq) + len(d_cmp), len(d_seq) + len(d_cmp) + len(d_slc)
    tab_seq = tab[:, :o1].reshape(SUBLANES, o1 // LANES, LANES)
    tabc = tab[:, o1:o2]
    tabs = jnp.transpose(tab[:, o2:o3].reshape(SUBLANES, n_slc, SLC_BLOCK), (1, 0, 2))
    tabs = jnp.concatenate([tabs, tabs], axis=2)
    tabw = tab[:, o3:]

    cache_t = jnp.transpose(cache_nsa_kv, (0, 1, 3, 4, 2))
    swa_t = jnp.transpose(cache_swa_kv, (0, 1, 3, 4, 2))
    cos_p, sin_p = _rope_tables(jnp.arange(seq))
    cos_s, sin_s = _rope_tables(jnp.arange(past, past + 1))

    xp = x_prompt
    xs = x_sample.reshape(bs, D_MODEL)
    p_states, s_states = [], []
    for l in range(depth):
        w_cat = _prep_w_in(w_in[l])
        w_out_bf = _bf(w_out[l])
        wg, wu, wd = _bf(w_ffn_gate[l]), _bf(w_ffn_up[l]), _bf(w_ffn_down[l])
        wab, pe = _prep_cmp_w(nsa_cmp_w[l], nsa_cmp_pe[l])
        ml_bias = _lane_vec([(SM_MI, mlstm_b_i[l]), (SM_MF, mlstm_b_f[l])])
        dtb = _lane_vec([(SM_GA, gdn_dt_bias[l])])
        alog = _lane_vec([(SM_GA, gdn_A_log[l])])

        z = _in_proj(xp.reshape(bp * seq, D_MODEL), norm_pre_mix[l], w_cat, ROW_TILE, seq=seq)
        z_ret, z_ml, z_gdn, z_nq, z_nkv, z_sm = [t.reshape(bp, seq, -1) for t in z[:6]]
        nkv_t = z[6].reshape(bp, 6, HEAD_DIM, seq)
        kx, vs, kvw = [t.reshape(bp, seq, -1) for t in z[7:]]
        y_ret, st_ret = _retention_seq(z_ret, cos_p, sin_p, ret_norm[l], SCAN_CHUNK, bp)
        y_ml, st_c, st_n, st_m = _mlstm_seq(z_ml, z_sm, ml_bias, mlstm_norm[l], SCAN_CHUNK, bp)
        y_gdn, st_g = _gdn_seq(z_gdn, z_sm, gdn_conv_w[l], dtb, alog, gdn_norm[l], SEQ_CHUNK, bp, GDN_CHUNKS)
        kvc = _compress_seq(z_nkv, wab, pe)
        y_nsa = _nsa_seq(rel_bias, z_nq, z_sm, kx, vs, kvw, kvc, tab_seq, p0)
        xp, tail = _ffn_seq(xp, (y_ret, y_ml, y_gdn, y_nsa), w_out_bf, norm_post_mix[l], norm_pre_ffn[l],
                            wg, wu, wd, ffn_conv_w[l], norm_post_ffn[l], min(seq, FFN_TILE))
        p_states.append((
            jnp.transpose(nkv_t[:, 0:4], (0, 3, 1, 2)),
            jnp.transpose(nkv_t[:, 4:6, :, seq - WINDOW:], (0, 3, 1, 2)),
            st_ret, st_c, st_n, st_m[:, 0, :N_HEADS], st_g,
            z_gdn[:, seq - (GDN_CONV - 1):, 0:3 * D_HEADS],
            tail[:, SUBLANES - (FFN_CONV - 1):, :]))

        z = _in_proj(xs, norm_pre_mix[l], w_cat, bs)
        z_ret, z_ml, z_gdn, z_nq, z_nkv, z_sm = [t.reshape(bs, 1, -1) for t in z]
        m_pad = jnp.pad(state_mlstm_m[l], ((0, 0), (0, LANES - N_HEADS))).reshape(bs, 1, LANES)
        (y_ret, y_ml, y_gdn, st_ret, st_c, st_n, st_m, st_g, st_gc) = _mixers_step(
            z_ret, z_ml, z_gdn, z_sm, cos_s, sin_s, ml_bias, dtb, alog, gdn_conv_w[l],
            ret_norm[l], mlstm_norm[l], gdn_norm[l],
            state_ret[l], state_mlstm_C[l], state_mlstm_n[l], m_pad, state_gdn[l], state_gdn_conv[l])
        o_c, top = _nsa_select(l, page_table, cache_t, _bf(wab), pe, z_nq, tabc, past)
        y_nsa, swa_new = _nsa_attend(l, page_table, top[:, 0, :SLC_TOPK], cache_t, z_nq, z_sm, z_nkv, o_c,
                                     swa_t, tabs, tabw, past)
        ys = [t.reshape(bs, D_HEADS) for t in (y_ret, y_ml, y_gdn, y_nsa)]
        x1, h2 = _out_proj(xs, ys, w_out_bf, norm_post_mix[l], norm_pre_ffn[l], bs)
        xs, g_new = _ffn_step(x1, h2, state_ffn_conv[l][:, 0], state_ffn_conv[l][:, 1], wg, wu, wd,
                              ffn_conv_w[l], norm_post_ffn[l])
        s_states.append((
            z_nkv[..., 0:4 * HEAD_DIM].reshape(bs, 1, 4, HEAD_DIM),
            jnp.transpose(swa_new, (0, 3, 1, 2)),
            st_ret, st_c, st_n, st_m[:, 0, :N_HEADS], st_g, st_gc,
            jnp.concatenate([state_ffn_conv[l][:, 1:], g_new[:, None, :]], axis=1)))

    stack = lambda states: tuple(jnp.stack([s[k] for s in states]) for k in range(len(states[0])))
    return (xp, xs.reshape(bs, 1, D_MODEL)) + stack(p_states) + stack(s_states)
```

```python
import functools
import math

import jax
import jax.numpy as jnp
import numpy as np
from jax import lax
from jax.experimental import pallas as pl
from jax.experimental.pallas import tpu as pltpu

F32 = jnp.float32
BF16 = jnp.bfloat16
HI = lax.Precision.HIGHEST

D_MODEL = 1024
HEAD_DIM = 64
N_HEADS = 4
D_HEADS = N_HEADS * HEAD_DIM
D_FF = 2816
GDN_CONV = 4
FFN_CONV = 3
PAGE_SIZE = 128
CMP_STRIDE = 16
CMP_LEN = 32
SLC_BLOCK = 64
SLC_TOPK = 16
WINDOW = 512
Q_BLOCK = 128
T5_BUCKETS = 32
T5_MAX_DIST = 1024
ROPE_BASE = 10000.0
EPS = 1e-6
SCALE = HEAD_DIM ** -0.5
IN_WIDTHS = (256, 256, 256, 256, 256, 256, 256, 4, 4, 256, 768, 4, 4, 256, 256, 384, 12)
LANES = 128
SUBLANES = 8
VMEM_LIMIT = 56 * 1024 * 1024
NEG_INF = float("-inf")

SM_MI, SM_MF, SM_GA, SM_GB, SM_NG = 0, 4, 8, 12, 16


def _cparams(*sem):
    return pltpu.CompilerParams(dimension_semantics=sem, vmem_limit_bytes=VMEM_LIMIT)


def _rms(x, g):
    return x * lax.rsqrt(jnp.mean(x * x, axis=-1, keepdims=True) + EPS) * g


def _dot(a, b, **kw):
    return jnp.dot(a, b, preferred_element_type=F32, **kw)


def _dot_nt(a, b, **kw):
    return lax.dot_general(a, b, (((1,), (1,)), ((), ())), preferred_element_type=F32, **kw)


def _dot_tn(a, b, **kw):
    return lax.dot_general(a, b, (((0,), (0,)), ((), ())), preferred_element_type=F32, **kw)


def _bf(x):
    return x.astype(BF16)


def _sigmoid(x):
    return 1.0 / (1.0 + jnp.exp(-x))


def _silu(x):
    return x * _sigmoid(x)


def _softplus(x):
    return jnp.maximum(x, 0.0) + jnp.log(1.0 + jnp.exp(-jnp.abs(x)))


def _head_rms(o):
    return o * lax.rsqrt(jnp.mean(o * o, axis=-1, keepdims=True) + EPS)


PROJ_WIDTHS = (1024, 1024, 1024, 256, 384, 128)


def _prep_w_in(w):
    wt = w.T
    parts, off = [], 0
    for wd in IN_WIDTHS:
        parts.append(wt[off:off + wd])
        off += wd
    (rq, rk, rv, rg, mq, mk, mv, mi, mf, mo, gqkv, ga, gbeta, gg, nq, nkv, ngate) = parts
    small = jnp.concatenate([mi, mf, ga, gbeta, ngate], axis=0)
    small = jnp.pad(small, ((0, LANES - small.shape[0]), (0, 0)))
    cat = jnp.concatenate([rq, rk, rv, rg, mq, mk, mv, mo, gqkv, gg, nq, nkv, small], axis=0)
    return cat.astype(BF16)


NKV_OFF = sum(PROJ_WIDTHS[:4])


def _in_proj_body(x_ref, g_ref, wt_ref, *out_refs, seq):
    h = _bf(_rms(x_ref[...], g_ref[...]))
    off = 0
    for ref in out_refs[:len(PROJ_WIDTHS)]:
        n = ref.shape[-1]
        ref[...] = _dot_nt(h, wt_ref[off:off + n, :])
        off += n
    if seq is not None:
        kvt_ref, kx_ref, vs_ref, kvw_ref = out_refs[len(PROJ_WIDTHS):]
        kvt_ref[0] = _dot_nt(wt_ref[NKV_OFF:NKV_OFF + PROJ_WIDTHS[4], :], h)
        nkv = out_refs[4][...]
        tm = nkv.shape[0]
        nblk = kx_ref.shape[-1] - HEAD_DIM
        pos = (pl.program_id(0) * tm) % seq + lax.broadcasted_iota(jnp.int32, (tm, nblk), 0)
        block_id = jnp.where(pos // SLC_BLOCK == lax.broadcasted_iota(jnp.int32, (tm, nblk), 1), 1.0, 0.0)
        kx_ref[...] = _bf(jnp.concatenate([nkv[:, 2 * HEAD_DIM:3 * HEAD_DIM], block_id], axis=1))
        vs_ref[...] = _bf(nkv[:, 3 * HEAD_DIM:4 * HEAD_DIM])
        kvw_ref[...] = _bf(nkv[:, 4 * HEAD_DIM:6 * HEAD_DIM])


def _in_proj(x2d, g, w_cat_t, tm, seq=None):
    m = x2d.shape[0]
    ntot = sum(PROJ_WIDTHS)
    const = lambda shape: pl.BlockSpec(shape, lambda i: (0,) * len(shape), pipeline_mode=pl.Buffered(1))
    out_specs = [pl.BlockSpec((tm, n), lambda i: (i, 0)) for n in PROJ_WIDTHS]
    out_shape = [jax.ShapeDtypeStruct((m, n), F32) for n in PROJ_WIDTHS]
    if seq is not None:
        per_seq = seq // tm
        nkv = PROJ_WIDTHS[4]
        out_specs.append(pl.BlockSpec((1, nkv, tm), lambda i: (i // per_seq, 0, i % per_seq)))
        out_shape.append(jax.ShapeDtypeStruct((m // seq, nkv, seq), F32))
        for n in (HEAD_DIM + seq // SLC_BLOCK, HEAD_DIM, 2 * HEAD_DIM):
            out_specs.append(pl.BlockSpec((tm, n), lambda i: (i, 0)))
            out_shape.append(jax.ShapeDtypeStruct((m, n), BF16))
    return pl.pallas_call(
        functools.partial(_in_proj_body, seq=seq),
        grid=(m // tm,),
        in_specs=[pl.BlockSpec((tm, D_MODEL), lambda i: (i, 0)), const((1, D_MODEL)), const((ntot, D_MODEL))],
        out_specs=out_specs,
        out_shape=out_shape,
        compiler_params=_cparams("parallel"),
        name="in_proj",
    )(x2d, g.reshape(1, D_MODEL), w_cat_t)


def _out_proj_body(x_ref, y0, y1, y2, y3, w_ref, gpost_ref, gpre_ref, x1_ref, h2_ref):
    acc = None
    for k, y in enumerate((y0, y1, y2, y3)):
        t = _dot(_bf(y[...]), w_ref[k * D_HEADS:(k + 1) * D_HEADS, :])
        acc = t if acc is None else acc + t
    x1 = x_ref[...] + _rms(acc, gpost_ref[...])
    x1_ref[...] = x1
    h2_ref[...] = _bf(_rms(x1, gpre_ref[...]))


def _out_proj(x2d, ys, w_out_bf, gpost, gpre, tm):
    m = x2d.shape[0]
    row = lambda n: pl.BlockSpec((tm, n), lambda i: (i, 0))
    vec = pl.BlockSpec((1, D_MODEL), lambda i: (0, 0))
    return pl.pallas_call(
        _out_proj_body,
        grid=(m // tm,),
        in_specs=[row(D_MODEL)] + [row(D_HEADS)] * 4
        + [pl.BlockSpec((D_MODEL, D_MODEL), lambda i: (0, 0)), vec, vec],
        out_specs=[row(D_MODEL), row(D_MODEL)],
        out_shape=[jax.ShapeDtypeStruct((m, D_MODEL), F32), jax.ShapeDtypeStruct((m, D_MODEL), BF16)],
        compiler_params=_cparams("parallel"),
        name="out_proj",
    )(x2d, *ys, w_out_bf, gpost.reshape(1, -1), gpre.reshape(1, -1))


FFN_TN = 256


def _gelu_tanh(x):
    c = 0.7978845608028654
    hx = 0.5 * x
    return hx + hx * jnp.tanh(x * (c + (0.044715 * c) * (x * x)))


def _ffn_seq_body(x_ref, y0, y1, y2, y3, wo_ref, gmix_ref, gpre_ref, wg_ref, wu_ref, wd_ref, cw_ref, gpost_ref,
                  x2_ref, tail_ref, carry_ref, act_ref, *, tm):
    @pl.when(pl.program_id(1) == 0)
    def _():
        carry_ref[...] = jnp.zeros_like(carry_ref)

    mix = None
    for k, y in enumerate((y0, y1, y2, y3)):
        t = _dot(_bf(y[0]), wo_ref[k * D_HEADS:(k + 1) * D_HEADS, :])
        mix = t if mix is None else mix + t
    x1 = x_ref[0] + _rms(mix, gmix_ref[...])
    h2 = _bf(_rms(x1, gpre_ref[...]))
    row8 = lax.broadcasted_iota(jnp.int32, (SUBLANES, FFN_TN), 0)
    starts = list(range(0, D_FF, FFN_TN))
    up = lambda n0: (_dot(h2, wg_ref[:, n0:n0 + FFN_TN]), _dot(h2, wu_ref[:, n0:n0 + FFN_TN]))
    nxt = up(starts[0])
    for i, n0 in enumerate(starts):
        g, u = nxt
        if i + 1 < len(starts):
            nxt = up(starts[i + 1])
        prev = carry_ref[:, n0:n0 + FFN_TN]
        g1, g2 = pltpu.roll(g, 1, 0), pltpu.roll(g, 2, 0)
        top1 = jnp.where(row8 == 0, prev[7:8, :], g1[0:SUBLANES])
        top2 = jnp.where(row8 == 0, prev[6:7, :], jnp.where(row8 == 1, prev[7:8, :], g2[0:SUBLANES]))
        g1 = jnp.concatenate([top1, g1[SUBLANES:]], axis=0)
        g2 = jnp.concatenate([top2, g2[SUBLANES:]], axis=0)
        cw = cw_ref[:, n0:n0 + FFN_TN]
        gate = cw[0:1, :] * g2 + cw[1:2, :] * g1 + cw[2:3, :] * g
        a = _gelu_tanh(gate) * u
        act_ref[:, n0:n0 + FFN_TN] = _bf(a)
        carry_ref[:, n0:n0 + FFN_TN] = g[tm - SUBLANES:tm, :]
    acc = _dot(act_ref[...], wd_ref[...])
    x2_ref[0] = x1 + _rms(acc, gpost_ref[...])
    tail_ref[0] = carry_ref[...]


def _ffn_seq(x, ys, w_out_bf, gmix, gpre, wg, wu, wd, cw, gpost, tm):
    b, l, _ = x.shape
    const = lambda shape: pl.BlockSpec(shape, lambda i, j: (0,) * len(shape), pipeline_mode=pl.Buffered(1))
    row = lambda n: pl.BlockSpec((1, tm, n), lambda i, j: (i, j, 0))
    return pl.pallas_call(
        functools.partial(_ffn_seq_body, tm=tm),
        grid=(b, l // tm),
        in_specs=[row(D_MODEL)] + [row(D_HEADS)] * 4
        + [const((D_MODEL, D_MODEL)), const((1, D_MODEL)), const((1, D_MODEL)),
           const((D_MODEL, D_FF)), const((D_MODEL, D_FF)), const((D_FF, D_MODEL)),
           const((FFN_CONV, D_FF)), const((1, D_MODEL))],
        out_specs=[pl.BlockSpec((1, tm, D_MODEL), lambda i, j: (i, j, 0)),
                   pl.BlockSpec((1, SUBLANES, D_FF), lambda i, j: (i, 0, 0))],
        out_shape=[jax.ShapeDtypeStruct((b, l, D_MODEL), F32),
                   jax.ShapeDtypeStruct((b, SUBLANES, D_FF), F32)],
        scratch_shapes=[pltpu.VMEM((SUBLANES, D_FF), F32), pltpu.VMEM((tm, D_FF), BF16)],
        compiler_params=_cparams("parallel", "arbitrary"),
        name="ffn_seq",
    )(x, *ys, w_out_bf, gmix.reshape(1, -1), gpre.reshape(1, -1), wg, wu, wd, cw, gpost.reshape(1, -1))


def _ffn_step_body(x1_ref, h2_ref, b0_ref, b1_ref, wg_ref, wu_ref, wd_ref, cw_ref, gpost_ref, x2_ref, g_ref):
    h2 = h2_ref[...]
    acc = jnp.zeros(x1_ref.shape, F32)
    for n0 in range(0, D_FF, FFN_TN):
        sl = slice(n0, n0 + FFN_TN)
        g = _dot(h2, wg_ref[:, sl])
        cw = cw_ref[:, sl]
        gate = cw[0:1, :] * b0_ref[:, sl] + cw[1:2, :] * b1_ref[:, sl] + cw[2:3, :] * g
        u = _dot(h2, wu_ref[:, sl])
        acc = acc + _dot(_bf(_gelu_tanh(gate) * u), wd_ref[sl, :])
        g_ref[:, sl] = g
    x2_ref[...] = x1_ref[...] + _rms(acc, gpost_ref[...])


def _ffn_step(x1, h2, b0, b1, wg, wu, wd, cw, gpost):
    m = x1.shape[0]
    return pl.pallas_call(
        _ffn_step_body,
        out_shape=[jax.ShapeDtypeStruct((m, D_MODEL), F32), jax.ShapeDtypeStruct((m, D_FF), F32)],
        compiler_params=pltpu.CompilerParams(vmem_limit_bytes=VMEM_LIMIT),
        name="ffn_step",
    )(x1, h2, b0, b1, wg, wu, wd, cw, gpost.reshape(1, -1))


def _rope_tables(pos):
    half = HEAD_DIM // 2
    inv = ROPE_BASE ** (-jnp.linspace(0.0, 1.0, half, dtype=F32))
    ang = pos.astype(F32)[:, None] * inv[None, :]
    return jnp.tile(jnp.cos(ang), (1, LANES // half)), jnp.tile(jnp.sin(ang), (1, LANES // half))


def _rope128(x, cos, sin):
    lane = lax.broadcasted_iota(jnp.int32, x.shape, 1)
    first = (lane % HEAD_DIM) < (HEAD_DIM // 2)
    other = jnp.where(first, -pltpu.roll(x, LANES - HEAD_DIM // 2, 1), pltpu.roll(x, HEAD_DIM // 2, 1))
    return x * cos + other * sin


def _rope256(x, cos, sin):
    return jnp.concatenate([_rope128(x[:, :LANES], cos, sin), _rope128(x[:, LANES:], cos, sin)], axis=1)


def _ret_log_decay(h):
    return math.log(1.0 - 2.0 ** (-5.0 - h))


def _head(x, h):
    return x[:, h * HEAD_DIM:(h + 1) * HEAD_DIM]


def _ret_body(z_ref, cos_ref, sin_ref, nw_ref, y_ref, s_ref, *, chunk):
    @pl.when(pl.program_id(1) == 0)
    def _():
        s_ref[...] = jnp.zeros_like(s_ref)

    nb = z_ref.shape[0]
    cos, sin = cos_ref[...], sin_ref[...]
    t = lax.broadcasted_iota(jnp.int32, (chunk, chunk), 0)
    s = lax.broadcasted_iota(jnp.int32, (chunk, chunk), 1)
    causal = t >= s
    diff = jnp.where(causal, t - s, 0).astype(F32)
    tcol = lax.broadcasted_iota(jnp.int32, (chunk, 1), 0).astype(F32)
    lgs = [_ret_log_decay(h) for h in range(N_HEADS)]
    dmat = [jnp.where(causal, jnp.exp(lg * diff), 0.0) for lg in lgs]
    xi = [jnp.exp(lg * (tcol + 1.0)) for lg in lgs]
    zeta = [jnp.exp(lg * (chunk - 1.0 - tcol)) for lg in lgs]
    probs = [(bb, h) for bb in range(nb) for h in range(N_HEADS)]
    qkv = []
    for bb in range(nb):
        q = _rope256(z_ref[bb, :, 0:256], cos, sin)
        k = _rope256(z_ref[bb, :, 256:512], cos, sin) * SCALE
        qkv.append((q, k, z_ref[bb, :, 512:768]))
    qh = [_head(qkv[bb][0], h) for bb, h in probs]
    kh = [_head(qkv[bb][1], h) for bb, h in probs]
    vh = [_bf(_head(qkv[bb][2], h)) for bb, h in probs]
    st = [s_ref[bb, h] for bb, h in probs]
    sc = [_dot_nt(_bf(q), _bf(k)) for q, k in zip(qh, kh)]
    cross = [_dot(_bf(q * xi[h]), _bf(s0)) for q, s0, (_, h) in zip(qh, st, probs)]
    upd = [_dot_tn(_bf(k * zeta[h]), v) for k, v, (_, h) in zip(kh, vh, probs)]
    o = [_dot(_bf(x * dmat[h]), v) + c for x, v, c, (_, h) in zip(sc, vh, cross, probs)]
    for i, (bb, h) in enumerate(probs):
        s_ref[bb, h] = st[i] * math.exp(lgs[h] * chunk) + upd[i]
    for bb in range(nb):
        outs = [_head_rms(o[bb * N_HEADS + h]) for h in range(N_HEADS)]
        y_ref[bb] = jnp.concatenate(outs, axis=1) * nw_ref[...] * _silu(z_ref[bb, :, 768:1024])


def _retention_seq(z_ret, cos, sin, nw, chunk, nb):
    b, l, _ = z_ret.shape
    return pl.pallas_call(
        functools.partial(_ret_body, chunk=chunk),
        grid=(b // nb, l // chunk),
        in_specs=[pl.BlockSpec((nb, chunk, 1024), lambda i, j: (i, j, 0)),
                  pl.BlockSpec((chunk, LANES), lambda i, j: (j, 0)),
                  pl.BlockSpec((chunk, LANES), lambda i, j: (j, 0)),
                  pl.BlockSpec((1, D_HEADS), lambda i, j: (0, 0))],
        out_specs=[pl.BlockSpec((nb, chunk, D_HEADS), lambda i, j: (i, j, 0)),
                   pl.BlockSpec((nb, N_HEADS, HEAD_DIM, HEAD_DIM), lambda i, j: (i, 0, 0, 0))],
        out_shape=[jax.ShapeDtypeStruct((b, l, D_HEADS), F32),
                   jax.ShapeDtypeStruct((b, N_HEADS, HEAD_DIM, HEAD_DIM), F32)],
        compiler_params=_cparams("parallel", "arbitrary"),
        name="retention_seq",
    )(z_ret, cos, sin, nw.reshape(1, -1))


def _lanes_to_rows(x, n=2 * SUBLANES):
    sel = (lax.broadcasted_iota(jnp.int32, (n, x.shape[1]), 0)
           == lax.broadcasted_iota(jnp.int32, (n, x.shape[1]), 1)).astype(F32)
    return _dot_nt(sel, x, precision=HI)


def _tri_incl(n):
    t = lax.broadcasted_iota(jnp.int32, (n, n), 0)
    s = lax.broadcasted_iota(jnp.int32, (n, n), 1)
    return t >= s


def _ml_body(z_ref, sm_ref, bias_ref, nw_ref, y_ref, c_ref, n_ref, m_ref, *, chunk):
    @pl.when(pl.program_id(1) == 0)
    def _():
        c_ref[...] = jnp.zeros_like(c_ref)
        n_ref[...] = jnp.zeros_like(n_ref)
        m_ref[...] = jnp.zeros_like(m_ref)

    nb = z_ref.shape[0]
    incl = _tri_incl(chunk)
    tri = incl.astype(F32)
    probs = [(bb, h) for bb in range(nb) for h in range(N_HEADS)]
    gate = []
    for bb in range(nb):
        pre = sm_ref[bb] + bias_ref[...]
        logf = jnp.minimum(pre, 0.0) - jnp.log(1.0 + jnp.exp(-jnp.abs(pre)))
        bcum = _dot(tri, logf, precision=HI)
        gate.append((pre, bcum, _lanes_to_rows(pre), _lanes_to_rows(bcum), m_ref[bb]))
    qh = [_head(z_ref[bb, :, 0:256], h) for bb, h in probs]
    kh = [_head(z_ref[bb, :, 256:512], h) * SCALE for bb, h in probs]
    vh = [_bf(_head(z_ref[bb, :, 512:768], h)) for bb, h in probs]
    cm = [c_ref[bb, h] for bb, h in probs]
    nv = [n_ref[bb, h:h + 1, :] for bb, h in probs]
    qk_raw = [_dot_nt(_bf(q), _bf(k)) for q, k in zip(qh, kh)]
    q_c = [_dot(_bf(q), _bf(c)) for q, c in zip(qh, cm)]
    i_col, b_col, inter, dlog = [], [], [], []
    for bb, h in probs:
        pre, bcum, pre_t, bcum_t, m_all = gate[bb]
        i_col.append(pre[:, SM_MI + h:SM_MI + h + 1])
        b_col.append(bcum[:, SM_MF + h:SM_MF + h + 1])
        inter.append(b_col[-1] + m_all[:, h:h + 1])
        dlog.append(jnp.where(incl, b_col[-1] - bcum_t[SM_MF + h:SM_MF + h + 1, :]
                              + pre_t[SM_MI + h:SM_MI + h + 1, :], NEG_INF))
    dmax = [jnp.max(x, axis=1, keepdims=True) for x in dlog]
    q_n = [jnp.sum(q * n, axis=1, keepdims=True) for q, n in zip(qh, nv)]
    stab = []
    for i in range(len(probs)):
        m_t = jnp.maximum(inter[i], dmax[i])
        m_end = m_t[chunk - 1:chunk, :]
        stab.append(dict(m_t=m_t, w=jnp.exp(dlog[i] - m_t), g_in=jnp.exp(inter[i] - m_t), m_end=m_end,
                         w_end=jnp.exp(b_col[i][chunk - 1:chunk, :] - b_col[i] + i_col[i] - m_end),
                         g_end=jnp.exp(inter[i][chunk - 1:chunk, :] - m_end)))
    qk = [x * d["w"] for x, d in zip(qk_raw, stab)]
    num = [_dot(_bf(x), v) + d["g_in"] * c for x, v, d, c in zip(qk, vh, stab, q_c)]
    upd = [_dot_tn(_bf(k * d["w_end"]), v) for k, d, v in zip(kh, stab, vh)]
    qk_sum = [jnp.sum(x, axis=1, keepdims=True) for x in qk]
    k_sum = [jnp.sum(d["w_end"] * k, axis=0, keepdims=True) for d, k in zip(stab, kh)]
    hh = []
    for i, (bb, h) in enumerate(probs):
        d = stab[i]
        den = qk_sum[i] + d["g_in"] * q_n[i]
        hh.append(num[i] / jnp.maximum(jnp.abs(den), jnp.exp(-d["m_t"])))
        c_ref[bb, h] = d["g_end"] * cm[i] + upd[i]
        n_ref[bb, h:h + 1, :] = d["g_end"] * nv[i] + k_sum[i]
    lane = lax.broadcasted_iota(jnp.int32, (1, LANES), 1)
    for bb in range(nb):
        m_out = jnp.zeros((1, LANES), F32)
        outs = []
        for h in range(N_HEADS):
            i = bb * N_HEADS + h
            m_out = jnp.where(lane == h, stab[i]["m_end"], m_out)
            outs.append(_head_rms(_sigmoid(_head(z_ref[bb, :, 768:1024], h)) * hh[i]))
        m_ref[bb] = m_out
        y_ref[bb] = jnp.concatenate(outs, axis=1) * nw_ref[...]


def _mlstm_seq(z_ml, small, bias_vec, nw, chunk, nb):
    b, l, _ = z_ml.shape
    return pl.pallas_call(
        functools.partial(_ml_body, chunk=chunk),
        grid=(b // nb, l // chunk),
        in_specs=[pl.BlockSpec((nb, chunk, 1024), lambda i, j: (i, j, 0)),
                  pl.BlockSpec((nb, chunk, LANES), lambda i, j: (i, j, 0)),
                  pl.BlockSpec((1, LANES), lambda i, j: (0, 0)),
                  pl.BlockSpec((1, D_HEADS), lambda i, j: (0, 0))],
        out_specs=[pl.BlockSpec((nb, chunk, D_HEADS), lambda i, j: (i, j, 0)),
                   pl.BlockSpec((nb, N_HEADS, HEAD_DIM, HEAD_DIM), lambda i, j: (i, 0, 0, 0)),
                   pl.BlockSpec((nb, N_HEADS, HEAD_DIM), lambda i, j: (i, 0, 0)),
                   pl.BlockSpec((nb, 1, LANES), lambda i, j: (i, 0, 0))],
        out_shape=[jax.ShapeDtypeStruct((b, l, D_HEADS), F32),
                   jax.ShapeDtypeStruct((b, N_HEADS, HEAD_DIM, HEAD_DIM), F32),
                   jax.ShapeDtypeStruct((b, N_HEADS, HEAD_DIM), F32),
                   jax.ShapeDtypeStruct((b, 1, LANES), F32)],
        compiler_params=_cparams("parallel", "arbitrary"),
        name="mlstm_seq",
    )(z_ml, small, bias_vec, nw.reshape(1, -1))


def _split_bf16(x):
    hi = _bf(x)
    return hi, _bf(x - hi.astype(F32))


def _dot3(a, b):
    (ah, al), (bh, bl) = a, b
    return _dot(ah, bh) + (_dot(ah, bl) + _dot(al, bh))


def _l2norm(x):
    return x * lax.rsqrt(jnp.sum(x * x, axis=-1, keepdims=True) + EPS)


def _gdn_body(z_ref, sm_ref, cw_ref, dtb_ref, alog_ref, nw_ref, y_ref, s_ref, buf_ref, *, chunk):
    @pl.when(pl.program_id(1) == 0)
    def _():
        s_ref[...] = jnp.zeros_like(s_ref)
        buf_ref[:, 0:SUBLANES, :] = jnp.zeros((buf_ref.shape[0], SUBLANES, 3 * D_HEADS), F32)

    nb = z_ref.shape[0]
    rows = z_ref.shape[1]
    nc = rows // chunk
    incl = _tri_incl(chunk)
    strict = lax.broadcasted_iota(jnp.int32, (chunk, chunk), 0) > lax.broadcasted_iota(jnp.int32, (chunk, chunk), 1)
    eye = (lax.broadcasted_iota(jnp.int32, (chunk, chunk), 0)
           == lax.broadcasted_iota(jnp.int32, (chunk, chunk), 1)).astype(F32)
    tri = incl.astype(F32)
    gates, acts = {}, []
    for bb in range(nb):
        buf_ref[bb, SUBLANES:SUBLANES + rows, :] = z_ref[bb, :, 0:768]
        conv = None
        for j in range(GDN_CONV):
            term = buf_ref[bb, pl.ds(SUBLANES - (GDN_CONV - 1) + j, rows), :] * cw_ref[j:j + 1, :]
            conv = term if conv is None else conv + term
        buf_ref[bb, 0:SUBLANES, :] = buf_ref[bb, rows:rows + SUBLANES, :]
        acts.append(_silu(conv))
        sm = sm_ref[bb]
        g_all = -jnp.exp(alog_ref[...]) * _softplus(sm + dtb_ref[...])
        beta_all = _sigmoid(sm)
        for cc in range(nc):
            rs = slice(cc * chunk, (cc + 1) * chunk)
            gcum = _dot(tri, g_all[rs], precision=HI)
            gates[bb, cc] = (gcum, _lanes_to_rows(gcum), beta_all[rs])

    probs = [(bb, cc, h) for cc in range(nc) for bb in range(nb) for h in range(N_HEADS)]
    pre = []
    for bb, cc, h in probs:
        rs = slice(cc * chunk, (cc + 1) * chunk)
        gcum, gcum_t, beta_all = gates[bb, cc]
        g_col = gcum[:, SM_GA + h:SM_GA + h + 1]
        g_row = gcum_t[SM_GA + h:SM_GA + h + 1, :]
        beta = beta_all[:, SM_GB + h:SM_GB + h + 1]
        decay = jnp.where(incl, jnp.exp(jnp.where(incl, g_col - g_row, 0.0)), 0.0)
        act = acts[bb]
        qh = _l2norm(_head(act[rs, 0:256], h)) * SCALE
        kh = _l2norm(_head(act[rs, 256:512], h))
        vh = _head(act[rs, 512:768], h)
        eg = jnp.exp(g_col)
        pre.append(dict(g_col=g_col, beta=beta, decay=decay, qh=qh, kh=kh, kb=_bf(kh), eg=eg,
                        rhs=jnp.concatenate([vh * beta, kh * (beta * eg)], axis=1)))
    a_l = [jnp.where(strict, d["beta"] * _dot_nt(d["kb"], d["kb"]) * d["decay"], 0.0) for d in pre]
    p_l = [eye - a for a in a_l]
    pw_s = [_split_bf16(a) for a in a_l]
    lvl = 2
    while lvl < chunk:
        pw_s = [_split_bf16(_dot3(x, x)) for x in pw_s]
        p_l = [p + _dot3(_split_bf16(p), x) for p, x in zip(p_l, pw_s)]
        lvl *= 2
    uw_l = [_dot3(_split_bf16(p), _split_bf16(d["rhs"])) for p, d in zip(p_l, pre)]
    qk_l = [_dot_nt(_bf(d["qh"]), d["kb"]) * d["decay"] for d in pre]
    per_chunk = nb * N_HEADS
    st_l = [s_ref[bb, h] for bb, cc, h in probs[:per_chunk]]
    o_all = []
    for cc in range(nc):
        sl = slice(cc * per_chunk, (cc + 1) * per_chunk)
        stb_l = [_bf(st) for st in st_l]
        delta_l = [uw[:, 0:HEAD_DIM] - _dot(_bf(uw[:, HEAD_DIM:]), stb) for uw, stb in zip(uw_l[sl], stb_l)]
        o_all += [_dot(_bf(qk), _bf(delta)) + _dot(_bf(d["qh"] * d["eg"]), stb)
                  for qk, delta, d, stb in zip(qk_l[sl], delta_l, pre[sl], stb_l)]
        new_st = []
        for d, st, delta in zip(pre[sl], st_l, delta_l):
            g_end = d["g_col"][chunk - 1:chunk, :]
            w_end = jnp.exp(g_end - d["g_col"])
            new_st.append(jnp.exp(g_end) * st + _dot_tn(_bf(d["kh"] * w_end), _bf(delta)))
        st_l = new_st
    for (bb, cc, h), st in zip(probs[:per_chunk], st_l):
        s_ref[bb, h] = st
    for bb in range(nb):
        for cc in range(nc):
            outs = [_head_rms(o_all[cc * per_chunk + bb * N_HEADS + h]) for h in range(N_HEADS)]
            rs = slice(cc * chunk, (cc + 1) * chunk)
            y_ref[bb, rs, :] = jnp.concatenate(outs, axis=1) * nw_ref[...] * _silu(z_ref[bb, rs, 768:1024])


def _gdn_seq(z_gdn, small, conv_w, dtb_vec, alog_vec, nw, chunk, nb, nc):
    b, l, _ = z_gdn.shape
    rows = nc * chunk
    return pl.pallas_call(
        functools.partial(_gdn_body, chunk=chunk),
        grid=(b // nb, l // rows),
        in_specs=[pl.BlockSpec((nb, rows, 1024), lambda i, j: (i, j, 0)),
                  pl.BlockSpec((nb, rows, LANES), lambda i, j: (i, j, 0)),
                  pl.BlockSpec((GDN_CONV, 3 * D_HEADS), lambda i, j: (0, 0)),
                  pl.BlockSpec((1, LANES), lambda i, j: (0, 0)),
                  pl.BlockSpec((1, LANES), lambda i, j: (0, 0)),
                  pl.BlockSpec((1, D_HEADS), lambda i, j: (0, 0))],
        out_specs=[pl.BlockSpec((nb, rows, D_HEADS), lambda i, j: (i, j, 0)),
                   pl.BlockSpec((nb, N_HEADS, HEAD_DIM, HEAD_DIM), lambda i, j: (i, 0, 0, 0))],
        out_shape=[jax.ShapeDtypeStruct((b, l, D_HEADS), F32),
                   jax.ShapeDtypeStruct((b, N_HEADS, HEAD_DIM, HEAD_DIM), F32)],
        scratch_shapes=[pltpu.VMEM((nb, SUBLANES + rows, 3 * D_HEADS), F32)],
        compiler_params=_cparams("parallel", "arbitrary"),
        name="gdn_seq",
    )(z_gdn, small, conv_w, dtb_vec, alog_vec, nw.reshape(1, -1))


STEP_SEQS = 4


def _row_to_col(row):
    n = row.shape[1]
    eye = lax.broadcasted_iota(jnp.int32, (n, n), 0) == lax.broadcasted_iota(jnp.int32, (n, n), 1)
    return jnp.sum(jnp.where(eye, jnp.broadcast_to(row, (n, n)), 0.0), axis=1, keepdims=True)


def _step_body(zr_ref, zm_ref, zg_ref, sm_ref, cos_ref, sin_ref, mlb_ref, dtb_ref, alog_ref, cw_ref,
               nwr_ref, nwm_ref, nwg_ref, sr_ref, mc_ref, mn_ref, mm_ref, gs_ref, gc_ref,
               yr_ref, ym_ref, yg_ref, sr_o, mc_o, mn_o, mm_o, gs_o, gc_o):
    nb = zr_ref.shape[0]
    d = HEAD_DIM
    eye = lax.broadcasted_iota(jnp.int32, (d, d), 0) == lax.broadcasted_iota(jnp.int32, (d, d), 1)
    lane = lax.broadcasted_iota(jnp.int32, (1, LANES), 1)
    cos, sin = cos_ref[...], sin_ref[...]
    rowsum = lambda x: jnp.sum(x, axis=1, keepdims=True)

    def vec_mat(row, mat):
        return _dot(_pad_rows(row), mat, precision=HI)[0:1, :]

    def outer(a_row, b_row):
        diag = jnp.where(eye, jnp.broadcast_to(a_row, (d, d)), 0.0)
        return _dot(diag, jnp.broadcast_to(b_row, (d, d)), precision=HI)

    ret, ml, gdn, seqs = [], [], [], []
    for bb in range(nb):
        sm, zr, zm, zg = sm_ref[bb], zr_ref[bb], zm_ref[bb], zg_ref[bb]
        rq = _rope256(zr[:, 0:256], cos, sin)
        rk = _rope256(zr[:, 256:512], cos, sin) * SCALE
        pre = sm + mlb_ref[...]
        logf = jnp.minimum(pre, 0.0) - jnp.log(1.0 + jnp.exp(-jnp.abs(pre)))
        m_all = mm_ref[bb]
        x = zg[:, 0:768]
        buf = gc_ref[bb]
        conv = x * cw_ref[GDN_CONV - 1:GDN_CONV, :]
        for j in range(GDN_CONV - 1):
            conv = conv + buf[j:j + 1, :] * cw_ref[j:j + 1, :]
        act = _silu(conv)
        g_all = -jnp.exp(alog_ref[...]) * _softplus(sm + dtb_ref[...])
        beta_all = _sigmoid(sm)
        seqs.append(dict(zr=zr, zm=zm, zg=zg, x=x, buf=buf))
        for h in range(N_HEADS):
            ret.append(dict(q=_head(rq, h), k=_head(rk, h), v=_head(zr[:, 512:768], h), st=sr_ref[bb, h],
                            gam=math.exp(_ret_log_decay(h))))
            i_g = pre[:, SM_MI + h:SM_MI + h + 1]
            inter = logf[:, SM_MF + h:SM_MF + h + 1] + m_all[:, h:h + 1]
            m_t = jnp.maximum(inter, i_g)
            ml.append(dict(q=_head(zm[:, 0:256], h), k=_head(zm[:, 256:512], h) * SCALE, v=_head(zm[:, 512:768], h),
                           cm=mc_ref[bb, h], nv=mn_ref[bb, h:h + 1, :], m_t=m_t, w=jnp.exp(i_g - m_t),
                           g_in=jnp.exp(inter - m_t)))
            gdn.append(dict(q=_head(act[:, 0:256], h), k=_head(act[:, 256:512], h), v=_head(act[:, 512:768], h),
                            st=gs_ref[bb, h], eg=jnp.exp(g_all[:, SM_GA + h:SM_GA + h + 1]),
                            beta=beta_all[:, SM_GB + h:SM_GB + h + 1]))
    for p in gdn:
        p["q"], p["k"] = _l2norm(p["q"]) * SCALE, _l2norm(p["k"])
    for p in ret + ml + gdn:
        p["qk"] = rowsum(p["q"] * p["k"])
    for p in ml:
        p["qn"] = rowsum(p["q"] * p["nv"])
    for p in ret:
        p["qs"] = vec_mat(p["q"] * p["gam"], p["st"])
    for p in ml:
        p["qc"] = vec_mat(p["q"], p["cm"])
    for p in gdn:
        p["ks"] = vec_mat(p["k"] * (p["beta"] * p["eg"]), p["st"])
        p["qs"] = vec_mat(p["q"] * p["eg"], p["st"])
    for p in ret:
        p["o"] = p["qk"] * p["v"] + p["qs"]
    for p in ml:
        qk = p["qk"] * p["w"]
        den = qk + p["g_in"] * p["qn"]
        p["o"] = (qk * p["v"] + p["g_in"] * p["qc"]) / jnp.maximum(jnp.abs(den), jnp.exp(-p["m_t"]))
    for p in gdn:
        p["delta"] = p["v"] * p["beta"] - p["ks"]
        p["o"] = p["qk"] * p["delta"] + p["qs"]
    for p in ret:
        p["new"] = p["st"] * p["gam"] + outer(p["k"], p["v"])
    for p in ml:
        p["new"] = p["g_in"] * p["cm"] + outer(p["k"] * p["w"], p["v"])
    for p in gdn:
        p["new"] = p["eg"] * p["st"] + outer(p["k"], p["delta"])
    for bb in range(nb):
        s = seqs[bb]
        ps = slice(bb * N_HEADS, (bb + 1) * N_HEADS)
        m_out = jnp.zeros((1, LANES), F32)
        for h, (pr, pm, pg) in enumerate(zip(ret[ps], ml[ps], gdn[ps])):
            sr_o[bb, h] = pr["new"]
            mc_o[bb, h] = pm["new"]
            mn_o[bb, h:h + 1, :] = pm["g_in"] * pm["nv"] + pm["w"] * pm["k"]
            gs_o[bb, h] = pg["new"]
            m_out = jnp.where(lane == h, pm["m_t"], m_out)
        mm_o[bb] = m_out
        gc_o[bb, 0:GDN_CONV - 2, :] = s["buf"][1:GDN_CONV - 1, :]
        gc_o[bb, GDN_CONV - 2:GDN_CONV - 1, :] = s["x"]
        yr_ref[bb] = (jnp.concatenate([_head_rms(p["o"]) for p in ret[ps]], axis=1) * nwr_ref[...]
                      * _silu(s["zr"][:, 768:1024]))
        og = s["zm"][:, 768:1024]
        ym_ref[bb] = jnp.concatenate([_head_rms(_sigmoid(_head(og, h)) * p["o"])
                                      for h, p in enumerate(ml[ps])], axis=1) * nwm_ref[...]
        yg_ref[bb] = (jnp.concatenate([_head_rms(p["o"]) for p in gdn[ps]], axis=1) * nwg_ref[...]
                      * _silu(s["zg"][:, 768:1024]))


def _mixers_step(z_ret, z_ml, z_gdn, small, cos, sin, mlb, dtb, alog, conv_w, nwr, nwm, nwg,
                 s_ret, ml_c, ml_n, ml_m, gdn_s, gdn_conv):
    b = z_ret.shape[0]
    nb = STEP_SEQS if b % STEP_SEQS == 0 else 1
    row = lambda n: pl.BlockSpec((nb, 1, n), lambda i: (i, 0, 0))
    vec = lambda n: pl.BlockSpec((1, n), lambda i: (0, 0))
    mat = pl.BlockSpec((nb, N_HEADS, HEAD_DIM, HEAD_DIM), lambda i: (i, 0, 0, 0))
    nsp = pl.BlockSpec((nb, N_HEADS, HEAD_DIM), lambda i: (i, 0, 0))
    csp = pl.BlockSpec((nb, GDN_CONV - 1, 3 * D_HEADS), lambda i: (i, 0, 0))
    mshape = jax.ShapeDtypeStruct((b, N_HEADS, HEAD_DIM, HEAD_DIM), F32)
    yshape = jax.ShapeDtypeStruct((b, 1, D_HEADS), F32)
    return pl.pallas_call(
        _step_body,
        grid=(b // nb,),
        in_specs=[row(1024), row(1024), row(1024), row(LANES), vec(LANES), vec(LANES), vec(LANES), vec(LANES), vec(LANES),
                  pl.BlockSpec((GDN_CONV, 3 * D_HEADS), lambda i: (0, 0)), vec(D_HEADS), vec(D_HEADS), vec(D_HEADS),
                  mat, mat, nsp, row(LANES), mat, csp],
        out_specs=[row(D_HEADS), row(D_HEADS), row(D_HEADS), mat, mat, nsp, row(LANES), mat, csp],
        out_shape=[yshape, yshape, yshape, mshape, mshape,
                   jax.ShapeDtypeStruct((b, N_HEADS, HEAD_DIM), F32), jax.ShapeDtypeStruct((b, 1, LANES), F32),
                   mshape, jax.ShapeDtypeStruct((b, GDN_CONV - 1, 3 * D_HEADS), F32)],
        compiler_params=_cparams("parallel"),
        name="mixers_step",
    )(z_ret, z_ml, z_gdn, small, cos, sin, mlb, dtb, alog, conv_w,
      nwr.reshape(1, -1), nwm.reshape(1, -1), nwg.reshape(1, -1), s_ret, ml_c, ml_n, ml_m, gdn_s, gdn_conv)


def _t5_bucket_np(n):
    exact = T5_BUCKETS // 2
    n = np.maximum(np.asarray(n, np.int64), 0)
    x = np.maximum(n, 1).astype(np.float32) / np.float32(exact)
    large = exact + (np.log(x) / np.float32(math.log(T5_MAX_DIST / exact)) * np.float32(T5_BUCKETS - exact)).astype(np.int32)
    return np.where(n < exact, n, np.minimum(large, T5_BUCKETS - 1)).astype(np.int32)


def _bucket_thresholds():
    b = _t5_bucket_np(np.arange(4 * T5_MAX_DIST))
    return [int(np.argmax(b >= k)) for k in range(T5_BUCKETS // 2 + 1, T5_BUCKETS)]


def _bias_table_body(rbt_ref, oh_ref, o_ref):
    o_ref[...] = _dot(rbt_ref[...], oh_ref[...].astype(F32), precision=HI)


def _bias_tables(rel_bias, dists):
    n = len(dists)
    tn = 4096
    npad = -(-n // tn) * tn
    onehot = np.zeros((T5_BUCKETS, npad), np.float32)
    onehot[_t5_bucket_np(dists), np.arange(n)] = 1.0
    rbt = jnp.pad(rel_bias.T, ((0, SUBLANES - N_HEADS), (0, 0)))
    out = pl.pallas_call(
        _bias_table_body,
        grid=(npad // tn,),
        in_specs=[pl.BlockSpec((SUBLANES, T5_BUCKETS), lambda i: (0, 0)),
                  pl.BlockSpec((T5_BUCKETS, tn), lambda i: (0, i))],
        out_specs=pl.BlockSpec((SUBLANES, tn), lambda i: (0, i)),
        out_shape=jax.ShapeDtypeStruct((SUBLANES, npad), F32),
        name="bias_tables",
    )(rbt, jnp.asarray(onehot, BF16))
    return out[:, :n]


def _prep_cmp_w(cmp_w, cmp_pe):
    eye2 = jnp.eye(2, dtype=F32)

    def half(w):
        full = jnp.einsum('cjde,cf->jcdfe', w, eye2)
        return full.reshape(CMP_STRIDE * 2 * HEAD_DIM, 2 * HEAD_DIM)

    wab = jnp.concatenate([half(cmp_w[:, :CMP_STRIDE]), half(cmp_w[:, CMP_STRIDE:])], axis=1)
    pe = jnp.concatenate([cmp_pe[:CMP_STRIDE].reshape(1, -1), cmp_pe[CMP_STRIDE:].reshape(1, -1)], axis=1)
    return wab, jnp.pad(pe, ((0, SUBLANES - 1), (0, 0)))


def _pe_term(pe_ref, wab_ref):
    kin = CMP_STRIDE * 2 * HEAD_DIM
    t = (_dot(pe_ref[:, 0:kin], wab_ref[:, 0:LANES].astype(F32), precision=HI)
         + _dot(pe_ref[:, kin:2 * kin], wab_ref[:, LANES:2 * LANES].astype(F32), precision=HI))
    return t[0:1, :]


def _combine_cmp(parts, pe_term):
    rows = parts.shape[0]
    nxt = pltpu.roll(parts[:, LANES:2 * LANES], rows - 1, 0)
    r = lax.broadcasted_iota(jnp.int32, (rows, LANES), 0)
    return jnp.where(r < rows - 1, parts[:, 0:LANES] + nxt + pe_term, 0.0)


def _compress_seq_body(kv_ref, wab_ref, pe_ref, o_ref):
    n16 = o_ref.shape[1]
    parts = jnp.zeros((n16, 2 * LANES), F32)
    for j in range(CMP_STRIDE):
        xj = kv_ref[0, pl.ds(j, n16, stride=CMP_STRIDE), :]
        parts = parts + _dot(_bf(xj), _bf(wab_ref[j * LANES:(j + 1) * LANES, :]))
    o_ref[0] = _combine_cmp(parts, _pe_term(pe_ref, wab_ref))


def _compress_seq(nkv, wab, pe):
    b, l, _ = nkv.shape
    n16 = l // CMP_STRIDE
    return pl.pallas_call(
        _compress_seq_body,
        grid=(b,),
        in_specs=[pl.BlockSpec((1, l, LANES), lambda i: (i, 0, 0)),
                  pl.BlockSpec(wab.shape, lambda i: (0, 0)),
                  pl.BlockSpec(pe.shape, lambda i: (0, 0))],
        out_specs=pl.BlockSpec((1, n16, LANES), lambda i: (i, 0, 0)),
        out_shape=jax.ShapeDtypeStruct((b, n16, LANES), F32),
        compiler_params=_cparams("parallel"),
        name="nsa_compress_seq",
    )(nkv, wab, pe)


def _softmax_rows(s, valid):
    s = jnp.where(valid, s, NEG_INF)
    m = jnp.max(s, axis=1, keepdims=True)
    m = jnp.where(m == NEG_INF, 0.0, m)
    p = jnp.exp(s - m)
    den = jnp.sum(p, axis=1, keepdims=True)
    return p / jnp.where(den > 0, den, 1.0)


def _stack_heads(x):
    return jnp.concatenate([_head(x, h) for h in range(N_HEADS)], axis=0)


def _toeplitz(tab_ref, h, r0, rows, ncol):
    nrow = (rows + ncol) // LANES
    strip = tab_ref[h, pl.ds(r0, nrow), :]
    flat = jnp.concatenate([strip[r:r + 1, :] for r in range(nrow)], axis=1)
    rolled = pltpu.roll(jnp.broadcast_to(flat, (rows, nrow * LANES)), 0, 1, stride=1, stride_axis=0)
    return rolled[:, rows:]


SLC_KC = 512


MASK_BIG = 1e30


def _top_blocks_t(score, k):
    row = lax.broadcasted_iota(jnp.int32, score.shape, 0)
    n = score.shape[0]
    sel = jnp.zeros(score.shape, jnp.bool_)
    for _ in range(k):
        m = jnp.max(score, axis=0, keepdims=True)
        idx = jnp.min(jnp.where(score == m, row, n), axis=0, keepdims=True)
        pick = (row == idx) & (m > NEG_INF)
        sel = sel | pick
        score = jnp.where(pick, NEG_INF, score)
    return sel


def _bucket_index(dist):
    n = jnp.maximum(dist, 0)
    exact = T5_BUCKETS // 2
    big = jnp.full(dist.shape, exact, jnp.int32)
    for thr in _bucket_thresholds():
        big = big + (n >= thr).astype(jnp.int32)
    return jnp.where(n < exact, n, big)


def _gather_bias(rbt_ref, h, bucket):
    rows, width = bucket.shape
    piece = min(width, LANES)
    tab = jnp.broadcast_to(rbt_ref[h:h + 1, 0:piece], (rows, piece))
    parts = [jnp.take_along_axis(tab, bucket[:, s:s + piece], axis=1) for s in range(0, width, piece)]
    return parts[0] if len(parts) == 1 else jnp.concatenate(parts, axis=1)


def _tile_rows(x, n=N_HEADS):
    return jnp.concatenate([x] * n, axis=0)


def _near_chunks():
    return (_bucket_thresholds()[-1] + SLC_KC - 2) // SLC_KC + 1


def _bias_tile(tab_ref, base, ncol, p0, lo, hi):
    r0 = (p0 - base - Q_BLOCK) // LANES
    bias = jnp.concatenate([_toeplitz(tab_ref, h, r0, Q_BLOCK, ncol) for h in range(N_HEADS)], axis=0)
    dist = (base + lax.broadcasted_iota(jnp.int32, (Q_BLOCK, ncol), 0)
            - lax.broadcasted_iota(jnp.int32, (Q_BLOCK, ncol), 1))
    return bias + _tile_rows(jnp.where((dist >= lo) & (dist < hi), 0.0, -MASK_BIG))


def _nsa_seq_body(rb_ref, q_ref, sm_ref, kx_ref, vs_ref, kvw_ref, kvc_ref, tab_ref, rbt_ref, ovlt_ref, y_ref,
                  stile_ref, wtile_ref, ctile_ref, *, seq, p0):
    qb = Q_BLOCK
    rows = N_HEADS * qb
    n16 = seq // CMP_STRIDE
    nblk = seq // SLC_BLOCK
    bi = pl.program_id(1)
    q0 = bi * qb

    @pl.when((pl.program_id(0) == 0) & (bi == 0))
    def _():
        per_chunk = SLC_KC // qb

        def slc_tile(t, carry):
            base = (t // _near_chunks()) * qb + (t % _near_chunks()) * SLC_KC
            stile_ref[t] = _bias_tile(tab_ref, base, SLC_KC, p0, 0, seq + SLC_KC)
            return carry

        def win_tile(t, carry):
            wtile_ref[t] = _bias_tile(tab_ref, t * qb, WINDOW + qb, p0, 0, WINDOW)
            return carry

        def cmp_tile(t, carry):
            dist = (t * qb + lax.broadcasted_iota(jnp.int32, (qb, n16), 0)
                    - lax.broadcasted_iota(jnp.int32, (qb, n16), 1) * CMP_STRIDE - (CMP_LEN - 1))
            bucket = _bucket_index(dist)
            bias = jnp.concatenate([_gather_bias(rbt_ref, h, bucket) for h in range(N_HEADS)], axis=0)
            ctile_ref[t] = bias + _tile_rows(jnp.where(dist >= 0, 0.0, -MASK_BIG))
            return carry

        lax.fori_loop(0, per_chunk * _near_chunks(), slc_tile, 0)
        lax.fori_loop(0, WINDOW // qb + 1, win_tile, 0)
        lax.fori_loop(0, seq // qb, cmp_tile, 0)
    qs = _stack_heads(q_ref[0]) * SCALE
    qsb = _bf(qs)

    kvc = kvc_ref[0]
    s_c = _dot_nt(qsb, _bf(kvc[:, 0:HEAD_DIM]))
    band = WINDOW + qb
    sw = pl.multiple_of(jnp.maximum(q0 - WINDOW, 0), qb)
    kw = kvw_ref[0, pl.ds(sw, band), 0:HEAD_DIM]
    vw = kvw_ref[0, pl.ds(sw, band), HEAD_DIM:2 * HEAD_DIM]
    s_w = _dot_nt(qsb, kw)

    sb_c = s_c + ctile_ref[bi]
    m_c = jnp.max(sb_c, axis=1, keepdims=True)
    e_c = jnp.exp(sb_c - m_c)
    den_c = jnp.sum(e_c, axis=1, keepdims=True)
    p_cb = _bf(e_c / jnp.where(m_c > -MASK_BIG / 10, den_c, jnp.inf))
    o_c = _dot(p_cb, _bf(kvc[:, HEAD_DIM:2 * HEAD_DIM]))
    p_r = p_cb.astype(F32)
    p_sum = p_r[0:qb] + p_r[qb:2 * qb] + p_r[2 * qb:3 * qb] + p_r[3 * qb:4 * qb]
    imp_t = _dot_nt(ovlt_ref[...], p_sum, precision=HI)

    base_w = q0 - sw
    sb_w = s_w + wtile_ref[base_w // qb]
    e_w = jnp.exp(sb_w - jnp.max(sb_w, axis=1, keepdims=True))
    o_w = _dot(_bf(e_w / jnp.sum(e_w, axis=1, keepdims=True)), vw)

    qlane = q0 + lax.broadcasted_iota(jnp.int32, (1, qb), 1)
    cur = qlane // SLC_BLOCK
    rowb = lax.broadcasted_iota(jnp.int32, (nblk, qb), 0)
    forced = (rowb == 0) | (rowb == cur) | (rowb == cur - 1)
    score = jnp.where(forced | (rowb > cur), NEG_INF, imp_t)
    sel = jnp.where(forced | _top_blocks_t(score, SLC_TOPK - 3), 1.0, 0.0).T
    negm = _bf(_tile_rows((sel - 1.0) * MASK_BIG))
    q_ext = jnp.concatenate([qsb, negm], axis=1)

    thr_far = _bucket_thresholds()[-1] + SLC_KC - 1
    n_chunks = q0 // SLC_KC + 1
    n_far = jnp.maximum(q0 - thr_far + SLC_KC, 0) // SLC_KC
    n_near = _near_chunks()

    far_bias = jnp.concatenate([jnp.full((qb, 1), rb_ref[T5_BUCKETS - 1, h], F32) for h in range(N_HEADS)], axis=0)

    def scores(c):
        k0 = pl.multiple_of(c * SLC_KC, SLC_KC)
        kx = kx_ref[0, pl.ds(k0, SLC_KC), :]
        return q0 - k0, _dot_nt(q_ext, kx), vs_ref[0, pl.ds(k0, SLC_KC), :]

    def online(carry, s, vs):
        m, l, acc = carry
        m_new = jnp.maximum(m, jnp.max(s, axis=1, keepdims=True))
        alpha = jnp.exp(m - m_new)
        p = jnp.exp(s - m_new)
        return m_new, alpha * l + jnp.sum(p, axis=1, keepdims=True), alpha * acc + _dot(_bf(p), vs)

    def far_step(c, carry):
        _, s, vs = scores(c)
        return online(carry, s + far_bias, vs)

    def near_step(c, carry):
        base, s, vs = scores(c)
        tile = ((q0 % SLC_KC) // qb) * n_near + base // SLC_KC
        return online(carry, s + stile_ref[tile], vs)

    init = (jnp.full((rows, 1), -MASK_BIG / 10, F32), jnp.zeros((rows, 1), F32), jnp.zeros((rows, HEAD_DIM), F32))
    carry = lax.fori_loop(0, n_far, far_step, init)
    _, l_s, acc_s = lax.fori_loop(n_far, n_chunks, near_step, carry)
    o_s = acc_s / jnp.where(l_s > 0, l_s, 1.0)

    gates = _sigmoid(sm_ref[0])
    outs = []
    for h in range(N_HEADS):
        rs = slice(h * qb, (h + 1) * qb)
        g = lambda t: gates[:, SM_NG + 3 * h + t:SM_NG + 3 * h + t + 1]
        outs.append(g(0) * o_c[rs] + g(1) * o_s[rs] + g(2) * o_w[rs])
    y_ref[0] = jnp.concatenate(outs, axis=1)


def _nsa_seq(rel_bias, nq, small, kx, vs, kvw, kvc, tab, seq_p0):
    b, l, _ = nq.shape
    n16 = l // CMP_STRIDE
    nblk = l // SLC_BLOCK
    cmp_start = np.arange(n16)[None, :] * CMP_STRIDE
    slc_start = np.arange(nblk)[:, None] * SLC_BLOCK
    ovlt = ((cmp_start < slc_start + SLC_BLOCK) & (cmp_start + CMP_LEN > slc_start)
            & (np.arange(n16)[None, :] < n16 - 1)).astype(np.float32)
    rbt = jnp.pad(rel_bias.T, ((0, SUBLANES - N_HEADS), (0, LANES - T5_BUCKETS)))
    nrows = tab.shape[1]
    const = lambda shape: pl.BlockSpec(shape, lambda i, j: (0,) * len(shape))
    seq_spec = lambda n: pl.BlockSpec((1, l, n), lambda i, j: (i, 0, 0))
    return pl.pallas_call(
        functools.partial(_nsa_seq_body, seq=l, p0=seq_p0),
        grid=(b, l // Q_BLOCK),
        in_specs=[pl.BlockSpec(memory_space=pltpu.SMEM),
                  pl.BlockSpec((1, Q_BLOCK, D_HEADS), lambda i, j: (i, j, 0)),
                  pl.BlockSpec((1, Q_BLOCK, LANES), lambda i, j: (i, j, 0)),
                  seq_spec(HEAD_DIM + nblk), seq_spec(HEAD_DIM), seq_spec(2 * HEAD_DIM),
                  pl.BlockSpec((1, n16, LANES), lambda i, j: (i, 0, 0)),
                  const((SUBLANES, nrows, LANES)), const((SUBLANES, LANES)), const((nblk, n16))],
        out_specs=pl.BlockSpec((1, Q_BLOCK, D_HEADS), lambda i, j: (i, j, 0)),
        out_shape=jax.ShapeDtypeStruct((b, l, D_HEADS), F32),
        scratch_shapes=[pltpu.VMEM(((SLC_KC // Q_BLOCK) * _near_chunks(), N_HEADS * Q_BLOCK, SLC_KC), F32),
                        pltpu.VMEM((WINDOW // Q_BLOCK + 1, N_HEADS * Q_BLOCK, WINDOW + Q_BLOCK), F32),
                        pltpu.VMEM((l // Q_BLOCK, N_HEADS * Q_BLOCK, n16), F32)],
        compiler_params=_cparams("arbitrary", "arbitrary"),
        name="nsa_seq",
    )(rel_bias, nq, small, kx, vs, kvw, kvc, tab, rbt, jnp.asarray(ovlt))


def _nsa_tab_seq_dists(seq):
    max_base = max(seq, SLC_KC + (_near_chunks() - 1) * SLC_KC)
    p0 = max_base + Q_BLOCK
    length = p0 + WINDOW + Q_BLOCK + LANES
    return p0, np.maximum(p0 - np.arange(length), 0)


PAGE_GROUP = 16
PAGES_PER_STEP = 128


def _pad_rows(x, rows=SUBLANES):
    return jnp.concatenate([x, jnp.zeros((rows - x.shape[0], x.shape[1]), x.dtype)], axis=0)


def _nsa_select_body(pt_ref, *refs, past, n_steps):
    del pt_ref
    pages = refs[:PAGES_PER_STEP]
    wab_ref, pe_ref, q_ref, tabc_ref, ovlt_ref, oc_ref, top_ref, parts_ref, rows_ref = refs[PAGES_PER_STEP:]
    step = pl.program_id(1)
    blocks_per_page = PAGE_SIZE // CMP_STRIDE
    group_rows = PAGE_GROUP * blocks_per_page
    n_groups = PAGES_PER_STEP // PAGE_GROUP
    for p, pg in enumerate(pages):
        g, pp = divmod(p, PAGE_GROUP)
        rows_ref[g, pp * PAGE_SIZE:(pp + 1) * PAGE_SIZE, :] = pg[...].reshape(2 * HEAD_DIM, PAGE_SIZE).T
    for g in range(n_groups):
        acc = jnp.zeros((group_rows, 2 * LANES), F32)
        for j in range(CMP_STRIDE):
            xj = rows_ref[g, pl.ds(j, group_rows, stride=CMP_STRIDE), :]
            acc = acc + _dot(_bf(xj), wab_ref[j * LANES:(j + 1) * LANES, :])
        row0 = pl.multiple_of((step * n_groups + g) * group_rows, group_rows)
        parts_ref[pl.ds(row0, group_rows), :] = acc

    @pl.when(step == n_steps - 1)
    def _():
        n16 = past // CMP_STRIDE
        n_cmp = n16 - 1
        cur = past // SLC_BLOCK
        pe_term = _pe_term(pe_ref, wab_ref)
        kvc = _combine_cmp(parts_ref[...], pe_term)
        qs = _pad_rows(_stack_heads(q_ref[0]) * SCALE)
        s_c = _dot_nt(_bf(qs), _bf(kvc[:, 0:HEAD_DIM])) + tabc_ref[...]
        lane = lax.broadcasted_iota(jnp.int32, (SUBLANES, n16), 1)
        p_c = _softmax_rows(s_c, lane < n_cmp)
        p_cb = _bf(p_c)
        o_c = _dot(p_cb, _bf(kvc[:, HEAD_DIM:2 * HEAD_DIM]))
        oc_ref[0] = jnp.concatenate([o_c[h:h + 1, :] for h in range(N_HEADS)], axis=1)
        imp_h = _dot_nt(ovlt_ref[...], p_cb)
        imp = imp_h[:, 0:1] + imp_h[:, 1:2] + imp_h[:, 2:3] + imp_h[:, 3:4]
        nrow = imp.shape[0]
        rowb = lax.broadcasted_iota(jnp.int32, (nrow, 1), 0)
        forced_ids = (0, cur - 1, cur)
        forced = (rowb == 0) | (rowb == cur) | (rowb == cur - 1)
        score = jnp.where(forced | (rowb > cur), NEG_INF, imp)
        out_lane = lax.broadcasted_iota(jnp.int32, (1, LANES), 1)
        top = jnp.zeros((1, LANES), jnp.int32)
        for r, blk in enumerate(forced_ids):
            top = jnp.where(out_lane == r, blk, top)
        for r in range(len(forced_ids), SLC_TOPK):
            m = jnp.max(score, axis=0, keepdims=True)
            idx = jnp.min(jnp.where(score == m, rowb, nrow), axis=0, keepdims=True)
            top = jnp.where(out_lane == r, idx, top)
            score = jnp.where(rowb == idx, NEG_INF, score)
        top_ref[0] = top


def _nsa_select(layer, page_table, cache_t, wab_bf, pe, nq, tabc, past):
    b, n_pages = page_table.shape
    n_steps = n_pages // PAGES_PER_STEP
    n16 = past // CMP_STRIDE
    n_slc = past // SLC_BLOCK + 1
    nrow = -(-n_slc // SUBLANES) * SUBLANES
    cmp_start = np.arange(n16)[None, :] * CMP_STRIDE
    slc_start = np.arange(nrow)[:, None] * SLC_BLOCK
    ovlt = ((cmp_start < slc_start + SLC_BLOCK) & (cmp_start + CMP_LEN > slc_start)
            & (np.arange(n16)[None, :] < n16 - 1) & (np.arange(nrow)[:, None] < n_slc)).astype(np.float32)

    n_pool = cache_t.shape[1]

    def page_spec(p):
        def imap(i, s, pt):
            pg = pt[jnp.minimum(i, b - 1), jnp.minimum(s, n_steps - 1) * PAGES_PER_STEP + p]
            return (layer, jnp.minimum(pg, n_pool - 1), 0, 0, 0)
        return pl.BlockSpec((None, None, 2, HEAD_DIM, PAGE_SIZE), imap)

    const = lambda shape: pl.BlockSpec(shape, lambda i, s, pt: (0,) * len(shape))
    grid_spec = pltpu.PrefetchScalarGridSpec(
        num_scalar_prefetch=1,
        grid=(b, n_steps),
        in_specs=[page_spec(p) for p in range(PAGES_PER_STEP)]
        + [const(wab_bf.shape), const(pe.shape), pl.BlockSpec((1, 1, D_HEADS), lambda i, s, pt: (i, 0, 0)),
           const(tabc.shape), const(ovlt.shape)],
        out_specs=[pl.BlockSpec((1, 1, D_HEADS), lambda i, s, pt: (i, 0, 0)),
                   pl.BlockSpec((1, 1, LANES), lambda i, s, pt: (i, 0, 0))],
        scratch_shapes=[pltpu.VMEM((n16, 2 * LANES), F32),
                        pltpu.VMEM((PAGES_PER_STEP // PAGE_GROUP, PAGE_GROUP * PAGE_SIZE, 2 * HEAD_DIM), F32)],
    )
    return pl.pallas_call(
        functools.partial(_nsa_select_body, past=past, n_steps=n_steps),
        grid_spec=grid_spec,
        out_shape=[jax.ShapeDtypeStruct((b, 1, D_HEADS), F32), jax.ShapeDtypeStruct((b, 1, LANES), jnp.int32)],
        compiler_params=_cparams("parallel", "arbitrary"),
        name="nsa_select",
    )(page_table, *([cache_t] * PAGES_PER_STEP), wab_bf, pe, nq, tabc, jnp.asarray(ovlt, BF16))


def _nsa_attend_body(pt_ref, top_ref, *refs, past):
    del pt_ref
    pages = refs[:SLC_TOPK]
    (q_ref, sm_ref, new_ref, oc_ref, swa_ref, tabs_ref, tabw_ref, y_ref, swa_out) = refs[SLC_TOPK:]
    b = pl.program_id(0)
    new_blk = past // SLC_BLOCK
    blocks_per_page = PAGE_SIZE // SLC_BLOCK
    qs = _bf(_pad_rows(_stack_heads(q_ref[0]) * SCALE))
    new = new_ref[0]
    lane_t = lax.broadcasted_iota(jnp.int32, (HEAD_DIM, PAGE_SIZE), 1)
    lane = lax.broadcasted_iota(jnp.int32, (SUBLANES, PAGE_SIZE), 1)
    new_kt = jnp.where(lane_t == 0, _row_to_col(new[:, 2 * HEAD_DIM:3 * HEAD_DIM]), 0.0)
    new_vt = jnp.where(lane_t == 0, _row_to_col(new[:, 3 * HEAD_DIM:4 * HEAD_DIM]), 0.0)
    kts, biases, oks, vts = [], [], [], []
    for s in range(SLC_TOPK):
        ti = jnp.clip(top_ref[b, s], 0, new_blk)
        is_new = ti == new_blk
        kts.append(jnp.where(is_new, new_kt, pages[s][0]))
        vts.append(jnp.where(is_new, new_vt, pages[s][1]))
        biases.append(tabs_ref[ti])
        kpos = ti * SLC_BLOCK + lane % SLC_BLOCK
        ok = (lane // SLC_BLOCK == ti % blocks_per_page) & (kpos <= past)
        oks.append(jnp.where(ok, 1.0, 0.0))
    wlane = lax.broadcasted_iota(jnp.int32, (HEAD_DIM, WINDOW), 1)
    win = []
    for c in range(2):
        col = _row_to_col(new[:, (4 + c) * HEAD_DIM:(5 + c) * HEAD_DIM])
        win.append(jnp.where(wlane == WINDOW - 1, col, pltpu.roll(swa_ref[c], WINDOW - 1, 1)))
    s_w = _dot(qs, _bf(win[0])) + tabw_ref[...]

    s_s = _dot(qs, _bf(jnp.concatenate(kts, axis=1))) + jnp.concatenate(biases, axis=1)
    p_s = _softmax_rows(s_s, jnp.concatenate(oks, axis=1) > 0.5)
    p_w = _softmax_rows(s_w, jnp.full(s_w.shape, True))
    o_s = _dot_nt(_bf(p_s), _bf(jnp.concatenate(vts, axis=1)))
    o_w = _dot_nt(_bf(p_w), _bf(win[1]))
    for c in range(2):
        swa_out[c] = win[c]

    gates = _sigmoid(sm_ref[0])
    o_c = oc_ref[0]
    outs = []
    for h in range(N_HEADS):
        g = lambda t: gates[:, SM_NG + 3 * h + t:SM_NG + 3 * h + t + 1]
        outs.append(g(0) * _head(o_c, h) + g(1) * o_s[h:h + 1, :] + g(2) * o_w[h:h + 1, :])
    y_ref[0] = jnp.concatenate(outs, axis=1)


def _nsa_attend(layer, page_table, top, cache_t, nq, small, nkv, o_c, swa_t, tabs, tabw, past):
    b = nq.shape[0]
    last_page = past // PAGE_SIZE - 1
    blocks_per_page = PAGE_SIZE // SLC_BLOCK

    n_pool = cache_t.shape[1]

    def page_spec(s):
        def imap(i, pt, tp):
            ii = jnp.minimum(i, b - 1)
            pg = jnp.clip(tp[ii, s] // blocks_per_page, 0, last_page)
            return (layer, jnp.clip(pt[ii, pg], 0, n_pool - 1), 1, 0, 0)
        return pl.BlockSpec((None, None, 2, HEAD_DIM, PAGE_SIZE), imap)

    row = lambda n: pl.BlockSpec((1, 1, n), lambda i, pt, tp: (i, 0, 0))
    const = lambda shape: pl.BlockSpec(shape, lambda i, pt, tp: (0,) * len(shape))
    grid_spec = pltpu.PrefetchScalarGridSpec(
        num_scalar_prefetch=2,
        grid=(b,),
        in_specs=[page_spec(s) for s in range(SLC_TOPK)]
        + [row(D_HEADS), row(LANES), row(6 * HEAD_DIM), row(D_HEADS),
           pl.BlockSpec((None, None, 2, HEAD_DIM, WINDOW), lambda i, pt, tp: (layer, i, 0, 0, 0)),
           const(tabs.shape), const(tabw.shape)],
        out_specs=[row(D_HEADS), pl.BlockSpec((None, 2, HEAD_DIM, WINDOW), lambda i, pt, tp: (i, 0, 0, 0))],
    )
    return pl.pallas_call(
        functools.partial(_nsa_attend_body, past=past),
        grid_spec=grid_spec,
        out_shape=[jax.ShapeDtypeStruct((b, 1, D_HEADS), F32),
                   jax.ShapeDtypeStruct((b, 2, HEAD_DIM, WINDOW), F32)],
        compiler_params=_cparams("parallel"),
        name="nsa_attend",
    )(page_table, top, *([cache_t] * SLC_TOPK), nq, small, nkv, o_c, swa_t, tabs, tabw)


SEQ_CHUNK = 64
GDN_CHUNKS = 1
SCAN_CHUNK = 256
ROW_TILE = 512
FFN_TILE = 512


def _lane_vec(pairs):
    v = jnp.zeros((1, LANES), F32)
    for off, vals in pairs:
        v = v.at[0, off:off + vals.shape[0]].set(vals.astype(F32))
    return v


def kernel(x_prompt, x_sample, cache_nsa_kv, cache_swa_kv, state_ret, state_mlstm_C, state_mlstm_n, state_mlstm_m, state_gdn, state_gdn_conv, state_ffn_conv, page_table, rel_bias, norm_pre_mix, w_in, mlstm_b_i, mlstm_b_f, gdn_conv_w, gdn_A_log, gdn_dt_bias, nsa_cmp_pe, nsa_cmp_w, ret_norm, mlstm_norm, gdn_norm, w_out, norm_post_mix, norm_pre_ffn, w_ffn_gate, w_ffn_up, ffn_conv_w, w_ffn_down, norm_post_ffn):
    depth = w_in.shape[0]
    bp, seq, _ = x_prompt.shape
    bs, dec_seq, _ = x_sample.shape
    assert dec_seq == 1 and seq % SLC_KC == 0 and seq >= WINDOW + Q_BLOCK
    n_pool = cache_nsa_kv.shape[1]
    past = page_table.shape[1] * PAGE_SIZE
    assert past >= WINDOW and past % SLC_BLOCK == 0 and page_table.shape[1] % PAGES_PER_STEP == 0
    assert past // SLC_BLOCK + 1 >= SLC_TOPK

    p0, d_seq = _nsa_tab_seq_dists(seq)
    n16 = past // CMP_STRIDE
    n_slc = past // SLC_BLOCK + 1
    d_cmp = np.maximum(past - (np.arange(n16) * CMP_STRIDE + CMP_LEN - 1), 0)
    d_slc = np.maximum(past - np.arange(n_slc * SLC_BLOCK), 0)
    d_win = WINDOW - 1 - np.arange(WINDOW)
    tab = _bias_tables(rel_bias, np.concatenate([d_seq, d_cmp, d_slc, d_win]))
    o1, o2, o3 = len(d_seq), len(d_seq) + len(d_cmp), len(d_seq) + len(d_cmp) + len(d_slc)
    tab_seq = tab[:, :o1].reshape(SUBLANES, o1 // LANES, LANES)
    tabc = tab[:, o1:o2]
    tabs = jnp.transpose(tab[:, o2:o3].reshape(SUBLANES, n_slc, SLC_BLOCK), (1, 0, 2))
    tabs = jnp.concatenate([tabs, tabs], axis=2)
    tabw = tab[:, o3:]

    cache_t = jnp.transpose(cache_nsa_kv, (0, 1, 3, 4, 2))
    swa_t = jnp.transpose(cache_swa_kv, (0, 1, 3, 4, 2))
    cos_p, sin_p = _rope_tables(jnp.arange(seq))
    cos_s, sin_s = _rope_tables(jnp.arange(past, past + 1))

    xp = x_prompt
    xs = x_sample.reshape(bs, D_MODEL)
    p_states, s_states = [], []
    for l in range(depth):
        w_cat = _prep_w_in(w_in[l])
        w_out_bf = _bf(w_out[l])
        wg, wu, wd = _bf(w_ffn_gate[l]), _bf(w_ffn_up[l]), _bf(w_ffn_down[l])
        wab, pe = _prep_cmp_w(nsa_cmp_w[l], nsa_cmp_pe[l])
        ml_bias = _lane_vec([(SM_MI, mlstm_b_i[l]), (SM_MF, mlstm_b_f[l])])
        dtb = _lane_vec([(SM_GA, gdn_dt_bias[l])])
        alog = _lane_vec([(SM_GA, gdn_A_log[l])])

        z = _in_proj(xp.reshape(bp * seq, D_MODEL), norm_pre_mix[l], w_cat, ROW_TILE, seq=seq)
        z_ret, z_ml, z_gdn, z_nq, z_nkv, z_sm = [t.reshape(bp, seq, -1) for t in z[:6]]
        nkv_t = z[6].reshape(bp, 6, HEAD_DIM, seq)
        kx, vs, kvw = [t.reshape(bp, seq, -1) for t in z[7:]]
        y_ret, st_ret = _retention_seq(z_ret, cos_p, sin_p, ret_norm[l], SCAN_CHUNK, bp)
        y_ml, st_c, st_n, st_m = _mlstm_seq(z_ml, z_sm, ml_bias, mlstm_norm[l], SCAN_CHUNK, bp)
        y_gdn, st_g = _gdn_seq(z_gdn, z_sm, gdn_conv_w[l], dtb, alog, gdn_norm[l], SEQ_CHUNK, bp, GDN_CHUNKS)
        kvc = _compress_seq(z_nkv, wab, pe)
        y_nsa = _nsa_seq(rel_bias, z_nq, z_sm, kx, vs, kvw, kvc, tab_seq, p0)
        xp, tail = _ffn_seq(xp, (y_ret, y_ml, y_gdn, y_nsa), w_out_bf, norm_post_mix[l], norm_pre_ffn[l],
                            wg, wu, wd, ffn_conv_w[l], norm_post_ffn[l], min(seq, FFN_TILE))
        p_states.append((
            jnp.transpose(nkv_t[:, 0:4], (0, 3, 1, 2)),
            jnp.transpose(nkv_t[:, 4:6, :, seq - WINDOW:], (0, 3, 1, 2)),
            st_ret, st_c, st_n, st_m[:, 0, :N_HEADS], st_g,
            z_gdn[:, seq - (GDN_CONV - 1):, 0:3 * D_HEADS],
            tail[:, SUBLANES - (FFN_CONV - 1):, :]))

        z = _in_proj(xs, norm_pre_mix[l], w_cat, bs)
        z_ret, z_ml, z_gdn, z_nq, z_nkv, z_sm = [t.reshape(bs, 1, -1) for t in z]
        m_pad = jnp.pad(state_mlstm_m[l], ((0, 0), (0, LANES - N_HEADS))).reshape(bs, 1, LANES)
        (y_ret, y_ml, y_gdn, st_ret, st_c, st_n, st_m, st_g, st_gc) = _mixers_step(
            z_ret, z_ml, z_gdn, z_sm, cos_s, sin_s, ml_bias, dtb, alog, gdn_conv_w[l],
            ret_norm[l], mlstm_norm[l], gdn_norm[l],
            state_ret[l], state_mlstm_C[l], state_mlstm_n[l], m_pad, state_gdn[l], state_gdn_conv[l])
        o_c, top = _nsa_select(l, page_table, cache_t, _bf(wab), pe, z_nq, tabc, past)
        y_nsa, swa_new = _nsa_attend(l, page_table, top[:, 0, :SLC_TOPK], cache_t, z_nq, z_sm, z_nkv, o_c,
                                     swa_t, tabs, tabw, past)
        ys = [t.reshape(bs, D_HEADS) for t in (y_ret, y_ml, y_gdn, y_nsa)]
        x1, h2 = _out_proj(xs, ys, w_out_bf, norm_post_mix[l], norm_pre_ffn[l], bs)
        xs, g_new = _ffn_step(x1, h2, state_ffn_conv[l][:, 0], state_ffn_conv[l][:, 1], wg, wu, wd,
                              ffn_conv_w[l], norm_post_ffn[l])
        s_states.append((
            z_nkv[..., 0:4 * HEAD_DIM].reshape(bs, 1, 4, HEAD_DIM),
            jnp.transpose(swa_new, (0, 3, 1, 2)),
            st_ret, st_c, st_n, st_m[:, 0, :N_HEADS], st_g, st_gc,
            jnp.concatenate([state_ffn_conv[l][:, 1:], g_new[:, None, :]], axis=1)))

    stack = lambda states: tuple(jnp.stack([s[k] for s in states]) for k in range(len(states[0])))
    return (xp, xs.reshape(bs, 1, D_MODEL)) + stack(p_states) + stack(s_states)
```

```python
import functools
import math

import jax
import jax.numpy as jnp
import numpy as np
from jax import lax
from jax.experimental import pallas as pl
from jax.experimental.pallas import tpu as pltpu

F32 = jnp.float32
BF16 = jnp.bfloat16
HI = lax.Precision.HIGHEST

D_MODEL = 1024
HEAD_DIM = 64
N_HEADS = 4
D_HEADS = N_HEADS * HEAD_DIM
D_FF = 2816
GDN_CONV = 4
FFN_CONV = 3
PAGE_SIZE = 128
CMP_STRIDE = 16
CMP_LEN = 32
SLC_BLOCK = 64
SLC_TOPK = 16
WINDOW = 512
Q_BLOCK = 128
T5_BUCKETS = 32
T5_MAX_DIST = 1024
ROPE_BASE = 10000.0
EPS = 1e-6
SCALE = HEAD_DIM ** -0.5
IN_WIDTHS = (256, 256, 256, 256, 256, 256, 256, 4, 4, 256, 768, 4, 4, 256, 256, 384, 12)
LANES = 128
SUBLANES = 8
VMEM_LIMIT = 56 * 1024 * 1024
NEG_INF = float("-inf")

SM_MI, SM_MF, SM_GA, SM_GB, SM_NG = 0, 4, 8, 12, 16


def _cparams(*sem):
    return pltpu.CompilerParams(dimension_semantics=sem, vmem_limit_bytes=VMEM_LIMIT)


def _rms(x, g):
    return x * lax.rsqrt(jnp.mean(x * x, axis=-1, keepdims=True) + EPS) * g


def _dot(a, b, **kw):
    return jnp.dot(a, b, preferred_element_type=F32, **kw)


def _dot_nt(a, b, **kw):
    return lax.dot_general(a, b, (((1,), (1,)), ((), ())), preferred_element_type=F32, **kw)


def _dot_tn(a, b, **kw):
    return lax.dot_general(a, b, (((0,), (0,)), ((), ())), preferred_element_type=F32, **kw)


def _bf(x):
    return x.astype(BF16)


def _sigmoid(x):
    return 1.0 / (1.0 + jnp.exp(-x))


def _silu(x):
    return x * _sigmoid(x)


def _softplus(x):
    return jnp.maximum(x, 0.0) + jnp.log(1.0 + jnp.exp(-jnp.abs(x)))


def _head_rms(o):
    return o * lax.rsqrt(jnp.mean(o * o, axis=-1, keepdims=True) + EPS)


PROJ_WIDTHS = (1024, 1024, 1024, 256, 384, 128)


def _prep_w_in(w):
    wt = w.T
    parts, off = [], 0
    for wd in IN_WIDTHS:
        parts.append(wt[off:off + wd])
        off += wd
    (rq, rk, rv, rg, mq, mk, mv, mi, mf, mo, gqkv, ga, gbeta, gg, nq, nkv, ngate) = parts
    small = jnp.concatenate([mi, mf, ga, gbeta, ngate], axis=0)
    small = jnp.pad(small, ((0, LANES - small.shape[0]), (0, 0)))
    cat = jnp.concatenate([rq, rk, rv, rg, mq, mk, mv, mo, gqkv, gg, nq, nkv, small], axis=0)
    return cat.astype(BF16)


NKV_OFF = sum(PROJ_WIDTHS[:4])


def _in_proj_body(x_ref, g_ref, wt_ref, *out_refs, seq):
    h = _bf(_rms(x_ref[...], g_ref[...]))
    off = 0
    for ref in out_refs[:len(PROJ_WIDTHS)]:
        n = ref.shape[-1]
        ref[...] = _dot_nt(h, wt_ref[off:off + n, :])
        off += n
    if seq is not None:
        kvt_ref, kx_ref, vs_ref, kvw_ref = out_refs[len(PROJ_WIDTHS):]
        kvt_ref[0] = _dot_nt(wt_ref[NKV_OFF:NKV_OFF + PROJ_WIDTHS[4], :], h)
        nkv = out_refs[4][...]
        tm = nkv.shape[0]
        nblk = kx_ref.shape[-1] - HEAD_DIM
        pos = (pl.program_id(0) * tm) % seq + lax.broadcasted_iota(jnp.int32, (tm, nblk), 0)
        block_id = jnp.where(pos // SLC_BLOCK == lax.broadcasted_iota(jnp.int32, (tm, nblk), 1), 1.0, 0.0)
        kx_ref[...] = _bf(jnp.concatenate([nkv[:, 2 * HEAD_DIM:3 * HEAD_DIM], block_id], axis=1))
        vs_ref[...] = _bf(nkv[:, 3 * HEAD_DIM:4 * HEAD_DIM])
        kvw_ref[...] = _bf(nkv[:, 4 * HEAD_DIM:6 * HEAD_DIM])


def _in_proj(x2d, g, w_cat_t, tm, seq=None):
    m = x2d.shape[0]
    ntot = sum(PROJ_WIDTHS)
    const = lambda shape: pl.BlockSpec(shape, lambda i: (0,) * len(shape), pipeline_mode=pl.Buffered(1))
    out_specs = [pl.BlockSpec((tm, n), lambda i: (i, 0)) for n in PROJ_WIDTHS]
    out_shape = [jax.ShapeDtypeStruct((m, n), F32) for n in PROJ_WIDTHS]
    if seq is not None:
        per_seq = seq // tm
        nkv = PROJ_WIDTHS[4]
        out_specs.append(pl.BlockSpec((1, nkv, tm), lambda i: (i // per_seq, 0, i % per_seq)))
        out_shape.append(jax.ShapeDtypeStruct((m // seq, nkv, seq), F32))
        for n in (HEAD_DIM + seq // SLC_BLOCK, HEAD_DIM, 2 * HEAD_DIM):
            out_specs.append(pl.BlockSpec((tm, n), lambda i: (i, 0)))
            out_shape.append(jax.ShapeDtypeStruct((m, n), BF16))
    return pl.pallas_call(
        functools.partial(_in_proj_body, seq=seq),
        grid=(m // tm,),
        in_specs=[pl.BlockSpec((tm, D_MODEL), lambda i: (i, 0)), const((1, D_MODEL)), const((ntot, D_MODEL))],
        out_specs=out_specs,
        out_shape=out_shape,
        compiler_params=_cparams("parallel"),
        name="in_proj",
    )(x2d, g.reshape(1, D_MODEL), w_cat_t)


def _out_proj_body(x_ref, y0, y1, y2, y3, w_ref, gpost_ref, gpre_ref, x1_ref, h2_ref):
    acc = None
    for k, y in enumerate((y0, y1, y2, y3)):
        t = _dot(_bf(y[...]), w_ref[k * D_HEADS:(k + 1) * D_HEADS, :])
        acc = t if acc is None else acc + t
    x1 = x_ref[...] + _rms(acc, gpost_ref[...])
    x1_ref[...] = x1
    h2_ref[...] = _bf(_rms(x1, gpre_ref[...]))


def _out_proj(x2d, ys, w_out_bf, gpost, gpre, tm):
    m = x2d.shape[0]
    row = lambda n: pl.BlockSpec((tm, n), lambda i: (i, 0))
    vec = pl.BlockSpec((1, D_MODEL), lambda i: (0, 0))
    return pl.pallas_call(
        _out_proj_body,
        grid=(m // tm,),
        in_specs=[row(D_MODEL)] + [row(D_HEADS)] * 4
        + [pl.BlockSpec((D_MODEL, D_MODEL), lambda i: (0, 0)), vec, vec],
        out_specs=[row(D_MODEL), row(D_MODEL)],
        out_shape=[jax.ShapeDtypeStruct((m, D_MODEL), F32), jax.ShapeDtypeStruct((m, D_MODEL), BF16)],
        compiler_params=_cparams("parallel"),
        name="out_proj",
    )(x2d, *ys, w_out_bf, gpost.reshape(1, -1), gpre.reshape(1, -1))


FFN_TN = 256


def _gelu_tanh(x):
    c = 0.7978845608028654
    hx = 0.5 * x
    return hx + hx * jnp.tanh(x * (c + (0.044715 * c) * (x * x)))


def _ffn_seq_body(x_ref, y0, y1, y2, y3, wo_ref, gmix_ref, gpre_ref, wg_ref, wu_ref, wd_ref, cw_ref, gpost_ref,
                  x2_ref, tail_ref, carry_ref, act_ref, *, tm):
    @pl.when(pl.program_id(1) == 0)
    def _():
        carry_ref[...] = jnp.zeros_like(carry_ref)

    mix = None
    for k, y in enumerate((y0, y1, y2, y3)):
        t = _dot(_bf(y[0]), wo_ref[k * D_HEADS:(k + 1) * D_HEADS, :])
        mix = t if mix is None else mix + t
    x1 = x_ref[0] + _rms(mix, gmix_ref[...])
    h2 = _bf(_rms(x1, gpre_ref[...]))
    row8 = lax.broadcasted_iota(jnp.int32, (SUBLANES, FFN_TN), 0)
    starts = list(range(0, D_FF, FFN_TN))
    up = lambda n0: (_dot(h2, wg_ref[:, n0:n0 + FFN_TN]), _dot(h2, wu_ref[:, n0:n0 + FFN_TN]))
    nxt = up(starts[0])
    for i, n0 in enumerate(starts):
        g, u = nxt
        if i + 1 < len(starts):
            nxt = up(starts[i + 1])
        prev = carry_ref[:, n0:n0 + FFN_TN]
        g1, g2 = pltpu.roll(g, 1, 0), pltpu.roll(g, 2, 0)
        top1 = jnp.where(row8 == 0, prev[7:8, :], g1[0:SUBLANES])
        top2 = jnp.where(row8 == 0, prev[6:7, :], jnp.where(row8 == 1, prev[7:8, :], g2[0:SUBLANES]))
        g1 = jnp.concatenate([top1, g1[SUBLANES:]], axis=0)
        g2 = jnp.concatenate([top2, g2[SUBLANES:]], axis=0)
        cw = cw_ref[:, n0:n0 + FFN_TN]
        gate = cw[0:1, :] * g2 + cw[1:2, :] * g1 + cw[2:3, :] * g
        a = _gelu_tanh(gate) * u
        act_ref[:, n0:n0 + FFN_TN] = _bf(a)
        carry_ref[:, n0:n0 + FFN_TN] = g[tm - SUBLANES:tm, :]
    acc = _dot(act_ref[...], wd_ref[...])
    x2_ref[0] = x1 + _rms(acc, gpost_ref[...])
    tail_ref[0] = carry_ref[...]


def _ffn_seq(x, ys, w_out_bf, gmix, gpre, wg, wu, wd, cw, gpost, tm):
    b, l, _ = x.shape
    const = lambda shape: pl.BlockSpec(shape, lambda i, j: (0,) * len(shape), pipeline_mode=pl.Buffered(1))
    row = lambda n: pl.BlockSpec((1, tm, n), lambda i, j: (i, j, 0))
    return pl.pallas_call(
        functools.partial(_ffn_seq_body, tm=tm),
        grid=(b, l // tm),
        in_specs=[row(D_MODEL)] + [row(D_HEADS)] * 4
        + [const((D_MODEL, D_MODEL)), const((1, D_MODEL)), const((1, D_MODEL)),
           const((D_MODEL, D_FF)), const((D_MODEL, D_FF)), const((D_FF, D_MODEL)),
           const((FFN_CONV, D_FF)), const((1, D_MODEL))],
        out_specs=[pl.BlockSpec((1, tm, D_MODEL), lambda i, j: (i, j, 0)),
                   pl.BlockSpec((1, SUBLANES, D_FF), lambda i, j: (i, 0, 0))],
        out_shape=[jax.ShapeDtypeStruct((b, l, D_MODEL), F32),
                   jax.ShapeDtypeStruct((b, SUBLANES, D_FF), F32)],
        scratch_shapes=[pltpu.VMEM((SUBLANES, D_FF), F32), pltpu.VMEM((tm, D_FF), BF16)],
        compiler_params=_cparams("parallel", "arbitrary"),
        name="ffn_seq",
    )(x, *ys, w_out_bf, gmix.reshape(1, -1), gpre.reshape(1, -1), wg, wu, wd, cw, gpost.reshape(1, -1))


def _ffn_step_body(x1_ref, h2_ref, b0_ref, b1_ref, wg_ref, wu_ref, wd_ref, cw_ref, gpost_ref, x2_ref, g_ref):
    h2 = h2_ref[...]
    acc = jnp.zeros(x1_ref.shape, F32)
    for n0 in range(0, D_FF, FFN_TN):
        sl = slice(n0, n0 + FFN_TN)
        g = _dot(h2, wg_ref[:, sl])
        cw = cw_ref[:, sl]
        gate = cw[0:1, :] * b0_ref[:, sl] + cw[1:2, :] * b1_ref[:, sl] + cw[2:3, :] * g
        u = _dot(h2, wu_ref[:, sl])
        acc = acc + _dot(_bf(_gelu_tanh(gate) * u), wd_ref[sl, :])
        g_ref[:, sl] = g
    x2_ref[...] = x1_ref[...] + _rms(acc, gpost_ref[...])


def _ffn_step(x1, h2, b0, b1, wg, wu, wd, cw, gpost):
    m = x1.shape[0]
    return pl.pallas_call(
        _ffn_step_body,
        out_shape=[jax.ShapeDtypeStruct((m, D_MODEL), F32), jax.ShapeDtypeStruct((m, D_FF), F32)],
        compiler_params=pltpu.CompilerParams(vmem_limit_bytes=VMEM_LIMIT),
        name="ffn_step",
    )(x1, h2, b0, b1, wg, wu, wd, cw, gpost.reshape(1, -1))


def _rope_tables(pos):
    half = HEAD_DIM // 2
    inv = ROPE_BASE ** (-jnp.linspace(0.0, 1.0, half, dtype=F32))
    ang = pos.astype(F32)[:, None] * inv[None, :]
    return jnp.tile(jnp.cos(ang), (1, LANES // half)), jnp.tile(jnp.sin(ang), (1, LANES // half))


def _rope128(x, cos, sin):
    lane = lax.broadcasted_iota(jnp.int32, x.shape, 1)
    first = (lane % HEAD_DIM) < (HEAD_DIM // 2)
    other = jnp.where(first, -pltpu.roll(x, LANES - HEAD_DIM // 2, 1), pltpu.roll(x, HEAD_DIM // 2, 1))
    return x * cos + other * sin


def _rope256(x, cos, sin):
    return jnp.concatenate([_rope128(x[:, :LANES], cos, sin), _rope128(x[:, LANES:], cos, sin)], axis=1)


def _ret_log_decay(h):
    return math.log(1.0 - 2.0 ** (-5.0 - h))


def _head(x, h):
    return x[:, h * HEAD_DIM:(h + 1) * HEAD_DIM]


def _ret_body(z_ref, cos_ref, sin_ref, nw_ref, y_ref, s_ref, *, chunk):
    @pl.when(pl.program_id(1) == 0)
    def _():
        s_ref[...] = jnp.zeros_like(s_ref)

    nb = z_ref.shape[0]
    cos, sin = cos_ref[...], sin_ref[...]
    t = lax.broadcasted_iota(jnp.int32, (chunk, chunk), 0)
    s = lax.broadcasted_iota(jnp.int32, (chunk, chunk), 1)
    causal = t >= s
    diff = jnp.where(causal, t - s, 0).astype(F32)
    tcol = lax.broadcasted_iota(jnp.int32, (chunk, 1), 0).astype(F32)
    lgs = [_ret_log_decay(h) for h in range(N_HEADS)]
    dmat = [jnp.where(causal, jnp.exp(lg * diff), 0.0) for lg in lgs]
    xi = [jnp.exp(lg * (tcol + 1.0)) for lg in lgs]
    zeta = [jnp.exp(lg * (chunk - 1.0 - tcol)) for lg in lgs]
    probs = [(bb, h) for bb in range(nb) for h in range(N_HEADS)]
    qkv = []
    for bb in range(nb):
        q = _rope256(z_ref[bb, :, 0:256], cos, sin)
        k = _rope256(z_ref[bb, :, 256:512], cos, sin) * SCALE
        qkv.append((q, k, z_ref[bb, :, 512:768]))
    qh = [_head(qkv[bb][0], h) for bb, h in probs]
    kh = [_head(qkv[bb][1], h) for bb, h in probs]
    vh = [_bf(_head(qkv[bb][2], h)) for bb, h in probs]
    st = [s_ref[bb, h] for bb, h in probs]
    sc = [_dot_nt(_bf(q), _bf(k)) for q, k in zip(qh, kh)]
    cross = [_dot(_bf(q * xi[h]), _bf(s0)) for q, s0, (_, h) in zip(qh, st, probs)]
    upd = [_dot_tn(_bf(k * zeta[h]), v) for k, v, (_, h) in zip(kh, vh, probs)]
    o = [_dot(_bf(x * dmat[h]), v) + c for x, v, c, (_, h) in zip(sc, vh, cross, probs)]
    for i, (bb, h) in enumerate(probs):
        s_ref[bb, h] = st[i] * math.exp(lgs[h] * chunk) + upd[i]
    for bb in range(nb):
        outs = [_head_rms(o[bb * N_HEADS + h]) for h in range(N_HEADS)]
        y_ref[bb] = jnp.concatenate(outs, axis=1) * nw_ref[...] * _silu(z_ref[bb, :, 768:1024])


def _retention_seq(z_ret, cos, sin, nw, chunk, nb):
    b, l, _ = z_ret.shape
    return pl.pallas_call(
        functools.partial(_ret_body, chunk=chunk),
        grid=(b // nb, l // chunk),
        in_specs=[pl.BlockSpec((nb, chunk, 1024), lambda i, j: (i, j, 0)),
                  pl.BlockSpec((chunk, LANES), lambda i, j: (j, 0)),
                  pl.BlockSpec((chunk, LANES), lambda i, j: (j, 0)),
                  pl.BlockSpec((1, D_HEADS), lambda i, j: (0, 0))],
        out_specs=[pl.BlockSpec((nb, chunk, D_HEADS), lambda i, j: (i, j, 0)),
                   pl.BlockSpec((nb, N_HEADS, HEAD_DIM, HEAD_DIM), lambda i, j: (i, 0, 0, 0))],
        out_shape=[jax.ShapeDtypeStruct((b, l, D_HEADS), F32),
                   jax.ShapeDtypeStruct((b, N_HEADS, HEAD_DIM, HEAD_DIM), F32)],
        compiler_params=_cparams("parallel", "arbitrary"),
        name="retention_seq",
    )(z_ret, cos, sin, nw.reshape(1, -1))


def _lanes_to_rows(x, n=2 * SUBLANES):
    sel = (lax.broadcasted_iota(jnp.int32, (n, x.shape[1]), 0)
           == lax.broadcasted_iota(jnp.int32, (n, x.shape[1]), 1)).astype(F32)
    return _dot_nt(sel, x, precision=HI)


def _tri_incl(n):
    t = lax.broadcasted_iota(jnp.int32, (n, n), 0)
    s = lax.broadcasted_iota(jnp.int32, (n, n), 1)
    return t >= s


def _ml_body(z_ref, sm_ref, bias_ref, nw_ref, y_ref, c_ref, n_ref, m_ref, *, chunk):
    @pl.when(pl.program_id(1) == 0)
    def _():
        c_ref[...] = jnp.zeros_like(c_ref)
        n_ref[...] = jnp.zeros_like(n_ref)
        m_ref[...] = jnp.zeros_like(m_ref)

    nb = z_ref.shape[0]
    incl = _tri_incl(chunk)
    tri = incl.astype(F32)
    probs = [(bb, h) for bb in range(nb) for h in range(N_HEADS)]
    gate = []
    for bb in range(nb):
        pre = sm_ref[bb] + bias_ref[...]
        logf = jnp.minimum(pre, 0.0) - jnp.log(1.0 + jnp.exp(-jnp.abs(pre)))
        bcum = _dot(tri, logf, precision=HI)
        gate.append((pre, bcum, _lanes_to_rows(pre), _lanes_to_rows(bcum), m_ref[bb]))
    qh = [_head(z_ref[bb, :, 0:256], h) for bb, h in probs]
    kh = [_head(z_ref[bb, :, 256:512], h) * SCALE for bb, h in probs]
    vh = [_bf(_head(z_ref[bb, :, 512:768], h)) for bb, h in probs]
    cm = [c_ref[bb, h] for bb, h in probs]
    nv = [n_ref[bb, h:h + 1, :] for bb, h in probs]
    qk_raw = [_dot_nt(_bf(q), _bf(k)) for q, k in zip(qh, kh)]
    q_c = [_dot(_bf(q), _bf(c)) for q, c in zip(qh, cm)]
    i_col, b_col, inter, dlog = [], [], [], []
    for bb, h in probs:
        pre, bcum, pre_t, bcum_t, m_all = gate[bb]
        i_col.append(pre[:, SM_MI + h:SM_MI + h + 1])
        b_col.append(bcum[:, SM_MF + h:SM_MF + h + 1])
        inter.append(b_col[-1] + m_all[:, h:h + 1])
        dlog.append(jnp.where(incl, b_col[-1] - bcum_t[SM_MF + h:SM_MF + h + 1, :]
                              + pre_t[SM_MI + h:SM_MI + h + 1, :], NEG_INF))
    dmax = [jnp.max(x, axis=1, keepdims=True) for x in dlog]
    q_n = [jnp.sum(q * n, axis=1, keepdims=True) for q, n in zip(qh, nv)]
    stab = []
    for i in range(len(probs)):
        m_t = jnp.maximum(inter[i], dmax[i])
        m_end = m_t[chunk - 1:chunk, :]
        stab.append(dict(m_t=m_t, w=jnp.exp(dlog[i] - m_t), g_in=jnp.exp(inter[i] - m_t), m_end=m_end,
                         w_end=jnp.exp(b_col[i][chunk - 1:chunk, :] - b_col[i] + i_col[i] - m_end),
                         g_end=jnp.exp(inter[i][chunk - 1:chunk, :] - m_end)))
    qk = [x * d["w"] for x, d in zip(qk_raw, stab)]
    num = [_dot(_bf(x), v) + d["g_in"] * c for x, v, d, c in zip(qk, vh, stab, q_c)]
    upd = [_dot_tn(_bf(k * d["w_end"]), v) for k, d, v in zip(kh, stab, vh)]
    qk_sum = [jnp.sum(x, axis=1, keepdims=True) for x in qk]
    k_sum = [jnp.sum(d["w_end"] * k, axis=0, keepdims=True) for d, k in zip(stab, kh)]
    hh = []
    for i, (bb, h) in enumerate(probs):
        d = stab[i]
        den = qk_sum[i] + d["g_in"] * q_n[i]
        hh.append(num[i] / jnp.maximum(jnp.abs(den), jnp.exp(-d["m_t"])))
        c_ref[bb, h] = d["g_end"] * cm[i] + upd[i]
        n_ref[bb, h:h + 1, :] = d["g_end"] * nv[i] + k_sum[i]
    lane = lax.broadcasted_iota(jnp.int32, (1, LANES), 1)
    for bb in range(nb):
        m_out = jnp.zeros((1, LANES), F32)
        outs = []
        for h in range(N_HEADS):
            i = bb * N_HEADS + h
            m_out = jnp.where(lane == h, stab[i]["m_end"], m_out)
            outs.append(_head_rms(_sigmoid(_head(z_ref[bb, :, 768:1024], h)) * hh[i]))
        m_ref[bb] = m_out
        y_ref[bb] = jnp.concatenate(outs, axis=1) * nw_ref[...]


def _mlstm_seq(z_ml, small, bias_vec, nw, chunk, nb):
    b, l, _ = z_ml.shape
    return pl.pallas_call(
        functools.partial(_ml_body, chunk=chunk),
        grid=(b // nb, l // chunk),
        in_specs=[pl.BlockSpec((nb, chunk, 1024), lambda i, j: (i, j, 0)),
                  pl.BlockSpec((nb, chunk, LANES), lambda i, j: (i, j, 0)),
                  pl.BlockSpec((1, LANES), lambda i, j: (0, 0)),
                  pl.BlockSpec((1, D_HEADS), lambda i, j: (0, 0))],
        out_specs=[pl.BlockSpec((nb, chunk, D_HEADS), lambda i, j: (i, j, 0)),
                   pl.BlockSpec((nb, N_HEADS, HEAD_DIM, HEAD_DIM), lambda i, j: (i, 0, 0, 0)),
                   pl.BlockSpec((nb, N_HEADS, HEAD_DIM), lambda i, j: (i, 0, 0)),
                   pl.BlockSpec((nb, 1, LANES), lambda i, j: (i, 0, 0))],
        out_shape=[jax.ShapeDtypeStruct((b, l, D_HEADS), F32),
                   jax.ShapeDtypeStruct((b, N_HEADS, HEAD_DIM, HEAD_DIM), F32),
                   jax.ShapeDtypeStruct((b, N_HEADS, HEAD_DIM), F32),
                   jax.ShapeDtypeStruct((b, 1, LANES), F32)],
        compiler_params=_cparams("parallel", "arbitrary"),
        name="mlstm_seq",
    )(z_ml, small, bias_vec, nw.reshape(1, -1))


def _split_bf16(x):
    hi = _bf(x)
    return hi, _bf(x - hi.astype(F32))


def _dot3(a, b):
    (ah, al), (bh, bl) = a, b
    return _dot(ah, bh) + (_dot(ah, bl) + _dot(al, bh))


def _l2norm(x):
    return x * lax.rsqrt(jnp.sum(x * x, axis=-1, keepdims=True) + EPS)


def _gdn_body(z_ref, sm_ref, cw_ref, dtb_ref, alog_ref, nw_ref, y_ref, s_ref, buf_ref, *, chunk):
    @pl.when(pl.program_id(1) == 0)
    def _():
        s_ref[...] = jnp.zeros_like(s_ref)
        buf_ref[:, 0:SUBLANES, :] = jnp.zeros((buf_ref.shape[0], SUBLANES, 3 * D_HEADS), F32)

    nb = z_ref.shape[0]
    rows = z_ref.shape[1]
    nc = rows // chunk
    incl = _tri_incl(chunk)
    strict = lax.broadcasted_iota(jnp.int32, (chunk, chunk), 0) > lax.broadcasted_iota(jnp.int32, (chunk, chunk), 1)
    eye = (lax.broadcasted_iota(jnp.int32, (chunk, chunk), 0)
           == lax.broadcasted_iota(jnp.int32, (chunk, chunk), 1)).astype(F32)
    tri = incl.astype(F32)
    gates, acts = {}, []
    for bb in range(nb):
        buf_ref[bb, SUBLANES:SUBLANES + rows, :] = z_ref[bb, :, 0:768]
        conv = None
        for j in range(GDN_CONV):
            term = buf_ref[bb, pl.ds(SUBLANES - (GDN_CONV - 1) + j, rows), :] * cw_ref[j:j + 1, :]
            conv = term if conv is None else conv + term
        buf_ref[bb, 0:SUBLANES, :] = buf_ref[bb, rows:rows + SUBLANES, :]
        acts.append(_silu(conv))
        sm = sm_ref[bb]
        g_all = -jnp.exp(alog_ref[...]) * _softplus(sm + dtb_ref[...])
        beta_all = _sigmoid(sm)
        for cc in range(nc):
            rs = slice(cc * chunk, (cc + 1) * chunk)
            gcum = _dot(tri, g_all[rs], precision=HI)
            gates[bb, cc] = (gcum, _lanes_to_rows(gcum), beta_all[rs])

    probs = [(bb, cc, h) for cc in range(nc) for bb in range(nb) for h in range(N_HEADS)]
    pre = []
    for bb, cc, h in probs:
        rs = slice(cc * chunk, (cc + 1) * chunk)
        gcum, gcum_t, beta_all = gates[bb, cc]
        g_col = gcum[:, SM_GA + h:SM_GA + h + 1]
        g_row = gcum_t[SM_GA + h:SM_GA + h + 1, :]
        beta = beta_all[:, SM_GB + h:SM_GB + h + 1]
        decay = jnp.where(incl, jnp.exp(jnp.where(incl, g_col - g_row, 0.0)), 0.0)
        act = acts[bb]
        qh = _l2norm(_head(act[rs, 0:256], h)) * SCALE
        kh = _l2norm(_head(act[rs, 256:512], h))
        vh = _head(act[rs, 512:768], h)
        eg = jnp.exp(g_col)
        pre.append(dict(g_col=g_col, beta=beta, decay=decay, qh=qh, kh=kh, kb=_bf(kh), eg=eg,
                        rhs=jnp.concatenate([vh * beta, kh * (beta * eg)], axis=1)))
    a_l = [jnp.where(strict, d["beta"] * _dot_nt(d["kb"], d["kb"]) * d["decay"], 0.0) for d in pre]
    p_l = [eye - a for a in a_l]
    pw_s = [_split_bf16(a) for a in a_l]
    lvl = 2
    while lvl < chunk:
        pw_s = [_split_bf16(_dot3(x, x)) for x in pw_s]
        p_l = [p + _dot3(_split_bf16(p), x) for p, x in zip(p_l, pw_s)]
        lvl *= 2
    uw_l = [_dot3(_split_bf16(p), _split_bf16(d["rhs"])) for p, d in zip(p_l, pre)]
    qk_l = [_dot_nt(_bf(d["qh"]), d["kb"]) * d["decay"] for d in pre]
    per_chunk = nb * N_HEADS
    st_l = [s_ref[bb, h] for bb, cc, h in probs[:per_chunk]]
    o_all = []
    for cc in range(nc):
        sl = slice(cc * per_chunk, (cc + 1) * per_chunk)
        stb_l = [_bf(st) for st in st_l]
        delta_l = [uw[:, 0:HEAD_DIM] - _dot(_bf(uw[:, HEAD_DIM:]), stb) for uw, stb in zip(uw_l[sl], stb_l)]
        o_all += [_dot(_bf(qk), _bf(delta)) + _dot(_bf(d["qh"] * d["eg"]), stb)
                  for qk, delta, d, stb in zip(qk_l[sl], delta_l, pre[sl], stb_l)]
        new_st = []
        for d, st, delta in zip(pre[sl], st_l, delta_l):
            g_end = d["g_col"][chunk - 1:chunk, :]
            w_end = jnp.exp(g_end - d["g_col"])
            new_st.append(jnp.exp(g_end) * st + _dot_tn(_bf(d["kh"] * w_end), _bf(delta)))
        st_l = new_st
    for (bb, cc, h), st in zip(probs[:per_chunk], st_l):
        s_ref[bb, h] = st
    for bb in range(nb):
        for cc in range(nc):
            outs = [_head_rms(o_all[cc * per_chunk + bb * N_HEADS + h]) for h in range(N_HEADS)]
            rs = slice(cc * chunk, (cc + 1) * chunk)
            y_ref[bb, rs, :] = jnp.concatenate(outs, axis=1) * nw_ref[...] * _silu(z_ref[bb, rs, 768:1024])


def _gdn_seq(z_gdn, small, conv_w, dtb_vec, alog_vec, nw, chunk, nb, nc):
    b, l, _ = z_gdn.shape
    rows = nc * chunk
    return pl.pallas_call(
        functools.partial(_gdn_body, chunk=chunk),
        grid=(b // nb, l // rows),
        in_specs=[pl.BlockSpec((nb, rows, 1024), lambda i, j: (i, j, 0)),
                  pl.BlockSpec((nb, rows, LANES), lambda i, j: (i, j, 0)),
                  pl.BlockSpec((GDN_CONV, 3 * D_HEADS), lambda i, j: (0, 0)),
                  pl.BlockSpec((1, LANES), lambda i, j: (0, 0)),
                  pl.BlockSpec((1, LANES), lambda i, j: (0, 0)),
                  pl.BlockSpec((1, D_HEADS), lambda i, j: (0, 0))],
        out_specs=[pl.BlockSpec((nb, rows, D_HEADS), lambda i, j: (i, j, 0)),
                   pl.BlockSpec((nb, N_HEADS, HEAD_DIM, HEAD_DIM), lambda i, j: (i, 0, 0, 0))],
        out_shape=[jax.ShapeDtypeStruct((b, l, D_HEADS), F32),
                   jax.ShapeDtypeStruct((b, N_HEADS, HEAD_DIM, HEAD_DIM), F32)],
        scratch_shapes=[pltpu.VMEM((nb, SUBLANES + rows, 3 * D_HEADS), F32)],
        compiler_params=_cparams("parallel", "arbitrary"),
        name="gdn_seq",
    )(z_gdn, small, conv_w, dtb_vec, alog_vec, nw.reshape(1, -1))


STEP_SEQS = 4


def _row_to_col(row):
    n = row.shape[1]
    eye = lax.broadcasted_iota(jnp.int32, (n, n), 0) == lax.broadcasted_iota(jnp.int32, (n, n), 1)
    return jnp.sum(jnp.where(eye, jnp.broadcast_to(row, (n, n)), 0.0), axis=1, keepdims=True)


def _step_body(zr_ref, zm_ref, zg_ref, sm_ref, cos_ref, sin_ref, mlb_ref, dtb_ref, alog_ref, cw_ref,
               nwr_ref, nwm_ref, nwg_ref, sr_ref, mc_ref, mn_ref, mm_ref, gs_ref, gc_ref,
               yr_ref, ym_ref, yg_ref, sr_o, mc_o, mn_o, mm_o, gs_o, gc_o):
    nb = zr_ref.shape[0]
    d = HEAD_DIM
    eye = lax.broadcasted_iota(jnp.int32, (d, d), 0) == lax.broadcasted_iota(jnp.int32, (d, d), 1)
    lane = lax.broadcasted_iota(jnp.int32, (1, LANES), 1)
    cos, sin = cos_ref[...], sin_ref[...]
    rowsum = lambda x: jnp.sum(x, axis=1, keepdims=True)

    def vec_mat(row, mat):
        return _dot(_pad_rows(row), mat, precision=HI)[0:1, :]

    def outer(a_row, b_row):
        diag = jnp.where(eye, jnp.broadcast_to(a_row, (d, d)), 0.0)
        return _dot(diag, jnp.broadcast_to(b_row, (d, d)), precision=HI)

    ret, ml, gdn, seqs = [], [], [], []
    for bb in range(nb):
        sm, zr, zm, zg = sm_ref[bb], zr_ref[bb], zm_ref[bb], zg_ref[bb]
        rq = _rope256(zr[:, 0:256], cos, sin)
        rk = _rope256(zr[:, 256:512], cos, sin) * SCALE
        pre = sm + mlb_ref[...]
        logf = jnp.minimum(pre, 0.0) - jnp.log(1.0 + jnp.exp(-jnp.abs(pre)))
        m_all = mm_ref[bb]
        x = zg[:, 0:768]
        buf = gc_ref[bb]
        conv = x * cw_ref[GDN_CONV - 1:GDN_CONV, :]
        for j in range(GDN_CONV - 1):
            conv = conv + buf[j:j + 1, :] * cw_ref[j:j + 1, :]
        act = _silu(conv)
        g_all = -jnp.exp(alog_ref[...]) * _softplus(sm + dtb_ref[...])
        beta_all = _sigmoid(sm)
        seqs.append(dict(zr=zr, zm=zm, zg=zg, x=x, buf=buf))
        for h in range(N_HEADS):
            ret.append(dict(q=_head(rq, h), k=_head(rk, h), v=_head(zr[:, 512:768], h), st=sr_ref[bb, h],
                            gam=math.exp(_ret_log_decay(h))))
            i_g = pre[:, SM_MI + h:SM_MI + h + 1]
            inter = logf[:, SM_MF + h:SM_MF + h + 1] + m_all[:, h:h + 1]
            m_t = jnp.maximum(inter, i_g)
            ml.append(dict(q=_head(zm[:, 0:256], h), k=_head(zm[:, 256:512], h) * SCALE, v=_head(zm[:, 512:768], h),
                           cm=mc_ref[bb, h], nv=mn_ref[bb, h:h + 1, :], m_t=m_t, w=jnp.exp(i_g - m_t),
                           g_in=jnp.exp(inter - m_t)))
            gdn.append(dict(q=_head(act[:, 0:256], h), k=_head(act[:, 256:512], h), v=_head(act[:, 512:768], h),
                            st=gs_ref[bb, h], eg=jnp.exp(g_all[:, SM_GA + h:SM_GA + h + 1]),
                            beta=beta_all[:, SM_GB + h:SM_GB + h + 1]))
    for p in gdn:
        p["q"], p["k"] = _l2norm(p["q"]) * SCALE, _l2norm(p["k"])
    for p in ret + ml + gdn:
        p["qk"] = rowsum(p["q"] * p["k"])
    for p in ml:
        p["qn"] = rowsum(p["q"] * p["nv"])
    for p in ret:
        p["qs"] = vec_mat(p["q"] * p["gam"], p["st"])
    for p in ml:
        p["qc"] = vec_mat(p["q"], p["cm"])
    for p in gdn:
        p["ks"] = vec_mat(p["k"] * (p["beta"] * p["eg"]), p["st"])
        p["qs"] = vec_mat(p["q"] * p["eg"], p["st"])
    for p in ret:
        p["o"] = p["qk"] * p["v"] + p["qs"]
    for p in ml:
        qk = p["qk"] * p["w"]
        den = qk + p["g_in"] * p["qn"]
        p["o"] = (qk * p["v"] + p["g_in"] * p["qc"]) / jnp.maximum(jnp.abs(den), jnp.exp(-p["m_t"]))
    for p in gdn:
        p["delta"] = p["v"] * p["beta"] - p["ks"]
        p["o"] = p["qk"] * p["delta"] + p["qs"]
    for p in ret:
        p["new"] = p["st"] * p["gam"] + outer(p["k"], p["v"])
    for p in ml:
        p["new"] = p["g_in"] * p["cm"] + outer(p["k"] * p["w"], p["v"])
    for p in gdn:
        p["new"] = p["eg"] * p["st"] + outer(p["k"], p["delta"])
    for bb in range(nb):
        s = seqs[bb]
        ps = slice(bb * N_HEADS, (bb + 1) * N_HEADS)
        m_out = jnp.zeros((1, LANES), F32)
        for h, (pr, pm, pg) in enumerate(zip(ret[ps], ml[ps], gdn[ps])):
            sr_o[bb, h] = pr["new"]
            mc_o[bb, h] = pm["new"]
            mn_o[bb, h:h + 1, :] = pm["g_in"] * pm["nv"] + pm["w"] * pm["k"]
            gs_o[bb, h] = pg["new"]
            m_out = jnp.where(lane == h, pm["m_t"], m_out)
        mm_o[bb] = m_out
        gc_o[bb, 0:GDN_CONV - 2, :] = s["buf"][1:GDN_CONV - 1, :]
        gc_o[bb, GDN_CONV - 2:GDN_CONV - 1, :] = s["x"]
        yr_ref[bb] = (jnp.concatenate([_head_rms(p["o"]) for p in ret[ps]], axis=1) * nwr_ref[...]
                      * _silu(s["zr"][:, 768:1024]))
        og = s["zm"][:, 768:1024]
        ym_ref[bb] = jnp.concatenate([_head_rms(_sigmoid(_head(og, h)) * p["o"])
                                      for h, p in enumerate(ml[ps])], axis=1) * nwm_ref[...]
        yg_ref[bb] = (jnp.concatenate([_head_rms(p["o"]) for p in gdn[ps]], axis=1) * nwg_ref[...]
                      * _silu(s["zg"][:, 768:1024]))


def _mixers_step(z_ret, z_ml, z_gdn, small, cos, sin, mlb, dtb, alog, conv_w, nwr, nwm, nwg,
                 s_ret, ml_c, ml_n, ml_m, gdn_s, gdn_conv):
    b = z_ret.shape[0]
    nb = STEP_SEQS if b % STEP_SEQS == 0 else 1
    row = lambda n: pl.BlockSpec((nb, 1, n), lambda i: (i, 0, 0))
    vec = lambda n: pl.BlockSpec((1, n), lambda i: (0, 0))
    mat = pl.BlockSpec((nb, N_HEADS, HEAD_DIM, HEAD_DIM), lambda i: (i, 0, 0, 0))
    nsp = pl.BlockSpec((nb, N_HEADS, HEAD_DIM), lambda i: (i, 0, 0))
    csp = pl.BlockSpec((nb, GDN_CONV - 1, 3 * D_HEADS), lambda i: (i, 0, 0))
    mshape = jax.ShapeDtypeStruct((b, N_HEADS, HEAD_DIM, HEAD_DIM), F32)
    yshape = jax.ShapeDtypeStruct((b, 1, D_HEADS), F32)
    return pl.pallas_call(
        _step_body,
        grid=(b // nb,),
        in_specs=[row(1024), row(1024), row(1024), row(LANES), vec(LANES), vec(LANES), vec(LANES), vec(LANES), vec(LANES),
                  pl.BlockSpec((GDN_CONV, 3 * D_HEADS), lambda i: (0, 0)), vec(D_HEADS), vec(D_HEADS), vec(D_HEADS),
                  mat, mat, nsp, row(LANES), mat, csp],
        out_specs=[row(D_HEADS), row(D_HEADS), row(D_HEADS), mat, mat, nsp, row(LANES), mat, csp],
        out_shape=[yshape, yshape, yshape, mshape, mshape,
                   jax.ShapeDtypeStruct((b, N_HEADS, HEAD_DIM), F32), jax.ShapeDtypeStruct((b, 1, LANES), F32),
                   mshape, jax.ShapeDtypeStruct((b, GDN_CONV - 1, 3 * D_HEADS), F32)],
        compiler_params=_cparams("parallel"),
        name="mixers_step",
    )(z_ret, z_ml, z_gdn, small, cos, sin, mlb, dtb, alog, conv_w,
      nwr.reshape(1, -1), nwm.reshape(1, -1), nwg.reshape(1, -1), s_ret, ml_c, ml_n, ml_m, gdn_s, gdn_conv)


def _t5_bucket_np(n):
    exact = T5_BUCKETS // 2
    n = np.maximum(np.asarray(n, np.int64), 0)
    x = np.maximum(n, 1).astype(np.float32) / np.float32(exact)
    large = exact + (np.log(x) / np.float32(math.log(T5_MAX_DIST / exact)) * np.float32(T5_BUCKETS - exact)).astype(np.int32)
    return np.where(n < exact, n, np.minimum(large, T5_BUCKETS - 1)).astype(np.int32)


def _bucket_thresholds():
    b = _t5_bucket_np(np.arange(4 * T5_MAX_DIST))
    return [int(np.argmax(b >= k)) for k in range(T5_BUCKETS // 2 + 1, T5_BUCKETS)]


def _bias_table_body(rbt_ref, oh_ref, o_ref):
    o_ref[...] = _dot(rbt_ref[...], oh_ref[...].astype(F32), precision=HI)


def _bias_tables(rel_bias, dists):
    n = len(dists)
    tn = 4096
    npad = -(-n // tn) * tn
    onehot = np.zeros((T5_BUCKETS, npad), np.float32)
    onehot[_t5_bucket_np(dists), np.arange(n)] = 1.0
    rbt = jnp.pad(rel_bias.T, ((0, SUBLANES - N_HEADS), (0, 0)))
    out = pl.pallas_call(
        _bias_table_body,
        grid=(npad // tn,),
        in_specs=[pl.BlockSpec((SUBLANES, T5_BUCKETS), lambda i: (0, 0)),
                  pl.BlockSpec((T5_BUCKETS, tn), lambda i: (0, i))],
        out_specs=pl.BlockSpec((SUBLANES, tn), lambda i: (0, i)),
        out_shape=jax.ShapeDtypeStruct((SUBLANES, npad), F32),
        name="bias_tables",
    )(rbt, jnp.asarray(onehot, BF16))
    return out[:, :n]


def _prep_cmp_w(cmp_w, cmp_pe):
    eye2 = jnp.eye(2, dtype=F32)

    def half(w):
        full = jnp.einsum('cjde,cf->jcdfe', w, eye2)
        return full.reshape(CMP_STRIDE * 2 * HEAD_DIM, 2 * HEAD_DIM)

    wab = jnp.concatenate([half(cmp_w[:, :CMP_STRIDE]), half(cmp_w[:, CMP_STRIDE:])], axis=1)
    pe = jnp.concatenate([cmp_pe[:CMP_STRIDE].reshape(1, -1), cmp_pe[CMP_STRIDE:].reshape(1, -1)], axis=1)
    return wab, jnp.pad(pe, ((0, SUBLANES - 1), (0, 0)))


def _pe_term(pe_ref, wab_ref):
    kin = CMP_STRIDE * 2 * HEAD_DIM
    t = (_dot(pe_ref[:, 0:kin], wab_ref[:, 0:LANES].astype(F32), precision=HI)
         + _dot(pe_ref[:, kin:2 * kin], wab_ref[:, LANES:2 * LANES].astype(F32), precision=HI))
    return t[0:1, :]


def _combine_cmp(parts, pe_term):
    rows = parts.shape[0]
    nxt = pltpu.roll(parts[:, LANES:2 * LANES], rows - 1, 0)
    r = lax.broadcasted_iota(jnp.int32, (rows, LANES), 0)
    return jnp.where(r < rows - 1, parts[:, 0:LANES] + nxt + pe_term, 0.0)


def _compress_seq_body(kv_ref, wab_ref, pe_ref, o_ref):
    n16 = o_ref.shape[1]
    parts = jnp.zeros((n16, 2 * LANES), F32)
    for j in range(CMP_STRIDE):
        xj = kv_ref[0, pl.ds(j, n16, stride=CMP_STRIDE), :]
        parts = parts + _dot(_bf(xj), _bf(wab_ref[j * LANES:(j + 1) * LANES, :]))
    o_ref[0] = _combine_cmp(parts, _pe_term(pe_ref, wab_ref))


def _compress_seq(nkv, wab, pe):
    b, l, _ = nkv.shape
    n16 = l // CMP_STRIDE
    return pl.pallas_call(
        _compress_seq_body,
        grid=(b,),
        in_specs=[pl.BlockSpec((1, l, LANES), lambda i: (i, 0, 0)),
                  pl.BlockSpec(wab.shape, lambda i: (0, 0)),
                  pl.BlockSpec(pe.shape, lambda i: (0, 0))],
        out_specs=pl.BlockSpec((1, n16, LANES), lambda i: (i, 0, 0)),
        out_shape=jax.ShapeDtypeStruct((b, n16, LANES), F32),
        compiler_params=_cparams("parallel"),
        name="nsa_compress_seq",
    )(nkv, wab, pe)


def _softmax_rows(s, valid):
    s = jnp.where(valid, s, NEG_INF)
    m = jnp.max(s, axis=1, keepdims=True)
    m = jnp.where(m == NEG_INF, 0.0, m)
    p = jnp.exp(s - m)
    den = jnp.sum(p, axis=1, keepdims=True)
    return p / jnp.where(den > 0, den, 1.0)


def _stack_heads(x):
    return jnp.concatenate([_head(x, h) for h in range(N_HEADS)], axis=0)


def _toeplitz(tab_ref, h, r0, rows, ncol):
    nrow = (rows + ncol) // LANES
    strip = tab_ref[h, pl.ds(r0, nrow), :]
    flat = jnp.concatenate([strip[r:r + 1, :] for r in range(nrow)], axis=1)
    rolled = pltpu.roll(jnp.broadcast_to(flat, (rows, nrow * LANES)), 0, 1, stride=1, stride_axis=0)
    return rolled[:, rows:]


SLC_KC = 512


MASK_BIG = 1e30


def _top_blocks_t(score, k):
    row = lax.broadcasted_iota(jnp.int32, score.shape, 0)
    n = score.shape[0]
    sel = jnp.zeros(score.shape, jnp.bool_)
    for _ in range(k):
        m = jnp.max(score, axis=0, keepdims=True)
        idx = jnp.min(jnp.where(score == m, row, n), axis=0, keepdims=True)
        pick = (row == idx) & (m > NEG_INF)
        sel = sel | pick
        score = jnp.where(pick, NEG_INF, score)
    return sel


def _bucket_index(dist):
    n = jnp.maximum(dist, 0)
    exact = T5_BUCKETS // 2
    big = jnp.full(dist.shape, exact, jnp.int32)
    for thr in _bucket_thresholds():
        big = big + (n >= thr).astype(jnp.int32)
    return jnp.where(n < exact, n, big)


def _gather_bias(rbt_ref, h, bucket):
    rows, width = bucket.shape
    piece = min(width, LANES)
    tab = jnp.broadcast_to(rbt_ref[h:h + 1, 0:piece], (rows, piece))
    parts = [jnp.take_along_axis(tab, bucket[:, s:s + piece], axis=1) for s in range(0, width, piece)]
    return parts[0] if len(parts) == 1 else jnp.concatenate(parts, axis=1)


def _tile_rows(x, n=N_HEADS):
    return jnp.concatenate([x] * n, axis=0)


def _near_chunks():
    return (_bucket_thresholds()[-1] + SLC_KC - 2) // SLC_KC + 1


def _bias_tile(tab_ref, base, ncol, p0, lo, hi):
    r0 = (p0 - base - Q_BLOCK) // LANES
    bias = jnp.concatenate([_toeplitz(tab_ref, h, r0, Q_BLOCK, ncol) for h in range(N_HEADS)], axis=0)
    dist = (base + lax.broadcasted_iota(jnp.int32, (Q_BLOCK, ncol), 0)
            - lax.broadcasted_iota(jnp.int32, (Q_BLOCK, ncol), 1))
    return bias + _tile_rows(jnp.where((dist >= lo) & (dist < hi), 0.0, -MASK_BIG))


def _nsa_seq_body(rb_ref, q_ref, sm_ref, kx_ref, vs_ref, kvw_ref, kvc_ref, tab_ref, rbt_ref, ovlt_ref, y_ref,
                  stile_ref, wtile_ref, ctile_ref, *, seq, p0):
    qb = Q_BLOCK
    rows = N_HEADS * qb
    n16 = seq // CMP_STRIDE
    nblk = seq // SLC_BLOCK
    bi = pl.program_id(1)
    q0 = bi * qb

    @pl.when((pl.program_id(0) == 0) & (bi == 0))
    def _():
        per_chunk = SLC_KC // qb

        def slc_tile(t, carry):
            base = (t // _near_chunks()) * qb + (t % _near_chunks()) * SLC_KC
            stile_ref[t] = _bias_tile(tab_ref, base, SLC_KC, p0, 0, seq + SLC_KC)
            return carry

        def win_tile(t, carry):
            wtile_ref[t] = _bias_tile(tab_ref, t * qb, WINDOW + qb, p0, 0, WINDOW)
            return carry

        def cmp_tile(t, carry):
            dist = (t * qb + lax.broadcasted_iota(jnp.int32, (qb, n16), 0)
                    - lax.broadcasted_iota(jnp.int32, (qb, n16), 1) * CMP_STRIDE - (CMP_LEN - 1))
            bucket = _bucket_index(dist)
            bias = jnp.concatenate([_gather_bias(rbt_ref, h, bucket) for h in range(N_HEADS)], axis=0)
            ctile_ref[t] = bias + _tile_rows(jnp.where(dist >= 0, 0.0, -MASK_BIG))
            return carry

        lax.fori_loop(0, per_chunk * _near_chunks(), slc_tile, 0)
        lax.fori_loop(0, WINDOW // qb + 1, win_tile, 0)
        lax.fori_loop(0, seq // qb, cmp_tile, 0)
    qs = _stack_heads(q_ref[0]) * SCALE
    qsb = _bf(qs)

    kvc = kvc_ref[0]
    s_c = _dot_nt(qsb, _bf(kvc[:, 0:HEAD_DIM]))
    band = WINDOW + qb
    sw = pl.multiple_of(jnp.maximum(q0 - WINDOW, 0), qb)
    kw = kvw_ref[0, pl.ds(sw, band), 0:HEAD_DIM]
    vw = kvw_ref[0, pl.ds(sw, band), HEAD_DIM:2 * HEAD_DIM]
    s_w = _dot_nt(qsb, kw)

    sb_c = s_c + ctile_ref[bi]
    m_c = jnp.max(sb_c, axis=1, keepdims=True)
    e_c = jnp.exp(sb_c - m_c)
    den_c = jnp.sum(e_c, axis=1, keepdims=True)
    p_cb = _bf(e_c * jnp.where(m_c > -MASK_BIG / 10, 1.0 / den_c, 0.0))
    o_c = _dot(p_cb, _bf(kvc[:, HEAD_DIM:2 * HEAD_DIM]))
    p_r = p_cb.astype(F32)
    p_sum = p_r[0:qb] + p_r[qb:2 * qb] + p_r[2 * qb:3 * qb] + p_r[3 * qb:4 * qb]
    imp_t = _dot_nt(ovlt_ref[...], p_sum, precision=HI)

    base_w = q0 - sw
    sb_w = s_w + wtile_ref[base_w // qb]
    e_w = jnp.exp(sb_w - jnp.max(sb_w, axis=1, keepdims=True))
    o_w = _dot(_bf(e_w * (1.0 / jnp.sum(e_w, axis=1, keepdims=True))), vw)

    qlane = q0 + lax.broadcasted_iota(jnp.int32, (1, qb), 1)
    cur = qlane // SLC_BLOCK
    rowb = lax.broadcasted_iota(jnp.int32, (nblk, qb), 0)
    forced = (rowb == 0) | (rowb == cur) | (rowb == cur - 1)
    score = jnp.where(forced | (rowb > cur), NEG_INF, imp_t)
    sel = jnp.where(forced | _top_blocks_t(score, SLC_TOPK - 3), 1.0, 0.0).T
    negm = _bf(_tile_rows((sel - 1.0) * MASK_BIG))
    q_ext = jnp.concatenate([qsb, negm], axis=1)

    thr_far = _bucket_thresholds()[-1] + SLC_KC - 1
    n_chunks = q0 // SLC_KC + 1
    n_far = jnp.maximum(q0 - thr_far + SLC_KC, 0) // SLC_KC
    n_near = _near_chunks()

    far_bias = jnp.concatenate([jnp.full((qb, 1), rb_ref[T5_BUCKETS - 1, h], F32) for h in range(N_HEADS)], axis=0)

    def scores(c):
        k0 = pl.multiple_of(c * SLC_KC, SLC_KC)
        kx = kx_ref[0, pl.ds(k0, SLC_KC), :]
        return q0 - k0, _dot_nt(q_ext, kx), vs_ref[0, pl.ds(k0, SLC_KC), :]

    def online(carry, s, vs):
        m, l, acc = carry
        m_new = jnp.maximum(m, jnp.max(s, axis=1, keepdims=True))
        alpha = jnp.exp(m - m_new)
        p = jnp.exp(s - m_new)
        return m_new, alpha * l + jnp.sum(p, axis=1, keepdims=True), alpha * acc + _dot(_bf(p), vs)

    def far_step(c, carry):
        _, s, vs = scores(c)
        return online(carry, s + far_bias, vs)

    def near_step(c, carry):
        base, s, vs = scores(c)
        tile = ((q0 % SLC_KC) // qb) * n_near + base // SLC_KC
        return online(carry, s + stile_ref[tile], vs)

    init = (jnp.full((rows, 1), -MASK_BIG / 10, F32), jnp.zeros((rows, 1), F32), jnp.zeros((rows, HEAD_DIM), F32))
    carry = lax.fori_loop(0, n_far, far_step, init)
    _, l_s, acc_s = lax.fori_loop(n_far, n_chunks, near_step, carry)
    o_s = acc_s / jnp.where(l_s > 0, l_s, 1.0)

    gates = _sigmoid(sm_ref[0])
    outs = []
    for h in range(N_HEADS):
        rs = slice(h * qb, (h + 1) * qb)
        g = lambda t: gates[:, SM_NG + 3 * h + t:SM_NG + 3 * h + t + 1]
        outs.append(g(0) * o_c[rs] + g(1) * o_s[rs] + g(2) * o_w[rs])
    y_ref[0] = jnp.concatenate(outs, axis=1)


def _nsa_seq(rel_bias, nq, small, kx, vs, kvw, kvc, tab, seq_p0):
    b, l, _ = nq.shape
    n16 = l // CMP_STRIDE
    nblk = l // SLC_BLOCK
    cmp_start = np.arange(n16)[None, :] * CMP_STRIDE
    slc_start = np.arange(nblk)[:, None] * SLC_BLOCK
    ovlt = ((cmp_start < slc_start + SLC_BLOCK) & (cmp_start + CMP_LEN > slc_start)
            & (np.arange(n16)[None, :] < n16 - 1)).astype(np.float32)
    rbt = jnp.pad(rel_bias.T, ((0, SUBLANES - N_HEADS), (0, LANES - T5_BUCKETS)))
    nrows = tab.shape[1]
    const = lambda shape: pl.BlockSpec(shape, lambda i, j: (0,) * len(shape))
    seq_spec = lambda n: pl.BlockSpec((1, l, n), lambda i, j: (i, 0, 0))
    return pl.pallas_call(
        functools.partial(_nsa_seq_body, seq=l, p0=seq_p0),
        grid=(b, l // Q_BLOCK),
        in_specs=[pl.BlockSpec(memory_space=pltpu.SMEM),
                  pl.BlockSpec((1, Q_BLOCK, D_HEADS), lambda i, j: (i, j, 0)),
                  pl.BlockSpec((1, Q_BLOCK, LANES), lambda i, j: (i, j, 0)),
                  seq_spec(HEAD_DIM + nblk), seq_spec(HEAD_DIM), seq_spec(2 * HEAD_DIM),
                  pl.BlockSpec((1, n16, LANES), lambda i, j: (i, 0, 0)),
                  const((SUBLANES, nrows, LANES)), const((SUBLANES, LANES)), const((nblk, n16))],
        out_specs=pl.BlockSpec((1, Q_BLOCK, D_HEADS), lambda i, j: (i, j, 0)),
        out_shape=jax.ShapeDtypeStruct((b, l, D_HEADS), F32),
        scratch_shapes=[pltpu.VMEM(((SLC_KC // Q_BLOCK) * _near_chunks(), N_HEADS * Q_BLOCK, SLC_KC), F32),
                        pltpu.VMEM((WINDOW // Q_BLOCK + 1, N_HEADS * Q_BLOCK, WINDOW + Q_BLOCK), F32),
                        pltpu.VMEM((l // Q_BLOCK, N_HEADS * Q_BLOCK, n16), F32)],
        compiler_params=_cparams("arbitrary", "arbitrary"),
        name="nsa_seq",
    )(rel_bias, nq, small, kx, vs, kvw, kvc, tab, rbt, jnp.asarray(ovlt))


def _nsa_tab_seq_dists(seq):
    max_base = max(seq, SLC_KC + (_near_chunks() - 1) * SLC_KC)
    p0 = max_base + Q_BLOCK
    length = p0 + WINDOW + Q_BLOCK + LANES
    return p0, np.maximum(p0 - np.arange(length), 0)


PAGE_GROUP = 16
PAGES_PER_STEP = 128


def _pad_rows(x, rows=SUBLANES):
    return jnp.concatenate([x, jnp.zeros((rows - x.shape[0], x.shape[1]), x.dtype)], axis=0)


def _nsa_select_body(pt_ref, *refs, past, n_steps):
    del pt_ref
    pages = refs[:PAGES_PER_STEP]
    wab_ref, pe_ref, q_ref, tabc_ref, ovlt_ref, oc_ref, top_ref, parts_ref, rows_ref = refs[PAGES_PER_STEP:]
    step = pl.program_id(1)
    blocks_per_page = PAGE_SIZE // CMP_STRIDE
    group_rows = PAGE_GROUP * blocks_per_page
    n_groups = PAGES_PER_STEP // PAGE_GROUP
    for p, pg in enumerate(pages):
        g, pp = divmod(p, PAGE_GROUP)
        rows_ref[g, pp * PAGE_SIZE:(pp + 1) * PAGE_SIZE, :] = pg[...].reshape(2 * HEAD_DIM, PAGE_SIZE).T
    for g in range(n_groups):
        acc = jnp.zeros((group_rows, 2 * LANES), F32)
        for j in range(CMP_STRIDE):
            xj = rows_ref[g, pl.ds(j, group_rows, stride=CMP_STRIDE), :]
            acc = acc + _dot(_bf(xj), wab_ref[j * LANES:(j + 1) * LANES, :])
        row0 = pl.multiple_of((step * n_groups + g) * group_rows, group_rows)
        parts_ref[pl.ds(row0, group_rows), :] = acc

    @pl.when(step == n_steps - 1)
    def _():
        n16 = past // CMP_STRIDE
        n_cmp = n16 - 1
        cur = past // SLC_BLOCK
        pe_term = _pe_term(pe_ref, wab_ref)
        kvc = _combine_cmp(parts_ref[...], pe_term)
        qs = _pad_rows(_stack_heads(q_ref[0]) * SCALE)
        s_c = _dot_nt(_bf(qs), _bf(kvc[:, 0:HEAD_DIM])) + tabc_ref[...]
        lane = lax.broadcasted_iota(jnp.int32, (SUBLANES, n16), 1)
        p_c = _softmax_rows(s_c, lane < n_cmp)
        p_cb = _bf(p_c)
        o_c = _dot(p_cb, _bf(kvc[:, HEAD_DIM:2 * HEAD_DIM]))
        oc_ref[0] = jnp.concatenate([o_c[h:h + 1, :] for h in range(N_HEADS)], axis=1)
        imp_h = _dot_nt(ovlt_ref[...], p_cb)
        imp = imp_h[:, 0:1] + imp_h[:, 1:2] + imp_h[:, 2:3] + imp_h[:, 3:4]
        nrow = imp.shape[0]
        rowb = lax.broadcasted_iota(jnp.int32, (nrow, 1), 0)
        forced_ids = (0, cur - 1, cur)
        forced = (rowb == 0) | (rowb == cur) | (rowb == cur - 1)
        score = jnp.where(forced | (rowb > cur), NEG_INF, imp)
        out_lane = lax.broadcasted_iota(jnp.int32, (1, LANES), 1)
        top = jnp.zeros((1, LANES), jnp.int32)
        for r, blk in enumerate(forced_ids):
            top = jnp.where(out_lane == r, blk, top)
        for r in range(len(forced_ids), SLC_TOPK):
            m = jnp.max(score, axis=0, keepdims=True)
            idx = jnp.min(jnp.where(score == m, rowb, nrow), axis=0, keepdims=True)
            top = jnp.where(out_lane == r, idx, top)
            score = jnp.where(rowb == idx, NEG_INF, score)
        top_ref[0] = top


def _nsa_select(layer, page_table, cache_t, wab_bf, pe, nq, tabc, past):
    b, n_pages = page_table.shape
    n_steps = n_pages // PAGES_PER_STEP
    n16 = past // CMP_STRIDE
    n_slc = past // SLC_BLOCK + 1
    nrow = -(-n_slc // SUBLANES) * SUBLANES
    cmp_start = np.arange(n16)[None, :] * CMP_STRIDE
    slc_start = np.arange(nrow)[:, None] * SLC_BLOCK
    ovlt = ((cmp_start < slc_start + SLC_BLOCK) & (cmp_start + CMP_LEN > slc_start)
            & (np.arange(n16)[None, :] < n16 - 1) & (np.arange(nrow)[:, None] < n_slc)).astype(np.float32)

    n_pool = cache_t.shape[1]

    def page_spec(p):
        def imap(i, s, pt):
            pg = pt[jnp.minimum(i, b - 1), jnp.minimum(s, n_steps - 1) * PAGES_PER_STEP + p]
            return (layer, jnp.minimum(pg, n_pool - 1), 0, 0, 0)
        return pl.BlockSpec((None, None, 2, HEAD_DIM, PAGE_SIZE), imap)

    const = lambda shape: pl.BlockSpec(shape, lambda i, s, pt: (0,) * len(shape))
    grid_spec = pltpu.PrefetchScalarGridSpec(
        num_scalar_prefetch=1,
        grid=(b, n_steps),
        in_specs=[page_spec(p) for p in range(PAGES_PER_STEP)]
        + [const(wab_bf.shape), const(pe.shape), pl.BlockSpec((1, 1, D_HEADS), lambda i, s, pt: (i, 0, 0)),
           const(tabc.shape), const(ovlt.shape)],
        out_specs=[pl.BlockSpec((1, 1, D_HEADS), lambda i, s, pt: (i, 0, 0)),
                   pl.BlockSpec((1, 1, LANES), lambda i, s, pt: (i, 0, 0))],
        scratch_shapes=[pltpu.VMEM((n16, 2 * LANES), F32),
                        pltpu.VMEM((PAGES_PER_STEP // PAGE_GROUP, PAGE_GROUP * PAGE_SIZE, 2 * HEAD_DIM), F32)],
    )
    return pl.pallas_call(
        functools.partial(_nsa_select_body, past=past, n_steps=n_steps),
        grid_spec=grid_spec,
        out_shape=[jax.ShapeDtypeStruct((b, 1, D_HEADS), F32), jax.ShapeDtypeStruct((b, 1, LANES), jnp.int32)],
        compiler_params=_cparams("parallel", "arbitrary"),
        name="nsa_select",
    )(page_table, *([cache_t] * PAGES_PER_STEP), wab_bf, pe, nq, tabc, jnp.asarray(ovlt, BF16))


def _nsa_attend_body(pt_ref, top_ref, *refs, past):
    del pt_ref
    pages = refs[:SLC_TOPK]
    (q_ref, sm_ref, new_ref, oc_ref, swa_ref, tabs_ref, tabw_ref, y_ref, swa_out) = refs[SLC_TOPK:]
    b = pl.program_id(0)
    new_blk = past // SLC_BLOCK
    blocks_per_page = PAGE_SIZE // SLC_BLOCK
    qs = _bf(_pad_rows(_stack_heads(q_ref[0]) * SCALE))
    new = new_ref[0]
    lane_t = lax.broadcasted_iota(jnp.int32, (HEAD_DIM, PAGE_SIZE), 1)
    lane = lax.broadcasted_iota(jnp.int32, (SUBLANES, PAGE_SIZE), 1)
    new_kt = jnp.where(lane_t == 0, _row_to_col(new[:, 2 * HEAD_DIM:3 * HEAD_DIM]), 0.0)
    new_vt = jnp.where(lane_t == 0, _row_to_col(new[:, 3 * HEAD_DIM:4 * HEAD_DIM]), 0.0)
    kts, biases, oks, vts = [], [], [], []
    for s in range(SLC_TOPK):
        ti = jnp.clip(top_ref[b, s], 0, new_blk)
        is_new = ti == new_blk
        kts.append(jnp.where(is_new, new_kt, pages[s][0]))
        vts.append(jnp.where(is_new, new_vt, pages[s][1]))
        biases.append(tabs_ref[ti])
        kpos = ti * SLC_BLOCK + lane % SLC_BLOCK
        ok = (lane // SLC_BLOCK == ti % blocks_per_page) & (kpos <= past)
        oks.append(jnp.where(ok, 1.0, 0.0))
    wlane = lax.broadcasted_iota(jnp.int32, (HEAD_DIM, WINDOW), 1)
    win = []
    for c in range(2):
        col = _row_to_col(new[:, (4 + c) * HEAD_DIM:(5 + c) * HEAD_DIM])
        win.append(jnp.where(wlane == WINDOW - 1, col, pltpu.roll(swa_ref[c], WINDOW - 1, 1)))
    s_w = _dot(qs, _bf(win[0])) + tabw_ref[...]

    s_s = _dot(qs, _bf(jnp.concatenate(kts, axis=1))) + jnp.concatenate(biases, axis=1)
    p_s = _softmax_rows(s_s, jnp.concatenate(oks, axis=1) > 0.5)
    p_w = _softmax_rows(s_w, jnp.full(s_w.shape, True))
    o_s = _dot_nt(_bf(p_s), _bf(jnp.concatenate(vts, axis=1)))
    o_w = _dot_nt(_bf(p_w), _bf(win[1]))
    for c in range(2):
        swa_out[c] = win[c]

    gates = _sigmoid(sm_ref[0])
    o_c = oc_ref[0]
    outs = []
    for h in range(N_HEADS):
        g = lambda t: gates[:, SM_NG + 3 * h + t:SM_NG + 3 * h + t + 1]
        outs.append(g(0) * _head(o_c, h) + g(1) * o_s[h:h + 1, :] + g(2) * o_w[h:h + 1, :])
    y_ref[0] = jnp.concatenate(outs, axis=1)


def _nsa_attend(layer, page_table, top, cache_t, nq, small, nkv, o_c, swa_t, tabs, tabw, past):
    b = nq.shape[0]
    last_page = past // PAGE_SIZE - 1
    blocks_per_page = PAGE_SIZE // SLC_BLOCK

    n_pool = cache_t.shape[1]

    def page_spec(s):
        def imap(i, pt, tp):
            ii = jnp.minimum(i, b - 1)
            pg = jnp.clip(tp[ii, s] // blocks_per_page, 0, last_page)
            return (layer, jnp.clip(pt[ii, pg], 0, n_pool - 1), 1, 0, 0)
        return pl.BlockSpec((None, None, 2, HEAD_DIM, PAGE_SIZE), imap)

    row = lambda n: pl.BlockSpec((1, 1, n), lambda i, pt, tp: (i, 0, 0))
    const = lambda shape: pl.BlockSpec(shape, lambda i, pt, tp: (0,) * len(shape))
    grid_spec = pltpu.PrefetchScalarGridSpec(
        num_scalar_prefetch=2,
        grid=(b,),
        in_specs=[page_spec(s) for s in range(SLC_TOPK)]
        + [row(D_HEADS), row(LANES), row(6 * HEAD_DIM), row(D_HEADS),
           pl.BlockSpec((None, None, 2, HEAD_DIM, WINDOW), lambda i, pt, tp: (layer, i, 0, 0, 0)),
           const(tabs.shape), const(tabw.shape)],
        out_specs=[row(D_HEADS), pl.BlockSpec((None, 2, HEAD_DIM, WINDOW), lambda i, pt, tp: (i, 0, 0, 0))],
    )
    return pl.pallas_call(
        functools.partial(_nsa_attend_body, past=past),
        grid_spec=grid_spec,
        out_shape=[jax.ShapeDtypeStruct((b, 1, D_HEADS), F32),
                   jax.ShapeDtypeStruct((b, 2, HEAD_DIM, WINDOW), F32)],
        compiler_params=_cparams("parallel"),
        name="nsa_attend",
    )(page_table, top, *([cache_t] * SLC_TOPK), nq, small, nkv, o_c, swa_t, tabs, tabw)


SEQ_CHUNK = 64
GDN_CHUNKS = 1
SCAN_CHUNK = 256
ROW_TILE = 512
FFN_TILE = 512


def _lane_vec(pairs):
    v = jnp.zeros((1, LANES), F32)
    for off, vals in pairs:
        v = v.at[0, off:off + vals.shape[0]].set(vals.astype(F32))
    return v


def kernel(x_prompt, x_sample, cache_nsa_kv, cache_swa_kv, state_ret, state_mlstm_C, state_mlstm_n, state_mlstm_m, state_gdn, state_gdn_conv, state_ffn_conv, page_table, rel_bias, norm_pre_mix, w_in, mlstm_b_i, mlstm_b_f, gdn_conv_w, gdn_A_log, gdn_dt_bias, nsa_cmp_pe, nsa_cmp_w, ret_norm, mlstm_norm, gdn_norm, w_out, norm_post_mix, norm_pre_ffn, w_ffn_gate, w_ffn_up, ffn_conv_w, w_ffn_down, norm_post_ffn):
    depth = w_in.shape[0]
    bp, seq, _ = x_prompt.shape
    bs, dec_seq, _ = x_sample.shape
    assert dec_seq == 1 and seq % SLC_KC == 0 and seq >= WINDOW + Q_BLOCK
    n_pool = cache_nsa_kv.shape[1]
    past = page_table.shape[1] * PAGE_SIZE
    assert past >= WINDOW and past % SLC_BLOCK == 0 and page_table.shape[1] % PAGES_PER_STEP == 0
    assert past // SLC_BLOCK + 1 >= SLC_TOPK

    p0, d_seq = _nsa_tab_seq_dists(seq)
    n16 = past // CMP_STRIDE
    n_slc = past // SLC_BLOCK + 1
    d_cmp = np.maximum(past - (np.arange(n16) * CMP_STRIDE + CMP_LEN - 1), 0)
    d_slc = np.maximum(past - np.arange(n_slc * SLC_BLOCK), 0)
    d_win = WINDOW - 1 - np.arange(WINDOW)
    tab = _bias_tables(rel_bias, np.concatenate([d_seq, d_cmp, d_slc, d_win]))
    o1, o2, o3 = len(d_seq), len(d_seq) + len(d_cmp), len(d_seq) + len(d_cmp) + len(d_slc)
    tab_seq = tab[:, :o1].reshape(SUBLANES, o1 // LANES, LANES)
    tabc = tab[:, o1:o2]
    tabs = jnp.transpose(tab[:, o2:o3].reshape(SUBLANES, n_slc, SLC_BLOCK), (1, 0, 2))
    tabs = jnp.concatenate([tabs, tabs], axis=2)
    tabw = tab[:, o3:]

    cache_t = jnp.transpose(cache_nsa_kv, (0, 1, 3, 4, 2))
    swa_t = jnp.transpose(cache_swa_kv, (0, 1, 3, 4, 2))
    cos_p, sin_p = _rope_tables(jnp.arange(seq))
    cos_s, sin_s = _rope_tables(jnp.arange(past, past + 1))

    xp = x_prompt
    xs = x_sample.reshape(bs, D_MODEL)
    p_states, s_states = [], []
    for l in range(depth):
        w_cat = _prep_w_in(w_in[l])
        w_out_bf = _bf(w_out[l])
        wg, wu, wd = _bf(w_ffn_gate[l]), _bf(w_ffn_up[l]), _bf(w_ffn_down[l])
        wab, pe = _prep_cmp_w(nsa_cmp_w[l], nsa_cmp_pe[l])
        ml_bias = _lane_vec([(SM_MI, mlstm_b_i[l]), (SM_MF, mlstm_b_f[l])])
        dtb = _lane_vec([(SM_GA, gdn_dt_bias[l])])
        alog = _lane_vec([(SM_GA, gdn_A_log[l])])

        z = _in_proj(xp.reshape(bp * seq, D_MODEL), norm_pre_mix[l], w_cat, ROW_TILE, seq=seq)
        z_ret, z_ml, z_gdn, z_nq, z_nkv, z_sm = [t.reshape(bp, seq, -1) for t in z[:6]]
        nkv_t = z[6].reshape(bp, 6, HEAD_DIM, seq)
        kx, vs, kvw = [t.reshape(bp, seq, -1) for t in z[7:]]
        y_ret, st_ret = _retention_seq(z_ret, cos_p, sin_p, ret_norm[l], SCAN_CHUNK, bp)
        y_ml, st_c, st_n, st_m = _mlstm_seq(z_ml, z_sm, ml_bias, mlstm_norm[l], SCAN_CHUNK, bp)
        y_gdn, st_g = _gdn_seq(z_gdn, z_sm, gdn_conv_w[l], dtb, alog, gdn_norm[l], SEQ_CHUNK, bp, GDN_CHUNKS)
        kvc = _compress_seq(z_nkv, wab, pe)
        y_nsa = _nsa_seq(rel_bias, z_nq, z_sm, kx, vs, kvw, kvc, tab_seq, p0)
        xp, tail = _ffn_seq(xp, (y_ret, y_ml, y_gdn, y_nsa), w_out_bf, norm_post_mix[l], norm_pre_ffn[l],
                            wg, wu, wd, ffn_conv_w[l], norm_post_ffn[l], min(seq, FFN_TILE))
        p_states.append((
            jnp.transpose(nkv_t[:, 0:4], (0, 3, 1, 2)),
            jnp.transpose(nkv_t[:, 4:6, :, seq - WINDOW:], (0, 3, 1, 2)),
            st_ret, st_c, st_n, st_m[:, 0, :N_HEADS], st_g,
            z_gdn[:, seq - (GDN_CONV - 1):, 0:3 * D_HEADS],
            tail[:, SUBLANES - (FFN_CONV - 1):, :]))

        z = _in_proj(xs, norm_pre_mix[l], w_cat, bs)
        z_ret, z_ml, z_gdn, z_nq, z_nkv, z_sm = [t.reshape(bs, 1, -1) for t in z]
        m_pad = jnp.pad(state_mlstm_m[l], ((0, 0), (0, LANES - N_HEADS))).reshape(bs, 1, LANES)
        (y_ret, y_ml, y_gdn, st_ret, st_c, st_n, st_m, st_g, st_gc) = _mixers_step(
            z_ret, z_ml, z_gdn, z_sm, cos_s, sin_s, ml_bias, dtb, alog, gdn_conv_w[l],
            ret_norm[l], mlstm_norm[l], gdn_norm[l],
            state_ret[l], state_mlstm_C[l], state_mlstm_n[l], m_pad, state_gdn[l], state_gdn_conv[l])
        o_c, top = _nsa_select(l, page_table, cache_t, _bf(wab), pe, z_nq, tabc, past)
        y_nsa, swa_new = _nsa_attend(l, page_table, top[:, 0, :SLC_TOPK], cache_t, z_nq, z_sm, z_nkv, o_c,
                                     swa_t, tabs, tabw, past)
        ys = [t.reshape(bs, D_HEADS) for t in (y_ret, y_ml, y_gdn, y_nsa)]
        x1, h2 = _out_proj(xs, ys, w_out_bf, norm_post_mix[l], norm_pre_ffn[l], bs)
        xs, g_new = _ffn_step(x1, h2, state_ffn_conv[l][:, 0], state_ffn_conv[l][:, 1], wg, wu, wd,
                              ffn_conv_w[l], norm_post_ffn[l])
        s_states.append((
            z_nkv[..., 0:4 * HEAD_DIM].reshape(bs, 1, 4, HEAD_DIM),
            jnp.transpose(swa_new, (0, 3, 1, 2)),
            st_ret, st_c, st_n, st_m[:, 0, :N_HEADS], st_g, st_gc,
            jnp.concatenate([state_ffn_conv[l][:, 1:], g_new[:, None, :]], axis=1)))

    stack = lambda states: tuple(jnp.stack([s[k] for s in states]) for k in range(len(states[0])))
    return (xp, xs.reshape(bs, 1, D_MODEL)) + stack(p_states) + stack(s_states)
```

```python
import functools
import math

import jax
import jax.numpy as jnp
import numpy as np
from jax import lax
from jax.experimental import pallas as pl
from jax.experimental.pallas import tpu as pltpu

F32 = jnp.float32
BF16 = jnp.bfloat16
HI = lax.Precision.HIGHEST

D_MODEL = 1024
HEAD_DIM = 64
N_HEADS = 4
D_HEADS = N_HEADS * HEAD_DIM
D_FF = 2816
GDN_CONV = 4
FFN_CONV = 3
PAGE_SIZE = 128
CMP_STRIDE = 16
CMP_LEN = 32
SLC_BLOCK = 64
SLC_TOPK = 16
WINDOW = 512
Q_BLOCK = 128
T5_BUCKETS = 32
T5_MAX_DIST = 1024
ROPE_BASE = 10000.0
EPS = 1e-6
SCALE = HEAD_DIM ** -0.5
IN_WIDTHS = (256, 256, 256, 256, 256, 256, 256, 4, 4, 256, 768, 4, 4, 256, 256, 384, 12)
LANES = 128
SUBLANES = 8
VMEM_LIMIT = 56 * 1024 * 1024
NEG_INF = float("-inf")

SM_MI, SM_MF, SM_GA, SM_GB, SM_NG = 0, 4, 8, 12, 16


def _cparams(*sem):
    return pltpu.CompilerParams(dimension_semantics=sem, vmem_limit_bytes=VMEM_LIMIT)


def _rms(x, g):
    return x * lax.rsqrt(jnp.mean(x * x, axis=-1, keepdims=True) + EPS) * g


def _dot(a, b, **kw):
    return jnp.dot(a, b, preferred_element_type=F32, **kw)


def _dot_nt(a, b, **kw):
    return lax.dot_general(a, b, (((1,), (1,)), ((), ())), preferred_element_type=F32, **kw)


def _dot_tn(a, b, **kw):
    return lax.dot_general(a, b, (((0,), (0,)), ((), ())), preferred_element_type=F32, **kw)


def _bf(x):
    return x.astype(BF16)


def _sigmoid(x):
    return 1.0 / (1.0 + jnp.exp(-x))


def _silu(x):
    return x * _sigmoid(x)


def _softplus(x):
    return jnp.maximum(x, 0.0) + jnp.log(1.0 + jnp.exp(-jnp.abs(x)))


def _head_rms(o):
    return o * lax.rsqrt(jnp.mean(o * o, axis=-1, keepdims=True) + EPS)


PROJ_WIDTHS = (1024, 1024, 1024, 256, 384, 128)


def _prep_w_in(w):
    wt = w.T
    parts, off = [], 0
    for wd in IN_WIDTHS:
        parts.append(wt[off:off + wd])
        off += wd
    (rq, rk, rv, rg, mq, mk, mv, mi, mf, mo, gqkv, ga, gbeta, gg, nq, nkv, ngate) = parts
    small = jnp.concatenate([mi, mf, ga, gbeta, ngate], axis=0)
    small = jnp.pad(small, ((0, LANES - small.shape[0]), (0, 0)))
    cat = jnp.concatenate([rq, rk, rv, rg, mq, mk, mv, mo, gqkv, gg, nq, nkv, small], axis=0)
    return cat.astype(BF16)


NKV_OFF = sum(PROJ_WIDTHS[:4])


def _in_proj_body(x_ref, g_ref, wt_ref, *out_refs, seq):
    h = _bf(_rms(x_ref[...], g_ref[...]))
    off = 0
    for ref in out_refs[:len(PROJ_WIDTHS)]:
        n = ref.shape[-1]
        ref[...] = _dot_nt(h, wt_ref[off:off + n, :])
        off += n
    if seq is not None:
        kvt_ref, kx_ref, vs_ref, kvw_ref = out_refs[len(PROJ_WIDTHS):]
        kvt_ref[0] = _dot_nt(wt_ref[NKV_OFF:NKV_OFF + PROJ_WIDTHS[4], :], h)
        nkv = out_refs[4][...]
        tm = nkv.shape[0]
        nblk = kx_ref.shape[-1] - HEAD_DIM
        pos = (pl.program_id(0) * tm) % seq + lax.broadcasted_iota(jnp.int32, (tm, nblk), 0)
        block_id = jnp.where(pos // SLC_BLOCK == lax.broadcasted_iota(jnp.int32, (tm, nblk), 1), 1.0, 0.0)
        kx_ref[...] = _bf(jnp.concatenate([nkv[:, 2 * HEAD_DIM:3 * HEAD_DIM], block_id], axis=1))
        vs_ref[...] = _bf(nkv[:, 3 * HEAD_DIM:4 * HEAD_DIM])
        kvw_ref[...] = _bf(nkv[:, 4 * HEAD_DIM:6 * HEAD_DIM])


def _in_proj(x2d, g, w_cat_t, tm, seq=None):
    m = x2d.shape[0]
    ntot = sum(PROJ_WIDTHS)
    const = lambda shape: pl.BlockSpec(shape, lambda i: (0,) * len(shape), pipeline_mode=pl.Buffered(1))
    out_specs = [pl.BlockSpec((tm, n), lambda i: (i, 0)) for n in PROJ_WIDTHS]
    out_shape = [jax.ShapeDtypeStruct((m, n), F32) for n in PROJ_WIDTHS]
    if seq is not None:
        per_seq = seq // tm
        nkv = PROJ_WIDTHS[4]
        out_specs.append(pl.BlockSpec((1, nkv, tm), lambda i: (i // per_seq, 0, i % per_seq)))
        out_shape.append(jax.ShapeDtypeStruct((m // seq, nkv, seq), F32))
        for n in (HEAD_DIM + seq // SLC_BLOCK, HEAD_DIM, 2 * HEAD_DIM):
            out_specs.append(pl.BlockSpec((tm, n), lambda i: (i, 0)))
            out_shape.append(jax.ShapeDtypeStruct((m, n), BF16))
    return pl.pallas_call(
        functools.partial(_in_proj_body, seq=seq),
        grid=(m // tm,),
        in_specs=[pl.BlockSpec((tm, D_MODEL), lambda i: (i, 0)), const((1, D_MODEL)), const((ntot, D_MODEL))],
        out_specs=out_specs,
        out_shape=out_shape,
        compiler_params=_cparams("parallel"),
        name="in_proj",
    )(x2d, g.reshape(1, D_MODEL), w_cat_t)


def _out_proj_body(x_ref, y0, y1, y2, y3, w_ref, gpost_ref, gpre_ref, x1_ref, h2_ref):
    acc = None
    for k, y in enumerate((y0, y1, y2, y3)):
        t = _dot(_bf(y[...]), w_ref[k * D_HEADS:(k + 1) * D_HEADS, :])
        acc = t if acc is None else acc + t
    x1 = x_ref[...] + _rms(acc, gpost_ref[...])
    x1_ref[...] = x1
    h2_ref[...] = _bf(_rms(x1, gpre_ref[...]))


def _out_proj(x2d, ys, w_out_bf, gpost, gpre, tm):
    m = x2d.shape[0]
    row = lambda n: pl.BlockSpec((tm, n), lambda i: (i, 0))
    vec = pl.BlockSpec((1, D_MODEL), lambda i: (0, 0))
    return pl.pallas_call(
        _out_proj_body,
        grid=(m // tm,),
        in_specs=[row(D_MODEL)] + [row(D_HEADS)] * 4
        + [pl.BlockSpec((D_MODEL, D_MODEL), lambda i: (0, 0)), vec, vec],
        out_specs=[row(D_MODEL), row(D_MODEL)],
        out_shape=[jax.ShapeDtypeStruct((m, D_MODEL), F32), jax.ShapeDtypeStruct((m, D_MODEL), BF16)],
        compiler_params=_cparams("parallel"),
        name="out_proj",
    )(x2d, *ys, w_out_bf, gpost.reshape(1, -1), gpre.reshape(1, -1))


FFN_TN = 256


def _gelu_tanh(x):
    c = 0.7978845608028654
    hx = 0.5 * x
    return hx + hx * jnp.tanh(x * (c + (0.044715 * c) * (x * x)))


def _ffn_seq_body(x_ref, y0, y1, y2, y3, wo_ref, gmix_ref, gpre_ref, wg_ref, wu_ref, wd_ref, cw_ref, gpost_ref,
                  x2_ref, tail_ref, carry_ref, act_ref, *, tm):
    @pl.when(pl.program_id(1) == 0)
    def _():
        carry_ref[...] = jnp.zeros_like(carry_ref)

    mix = None
    for k, y in enumerate((y0, y1, y2, y3)):
        t = _dot(_bf(y[0]), wo_ref[k * D_HEADS:(k + 1) * D_HEADS, :])
        mix = t if mix is None else mix + t
    x1 = x_ref[0] + _rms(mix, gmix_ref[...])
    h2 = _bf(_rms(x1, gpre_ref[...]))
    row8 = lax.broadcasted_iota(jnp.int32, (SUBLANES, FFN_TN), 0)
    starts = list(range(0, D_FF, FFN_TN))
    up = lambda n0: (_dot(h2, wg_ref[:, n0:n0 + FFN_TN]), _dot(h2, wu_ref[:, n0:n0 + FFN_TN]))
    nxt = up(starts[0])
    for i, n0 in enumerate(starts):
        g, u = nxt
        if i + 1 < len(starts):
            nxt = up(starts[i + 1])
        prev = carry_ref[:, n0:n0 + FFN_TN]
        g1, g2 = pltpu.roll(g, 1, 0), pltpu.roll(g, 2, 0)
        top1 = jnp.where(row8 == 0, prev[7:8, :], g1[0:SUBLANES])
        top2 = jnp.where(row8 == 0, prev[6:7, :], jnp.where(row8 == 1, prev[7:8, :], g2[0:SUBLANES]))
        g1 = jnp.concatenate([top1, g1[SUBLANES:]], axis=0)
        g2 = jnp.concatenate([top2, g2[SUBLANES:]], axis=0)
        cw = cw_ref[:, n0:n0 + FFN_TN]
        gate = cw[0:1, :] * g2 + cw[1:2, :] * g1 + cw[2:3, :] * g
        a = _gelu_tanh(gate) * u
        act_ref[:, n0:n0 + FFN_TN] = _bf(a)
        carry_ref[:, n0:n0 + FFN_TN] = g[tm - SUBLANES:tm, :]
    acc = _dot(act_ref[...], wd_ref[...])
    x2_ref[0] = x1 + _rms(acc, gpost_ref[...])
    tail_ref[0] = carry_ref[...]


def _ffn_seq(x, ys, w_out_bf, gmix, gpre, wg, wu, wd, cw, gpost, tm):
    b, l, _ = x.shape
    const = lambda shape: pl.BlockSpec(shape, lambda i, j: (0,) * len(shape), pipeline_mode=pl.Buffered(1))
    row = lambda n: pl.BlockSpec((1, tm, n), lambda i, j: (i, j, 0))
    return pl.pallas_call(
        functools.partial(_ffn_seq_body, tm=tm),
        grid=(b, l // tm),
        in_specs=[row(D_MODEL)] + [row(D_HEADS)] * 4
        + [const((D_MODEL, D_MODEL)), const((1, D_MODEL)), const((1, D_MODEL)),
           const((D_MODEL, D_FF)), const((D_MODEL, D_FF)), const((D_FF, D_MODEL)),
           const((FFN_CONV, D_FF)), const((1, D_MODEL))],
        out_specs=[pl.BlockSpec((1, tm, D_MODEL), lambda i, j: (i, j, 0)),
                   pl.BlockSpec((1, SUBLANES, D_FF), lambda i, j: (i, 0, 0))],
        out_shape=[jax.ShapeDtypeStruct((b, l, D_MODEL), F32),
                   jax.ShapeDtypeStruct((b, SUBLANES, D_FF), F32)],
        scratch_shapes=[pltpu.VMEM((SUBLANES, D_FF), F32), pltpu.VMEM((tm, D_FF), BF16)],
        compiler_params=_cparams("parallel", "arbitrary"),
        name="ffn_seq",
    )(x, *ys, w_out_bf, gmix.reshape(1, -1), gpre.reshape(1, -1), wg, wu, wd, cw, gpost.reshape(1, -1))


def _ffn_step_body(x1_ref, h2_ref, b0_ref, b1_ref, wg_ref, wu_ref, wd_ref, cw_ref, gpost_ref, x2_ref, g_ref):
    h2 = h2_ref[...]
    acc = jnp.zeros(x1_ref.shape, F32)
    for n0 in range(0, D_FF, FFN_TN):
        sl = slice(n0, n0 + FFN_TN)
        g = _dot(h2, wg_ref[:, sl])
        cw = cw_ref[:, sl]
        gate = cw[0:1, :] * b0_ref[:, sl] + cw[1:2, :] * b1_ref[:, sl] + cw[2:3, :] * g
        u = _dot(h2, wu_ref[:, sl])
        acc = acc + _dot(_bf(_gelu_tanh(gate) * u), wd_ref[sl, :])
        g_ref[:, sl] = g
    x2_ref[...] = x1_ref[...] + _rms(acc, gpost_ref[...])


def _ffn_step(x1, h2, b0, b1, wg, wu, wd, cw, gpost):
    m = x1.shape[0]
    return pl.pallas_call(
        _ffn_step_body,
        out_shape=[jax.ShapeDtypeStruct((m, D_MODEL), F32), jax.ShapeDtypeStruct((m, D_FF), F32)],
        compiler_params=pltpu.CompilerParams(vmem_limit_bytes=VMEM_LIMIT),
        name="ffn_step",
    )(x1, h2, b0, b1, wg, wu, wd, cw, gpost.reshape(1, -1))


def _rope_tables(pos):
    half = HEAD_DIM // 2
    inv = ROPE_BASE ** (-jnp.linspace(0.0, 1.0, half, dtype=F32))
    ang = pos.astype(F32)[:, None] * inv[None, :]
    return jnp.tile(jnp.cos(ang), (1, LANES // half)), jnp.tile(jnp.sin(ang), (1, LANES // half))


def _rope128(x, cos, sin):
    lane = lax.broadcasted_iota(jnp.int32, x.shape, 1)
    first = (lane % HEAD_DIM) < (HEAD_DIM // 2)
    other = jnp.where(first, -pltpu.roll(x, LANES - HEAD_DIM // 2, 1), pltpu.roll(x, HEAD_DIM // 2, 1))
    return x * cos + other * sin


def _rope256(x, cos, sin):
    return jnp.concatenate([_rope128(x[:, :LANES], cos, sin), _rope128(x[:, LANES:], cos, sin)], axis=1)


def _ret_log_decay(h):
    return math.log(1.0 - 2.0 ** (-5.0 - h))


def _head(x, h):
    return x[:, h * HEAD_DIM:(h + 1) * HEAD_DIM]


def _ret_body(z_ref, cos_ref, sin_ref, nw_ref, y_ref, s_ref, *, chunk):
    @pl.when(pl.program_id(1) == 0)
    def _():
        s_ref[...] = jnp.zeros_like(s_ref)

    nb = z_ref.shape[0]
    cos, sin = cos_ref[...], sin_ref[...]
    t = lax.broadcasted_iota(jnp.int32, (chunk, chunk), 0)
    s = lax.broadcasted_iota(jnp.int32, (chunk, chunk), 1)
    causal = t >= s
    diff = jnp.where(causal, t - s, 0).astype(F32)
    tcol = lax.broadcasted_iota(jnp.int32, (chunk, 1), 0).astype(F32)
    lgs = [_ret_log_decay(h) for h in range(N_HEADS)]
    dmat = [jnp.where(causal, jnp.exp(lg * diff), 0.0) for lg in lgs]
    xi = [jnp.exp(lg * (tcol + 1.0)) for lg in lgs]
    zeta = [jnp.exp(lg * (chunk - 1.0 - tcol)) for lg in lgs]
    probs = [(bb, h) for bb in range(nb) for h in range(N_HEADS)]
    qkv = []
    for bb in range(nb):
        q = _rope256(z_ref[bb, :, 0:256], cos, sin)
        k = _rope256(z_ref[bb, :, 256:512], cos, sin) * SCALE
        qkv.append((q, k, z_ref[bb, :, 512:768]))
    qh = [_head(qkv[bb][0], h) for bb, h in probs]
    kh = [_head(qkv[bb][1], h) for bb, h in probs]
    vh = [_bf(_head(qkv[bb][2], h)) for bb, h in probs]
    st = [s_ref[bb, h] for bb, h in probs]
    sc = [_dot_nt(_bf(q), _bf(k)) for q, k in zip(qh, kh)]
    cross = [_dot(_bf(q * xi[h]), _bf(s0)) for q, s0, (_, h) in zip(qh, st, probs)]
    upd = [_dot_tn(_bf(k * zeta[h]), v) for k, v, (_, h) in zip(kh, vh, probs)]
    o = [_dot(_bf(x * dmat[h]), v) + c for x, v, c, (_, h) in zip(sc, vh, cross, probs)]
    for i, (bb, h) in enumerate(probs):
        s_ref[bb, h] = st[i] * math.exp(lgs[h] * chunk) + upd[i]
    for bb in range(nb):
        outs = [_head_rms(o[bb * N_HEADS + h]) for h in range(N_HEADS)]
        y_ref[bb] = jnp.concatenate(outs, axis=1) * nw_ref[...] * _silu(z_ref[bb, :, 768:1024])


def _retention_seq(z_ret, cos, sin, nw, chunk, nb):
    b, l, _ = z_ret.shape
    return pl.pallas_call(
        functools.partial(_ret_body, chunk=chunk),
        grid=(b // nb, l // chunk),
        in_specs=[pl.BlockSpec((nb, chunk, 1024), lambda i, j: (i, j, 0)),
                  pl.BlockSpec((chunk, LANES), lambda i, j: (j, 0)),
                  pl.BlockSpec((chunk, LANES), lambda i, j: (j, 0)),
                  pl.BlockSpec((1, D_HEADS), lambda i, j: (0, 0))],
        out_specs=[pl.BlockSpec((nb, chunk, D_HEADS), lambda i, j: (i, j, 0)),
                   pl.BlockSpec((nb, N_HEADS, HEAD_DIM, HEAD_DIM), lambda i, j: (i, 0, 0, 0))],
        out_shape=[jax.ShapeDtypeStruct((b, l, D_HEADS), F32),
                   jax.ShapeDtypeStruct((b, N_HEADS, HEAD_DIM, HEAD_DIM), F32)],
        compiler_params=_cparams("parallel", "arbitrary"),
        name="retention_seq",
    )(z_ret, cos, sin, nw.reshape(1, -1))


def _lanes_to_rows(x, n=2 * SUBLANES):
    sel = (lax.broadcasted_iota(jnp.int32, (n, x.shape[1]), 0)
           == lax.broadcasted_iota(jnp.int32, (n, x.shape[1]), 1)).astype(F32)
    return _dot_nt(sel, x, precision=HI)


def _tri_incl(n):
    t = lax.broadcasted_iota(jnp.int32, (n, n), 0)
    s = lax.broadcasted_iota(jnp.int32, (n, n), 1)
    return t >= s


def _ml_body(z_ref, sm_ref, bias_ref, nw_ref, y_ref, c_ref, n_ref, m_ref, *, chunk):
    @pl.when(pl.program_id(1) == 0)
    def _():
        c_ref[...] = jnp.zeros_like(c_ref)
        n_ref[...] = jnp.zeros_like(n_ref)
        m_ref[...] = jnp.zeros_like(m_ref)

    nb = z_ref.shape[0]
    incl = _tri_incl(chunk)
    tri = incl.astype(F32)
    probs = [(bb, h) for bb in range(nb) for h in range(N_HEADS)]
    gate = []
    for bb in range(nb):
        pre = sm_ref[bb] + bias_ref[...]
        logf = jnp.minimum(pre, 0.0) - jnp.log(1.0 + jnp.exp(-jnp.abs(pre)))
        bcum = _dot(tri, logf, precision=HI)
        gate.append((pre, bcum, _lanes_to_rows(pre), _lanes_to_rows(bcum), m_ref[bb]))
    qh = [_head(z_ref[bb, :, 0:256], h) for bb, h in probs]
    kh = [_head(z_ref[bb, :, 256:512], h) * SCALE for bb, h in probs]
    vh = [_bf(_head(z_ref[bb, :, 512:768], h)) for bb, h in probs]
    cm = [c_ref[bb, h] for bb, h in probs]
    nv = [n_ref[bb, h:h + 1, :] for bb, h in probs]
    qk_raw = [_dot_nt(_bf(q), _bf(k)) for q, k in zip(qh, kh)]
    q_c = [_dot(_bf(q), _bf(c)) for q, c in zip(qh, cm)]
    i_col, b_col, inter, dlog = [], [], [], []
    for bb, h in probs:
        pre, bcum, pre_t, bcum_t, m_all = gate[bb]
        i_col.append(pre[:, SM_MI + h:SM_MI + h + 1])
        b_col.append(bcum[:, SM_MF + h:SM_MF + h + 1])
        inter.append(b_col[-1] + m_all[:, h:h + 1])
        dlog.append(jnp.where(incl, b_col[-1] - bcum_t[SM_MF + h:SM_MF + h + 1, :]
                              + pre_t[SM_MI + h:SM_MI + h + 1, :], NEG_INF))
    dmax = [jnp.max(x, axis=1, keepdims=True) for x in dlog]
    q_n = [jnp.sum(q * n, axis=1, keepdims=True) for q, n in zip(qh, nv)]
    stab = []
    for i in range(len(probs)):
        m_t = jnp.maximum(inter[i], dmax[i])
        m_end = m_t[chunk - 1:chunk, :]
        stab.append(dict(m_t=m_t, w=jnp.exp(dlog[i] - m_t), g_in=jnp.exp(inter[i] - m_t), m_end=m_end,
                         w_end=jnp.exp(b_col[i][chunk - 1:chunk, :] - b_col[i] + i_col[i] - m_end),
                         g_end=jnp.exp(inter[i][chunk - 1:chunk, :] - m_end)))
    qk = [x * d["w"] for x, d in zip(qk_raw, stab)]
    num = [_dot(_bf(x), v) + d["g_in"] * c for x, v, d, c in zip(qk, vh, stab, q_c)]
    upd = [_dot_tn(_bf(k * d["w_end"]), v) for k, d, v in zip(kh, stab, vh)]
    qk_sum = [jnp.sum(x, axis=1, keepdims=True) for x in qk]
    k_sum = [jnp.sum(d["w_end"] * k, axis=0, keepdims=True) for d, k in zip(stab, kh)]
    hh = []
    for i, (bb, h) in enumerate(probs):
        d = stab[i]
        den = qk_sum[i] + d["g_in"] * q_n[i]
        hh.append(num[i] / jnp.maximum(jnp.abs(den), jnp.exp(-d["m_t"])))
        c_ref[bb, h] = d["g_end"] * cm[i] + upd[i]
        n_ref[bb, h:h + 1, :] = d["g_end"] * nv[i] + k_sum[i]
    lane = lax.broadcasted_iota(jnp.int32, (1, LANES), 1)
    for bb in range(nb):
        m_out = jnp.zeros((1, LANES), F32)
        outs = []
        for h in range(N_HEADS):
            i = bb * N_HEADS + h
            m_out = jnp.where(lane == h, stab[i]["m_end"], m_out)
            outs.append(_head_rms(_sigmoid(_head(z_ref[bb, :, 768:1024], h)) * hh[i]))
        m_ref[bb] = m_out
        y_ref[bb] = jnp.concatenate(outs, axis=1) * nw_ref[...]


def _mlstm_seq(z_ml, small, bias_vec, nw, chunk, nb):
    b, l, _ = z_ml.shape
    return pl.pallas_call(
        functools.partial(_ml_body, chunk=chunk),
        grid=(b // nb, l // chunk),
        in_specs=[pl.BlockSpec((nb, chunk, 1024), lambda i, j: (i, j, 0)),
                  pl.BlockSpec((nb, chunk, LANES), lambda i, j: (i, j, 0)),
                  pl.BlockSpec((1, LANES), lambda i, j: (0, 0)),
                  pl.BlockSpec((1, D_HEADS), lambda i, j: (0, 0))],
        out_specs=[pl.BlockSpec((nb, chunk, D_HEADS), lambda i, j: (i, j, 0)),
                   pl.BlockSpec((nb, N_HEADS, HEAD_DIM, HEAD_DIM), lambda i, j: (i, 0, 0, 0)),
                   pl.BlockSpec((nb, N_HEADS, HEAD_DIM), lambda i, j: (i, 0, 0)),
                   pl.BlockSpec((nb, 1, LANES), lambda i, j: (i, 0, 0))],
        out_shape=[jax.ShapeDtypeStruct((b, l, D_HEADS), F32),
                   jax.ShapeDtypeStruct((b, N_HEADS, HEAD_DIM, HEAD_DIM), F32),
                   jax.ShapeDtypeStruct((b, N_HEADS, HEAD_DIM), F32),
                   jax.ShapeDtypeStruct((b, 1, LANES), F32)],
        compiler_params=_cparams("parallel", "arbitrary"),
        name="mlstm_seq",
    )(z_ml, small, bias_vec, nw.reshape(1, -1))


def _split_bf16(x):
    hi = _bf(x)
    return hi, _bf(x - hi.astype(F32))


def _dot3(a, b):
    (ah, al), (bh, bl) = a, b
    return _dot(ah, bh) + (_dot(ah, bl) + _dot(al, bh))


def _l2norm(x):
    return x * lax.rsqrt(jnp.sum(x * x, axis=-1, keepdims=True) + EPS)


def _gdn_body(z_ref, sm_ref, cw_ref, dtb_ref, alog_ref, nw_ref, y_ref, s_ref, buf_ref, *, chunk):
    @pl.when(pl.program_id(1) == 0)
    def _():
        s_ref[...] = jnp.zeros_like(s_ref)
        buf_ref[:, 0:SUBLANES, :] = jnp.zeros((buf_ref.shape[0], SUBLANES, 3 * D_HEADS), F32)

    nb = z_ref.shape[0]
    rows = z_ref.shape[1]
    nc = rows // chunk
    incl = _tri_incl(chunk)
    strict = lax.broadcasted_iota(jnp.int32, (chunk, chunk), 0) > lax.broadcasted_iota(jnp.int32, (chunk, chunk), 1)
    eye = (lax.broadcasted_iota(jnp.int32, (chunk, chunk), 0)
           == lax.broadcasted_iota(jnp.int32, (chunk, chunk), 1)).astype(F32)
    tri = incl.astype(F32)
    gates, acts = {}, []
    for bb in range(nb):
        buf_ref[bb, SUBLANES:SUBLANES + rows, :] = z_ref[bb, :, 0:768]
        conv = None
        for j in range(GDN_CONV):
            term = buf_ref[bb, pl.ds(SUBLANES - (GDN_CONV - 1) + j, rows), :] * cw_ref[j:j + 1, :]
            conv = term if conv is None else conv + term
        buf_ref[bb, 0:SUBLANES, :] = buf_ref[bb, rows:rows + SUBLANES, :]
        acts.append(_silu(conv))
        sm = sm_ref[bb]
        g_all = -jnp.exp(alog_ref[...]) * _softplus(sm + dtb_ref[...])
        beta_all = _sigmoid(sm)
        for cc in range(nc):
            rs = slice(cc * chunk, (cc + 1) * chunk)
            gcum = _dot(tri, g_all[rs], precision=HI)
            gates[bb, cc] = (gcum, _lanes_to_rows(gcum), beta_all[rs])

    probs = [(bb, cc, h) for cc in range(nc) for bb in range(nb) for h in range(N_HEADS)]
    pre = []
    for bb, cc, h in probs:
        rs = slice(cc * chunk, (cc + 1) * chunk)
        gcum, gcum_t, beta_all = gates[bb, cc]
        g_col = gcum[:, SM_GA + h:SM_GA + h + 1]
        g_row = gcum_t[SM_GA + h:SM_GA + h + 1, :]
        beta = beta_all[:, SM_GB + h:SM_GB + h + 1]
        decay = jnp.where(incl, jnp.exp(jnp.where(incl, g_col - g_row, 0.0)), 0.0)
        act = acts[bb]
        qh = _l2norm(_head(act[rs, 0:256], h)) * SCALE
        kh = _l2norm(_head(act[rs, 256:512], h))
        vh = _head(act[rs, 512:768], h)
        eg = jnp.exp(g_col)
        pre.append(dict(g_col=g_col, beta=beta, decay=decay, qh=qh, kh=kh, kb=_bf(kh), eg=eg,
                        rhs=jnp.concatenate([vh * beta, kh * (beta * eg)], axis=1)))
    a_l = [jnp.where(strict, d["beta"] * _dot_nt(d["kb"], d["kb"]) * d["decay"], 0.0) for d in pre]
    p_l = [eye - a for a in a_l]
    pw_s = [_split_bf16(a) for a in a_l]
    lvl = 2
    while lvl < chunk:
        pw_s = [_split_bf16(_dot3(x, x)) for x in pw_s]
        p_l = [p + _dot3(_split_bf16(p), x) for p, x in zip(p_l, pw_s)]
        lvl *= 2
    uw_l = [_dot3(_split_bf16(p), _split_bf16(d["rhs"])) for p, d in zip(p_l, pre)]
    qk_l = [_dot_nt(_bf(d["qh"]), d["kb"]) * d["decay"] for d in pre]
    per_chunk = nb * N_HEADS
    st_l = [s_ref[bb, h] for bb, cc, h in probs[:per_chunk]]
    o_all = []
    for cc in range(nc):
        sl = slice(cc * per_chunk, (cc + 1) * per_chunk)
        stb_l = [_bf(st) for st in st_l]
        delta_l = [uw[:, 0:HEAD_DIM] - _dot(_bf(uw[:, HEAD_DIM:]), stb) for uw, stb in zip(uw_l[sl], stb_l)]
        o_all += [_dot(_bf(qk), _bf(delta)) + _dot(_bf(d["qh"] * d["eg"]), stb)
                  for qk, delta, d, stb in zip(qk_l[sl], delta_l, pre[sl], stb_l)]
        new_st = []
        for d, st, delta in zip(pre[sl], st_l, delta_l):
            g_end = d["g_col"][chunk - 1:chunk, :]
            w_end = jnp.exp(g_end - d["g_col"])
            new_st.append(jnp.exp(g_end) * st + _dot_tn(_bf(d["kh"] * w_end), _bf(delta)))
        st_l = new_st
    for (bb, cc, h), st in zip(probs[:per_chunk], st_l):
        s_ref[bb, h] = st
    for bb in range(nb):
        for cc in range(nc):
            outs = [_head_rms(o_all[cc * per_chunk + bb * N_HEADS + h]) for h in range(N_HEADS)]
            rs = slice(cc * chunk, (cc + 1) * chunk)
            y_ref[bb, rs, :] = jnp.concatenate(outs, axis=1) * nw_ref[...] * _silu(z_ref[bb, rs, 768:1024])


def _gdn_seq(z_gdn, small, conv_w, dtb_vec, alog_vec, nw, chunk, nb, nc):
    b, l, _ = z_gdn.shape
    rows = nc * chunk
    return pl.pallas_call(
        functools.partial(_gdn_body, chunk=chunk),
        grid=(b // nb, l // rows),
        in_specs=[pl.BlockSpec((nb, rows, 1024), lambda i, j: (i, j, 0)),
                  pl.BlockSpec((nb, rows, LANES), lambda i, j: (i, j, 0)),
                  pl.BlockSpec((GDN_CONV, 3 * D_HEADS), lambda i, j: (0, 0)),
                  pl.BlockSpec((1, LANES), lambda i, j: (0, 0)),
                  pl.BlockSpec((1, LANES), lambda i, j: (0, 0)),
                  pl.BlockSpec((1, D_HEADS), lambda i, j: (0, 0))],
        out_specs=[pl.BlockSpec((nb, rows, D_HEADS), lambda i, j: (i, j, 0)),
                   pl.BlockSpec((nb, N_HEADS, HEAD_DIM, HEAD_DIM), lambda i, j: (i, 0, 0, 0))],
        out_shape=[jax.ShapeDtypeStruct((b, l, D_HEADS), F32),
                   jax.ShapeDtypeStruct((b, N_HEADS, HEAD_DIM, HEAD_DIM), F32)],
        scratch_shapes=[pltpu.VMEM((nb, SUBLANES + rows, 3 * D_HEADS), F32)],
        compiler_params=_cparams("parallel", "arbitrary"),
        name="gdn_seq",
    )(z_gdn, small, conv_w, dtb_vec, alog_vec, nw.reshape(1, -1))


STEP_SEQS = 4


def _row_to_col(row):
    n = row.shape[1]
    eye = lax.broadcasted_iota(jnp.int32, (n, n), 0) == lax.broadcasted_iota(jnp.int32, (n, n), 1)
    return jnp.sum(jnp.where(eye, jnp.broadcast_to(row, (n, n)), 0.0), axis=1, keepdims=True)


def _step_body(zr_ref, zm_ref, zg_ref, sm_ref, cos_ref, sin_ref, mlb_ref, dtb_ref, alog_ref, cw_ref,
               nwr_ref, nwm_ref, nwg_ref, sr_ref, mc_ref, mn_ref, mm_ref, gs_ref, gc_ref,
               yr_ref, ym_ref, yg_ref, sr_o, mc_o, mn_o, mm_o, gs_o, gc_o):
    nb = zr_ref.shape[0]
    d = HEAD_DIM
    eye = lax.broadcasted_iota(jnp.int32, (d, d), 0) == lax.broadcasted_iota(jnp.int32, (d, d), 1)
    lane = lax.broadcasted_iota(jnp.int32, (1, LANES), 1)
    cos, sin = cos_ref[...], sin_ref[...]
    rowsum = lambda x: jnp.sum(x, axis=1, keepdims=True)

    def vec_mat(row, mat):
        return _dot(_pad_rows(row), mat, precision=HI)[0:1, :]

    def outer(a_row, b_row):
        diag = jnp.where(eye, jnp.broadcast_to(a_row, (d, d)), 0.0)
        return _dot(diag, jnp.broadcast_to(b_row, (d, d)), precision=HI)

    ret, ml, gdn, seqs = [], [], [], []
    for bb in range(nb):
        sm, zr, zm, zg = sm_ref[bb], zr_ref[bb], zm_ref[bb], zg_ref[bb]
        rq = _rope256(zr[:, 0:256], cos, sin)
        rk = _rope256(zr[:, 256:512], cos, sin) * SCALE
        pre = sm + mlb_ref[...]
        logf = jnp.minimum(pre, 0.0) - jnp.log(1.0 + jnp.exp(-jnp.abs(pre)))
        m_all = mm_ref[bb]
        x = zg[:, 0:768]
        buf = gc_ref[bb]
        conv = x * cw_ref[GDN_CONV - 1:GDN_CONV, :]
        for j in range(GDN_CONV - 1):
            conv = conv + buf[j:j + 1, :] * cw_ref[j:j + 1, :]
        act = _silu(conv)
        g_all = -jnp.exp(alog_ref[...]) * _softplus(sm + dtb_ref[...])
        beta_all = _sigmoid(sm)
        seqs.append(dict(zr=zr, zm=zm, zg=zg, x=x, buf=buf))
        for h in range(N_HEADS):
            ret.append(dict(q=_head(rq, h), k=_head(rk, h), v=_head(zr[:, 512:768], h), st=sr_ref[bb, h],
                            gam=math.exp(_ret_log_decay(h))))
            i_g = pre[:, SM_MI + h:SM_MI + h + 1]
            inter = logf[:, SM_MF + h:SM_MF + h + 1] + m_all[:, h:h + 1]
            m_t = jnp.maximum(inter, i_g)
            ml.append(dict(q=_head(zm[:, 0:256], h), k=_head(zm[:, 256:512], h) * SCALE, v=_head(zm[:, 512:768], h),
                           cm=mc_ref[bb, h], nv=mn_ref[bb, h:h + 1, :], m_t=m_t, w=jnp.exp(i_g - m_t),
                           g_in=jnp.exp(inter - m_t)))
            gdn.append(dict(q=_head(act[:, 0:256], h), k=_head(act[:, 256:512], h), v=_head(act[:, 512:768], h),
                            st=gs_ref[bb, h], eg=jnp.exp(g_all[:, SM_GA + h:SM_GA + h + 1]),
                            beta=beta_all[:, SM_GB + h:SM_GB + h + 1]))
    for p in gdn:
        p["q"], p["k"] = _l2norm(p["q"]) * SCALE, _l2norm(p["k"])
    for p in ret + ml + gdn:
        p["qk"] = rowsum(p["q"] * p["k"])
    for p in ml:
        p["qn"] = rowsum(p["q"] * p["nv"])
    for p in ret:
        p["qs"] = vec_mat(p["q"] * p["gam"], p["st"])
    for p in ml:
        p["qc"] = vec_mat(p["q"], p["cm"])
    for p in gdn:
        p["ks"] = vec_mat(p["k"] * (p["beta"] * p["eg"]), p["st"])
        p["qs"] = vec_mat(p["q"] * p["eg"], p["st"])
    for p in ret:
        p["o"] = p["qk"] * p["v"] + p["qs"]
    for p in ml:
        qk = p["qk"] * p["w"]
        den = qk + p["g_in"] * p["qn"]
        p["o"] = (qk * p["v"] + p["g_in"] * p["qc"]) / jnp.maximum(jnp.abs(den), jnp.exp(-p["m_t"]))
    for p in gdn:
        p["delta"] = p["v"] * p["beta"] - p["ks"]
        p["o"] = p["qk"] * p["delta"] + p["qs"]
    for p in ret:
        p["new"] = p["st"] * p["gam"] + outer(p["k"], p["v"])
    for p in ml:
        p["new"] = p["g_in"] * p["cm"] + outer(p["k"] * p["w"], p["v"])
    for p in gdn:
        p["new"] = p["eg"] * p["st"] + outer(p["k"], p["delta"])
    for bb in range(nb):
        s = seqs[bb]
        ps = slice(bb * N_HEADS, (bb + 1) * N_HEADS)
        m_out = jnp.zeros((1, LANES), F32)
        for h, (pr, pm, pg) in enumerate(zip(ret[ps], ml[ps], gdn[ps])):
            sr_o[bb, h] = pr["new"]
            mc_o[bb, h] = pm["new"]
            mn_o[bb, h:h + 1, :] = pm["g_in"] * pm["nv"] + pm["w"] * pm["k"]
            gs_o[bb, h] = pg["new"]
            m_out = jnp.where(lane == h, pm["m_t"], m_out)
        mm_o[bb] = m_out
        gc_o[bb, 0:GDN_CONV - 2, :] = s["buf"][1:GDN_CONV - 1, :]
        gc_o[bb, GDN_CONV - 2:GDN_CONV - 1, :] = s["x"]
        yr_ref[bb] = (jnp.concatenate([_head_rms(p["o"]) for p in ret[ps]], axis=1) * nwr_ref[...]
                      * _silu(s["zr"][:, 768:1024]))
        og = s["zm"][:, 768:1024]
        ym_ref[bb] = jnp.concatenate([_head_rms(_sigmoid(_head(og, h)) * p["o"])
                                      for h, p in enumerate(ml[ps])], axis=1) * nwm_ref[...]
        yg_ref[bb] = (jnp.concatenate([_head_rms(p["o"]) for p in gdn[ps]], axis=1) * nwg_ref[...]
                      * _silu(s["zg"][:, 768:1024]))


def _mixers_step(z_ret, z_ml, z_gdn, small, cos, sin, mlb, dtb, alog, conv_w, nwr, nwm, nwg,
                 s_ret, ml_c, ml_n, ml_m, gdn_s, gdn_conv):
    b = z_ret.shape[0]
    nb = STEP_SEQS if b % STEP_SEQS == 0 else 1
    row = lambda n: pl.BlockSpec((nb, 1, n), lambda i: (i, 0, 0))
    vec = lambda n: pl.BlockSpec((1, n), lambda i: (0, 0))
    mat = pl.BlockSpec((nb, N_HEADS, HEAD_DIM, HEAD_DIM), lambda i: (i, 0, 0, 0))
    nsp = pl.BlockSpec((nb, N_HEADS, HEAD_DIM), lambda i: (i, 0, 0))
    csp = pl.BlockSpec((nb, GDN_CONV - 1, 3 * D_HEADS), lambda i: (i, 0, 0))
    mshape = jax.ShapeDtypeStruct((b, N_HEADS, HEAD_DIM, HEAD_DIM), F32)
    yshape = jax.ShapeDtypeStruct((b, 1, D_HEADS), F32)
    return pl.pallas_call(
        _step_body,
        grid=(b // nb,),
        in_specs=[row(1024), row(1024), row(1024), row(LANES), vec(LANES), vec(LANES), vec(LANES), vec(LANES), vec(LANES),
                  pl.BlockSpec((GDN_CONV, 3 * D_HEADS), lambda i: (0, 0)), vec(D_HEADS), vec(D_HEADS), vec(D_HEADS),
                  mat, mat, nsp, row(LANES), mat, csp],
        out_specs=[row(D_HEADS), row(D_HEADS), row(D_HEADS), mat, mat, nsp, row(LANES), mat, csp],
        out_shape=[yshape, yshape, yshape, mshape, mshape,
                   jax.ShapeDtypeStruct((b, N_HEADS, HEAD_DIM), F32), jax.ShapeDtypeStruct((b, 1, LANES), F32),
                   mshape, jax.ShapeDtypeStruct((b, GDN_CONV - 1, 3 * D_HEADS), F32)],
        compiler_params=_cparams("parallel"),
        name="mixers_step",
    )(z_ret, z_ml, z_gdn, small, cos, sin, mlb, dtb, alog, conv_w,
      nwr.reshape(1, -1), nwm.reshape(1, -1), nwg.reshape(1, -1), s_ret, ml_c, ml_n, ml_m, gdn_s, gdn_conv)


def _t5_bucket_np(n):
    exact = T5_BUCKETS // 2
    n = np.maximum(np.asarray(n, np.int64), 0)
    x = np.maximum(n, 1).astype(np.float32) / np.float32(exact)
    large = exact + (np.log(x) / np.float32(math.log(T5_MAX_DIST / exact)) * np.float32(T5_BUCKETS - exact)).astype(np.int32)
    return np.where(n < exact, n, np.minimum(large, T5_BUCKETS - 1)).astype(np.int32)


def _bucket_thresholds():
    b = _t5_bucket_np(np.arange(4 * T5_MAX_DIST))
    return [int(np.argmax(b >= k)) for k in range(T5_BUCKETS // 2 + 1, T5_BUCKETS)]


def _bias_table_body(rbt_ref, oh_ref, o_ref):
    o_ref[...] = _dot(rbt_ref[...], oh_ref[...].astype(F32), precision=HI)


def _bias_tables(rel_bias, dists):
    n = len(dists)
    tn = 4096
    npad = -(-n // tn) * tn
    onehot = np.zeros((T5_BUCKETS, npad), np.float32)
    onehot[_t5_bucket_np(dists), np.arange(n)] = 1.0
    rbt = jnp.pad(rel_bias.T, ((0, SUBLANES - N_HEADS), (0, 0)))
    out = pl.pallas_call(
        _bias_table_body,
        grid=(npad // tn,),
        in_specs=[pl.BlockSpec((SUBLANES, T5_BUCKETS), lambda i: (0, 0)),
                  pl.BlockSpec((T5_BUCKETS, tn), lambda i: (0, i))],
        out_specs=pl.BlockSpec((SUBLANES, tn), lambda i: (0, i)),
        out_shape=jax.ShapeDtypeStruct((SUBLANES, npad), F32),
        name="bias_tables",
    )(rbt, jnp.asarray(onehot, BF16))
    return out[:, :n]


def _prep_cmp_w(cmp_w, cmp_pe):
    eye2 = jnp.eye(2, dtype=F32)

    def half(w):
        full = jnp.einsum('cjde,cf->jcdfe', w, eye2)
        return full.reshape(CMP_STRIDE * 2 * HEAD_DIM, 2 * HEAD_DIM)

    wab = jnp.concatenate([half(cmp_w[:, :CMP_STRIDE]), half(cmp_w[:, CMP_STRIDE:])], axis=1)
    pe = jnp.concatenate([cmp_pe[:CMP_STRIDE].reshape(1, -1), cmp_pe[CMP_STRIDE:].reshape(1, -1)], axis=1)
    return wab, jnp.pad(pe, ((0, SUBLANES - 1), (0, 0)))


def _pe_term(pe_ref, wab_ref):
    kin = CMP_STRIDE * 2 * HEAD_DIM
    t = (_dot(pe_ref[:, 0:kin], wab_ref[:, 0:LANES].astype(F32), precision=HI)
         + _dot(pe_ref[:, kin:2 * kin], wab_ref[:, LANES:2 * LANES].astype(F32), precision=HI))
    return t[0:1, :]


def _combine_cmp(parts, pe_term):
    rows = parts.shape[0]
    nxt = pltpu.roll(parts[:, LANES:2 * LANES], rows - 1, 0)
    r = lax.broadcasted_iota(jnp.int32, (rows, LANES), 0)
    return jnp.where(r < rows - 1, parts[:, 0:LANES] + nxt + pe_term, 0.0)


def _compress_seq_body(kv_ref, wab_ref, pe_ref, o_ref):
    n16 = o_ref.shape[1]
    parts = jnp.zeros((n16, 2 * LANES), F32)
    for j in range(CMP_STRIDE):
        xj = kv_ref[0, pl.ds(j, n16, stride=CMP_STRIDE), :]
        parts = parts + _dot(_bf(xj), _bf(wab_ref[j * LANES:(j + 1) * LANES, :]))
    o_ref[0] = _combine_cmp(parts, _pe_term(pe_ref, wab_ref))


def _compress_seq(nkv, wab, pe):
    b, l, _ = nkv.shape
    n16 = l // CMP_STRIDE
    return pl.pallas_call(
        _compress_seq_body,
        grid=(b,),
        in_specs=[pl.BlockSpec((1, l, LANES), lambda i: (i, 0, 0)),
                  pl.BlockSpec(wab.shape, lambda i: (0, 0)),
                  pl.BlockSpec(pe.shape, lambda i: (0, 0))],
        out_specs=pl.BlockSpec((1, n16, LANES), lambda i: (i, 0, 0)),
        out_shape=jax.ShapeDtypeStruct((b, n16, LANES), F32),
        compiler_params=_cparams("parallel"),
        name="nsa_compress_seq",
    )(nkv, wab, pe)


def _softmax_rows(s, valid):
    s = jnp.where(valid, s, NEG_INF)
    m = jnp.max(s, axis=1, keepdims=True)
    m = jnp.where(m == NEG_INF, 0.0, m)
    p = jnp.exp(s - m)
    den = jnp.sum(p, axis=1, keepdims=True)
    return p / jnp.where(den > 0, den, 1.0)


def _stack_heads(x):
    return jnp.concatenate([_head(x, h) for h in range(N_HEADS)], axis=0)


def _toeplitz(tab_ref, h, r0, rows, ncol):
    nrow = (rows + ncol) // LANES
    strip = tab_ref[h, pl.ds(r0, nrow), :]
    flat = jnp.concatenate([strip[r:r + 1, :] for r in range(nrow)], axis=1)
    rolled = pltpu.roll(jnp.broadcast_to(flat, (rows, nrow * LANES)), 0, 1, stride=1, stride_axis=0)
    return rolled[:, rows:]


SLC_KC = 512


MASK_BIG = 1e30


def _top_blocks_t(score, k):
    row = lax.broadcasted_iota(jnp.int32, score.shape, 0)
    n = score.shape[0]
    sel = jnp.zeros(score.shape, jnp.bool_)
    for _ in range(k):
        m = jnp.max(score, axis=0, keepdims=True)
        idx = jnp.min(jnp.where(score == m, row, n), axis=0, keepdims=True)
        pick = (row == idx) & (m > NEG_INF)
        sel = sel | pick
        score = jnp.where(pick, NEG_INF, score)
    return sel


def _bucket_index(dist):
    n = jnp.maximum(dist, 0)
    exact = T5_BUCKETS // 2
    big = jnp.full(dist.shape, exact, jnp.int32)
    for thr in _bucket_thresholds():
        big = big + (n >= thr).astype(jnp.int32)
    return jnp.where(n < exact, n, big)


def _gather_bias(rbt_ref, h, bucket):
    rows, width = bucket.shape
    piece = min(width, LANES)
    tab = jnp.broadcast_to(rbt_ref[h:h + 1, 0:piece], (rows, piece))
    parts = [jnp.take_along_axis(tab, bucket[:, s:s + piece], axis=1) for s in range(0, width, piece)]
    return parts[0] if len(parts) == 1 else jnp.concatenate(parts, axis=1)


def _tile_rows(x, n=N_HEADS):
    return jnp.concatenate([x] * n, axis=0)


def _near_chunks():
    return (_bucket_thresholds()[-1] + SLC_KC - 2) // SLC_KC + 1


def _bias_tile(tab_ref, base, ncol, p0, lo, hi):
    r0 = (p0 - base - Q_BLOCK) // LANES
    bias = jnp.concatenate([_toeplitz(tab_ref, h, r0, Q_BLOCK, ncol) for h in range(N_HEADS)], axis=0)
    dist = (base + lax.broadcasted_iota(jnp.int32, (Q_BLOCK, ncol), 0)
            - lax.broadcasted_iota(jnp.int32, (Q_BLOCK, ncol), 1))
    return bias + _tile_rows(jnp.where((dist >= lo) & (dist < hi), 0.0, -MASK_BIG))


def _nsa_seq_body(rb_ref, q_ref, sm_ref, kx_ref, vs_ref, kvw_ref, kvc_ref, tab_ref, rbt_ref, ovlt_ref, y_ref,
                  stile_ref, wtile_ref, ctile_ref, *, seq, p0):
    qb = Q_BLOCK
    rows = N_HEADS * qb
    n16 = seq // CMP_STRIDE
    nblk = seq // SLC_BLOCK
    bi = pl.program_id(1)
    q0 = bi * qb

    @pl.when((pl.program_id(0) == 0) & (bi == 0))
    def _():
        per_chunk = SLC_KC // qb

        def slc_tile(t, carry):
            base = (t // _near_chunks()) * qb + (t % _near_chunks()) * SLC_KC
            stile_ref[t] = _bias_tile(tab_ref, base, SLC_KC, p0, 0, seq + SLC_KC)
            return carry

        def win_tile(t, carry):
            wtile_ref[t] = _bias_tile(tab_ref, t * qb, WINDOW + qb, p0, 0, WINDOW)
            return carry

        def cmp_tile(t, carry):
            dist = (t * qb + lax.broadcasted_iota(jnp.int32, (qb, n16), 0)
                    - lax.broadcasted_iota(jnp.int32, (qb, n16), 1) * CMP_STRIDE - (CMP_LEN - 1))
            bucket = _bucket_index(dist)
            bias = jnp.concatenate([_gather_bias(rbt_ref, h, bucket) for h in range(N_HEADS)], axis=0)
            ctile_ref[t] = bias + _tile_rows(jnp.where(dist >= 0, 0.0, -MASK_BIG))
            return carry

        lax.fori_loop(0, per_chunk * _near_chunks(), slc_tile, 0)
        lax.fori_loop(0, WINDOW // qb + 1, win_tile, 0)
        lax.fori_loop(0, seq // qb, cmp_tile, 0)
    qs = _stack_heads(q_ref[0]) * SCALE
    qsb = _bf(qs)

    kvc = kvc_ref[0]
    s_c = _dot_nt(qsb, _bf(kvc[:, 0:HEAD_DIM]))
    band = WINDOW + qb
    sw = pl.multiple_of(jnp.maximum(q0 - WINDOW, 0), qb)
    kw = kvw_ref[0, pl.ds(sw, band), 0:HEAD_DIM]
    vw = kvw_ref[0, pl.ds(sw, band), HEAD_DIM:2 * HEAD_DIM]
    s_w = _dot_nt(qsb, kw)

    sb_c = s_c + ctile_ref[bi]
    m_c = jnp.max(sb_c, axis=1, keepdims=True)
    e_c = jnp.exp(sb_c - m_c)
    den_c = jnp.sum(e_c, axis=1, keepdims=True)
    p_cb = _bf(e_c * jnp.where(m_c > -MASK_BIG / 10, 1.0 / den_c, 0.0))
    o_c = _dot(p_cb, _bf(kvc[:, HEAD_DIM:2 * HEAD_DIM]))
    imp_h = _dot_nt(_bf(ovlt_ref[...]), p_cb)
    imp_t = imp_h[:, 0:qb] + imp_h[:, qb:2 * qb] + imp_h[:, 2 * qb:3 * qb] + imp_h[:, 3 * qb:4 * qb]

    base_w = q0 - sw
    sb_w = s_w + wtile_ref[base_w // qb]
    e_w = jnp.exp(sb_w - jnp.max(sb_w, axis=1, keepdims=True))
    o_w = _dot(_bf(e_w * (1.0 / jnp.sum(e_w, axis=1, keepdims=True))), vw)

    qlane = q0 + lax.broadcasted_iota(jnp.int32, (1, qb), 1)
    cur = qlane // SLC_BLOCK
    rowb = lax.broadcasted_iota(jnp.int32, (nblk, qb), 0)
    forced = (rowb == 0) | (rowb == cur) | (rowb == cur - 1)
    score = jnp.where(forced | (rowb > cur), NEG_INF, imp_t)
    sel = jnp.where(forced | _top_blocks_t(score, SLC_TOPK - 3), 1.0, 0.0).T
    negm = _bf(_tile_rows((sel - 1.0) * MASK_BIG))
    q_ext = jnp.concatenate([qsb, negm], axis=1)

    thr_far = _bucket_thresholds()[-1] + SLC_KC - 1
    n_chunks = q0 // SLC_KC + 1
    n_far = jnp.maximum(q0 - thr_far + SLC_KC, 0) // SLC_KC
    n_near = _near_chunks()

    far_bias = jnp.concatenate([jnp.full((qb, 1), rb_ref[T5_BUCKETS - 1, h], F32) for h in range(N_HEADS)], axis=0)

    def scores(c):
        k0 = pl.multiple_of(c * SLC_KC, SLC_KC)
        kx = kx_ref[0, pl.ds(k0, SLC_KC), :]
        return q0 - k0, _dot_nt(q_ext, kx), vs_ref[0, pl.ds(k0, SLC_KC), :]

    def online(carry, s, vs):
        m, l, acc = carry
        m_new = jnp.maximum(m, jnp.max(s, axis=1, keepdims=True))
        alpha = jnp.exp(m - m_new)
        p = jnp.exp(s - m_new)
        return m_new, alpha * l + jnp.sum(p, axis=1, keepdims=True), alpha * acc + _dot(_bf(p), vs)

    def far_step(c, carry):
        _, s, vs = scores(c)
        return online(carry, s + far_bias, vs)

    def near_step(c, carry):
        base, s, vs = scores(c)
        tile = ((q0 % SLC_KC) // qb) * n_near + base // SLC_KC
        return online(carry, s + stile_ref[tile], vs)

    init = (jnp.full((rows, 1), -MASK_BIG / 10, F32), jnp.zeros((rows, 1), F32), jnp.zeros((rows, HEAD_DIM), F32))
    carry = lax.fori_loop(0, n_far, far_step, init)
    _, l_s, acc_s = lax.fori_loop(n_far, n_chunks, near_step, carry)
    o_s = acc_s / jnp.where(l_s > 0, l_s, 1.0)

    gates = _sigmoid(sm_ref[0])
    outs = []
    for h in range(N_HEADS):
        rs = slice(h * qb, (h + 1) * qb)
        g = lambda t: gates[:, SM_NG + 3 * h + t:SM_NG + 3 * h + t + 1]
        outs.append(g(0) * o_c[rs] + g(1) * o_s[rs] + g(2) * o_w[rs])
    y_ref[0] = jnp.concatenate(outs, axis=1)


def _nsa_seq(rel_bias, nq, small, kx, vs, kvw, kvc, tab, seq_p0):
    b, l, _ = nq.shape
    n16 = l // CMP_STRIDE
    nblk = l // SLC_BLOCK
    cmp_start = np.arange(n16)[None, :] * CMP_STRIDE
    slc_start = np.arange(nblk)[:, None] * SLC_BLOCK
    ovlt = ((cmp_start < slc_start + SLC_BLOCK) & (cmp_start + CMP_LEN > slc_start)
            & (np.arange(n16)[None, :] < n16 - 1)).astype(np.float32)
    rbt = jnp.pad(rel_bias.T, ((0, SUBLANES - N_HEADS), (0, LANES - T5_BUCKETS)))
    nrows = tab.shape[1]
    const = lambda shape: pl.BlockSpec(shape, lambda i, j: (0,) * len(shape))
    seq_spec = lambda n: pl.BlockSpec((1, l, n), lambda i, j: (i, 0, 0))
    return pl.pallas_call(
        functools.partial(_nsa_seq_body, seq=l, p0=seq_p0),
        grid=(b, l // Q_BLOCK),
        in_specs=[pl.BlockSpec(memory_space=pltpu.SMEM),
                  pl.BlockSpec((1, Q_BLOCK, D_HEADS), lambda i, j: (i, j, 0)),
                  pl.BlockSpec((1, Q_BLOCK, LANES), lambda i, j: (i, j, 0)),
                  seq_spec(HEAD_DIM + nblk), seq_spec(HEAD_DIM), seq_spec(2 * HEAD_DIM),
                  pl.BlockSpec((1, n16, LANES), lambda i, j: (i, 0, 0)),
                  const((SUBLANES, nrows, LANES)), const((SUBLANES, LANES)), const((nblk, n16))],
        out_specs=pl.BlockSpec((1, Q_BLOCK, D_HEADS), lambda i, j: (i, j, 0)),
        out_shape=jax.ShapeDtypeStruct((b, l, D_HEADS), F32),
        scratch_shapes=[pltpu.VMEM(((SLC_KC // Q_BLOCK) * _near_chunks(), N_HEADS * Q_BLOCK, SLC_KC), F32),
                        pltpu.VMEM((WINDOW // Q_BLOCK + 1, N_HEADS * Q_BLOCK, WINDOW + Q_BLOCK), F32),
                        pltpu.VMEM((l // Q_BLOCK, N_HEADS * Q_BLOCK, n16), F32)],
        compiler_params=_cparams("arbitrary", "arbitrary"),
        name="nsa_seq",
    )(rel_bias, nq, small, kx, vs, kvw, kvc, tab, rbt, jnp.asarray(ovlt))


def _nsa_tab_seq_dists(seq):
    max_base = max(seq, SLC_KC + (_near_chunks() - 1) * SLC_KC)
    p0 = max_base + Q_BLOCK
    length = p0 + WINDOW + Q_BLOCK + LANES
    return p0, np.maximum(p0 - np.arange(length), 0)


PAGE_GROUP = 16
PAGES_PER_STEP = 128


def _pad_rows(x, rows=SUBLANES):
    return jnp.concatenate([x, jnp.zeros((rows - x.shape[0], x.shape[1]), x.dtype)], axis=0)


def _nsa_select_body(pt_ref, *refs, past, n_steps):
    del pt_ref
    pages = refs[:PAGES_PER_STEP]
    wab_ref, pe_ref, q_ref, tabc_ref, ovlt_ref, oc_ref, top_ref, parts_ref, rows_ref = refs[PAGES_PER_STEP:]
    step = pl.program_id(1)
    blocks_per_page = PAGE_SIZE // CMP_STRIDE
    group_rows = PAGE_GROUP * blocks_per_page
    n_groups = PAGES_PER_STEP // PAGE_GROUP
    for p, pg in enumerate(pages):
        g, pp = divmod(p, PAGE_GROUP)
        rows_ref[g, pp * PAGE_SIZE:(pp + 1) * PAGE_SIZE, :] = pg[...].reshape(2 * HEAD_DIM, PAGE_SIZE).T
    for g in range(n_groups):
        acc = jnp.zeros((group_rows, 2 * LANES), F32)
        for j in range(CMP_STRIDE):
            xj = rows_ref[g, pl.ds(j, group_rows, stride=CMP_STRIDE), :]
            acc = acc + _dot(_bf(xj), wab_ref[j * LANES:(j + 1) * LANES, :])
        row0 = pl.multiple_of((step * n_groups + g) * group_rows, group_rows)
        parts_ref[pl.ds(row0, group_rows), :] = acc

    @pl.when(step == n_steps - 1)
    def _():
        n16 = past // CMP_STRIDE
        n_cmp = n16 - 1
        cur = past // SLC_BLOCK
        pe_term = _pe_term(pe_ref, wab_ref)
        kvc = _combine_cmp(parts_ref[...], pe_term)
        qs = _pad_rows(_stack_heads(q_ref[0]) * SCALE)
        s_c = _dot_nt(_bf(qs), _bf(kvc[:, 0:HEAD_DIM])) + tabc_ref[...]
        lane = lax.broadcasted_iota(jnp.int32, (SUBLANES, n16), 1)
        p_c = _softmax_rows(s_c, lane < n_cmp)
        p_cb = _bf(p_c)
        o_c = _dot(p_cb, _bf(kvc[:, HEAD_DIM:2 * HEAD_DIM]))
        oc_ref[0] = jnp.concatenate([o_c[h:h + 1, :] for h in range(N_HEADS)], axis=1)
        imp_h = _dot_nt(ovlt_ref[...], p_cb)
        imp = imp_h[:, 0:1] + imp_h[:, 1:2] + imp_h[:, 2:3] + imp_h[:, 3:4]
        nrow = imp.shape[0]
        rowb = lax.broadcasted_iota(jnp.int32, (nrow, 1), 0)
        forced_ids = (0, cur - 1, cur)
        forced = (rowb == 0) | (rowb == cur) | (rowb == cur - 1)
        score = jnp.where(forced | (rowb > cur), NEG_INF, imp)
        out_lane = lax.broadcasted_iota(jnp.int32, (1, LANES), 1)
        top = jnp.zeros((1, LANES), jnp.int32)
        for r, blk in enumerate(forced_ids):
            top = jnp.where(out_lane == r, blk, top)
        for r in range(len(forced_ids), SLC_TOPK):
            m = jnp.max(score, axis=0, keepdims=True)
            idx = jnp.min(jnp.where(score == m, rowb, nrow), axis=0, keepdims=True)
            top = jnp.where(out_lane == r, idx, top)
            score = jnp.where(rowb == idx, NEG_INF, score)
        top_ref[0] = top


def _nsa_select(layer, page_table, cache_t, wab_bf, pe, nq, tabc, past):
    b, n_pages = page_table.shape
    n_steps = n_pages // PAGES_PER_STEP
    n16 = past // CMP_STRIDE
    n_slc = past // SLC_BLOCK + 1
    nrow = -(-n_slc // SUBLANES) * SUBLANES
    cmp_start = np.arange(n16)[None, :] * CMP_STRIDE
    slc_start = np.arange(nrow)[:, None] * SLC_BLOCK
    ovlt = ((cmp_start < slc_start + SLC_BLOCK) & (cmp_start + CMP_LEN > slc_start)
            & (np.arange(n16)[None, :] < n16 - 1) & (np.arange(nrow)[:, None] < n_slc)).astype(np.float32)

    n_pool = cache_t.shape[1]

    def page_spec(p):
        def imap(i, s, pt):
            pg = pt[jnp.minimum(i, b - 1), jnp.minimum(s, n_steps - 1) * PAGES_PER_STEP + p]
            return (layer, jnp.minimum(pg, n_pool - 1), 0, 0, 0)
        return pl.BlockSpec((None, None, 2, HEAD_DIM, PAGE_SIZE), imap)

    const = lambda shape: pl.BlockSpec(shape, lambda i, s, pt: (0,) * len(shape))
    grid_spec = pltpu.PrefetchScalarGridSpec(
        num_scalar_prefetch=1,
        grid=(b, n_steps),
        in_specs=[page_spec(p) for p in range(PAGES_PER_STEP)]
        + [const(wab_bf.shape), const(pe.shape), pl.BlockSpec((1, 1, D_HEADS), lambda i, s, pt: (i, 0, 0)),
           const(tabc.shape), const(ovlt.shape)],
        out_specs=[pl.BlockSpec((1, 1, D_HEADS), lambda i, s, pt: (i, 0, 0)),
                   pl.BlockSpec((1, 1, LANES), lambda i, s, pt: (i, 0, 0))],
        scratch_shapes=[pltpu.VMEM((n16, 2 * LANES), F32),
                        pltpu.VMEM((PAGES_PER_STEP // PAGE_GROUP, PAGE_GROUP * PAGE_SIZE, 2 * HEAD_DIM), F32)],
    )
    return pl.pallas_call(
        functools.partial(_nsa_select_body, past=past, n_steps=n_steps),
        grid_spec=grid_spec,
        out_shape=[jax.ShapeDtypeStruct((b, 1, D_HEADS), F32), jax.ShapeDtypeStruct((b, 1, LANES), jnp.int32)],
        compiler_params=_cparams("parallel", "arbitrary"),
        name="nsa_select",
    )(page_table, *([cache_t] * PAGES_PER_STEP), wab_bf, pe, nq, tabc, jnp.asarray(ovlt, BF16))


def _nsa_attend_body(pt_ref, top_ref, *refs, past):
    del pt_ref
    pages = refs[:SLC_TOPK]
    (q_ref, sm_ref, new_ref, oc_ref, swa_ref, tabs_ref, tabw_ref, y_ref, swa_out) = refs[SLC_TOPK:]
    b = pl.program_id(0)
    new_blk = past // SLC_BLOCK
    blocks_per_page = PAGE_SIZE // SLC_BLOCK
    qs = _bf(_pad_rows(_stack_heads(q_ref[0]) * SCALE))
    new = new_ref[0]
    lane_t = lax.broadcasted_iota(jnp.int32, (HEAD_DIM, PAGE_SIZE), 1)
    lane = lax.broadcasted_iota(jnp.int32, (SUBLANES, PAGE_SIZE), 1)
    new_kt = jnp.where(lane_t == 0, _row_to_col(new[:, 2 * HEAD_DIM:3 * HEAD_DIM]), 0.0)
    new_vt = jnp.where(lane_t == 0, _row_to_col(new[:, 3 * HEAD_DIM:4 * HEAD_DIM]), 0.0)
    kts, biases, oks, vts = [], [], [], []
    for s in range(SLC_TOPK):
        ti = jnp.clip(top_ref[b, s], 0, new_blk)
        is_new = ti == new_blk
        kts.append(jnp.where(is_new, new_kt, pages[s][0]))
        vts.append(jnp.where(is_new, new_vt, pages[s][1]))
        biases.append(tabs_ref[ti])
        kpos = ti * SLC_BLOCK + lane % SLC_BLOCK
        ok = (lane // SLC_BLOCK == ti % blocks_per_page) & (kpos <= past)
        oks.append(jnp.where(ok, 1.0, 0.0))
    wlane = lax.broadcasted_iota(jnp.int32, (HEAD_DIM, WINDOW), 1)
    win = []
    for c in range(2):
        col = _row_to_col(new[:, (4 + c) * HEAD_DIM:(5 + c) * HEAD_DIM])
        win.append(jnp.where(wlane == WINDOW - 1, col, pltpu.roll(swa_ref[c], WINDOW - 1, 1)))
    s_w = _dot(qs, _bf(win[0])) + tabw_ref[...]

    s_s = _dot(qs, _bf(jnp.concatenate(kts, axis=1))) + jnp.concatenate(biases, axis=1)
    p_s = _softmax_rows(s_s, jnp.concatenate(oks, axis=1) > 0.5)
    p_w = _softmax_rows(s_w, jnp.full(s_w.shape, True))
    o_s = _dot_nt(_bf(p_s), _bf(jnp.concatenate(vts, axis=1)))
    o_w = _dot_nt(_bf(p_w), _bf(win[1]))
    for c in range(2):
        swa_out[c] = win[c]

    gates = _sigmoid(sm_ref[0])
    o_c = oc_ref[0]
    outs = []
    for h in range(N_HEADS):
        g = lambda t: gates[:, SM_NG + 3 * h + t:SM_NG + 3 * h + t + 1]
        outs.append(g(0) * _head(o_c, h) + g(1) * o_s[h:h + 1, :] + g(2) * o_w[h:h + 1, :])
    y_ref[0] = jnp.concatenate(outs, axis=1)


def _nsa_attend(layer, page_table, top, cache_t, nq, small, nkv, o_c, swa_t, tabs, tabw, past):
    b = nq.shape[0]
    last_page = past // PAGE_SIZE - 1
    blocks_per_page = PAGE_SIZE // SLC_BLOCK

    n_pool = cache_t.shape[1]

    def page_spec(s):
        def imap(i, pt, tp):
            ii = jnp.minimum(i, b - 1)
            pg = jnp.clip(tp[ii, s] // blocks_per_page, 0, last_page)
            return (layer, jnp.clip(pt[ii, pg], 0, n_pool - 1), 1, 0, 0)
        return pl.BlockSpec((None, None, 2, HEAD_DIM, PAGE_SIZE), imap)

    row = lambda n: pl.BlockSpec((1, 1, n), lambda i, pt, tp: (i, 0, 0))
    const = lambda shape: pl.BlockSpec(shape, lambda i, pt, tp: (0,) * len(shape))
    grid_spec = pltpu.PrefetchScalarGridSpec(
        num_scalar_prefetch=2,
        grid=(b,),
        in_specs=[page_spec(s) for s in range(SLC_TOPK)]
        + [row(D_HEADS), row(LANES), row(6 * HEAD_DIM), row(D_HEADS),
           pl.BlockSpec((None, None, 2, HEAD_DIM, WINDOW), lambda i, pt, tp: (layer, i, 0, 0, 0)),
           const(tabs.shape), const(tabw.shape)],
        out_specs=[row(D_HEADS), pl.BlockSpec((None, 2, HEAD_DIM, WINDOW), lambda i, pt, tp: (i, 0, 0, 0))],
    )
    return pl.pallas_call(
        functools.partial(_nsa_attend_body, past=past),
        grid_spec=grid_spec,
        out_shape=[jax.ShapeDtypeStruct((b, 1, D_HEADS), F32),
                   jax.ShapeDtypeStruct((b, 2, HEAD_DIM, WINDOW), F32)],
        compiler_params=_cparams("parallel"),
        name="nsa_attend",
    )(page_table, top, *([cache_t] * SLC_TOPK), nq, small, nkv, o_c, swa_t, tabs, tabw)


SEQ_CHUNK = 64
GDN_CHUNKS = 1
SCAN_CHUNK = 256
ROW_TILE = 512
FFN_TILE = 512


def _lane_vec(pairs):
    v = jnp.zeros((1, LANES), F32)
    for off, vals in pairs:
        v = v.at[0, off:off + vals.shape[0]].set(vals.astype(F32))
    return v


def kernel(x_prompt, x_sample, cache_nsa_kv, cache_swa_kv, state_ret, state_mlstm_C, state_mlstm_n, state_mlstm_m, state_gdn, state_gdn_conv, state_ffn_conv, page_table, rel_bias, norm_pre_mix, w_in, mlstm_b_i, mlstm_b_f, gdn_conv_w, gdn_A_log, gdn_dt_bias, nsa_cmp_pe, nsa_cmp_w, ret_norm, mlstm_norm, gdn_norm, w_out, norm_post_mix, norm_pre_ffn, w_ffn_gate, w_ffn_up, ffn_conv_w, w_ffn_down, norm_post_ffn):
    depth = w_in.shape[0]
    bp, seq, _ = x_prompt.shape
    bs, dec_seq, _ = x_sample.shape
    assert dec_seq == 1 and seq % SLC_KC == 0 and seq >= WINDOW + Q_BLOCK
    n_pool = cache_nsa_kv.shape[1]
    past = page_table.shape[1] * PAGE_SIZE
    assert past >= WINDOW and past % SLC_BLOCK == 0 and page_table.shape[1] % PAGES_PER_STEP == 0
    assert past // SLC_BLOCK + 1 >= SLC_TOPK

    p0, d_seq = _nsa_tab_seq_dists(seq)
    n16 = past // CMP_STRIDE
    n_slc = past // SLC_BLOCK + 1
    d_cmp = np.maximum(past - (np.arange(n16) * CMP_STRIDE + CMP_LEN - 1), 0)
    d_slc = np.maximum(past - np.arange(n_slc * SLC_BLOCK), 0)
    d_win = WINDOW - 1 - np.arange(WINDOW)
    tab = _bias_tables(rel_bias, np.concatenate([d_seq, d_cmp, d_slc, d_win]))
    o1, o2, o3 = len(d_seq), len(d_seq) + len(d_cmp), len(d_seq) + len(d_cmp) + len(d_slc)
    tab_seq = tab[:, :o1].reshape(SUBLANES, o1 // LANES, LANES)
    tabc = tab[:, o1:o2]
    tabs = jnp.transpose(tab[:, o2:o3].reshape(SUBLANES, n_slc, SLC_BLOCK), (1, 0, 2))
    tabs = jnp.concatenate([tabs, tabs], axis=2)
    tabw = tab[:, o3:]

    cache_t = jnp.transpose(cache_nsa_kv, (0, 1, 3, 4, 2))
    swa_t = jnp.transpose(cache_swa_kv, (0, 1, 3, 4, 2))
    cos_p, sin_p = _rope_tables(jnp.arange(seq))
    cos_s, sin_s = _rope_tables(jnp.arange(past, past + 1))

    xp = x_prompt
    xs = x_sample.reshape(bs, D_MODEL)
    p_states, s_states = [], []
    for l in range(depth):
        w_cat = _prep_w_in(w_in[l])
        w_out_bf = _bf(w_out[l])
        wg, wu, wd = _bf(w_ffn_gate[l]), _bf(w_ffn_up[l]), _bf(w_ffn_down[l])
        wab, pe = _prep_cmp_w(nsa_cmp_w[l], nsa_cmp_pe[l])
        ml_bias = _lane_vec([(SM_MI, mlstm_b_i[l]), (SM_MF, mlstm_b_f[l])])
        dtb = _lane_vec([(SM_GA, gdn_dt_bias[l])])
        alog = _lane_vec([(SM_GA, gdn_A_log[l])])

        z = _in_proj(xp.reshape(bp * seq, D_MODEL), norm_pre_mix[l], w_cat, ROW_TILE, seq=seq)
        z_ret, z_ml, z_gdn, z_nq, z_nkv, z_sm = [t.reshape(bp, seq, -1) for t in z[:6]]
        nkv_t = z[6].reshape(bp, 6, HEAD_DIM, seq)
        kx, vs, kvw = [t.reshape(bp, seq, -1) for t in z[7:]]
        y_ret, st_ret = _retention_seq(z_ret, cos_p, sin_p, ret_norm[l], SCAN_CHUNK, bp)
        y_ml, st_c, st_n, st_m = _mlstm_seq(z_ml, z_sm, ml_bias, mlstm_norm[l], SCAN_CHUNK, bp)
        y_gdn, st_g = _gdn_seq(z_gdn, z_sm, gdn_conv_w[l], dtb, alog, gdn_norm[l], SEQ_CHUNK, bp, GDN_CHUNKS)
        kvc = _compress_seq(z_nkv, wab, pe)
        y_nsa = _nsa_seq(rel_bias, z_nq, z_sm, kx, vs, kvw, kvc, tab_seq, p0)
        xp, tail = _ffn_seq(xp, (y_ret, y_ml, y_gdn, y_nsa), w_out_bf, norm_post_mix[l], norm_pre_ffn[l],
                            wg, wu, wd, ffn_conv_w[l], norm_post_ffn[l], min(seq, FFN_TILE))
        p_states.append((
            jnp.transpose(nkv_t[:, 0:4], (0, 3, 1, 2)),
            jnp.transpose(nkv_t[:, 4:6, :, seq - WINDOW:], (0, 3, 1, 2)),
            st_ret, st_c, st_n, st_m[:, 0, :N_HEADS], st_g,
            z_gdn[:, seq - (GDN_CONV - 1):, 0:3 * D_HEADS],
            tail[:, SUBLANES - (FFN_CONV - 1):, :]))

        z = _in_proj(xs, norm_pre_mix[l], w_cat, bs)
        z_ret, z_ml, z_gdn, z_nq, z_nkv, z_sm = [t.reshape(bs, 1, -1) for t in z]
        m_pad = jnp.pad(state_mlstm_m[l], ((0, 0), (0, LANES - N_HEADS))).reshape(bs, 1, LANES)
        (y_ret, y_ml, y_gdn, st_ret, st_c, st_n, st_m, st_g, st_gc) = _mixers_step(
            z_ret, z_ml, z_gdn, z_sm, cos_s, sin_s, ml_bias, dtb, alog, gdn_conv_w[l],
            ret_norm[l], mlstm_norm[l], gdn_norm[l],
            state_ret[l], state_mlstm_C[l], state_mlstm_n[l], m_pad, state_gdn[l], state_gdn_conv[l])
        o_c, top = _nsa_select(l, page_table, cache_t, _bf(wab), pe, z_nq, tabc, past)
        y_nsa, swa_new = _nsa_attend(l, page_table, top[:, 0, :SLC_TOPK], cache_t, z_nq, z_sm, z_nkv, o_c,
                                     swa_t, tabs, tabw, past)
        ys = [t.reshape(bs, D_HEADS) for t in (y_ret, y_ml, y_gdn, y_nsa)]
        x1, h2 = _out_proj(xs, ys, w_out_bf, norm_post_mix[l], norm_pre_ffn[l], bs)
        xs, g_new = _ffn_step(x1, h2, state_ffn_conv[l][:, 0], state_ffn_conv[l][:, 1], wg, wu, wd,
                              ffn_conv_w[l], norm_post_ffn[l])
        s_states.append((
            z_nkv[..., 0:4 * HEAD_DIM].reshape(bs, 1, 4, HEAD_DIM),
            jnp.transpose(swa_new, (0, 3, 1, 2)),
            st_ret, st_c, st_n, st_m[:, 0, :N_HEADS], st_g, st_gc,
            jnp.concatenate([state_ffn_conv[l][:, 1:], g_new[:, None, :]], axis=1)))

    stack = lambda states: tuple(jnp.stack([s[k] for s in states]) for k in range(len(states[0])))
    return (xp, xs.reshape(bs, 1, D_MODEL)) + stack(p_states) + stack(s_states)
```
